```python
import math
import jax
import jax.numpy as jnp
from jax import lax
import numpy as np

D_MODEL = 1024
BATCH = 8
SEQ = 4096
DEPTH = 2

ATT_HEADS = 4
ATT_HEAD_DIM = 64
ATT_WIDTH = ATT_HEADS * ATT_HEAD_DIM
DILATED_PATTERNS = ((128, 1), (512, 4), (2048, 16))
POOL_WINDOWS = (2, 4, 8, 16)
POOL_GROUP_DIM = 64
POOL_WIDTH = len(POOL_WINDOWS) * POOL_GROUP_DIM
DN_HEADS = 4
DN_HEAD_DIM = 128
DN_WIDTH = DN_HEADS * DN_HEAD_DIM
DN_CONV = 4
DN_CHUNK = 64
MIX_WIDTH = ATT_WIDTH + POOL_WIDTH + DN_WIDTH
IN_SPLITS = (ATT_WIDTH, ATT_WIDTH, ATT_WIDTH, POOL_WIDTH,
             DN_WIDTH, DN_WIDTH, DN_WIDTH, DN_WIDTH, DN_HEADS, DN_HEADS)
IN_WIDTH = sum(IN_SPLITS)
D_FF = 2816
ROPE_THETA = 10000.0
EPS = 1e-6

kernel_name = "hybrid_dilated_pool_deltanet_macaron"


def rms_norm(x, w):
    xf = x.astype(jnp.float32)
    y = xf * lax.rsqrt(jnp.mean(xf * xf, axis=-1, keepdims=True) + EPS)
    return (y * w).astype(x.dtype)


def swiglu(h, w_gate, w_up, w_down):
    return (jax.nn.silu(h @ w_gate) * (h @ w_up)) @ w_down


def rotary_tables(positions, dim):
    inv_freq = ROPE_THETA ** (-jnp.arange(0, dim, 2, dtype=jnp.float32) / dim)
    ang = positions.astype(jnp.float32)[..., None] * inv_freq
    return jnp.cos(ang)[:, :, None, :], jnp.sin(ang)[:, :, None, :]


def apply_rope(t, cos, sin):
    t1, t2 = jnp.split(t.astype(jnp.float32), 2, axis=-1)
    return jnp.concatenate([t1 * cos - t2 * sin, t2 * cos + t1 * sin], axis=-1).astype(t.dtype)


def dilated_window_attention(q, k, v, window, dilation):
    B, S, H, E = q.shape
    n = window // dilation
    span = n * dilation
    L = -(-S // span) * span
    M = L // dilation
    nb = M // n

    def to_blocks(t):
        t = jnp.pad(t, ((0, 0), (0, L - S), (0, 0), (0, 0)))
        return t.reshape(B, nb, n, dilation, H, E)

    def with_prev(t):
        prev = jnp.pad(t, ((0, 0), (1, 0), (0, 0), (0, 0), (0, 0), (0, 0)))[:, :-1]
        return jnp.concatenate([prev, t], axis=2)

    qb = to_blocks(q)
    kk = with_prev(to_blocks(k))
    vv = with_prev(to_blocks(v))
    s = jnp.einsum('bcqrhe,bckrhe->bcrhqk', qb, kk).astype(jnp.float32) / math.sqrt(E)
    qi = jnp.arange(n)[:, None]
    ki = jnp.arange(2 * n)[None, :]
    dist = n + qi - ki
    blk = jnp.arange(nb)[:, None, None]
    valid = (dist >= 0) & (dist <= n) & ((blk - 1) * n + ki >= 0)
    s = jnp.where(valid[None, :, None, None], s, -jnp.inf)
    m = jnp.max(s, axis=-1, keepdims=True)
    p = jnp.exp(s - m)
    den = jnp.sum(p, axis=-1, keepdims=True)
    o = jnp.einsum('bcrhqk,bckrhe->bcqrhe', (p / den).astype(vv.dtype), vv)
    lse = (m + jnp.log(den))[..., 0]
    o = o.reshape(B, L, H, E)[:, :S]
    lse = lse.transpose(0, 1, 4, 2, 3).reshape(B, L, H)[:, :S]
    return o, lse


def dilated_attention(q, k, v):
    outs, lses = [], []
    for window, dilation in DILATED_PATTERNS:
        o, lse = dilated_window_attention(q, k, v, window, dilation)
        outs.append(o)
        lses.append(lse)
    wts = jax.nn.softmax(jnp.stack(lses, axis=0), axis=0)
    y = sum(wts[i][..., None] * outs[i].astype(jnp.float32) for i in range(len(outs)))
    return y.astype(q.dtype)


def multiscale_pool(u, pool_w, pool_scale):
    B, S, _ = u.shape
    G = len(POOL_WINDOWS)
    ug = u.reshape(B, S, G, POOL_GROUP_DIM)
    cs = jnp.cumsum(ug.astype(jnp.float32), axis=1)
    t = jnp.arange(S)
    pooled = []
    for g, w in enumerate(POOL_WINDOWS):
        csg = cs[:, :, g]
        lower = jnp.pad(csg, ((0, 0), (w, 0), (0, 0)))[:, :S]
        count = jnp.minimum(t + 1, w).astype(jnp.float32)[None, :, None]
        pooled.append((csg - lower) / count)
    pooled = jnp.stack(pooled, axis=2) - ug.astype(jnp.float32)
    y = jnp.einsum('bsgc,gcd->bsgd', pooled.astype(u.dtype), pool_w)
    return y.reshape(B, S, POOL_WIDTH) * pool_scale


def causal_depthwise_conv(u, w):
    K = w.shape[0]
    up = jnp.pad(u, ((0, 0), (K - 1, 0), (0, 0)))
    return lax.conv_general_dilated(up, w[:, None, :], window_strides=(1,), padding='VALID',
                                    dimension_numbers=('NWC', 'WIO', 'NWC'),
                                    feature_group_count=u.shape[-1])


def l2_normalize(t):
    tf = t.astype(jnp.float32)
    return tf * lax.rsqrt(jnp.sum(tf * tf, axis=-1, keepdims=True) + EPS)


def chunk_gated_delta_rule(q, k, v, g, beta):
    B, S, H, Dk = q.shape
    Dv = v.shape[-1]
    C = DN_CHUNK
    N = S // C
    to_c = lambda t: t.astype(jnp.float32).reshape(B, N, C, H, -1).transpose(1, 0, 3, 2, 4)
    qc, kc, vc = to_c(q), to_c(k), to_c(v)
    beta = beta.astype(jnp.float32).reshape(B, N, C, H).transpose(1, 0, 3, 2)
    g = jnp.cumsum(g.astype(jnp.float32).reshape(B, N, C, H).transpose(1, 0, 3, 2), axis=-1)
    kb = kc * beta[..., None]
    vb = vc * beta[..., None]
    lower = jnp.tril(jnp.ones((C, C), dtype=bool))
    strict = jnp.tril(jnp.ones((C, C), dtype=bool), -1)
    diff = g[..., :, None] - g[..., None, :]
    decay = jnp.where(lower, jnp.exp(jnp.where(lower, diff, 0.0)), 0.0)
    A = jnp.where(strict, jnp.einsum('nbhid,nbhjd->nbhij', kb, kc) * decay, 0.0)
    eye = jnp.eye(C, dtype=jnp.float32)
    T = lax.linalg.triangular_solve(eye + A, jnp.broadcast_to(eye, A.shape),
                                    left_side=True, lower=True)
    u = T @ vb
    w = T @ (kb * jnp.exp(g)[..., None])
    intra = jnp.where(lower, jnp.einsum('nbhid,nbhjd->nbhij', qc, kc) * decay, 0.0)

    def step(state, xs):
        q_i, k_i, u_i, w_i, g_i, a_i = xs
        v_new = u_i - w_i @ state
        o = (q_i * jnp.exp(g_i)[..., None]) @ state + a_i @ v_new
        g_last = g_i[..., -1]
        k_dec = k_i * jnp.exp(g_last[..., None] - g_i)[..., None]
        state = state * jnp.exp(g_last)[..., None, None] + jnp.einsum('bhck,bhcv->bhkv', k_dec, v_new)
        return state, o

    state0 = jnp.zeros((B, H, Dk, Dv), dtype=jnp.float32)
    _, o = lax.scan(step, state0, (qc, kc, u, w, g, intra))
    return o.transpose(1, 0, 3, 2, 4).reshape(B, S, H, Dv)


def gated_deltanet(q, k, v, z, b, a, conv_w, a_log, dt_bias, norm_w):
    B, S, _ = q.shape
    qkv = jax.nn.silu(causal_depthwise_conv(jnp.concatenate([q, k, v], axis=-1), conv_w))
    q, k, v = jnp.split(qkv, 3, axis=-1)
    q = l2_normalize(q.reshape(B, S, DN_HEADS, DN_HEAD_DIM)) * (DN_HEAD_DIM ** -0.5)
    k = l2_normalize(k.reshape(B, S, DN_HEADS, DN_HEAD_DIM))
    v = v.reshape(B, S, DN_HEADS, DN_HEAD_DIM)
    beta = jax.nn.sigmoid(b.astype(jnp.float32))
    g = -jnp.exp(a_log.astype(jnp.float32)) * jax.nn.softplus(a.astype(jnp.float32) + dt_bias)
    o = chunk_gated_delta_rule(q, k, v, g, beta)
    o = rms_norm(o, norm_w) * jax.nn.silu(z.reshape(B, S, DN_HEADS, DN_HEAD_DIM).astype(jnp.float32))
    return o.reshape(B, S, DN_WIDTH).astype(z.dtype)


def _fwd_setup_inputs(seed: int = 0) -> dict:
    key = jax.random.key(seed)
    ks = jax.random.split(key, 24)
    nrm = lambda kk, shape, fan_in: jax.random.normal(kk, shape, jnp.float32) * (fan_in ** -0.5)
    gain = lambda kk, shape: 1.0 + 0.05 * jax.random.normal(kk, shape, jnp.float32)
    L = DEPTH
    x = jax.random.normal(ks[0], (BATCH, SEQ, D_MODEL), jnp.float32)
    positions = jnp.broadcast_to(jnp.arange(SEQ, dtype=jnp.int32), (BATCH, SEQ))
    dt = jnp.exp(jax.random.uniform(ks[12], (L, DN_HEADS), jnp.float32,
                                    math.log(1e-3), math.log(1e-1)))
    dt_bias = dt + jnp.log(-jnp.expm1(-dt))
    a_log = jnp.log(jax.random.uniform(ks[11], (L, DN_HEADS), jnp.float32, 1.0, 16.0))
    return {
        "x": x,
        "positions": positions,
        "ffn1_norm": gain(ks[1], (L, D_MODEL)),
        "ffn1_w_gate": nrm(ks[2], (L, D_MODEL, D_FF), D_MODEL),
        "ffn1_w_up": nrm(ks[3], (L, D_MODEL, D_FF), D_MODEL),
        "ffn1_w_down": nrm(ks[4], (L, D_FF, D_MODEL), D_FF),
        "mix_norm": gain(ks[5], (L, D_MODEL)),
        "w_in": nrm(ks[6], (L, D_MODEL, IN_WIDTH), D_MODEL),
        "pool_w": nrm(ks[7], (L, len(POOL_WINDOWS), POOL_GROUP_DIM, POOL_GROUP_DIM), POOL_GROUP_DIM),
        "pool_scale": gain(ks[8], (L, POOL_WIDTH)),
        "dn_conv_w": nrm(ks[9], (L, DN_CONV, 3 * DN_WIDTH), DN_CONV),
        "dn_a_log": a_log,
        "dn_dt_bias": dt_bias,
        "dn_out_norm": gain(ks[13], (L, DN_HEAD_DIM)),
        "w_out": nrm(ks[14], (L, MIX_WIDTH, D_MODEL), MIX_WIDTH),
        "ffn2_norm": gain(ks[15], (L, D_MODEL)),
        "ffn2_w_gate": nrm(ks[16], (L, D_MODEL, D_FF), D_MODEL),
        "ffn2_w_up": nrm(ks[17], (L, D_MODEL, D_FF), D_MODEL),
        "ffn2_w_down": nrm(ks[18], (L, D_FF, D_MODEL), D_FF),
        "final_norm": gain(ks[19], (D_MODEL,)),
    }


def _fwd_reference(x, positions, ffn1_norm, ffn1_w_gate, ffn1_w_up, ffn1_w_down, mix_norm, w_in,
              pool_w, pool_scale, dn_conv_w, dn_a_log, dn_dt_bias, dn_out_norm, w_out,
              ffn2_norm, ffn2_w_gate, ffn2_w_up, ffn2_w_down, final_norm):
    B, S, _ = x.shape
    cos, sin = rotary_tables(positions, ATT_HEAD_DIM)
    split_at = np.cumsum(IN_SPLITS)[:-1].tolist()
    for l in range(DEPTH):
        h = rms_norm(x, ffn1_norm[l])
        x = x + 0.5 * swiglu(h, ffn1_w_gate[l], ffn1_w_up[l], ffn1_w_down[l])
        h = rms_norm(x, mix_norm[l])
        proj = h @ w_in[l]
        aq, ak, av, pu, dq, dk, dv, dz, db, da = jnp.split(proj, split_at, axis=-1)
        aq = apply_rope(aq.reshape(B, S, ATT_HEADS, ATT_HEAD_DIM), cos, sin)
        ak = apply_rope(ak.reshape(B, S, ATT_HEADS, ATT_HEAD_DIM), cos, sin)
        av = av.reshape(B, S, ATT_HEADS, ATT_HEAD_DIM)
        ya = dilated_attention(aq, ak, av).reshape(B, S, ATT_WIDTH)
        yb = multiscale_pool(pu, pool_w[l], pool_scale[l])
        yc = gated_deltanet(dq, dk, dv, dz, db, da, dn_conv_w[l], dn_a_log[l],
                            dn_dt_bias[l], dn_out_norm[l])
        x = x + jnp.concatenate([ya, yb, yc], axis=-1) @ w_out[l]
        h = rms_norm(x, ffn2_norm[l])
        x = x + 0.5 * swiglu(h, ffn2_w_gate[l], ffn2_w_up[l], ffn2_w_down[l])
    return rms_norm(x, final_norm)


import jax as _jax
import jax.numpy as _jnp

TWIN_FORMAT = 'train_step'
FWD_PARAMS = ['x', 'positions', 'ffn1_norm', 'ffn1_w_gate', 'ffn1_w_up', 'ffn1_w_down', 'mix_norm', 'w_in', 'pool_w', 'pool_scale', 'dn_conv_w', 'dn_a_log', 'dn_dt_bias', 'dn_out_norm', 'w_out', 'ffn2_norm', 'ffn2_w_gate', 'ffn2_w_up', 'ffn2_w_down', 'final_norm']
TWIN_WEIGHTS = ['ffn1_norm', 'ffn1_w_gate', 'ffn1_w_up', 'ffn1_w_down', 'mix_norm', 'w_in', 'pool_w', 'pool_scale', 'dn_conv_w', 'dn_a_log', 'dn_dt_bias', 'dn_out_norm', 'w_out', 'ffn2_norm', 'ffn2_w_gate', 'ffn2_w_up', 'ffn2_w_down', 'final_norm']
TWIN_DIFF_INPUT = 'x'
TWIN_INPUTS = ['x', 'positions', 'ffn1_norm', 'ffn1_w_gate', 'ffn1_w_up', 'ffn1_w_down', 'mix_norm', 'w_in', 'pool_w', 'pool_scale', 'dn_conv_w', 'dn_a_log', 'dn_dt_bias', 'dn_out_norm', 'w_out', 'ffn2_norm', 'ffn2_w_gate', 'ffn2_w_up', 'ffn2_w_down', 'final_norm', 'loss_target', 'm_ffn1_norm', 'm_ffn1_w_gate', 'm_ffn1_w_up', 'm_ffn1_w_down', 'm_mix_norm', 'm_w_in', 'm_pool_w', 'm_pool_scale', 'm_dn_conv_w', 'm_dn_a_log', 'm_dn_dt_bias', 'm_dn_out_norm', 'm_w_out', 'm_ffn2_norm', 'm_ffn2_w_gate', 'm_ffn2_w_up', 'm_ffn2_w_down', 'm_final_norm', 'v_ffn1_norm', 'v_ffn1_w_gate', 'v_ffn1_w_up', 'v_ffn1_w_down', 'v_mix_norm', 'v_w_in', 'v_pool_w', 'v_pool_scale', 'v_dn_conv_w', 'v_dn_a_log', 'v_dn_dt_bias', 'v_dn_out_norm', 'v_w_out', 'v_ffn2_norm', 'v_ffn2_w_gate', 'v_ffn2_w_up', 'v_ffn2_w_down', 'v_final_norm']
TWIN_OUTPUTS = ['loss', 'grad_x', 'grad_ffn1_norm', 'grad_ffn1_w_gate', 'grad_ffn1_w_up', 'grad_ffn1_w_down', 'grad_mix_norm', 'grad_w_in', 'grad_pool_w', 'grad_pool_scale', 'grad_dn_conv_w', 'grad_dn_a_log', 'grad_dn_dt_bias', 'grad_dn_out_norm', 'grad_w_out', 'grad_ffn2_norm', 'grad_ffn2_w_gate', 'grad_ffn2_w_up', 'grad_ffn2_w_down', 'grad_final_norm', 'delta_ffn1_norm', 'delta_ffn1_w_gate', 'delta_ffn1_w_up', 'delta_ffn1_w_down', 'delta_mix_norm', 'delta_w_in', 'delta_pool_w', 'delta_pool_scale', 'delta_dn_conv_w', 'delta_dn_a_log', 'delta_dn_dt_bias', 'delta_dn_out_norm', 'delta_w_out', 'delta_ffn2_norm', 'delta_ffn2_w_gate', 'delta_ffn2_w_up', 'delta_ffn2_w_down', 'delta_final_norm', 'new_m_ffn1_norm', 'new_m_ffn1_w_gate', 'new_m_ffn1_w_up', 'new_m_ffn1_w_down', 'new_m_mix_norm', 'new_m_w_in', 'new_m_pool_w', 'new_m_pool_scale', 'new_m_dn_conv_w', 'new_m_dn_a_log', 'new_m_dn_dt_bias', 'new_m_dn_out_norm', 'new_m_w_out', 'new_m_ffn2_norm', 'new_m_ffn2_w_gate', 'new_m_ffn2_w_up', 'new_m_ffn2_w_down', 'new_m_final_norm', 'new_v_ffn1_norm', 'new_v_ffn1_w_gate', 'new_v_ffn1_w_up', 'new_v_ffn1_w_down', 'new_v_mix_norm', 'new_v_w_in', 'new_v_pool_w', 'new_v_pool_scale', 'new_v_dn_conv_w', 'new_v_dn_a_log', 'new_v_dn_dt_bias', 'new_v_dn_out_norm', 'new_v_w_out', 'new_v_ffn2_norm', 'new_v_ffn2_w_gate', 'new_v_ffn2_w_up', 'new_v_ffn2_w_down', 'new_v_final_norm']
TWIN_LEAF_KINDS = {'loss': 'loss', 'grad_x': 'grad_x', 'grad_ffn1_norm': 'grad_w', 'grad_ffn1_w_gate': 'grad_w', 'grad_ffn1_w_up': 'grad_w', 'grad_ffn1_w_down': 'grad_w', 'grad_mix_norm': 'grad_w', 'grad_w_in': 'grad_w', 'grad_pool_w': 'grad_w', 'grad_pool_scale': 'grad_w', 'grad_dn_conv_w': 'grad_w', 'grad_dn_a_log': 'grad_w', 'grad_dn_dt_bias': 'grad_w', 'grad_dn_out_norm': 'grad_w', 'grad_w_out': 'grad_w', 'grad_ffn2_norm': 'grad_w', 'grad_ffn2_w_gate': 'grad_w', 'grad_ffn2_w_up': 'grad_w', 'grad_ffn2_w_down': 'grad_w', 'grad_final_norm': 'grad_w', 'delta_ffn1_norm': 'delta_w', 'delta_ffn1_w_gate': 'delta_w', 'delta_ffn1_w_up': 'delta_w', 'delta_ffn1_w_down': 'delta_w', 'delta_mix_norm': 'delta_w', 'delta_w_in': 'delta_w', 'delta_pool_w': 'delta_w', 'delta_pool_scale': 'delta_w', 'delta_dn_conv_w': 'delta_w', 'delta_dn_a_log': 'delta_w', 'delta_dn_dt_bias': 'delta_w', 'delta_dn_out_norm': 'delta_w', 'delta_w_out': 'delta_w', 'delta_ffn2_norm': 'delta_w', 'delta_ffn2_w_gate': 'delta_w', 'delta_ffn2_w_up': 'delta_w', 'delta_ffn2_w_down': 'delta_w', 'delta_final_norm': 'delta_w', 'new_m_ffn1_norm': 'new_m', 'new_m_ffn1_w_gate': 'new_m', 'new_m_ffn1_w_up': 'new_m', 'new_m_ffn1_w_down': 'new_m', 'new_m_mix_norm': 'new_m', 'new_m_w_in': 'new_m', 'new_m_pool_w': 'new_m', 'new_m_pool_scale': 'new_m', 'new_m_dn_conv_w': 'new_m', 'new_m_dn_a_log': 'new_m', 'new_m_dn_dt_bias': 'new_m', 'new_m_dn_out_norm': 'new_m', 'new_m_w_out': 'new_m', 'new_m_ffn2_norm': 'new_m', 'new_m_ffn2_w_gate': 'new_m', 'new_m_ffn2_w_up': 'new_m', 'new_m_ffn2_w_down': 'new_m', 'new_m_final_norm': 'new_m', 'new_v_ffn1_norm': 'new_v', 'new_v_ffn1_w_gate': 'new_v', 'new_v_ffn1_w_up': 'new_v', 'new_v_ffn1_w_down': 'new_v', 'new_v_mix_norm': 'new_v', 'new_v_w_in': 'new_v', 'new_v_pool_w': 'new_v', 'new_v_pool_scale': 'new_v', 'new_v_dn_conv_w': 'new_v', 'new_v_dn_a_log': 'new_v', 'new_v_dn_dt_bias': 'new_v', 'new_v_dn_out_norm': 'new_v', 'new_v_w_out': 'new_v', 'new_v_ffn2_norm': 'new_v', 'new_v_ffn2_w_gate': 'new_v', 'new_v_ffn2_w_up': 'new_v', 'new_v_ffn2_w_down': 'new_v', 'new_v_final_norm': 'new_v'}


def _forward(args):
    return _fwd_reference(*[args[k] for k in FWD_PARAMS])


def _output_shape():
    out = _jax.eval_shape(lambda: _forward(_fwd_setup_inputs(0)))
    return out.shape, out.dtype

N_MICROBATCH = 1
ADAM_LR = 0.001
ADAM_B1 = 0.9
ADAM_B2 = 0.999
ADAM_EPS = 1e-08
ADAM_WD = 0.01
ADAM_STEP = 10
PER_EXAMPLE_BATCH_AXIS = {'x': 0, 'positions': 0, 'loss_target': 0}
SHARED_INPUTS = []
_WEIGHT_DTYPES = {'ffn1_norm': _jnp.float32, 'ffn1_w_gate': _jnp.float32, 'ffn1_w_up': _jnp.float32, 'ffn1_w_down': _jnp.float32, 'mix_norm': _jnp.float32, 'w_in': _jnp.float32, 'pool_w': _jnp.float32, 'pool_scale': _jnp.float32, 'dn_conv_w': _jnp.float32, 'dn_a_log': _jnp.float32, 'dn_dt_bias': _jnp.float32, 'dn_out_norm': _jnp.float32, 'w_out': _jnp.float32, 'ffn2_norm': _jnp.float32, 'ffn2_w_gate': _jnp.float32, 'ffn2_w_up': _jnp.float32, 'ffn2_w_down': _jnp.float32, 'final_norm': _jnp.float32}
MOMENT_SCALE = {'ffn1_norm': 8.155664e-02, 'ffn1_w_gate': 3.522631e-02, 'ffn1_w_up': 3.408078e-02, 'ffn1_w_down': 5.650708e-02, 'mix_norm': 1.275753e-01, 'w_in': 7.301734e-02, 'pool_w': 1.357826e-01, 'pool_scale': 1.352987e-01, 'dn_conv_w': 6.683826e-02, 'dn_a_log': 2.696511e-01, 'dn_dt_bias': 2.656407e-01, 'dn_out_norm': 1.858054e-01, 'w_out': 9.420711e-02, 'ffn2_norm': 6.262122e-02, 'ffn2_w_gate': 2.666071e-02, 'ffn2_w_up': 2.599475e-02, 'ffn2_w_down': 4.312175e-02, 'final_norm': 3.195787e+01}


def _to_microbatches(a, axis):
    t = _jnp.moveaxis(a, axis, 0)
    t = t.reshape((N_MICROBATCH, t.shape[0] // N_MICROBATCH) + t.shape[1:])
    return _jnp.moveaxis(t, 1, axis + 1)


def setup_inputs(seed: int = 0) -> dict:
    inp = _fwd_setup_inputs(seed)
    key = _jax.random.fold_in(_jax.random.key(seed), 7919)
    shape, _ = _output_shape()
    out = dict(inp)
    out["loss_target"] = _jax.random.normal(_jax.random.fold_in(key, 0), shape, _jnp.float32)
    for i, name in enumerate(TWIN_WEIGHTS):
        w = inp[name].astype(_jnp.float32)
        if MOMENT_SCALE is None:
            s = _jnp.sqrt(_jnp.mean(_jnp.square(w)) + 1e-30)
        else:
            s = MOMENT_SCALE[name]
        km, kv = _jax.random.split(_jax.random.fold_in(key, i + 1))
        out[name] = w
        out["m_" + name] = s * _jax.random.normal(km, w.shape, _jnp.float32)
        out["v_" + name] = (s * s) * _jax.random.uniform(kv, w.shape, _jnp.float32, 0.5, 1.5)
    if N_MICROBATCH > 1:
        for name, axis in PER_EXAMPLE_BATCH_AXIS.items():
            out[name] = _to_microbatches(out[name], axis)
    return {'x': out['x'], 'positions': out['positions'], 'ffn1_norm': out['ffn1_norm'], 'ffn1_w_gate': out['ffn1_w_gate'], 'ffn1_w_up': out['ffn1_w_up'], 'ffn1_w_down': out['ffn1_w_down'], 'mix_norm': out['mix_norm'], 'w_in': out['w_in'], 'pool_w': out['pool_w'], 'pool_scale': out['pool_scale'], 'dn_conv_w': out['dn_conv_w'], 'dn_a_log': out['dn_a_log'], 'dn_dt_bias': out['dn_dt_bias'], 'dn_out_norm': out['dn_out_norm'], 'w_out': out['w_out'], 'ffn2_norm': out['ffn2_norm'], 'ffn2_w_gate': out['ffn2_w_gate'], 'ffn2_w_up': out['ffn2_w_up'], 'ffn2_w_down': out['ffn2_w_down'], 'final_norm': out['final_norm'], 'loss_target': out['loss_target'], 'm_ffn1_norm': out['m_ffn1_norm'], 'm_ffn1_w_gate': out['m_ffn1_w_gate'], 'm_ffn1_w_up': out['m_ffn1_w_up'], 'm_ffn1_w_down': out['m_ffn1_w_down'], 'm_mix_norm': out['m_mix_norm'], 'm_w_in': out['m_w_in'], 'm_pool_w': out['m_pool_w'], 'm_pool_scale': out['m_pool_scale'], 'm_dn_conv_w': out['m_dn_conv_w'], 'm_dn_a_log': out['m_dn_a_log'], 'm_dn_dt_bias': out['m_dn_dt_bias'], 'm_dn_out_norm': out['m_dn_out_norm'], 'm_w_out': out['m_w_out'], 'm_ffn2_norm': out['m_ffn2_norm'], 'm_ffn2_w_gate': out['m_ffn2_w_gate'], 'm_ffn2_w_up': out['m_ffn2_w_up'], 'm_ffn2_w_down': out['m_ffn2_w_down'], 'm_final_norm': out['m_final_norm'], 'v_ffn1_norm': out['v_ffn1_norm'], 'v_ffn1_w_gate': out['v_ffn1_w_gate'], 'v_ffn1_w_up': out['v_ffn1_w_up'], 'v_ffn1_w_down': out['v_ffn1_w_down'], 'v_mix_norm': out['v_mix_norm'], 'v_w_in': out['v_w_in'], 'v_pool_w': out['v_pool_w'], 'v_pool_scale': out['v_pool_scale'], 'v_dn_conv_w': out['v_dn_conv_w'], 'v_dn_a_log': out['v_dn_a_log'], 'v_dn_dt_bias': out['v_dn_dt_bias'], 'v_dn_out_norm': out['v_dn_out_norm'], 'v_w_out': out['v_w_out'], 'v_ffn2_norm': out['v_ffn2_norm'], 'v_ffn2_w_gate': out['v_ffn2_w_gate'], 'v_ffn2_w_up': out['v_ffn2_w_up'], 'v_ffn2_w_down': out['v_ffn2_w_down'], 'v_final_norm': out['v_final_norm']}


def _loss(weights, diff, rest, loss_target):
    with _jax.named_scope("forward"):
        args = {**rest, TWIN_DIFF_INPUT: diff, **{k: w.astype(_WEIGHT_DTYPES[k]) for k, w in weights.items()}}
        y = _forward(args)
    with _jax.named_scope("loss_head"):
        err = _jnp.square(y.astype(_jnp.float32) - loss_target)
        return 0.5 * _jnp.sum(_jnp.mean(err, axis=-1)) if err.ndim else 0.5 * err


def _adamw(w, g, m, v):
    m = ADAM_B1 * m + (1.0 - ADAM_B1) * g
    v = ADAM_B2 * v + (1.0 - ADAM_B2) * _jnp.square(g)
    m_hat = m / (1.0 - ADAM_B1 ** ADAM_STEP)
    v_hat = v / (1.0 - ADAM_B2 ** ADAM_STEP)
    delta = -ADAM_LR * (m_hat / (_jnp.sqrt(v_hat) + ADAM_EPS) + ADAM_WD * w)
    return delta, m, v


def reference(x, positions, ffn1_norm, ffn1_w_gate, ffn1_w_up, ffn1_w_down, mix_norm, w_in, pool_w, pool_scale, dn_conv_w, dn_a_log, dn_dt_bias, dn_out_norm, w_out, ffn2_norm, ffn2_w_gate, ffn2_w_up, ffn2_w_down, final_norm, loss_target, m_ffn1_norm, m_ffn1_w_gate, m_ffn1_w_up, m_ffn1_w_down, m_mix_norm, m_w_in, m_pool_w, m_pool_scale, m_dn_conv_w, m_dn_a_log, m_dn_dt_bias, m_dn_out_norm, m_w_out, m_ffn2_norm, m_ffn2_w_gate, m_ffn2_w_up, m_ffn2_w_down, m_final_norm, v_ffn1_norm, v_ffn1_w_gate, v_ffn1_w_up, v_ffn1_w_down, v_mix_norm, v_w_in, v_pool_w, v_pool_scale, v_dn_conv_w, v_dn_a_log, v_dn_dt_bias, v_dn_out_norm, v_w_out, v_ffn2_norm, v_ffn2_w_gate, v_ffn2_w_up, v_ffn2_w_down, v_final_norm):
    given = dict(x=x, positions=positions, ffn1_norm=ffn1_norm, ffn1_w_gate=ffn1_w_gate, ffn1_w_up=ffn1_w_up, ffn1_w_down=ffn1_w_down, mix_norm=mix_norm, w_in=w_in, pool_w=pool_w, pool_scale=pool_scale, dn_conv_w=dn_conv_w, dn_a_log=dn_a_log, dn_dt_bias=dn_dt_bias, dn_out_norm=dn_out_norm, w_out=w_out, ffn2_norm=ffn2_norm, ffn2_w_gate=ffn2_w_gate, ffn2_w_up=ffn2_w_up, ffn2_w_down=ffn2_w_down, final_norm=final_norm, loss_target=loss_target, m_ffn1_norm=m_ffn1_norm, m_ffn1_w_gate=m_ffn1_w_gate, m_ffn1_w_up=m_ffn1_w_up, m_ffn1_w_down=m_ffn1_w_down, m_mix_norm=m_mix_norm, m_w_in=m_w_in, m_pool_w=m_pool_w, m_pool_scale=m_pool_scale, m_dn_conv_w=m_dn_conv_w, m_dn_a_log=m_dn_a_log, m_dn_dt_bias=m_dn_dt_bias, m_dn_out_norm=m_dn_out_norm, m_w_out=m_w_out, m_ffn2_norm=m_ffn2_norm, m_ffn2_w_gate=m_ffn2_w_gate, m_ffn2_w_up=m_ffn2_w_up, m_ffn2_w_down=m_ffn2_w_down, m_final_norm=m_final_norm, v_ffn1_norm=v_ffn1_norm, v_ffn1_w_gate=v_ffn1_w_gate, v_ffn1_w_up=v_ffn1_w_up, v_ffn1_w_down=v_ffn1_w_down, v_mix_norm=v_mix_norm, v_w_in=v_w_in, v_pool_w=v_pool_w, v_pool_scale=v_pool_scale, v_dn_conv_w=v_dn_conv_w, v_dn_a_log=v_dn_a_log, v_dn_dt_bias=v_dn_dt_bias, v_dn_out_norm=v_dn_out_norm, v_w_out=v_w_out, v_ffn2_norm=v_ffn2_norm, v_ffn2_w_gate=v_ffn2_w_gate, v_ffn2_w_up=v_ffn2_w_up, v_ffn2_w_down=v_ffn2_w_down, v_final_norm=v_final_norm)
    weights = {n: given[n] for n in TWIN_WEIGHTS}
    shared = {n: given[n] for n in SHARED_INPUTS}
    per_example = {n: given[n] for n in ['x', 'positions']}
    grad_fn = _jax.value_and_grad(_loss, argnums=(0, 1))

    def one_microbatch(ex, loss_target):
        ex = dict(ex)
        diff = ex.pop(TWIN_DIFF_INPUT)
        return grad_fn(weights, diff, {**shared, **ex}, loss_target)

    if N_MICROBATCH == 1:
        loss, (grad_w, grad_x) = one_microbatch(per_example, given["loss_target"])
    else:
        def body(carry, xs):
            loss_sum, grad_sum = carry
            l_k, (gw_k, gx_k) = one_microbatch(xs[0], xs[1])
            with _jax.named_scope("update"):
                return (loss_sum + l_k, _jax.tree.map(_jnp.add, grad_sum, gw_k)), gx_k

        init = (_jnp.zeros((), _jnp.float32), _jax.tree.map(_jnp.zeros_like, weights))
        (loss, grad_w), grad_x = _jax.lax.scan(body, init, (per_example, given["loss_target"]))
    with _jax.named_scope("update"):
        delta_w, new_m, new_v = {}, {}, {}
        for n in TWIN_WEIGHTS:
            delta_w[n], new_m[n], new_v[n] = _adamw(weights[n], grad_w[n], given["m_" + n], given["v_" + n])
    return (loss, grad_x, *[grad_w[n] for n in TWIN_WEIGHTS], *[delta_w[n] for n in TWIN_WEIGHTS],
            *[new_m[n] for n in TWIN_WEIGHTS], *[new_v[n] for n in TWIN_WEIGHTS])
```

```python
import functools
import math

import jax
import jax.numpy as jnp
from jax import lax
from jax.experimental import pallas as pl
from jax.experimental.pallas import tpu as pltpu

F32 = jnp.float32
BF16 = jnp.bfloat16
HI = lax.Precision.HIGHEST
SDS = jax.ShapeDtypeStruct

N_DEV = 8
SEQ = 4096
D_MODEL = 1024
DEPTH = 2
D_FF = 2816
FF_BLK = D_FF // N_DEV
ATT_W = 256
ATT_E = 64
ATT_BLK = 128
DILATIONS = (1, 4, 16)
POOL_W = 256
POOL_HALO = 16
DN_W = 512
DN_H = 4
DN_E = 128
DN_C = 64
N_CHUNK = SEQ // DN_C
IN_W = 3080
IN_BLK = IN_W // N_DEV
EPS = 1e-6
EXT_ATT = 1280
EXT_REST = 3328
EXT_W = EXT_ATT + EXT_REST
R_DQKV, R_DZ, R_BB, R_AB, R_PU = 0, 1536, 2048, 2560, 3072

ADAM_LR, ADAM_B1, ADAM_B2, ADAM_EPS, ADAM_WD, ADAM_STEP = 0.001, 0.9, 0.999, 1e-08, 0.01, 10

VMEM_LIMIT = 60 * 1024 * 1024
MESH = pl.DeviceIdType.MESH


def _cparams(sem=None):
    kw = dict(vmem_limit_bytes=VMEM_LIMIT)
    if sem is not None:
        kw["dimension_semantics"] = sem
    return pltpu.CompilerParams(**kw)


def _dot(a, b, prec=None):
    return jnp.dot(a, b, preferred_element_type=F32, precision=prec)


def _dot_nt(a, b, prec=None):
    return lax.dot_general(a, b, (((1,), (1,)), ((), ())), preferred_element_type=F32, precision=prec)


def _dot_tn(a, b, prec=None):
    return lax.dot_general(a, b, (((0,), (0,)), ((), ())), preferred_element_type=F32, precision=prec)


def _sigmoid(x):
    return jax.nn.sigmoid(x)


def _rms_stats(x):
    r = lax.rsqrt(jnp.mean(x * x, axis=-1, keepdims=True) + EPS)
    return x * r, r


def _rms_bwd(xh, r, w, dh):
    dxh = dh * w
    dx = r * (dxh - xh * jnp.mean(dxh * xh, axis=-1, keepdims=True))
    return dx, jnp.sum(dh * xh, axis=0, keepdims=True)


FFN_T_FWD = 1024
FFN_T_BWD = 512


def ffn_fwd(x, nw, wgu, wd, name):
    t = FFN_T_FWD

    def body(x_ref, nw_ref, wgu_ref, wd_ref, o_ref, h_scr, acc_scr):
        k = pl.program_id(1)

        @pl.when(k == 0)
        def _():
            xh, _r = _rms_stats(x_ref[...])
            h_scr[...] = (xh * nw_ref[...]).astype(BF16)
            acc_scr[...] = jnp.zeros_like(acc_scr)

        h = h_scr[...]
        hg = _dot(h, wgu_ref[0, 0])
        hu = _dot(h, wgu_ref[0, 1])
        a = (hg * _sigmoid(hg) * hu).astype(BF16)
        acc_scr[...] += _dot(a, wd_ref[0])

        @pl.when(k == N_DEV - 1)
        def _():
            o_ref[...] = x_ref[...] + 0.5 * acc_scr[...]

    return pl.pallas_call(
        body, name=name, grid=(SEQ // t, N_DEV),
        in_specs=[pl.BlockSpec((t, D_MODEL), lambda i, k: (i, 0)),
                  pl.BlockSpec((1, D_MODEL), lambda i, k: (0, 0)),
                  pl.BlockSpec((1, 2, D_MODEL, FF_BLK), lambda i, k: (k, 0, 0, 0)),
                  pl.BlockSpec((1, FF_BLK, D_MODEL), lambda i, k: (k, 0, 0))],
        out_specs=pl.BlockSpec((t, D_MODEL), lambda i, k: (i, 0)),
        out_shape=SDS((SEQ, D_MODEL), F32),
        scratch_shapes=[pltpu.VMEM((t, D_MODEL), BF16), pltpu.VMEM((t, D_MODEL), F32)],
        compiler_params=_cparams(("arbitrary", "arbitrary")),
    )(x, nw, wgu, wd)


def ffn_bwd(x, dxo, nw, wgu, wd, name):
    t = FFN_T_BWD
    nt = SEQ // t

    def body(x_ref, dxo_ref, nw_ref, wgu_ref, wd_ref, dx_ref, dwgu_ref, dwd_ref, dnw_ref,
             dh_scr, ag_scr, au_scr, ad_scr):
        k = pl.program_id(0)
        i = pl.program_id(1)
        rows = pl.ds(pl.multiple_of(i * t, t), t)
        xh, r = _rms_stats(x_ref[...])
        nw_v = nw_ref[...]
        h = (xh * nw_v).astype(BF16)
        dxo_v = dxo_ref[...]
        dy = (0.5 * dxo_v).astype(BF16)
        wg = wgu_ref[0, 0]
        wu = wgu_ref[0, 1]
        hg = _dot(h, wg)
        hu = _dot(h, wu)
        sg = _sigmoid(hg)
        sil = hg * sg
        a = (sil * hu).astype(BF16)
        da = _dot_nt(dy, wd_ref[0])
        dhu = (da * sil).astype(BF16)
        dhg = (da * hu * (sg * (1.0 + hg * (1.0 - sg)))).astype(BF16)
        p_d = _dot_tn(a, dy)
        p_g = _dot_tn(h, dhg)
        p_u = _dot_tn(h, dhu)
        dh = _dot_nt(dhg, wg) + _dot_nt(dhu, wu)

        @pl.when(i == 0)
        def _():
            ad_scr[...] = p_d
            ag_scr[...] = p_g
            au_scr[...] = p_u

        @pl.when(i > 0)
        def _():
            ad_scr[...] += p_d
            ag_scr[...] += p_g
            au_scr[...] += p_u

        @pl.when(i == nt - 1)
        def _():
            dwd_ref[0] = ad_scr[...].astype(BF16)
            dwgu_ref[0, 0] = ag_scr[...].astype(BF16)
            dwgu_ref[0, 1] = au_scr[...].astype(BF16)

        @pl.when(k == 0)
        def _():
            dh_scr[rows, :] = dh

        @pl.when(k > 0)
        def _():
            dh_scr[rows, :] += dh

        @pl.when(jnp.logical_and(k == 0, i == 0))
        def _():
            dnw_ref[...] = jnp.zeros_like(dnw_ref)

        @pl.when(k == N_DEV - 1)
        def _():
            dx, dw = _rms_bwd(xh, r, nw_v, dh_scr[rows, :])
            dx_ref[...] = dxo_v + dx
            dnw_ref[...] += dw

    last = N_DEV - 1
    return pl.pallas_call(
        body, name=name, grid=(N_DEV, nt),
        in_specs=[pl.BlockSpec((t, D_MODEL), lambda k, i: (i, 0)),
                  pl.BlockSpec((t, D_MODEL), lambda k, i: (i, 0)),
                  pl.BlockSpec((1, D_MODEL), lambda k, i: (0, 0)),
                  pl.BlockSpec((1, 2, D_MODEL, FF_BLK), lambda k, i: (k, 0, 0, 0)),
                  pl.BlockSpec((1, FF_BLK, D_MODEL), lambda k, i: (k, 0, 0))],
        out_specs=[pl.BlockSpec((t, D_MODEL), lambda k, i: (jnp.where(k == last, i, 0), 0)),
                   pl.BlockSpec((1, 2, D_MODEL, FF_BLK), lambda k, i: (k, 0, 0, 0)),
                   pl.BlockSpec((1, FF_BLK, D_MODEL), lambda k, i: (k, 0, 0)),
                   pl.BlockSpec((1, D_MODEL), lambda k, i: (0, 0))],
        out_shape=[SDS((SEQ, D_MODEL), F32), SDS((N_DEV, 2, D_MODEL, FF_BLK), BF16),
                   SDS((N_DEV, FF_BLK, D_MODEL), BF16), SDS((1, D_MODEL), F32)],
        scratch_shapes=[pltpu.VMEM((SEQ, D_MODEL), F32), pltpu.VMEM((D_MODEL, FF_BLK), F32),
                        pltpu.VMEM((D_MODEL, FF_BLK), F32), pltpu.VMEM((FF_BLK, D_MODEL), F32)],
        compiler_params=_cparams(("arbitrary", "arbitrary")),
    )(x, dxo, nw, wgu, wd)


def loss_head(x, fw, target, name):
    t = 512

    def body(x_ref, fw_ref, tg_ref, loss_ref, dx_ref, dfw_ref):
        i = pl.program_id(0)
        xh, r = _rms_stats(x_ref[...])
        w = fw_ref[...]
        err = xh * w - tg_ref[...]
        part = 0.5 * jnp.sum(jnp.sum(err * err, axis=-1, keepdims=True), axis=0, keepdims=True) / D_MODEL
        dx, dw = _rms_bwd(xh, r, w, err * (1.0 / D_MODEL))
        dx_ref[...] = dx

        @pl.when(i == 0)
        def _():
            loss_ref[...] = jnp.zeros_like(loss_ref)
            dfw_ref[...] = jnp.zeros_like(dfw_ref)

        loss_ref[...] += jnp.broadcast_to(part, loss_ref.shape)
        dfw_ref[...] += dw

    return pl.pallas_call(
        body, name=name, grid=(SEQ // t,),
        in_specs=[pl.BlockSpec((t, D_MODEL), lambda i: (i, 0)),
                  pl.BlockSpec((1, D_MODEL), lambda i: (0, 0)),
                  pl.BlockSpec((t, D_MODEL), lambda i: (i, 0))],
        out_specs=[pl.BlockSpec((1, 128), lambda i: (0, 0)),
                   pl.BlockSpec((t, D_MODEL), lambda i: (i, 0)),
                   pl.BlockSpec((1, D_MODEL), lambda i: (0, 0))],
        out_shape=[SDS((1, 128), F32), SDS((SEQ, D_MODEL), F32), SDS((1, D_MODEL), F32)],
        compiler_params=_cparams(("arbitrary",)),
    )(x, fw, target)


MIX_T = 256


def mix_in_fwd(x, nw, wext, cos, sin, name):
    t = MIX_T

    def body(x_ref, nw_ref, w_ref, cos_ref, sin_ref, att_ref, rest_ref):
        xh, _r = _rms_stats(x_ref[...])
        h = (xh * nw_ref[...]).astype(BF16)
        pa = _dot(h, w_ref[:, 0:EXT_ATT])
        c = cos_ref[...]
        s = sin_ref[...]
        att_ref[:, 0:256] = pa[:, 0:256] * c + pa[:, 768:1024] * s
        att_ref[:, 256:512] = pa[:, 256:512] * c + pa[:, 1024:1280] * s
        att_ref[:, 512:768] = pa[:, 512:768]
        for j in range(EXT_REST // 256):
            rest_ref[:, 256 * j:256 * j + 256] = _dot(h, w_ref[:, EXT_ATT + 256 * j:EXT_ATT + 256 * j + 256])

    return pl.pallas_call(
        body, name=name, grid=(SEQ // t,),
        in_specs=[pl.BlockSpec((t, D_MODEL), lambda i: (i, 0)),
                  pl.BlockSpec((1, D_MODEL), lambda i: (0, 0)),
                  pl.BlockSpec((D_MODEL, EXT_W), lambda i: (0, 0)),
                  pl.BlockSpec((t, ATT_W), lambda i: (i, 0)),
                  pl.BlockSpec((t, ATT_W), lambda i: (i, 0))],
        out_specs=[pl.BlockSpec((t, 768), lambda i: (i, 0)),
                   pl.BlockSpec((t, EXT_REST), lambda i: (i, 0))],
        out_shape=[SDS((SEQ, 768), F32), SDS((SEQ, EXT_REST), F32)],
        compiler_params=_cparams(("arbitrary",)),
    )(x, nw, wext, cos, sin)


def assemble_dproj(datts, cos, sin, d_dqkv, dz, dbb, dab, dpu, name):
    t = 512

    def body(d1_ref, d4_ref, d16_ref, cos_ref, sin_ref, dqkv_ref, dz_ref, dbb_ref, dab_ref, dpu_ref, o_ref):
        da = d1_ref[...] + d4_ref[...] + d16_ref[...]
        c = cos_ref[...]
        s = sin_ref[...]
        dq = da[:, 0:256]
        dk = da[:, 256:512]
        o_ref[:, 0:256] = (dq * c).astype(BF16)
        o_ref[:, 256:512] = (dk * c).astype(BF16)
        o_ref[:, 512:768] = da[:, 512:768].astype(BF16)
        o_ref[:, 768:1024] = (dq * s).astype(BF16)
        o_ref[:, 1024:1280] = (dk * s).astype(BF16)
        b = EXT_ATT
        o_ref[:, b + R_DQKV:b + R_DQKV + 1536] = dqkv_ref[...].astype(BF16)
        o_ref[:, b + R_DZ:b + R_DZ + 512] = dz_ref[...].astype(BF16)
        o_ref[:, b + R_BB:b + R_BB + 512] = dbb_ref[...].astype(BF16)
        o_ref[:, b + R_AB:b + R_AB + 512] = dab_ref[...].astype(BF16)
        o_ref[:, b + R_PU:b + R_PU + 256] = dpu_ref[...].astype(BF16)

    row = lambda w: pl.BlockSpec((t, w), lambda i: (i, 0))
    return pl.pallas_call(
        body, name=name, grid=(SEQ // t,),
        in_specs=[row(768), row(768), row(768), row(256), row(256), row(1536), row(512), row(512), row(512), row(256)],
        out_specs=row(EXT_W),
        out_shape=SDS((SEQ, EXT_W), BF16),
        compiler_params=_cparams(("arbitrary",)),
    )(*datts, cos, sin, d_dqkv, dz, dbb, dab, dpu)


def linear_bwd(x, dxo, nw, dy, w, name):
    t = 512
    nb = 512
    n = w.shape[1]
    nt = SEQ // t
    nn = n // nb

    def body(x_ref, dxo_ref, nw_ref, dy_ref, w_ref, dx_ref, dw_ref, dnw_ref, dh_scr):
        k = pl.program_id(0)
        i = pl.program_id(1)
        rows = pl.ds(pl.multiple_of(i * t, t), t)
        xh, r = _rms_stats(x_ref[...])
        nw_v = nw_ref[...]
        h = (xh * nw_v).astype(BF16)
        dyv = dy_ref[...]
        p_w = _dot_tn(h, dyv)
        dh = _dot_nt(dyv, w_ref[...])

        @pl.when(i == 0)
        def _():
            dw_ref[...] = p_w

        @pl.when(i > 0)
        def _():
            dw_ref[...] += p_w

        @pl.when(k == 0)
        def _():
            dh_scr[rows, :] = dh

        @pl.when(k > 0)
        def _():
            dh_scr[rows, :] += dh

        @pl.when(jnp.logical_and(k == 0, i == 0))
        def _():
            dnw_ref[...] = jnp.zeros_like(dnw_ref)

        @pl.when(k == nn - 1)
        def _():
            dx, dw = _rms_bwd(xh, r, nw_v, dh_scr[rows, :])
            dx_ref[...] = dxo_ref[...] + dx
            dnw_ref[...] += dw

    last = nn - 1
    return pl.pallas_call(
        body, name=name, grid=(nn, nt),
        in_specs=[pl.BlockSpec((t, D_MODEL), lambda k, i: (i, 0)),
                  pl.BlockSpec((t, D_MODEL), lambda k, i: (i, 0)),
                  pl.BlockSpec((1, D_MODEL), lambda k, i: (0, 0)),
                  pl.BlockSpec((t, nb), lambda k, i: (i, k)),
                  pl.BlockSpec((D_MODEL, nb), lambda k, i: (0, k))],
        out_specs=[pl.BlockSpec((t, D_MODEL), lambda k, i: (jnp.where(k == last, i, 0), 0)),
                   pl.BlockSpec((D_MODEL, nb), lambda k, i: (0, k)),
                   pl.BlockSpec((1, D_MODEL), lambda k, i: (0, 0))],
        out_shape=[SDS((SEQ, D_MODEL), F32), SDS((D_MODEL, n), F32), SDS((1, D_MODEL), F32)],
        scratch_shapes=[pltpu.VMEM((SEQ, D_MODEL), F32)],
        compiler_params=_cparams(("arbitrary", "arbitrary")),
    )(x, dxo, nw, dy, w)


def _att_masks():
    qi = lax.broadcasted_iota(jnp.int32, (ATT_BLK, ATT_BLK), 0)
    ki = lax.broadcasted_iota(jnp.int32, (ATT_BLK, ATT_BLK), 1)
    return ki <= qi, ki >= qi


NEG = -1e30


def att_fwd(att, blocks_per_class, name):
    nblk = SEQ // ATT_BLK

    def body(cur_ref, prev_ref, o_ref):
        i = pl.program_id(0)
        has_prev = (i % blocks_per_class) != 0
        m_d, m_p = _att_masks()
        m_p = jnp.logical_and(m_p, has_prev)
        for hd in range(4):
            sl = slice(ATT_E * hd, ATT_E * hd + ATT_E)
            q = cur_ref[:, sl].astype(BF16)
            kc = cur_ref[:, 256 + ATT_E * hd:256 + ATT_E * hd + ATT_E].astype(BF16)
            vc = cur_ref[:, 512 + ATT_E * hd:512 + ATT_E * hd + ATT_E].astype(BF16)
            kp = prev_ref[:, 256 + ATT_E * hd:256 + ATT_E * hd + ATT_E].astype(BF16)
            vp = prev_ref[:, 512 + ATT_E * hd:512 + ATT_E * hd + ATT_E].astype(BF16)
            sd = jnp.where(m_d, _dot_nt(q, kc) * 0.125, NEG)
            sp = jnp.where(m_p, _dot_nt(q, kp) * 0.125, NEG)
            m = jnp.maximum(jnp.max(sd, axis=-1, keepdims=True), jnp.max(sp, axis=-1, keepdims=True))
            pd = jnp.exp(sd - m)
            pp = jnp.exp(sp - m)
            den = jnp.sum(pd, axis=-1, keepdims=True) + jnp.sum(pp, axis=-1, keepdims=True)
            inv = 1.0 / den
            o = _dot((pd * inv).astype(BF16), vc) + _dot((pp * inv).astype(BF16), vp)
            o_ref[:, sl] = o
            o_ref[:, 256 + ATT_E * hd:256 + ATT_E * hd + ATT_E] = jnp.broadcast_to(m + jnp.log(den), (ATT_BLK, ATT_E))

    return pl.pallas_call(
        body, name=name, grid=(nblk,),
        in_specs=[pl.BlockSpec((ATT_BLK, 768), lambda i: (i, 0)),
                  pl.BlockSpec((ATT_BLK, 768), lambda i: (jnp.maximum(i - 1, 0), 0))],
        out_specs=pl.BlockSpec((ATT_BLK, 512), lambda i: (i, 0)),
        out_shape=SDS((SEQ, 512), F32),
        compiler_params=_cparams(("arbitrary",)),
    )(att, att)


def att_bwd(att, ol, dol, blocks_per_class, name):
    nblk = SEQ // ATT_BLK

    def body(prev_ref, cur_ref, nxt_ref, ol_c_ref, ol_n_ref, dol_c_ref, dol_n_ref, d_ref):
        i = pl.program_id(0)
        has_prev = (i % blocks_per_class) != 0
        has_next = ((i + 1) % blocks_per_class) != 0
        m_d, m_band = _att_masks()
        m_p = jnp.logical_and(m_band, has_prev)
        m_n = jnp.logical_and(m_band, has_next)

        def pair(q, k, v, lse, do, dterm, mask):
            s = jnp.where(mask, _dot_nt(q, k) * 0.125, NEG)
            p = jnp.exp(s - lse)
            dp = _dot_nt(do, v)
            ds = (p * (dp + dterm) * 0.125).astype(BF16)
            return p.astype(BF16), ds

        for hd in range(4):
            a = ATT_E * hd
            q_c = cur_ref[:, a:a + ATT_E].astype(BF16)
            k_c = cur_ref[:, 256 + a:256 + a + ATT_E].astype(BF16)
            v_c = cur_ref[:, 512 + a:512 + a + ATT_E].astype(BF16)
            k_p = prev_ref[:, 256 + a:256 + a + ATT_E].astype(BF16)
            v_p = prev_ref[:, 512 + a:512 + a + ATT_E].astype(BF16)
            q_n = nxt_ref[:, a:a + ATT_E].astype(BF16)
            o_c = ol_c_ref[:, a:a + ATT_E]
            o_n = ol_n_ref[:, a:a + ATT_E]
            lse_c = ol_c_ref[:, 256 + a:256 + a + ATT_E][:, 0:1]
            lse_n = ol_n_ref[:, 256 + a:256 + a + ATT_E][:, 0:1]
            do_c = dol_c_ref[:, a:a + ATT_E]
            do_n = dol_n_ref[:, a:a + ATT_E]
            t_c = dol_c_ref[:, 256 + a:256 + a + ATT_E][:, 0:1] - jnp.sum(do_c * o_c, axis=-1, keepdims=True)
            t_n = dol_n_ref[:, 256 + a:256 + a + ATT_E][:, 0:1] - jnp.sum(do_n * o_n, axis=-1, keepdims=True)
            do_cb = do_c.astype(BF16)
            do_nb = do_n.astype(BF16)
            p1, ds1 = pair(q_c, k_c, v_c, lse_c, do_cb, t_c, m_d)
            _p2, ds2 = pair(q_c, k_p, v_p, lse_c, do_cb, t_c, m_p)
            p3, ds3 = pair(q_n, k_c, v_c, lse_n, do_nb, t_n, m_n)
            d_ref[:, a:a + ATT_E] = _dot(ds1, k_c) + _dot(ds2, k_p)
            d_ref[:, 256 + a:256 + a + ATT_E] = _dot_tn(ds1, q_c) + _dot_tn(ds3, q_n)
            d_ref[:, 512 + a:512 + a + ATT_E] = _dot_tn(p1, do_cb) + _dot_tn(p3, do_nb)

    prv = lambda i: (jnp.maximum(i - 1, 0), 0)
    cur = lambda i: (i, 0)
    nxt = lambda i: (jnp.minimum(i + 1, nblk - 1), 0)
    return pl.pallas_call(
        body, name=name, grid=(nblk,),
        in_specs=[pl.BlockSpec((ATT_BLK, 768), prv), pl.BlockSpec((ATT_BLK, 768), cur),
                  pl.BlockSpec((ATT_BLK, 768), nxt),
                  pl.BlockSpec((ATT_BLK, 512), cur), pl.BlockSpec((ATT_BLK, 512), nxt),
                  pl.BlockSpec((ATT_BLK, 512), cur), pl.BlockSpec((ATT_BLK, 512), nxt)],
        out_specs=pl.BlockSpec((ATT_BLK, 768), cur),
        out_shape=SDS((SEQ, 768), F32),
        compiler_params=_cparams(("arbitrary",)),
    )(att, att, att, ol, ol, dol, dol)


def to_classes(a, d):
    if d == 1:
        return a
    w = a.shape[1]
    return a.reshape(SEQ // d, d, w).transpose(1, 0, 2).reshape(SEQ, w)


def from_classes(a, d):
    if d == 1:
        return a
    w = a.shape[1]
    return a.reshape(d, SEQ // d, w).transpose(1, 0, 2).reshape(SEQ, w)


def _shift_down(x, k):
    rows = lax.broadcasted_iota(jnp.int32, x.shape, 0)
    return jnp.where(rows >= k, pltpu.roll(x, k, 0), 0.0)


def _shift_up(x, k):
    n = x.shape[0]
    rows = lax.broadcasted_iota(jnp.int32, x.shape, 0)
    return jnp.where(rows < n - k, pltpu.roll(x, n - k, 0), 0.0)


@functools.partial(jax.custom_vjp, nondiff_argnums=(1,))
def _delay(x, k):
    return _shift_down(x, k)


def _delay_fwd(x, k):
    return _shift_down(x, k), None


def _delay_bwd(k, _res, g):
    return (_shift_up(g, k),)


_delay.defvjp(_delay_fwd, _delay_bwd)

DN_CONV = 4


def _dn_prep_fn(u, w, kind):
    y = w[DN_CONV - 1:DN_CONV] * u
    for j in range(DN_CONV - 1):
        y = y + w[j:j + 1] * _delay(u, DN_CONV - 1 - j)
    y = y * _sigmoid(y)
    nrm = y * lax.rsqrt(jnp.sum(y * y, axis=-1, keepdims=True) + EPS)
    return jnp.where(kind == 0, nrm * (DN_E ** -0.5), jnp.where(kind == 1, nrm, y))


def dn_prep_fwd(rest, conv_w, name):
    def body(u_ref, w_ref, o_ref):
        j = pl.program_id(0)
        kind = (j >= DN_H).astype(jnp.int32) + (j >= 2 * DN_H).astype(jnp.int32)
        o_ref[...] = _dn_prep_fn(u_ref[...], w_ref[...], kind)

    return pl.pallas_call(
        body, name=name, grid=(3 * DN_H,),
        in_specs=[pl.BlockSpec((SEQ, DN_E), lambda j: (0, j)),
                  pl.BlockSpec((DN_CONV, DN_E), lambda j: (0, j))],
        out_specs=pl.BlockSpec((SEQ, DN_E), lambda j: (0, j)),
        out_shape=SDS((SEQ, 3 * DN_W), F32),
        compiler_params=_cparams(("arbitrary",)),
    )(rest, conv_w)


def dn_prep_bwd(rest, conv_w, dqkv, name):
    def body(u_ref, w_ref, g_ref, du_ref, dw_ref):
        j = pl.program_id(0)
        kind = (j >= DN_H).astype(jnp.int32) + (j >= 2 * DN_H).astype(jnp.int32)
        _y, vjp = jax.vjp(lambda u, w: _dn_prep_fn(u, w, kind), u_ref[...], w_ref[...])
        du, dw = vjp(g_ref[...])
        du_ref[...] = du
        dw_ref[...] = dw

    return pl.pallas_call(
        body, name=name, grid=(3 * DN_H,),
        in_specs=[pl.BlockSpec((SEQ, DN_E), lambda j: (0, j)),
                  pl.BlockSpec((DN_CONV, DN_E), lambda j: (0, j)),
                  pl.BlockSpec((SEQ, DN_E), lambda j: (0, j))],
        out_specs=[pl.BlockSpec((SEQ, DN_E), lambda j: (0, j)),
                   pl.BlockSpec((DN_CONV, DN_E), lambda j: (0, j))],
        out_shape=[SDS((SEQ, 3 * DN_W), F32), SDS((DN_CONV, 3 * DN_W), F32)],
        compiler_params=_cparams(("arbitrary",)),
    )(rest, conv_w, dqkv)


def _eye(n):
    return (lax.broadcasted_iota(jnp.int32, (n, n), 0) == lax.broadcasted_iota(jnp.int32, (n, n), 1)).astype(F32)


def _unit_lower_inverse(a):
    p = _eye(DN_C) - a
    b = _dot(a, a, HI)
    for lvl in range(5):
        p = p + _dot(p, b, HI)
        if lvl < 4:
            b = _dot(b, b, HI)
    return p


@jax.custom_vjp
def _tri_inv(a):
    return _unit_lower_inverse(a)


def _tri_inv_fwd(a):
    t = _unit_lower_inverse(a)
    return t, t


def _tri_inv_bwd(t, g):
    return (-_dot_nt(_dot_tn(t, g, HI), t, HI),)


_tri_inv.defvjp(_tri_inv_fwd, _tri_inv_bwd)


def _b16(x):
    return x.astype(BF16)


def _dn_chunk(q, k, v, bb, ab, alog, dtb, state):
    ri = lax.broadcasted_iota(jnp.int32, (DN_C, DN_C), 0)
    ci = lax.broadcasted_iota(jnp.int32, (DN_C, DN_C), 1)
    lower = ri >= ci
    strict = ri > ci
    beta = _sigmoid(bb)
    xg = ab + dtb
    softplus = jnp.maximum(xg, 0.0) + jnp.log(1.0 + jnp.exp(-jnp.abs(xg)))
    gi = -jnp.exp(alog) * softplus
    g = _dot(lower.astype(F32), gi, HI)
    eg = jnp.exp(g)
    kb = k * beta
    vb = v * beta
    g_col = g[:, 0:DN_C]
    g_row = _dot_nt(jnp.full((DN_C, DN_E), 1.0 / DN_E, F32), g, HI)
    decay = jnp.where(lower, jnp.exp(jnp.where(lower, g_col - g_row, 0.0)), 0.0)
    a = jnp.where(strict, _dot_nt(_b16(kb), _b16(k)) * decay, 0.0)
    t = _tri_inv(a)
    tb = _b16(t)
    u = _dot(tb, _b16(vb))
    w = _dot(tb, _b16(kb * eg))
    intra = jnp.where(lower, _dot_nt(_b16(q), _b16(k)) * decay, 0.0)
    sb = _b16(state)
    v_new = u - _dot(_b16(w), sb)
    o = _dot(_b16(q * eg), sb) + _dot(_b16(intra), _b16(v_new))
    g_last = g[DN_C - 1:DN_C, :]
    k_dec = k * jnp.exp(g_last - g)
    new_state = state * jnp.exp(g_last) + _dot_tn(_b16(k_dec), _b16(v_new))
    return o, new_state


def dn_chunk_fwd(qkv, rest, alog_b, dtb_b, name):
    def body(qkv_ref, bb_ref, ab_ref, alog_ref, dtb_ref, o_ref, st_ref, state_scr):
        n = pl.program_id(0)

        @pl.when(n == 0)
        def _():
            state_scr[...] = jnp.zeros_like(state_scr)

        for hd in range(DN_H):
            c = DN_E * hd
            sl = slice(c, c + DN_E)
            st = state_scr[hd]
            st_ref[0, hd] = st
            o, ns = _dn_chunk(qkv_ref[:, c:c + DN_E], qkv_ref[:, DN_W + c:DN_W + c + DN_E],
                              qkv_ref[:, 2 * DN_W + c:2 * DN_W + c + DN_E],
                              bb_ref[:, sl], ab_ref[:, sl], alog_ref[:, sl], dtb_ref[:, sl], st)
            o_ref[:, sl] = o
            state_scr[hd] = ns

    return pl.pallas_call(
        body, name=name, grid=(N_CHUNK,),
        in_specs=[pl.BlockSpec((DN_C, 3 * DN_W), lambda n: (n, 0)),
                  pl.BlockSpec((DN_C, DN_W), lambda n: (n, R_BB // DN_W)),
                  pl.BlockSpec((DN_C, DN_W), lambda n: (n, R_AB // DN_W)),
                  pl.BlockSpec((1, DN_W), lambda n: (0, 0)),
                  pl.BlockSpec((1, DN_W), lambda n: (0, 0))],
        out_specs=[pl.BlockSpec((DN_C, DN_W), lambda n: (n, 0)),
                   pl.BlockSpec((1, DN_H, DN_E, DN_E), lambda n: (n, 0, 0, 0))],
        out_shape=[SDS((SEQ, DN_W), F32), SDS((N_CHUNK, DN_H, DN_E, DN_E), F32)],
        scratch_shapes=[pltpu.VMEM((DN_H, DN_E, DN_E), F32)],
        compiler_params=_cparams(("arbitrary",)),
    )(qkv, rest, rest, alog_b, dtb_b)


def dn_chunk_bwd(qkv, rest, alog_b, dtb_b, states, do, name):
    last = N_CHUNK - 1

    def body(qkv_ref, bb_ref, ab_ref, alog_ref, dtb_ref, st_ref, do_ref,
             dqkv_ref, dbb_ref, dab_ref, dalog_ref, ddtb_ref, dstate_scr):
        s = pl.program_id(0)

        @pl.when(s == 0)
        def _():
            dstate_scr[...] = jnp.zeros_like(dstate_scr)
            dalog_ref[...] = jnp.zeros_like(dalog_ref)
            ddtb_ref[...] = jnp.zeros_like(ddtb_ref)

        for hd in range(DN_H):
            c = DN_E * hd
            sl = slice(c, c + DN_E)
            args = (qkv_ref[:, c:c + DN_E], qkv_ref[:, DN_W + c:DN_W + c + DN_E],
                    qkv_ref[:, 2 * DN_W + c:2 * DN_W + c + DN_E],
                    bb_ref[:, sl], ab_ref[:, sl], alog_ref[:, sl], dtb_ref[:, sl], st_ref[0, hd])
            _out, vjp = jax.vjp(_dn_chunk, *args)
            dq, dk, dv, dbb, dab, dalog, ddtb, dst = vjp((do_ref[:, sl], dstate_scr[hd]))
            dqkv_ref[:, c:c + DN_E] = dq
            dqkv_ref[:, DN_W + c:DN_W + c + DN_E] = dk
            dqkv_ref[:, 2 * DN_W + c:2 * DN_W + c + DN_E] = dv
            dbb_ref[:, sl] = dbb
            dab_ref[:, sl] = dab
            dalog_ref[:, sl] += dalog
            ddtb_ref[:, sl] += ddtb
            dstate_scr[hd] = dst

    rev = lambda w: (lambda s: (last - s, w))
    return pl.pallas_call(
        body, name=name, grid=(N_CHUNK,),
        in_specs=[pl.BlockSpec((DN_C, 3 * DN_W), rev(0)),
                  pl.BlockSpec((DN_C, DN_W), rev(R_BB // DN_W)),
                  pl.BlockSpec((DN_C, DN_W), rev(R_AB // DN_W)),
                  pl.BlockSpec((1, DN_W), lambda s: (0, 0)),
                  pl.BlockSpec((1, DN_W), lambda s: (0, 0)),
                  pl.BlockSpec((1, DN_H, DN_E, DN_E), lambda s: (last - s, 0, 0, 0)),
                  pl.BlockSpec((DN_C, DN_W), rev(0))],
        out_specs=[pl.BlockSpec((DN_C, 3 * DN_W), rev(0)),
                   pl.BlockSpec((DN_C, DN_W), rev(0)),
                   pl.BlockSpec((DN_C, DN_W), rev(0)),
                   pl.BlockSpec((1, DN_W), lambda s: (0, 0)),
                   pl.BlockSpec((1, DN_W), lambda s: (0, 0))],
        out_shape=[SDS((SEQ, 3 * DN_W), F32), SDS((SEQ, DN_W), F32), SDS((SEQ, DN_W), F32),
                   SDS((1, DN_W), F32), SDS((1, DN_W), F32)],
        scratch_shapes=[pltpu.VMEM((DN_H, DN_E, DN_E), F32)],
        compiler_params=_cparams(("arbitrary",)),
    )(qkv, rest, rest, alog_b, dtb_b, states, do)


OUT_T = 256


def _pool_consts(rows_total, t0, halo_before):
    lane = lax.broadcasted_iota(jnp.int32, (rows_total, POOL_W), 1)
    row = lax.broadcasted_iota(jnp.int32, (rows_total, POOL_W), 0)
    grp = (lane >= 64).astype(jnp.int32) + (lane >= 128).astype(jnp.int32) + (lane >= 192).astype(jnp.int32)
    win = jnp.where(grp == 0, 2, jnp.where(grp == 1, 4, jnp.where(grp == 2, 8, 16)))
    pos = t0 + row - halo_before
    cnt = jnp.minimum(pos + 1, win).astype(F32)
    return grp, cnt


def _pool_select(grp, s2, s4, s8, s16):
    return jnp.where(grp == 0, s2, jnp.where(grp == 1, s4, jnp.where(grp == 2, s8, s16)))


def _pooled(u_ext, t0):
    n = u_ext.shape[0]
    grp, cnt = _pool_consts(n, t0, POOL_HALO)
    s2 = u_ext + pltpu.roll(u_ext, 1, 0)
    s4 = s2 + pltpu.roll(s2, 2, 0)
    s8 = s4 + pltpu.roll(s4, 4, 0)
    s16 = s8 + pltpu.roll(s8, 8, 0)
    out = _pool_select(grp, s2, s4, s8, s16) / jnp.maximum(cnt, 1.0) - u_ext
    return out[POOL_HALO:, :]


def _merge_weights(l1, l4, l16):
    m = jnp.maximum(jnp.maximum(l1, l4), l16)
    e1 = jnp.exp(l1 - m)
    e4 = jnp.exp(l4 - m)
    e16 = jnp.exp(l16 - m)
    inv = 1.0 / (e1 + e4 + e16)
    return e1 * inv, e4 * inv, e16 * inv


def _out_parts(ol1_ref, ol4_ref, ol16_ref, pu_ref, puh_ref, odn_ref, z_ref, wbd_ref, i, t):
    w1, w4, w16 = _merge_weights(ol1_ref[:, 256:512], ol4_ref[:, 256:512], ol16_ref[:, 256:512])
    ya = w1 * ol1_ref[:, 0:256] + w4 * ol4_ref[:, 0:256] + w16 * ol16_ref[:, 0:256]
    halo = jnp.where(i > 0, puh_ref[...], 0.0)
    pooled = _pooled(jnp.concatenate([halo, pu_ref[...]], axis=0), i * t)
    pw = _dot(pooled.astype(BF16), wbd_ref[...])
    return ya, pooled, pw, (w1, w4, w16)


def _out_specs_common(t):
    def row(w, cb=0):
        return pl.BlockSpec((t, w), lambda i: (i, cb))

    halo = pl.BlockSpec((POOL_HALO, POOL_W),
                        lambda i: (jnp.maximum(i * (t // POOL_HALO) - 1, 0), R_PU // POOL_W))
    full = lambda a, b: pl.BlockSpec((a, b), lambda i: (0, 0))
    return [row(512), row(512), row(512), row(POOL_W, R_PU // POOL_W), halo, row(DN_W), row(DN_W, R_DZ // DN_W),
            full(POOL_W, POOL_W), full(1, POOL_W), full(1, DN_W), full(D_MODEL, D_MODEL)]


def mix_out_fwd(x, ol1, ol4, ol16, rest, odn, wbd, scale, onorm_b, wout, name):
    t = OUT_T

    def body(x_ref, ol1_ref, ol4_ref, ol16_ref, pu_ref, puh_ref, odn_ref, z_ref, wbd_ref, sc_ref, on_ref, wo_ref, o_ref):
        i = pl.program_id(0)
        ya, _pooled_v, pw, _w = _out_parts(ol1_ref, ol4_ref, ol16_ref, pu_ref, puh_ref, odn_ref, z_ref, wbd_ref, i, t)
        yb = pw * sc_ref[...]
        acc = x_ref[...] + _dot(ya.astype(BF16), wo_ref[0:256, :]) + _dot(yb.astype(BF16), wo_ref[256:512, :])
        for hd in range(DN_H):
            sl = slice(DN_E * hd, DN_E * hd + DN_E)
            oh, _r = _rms_stats(odn_ref[:, sl])
            z = z_ref[:, sl]
            yc = oh * on_ref[:, sl] * (z * _sigmoid(z))
            acc = acc + _dot(yc.astype(BF16), wo_ref[512 + DN_E * hd:512 + DN_E * hd + DN_E, :])
        o_ref[...] = acc

    return pl.pallas_call(
        body, name=name, grid=(SEQ // t,),
        in_specs=[pl.BlockSpec((t, D_MODEL), lambda i: (i, 0))] + _out_specs_common(t),
        out_specs=pl.BlockSpec((t, D_MODEL), lambda i: (i, 0)),
        out_shape=SDS((SEQ, D_MODEL), F32),
        compiler_params=_cparams(("arbitrary",)),
    )(x, ol1, ol4, ol16, rest, rest, odn, rest, wbd, scale, onorm_b, wout)


def mix_out_bwd(dxo, ol1, ol4, ol16, rest, odn, wbd, scale, onorm_b, wout, headsum, name):
    t = OUT_T

    def body(dxo_ref, ol1_ref, ol4_ref, ol16_ref, pu_ref, puh_ref, odn_ref, z_ref, wbd_ref, sc_ref, on_ref, wo_ref, hs_ref,
             dwo_ref, d1_ref, d4_ref, d16_ref, dpl_ref, dodn_ref, dz_ref, dsc_ref, don_ref, dwbd_ref):
        i = pl.program_id(0)

        @pl.when(i == 0)
        def _():
            dwo_ref[...] = jnp.zeros_like(dwo_ref)
            dsc_ref[...] = jnp.zeros_like(dsc_ref)
            don_ref[...] = jnp.zeros_like(don_ref)
            dwbd_ref[...] = jnp.zeros_like(dwbd_ref)

        ya, pooled, pw, (w1, w4, w16) = _out_parts(ol1_ref, ol4_ref, ol16_ref, pu_ref, puh_ref, odn_ref, z_ref, wbd_ref, i, t)
        sc = sc_ref[...]
        dxb = dxo_ref[...].astype(BF16)
        dwo_ref[0:256, :] += _dot_tn(ya.astype(BF16), dxb)
        dwo_ref[256:512, :] += _dot_tn((pw * sc).astype(BF16), dxb)
        dya = _dot_nt(dxb, wo_ref[0:256, :])
        o1 = ol1_ref[:, 0:256]
        o4 = ol4_ref[:, 0:256]
        o16 = ol16_ref[:, 0:256]
        hs = hs_ref[...]
        s1 = _dot(dya * o1, hs, HI)
        s4 = _dot(dya * o4, hs, HI)
        s16 = _dot(dya * o16, hs, HI)
        sbar = w1 * s1 + w4 * s4 + w16 * s16
        d1_ref[:, 0:256] = w1 * dya
        d1_ref[:, 256:512] = w1 * (s1 - sbar)
        d4_ref[:, 0:256] = w4 * dya
        d4_ref[:, 256:512] = w4 * (s4 - sbar)
        d16_ref[:, 0:256] = w16 * dya
        d16_ref[:, 256:512] = w16 * (s16 - sbar)
        dyb = _dot_nt(dxb, wo_ref[256:512, :])
        dsc_ref[...] += jnp.sum(dyb * pw, axis=0, keepdims=True)
        dpw = (dyb * sc).astype(BF16)
        dwbd_ref[...] += _dot_tn(pooled.astype(BF16), dpw)
        dpl_ref[...] = _dot_nt(dpw, wbd_ref[...])
        for hd in range(DN_H):
            sl = slice(DN_E * hd, DN_E * hd + DN_E)
            rows_w = slice(512 + DN_E * hd, 512 + DN_E * hd + DN_E)
            oh, r = _rms_stats(odn_ref[:, sl])
            z = z_ref[:, sl]
            sg = _sigmoid(z)
            sz = z * sg
            nw = on_ref[:, sl]
            on = oh * nw
            dwo_ref[rows_w, :] += _dot_tn((on * sz).astype(BF16), dxb)
            dyc = _dot_nt(dxb, wo_ref[rows_w, :])
            dz_ref[:, sl] = dyc * on * (sg * (1.0 + z * (1.0 - sg)))
            dx, dw = _rms_bwd(oh, r, nw, dyc * sz)
            dodn_ref[:, sl] = dx
            don_ref[:, sl] += dw

    row = lambda w: pl.BlockSpec((t, w), lambda i: (i, 0))
    full = lambda a, b: pl.BlockSpec((a, b), lambda i: (0, 0))
    return pl.pallas_call(
        body, name=name, grid=(SEQ // t,),
        in_specs=[row(D_MODEL)] + _out_specs_common(t) + [full(ATT_W, ATT_W)],
        out_specs=[full(D_MODEL, D_MODEL), row(512), row(512), row(512), row(POOL_W), row(DN_W), row(DN_W),
                   full(1, POOL_W), full(1, DN_W), full(POOL_W, POOL_W)],
        out_shape=[SDS((D_MODEL, D_MODEL), F32), SDS((SEQ, 512), F32), SDS((SEQ, 512), F32), SDS((SEQ, 512), F32),
                   SDS((SEQ, POOL_W), F32), SDS((SEQ, DN_W), F32), SDS((SEQ, DN_W), F32),
                   SDS((1, POOL_W), F32), SDS((1, DN_W), F32), SDS((POOL_W, POOL_W), F32)],
        compiler_params=_cparams(("arbitrary",)),
    )(dxo, ol1, ol4, ol16, rest, rest, odn, rest, wbd, scale, onorm_b, wout, headsum)


def pool_bwd(dpooled, name):
    t = 512
    nt = SEQ // t

    def body(d_ref, dn_ref, o_ref):
        i = pl.program_id(0)
        halo = jnp.where(i < nt - 1, dn_ref[...], 0.0)
        d_ext = jnp.concatenate([d_ref[...], halo], axis=0)
        n = t + POOL_HALO
        grp, cnt = _pool_consts(n, i * t, 0)
        dq = d_ext / cnt
        s2 = dq + pltpu.roll(dq, n - 1, 0)
        s4 = s2 + pltpu.roll(s2, n - 2, 0)
        s8 = s4 + pltpu.roll(s4, n - 4, 0)
        s16 = s8 + pltpu.roll(s8, n - 8, 0)
        o_ref[...] = (_pool_select(grp, s2, s4, s8, s16) - d_ext)[0:t, :]

    return pl.pallas_call(
        body, name=name, grid=(nt,),
        in_specs=[pl.BlockSpec((t, POOL_W), lambda i: (i, 0)),
                  pl.BlockSpec((POOL_HALO, POOL_W),
                               lambda i: (jnp.minimum((i + 1) * (t // POOL_HALO), SEQ // POOL_HALO - 1), 0))],
        out_specs=pl.BlockSpec((t, POOL_W), lambda i: (i, 0)),
        out_shape=SDS((SEQ, POOL_W), F32),
        compiler_params=_cparams(("arbitrary",)),
    )(dpooled, dpooled)


def _peers():
    x, y, c = lax.axis_index("x"), lax.axis_index("y"), lax.axis_index("c")
    out = []
    for fx, fy, fc in ((0, 0, 1), (1, 0, 0), (0, 1, 0), (1, 1, 0), (1, 0, 1), (0, 1, 1), (1, 1, 1)):
        px, py, pc = x ^ fx, y ^ fy, c ^ fc
        out.append(((px, py, pc), 4 * px + 2 * py + pc))
    return 4 * x + 2 * y + c, out


def exchange(arrays, scatter, name):
    n = len(arrays)
    n_peer = N_DEV - 1

    def body(*refs):
        ins, outs = refs[:n], refs[n:2 * n]
        send_sems, recv_sems, local_sems = refs[2 * n:]
        me, peers = _peers()
        sends, recvs, locals_ = [], [], []
        for a in range(n):
            src_me = ins[a].at[me] if scatter else ins[a]
            lc = pltpu.make_async_copy(src_me, outs[a].at[me], local_sems.at[a])
            lc.start()
            locals_.append(lc)
            for j, (peer, pidx) in enumerate(peers):
                src = ins[a].at[pidx] if scatter else ins[a]
                cp = pltpu.make_async_remote_copy(
                    src_ref=src, dst_ref=outs[a].at[me],
                    send_sem=send_sems.at[a * n_peer + j], recv_sem=recv_sems.at[a * n_peer + j],
                    device_id=peer, device_id_type=MESH)
                cp.start()
                sends.append(cp)
                recvs.append(pltpu.make_async_remote_copy(
                    src_ref=src, dst_ref=outs[a].at[pidx],
                    send_sem=send_sems.at[a * n_peer + j], recv_sem=recv_sems.at[a * n_peer + j],
                    device_id=peer, device_id_type=MESH))
        for cp in recvs:
            cp.wait_recv()
        for cp in sends:
            cp.wait_send()
        for lc in locals_:
            lc.wait()

    any_spec = pl.BlockSpec(memory_space=pl.ANY)
    if scatter:
        out_shape = [SDS(a.shape, a.dtype) for a in arrays]
    else:
        out_shape = [SDS((N_DEV,) + a.shape, a.dtype) for a in arrays]
    return pl.pallas_call(
        body, name=name,
        in_specs=[any_spec] * n, out_specs=[any_spec] * n, out_shape=out_shape,
        scratch_shapes=[pltpu.SemaphoreType.DMA((n * n_peer,)), pltpu.SemaphoreType.DMA((n * n_peer,)),
                        pltpu.SemaphoreType.DMA((n,))],
    )(*arrays)


def _adam_math(w, g, m, v):
    m2 = ADAM_B1 * m + (1.0 - ADAM_B1) * g
    v2 = ADAM_B2 * v + (1.0 - ADAM_B2) * (g * g)
    m_hat = m2 / (1.0 - ADAM_B1 ** ADAM_STEP)
    v_hat = v2 / (1.0 - ADAM_B2 ** ADAM_STEP)
    delta = -ADAM_LR * (m_hat / (jnp.sqrt(v_hat) + ADAM_EPS) + ADAM_WD * w)
    return delta, m2, v2


def adam_shard(parts, w, m, v, name, part_slice=None):
    _, r, c = w.shape
    sub = part_slice

    def body(p_ref, w_ref, m_ref, v_ref, g_ref, d_ref, m2_ref, v2_ref):
        g = p_ref[0, 0].astype(F32)
        for i in range(1, N_DEV):
            g = g + p_ref[i, 0].astype(F32)
        delta, m2, v2 = _adam_math(w_ref[0], g, m_ref[0], v_ref[0])
        g_ref[0] = g
        d_ref[0] = delta
        m2_ref[0] = m2
        v2_ref[0] = v2

    if sub is None:
        p_spec = pl.BlockSpec((N_DEV, 1, r, c), lambda l: (0, l, 0, 0))
    else:
        p_spec = pl.BlockSpec((N_DEV, 1, None, r, c), lambda l: (0, l, sub, 0, 0))
    blk = pl.BlockSpec((1, r, c), lambda l: (l, 0, 0))
    return pl.pallas_call(
        body, name=name, grid=(DEPTH,),
        in_specs=[p_spec, blk, blk, blk], out_specs=[blk] * 4,
        out_shape=[SDS(w.shape, F32)] * 4,
        compiler_params=_cparams(("arbitrary",)),
    )(parts, w, m, v)


def adam_small(parts, w, m, v, name):
    def body(p_ref, w_ref, m_ref, v_ref, g_ref, d_ref, m2_ref, v2_ref):
        g = p_ref[0]
        for i in range(1, N_DEV):
            g = g + p_ref[i]
        delta, m2, v2 = _adam_math(w_ref[...], g, m_ref[...], v_ref[...])
        g_ref[...] = g
        d_ref[...] = delta
        m2_ref[...] = m2
        v2_ref[...] = v2

    return pl.pallas_call(
        body, name=name, out_shape=[SDS(w.shape, F32)] * 4, compiler_params=_cparams(),
    )(parts, w, m, v)


def _rot_cols(w):
    w4 = w.reshape(w.shape[0], 4, 2, 32)
    return jnp.stack([-w4[:, :, 1], w4[:, :, 0]], axis=2).reshape(w.shape[0], ATT_W)


def _rot_cols_t(dw_rot):
    d4 = dw_rot.reshape(dw_rot.shape[0], 4, 2, 32)
    return jnp.stack([d4[:, :, 1], -d4[:, :, 0]], axis=2).reshape(dw_rot.shape[0], ATT_W)


def build_wext(w_in):
    aq, ak, av, pu = w_in[:, 0:256], w_in[:, 256:512], w_in[:, 512:768], w_in[:, 768:1024]
    dqkvz = w_in[:, 1024:3072]
    gates = jnp.repeat(w_in[:, 3072:3080], DN_E, axis=1)
    return jnp.concatenate([aq, ak, av, _rot_cols(aq), _rot_cols(ak), dqkvz, gates, pu], axis=1)


def fold_dwext(d):
    b = EXT_ATT
    aq = d[:, 0:256] + _rot_cols_t(d[:, 768:1024])
    ak = d[:, 256:512] + _rot_cols_t(d[:, 1024:1280])
    av = d[:, 512:768]
    dqkvz = d[:, b:b + 2048]
    gates = d[:, b + R_BB:b + R_BB + 1024].reshape(d.shape[0], 8, DN_E).sum(axis=-1)
    pu = d[:, b + R_PU:b + R_PU + 256]
    return jnp.concatenate([aq, ak, av, pu, dqkvz, gates], axis=1)


def _block_diag(pw):
    z = jnp.zeros((4, 64, 4, 64), pw.dtype)
    for g in range(4):
        z = z.at[g, :, g, :].set(pw[g])
    return z.reshape(POOL_W, POOL_W)


def _diag_blocks(m):
    m4 = m.reshape(4, 64, 4, 64)
    return jnp.stack([m4[g, :, g, :] for g in range(4)], axis=0)


def _lanes(v, reps):
    return jnp.repeat(v, reps)[None, :]


def layer_fwd(p, xa, cos, sin, l):
    xb = ffn_fwd(xa, p["n1"], p["f1gu"], p["f1d"], f"ffn1_fwd_{l}")
    att, rest = mix_in_fwd(xb, p["nm"], p["wext"], cos, sin, f"mix_in_fwd_{l}")
    ols = [from_classes(att_fwd(to_classes(att, d), SEQ // d // ATT_BLK, f"att_fwd_{l}_{d}"), d) for d in DILATIONS]
    qkv = dn_prep_fwd(rest, p["conv"], f"dn_prep_fwd_{l}")
    odn, states = dn_chunk_fwd(qkv, rest, p["alog"], p["dtb"], f"dn_chunk_fwd_{l}")
    xc = mix_out_fwd(xb, ols[0], ols[1], ols[2], rest, odn, p["wbd"], p["scale"], p["onorm"], p["wout"], f"mix_out_fwd_{l}")
    xd = ffn_fwd(xc, p["n2"], p["f2gu"], p["f2d"], f"ffn2_fwd_{l}")
    return xd, dict(xa=xa, xb=xb, xc=xc, att=att, rest=rest, ols=ols, qkv=qkv, odn=odn, states=states)


def layer_bwd(p, s, dx, cos, sin, headsum, l):
    dx, d_f2gu, d_f2d, d_n2 = ffn_bwd(s["xc"], dx, p["n2"], p["f2gu"], p["f2d"], f"ffn2_bwd_{l}")
    (d_wout, dol1, dol4, dol16, dpooled, dodn, dz, dscale, donorm, dwbd) = mix_out_bwd(
        dx, s["ols"][0], s["ols"][1], s["ols"][2], s["rest"], s["odn"], p["wbd"], p["scale"], p["onorm"], p["wout"],
        headsum, f"mix_out_bwd_{l}")
    dpu = pool_bwd(dpooled, f"pool_bwd_{l}")
    dqkv, dbb, dab, dalog, ddtb = dn_chunk_bwd(s["qkv"], s["rest"], p["alog"], p["dtb"], s["states"], dodn, f"dn_chunk_bwd_{l}")
    d_dqkv, dconv = dn_prep_bwd(s["rest"], p["conv"], dqkv, f"dn_prep_bwd_{l}")
    datts = []
    for d, ol, dol in zip(DILATIONS, s["ols"], (dol1, dol4, dol16)):
        da = att_bwd(to_classes(s["att"], d), to_classes(ol, d), to_classes(dol, d), SEQ // d // ATT_BLK, f"att_bwd_{l}_{d}")
        datts.append(from_classes(da, d))
    dproj = assemble_dproj(datts, cos, sin, d_dqkv, dz, dbb, dab, dpu, f"assemble_dproj_{l}")
    dx, d_wext, d_nm = linear_bwd(s["xb"], dx, p["nm"], dproj, p["wext"], f"mix_in_bwd_{l}")
    dx, d_f1gu, d_f1d, d_n1 = ffn_bwd(s["xa"], dx, p["n1"], p["f1gu"], p["f1d"], f"ffn1_bwd_{l}")

    d_win = fold_dwext(d_wext).reshape(D_MODEL, N_DEV, IN_BLK).transpose(1, 0, 2).astype(BF16)
    big = [d_f1gu, d_f1d, d_f2gu, d_f2d, d_win, d_wout.reshape(N_DEV, D_MODEL // N_DEV, D_MODEL).astype(BF16)]
    small = dict(ffn1_norm=d_n1[0], mix_norm=d_nm[0], ffn2_norm=d_n2[0], pool_w=_diag_blocks(dwbd),
                 pool_scale=dscale[0], dn_a_log=dalog.reshape(DN_H, DN_E).sum(-1),
                 dn_dt_bias=ddtb.reshape(DN_H, DN_E).sum(-1),
                 dn_out_norm=donorm.reshape(DN_H, DN_E).sum(0), dn_conv_w=dconv)
    return dx, big, small


def make_layer(l, f1gu, f1d, f2gu, f2d, win_full, wout_full, conv_full, pool_w, pool_scale, dn_out_norm, dn_a_log,
               dn_dt_bias, ffn1_norm, mix_norm, ffn2_norm):
    return dict(
        f1gu=f1gu, f1d=f1d, f2gu=f2gu, f2d=f2d,
        wext=build_wext(win_full), wout=wout_full, conv=conv_full,
        wbd=_block_diag(pool_w[l]).astype(BF16),
        scale=pool_scale[l][None, :],
        onorm=jnp.tile(dn_out_norm[l], DN_H)[None, :],
        alog=_lanes(dn_a_log[l], DN_E),
        dtb=_lanes(dn_dt_bias[l], DN_E),
        n1=ffn1_norm[l][None, :], nm=mix_norm[l][None, :], n2=ffn2_norm[l][None, :])


def rope_tables(pos):
    inv_freq = 10000.0 ** (-jnp.arange(0, ATT_E, 2, dtype=F32) / ATT_E)
    ang = pos.astype(F32)[:, None] * inv_freq
    return jnp.tile(jnp.cos(ang), (1, 8)), jnp.tile(jnp.sin(ang), (1, 8))


def head_sum_matrix():
    return jnp.kron(jnp.eye(4, dtype=F32), jnp.ones((ATT_E, ATT_E), F32))


SMALL_NAMES = ("ffn1_norm", "mix_norm", "ffn2_norm", "pool_w", "pool_scale", "dn_a_log", "dn_dt_bias",
               "dn_out_norm", "final_norm", "dn_conv_w")


def _pack(parts):
    flat = jnp.concatenate([p.reshape(-1) for p in parts])
    n = flat.shape[0]
    rows = -(-n // 1024) * 8
    return jnp.pad(flat, (0, rows * 128 - n)).reshape(rows, 128)


def _unpack(packed, shapes):
    flat = packed.reshape(-1)
    out, off = [], 0
    for s in shapes:
        n = math.prod(s)
        out.append(flat[off:off + n].reshape(s))
        off += n
    return out


def kernel(x, positions, ffn1_norm, ffn1_w_gate, ffn1_w_up, ffn1_w_down, mix_norm, w_in, pool_w, pool_scale, dn_conv_w, dn_a_log, dn_dt_bias, dn_out_norm, w_out, ffn2_norm, ffn2_w_gate, ffn2_w_up, ffn2_w_down, final_norm, loss_target, m_ffn1_norm, m_ffn1_w_gate, m_ffn1_w_up, m_ffn1_w_down, m_mix_norm, m_w_in, m_pool_w, m_pool_scale, m_dn_conv_w, m_dn_a_log, m_dn_dt_bias, m_dn_out_norm, m_w_out, m_ffn2_norm, m_ffn2_w_gate, m_ffn2_w_up, m_ffn2_w_down, m_final_norm, v_ffn1_norm, v_ffn1_w_gate, v_ffn1_w_up, v_ffn1_w_down, v_mix_norm, v_w_in, v_pool_w, v_pool_scale, v_dn_conv_w, v_dn_a_log, v_dn_dt_bias, v_dn_out_norm, v_w_out, v_ffn2_norm, v_ffn2_w_gate, v_ffn2_w_up, v_ffn2_w_down, v_final_norm):
    me = 4 * lax.axis_index("x") + 2 * lax.axis_index("y") + lax.axis_index("c")
    x0 = x[0]
    target = loss_target[0]

    cos, sin = rope_tables(positions[0])
    headsum = head_sum_matrix()

    gathered = []
    for l in range(DEPTH):
        shards = [
            jnp.stack([ffn1_w_gate[l], ffn1_w_up[l]]).astype(BF16),
            ffn1_w_down[l].astype(BF16),
            jnp.stack([ffn2_w_gate[l], ffn2_w_up[l]]).astype(BF16),
            ffn2_w_down[l].astype(BF16),
            w_in[l].astype(BF16),
            w_out[l].astype(BF16),
            dn_conv_w[l],
        ]
        gathered.append(exchange(shards, False, f"gather_weights_{l}"))

    layers = []
    for l in range(DEPTH):
        f1gu, f1d, f2gu, f2d, win_g, wout_g, conv_g = gathered[l]
        layers.append(make_layer(
            l, f1gu, f1d, f2gu, f2d,
            win_g.transpose(1, 0, 2).reshape(D_MODEL, IN_W),
            wout_g.reshape(D_MODEL, D_MODEL),
            conv_g.transpose(1, 0, 2).reshape(DN_CONV, 3 * DN_W),
            pool_w, pool_scale, dn_out_norm, dn_a_log, dn_dt_bias, ffn1_norm, mix_norm, ffn2_norm))

    saved = []
    xa = x0
    for l in range(DEPTH):
        xa, s = layer_fwd(layers[l], xa, cos, sin, l)
        saved.append(s)

    loss_row, dx, d_final = loss_head(xa, final_norm[None, :], target, "loss_head")
    loss = lax.psum(loss_row[0, 0], ("x", "y", "c"))

    small = {}
    big_parts = [None] * DEPTH
    for l in reversed(range(DEPTH)):
        dx, big, small[l] = layer_bwd(layers[l], saved[l], dx, cos, sin, headsum, l)
        big_parts[l] = exchange(big, True, f"scatter_grads_{l}")
    grad_x = dx[None]

    small_shapes = {"ffn1_norm": (DEPTH, D_MODEL), "mix_norm": (DEPTH, D_MODEL), "ffn2_norm": (DEPTH, D_MODEL),
                    "pool_w": (DEPTH, 4, 64, 64), "pool_scale": (DEPTH, POOL_W), "dn_a_log": (DEPTH, DN_H),
                    "dn_dt_bias": (DEPTH, DN_H), "dn_out_norm": (DEPTH, DN_E), "final_norm": (D_MODEL,),
                    "dn_conv_w": (DEPTH, DN_CONV, 3 * DN_W)}
    g_small = {n: (d_final[0] if n == "final_norm" else jnp.stack([small[l][n] for l in range(DEPTH)]))
               for n in SMALL_NAMES}
    (small_parts,) = exchange([_pack([g_small[n] for n in SMALL_NAMES])], False, "gather_small_grads")

    def conv_full(a):
        return lax.dynamic_update_slice(jnp.zeros((DEPTH, DN_CONV, 3 * DN_W), F32), a, (0, 0, me * (3 * DN_W // N_DEV)))

    given = dict(ffn1_norm=(ffn1_norm, m_ffn1_norm, v_ffn1_norm), mix_norm=(mix_norm, m_mix_norm, v_mix_norm),
                 ffn2_norm=(ffn2_norm, m_ffn2_norm, v_ffn2_norm), pool_w=(pool_w, m_pool_w, v_pool_w),
                 pool_scale=(pool_scale, m_pool_scale, v_pool_scale), dn_a_log=(dn_a_log, m_dn_a_log, v_dn_a_log),
                 dn_dt_bias=(dn_dt_bias, m_dn_dt_bias, v_dn_dt_bias),
                 dn_out_norm=(dn_out_norm, m_dn_out_norm, v_dn_out_norm),
                 final_norm=(final_norm, m_final_norm, v_final_norm),
                 dn_conv_w=(conv_full(dn_conv_w), conv_full(m_dn_conv_w), conv_full(v_dn_conv_w)))
    packed_wmv = [_pack([given[n][k] for n in SMALL_NAMES]) for k in range(3)]
    small_out = adam_small(small_parts, *packed_wmv, "adam_small")
    shapes = [small_shapes[n] for n in SMALL_NAMES]
    small_res = {n: [] for n in SMALL_NAMES}
    for arr in small_out:
        for n, v_ in zip(SMALL_NAMES, _unpack(arr, shapes)):
            if n == "dn_conv_w":
                v_ = lax.dynamic_slice(v_, (0, 0, me * (3 * DN_W // N_DEV)), (DEPTH, DN_CONV, 3 * DN_W // N_DEV))
            small_res[n].append(v_)

    def parts_of(idx):
        return jnp.stack([big_parts[l][idx] for l in range(DEPTH)], axis=1)

    p_f1gu, p_f1d, p_f2gu, p_f2d, p_win, p_wout = (parts_of(i) for i in range(6))
    big_res = dict(
        ffn1_w_gate=adam_shard(p_f1gu, ffn1_w_gate, m_ffn1_w_gate, v_ffn1_w_gate, "adam_ffn1_gate", 0),
        ffn1_w_up=adam_shard(p_f1gu, ffn1_w_up, m_ffn1_w_up, v_ffn1_w_up, "adam_ffn1_up", 1),
        ffn1_w_down=adam_shard(p_f1d, ffn1_w_down, m_ffn1_w_down, v_ffn1_w_down, "adam_ffn1_down"),
        ffn2_w_gate=adam_shard(p_f2gu, ffn2_w_gate, m_ffn2_w_gate, v_ffn2_w_gate, "adam_ffn2_gate", 0),
        ffn2_w_up=adam_shard(p_f2gu, ffn2_w_up, m_ffn2_w_up, v_ffn2_w_up, "adam_ffn2_up", 1),
        ffn2_w_down=adam_shard(p_f2d, ffn2_w_down, m_ffn2_w_down, v_ffn2_w_down, "adam_ffn2_down"),
        w_in=adam_shard(p_win, w_in, m_w_in, v_w_in, "adam_w_in"),
        w_out=adam_shard(p_wout, w_out, m_w_out, v_w_out, "adam_w_out"),
    )

    order = ("ffn1_norm", "ffn1_w_gate", "ffn1_w_up", "ffn1_w_down", "mix_norm", "w_in", "pool_w", "pool_scale",
             "dn_conv_w", "dn_a_log", "dn_dt_bias", "dn_out_norm", "w_out", "ffn2_norm", "ffn2_w_gate", "ffn2_w_up",
             "ffn2_w_down", "final_norm")
    res = {**small_res, **big_res}
    outs = [loss, grad_x]
    for k in range(4):
        outs.extend(res[n][k] for n in order)
    return tuple(outs)
```

```python
import functools
import math

import jax
import jax.numpy as jnp
from jax import lax
from jax.experimental import pallas as pl
from jax.experimental.pallas import tpu as pltpu

F32 = jnp.float32
BF16 = jnp.bfloat16
HI = lax.Precision.HIGHEST
SDS = jax.ShapeDtypeStruct

N_DEV = 8
SEQ = 4096
D_MODEL = 1024
DEPTH = 2
D_FF = 2816
FF_BLK = D_FF // N_DEV
ATT_W = 256
ATT_E = 64
ATT_BLK = 128
DILATIONS = (1, 4, 16)
POOL_W = 256
POOL_HALO = 16
DN_W = 512
DN_H = 4
DN_E = 128
DN_C = 64
N_CHUNK = SEQ // DN_C
IN_W = 3080
IN_BLK = IN_W // N_DEV
EPS = 1e-6
EXT_ATT = 1280
EXT_REST = 3328
EXT_W = EXT_ATT + EXT_REST
R_DQKV, R_DZ, R_BB, R_AB, R_PU = 0, 1536, 2048, 2560, 3072

ADAM_LR, ADAM_B1, ADAM_B2, ADAM_EPS, ADAM_WD, ADAM_STEP = 0.001, 0.9, 0.999, 1e-08, 0.01, 10

VMEM_LIMIT = 60 * 1024 * 1024
MESH = pl.DeviceIdType.MESH


def _cparams(sem=None):
    kw = dict(vmem_limit_bytes=VMEM_LIMIT)
    if sem is not None:
        kw["dimension_semantics"] = sem
    return pltpu.CompilerParams(**kw)


def _dot(a, b, prec=None):
    return jnp.dot(a, b, preferred_element_type=F32, precision=prec)


def _dot_nt(a, b, prec=None):
    return lax.dot_general(a, b, (((1,), (1,)), ((), ())), preferred_element_type=F32, precision=prec)


def _dot_tn(a, b, prec=None):
    return lax.dot_general(a, b, (((0,), (0,)), ((), ())), preferred_element_type=F32, precision=prec)


def _sigmoid(x):
    return jax.nn.sigmoid(x)


def _rms_stats(x):
    r = lax.rsqrt(jnp.mean(x * x, axis=-1, keepdims=True) + EPS)
    return x * r, r


def _rms_bwd(xh, r, w, dh):
    dxh = dh * w
    dx = r * (dxh - xh * jnp.mean(dxh * xh, axis=-1, keepdims=True))
    return dx, jnp.sum(dh * xh, axis=0, keepdims=True)


FFN_T_FWD = 1024
FFN_T_BWD = 512


def ffn_fwd(x, nw, wgu, wd, name):
    t = FFN_T_FWD

    def body(x_ref, nw_ref, wgu_ref, wd_ref, o_ref, h_scr, acc_scr):
        k = pl.program_id(1)

        @pl.when(k == 0)
        def _():
            xh, _r = _rms_stats(x_ref[...])
            h_scr[...] = (xh * nw_ref[...]).astype(BF16)
            acc_scr[...] = jnp.zeros_like(acc_scr)

        h = h_scr[...]
        hg = _dot(h, wgu_ref[0, 0])
        hu = _dot(h, wgu_ref[0, 1])
        a = (hg * _sigmoid(hg) * hu).astype(BF16)
        acc_scr[...] += _dot(a, wd_ref[0])

        @pl.when(k == N_DEV - 1)
        def _():
            o_ref[...] = x_ref[...] + 0.5 * acc_scr[...]

    return pl.pallas_call(
        body, name=name, grid=(SEQ // t, N_DEV),
        in_specs=[pl.BlockSpec((t, D_MODEL), lambda i, k: (i, 0)),
                  pl.BlockSpec((1, D_MODEL), lambda i, k: (0, 0)),
                  pl.BlockSpec((1, 2, D_MODEL, FF_BLK), lambda i, k: (k, 0, 0, 0)),
                  pl.BlockSpec((1, FF_BLK, D_MODEL), lambda i, k: (k, 0, 0))],
        out_specs=pl.BlockSpec((t, D_MODEL), lambda i, k: (i, 0)),
        out_shape=SDS((SEQ, D_MODEL), F32),
        scratch_shapes=[pltpu.VMEM((t, D_MODEL), BF16), pltpu.VMEM((t, D_MODEL), F32)],
        compiler_params=_cparams(("arbitrary", "arbitrary")),
    )(x, nw, wgu, wd)


def ffn_bwd(x, dxo, nw, wgu, wd, name):
    t = FFN_T_BWD
    nt = SEQ // t

    def body(x_ref, dxo_ref, nw_ref, wgu_ref, wd_ref, dx_ref, dwgu_ref, dwd_ref, dnw_ref,
             dh_scr, ag_scr, au_scr, ad_scr):
        k = pl.program_id(0)
        i = pl.program_id(1)
        rows = pl.ds(pl.multiple_of(i * t, t), t)
        xh, r = _rms_stats(x_ref[...])
        nw_v = nw_ref[...]
        h = (xh * nw_v).astype(BF16)
        dxo_v = dxo_ref[...]
        dy = (0.5 * dxo_v).astype(BF16)
        wg = wgu_ref[0, 0]
        wu = wgu_ref[0, 1]
        hg = _dot(h, wg)
        hu = _dot(h, wu)
        sg = _sigmoid(hg)
        sil = hg * sg
        a = (sil * hu).astype(BF16)
        da = _dot_nt(dy, wd_ref[0])
        dhu = (da * sil).astype(BF16)
        dhg = (da * hu * (sg * (1.0 + hg * (1.0 - sg)))).astype(BF16)
        p_d = _dot_tn(a, dy)
        p_g = _dot_tn(h, dhg)
        p_u = _dot_tn(h, dhu)
        dh = _dot_nt(dhg, wg) + _dot_nt(dhu, wu)

        @pl.when(i == 0)
        def _():
            ad_scr[...] = p_d
            ag_scr[...] = p_g
            au_scr[...] = p_u

        @pl.when(i > 0)
        def _():
            ad_scr[...] += p_d
            ag_scr[...] += p_g
            au_scr[...] += p_u

        @pl.when(i == nt - 1)
        def _():
            dwd_ref[0] = ad_scr[...].astype(BF16)
            dwgu_ref[0, 0] = ag_scr[...].astype(BF16)
            dwgu_ref[0, 1] = au_scr[...].astype(BF16)

        @pl.when(k == 0)
        def _():
            dh_scr[rows, :] = dh

        @pl.when(k > 0)
        def _():
            dh_scr[rows, :] += dh

        @pl.when(jnp.logical_and(k == 0, i == 0))
        def _():
            dnw_ref[...] = jnp.zeros_like(dnw_ref)

        @pl.when(k == N_DEV - 1)
        def _():
            dx, dw = _rms_bwd(xh, r, nw_v, dh_scr[rows, :])
            dx_ref[...] = dxo_v + dx
            dnw_ref[...] += dw

    last = N_DEV - 1
    return pl.pallas_call(
        body, name=name, grid=(N_DEV, nt),
        in_specs=[pl.BlockSpec((t, D_MODEL), lambda k, i: (i, 0)),
                  pl.BlockSpec((t, D_MODEL), lambda k, i: (i, 0)),
                  pl.BlockSpec((1, D_MODEL), lambda k, i: (0, 0)),
                  pl.BlockSpec((1, 2, D_MODEL, FF_BLK), lambda k, i: (k, 0, 0, 0)),
                  pl.BlockSpec((1, FF_BLK, D_MODEL), lambda k, i: (k, 0, 0))],
        out_specs=[pl.BlockSpec((t, D_MODEL), lambda k, i: (jnp.where(k == last, i, 0), 0)),
                   pl.BlockSpec((1, 2, D_MODEL, FF_BLK), lambda k, i: (k, 0, 0, 0)),
                   pl.BlockSpec((1, FF_BLK, D_MODEL), lambda k, i: (k, 0, 0)),
                   pl.BlockSpec((1, D_MODEL), lambda k, i: (0, 0))],
        out_shape=[SDS((SEQ, D_MODEL), F32), SDS((N_DEV, 2, D_MODEL, FF_BLK), BF16),
                   SDS((N_DEV, FF_BLK, D_MODEL), BF16), SDS((1, D_MODEL), F32)],
        scratch_shapes=[pltpu.VMEM((SEQ, D_MODEL), F32), pltpu.VMEM((D_MODEL, FF_BLK), F32),
                        pltpu.VMEM((D_MODEL, FF_BLK), F32), pltpu.VMEM((FF_BLK, D_MODEL), F32)],
        compiler_params=_cparams(("arbitrary", "arbitrary")),
    )(x, dxo, nw, wgu, wd)


def loss_head(x, fw, target, name):
    t = 512

    def body(x_ref, fw_ref, tg_ref, loss_ref, dx_ref, dfw_ref):
        i = pl.program_id(0)
        xh, r = _rms_stats(x_ref[...])
        w = fw_ref[...]
        err = xh * w - tg_ref[...]
        part = 0.5 * jnp.sum(jnp.sum(err * err, axis=-1, keepdims=True), axis=0, keepdims=True) / D_MODEL
        dx, dw = _rms_bwd(xh, r, w, err * (1.0 / D_MODEL))
        dx_ref[...] = dx

        @pl.when(i == 0)
        def _():
            loss_ref[...] = jnp.zeros_like(loss_ref)
            dfw_ref[...] = jnp.zeros_like(dfw_ref)

        loss_ref[...] += jnp.broadcast_to(part, loss_ref.shape)
        dfw_ref[...] += dw

    return pl.pallas_call(
        body, name=name, grid=(SEQ // t,),
        in_specs=[pl.BlockSpec((t, D_MODEL), lambda i: (i, 0)),
                  pl.BlockSpec((1, D_MODEL), lambda i: (0, 0)),
                  pl.BlockSpec((t, D_MODEL), lambda i: (i, 0))],
        out_specs=[pl.BlockSpec((1, 128), lambda i: (0, 0)),
                   pl.BlockSpec((t, D_MODEL), lambda i: (i, 0)),
                   pl.BlockSpec((1, D_MODEL), lambda i: (0, 0))],
        out_shape=[SDS((1, 128), F32), SDS((SEQ, D_MODEL), F32), SDS((1, D_MODEL), F32)],
        compiler_params=_cparams(("arbitrary",)),
    )(x, fw, target)


MIX_T = 256


def mix_in_fwd(x, nw, wext, cos, sin, name):
    t = MIX_T

    def body(x_ref, nw_ref, w_ref, cos_ref, sin_ref, att_ref, rest_ref):
        xh, _r = _rms_stats(x_ref[...])
        h = (xh * nw_ref[...]).astype(BF16)
        pa = _dot(h, w_ref[:, 0:EXT_ATT])
        c = cos_ref[...]
        s = sin_ref[...]
        att_ref[:, 0:256] = pa[:, 0:256] * c + pa[:, 768:1024] * s
        att_ref[:, 256:512] = pa[:, 256:512] * c + pa[:, 1024:1280] * s
        att_ref[:, 512:768] = pa[:, 512:768]
        for j in range(EXT_REST // 256):
            rest_ref[:, 256 * j:256 * j + 256] = _dot(h, w_ref[:, EXT_ATT + 256 * j:EXT_ATT + 256 * j + 256])

    return pl.pallas_call(
        body, name=name, grid=(SEQ // t,),
        in_specs=[pl.BlockSpec((t, D_MODEL), lambda i: (i, 0)),
                  pl.BlockSpec((1, D_MODEL), lambda i: (0, 0)),
                  pl.BlockSpec((D_MODEL, EXT_W), lambda i: (0, 0)),
                  pl.BlockSpec((t, ATT_W), lambda i: (i, 0)),
                  pl.BlockSpec((t, ATT_W), lambda i: (i, 0))],
        out_specs=[pl.BlockSpec((t, 768), lambda i: (i, 0)),
                   pl.BlockSpec((t, EXT_REST), lambda i: (i, 0))],
        out_shape=[SDS((SEQ, 768), F32), SDS((SEQ, EXT_REST), F32)],
        compiler_params=_cparams(("arbitrary",)),
    )(x, nw, wext, cos, sin)


def assemble_dproj(datts, cos, sin, d_dqkv, dz, dbb, dab, dpu, name):
    t = 512

    def body(d1_ref, d4_ref, d16_ref, cos_ref, sin_ref, dqkv_ref, dz_ref, dbb_ref, dab_ref, dpu_ref, o_ref):
        da = d1_ref[...] + d4_ref[...] + d16_ref[...]
        c = cos_ref[...]
        s = sin_ref[...]
        dq = da[:, 0:256]
        dk = da[:, 256:512]
        o_ref[:, 0:256] = (dq * c).astype(BF16)
        o_ref[:, 256:512] = (dk * c).astype(BF16)
        o_ref[:, 512:768] = da[:, 512:768].astype(BF16)
        o_ref[:, 768:1024] = (dq * s).astype(BF16)
        o_ref[:, 1024:1280] = (dk * s).astype(BF16)
        b = EXT_ATT
        o_ref[:, b + R_DQKV:b + R_DQKV + 1536] = dqkv_ref[...].astype(BF16)
        o_ref[:, b + R_DZ:b + R_DZ + 512] = dz_ref[...].astype(BF16)
        o_ref[:, b + R_BB:b + R_BB + 512] = dbb_ref[...].astype(BF16)
        o_ref[:, b + R_AB:b + R_AB + 512] = dab_ref[...].astype(BF16)
        o_ref[:, b + R_PU:b + R_PU + 256] = dpu_ref[...].astype(BF16)

    row = lambda w: pl.BlockSpec((t, w), lambda i: (i, 0))
    return pl.pallas_call(
        body, name=name, grid=(SEQ // t,),
        in_specs=[row(768), row(768), row(768), row(256), row(256), row(1536), row(512), row(512), row(512), row(256)],
        out_specs=row(EXT_W),
        out_shape=SDS((SEQ, EXT_W), BF16),
        compiler_params=_cparams(("arbitrary",)),
    )(*datts, cos, sin, d_dqkv, dz, dbb, dab, dpu)


def linear_bwd(x, dxo, nw, dy, w, name):
    t = 512
    nb = 512
    n = w.shape[1]
    nt = SEQ // t
    nn = n // nb

    def body(x_ref, dxo_ref, nw_ref, dy_ref, w_ref, dx_ref, dw_ref, dnw_ref, dh_scr):
        k = pl.program_id(0)
        i = pl.program_id(1)
        rows = pl.ds(pl.multiple_of(i * t, t), t)
        xh, r = _rms_stats(x_ref[...])
        nw_v = nw_ref[...]
        h = (xh * nw_v).astype(BF16)
        dyv = dy_ref[...]
        p_w = _dot_tn(h, dyv)
        dh = _dot_nt(dyv, w_ref[...])

        @pl.when(i == 0)
        def _():
            dw_ref[...] = p_w

        @pl.when(i > 0)
        def _():
            dw_ref[...] += p_w

        @pl.when(k == 0)
        def _():
            dh_scr[rows, :] = dh

        @pl.when(k > 0)
        def _():
            dh_scr[rows, :] += dh

        @pl.when(jnp.logical_and(k == 0, i == 0))
        def _():
            dnw_ref[...] = jnp.zeros_like(dnw_ref)

        @pl.when(k == nn - 1)
        def _():
            dx, dw = _rms_bwd(xh, r, nw_v, dh_scr[rows, :])
            dx_ref[...] = dxo_ref[...] + dx
            dnw_ref[...] += dw

    last = nn - 1
    return pl.pallas_call(
        body, name=name, grid=(nn, nt),
        in_specs=[pl.BlockSpec((t, D_MODEL), lambda k, i: (i, 0)),
                  pl.BlockSpec((t, D_MODEL), lambda k, i: (i, 0)),
                  pl.BlockSpec((1, D_MODEL), lambda k, i: (0, 0)),
                  pl.BlockSpec((t, nb), lambda k, i: (i, k)),
                  pl.BlockSpec((D_MODEL, nb), lambda k, i: (0, k))],
        out_specs=[pl.BlockSpec((t, D_MODEL), lambda k, i: (jnp.where(k == last, i, 0), 0)),
                   pl.BlockSpec((D_MODEL, nb), lambda k, i: (0, k)),
                   pl.BlockSpec((1, D_MODEL), lambda k, i: (0, 0))],
        out_shape=[SDS((SEQ, D_MODEL), F32), SDS((D_MODEL, n), F32), SDS((1, D_MODEL), F32)],
        scratch_shapes=[pltpu.VMEM((SEQ, D_MODEL), F32)],
        compiler_params=_cparams(("arbitrary", "arbitrary")),
    )(x, dxo, nw, dy, w)


def _att_masks():
    qi = lax.broadcasted_iota(jnp.int32, (ATT_BLK, ATT_BLK), 0)
    ki = lax.broadcasted_iota(jnp.int32, (ATT_BLK, ATT_BLK), 1)
    return ki <= qi, ki >= qi


NEG = -1e30


def att_fwd(att, blocks_per_class, name):
    nblk = SEQ // ATT_BLK

    def body(cur_ref, prev_ref, o_ref):
        i = pl.program_id(0)
        has_prev = (i % blocks_per_class) != 0
        m_d, m_p = _att_masks()
        m_p = jnp.logical_and(m_p, has_prev)
        for hd in range(4):
            sl = slice(ATT_E * hd, ATT_E * hd + ATT_E)
            q = cur_ref[:, sl].astype(BF16)
            kc = cur_ref[:, 256 + ATT_E * hd:256 + ATT_E * hd + ATT_E].astype(BF16)
            vc = cur_ref[:, 512 + ATT_E * hd:512 + ATT_E * hd + ATT_E].astype(BF16)
            kp = prev_ref[:, 256 + ATT_E * hd:256 + ATT_E * hd + ATT_E].astype(BF16)
            vp = prev_ref[:, 512 + ATT_E * hd:512 + ATT_E * hd + ATT_E].astype(BF16)
            sd = jnp.where(m_d, _dot_nt(q, kc) * 0.125, NEG)
            sp = jnp.where(m_p, _dot_nt(q, kp) * 0.125, NEG)
            m = jnp.maximum(jnp.max(sd, axis=-1, keepdims=True), jnp.max(sp, axis=-1, keepdims=True))
            pd = jnp.exp(sd - m)
            pp = jnp.exp(sp - m)
            den = jnp.sum(pd, axis=-1, keepdims=True) + jnp.sum(pp, axis=-1, keepdims=True)
            inv = 1.0 / den
            o = _dot((pd * inv).astype(BF16), vc) + _dot((pp * inv).astype(BF16), vp)
            o_ref[:, sl] = o
            o_ref[:, 256 + ATT_E * hd:256 + ATT_E * hd + ATT_E] = jnp.broadcast_to(m + jnp.log(den), (ATT_BLK, ATT_E))

    return pl.pallas_call(
        body, name=name, grid=(nblk,),
        in_specs=[pl.BlockSpec((ATT_BLK, 768), lambda i: (i, 0)),
                  pl.BlockSpec((ATT_BLK, 768), lambda i: (jnp.maximum(i - 1, 0), 0))],
        out_specs=pl.BlockSpec((ATT_BLK, 512), lambda i: (i, 0)),
        out_shape=SDS((SEQ, 512), F32),
        compiler_params=_cparams(("arbitrary",)),
    )(att, att)


def att_bwd(att, ol, dol, blocks_per_class, name):
    nblk = SEQ // ATT_BLK

    def body(prev_ref, cur_ref, nxt_ref, ol_c_ref, ol_n_ref, dol_c_ref, dol_n_ref, d_ref):
        i = pl.program_id(0)
        has_prev = (i % blocks_per_class) != 0
        has_next = ((i + 1) % blocks_per_class) != 0
        m_d, m_band = _att_masks()
        m_p = jnp.logical_and(m_band, has_prev)
        m_n = jnp.logical_and(m_band, has_next)

        def pair(q, k, v, lse, do, dterm, mask):
            s = jnp.where(mask, _dot_nt(q, k) * 0.125, NEG)
            p = jnp.exp(s - lse)
            dp = _dot_nt(do, v)
            ds = (p * (dp + dterm) * 0.125).astype(BF16)
            return p.astype(BF16), ds

        for hd in range(4):
            a = ATT_E * hd
            q_c = cur_ref[:, a:a + ATT_E].astype(BF16)
            k_c = cur_ref[:, 256 + a:256 + a + ATT_E].astype(BF16)
            v_c = cur_ref[:, 512 + a:512 + a + ATT_E].astype(BF16)
            k_p = prev_ref[:, 256 + a:256 + a + ATT_E].astype(BF16)
            v_p = prev_ref[:, 512 + a:512 + a + ATT_E].astype(BF16)
            q_n = nxt_ref[:, a:a + ATT_E].astype(BF16)
            o_c = ol_c_ref[:, a:a + ATT_E]
            o_n = ol_n_ref[:, a:a + ATT_E]
            lse_c = ol_c_ref[:, 256 + a:256 + a + ATT_E][:, 0:1]
            lse_n = ol_n_ref[:, 256 + a:256 + a + ATT_E][:, 0:1]
            do_c = dol_c_ref[:, a:a + ATT_E]
            do_n = dol_n_ref[:, a:a + ATT_E]
            t_c = dol_c_ref[:, 256 + a:256 + a + ATT_E][:, 0:1] - jnp.sum(do_c * o_c, axis=-1, keepdims=True)
            t_n = dol_n_ref[:, 256 + a:256 + a + ATT_E][:, 0:1] - jnp.sum(do_n * o_n, axis=-1, keepdims=True)
            do_cb = do_c.astype(BF16)
            do_nb = do_n.astype(BF16)
            p1, ds1 = pair(q_c, k_c, v_c, lse_c, do_cb, t_c, m_d)
            _p2, ds2 = pair(q_c, k_p, v_p, lse_c, do_cb, t_c, m_p)
            p3, ds3 = pair(q_n, k_c, v_c, lse_n, do_nb, t_n, m_n)
            d_ref[:, a:a + ATT_E] = _dot(ds1, k_c) + _dot(ds2, k_p)
            d_ref[:, 256 + a:256 + a + ATT_E] = _dot_tn(ds1, q_c) + _dot_tn(ds3, q_n)
            d_ref[:, 512 + a:512 + a + ATT_E] = _dot_tn(p1, do_cb) + _dot_tn(p3, do_nb)

    prv = lambda i: (jnp.maximum(i - 1, 0), 0)
    cur = lambda i: (i, 0)
    nxt = lambda i: (jnp.minimum(i + 1, nblk - 1), 0)
    return pl.pallas_call(
        body, name=name, grid=(nblk,),
        in_specs=[pl.BlockSpec((ATT_BLK, 768), prv), pl.BlockSpec((ATT_BLK, 768), cur),
                  pl.BlockSpec((ATT_BLK, 768), nxt),
                  pl.BlockSpec((ATT_BLK, 512), cur), pl.BlockSpec((ATT_BLK, 512), nxt),
                  pl.BlockSpec((ATT_BLK, 512), cur), pl.BlockSpec((ATT_BLK, 512), nxt)],
        out_specs=pl.BlockSpec((ATT_BLK, 768), cur),
        out_shape=SDS((SEQ, 768), F32),
        compiler_params=_cparams(("arbitrary",)),
    )(att, att, att, ol, ol, dol, dol)


def to_classes(a, d):
    if d == 1:
        return a
    w = a.shape[1]
    return a.reshape(SEQ // d, d, w).transpose(1, 0, 2).reshape(SEQ, w)


def from_classes(a, d):
    if d == 1:
        return a
    w = a.shape[1]
    return a.reshape(d, SEQ // d, w).transpose(1, 0, 2).reshape(SEQ, w)


def _shift_down(x, k):
    rows = lax.broadcasted_iota(jnp.int32, x.shape, 0)
    return jnp.where(rows >= k, pltpu.roll(x, k, 0), 0.0)


def _shift_up(x, k):
    n = x.shape[0]
    rows = lax.broadcasted_iota(jnp.int32, x.shape, 0)
    return jnp.where(rows < n - k, pltpu.roll(x, n - k, 0), 0.0)


@functools.partial(jax.custom_vjp, nondiff_argnums=(1,))
def _delay(x, k):
    return _shift_down(x, k)


def _delay_fwd(x, k):
    return _shift_down(x, k), None


def _delay_bwd(k, _res, g):
    return (_shift_up(g, k),)


_delay.defvjp(_delay_fwd, _delay_bwd)

DN_CONV = 4


def _dn_prep_fn(u, w, kind):
    y = w[DN_CONV - 1:DN_CONV] * u
    for j in range(DN_CONV - 1):
        y = y + w[j:j + 1] * _delay(u, DN_CONV - 1 - j)
    y = y * _sigmoid(y)
    nrm = y * lax.rsqrt(jnp.sum(y * y, axis=-1, keepdims=True) + EPS)
    return jnp.where(kind == 0, nrm * (DN_E ** -0.5), jnp.where(kind == 1, nrm, y))


def dn_prep_fwd(rest, conv_w, name):
    def body(u_ref, w_ref, o_ref):
        j = pl.program_id(0)
        kind = (j >= DN_H).astype(jnp.int32) + (j >= 2 * DN_H).astype(jnp.int32)
        o_ref[...] = _dn_prep_fn(u_ref[...], w_ref[...], kind)

    return pl.pallas_call(
        body, name=name, grid=(3 * DN_H,),
        in_specs=[pl.BlockSpec((SEQ, DN_E), lambda j: (0, j)),
                  pl.BlockSpec((DN_CONV, DN_E), lambda j: (0, j))],
        out_specs=pl.BlockSpec((SEQ, DN_E), lambda j: (0, j)),
        out_shape=SDS((SEQ, 3 * DN_W), F32),
        compiler_params=_cparams(("arbitrary",)),
    )(rest, conv_w)


def dn_prep_bwd(rest, conv_w, dqkv, name):
    def body(u_ref, w_ref, g_ref, du_ref, dw_ref):
        j = pl.program_id(0)
        kind = (j >= DN_H).astype(jnp.int32) + (j >= 2 * DN_H).astype(jnp.int32)
        _y, vjp = jax.vjp(lambda u, w: _dn_prep_fn(u, w, kind), u_ref[...], w_ref[...])
        du, dw = vjp(g_ref[...])
        du_ref[...] = du
        dw_ref[...] = dw

    return pl.pallas_call(
        body, name=name, grid=(3 * DN_H,),
        in_specs=[pl.BlockSpec((SEQ, DN_E), lambda j: (0, j)),
                  pl.BlockSpec((DN_CONV, DN_E), lambda j: (0, j)),
                  pl.BlockSpec((SEQ, DN_E), lambda j: (0, j))],
        out_specs=[pl.BlockSpec((SEQ, DN_E), lambda j: (0, j)),
                   pl.BlockSpec((DN_CONV, DN_E), lambda j: (0, j))],
        out_shape=[SDS((SEQ, 3 * DN_W), F32), SDS((DN_CONV, 3 * DN_W), F32)],
        compiler_params=_cparams(("arbitrary",)),
    )(rest, conv_w, dqkv)


def _bdot(a, b, ca, cb, prec=None):
    return lax.dot_general(a, b, (((ca,), (cb,)), ((0,), (0,))), preferred_element_type=F32, precision=prec)


def _unit_lower_inverse(a):
    eye = (lax.broadcasted_iota(jnp.int32, (DN_C, DN_C), 0) == lax.broadcasted_iota(jnp.int32, (DN_C, DN_C), 1)).astype(F32)
    p = eye - a
    b = _bdot(a, a, 2, 1, HI)
    for lvl in range(5):
        p = p + _bdot(p, b, 2, 1, HI)
        if lvl < 4:
            b = _bdot(b, b, 2, 1, HI)
    return p


@jax.custom_vjp
def _tri_inv(a):
    return _unit_lower_inverse(a)


def _tri_inv_fwd(a):
    t = _unit_lower_inverse(a)
    return t, t


def _tri_inv_bwd(t, g):
    return (-_bdot(_bdot(t, g, 1, 1, HI), t, 2, 2, HI),)


_tri_inv.defvjp(_tri_inv_fwd, _tri_inv_bwd)


def _b16(x):
    return x.astype(BF16)


def _dn_chunk(q, k, v, bb, ab, alog, dtb, state):
    ri = lax.broadcasted_iota(jnp.int32, (DN_C, DN_C), 0)
    ci = lax.broadcasted_iota(jnp.int32, (DN_C, DN_C), 1)
    lower = ri >= ci
    strict = ri > ci
    nh = q.shape[0]
    beta = _sigmoid(bb)
    xg = ab + dtb
    softplus = jnp.maximum(xg, 0.0) + jnp.log(1.0 + jnp.exp(-jnp.abs(xg)))
    gi = -jnp.exp(alog) * softplus
    g = _bdot(jnp.broadcast_to(lower.astype(F32), (nh, DN_C, DN_C)), gi, 2, 1, HI)
    eg = jnp.exp(g)
    kb = k * beta
    vb = v * beta
    g_col = g[:, :, 0:DN_C]
    g_row = _bdot(jnp.full((nh, DN_C, DN_E), 1.0 / DN_E, F32), g, 2, 2, HI)
    decay = jnp.where(lower, jnp.exp(jnp.where(lower, g_col - g_row, 0.0)), 0.0)
    kbf = _b16(k)
    a = jnp.where(strict, _bdot(_b16(kb), kbf, 2, 2) * decay, 0.0)
    t = _tri_inv(a)
    tb = _b16(t)
    u = _bdot(tb, _b16(vb), 2, 1)
    w = _bdot(tb, _b16(kb * eg), 2, 1)
    intra = jnp.where(lower, _bdot(_b16(q), kbf, 2, 2) * decay, 0.0)
    sb = _b16(state)
    v_new = u - _bdot(_b16(w), sb, 2, 1)
    o = _bdot(_b16(q * eg), sb, 2, 1) + _bdot(_b16(intra), _b16(v_new), 2, 1)
    g_last = g[:, DN_C - 1:DN_C, :]
    k_dec = k * jnp.exp(g_last - g)
    new_state = state * jnp.exp(g_last) + _bdot(_b16(k_dec), _b16(v_new), 1, 1)
    return o, new_state


def _heads(ref, base=0):
    return jnp.stack([ref[:, base + DN_E * hd:base + DN_E * hd + DN_E] for hd in range(DN_H)], axis=0)


def _put_heads(ref, val, base=0):
    for hd in range(DN_H):
        ref[:, base + DN_E * hd:base + DN_E * hd + DN_E] = val[hd]


def _dn_args(qkv_ref, bb_ref, ab_ref, alog_ref, dtb_ref, state):
    return (_heads(qkv_ref), _heads(qkv_ref, DN_W), _heads(qkv_ref, 2 * DN_W), _heads(bb_ref), _heads(ab_ref),
            _heads(alog_ref), _heads(dtb_ref), state)


def dn_chunk_fwd(qkv, rest, alog_b, dtb_b, name):
    def body(qkv_ref, bb_ref, ab_ref, alog_ref, dtb_ref, o_ref, st_ref, state_scr):
        n = pl.program_id(0)

        @pl.when(n == 0)
        def _():
            state_scr[...] = jnp.zeros_like(state_scr)

        st = state_scr[...]
        st_ref[0] = st
        o, ns = _dn_chunk(*_dn_args(qkv_ref, bb_ref, ab_ref, alog_ref, dtb_ref, st))
        _put_heads(o_ref, o)
        state_scr[...] = ns

    return pl.pallas_call(
        body, name=name, grid=(N_CHUNK,),
        in_specs=[pl.BlockSpec((DN_C, 3 * DN_W), lambda n: (n, 0)),
                  pl.BlockSpec((DN_C, DN_W), lambda n: (n, R_BB // DN_W)),
                  pl.BlockSpec((DN_C, DN_W), lambda n: (n, R_AB // DN_W)),
                  pl.BlockSpec((1, DN_W), lambda n: (0, 0)),
                  pl.BlockSpec((1, DN_W), lambda n: (0, 0))],
        out_specs=[pl.BlockSpec((DN_C, DN_W), lambda n: (n, 0)),
                   pl.BlockSpec((1, DN_H, DN_E, DN_E), lambda n: (n, 0, 0, 0))],
        out_shape=[SDS((SEQ, DN_W), F32), SDS((N_CHUNK, DN_H, DN_E, DN_E), F32)],
        scratch_shapes=[pltpu.VMEM((DN_H, DN_E, DN_E), F32)],
        compiler_params=_cparams(("arbitrary",)),
    )(qkv, rest, rest, alog_b, dtb_b)


def dn_chunk_bwd(qkv, rest, alog_b, dtb_b, states, do, name):
    last = N_CHUNK - 1

    def body(qkv_ref, bb_ref, ab_ref, alog_ref, dtb_ref, st_ref, do_ref,
             dqkv_ref, dbb_ref, dab_ref, dalog_ref, ddtb_ref, dstate_scr):
        s = pl.program_id(0)

        @pl.when(s == 0)
        def _():
            dstate_scr[...] = jnp.zeros_like(dstate_scr)
            dalog_ref[...] = jnp.zeros_like(dalog_ref)
            ddtb_ref[...] = jnp.zeros_like(ddtb_ref)

        _out, vjp = jax.vjp(_dn_chunk, *_dn_args(qkv_ref, bb_ref, ab_ref, alog_ref, dtb_ref, st_ref[0]))
        dq, dk, dv, dbb, dab, dalog, ddtb, dst = vjp((_heads(do_ref), dstate_scr[...]))
        _put_heads(dqkv_ref, dq)
        _put_heads(dqkv_ref, dk, DN_W)
        _put_heads(dqkv_ref, dv, 2 * DN_W)
        _put_heads(dbb_ref, dbb)
        _put_heads(dab_ref, dab)
        _put_heads(dalog_ref, _heads(dalog_ref) + dalog)
        _put_heads(ddtb_ref, _heads(ddtb_ref) + ddtb)
        dstate_scr[...] = dst

    rev = lambda w: (lambda s: (last - s, w))
    return pl.pallas_call(
        body, name=name, grid=(N_CHUNK,),
        in_specs=[pl.BlockSpec((DN_C, 3 * DN_W), rev(0)),
                  pl.BlockSpec((DN_C, DN_W), rev(R_BB // DN_W)),
                  pl.BlockSpec((DN_C, DN_W), rev(R_AB // DN_W)),
                  pl.BlockSpec((1, DN_W), lambda s: (0, 0)),
                  pl.BlockSpec((1, DN_W), lambda s: (0, 0)),
                  pl.BlockSpec((1, DN_H, DN_E, DN_E), lambda s: (last - s, 0, 0, 0)),
                  pl.BlockSpec((DN_C, DN_W), rev(0))],
        out_specs=[pl.BlockSpec((DN_C, 3 * DN_W), rev(0)),
                   pl.BlockSpec((DN_C, DN_W), rev(0)),
                   pl.BlockSpec((DN_C, DN_W), rev(0)),
                   pl.BlockSpec((1, DN_W), lambda s: (0, 0)),
                   pl.BlockSpec((1, DN_W), lambda s: (0, 0))],
        out_shape=[SDS((SEQ, 3 * DN_W), F32), SDS((SEQ, DN_W), F32), SDS((SEQ, DN_W), F32),
                   SDS((1, DN_W), F32), SDS((1, DN_W), F32)],
        scratch_shapes=[pltpu.VMEM((DN_H, DN_E, DN_E), F32)],
        compiler_params=_cparams(("arbitrary",)),
    )(qkv, rest, rest, alog_b, dtb_b, states, do)


OUT_T = 256


def _pool_consts(rows_total, t0, halo_before):
    lane = lax.broadcasted_iota(jnp.int32, (rows_total, POOL_W), 1)
    row = lax.broadcasted_iota(jnp.int32, (rows_total, POOL_W), 0)
    grp = (lane >= 64).astype(jnp.int32) + (lane >= 128).astype(jnp.int32) + (lane >= 192).astype(jnp.int32)
    win = jnp.where(grp == 0, 2, jnp.where(grp == 1, 4, jnp.where(grp == 2, 8, 16)))
    pos = t0 + row - halo_before
    cnt = jnp.minimum(pos + 1, win).astype(F32)
    return grp, cnt


def _pool_select(grp, s2, s4, s8, s16):
    return jnp.where(grp == 0, s2, jnp.where(grp == 1, s4, jnp.where(grp == 2, s8, s16)))


def _pooled(u_ext, t0):
    n = u_ext.shape[0]
    grp, cnt = _pool_consts(n, t0, POOL_HALO)
    s2 = u_ext + pltpu.roll(u_ext, 1, 0)
    s4 = s2 + pltpu.roll(s2, 2, 0)
    s8 = s4 + pltpu.roll(s4, 4, 0)
    s16 = s8 + pltpu.roll(s8, 8, 0)
    out = _pool_select(grp, s2, s4, s8, s16) / jnp.maximum(cnt, 1.0) - u_ext
    return out[POOL_HALO:, :]


def _merge_weights(l1, l4, l16):
    m = jnp.maximum(jnp.maximum(l1, l4), l16)
    e1 = jnp.exp(l1 - m)
    e4 = jnp.exp(l4 - m)
    e16 = jnp.exp(l16 - m)
    inv = 1.0 / (e1 + e4 + e16)
    return e1 * inv, e4 * inv, e16 * inv


def _out_parts(ol1_ref, ol4_ref, ol16_ref, pu_ref, puh_ref, odn_ref, z_ref, wbd_ref, i, t):
    w1, w4, w16 = _merge_weights(ol1_ref[:, 256:512], ol4_ref[:, 256:512], ol16_ref[:, 256:512])
    ya = w1 * ol1_ref[:, 0:256] + w4 * ol4_ref[:, 0:256] + w16 * ol16_ref[:, 0:256]
    halo = jnp.where(i > 0, puh_ref[...], 0.0)
    pooled = _pooled(jnp.concatenate([halo, pu_ref[...]], axis=0), i * t)
    pw = _dot(pooled.astype(BF16), wbd_ref[...])
    return ya, pooled, pw, (w1, w4, w16)


def _out_specs_common(t):
    def row(w, cb=0):
        return pl.BlockSpec((t, w), lambda i: (i, cb))

    halo = pl.BlockSpec((POOL_HALO, POOL_W),
                        lambda i: (jnp.maximum(i * (t // POOL_HALO) - 1, 0), R_PU // POOL_W))
    full = lambda a, b: pl.BlockSpec((a, b), lambda i: (0, 0))
    return [row(512), row(512), row(512), row(POOL_W, R_PU // POOL_W), halo, row(DN_W), row(DN_W, R_DZ // DN_W),
            full(POOL_W, POOL_W), full(1, POOL_W), full(1, DN_W), full(D_MODEL, D_MODEL)]


def mix_out_fwd(x, ol1, ol4, ol16, rest, odn, wbd, scale, onorm_b, wout, name):
    t = OUT_T

    def body(x_ref, ol1_ref, ol4_ref, ol16_ref, pu_ref, puh_ref, odn_ref, z_ref, wbd_ref, sc_ref, on_ref, wo_ref, o_ref):
        i = pl.program_id(0)
        ya, _pooled_v, pw, _w = _out_parts(ol1_ref, ol4_ref, ol16_ref, pu_ref, puh_ref, odn_ref, z_ref, wbd_ref, i, t)
        yb = pw * sc_ref[...]
        acc = x_ref[...] + _dot(ya.astype(BF16), wo_ref[0:256, :]) + _dot(yb.astype(BF16), wo_ref[256:512, :])
        for hd in range(DN_H):
            sl = slice(DN_E * hd, DN_E * hd + DN_E)
            oh, _r = _rms_stats(odn_ref[:, sl])
            z = z_ref[:, sl]
            yc = oh * on_ref[:, sl] * (z * _sigmoid(z))
            acc = acc + _dot(yc.astype(BF16), wo_ref[512 + DN_E * hd:512 + DN_E * hd + DN_E, :])
        o_ref[...] = acc

    return pl.pallas_call(
        body, name=name, grid=(SEQ // t,),
        in_specs=[pl.BlockSpec((t, D_MODEL), lambda i: (i, 0))] + _out_specs_common(t),
        out_specs=pl.BlockSpec((t, D_MODEL), lambda i: (i, 0)),
        out_shape=SDS((SEQ, D_MODEL), F32),
        compiler_params=_cparams(("arbitrary",)),
    )(x, ol1, ol4, ol16, rest, rest, odn, rest, wbd, scale, onorm_b, wout)


def mix_out_bwd(dxo, ol1, ol4, ol16, rest, odn, wbd, scale, onorm_b, wout, headsum, name):
    t = OUT_T

    def body(dxo_ref, ol1_ref, ol4_ref, ol16_ref, pu_ref, puh_ref, odn_ref, z_ref, wbd_ref, sc_ref, on_ref, wo_ref, hs_ref,
             dwo_ref, d1_ref, d4_ref, d16_ref, dpl_ref, dodn_ref, dz_ref, dsc_ref, don_ref, dwbd_ref):
        i = pl.program_id(0)

        @pl.when(i == 0)
        def _():
            dwo_ref[...] = jnp.zeros_like(dwo_ref)
            dsc_ref[...] = jnp.zeros_like(dsc_ref)
            don_ref[...] = jnp.zeros_like(don_ref)
            dwbd_ref[...] = jnp.zeros_like(dwbd_ref)

        ya, pooled, pw, (w1, w4, w16) = _out_parts(ol1_ref, ol4_ref, ol16_ref, pu_ref, puh_ref, odn_ref, z_ref, wbd_ref, i, t)
        sc = sc_ref[...]
        dxb = dxo_ref[...].astype(BF16)
        dwo_ref[0:256, :] += _dot_tn(ya.astype(BF16), dxb)
        dwo_ref[256:512, :] += _dot_tn((pw * sc).astype(BF16), dxb)
        dya = _dot_nt(dxb, wo_ref[0:256, :])
        o1 = ol1_ref[:, 0:256]
        o4 = ol4_ref[:, 0:256]
        o16 = ol16_ref[:, 0:256]
        hs = hs_ref[...]
        s1 = _dot(dya * o1, hs, HI)
        s4 = _dot(dya * o4, hs, HI)
        s16 = _dot(dya * o16, hs, HI)
        sbar = w1 * s1 + w4 * s4 + w16 * s16
        d1_ref[:, 0:256] = w1 * dya
        d1_ref[:, 256:512] = w1 * (s1 - sbar)
        d4_ref[:, 0:256] = w4 * dya
        d4_ref[:, 256:512] = w4 * (s4 - sbar)
        d16_ref[:, 0:256] = w16 * dya
        d16_ref[:, 256:512] = w16 * (s16 - sbar)
        dyb = _dot_nt(dxb, wo_ref[256:512, :])
        dsc_ref[...] += jnp.sum(dyb * pw, axis=0, keepdims=True)
        dpw = (dyb * sc).astype(BF16)
        dwbd_ref[...] += _dot_tn(pooled.astype(BF16), dpw)
        dpl_ref[...] = _dot_nt(dpw, wbd_ref[...])
        for hd in range(DN_H):
            sl = slice(DN_E * hd, DN_E * hd + DN_E)
            rows_w = slice(512 + DN_E * hd, 512 + DN_E * hd + DN_E)
            oh, r = _rms_stats(odn_ref[:, sl])
            z = z_ref[:, sl]
            sg = _sigmoid(z)
            sz = z * sg
            nw = on_ref[:, sl]
            on = oh * nw
            dwo_ref[rows_w, :] += _dot_tn((on * sz).astype(BF16), dxb)
            dyc = _dot_nt(dxb, wo_ref[rows_w, :])
            dz_ref[:, sl] = dyc * on * (sg * (1.0 + z * (1.0 - sg)))
            dx, dw = _rms_bwd(oh, r, nw, dyc * sz)
            dodn_ref[:, sl] = dx
            don_ref[:, sl] += dw

    row = lambda w: pl.BlockSpec((t, w), lambda i: (i, 0))
    full = lambda a, b: pl.BlockSpec((a, b), lambda i: (0, 0))
    return pl.pallas_call(
        body, name=name, grid=(SEQ // t,),
        in_specs=[row(D_MODEL)] + _out_specs_common(t) + [full(ATT_W, ATT_W)],
        out_specs=[full(D_MODEL, D_MODEL), row(512), row(512), row(512), row(POOL_W), row(DN_W), row(DN_W),
                   full(1, POOL_W), full(1, DN_W), full(POOL_W, POOL_W)],
        out_shape=[SDS((D_MODEL, D_MODEL), F32), SDS((SEQ, 512), F32), SDS((SEQ, 512), F32), SDS((SEQ, 512), F32),
                   SDS((SEQ, POOL_W), F32), SDS((SEQ, DN_W), F32), SDS((SEQ, DN_W), F32),
                   SDS((1, POOL_W), F32), SDS((1, DN_W), F32), SDS((POOL_W, POOL_W), F32)],
        compiler_params=_cparams(("arbitrary",)),
    )(dxo, ol1, ol4, ol16, rest, rest, odn, rest, wbd, scale, onorm_b, wout, headsum)


def pool_bwd(dpooled, name):
    t = 512
    nt = SEQ // t

    def body(d_ref, dn_ref, o_ref):
        i = pl.program_id(0)
        halo = jnp.where(i < nt - 1, dn_ref[...], 0.0)
        d_ext = jnp.concatenate([d_ref[...], halo], axis=0)
        n = t + POOL_HALO
        grp, cnt = _pool_consts(n, i * t, 0)
        dq = d_ext / cnt
        s2 = dq + pltpu.roll(dq, n - 1, 0)
        s4 = s2 + pltpu.roll(s2, n - 2, 0)
        s8 = s4 + pltpu.roll(s4, n - 4, 0)
        s16 = s8 + pltpu.roll(s8, n - 8, 0)
        o_ref[...] = (_pool_select(grp, s2, s4, s8, s16) - d_ext)[0:t, :]

    return pl.pallas_call(
        body, name=name, grid=(nt,),
        in_specs=[pl.BlockSpec((t, POOL_W), lambda i: (i, 0)),
                  pl.BlockSpec((POOL_HALO, POOL_W),
                               lambda i: (jnp.minimum((i + 1) * (t // POOL_HALO), SEQ // POOL_HALO - 1), 0))],
        out_specs=pl.BlockSpec((t, POOL_W), lambda i: (i, 0)),
        out_shape=SDS((SEQ, POOL_W), F32),
        compiler_params=_cparams(("arbitrary",)),
    )(dpooled, dpooled)


def _peers():
    x, y, c = lax.axis_index("x"), lax.axis_index("y"), lax.axis_index("c")
    out = []
    for fx, fy, fc in ((0, 0, 1), (1, 0, 0), (0, 1, 0), (1, 1, 0), (1, 0, 1), (0, 1, 1), (1, 1, 1)):
        px, py, pc = x ^ fx, y ^ fy, c ^ fc
        out.append(((px, py, pc), 4 * px + 2 * py + pc))
    return 4 * x + 2 * y + c, out


def exchange(arrays, scatter, name):
    n = len(arrays)
    n_peer = N_DEV - 1

    def body(*refs):
        ins, outs = refs[:n], refs[n:2 * n]
        send_sems, recv_sems, local_sems = refs[2 * n:]
        me, peers = _peers()
        sends, recvs, locals_ = [], [], []
        for a in range(n):
            src_me = ins[a].at[me] if scatter else ins[a]
            lc = pltpu.make_async_copy(src_me, outs[a].at[me], local_sems.at[a])
            lc.start()
            locals_.append(lc)
            for j, (peer, pidx) in enumerate(peers):
                src = ins[a].at[pidx] if scatter else ins[a]
                cp = pltpu.make_async_remote_copy(
                    src_ref=src, dst_ref=outs[a].at[me],
                    send_sem=send_sems.at[a * n_peer + j], recv_sem=recv_sems.at[a * n_peer + j],
                    device_id=peer, device_id_type=MESH)
                cp.start()
                sends.append(cp)
                recvs.append(pltpu.make_async_remote_copy(
                    src_ref=src, dst_ref=outs[a].at[pidx],
                    send_sem=send_sems.at[a * n_peer + j], recv_sem=recv_sems.at[a * n_peer + j],
                    device_id=peer, device_id_type=MESH))
        for cp in recvs:
            cp.wait_recv()
        for cp in sends:
            cp.wait_send()
        for lc in locals_:
            lc.wait()

    any_spec = pl.BlockSpec(memory_space=pl.ANY)
    if scatter:
        out_shape = [SDS(a.shape, a.dtype) for a in arrays]
    else:
        out_shape = [SDS((N_DEV,) + a.shape, a.dtype) for a in arrays]
    return pl.pallas_call(
        body, name=name,
        in_specs=[any_spec] * n, out_specs=[any_spec] * n, out_shape=out_shape,
        scratch_shapes=[pltpu.SemaphoreType.DMA((n * n_peer,)), pltpu.SemaphoreType.DMA((n * n_peer,)),
                        pltpu.SemaphoreType.DMA((n,))],
    )(*arrays)


def _adam_math(w, g, m, v):
    m2 = ADAM_B1 * m + (1.0 - ADAM_B1) * g
    v2 = ADAM_B2 * v + (1.0 - ADAM_B2) * (g * g)
    m_hat = m2 / (1.0 - ADAM_B1 ** ADAM_STEP)
    v_hat = v2 / (1.0 - ADAM_B2 ** ADAM_STEP)
    delta = -ADAM_LR * (m_hat / (jnp.sqrt(v_hat) + ADAM_EPS) + ADAM_WD * w)
    return delta, m2, v2


def adam_shard(parts, w, m, v, name, part_slice=None):
    _, r, c = w.shape
    sub = part_slice

    def body(p_ref, w_ref, m_ref, v_ref, g_ref, d_ref, m2_ref, v2_ref):
        g = p_ref[0, 0].astype(F32)
        for i in range(1, N_DEV):
            g = g + p_ref[i, 0].astype(F32)
        delta, m2, v2 = _adam_math(w_ref[0], g, m_ref[0], v_ref[0])
        g_ref[0] = g
        d_ref[0] = delta
        m2_ref[0] = m2
        v2_ref[0] = v2

    if sub is None:
        p_spec = pl.BlockSpec((N_DEV, 1, r, c), lambda l: (0, l, 0, 0))
    else:
        p_spec = pl.BlockSpec((N_DEV, 1, None, r, c), lambda l: (0, l, sub, 0, 0))
    blk = pl.BlockSpec((1, r, c), lambda l: (l, 0, 0))
    return pl.pallas_call(
        body, name=name, grid=(DEPTH,),
        in_specs=[p_spec, blk, blk, blk], out_specs=[blk] * 4,
        out_shape=[SDS(w.shape, F32)] * 4,
        compiler_params=_cparams(("arbitrary",)),
    )(parts, w, m, v)


def adam_small(parts, w, m, v, name):
    def body(p_ref, w_ref, m_ref, v_ref, g_ref, d_ref, m2_ref, v2_ref):
        g = p_ref[0]
        for i in range(1, N_DEV):
            g = g + p_ref[i]
        delta, m2, v2 = _adam_math(w_ref[...], g, m_ref[...], v_ref[...])
        g_ref[...] = g
        d_ref[...] = delta
        m2_ref[...] = m2
        v2_ref[...] = v2

    return pl.pallas_call(
        body, name=name, out_shape=[SDS(w.shape, F32)] * 4, compiler_params=_cparams(),
    )(parts, w, m, v)


def _rot_cols(w):
    w4 = w.reshape(w.shape[0], 4, 2, 32)
    return jnp.stack([-w4[:, :, 1], w4[:, :, 0]], axis=2).reshape(w.shape[0], ATT_W)


def _rot_cols_t(dw_rot):
    d4 = dw_rot.reshape(dw_rot.shape[0], 4, 2, 32)
    return jnp.stack([d4[:, :, 1], -d4[:, :, 0]], axis=2).reshape(dw_rot.shape[0], ATT_W)


def build_wext(w_in):
    aq, ak, av, pu = w_in[:, 0:256], w_in[:, 256:512], w_in[:, 512:768], w_in[:, 768:1024]
    dqkvz = w_in[:, 1024:3072]
    gates = jnp.repeat(w_in[:, 3072:3080], DN_E, axis=1)
    return jnp.concatenate([aq, ak, av, _rot_cols(aq), _rot_cols(ak), dqkvz, gates, pu], axis=1)


def fold_dwext(d):
    b = EXT_ATT
    aq = d[:, 0:256] + _rot_cols_t(d[:, 768:1024])
    ak = d[:, 256:512] + _rot_cols_t(d[:, 1024:1280])
    av = d[:, 512:768]
    dqkvz = d[:, b:b + 2048]
    gates = d[:, b + R_BB:b + R_BB + 1024].reshape(d.shape[0], 8, DN_E).sum(axis=-1)
    pu = d[:, b + R_PU:b + R_PU + 256]
    return jnp.concatenate([aq, ak, av, pu, dqkvz, gates], axis=1)


def _block_diag(pw):
    z = jnp.zeros((4, 64, 4, 64), pw.dtype)
    for g in range(4):
        z = z.at[g, :, g, :].set(pw[g])
    return z.reshape(POOL_W, POOL_W)


def _diag_blocks(m):
    m4 = m.reshape(4, 64, 4, 64)
    return jnp.stack([m4[g, :, g, :] for g in range(4)], axis=0)


def _lanes(v, reps):
    return jnp.repeat(v, reps)[None, :]


def layer_fwd(p, xa, cos, sin, l):
    xb = ffn_fwd(xa, p["n1"], p["f1gu"], p["f1d"], f"ffn1_fwd_{l}")
    att, rest = mix_in_fwd(xb, p["nm"], p["wext"], cos, sin, f"mix_in_fwd_{l}")
    ols = [from_classes(att_fwd(to_classes(att, d), SEQ // d // ATT_BLK, f"att_fwd_{l}_{d}"), d) for d in DILATIONS]
    qkv = dn_prep_fwd(rest, p["conv"], f"dn_prep_fwd_{l}")
    odn, states = dn_chunk_fwd(qkv, rest, p["alog"], p["dtb"], f"dn_chunk_fwd_{l}")
    xc = mix_out_fwd(xb, ols[0], ols[1], ols[2], rest, odn, p["wbd"], p["scale"], p["onorm"], p["wout"], f"mix_out_fwd_{l}")
    xd = ffn_fwd(xc, p["n2"], p["f2gu"], p["f2d"], f"ffn2_fwd_{l}")
    return xd, dict(xa=xa, xb=xb, xc=xc, att=att, rest=rest, ols=ols, qkv=qkv, odn=odn, states=states)


def layer_bwd(p, s, dx, cos, sin, headsum, l):
    dx, d_f2gu, d_f2d, d_n2 = ffn_bwd(s["xc"], dx, p["n2"], p["f2gu"], p["f2d"], f"ffn2_bwd_{l}")
    (d_wout, dol1, dol4, dol16, dpooled, dodn, dz, dscale, donorm, dwbd) = mix_out_bwd(
        dx, s["ols"][0], s["ols"][1], s["ols"][2], s["rest"], s["odn"], p["wbd"], p["scale"], p["onorm"], p["wout"],
        headsum, f"mix_out_bwd_{l}")
    dpu = pool_bwd(dpooled, f"pool_bwd_{l}")
    dqkv, dbb, dab, dalog, ddtb = dn_chunk_bwd(s["qkv"], s["rest"], p["alog"], p["dtb"], s["states"], dodn, f"dn_chunk_bwd_{l}")
    d_dqkv, dconv = dn_prep_bwd(s["rest"], p["conv"], dqkv, f"dn_prep_bwd_{l}")
    datts = []
    for d, ol, dol in zip(DILATIONS, s["ols"], (dol1, dol4, dol16)):
        da = att_bwd(to_classes(s["att"], d), to_classes(ol, d), to_classes(dol, d), SEQ // d // ATT_BLK, f"att_bwd_{l}_{d}")
        datts.append(from_classes(da, d))
    dproj = assemble_dproj(datts, cos, sin, d_dqkv, dz, dbb, dab, dpu, f"assemble_dproj_{l}")
    dx, d_wext, d_nm = linear_bwd(s["xb"], dx, p["nm"], dproj, p["wext"], f"mix_in_bwd_{l}")
    dx, d_f1gu, d_f1d, d_n1 = ffn_bwd(s["xa"], dx, p["n1"], p["f1gu"], p["f1d"], f"ffn1_bwd_{l}")

    d_win = fold_dwext(d_wext).reshape(D_MODEL, N_DEV, IN_BLK).transpose(1, 0, 2).astype(BF16)
    big = [d_f1gu, d_f1d, d_f2gu, d_f2d, d_win, d_wout.reshape(N_DEV, D_MODEL // N_DEV, D_MODEL).astype(BF16)]
    small = dict(ffn1_norm=d_n1[0], mix_norm=d_nm[0], ffn2_norm=d_n2[0], pool_w=_diag_blocks(dwbd),
                 pool_scale=dscale[0], dn_a_log=dalog.reshape(DN_H, DN_E).sum(-1),
                 dn_dt_bias=ddtb.reshape(DN_H, DN_E).sum(-1),
                 dn_out_norm=donorm.reshape(DN_H, DN_E).sum(0), dn_conv_w=dconv)
    return dx, big, small


def make_layer(l, f1gu, f1d, f2gu, f2d, win_full, wout_full, conv_full, pool_w, pool_scale, dn_out_norm, dn_a_log,
               dn_dt_bias, ffn1_norm, mix_norm, ffn2_norm):
    return dict(
        f1gu=f1gu, f1d=f1d, f2gu=f2gu, f2d=f2d,
        wext=build_wext(win_full), wout=wout_full, conv=conv_full,
        wbd=_block_diag(pool_w[l]).astype(BF16),
        scale=pool_scale[l][None, :],
        onorm=jnp.tile(dn_out_norm[l], DN_H)[None, :],
        alog=_lanes(dn_a_log[l], DN_E),
        dtb=_lanes(dn_dt_bias[l], DN_E),
        n1=ffn1_norm[l][None, :], nm=mix_norm[l][None, :], n2=ffn2_norm[l][None, :])


def rope_tables(pos):
    inv_freq = 10000.0 ** (-jnp.arange(0, ATT_E, 2, dtype=F32) / ATT_E)
    ang = pos.astype(F32)[:, None] * inv_freq
    return jnp.tile(jnp.cos(ang), (1, 8)), jnp.tile(jnp.sin(ang), (1, 8))


def head_sum_matrix():
    return jnp.kron(jnp.eye(4, dtype=F32), jnp.ones((ATT_E, ATT_E), F32))


SMALL_NAMES = ("ffn1_norm", "mix_norm", "ffn2_norm", "pool_w", "pool_scale", "dn_a_log", "dn_dt_bias",
               "dn_out_norm", "final_norm", "dn_conv_w")


def _pack(parts):
    flat = jnp.concatenate([p.reshape(-1) for p in parts])
    n = flat.shape[0]
    rows = -(-n // 1024) * 8
    return jnp.pad(flat, (0, rows * 128 - n)).reshape(rows, 128)


def _unpack(packed, shapes):
    flat = packed.reshape(-1)
    out, off = [], 0
    for s in shapes:
        n = math.prod(s)
        out.append(flat[off:off + n].reshape(s))
        off += n
    return out


def kernel(x, positions, ffn1_norm, ffn1_w_gate, ffn1_w_up, ffn1_w_down, mix_norm, w_in, pool_w, pool_scale, dn_conv_w, dn_a_log, dn_dt_bias, dn_out_norm, w_out, ffn2_norm, ffn2_w_gate, ffn2_w_up, ffn2_w_down, final_norm, loss_target, m_ffn1_norm, m_ffn1_w_gate, m_ffn1_w_up, m_ffn1_w_down, m_mix_norm, m_w_in, m_pool_w, m_pool_scale, m_dn_conv_w, m_dn_a_log, m_dn_dt_bias, m_dn_out_norm, m_w_out, m_ffn2_norm, m_ffn2_w_gate, m_ffn2_w_up, m_ffn2_w_down, m_final_norm, v_ffn1_norm, v_ffn1_w_gate, v_ffn1_w_up, v_ffn1_w_down, v_mix_norm, v_w_in, v_pool_w, v_pool_scale, v_dn_conv_w, v_dn_a_log, v_dn_dt_bias, v_dn_out_norm, v_w_out, v_ffn2_norm, v_ffn2_w_gate, v_ffn2_w_up, v_ffn2_w_down, v_final_norm):
    me = 4 * lax.axis_index("x") + 2 * lax.axis_index("y") + lax.axis_index("c")
    x0 = x[0]
    target = loss_target[0]

    cos, sin = rope_tables(positions[0])
    headsum = head_sum_matrix()

    gathered = []
    for l in range(DEPTH):
        shards = [
            jnp.stack([ffn1_w_gate[l], ffn1_w_up[l]]).astype(BF16),
            ffn1_w_down[l].astype(BF16),
            jnp.stack([ffn2_w_gate[l], ffn2_w_up[l]]).astype(BF16),
            ffn2_w_down[l].astype(BF16),
            w_in[l].astype(BF16),
            w_out[l].astype(BF16),
            dn_conv_w[l],
        ]
        gathered.append(exchange(shards, False, f"gather_weights_{l}"))

    layers = []
    for l in range(DEPTH):
        f1gu, f1d, f2gu, f2d, win_g, wout_g, conv_g = gathered[l]
        layers.append(make_layer(
            l, f1gu, f1d, f2gu, f2d,
            win_g.transpose(1, 0, 2).reshape(D_MODEL, IN_W),
            wout_g.reshape(D_MODEL, D_MODEL),
            conv_g.transpose(1, 0, 2).reshape(DN_CONV, 3 * DN_W),
            pool_w, pool_scale, dn_out_norm, dn_a_log, dn_dt_bias, ffn1_norm, mix_norm, ffn2_norm))

    saved = []
    xa = x0
    for l in range(DEPTH):
        xa, s = layer_fwd(layers[l], xa, cos, sin, l)
        saved.append(s)

    loss_row, dx, d_final = loss_head(xa, final_norm[None, :], target, "loss_head")
    loss = lax.psum(loss_row[0, 0], ("x", "y", "c"))

    small = {}
    big_parts = [None] * DEPTH
    for l in reversed(range(DEPTH)):
        dx, big, small[l] = layer_bwd(layers[l], saved[l], dx, cos, sin, headsum, l)
        big_parts[l] = exchange(big, True, f"scatter_grads_{l}")
    grad_x = dx[None]

    small_shapes = {"ffn1_norm": (DEPTH, D_MODEL), "mix_norm": (DEPTH, D_MODEL), "ffn2_norm": (DEPTH, D_MODEL),
                    "pool_w": (DEPTH, 4, 64, 64), "pool_scale": (DEPTH, POOL_W), "dn_a_log": (DEPTH, DN_H),
                    "dn_dt_bias": (DEPTH, DN_H), "dn_out_norm": (DEPTH, DN_E), "final_norm": (D_MODEL,),
                    "dn_conv_w": (DEPTH, DN_CONV, 3 * DN_W)}
    g_small = {n: (d_final[0] if n == "final_norm" else jnp.stack([small[l][n] for l in range(DEPTH)]))
               for n in SMALL_NAMES}
    (small_parts,) = exchange([_pack([g_small[n] for n in SMALL_NAMES])], False, "gather_small_grads")

    def conv_full(a):
        return lax.dynamic_update_slice(jnp.zeros((DEPTH, DN_CONV, 3 * DN_W), F32), a, (0, 0, me * (3 * DN_W // N_DEV)))

    given = dict(ffn1_norm=(ffn1_norm, m_ffn1_norm, v_ffn1_norm), mix_norm=(mix_norm, m_mix_norm, v_mix_norm),
                 ffn2_norm=(ffn2_norm, m_ffn2_norm, v_ffn2_norm), pool_w=(pool_w, m_pool_w, v_pool_w),
                 pool_scale=(pool_scale, m_pool_scale, v_pool_scale), dn_a_log=(dn_a_log, m_dn_a_log, v_dn_a_log),
                 dn_dt_bias=(dn_dt_bias, m_dn_dt_bias, v_dn_dt_bias),
                 dn_out_norm=(dn_out_norm, m_dn_out_norm, v_dn_out_norm),
                 final_norm=(final_norm, m_final_norm, v_final_norm),
                 dn_conv_w=(conv_full(dn_conv_w), conv_full(m_dn_conv_w), conv_full(v_dn_conv_w)))
    packed_wmv = [_pack([given[n][k] for n in SMALL_NAMES]) for k in range(3)]
    small_out = adam_small(small_parts, *packed_wmv, "adam_small")
    shapes = [small_shapes[n] for n in SMALL_NAMES]
    small_res = {n: [] for n in SMALL_NAMES}
    for arr in small_out:
        for n, v_ in zip(SMALL_NAMES, _unpack(arr, shapes)):
            if n == "dn_conv_w":
                v_ = lax.dynamic_slice(v_, (0, 0, me * (3 * DN_W // N_DEV)), (DEPTH, DN_CONV, 3 * DN_W // N_DEV))
            small_res[n].append(v_)

    def parts_of(idx):
        return jnp.stack([big_parts[l][idx] for l in range(DEPTH)], axis=1)

    p_f1gu, p_f1d, p_f2gu, p_f2d, p_win, p_wout = (parts_of(i) for i in range(6))
    big_res = dict(
        ffn1_w_gate=adam_shard(p_f1gu, ffn1_w_gate, m_ffn1_w_gate, v_ffn1_w_gate, "adam_ffn1_gate", 0),
        ffn1_w_up=adam_shard(p_f1gu, ffn1_w_up, m_ffn1_w_up, v_ffn1_w_up, "adam_ffn1_up", 1),
        ffn1_w_down=adam_shard(p_f1d, ffn1_w_down, m_ffn1_w_down, v_ffn1_w_down, "adam_ffn1_down"),
        ffn2_w_gate=adam_shard(p_f2gu, ffn2_w_gate, m_ffn2_w_gate, v_ffn2_w_gate, "adam_ffn2_gate", 0),
        ffn2_w_up=adam_shard(p_f2gu, ffn2_w_up, m_ffn2_w_up, v_ffn2_w_up, "adam_ffn2_up", 1),
        ffn2_w_down=adam_shard(p_f2d, ffn2_w_down, m_ffn2_w_down, v_ffn2_w_down, "adam_ffn2_down"),
        w_in=adam_shard(p_win, w_in, m_w_in, v_w_in, "adam_w_in"),
        w_out=adam_shard(p_wout, w_out, m_w_out, v_w_out, "adam_w_out"),
    )

    order = ("ffn1_norm", "ffn1_w_gate", "ffn1_w_up", "ffn1_w_down", "mix_norm", "w_in", "pool_w", "pool_scale",
             "dn_conv_w", "dn_a_log", "dn_dt_bias", "dn_out_norm", "w_out", "ffn2_norm", "ffn2_w_gate", "ffn2_w_up",
             "ffn2_w_down", "final_norm")
    res = {**small_res, **big_res}
    outs = [loss, grad_x]
    for k in range(4):
        outs.extend(res[n][k] for n in order)
    return tuple(outs)
```

```python
import functools
import math

import jax
import jax.numpy as jnp
from jax import lax
from jax.experimental import pallas as pl
from jax.experimental.pallas import tpu as pltpu

F32 = jnp.float32
BF16 = jnp.bfloat16
HI = lax.Precision.HIGHEST
SDS = jax.ShapeDtypeStruct

N_DEV = 8
SEQ = 4096
D_MODEL = 1024
DEPTH = 2
D_FF = 2816
FF_BLK = D_FF // N_DEV
ATT_W = 256
ATT_E = 64
ATT_BLK = 128
DILATIONS = (1, 4, 16)
POOL_W = 256
POOL_HALO = 16
DN_W = 512
DN_H = 4
DN_E = 128
DN_C = 64
N_CHUNK = SEQ // DN_C
IN_W = 3080
IN_BLK = IN_W // N_DEV
EPS = 1e-6
EXT_ATT = 1280
EXT_REST = 3328
EXT_W = EXT_ATT + EXT_REST
R_DQKV, R_DZ, R_BB, R_AB, R_PU = 0, 1536, 2048, 2560, 3072

ADAM_LR, ADAM_B1, ADAM_B2, ADAM_EPS, ADAM_WD, ADAM_STEP = 0.001, 0.9, 0.999, 1e-08, 0.01, 10

VMEM_LIMIT = 60 * 1024 * 1024
MESH = pl.DeviceIdType.MESH


def _cparams(sem=None):
    kw = dict(vmem_limit_bytes=VMEM_LIMIT)
    if sem is not None:
        kw["dimension_semantics"] = sem
    return pltpu.CompilerParams(**kw)


def _dot(a, b, prec=None):
    return jnp.dot(a, b, preferred_element_type=F32, precision=prec)


def _dot_nt(a, b, prec=None):
    return lax.dot_general(a, b, (((1,), (1,)), ((), ())), preferred_element_type=F32, precision=prec)


def _dot_tn(a, b, prec=None):
    return lax.dot_general(a, b, (((0,), (0,)), ((), ())), preferred_element_type=F32, precision=prec)


def _sigmoid(x):
    return jax.nn.sigmoid(x)


def _rms_stats(x):
    r = lax.rsqrt(jnp.mean(x * x, axis=-1, keepdims=True) + EPS)
    return x * r, r


def _rms_bwd(xh, r, w, dh):
    dxh = dh * w
    dx = r * (dxh - xh * jnp.mean(dxh * xh, axis=-1, keepdims=True))
    return dx, jnp.sum(dh * xh, axis=0, keepdims=True)


FFN_T_FWD = 1024
FFN_T_BWD = 512


def ffn_fwd(x, nw, wgu, wd, name, exch=None):
    t = FFN_T_FWD

    def body(x_ref, nw_ref, wgu_ref, wd_ref, o_ref, h_scr, acc_scr):
        k = pl.program_id(1)

        @pl.when(k == 0)
        def _():
            xh, _r = _rms_stats(x_ref[...])
            h_scr[...] = (xh * nw_ref[...]).astype(BF16)
            acc_scr[...] = jnp.zeros_like(acc_scr)

        h = h_scr[...]
        hg = _dot(h, wgu_ref[0, 0])
        hu = _dot(h, wgu_ref[0, 1])
        a = (hg * _sigmoid(hg) * hu).astype(BF16)
        acc_scr[...] += _dot(a, wd_ref[0])

        @pl.when(k == N_DEV - 1)
        def _():
            o_ref[...] = x_ref[...] + 0.5 * acc_scr[...]

    return _call(
        body, name=name, grid=(SEQ // t, N_DEV),
        in_specs=[pl.BlockSpec((t, D_MODEL), lambda i, k: (i, 0)),
                  pl.BlockSpec((1, D_MODEL), lambda i, k: (0, 0)),
                  pl.BlockSpec((1, 2, D_MODEL, FF_BLK), lambda i, k: (k, 0, 0, 0)),
                  pl.BlockSpec((1, FF_BLK, D_MODEL), lambda i, k: (k, 0, 0))],
        out_specs=pl.BlockSpec((t, D_MODEL), lambda i, k: (i, 0)),
        out_shape=SDS((SEQ, D_MODEL), F32),
        scratch_shapes=[pltpu.VMEM((t, D_MODEL), BF16), pltpu.VMEM((t, D_MODEL), F32)],
        sem=("arbitrary", "arbitrary"), args=(x, nw, wgu, wd), exch=exch)


def ffn_bwd(x, dxo, nw, wgu, wd, name, exch=None):
    t = FFN_T_BWD
    nt = SEQ // t

    def body(x_ref, dxo_ref, nw_ref, wgu_ref, wd_ref, dx_ref, dwgu_ref, dwd_ref, dnw_ref,
             dh_scr, ag_scr, au_scr, ad_scr):
        k = pl.program_id(0)
        i = pl.program_id(1)
        rows = pl.ds(pl.multiple_of(i * t, t), t)
        xh, r = _rms_stats(x_ref[...])
        nw_v = nw_ref[...]
        h = (xh * nw_v).astype(BF16)
        dxo_v = dxo_ref[...]
        dy = (0.5 * dxo_v).astype(BF16)
        wg = wgu_ref[0, 0]
        wu = wgu_ref[0, 1]
        hg = _dot(h, wg)
        hu = _dot(h, wu)
        sg = _sigmoid(hg)
        sil = hg * sg
        a = (sil * hu).astype(BF16)
        da = _dot_nt(dy, wd_ref[0])
        dhu = (da * sil).astype(BF16)
        dhg = (da * hu * (sg * (1.0 + hg * (1.0 - sg)))).astype(BF16)
        p_d = _dot_tn(a, dy)
        p_g = _dot_tn(h, dhg)
        p_u = _dot_tn(h, dhu)
        dh = _dot_nt(dhg, wg) + _dot_nt(dhu, wu)

        @pl.when(i == 0)
        def _():
            ad_scr[...] = p_d
            ag_scr[...] = p_g
            au_scr[...] = p_u

        @pl.when(i > 0)
        def _():
            ad_scr[...] += p_d
            ag_scr[...] += p_g
            au_scr[...] += p_u

        @pl.when(i == nt - 1)
        def _():
            dwd_ref[0] = ad_scr[...].astype(BF16)
            dwgu_ref[0, 0] = ag_scr[...].astype(BF16)
            dwgu_ref[0, 1] = au_scr[...].astype(BF16)

        @pl.when(k == 0)
        def _():
            dh_scr[rows, :] = dh

        @pl.when(k > 0)
        def _():
            dh_scr[rows, :] += dh

        @pl.when(jnp.logical_and(k == 0, i == 0))
        def _():
            dnw_ref[...] = jnp.zeros_like(dnw_ref)

        @pl.when(k == N_DEV - 1)
        def _():
            dx, dw = _rms_bwd(xh, r, nw_v, dh_scr[rows, :])
            dx_ref[...] = dxo_v + dx
            dnw_ref[...] += dw

    last = N_DEV - 1
    return _call(
        body, name=name, grid=(N_DEV, nt),
        in_specs=[pl.BlockSpec((t, D_MODEL), lambda k, i: (i, 0)),
                  pl.BlockSpec((t, D_MODEL), lambda k, i: (i, 0)),
                  pl.BlockSpec((1, D_MODEL), lambda k, i: (0, 0)),
                  pl.BlockSpec((1, 2, D_MODEL, FF_BLK), lambda k, i: (k, 0, 0, 0)),
                  pl.BlockSpec((1, FF_BLK, D_MODEL), lambda k, i: (k, 0, 0))],
        out_specs=[pl.BlockSpec((t, D_MODEL), lambda k, i: (jnp.where(k == last, i, 0), 0)),
                   pl.BlockSpec((1, 2, D_MODEL, FF_BLK), lambda k, i: (k, 0, 0, 0)),
                   pl.BlockSpec((1, FF_BLK, D_MODEL), lambda k, i: (k, 0, 0)),
                   pl.BlockSpec((1, D_MODEL), lambda k, i: (0, 0))],
        out_shape=[SDS((SEQ, D_MODEL), F32), SDS((N_DEV, 2, D_MODEL, FF_BLK), BF16),
                   SDS((N_DEV, FF_BLK, D_MODEL), BF16), SDS((1, D_MODEL), F32)],
        scratch_shapes=[pltpu.VMEM((SEQ, D_MODEL), F32), pltpu.VMEM((D_MODEL, FF_BLK), F32),
                        pltpu.VMEM((D_MODEL, FF_BLK), F32), pltpu.VMEM((FF_BLK, D_MODEL), F32)],
        sem=("arbitrary", "arbitrary"), args=(x, dxo, nw, wgu, wd), exch=exch)


def loss_head(x, fw, target, name):
    t = 512

    def body(x_ref, fw_ref, tg_ref, loss_ref, dx_ref, dfw_ref):
        i = pl.program_id(0)
        xh, r = _rms_stats(x_ref[...])
        w = fw_ref[...]
        err = xh * w - tg_ref[...]
        part = 0.5 * jnp.sum(jnp.sum(err * err, axis=-1, keepdims=True), axis=0, keepdims=True) / D_MODEL
        dx, dw = _rms_bwd(xh, r, w, err * (1.0 / D_MODEL))
        dx_ref[...] = dx

        @pl.when(i == 0)
        def _():
            loss_ref[...] = jnp.zeros_like(loss_ref)
            dfw_ref[...] = jnp.zeros_like(dfw_ref)

        loss_ref[...] += jnp.broadcast_to(part, loss_ref.shape)
        dfw_ref[...] += dw

    return pl.pallas_call(
        body, name=name, grid=(SEQ // t,),
        in_specs=[pl.BlockSpec((t, D_MODEL), lambda i: (i, 0)),
                  pl.BlockSpec((1, D_MODEL), lambda i: (0, 0)),
                  pl.BlockSpec((t, D_MODEL), lambda i: (i, 0))],
        out_specs=[pl.BlockSpec((1, 128), lambda i: (0, 0)),
                   pl.BlockSpec((t, D_MODEL), lambda i: (i, 0)),
                   pl.BlockSpec((1, D_MODEL), lambda i: (0, 0))],
        out_shape=[SDS((1, 128), F32), SDS((SEQ, D_MODEL), F32), SDS((1, D_MODEL), F32)],
        compiler_params=_cparams(("arbitrary",)),
    )(x, fw, target)


MIX_T = 256


def mix_in_fwd(x, nw, wext, cos, sin, name):
    t = MIX_T

    def body(x_ref, nw_ref, w_ref, cos_ref, sin_ref, att_ref, rest_ref):
        xh, _r = _rms_stats(x_ref[...])
        h = (xh * nw_ref[...]).astype(BF16)
        pa = _dot(h, w_ref[:, 0:EXT_ATT])
        c = cos_ref[...]
        s = sin_ref[...]
        att_ref[:, 0:256] = pa[:, 0:256] * c + pa[:, 768:1024] * s
        att_ref[:, 256:512] = pa[:, 256:512] * c + pa[:, 1024:1280] * s
        att_ref[:, 512:768] = pa[:, 512:768]
        for j in range(EXT_REST // 256):
            rest_ref[:, 256 * j:256 * j + 256] = _dot(h, w_ref[:, EXT_ATT + 256 * j:EXT_ATT + 256 * j + 256])

    return pl.pallas_call(
        body, name=name, grid=(SEQ // t,),
        in_specs=[pl.BlockSpec((t, D_MODEL), lambda i: (i, 0)),
                  pl.BlockSpec((1, D_MODEL), lambda i: (0, 0)),
                  pl.BlockSpec((D_MODEL, EXT_W), lambda i: (0, 0)),
                  pl.BlockSpec((t, ATT_W), lambda i: (i, 0)),
                  pl.BlockSpec((t, ATT_W), lambda i: (i, 0))],
        out_specs=[pl.BlockSpec((t, 768), lambda i: (i, 0)),
                   pl.BlockSpec((t, EXT_REST), lambda i: (i, 0))],
        out_shape=[SDS((SEQ, 768), F32), SDS((SEQ, EXT_REST), F32)],
        compiler_params=_cparams(("arbitrary",)),
    )(x, nw, wext, cos, sin)


def assemble_dproj(datts, cos, sin, d_dqkv, dz, dbb, dab, dpu, name):
    t = 512

    def body(d1_ref, d4_ref, d16_ref, cos_ref, sin_ref, dqkv_ref, dz_ref, dbb_ref, dab_ref, dpu_ref, o_ref):
        da = d1_ref[...] + d4_ref[...] + d16_ref[...]
        c = cos_ref[...]
        s = sin_ref[...]
        dq = da[:, 0:256]
        dk = da[:, 256:512]
        o_ref[:, 0:256] = (dq * c).astype(BF16)
        o_ref[:, 256:512] = (dk * c).astype(BF16)
        o_ref[:, 512:768] = da[:, 512:768].astype(BF16)
        o_ref[:, 768:1024] = (dq * s).astype(BF16)
        o_ref[:, 1024:1280] = (dk * s).astype(BF16)
        b = EXT_ATT
        o_ref[:, b + R_DQKV:b + R_DQKV + 1536] = dqkv_ref[...].astype(BF16)
        o_ref[:, b + R_DZ:b + R_DZ + 512] = dz_ref[...].astype(BF16)
        o_ref[:, b + R_BB:b + R_BB + 512] = dbb_ref[...].astype(BF16)
        o_ref[:, b + R_AB:b + R_AB + 512] = dab_ref[...].astype(BF16)
        o_ref[:, b + R_PU:b + R_PU + 256] = dpu_ref[...].astype(BF16)

    row = lambda w: pl.BlockSpec((t, w), lambda i: (i, 0))
    return pl.pallas_call(
        body, name=name, grid=(SEQ // t,),
        in_specs=[row(768), row(768), row(768), row(256), row(256), row(1536), row(512), row(512), row(512), row(256)],
        out_specs=row(EXT_W),
        out_shape=SDS((SEQ, EXT_W), BF16),
        compiler_params=_cparams(("arbitrary",)),
    )(*datts, cos, sin, d_dqkv, dz, dbb, dab, dpu)


def linear_bwd(x, dxo, nw, dy, w, name):
    t = 512
    nb = 512
    n = w.shape[1]
    nt = SEQ // t
    nn = n // nb

    def body(x_ref, dxo_ref, nw_ref, dy_ref, w_ref, dx_ref, dw_ref, dnw_ref, dh_scr):
        k = pl.program_id(0)
        i = pl.program_id(1)
        rows = pl.ds(pl.multiple_of(i * t, t), t)
        xh, r = _rms_stats(x_ref[...])
        nw_v = nw_ref[...]
        h = (xh * nw_v).astype(BF16)
        dyv = dy_ref[...]
        p_w = _dot_tn(h, dyv)
        dh = _dot_nt(dyv, w_ref[...])

        @pl.when(i == 0)
        def _():
            dw_ref[...] = p_w

        @pl.when(i > 0)
        def _():
            dw_ref[...] += p_w

        @pl.when(k == 0)
        def _():
            dh_scr[rows, :] = dh

        @pl.when(k > 0)
        def _():
            dh_scr[rows, :] += dh

        @pl.when(jnp.logical_and(k == 0, i == 0))
        def _():
            dnw_ref[...] = jnp.zeros_like(dnw_ref)

        @pl.when(k == nn - 1)
        def _():
            dx, dw = _rms_bwd(xh, r, nw_v, dh_scr[rows, :])
            dx_ref[...] = dxo_ref[...] + dx
            dnw_ref[...] += dw

    last = nn - 1
    return pl.pallas_call(
        body, name=name, grid=(nn, nt),
        in_specs=[pl.BlockSpec((t, D_MODEL), lambda k, i: (i, 0)),
                  pl.BlockSpec((t, D_MODEL), lambda k, i: (i, 0)),
                  pl.BlockSpec((1, D_MODEL), lambda k, i: (0, 0)),
                  pl.BlockSpec((t, nb), lambda k, i: (i, k)),
                  pl.BlockSpec((D_MODEL, nb), lambda k, i: (0, k))],
        out_specs=[pl.BlockSpec((t, D_MODEL), lambda k, i: (jnp.where(k == last, i, 0), 0)),
                   pl.BlockSpec((D_MODEL, nb), lambda k, i: (0, k)),
                   pl.BlockSpec((1, D_MODEL), lambda k, i: (0, 0))],
        out_shape=[SDS((SEQ, D_MODEL), F32), SDS((D_MODEL, n), F32), SDS((1, D_MODEL), F32)],
        scratch_shapes=[pltpu.VMEM((SEQ, D_MODEL), F32)],
        compiler_params=_cparams(("arbitrary", "arbitrary")),
    )(x, dxo, nw, dy, w)


def _att_masks():
    qi = lax.broadcasted_iota(jnp.int32, (ATT_BLK, ATT_BLK), 0)
    ki = lax.broadcasted_iota(jnp.int32, (ATT_BLK, ATT_BLK), 1)
    return ki <= qi, ki >= qi


NEG = -1e30


def att_fwd(att, blocks_per_class, name):
    nblk = SEQ // ATT_BLK

    def body(cur_ref, prev_ref, o_ref):
        i = pl.program_id(0)
        has_prev = (i % blocks_per_class) != 0
        m_d, m_p = _att_masks()
        m_p = jnp.logical_and(m_p, has_prev)
        for hd in range(4):
            sl = slice(ATT_E * hd, ATT_E * hd + ATT_E)
            q = cur_ref[:, sl].astype(BF16)
            kc = cur_ref[:, 256 + ATT_E * hd:256 + ATT_E * hd + ATT_E].astype(BF16)
            vc = cur_ref[:, 512 + ATT_E * hd:512 + ATT_E * hd + ATT_E].astype(BF16)
            kp = prev_ref[:, 256 + ATT_E * hd:256 + ATT_E * hd + ATT_E].astype(BF16)
            vp = prev_ref[:, 512 + ATT_E * hd:512 + ATT_E * hd + ATT_E].astype(BF16)
            sd = jnp.where(m_d, _dot_nt(q, kc) * 0.125, NEG)
            sp = jnp.where(m_p, _dot_nt(q, kp) * 0.125, NEG)
            m = jnp.maximum(jnp.max(sd, axis=-1, keepdims=True), jnp.max(sp, axis=-1, keepdims=True))
            pd = jnp.exp(sd - m)
            pp = jnp.exp(sp - m)
            den = jnp.sum(pd, axis=-1, keepdims=True) + jnp.sum(pp, axis=-1, keepdims=True)
            inv = 1.0 / den
            o = _dot((pd * inv).astype(BF16), vc) + _dot((pp * inv).astype(BF16), vp)
            o_ref[:, sl] = o
            o_ref[:, 256 + ATT_E * hd:256 + ATT_E * hd + ATT_E] = jnp.broadcast_to(m + jnp.log(den), (ATT_BLK, ATT_E))

    return pl.pallas_call(
        body, name=name, grid=(nblk,),
        in_specs=[pl.BlockSpec((ATT_BLK, 768), lambda i: (i, 0)),
                  pl.BlockSpec((ATT_BLK, 768), lambda i: (jnp.maximum(i - 1, 0), 0))],
        out_specs=pl.BlockSpec((ATT_BLK, 512), lambda i: (i, 0)),
        out_shape=SDS((SEQ, 512), F32),
        compiler_params=_cparams(("arbitrary",)),
    )(att, att)


def att_bwd(att, ol, dol, blocks_per_class, name):
    nblk = SEQ // ATT_BLK

    def body(prev_ref, cur_ref, nxt_ref, ol_c_ref, ol_n_ref, dol_c_ref, dol_n_ref, d_ref):
        i = pl.program_id(0)
        has_prev = (i % blocks_per_class) != 0
        has_next = ((i + 1) % blocks_per_class) != 0
        m_d, m_band = _att_masks()
        m_p = jnp.logical_and(m_band, has_prev)
        m_n = jnp.logical_and(m_band, has_next)

        def pair(q, k, v, lse, do, dterm, mask):
            s = jnp.where(mask, _dot_nt(q, k) * 0.125, NEG)
            p = jnp.exp(s - lse)
            dp = _dot_nt(do, v)
            ds = (p * (dp + dterm) * 0.125).astype(BF16)
            return p.astype(BF16), ds

        for hd in range(4):
            a = ATT_E * hd
            q_c = cur_ref[:, a:a + ATT_E].astype(BF16)
            k_c = cur_ref[:, 256 + a:256 + a + ATT_E].astype(BF16)
            v_c = cur_ref[:, 512 + a:512 + a + ATT_E].astype(BF16)
            k_p = prev_ref[:, 256 + a:256 + a + ATT_E].astype(BF16)
            v_p = prev_ref[:, 512 + a:512 + a + ATT_E].astype(BF16)
            q_n = nxt_ref[:, a:a + ATT_E].astype(BF16)
            o_c = ol_c_ref[:, a:a + ATT_E]
            o_n = ol_n_ref[:, a:a + ATT_E]
            lse_c = ol_c_ref[:, 256 + a:256 + a + ATT_E][:, 0:1]
            lse_n = ol_n_ref[:, 256 + a:256 + a + ATT_E][:, 0:1]
            do_c = dol_c_ref[:, a:a + ATT_E]
            do_n = dol_n_ref[:, a:a + ATT_E]
            t_c = dol_c_ref[:, 256 + a:256 + a + ATT_E][:, 0:1] - jnp.sum(do_c * o_c, axis=-1, keepdims=True)
            t_n = dol_n_ref[:, 256 + a:256 + a + ATT_E][:, 0:1] - jnp.sum(do_n * o_n, axis=-1, keepdims=True)
            do_cb = do_c.astype(BF16)
            do_nb = do_n.astype(BF16)
            p1, ds1 = pair(q_c, k_c, v_c, lse_c, do_cb, t_c, m_d)
            _p2, ds2 = pair(q_c, k_p, v_p, lse_c, do_cb, t_c, m_p)
            p3, ds3 = pair(q_n, k_c, v_c, lse_n, do_nb, t_n, m_n)
            d_ref[:, a:a + ATT_E] = _dot(ds1, k_c) + _dot(ds2, k_p)
            d_ref[:, 256 + a:256 + a + ATT_E] = _dot_tn(ds1, q_c) + _dot_tn(ds3, q_n)
            d_ref[:, 512 + a:512 + a + ATT_E] = _dot_tn(p1, do_cb) + _dot_tn(p3, do_nb)

    prv = lambda i: (jnp.maximum(i - 1, 0), 0)
    cur = lambda i: (i, 0)
    nxt = lambda i: (jnp.minimum(i + 1, nblk - 1), 0)
    return pl.pallas_call(
        body, name=name, grid=(nblk,),
        in_specs=[pl.BlockSpec((ATT_BLK, 768), prv), pl.BlockSpec((ATT_BLK, 768), cur),
                  pl.BlockSpec((ATT_BLK, 768), nxt),
                  pl.BlockSpec((ATT_BLK, 512), cur), pl.BlockSpec((ATT_BLK, 512), nxt),
                  pl.BlockSpec((ATT_BLK, 512), cur), pl.BlockSpec((ATT_BLK, 512), nxt)],
        out_specs=pl.BlockSpec((ATT_BLK, 768), cur),
        out_shape=SDS((SEQ, 768), F32),
        compiler_params=_cparams(("arbitrary",)),
    )(att, att, att, ol, ol, dol, dol)


def to_classes(a, d):
    if d == 1:
        return a
    w = a.shape[1]
    return a.reshape(SEQ // d, d, w).transpose(1, 0, 2).reshape(SEQ, w)


def from_classes(a, d):
    if d == 1:
        return a
    w = a.shape[1]
    return a.reshape(d, SEQ // d, w).transpose(1, 0, 2).reshape(SEQ, w)


def _shift_down(x, k):
    rows = lax.broadcasted_iota(jnp.int32, x.shape, 0)
    return jnp.where(rows >= k, pltpu.roll(x, k, 0), 0.0)


def _shift_up(x, k):
    n = x.shape[0]
    rows = lax.broadcasted_iota(jnp.int32, x.shape, 0)
    return jnp.where(rows < n - k, pltpu.roll(x, n - k, 0), 0.0)


@functools.partial(jax.custom_vjp, nondiff_argnums=(1,))
def _delay(x, k):
    return _shift_down(x, k)


def _delay_fwd(x, k):
    return _shift_down(x, k), None


def _delay_bwd(k, _res, g):
    return (_shift_up(g, k),)


_delay.defvjp(_delay_fwd, _delay_bwd)

DN_CONV = 4


def _dn_prep_fn(u, w, kind):
    y = w[DN_CONV - 1:DN_CONV] * u
    for j in range(DN_CONV - 1):
        y = y + w[j:j + 1] * _delay(u, DN_CONV - 1 - j)
    y = y * _sigmoid(y)
    nrm = y * lax.rsqrt(jnp.sum(y * y, axis=-1, keepdims=True) + EPS)
    return jnp.where(kind == 0, nrm * (DN_E ** -0.5), jnp.where(kind == 1, nrm, y))


def dn_prep_fwd(rest, conv_w, name):
    def body(u_ref, w_ref, o_ref):
        j = pl.program_id(0)
        kind = (j >= DN_H).astype(jnp.int32) + (j >= 2 * DN_H).astype(jnp.int32)
        o_ref[...] = _dn_prep_fn(u_ref[...], w_ref[...], kind)

    return pl.pallas_call(
        body, name=name, grid=(3 * DN_H,),
        in_specs=[pl.BlockSpec((SEQ, DN_E), lambda j: (0, j)),
                  pl.BlockSpec((DN_CONV, DN_E), lambda j: (0, j))],
        out_specs=pl.BlockSpec((SEQ, DN_E), lambda j: (0, j)),
        out_shape=SDS((SEQ, 3 * DN_W), F32),
        compiler_params=_cparams(("arbitrary",)),
    )(rest, conv_w)


def dn_prep_bwd(rest, conv_w, dqkv, name):
    def body(u_ref, w_ref, g_ref, du_ref, dw_ref):
        j = pl.program_id(0)
        kind = (j >= DN_H).astype(jnp.int32) + (j >= 2 * DN_H).astype(jnp.int32)
        _y, vjp = jax.vjp(lambda u, w: _dn_prep_fn(u, w, kind), u_ref[...], w_ref[...])
        du, dw = vjp(g_ref[...])
        du_ref[...] = du
        dw_ref[...] = dw

    return pl.pallas_call(
        body, name=name, grid=(3 * DN_H,),
        in_specs=[pl.BlockSpec((SEQ, DN_E), lambda j: (0, j)),
                  pl.BlockSpec((DN_CONV, DN_E), lambda j: (0, j)),
                  pl.BlockSpec((SEQ, DN_E), lambda j: (0, j))],
        out_specs=[pl.BlockSpec((SEQ, DN_E), lambda j: (0, j)),
                   pl.BlockSpec((DN_CONV, DN_E), lambda j: (0, j))],
        out_shape=[SDS((SEQ, 3 * DN_W), F32), SDS((DN_CONV, 3 * DN_W), F32)],
        compiler_params=_cparams(("arbitrary",)),
    )(rest, conv_w, dqkv)


def _bdot(a, b, ca, cb, prec=None):
    return lax.dot_general(a, b, (((ca,), (cb,)), ((0,), (0,))), preferred_element_type=F32, precision=prec)


def _unit_lower_inverse(a):
    eye = (lax.broadcasted_iota(jnp.int32, (DN_C, DN_C), 0) == lax.broadcasted_iota(jnp.int32, (DN_C, DN_C), 1)).astype(F32)
    p = eye - a
    b = _bdot(a, a, 2, 1, HI)
    for lvl in range(5):
        p = p + _bdot(p, b, 2, 1, HI)
        if lvl < 4:
            b = _bdot(b, b, 2, 1, HI)
    return p


@jax.custom_vjp
def _tri_inv(a):
    return _unit_lower_inverse(a)


def _tri_inv_fwd(a):
    t = _unit_lower_inverse(a)
    return t, t


def _tri_inv_bwd(t, g):
    return (-_bdot(_bdot(t, g, 1, 1, HI), t, 2, 2, HI),)


_tri_inv.defvjp(_tri_inv_fwd, _tri_inv_bwd)


def _b16(x):
    return x.astype(BF16)


def _dn_chunk(q, k, v, bb, ab, alog, dtb, state):
    ri = lax.broadcasted_iota(jnp.int32, (DN_C, DN_C), 0)
    ci = lax.broadcasted_iota(jnp.int32, (DN_C, DN_C), 1)
    lower = ri >= ci
    strict = ri > ci
    nh = q.shape[0]
    beta = _sigmoid(bb)
    xg = ab + dtb
    softplus = jnp.maximum(xg, 0.0) + jnp.log(1.0 + jnp.exp(-jnp.abs(xg)))
    gi = -jnp.exp(alog) * softplus
    g = _bdot(jnp.broadcast_to(lower.astype(F32), (nh, DN_C, DN_C)), gi, 2, 1, HI)
    eg = jnp.exp(g)
    kb = k * beta
    vb = v * beta
    g_col = g[:, :, 0:DN_C]
    g_row = _bdot(jnp.full((nh, DN_C, DN_E), 1.0 / DN_E, F32), g, 2, 2, HI)
    decay = jnp.where(lower, jnp.exp(jnp.where(lower, g_col - g_row, 0.0)), 0.0)
    kbf = _b16(k)
    a = jnp.where(strict, _bdot(_b16(kb), kbf, 2, 2) * decay, 0.0)
    t = _tri_inv(a)
    tb = _b16(t)
    u = _bdot(tb, _b16(vb), 2, 1)
    w = _bdot(tb, _b16(kb * eg), 2, 1)
    intra = jnp.where(lower, _bdot(_b16(q), kbf, 2, 2) * decay, 0.0)
    sb = _b16(state)
    v_new = u - _bdot(_b16(w), sb, 2, 1)
    o = _bdot(_b16(q * eg), sb, 2, 1) + _bdot(_b16(intra), _b16(v_new), 2, 1)
    g_last = g[:, DN_C - 1:DN_C, :]
    k_dec = k * jnp.exp(g_last - g)
    new_state = state * jnp.exp(g_last) + _bdot(_b16(k_dec), _b16(v_new), 1, 1)
    return o, new_state


def _heads(ref, base=0):
    return jnp.stack([ref[:, base + DN_E * hd:base + DN_E * hd + DN_E] for hd in range(DN_H)], axis=0)


def _put_heads(ref, val, base=0):
    for hd in range(DN_H):
        ref[:, base + DN_E * hd:base + DN_E * hd + DN_E] = val[hd]


def _dn_args(qkv_ref, bb_ref, ab_ref, alog_ref, dtb_ref, state):
    return (_heads(qkv_ref), _heads(qkv_ref, DN_W), _heads(qkv_ref, 2 * DN_W), _heads(bb_ref), _heads(ab_ref),
            _heads(alog_ref), _heads(dtb_ref), state)


def dn_chunk_fwd(qkv, rest, alog_b, dtb_b, name, exch=None):
    def body(qkv_ref, bb_ref, ab_ref, alog_ref, dtb_ref, o_ref, st_ref, state_scr):
        n = pl.program_id(0)

        @pl.when(n == 0)
        def _():
            state_scr[...] = jnp.zeros_like(state_scr)

        st = state_scr[...]
        st_ref[0] = st
        o, ns = _dn_chunk(*_dn_args(qkv_ref, bb_ref, ab_ref, alog_ref, dtb_ref, st))
        _put_heads(o_ref, o)
        state_scr[...] = ns

    return _call(
        body, name=name, grid=(N_CHUNK,),
        in_specs=[pl.BlockSpec((DN_C, 3 * DN_W), lambda n: (n, 0)),
                  pl.BlockSpec((DN_C, DN_W), lambda n: (n, R_BB // DN_W)),
                  pl.BlockSpec((DN_C, DN_W), lambda n: (n, R_AB // DN_W)),
                  pl.BlockSpec((1, DN_W), lambda n: (0, 0)),
                  pl.BlockSpec((1, DN_W), lambda n: (0, 0))],
        out_specs=[pl.BlockSpec((DN_C, DN_W), lambda n: (n, 0)),
                   pl.BlockSpec((1, DN_H, DN_E, DN_E), lambda n: (n, 0, 0, 0))],
        out_shape=[SDS((SEQ, DN_W), F32), SDS((N_CHUNK, DN_H, DN_E, DN_E), F32)],
        scratch_shapes=[pltpu.VMEM((DN_H, DN_E, DN_E), F32)],
        sem=("arbitrary",), args=(qkv, rest, rest, alog_b, dtb_b), exch=exch)


def dn_chunk_bwd(qkv, rest, alog_b, dtb_b, states, do, name, exch=None):
    last = N_CHUNK - 1

    def body(qkv_ref, bb_ref, ab_ref, alog_ref, dtb_ref, st_ref, do_ref,
             dqkv_ref, dbb_ref, dab_ref, dalog_ref, ddtb_ref, dstate_scr):
        s = pl.program_id(0)

        @pl.when(s == 0)
        def _():
            dstate_scr[...] = jnp.zeros_like(dstate_scr)
            dalog_ref[...] = jnp.zeros_like(dalog_ref)
            ddtb_ref[...] = jnp.zeros_like(ddtb_ref)

        _out, vjp = jax.vjp(_dn_chunk, *_dn_args(qkv_ref, bb_ref, ab_ref, alog_ref, dtb_ref, st_ref[0]))
        dq, dk, dv, dbb, dab, dalog, ddtb, dst = vjp((_heads(do_ref), dstate_scr[...]))
        _put_heads(dqkv_ref, dq)
        _put_heads(dqkv_ref, dk, DN_W)
        _put_heads(dqkv_ref, dv, 2 * DN_W)
        _put_heads(dbb_ref, dbb)
        _put_heads(dab_ref, dab)
        _put_heads(dalog_ref, _heads(dalog_ref) + dalog)
        _put_heads(ddtb_ref, _heads(ddtb_ref) + ddtb)
        dstate_scr[...] = dst

    rev = lambda w: (lambda s: (last - s, w))
    return _call(
        body, name=name, grid=(N_CHUNK,),
        in_specs=[pl.BlockSpec((DN_C, 3 * DN_W), rev(0)),
                  pl.BlockSpec((DN_C, DN_W), rev(R_BB // DN_W)),
                  pl.BlockSpec((DN_C, DN_W), rev(R_AB // DN_W)),
                  pl.BlockSpec((1, DN_W), lambda s: (0, 0)),
                  pl.BlockSpec((1, DN_W), lambda s: (0, 0)),
                  pl.BlockSpec((1, DN_H, DN_E, DN_E), lambda s: (last - s, 0, 0, 0)),
                  pl.BlockSpec((DN_C, DN_W), rev(0))],
        out_specs=[pl.BlockSpec((DN_C, 3 * DN_W), rev(0)),
                   pl.BlockSpec((DN_C, DN_W), rev(0)),
                   pl.BlockSpec((DN_C, DN_W), rev(0)),
                   pl.BlockSpec((1, DN_W), lambda s: (0, 0)),
                   pl.BlockSpec((1, DN_W), lambda s: (0, 0))],
        out_shape=[SDS((SEQ, 3 * DN_W), F32), SDS((SEQ, DN_W), F32), SDS((SEQ, DN_W), F32),
                   SDS((1, DN_W), F32), SDS((1, DN_W), F32)],
        scratch_shapes=[pltpu.VMEM((DN_H, DN_E, DN_E), F32)],
        sem=("arbitrary",), args=(qkv, rest, rest, alog_b, dtb_b, states, do), exch=exch)


OUT_T = 256


def _pool_consts(rows_total, t0, halo_before):
    lane = lax.broadcasted_iota(jnp.int32, (rows_total, POOL_W), 1)
    row = lax.broadcasted_iota(jnp.int32, (rows_total, POOL_W), 0)
    grp = (lane >= 64).astype(jnp.int32) + (lane >= 128).astype(jnp.int32) + (lane >= 192).astype(jnp.int32)
    win = jnp.where(grp == 0, 2, jnp.where(grp == 1, 4, jnp.where(grp == 2, 8, 16)))
    pos = t0 + row - halo_before
    cnt = jnp.minimum(pos + 1, win).astype(F32)
    return grp, cnt


def _pool_select(grp, s2, s4, s8, s16):
    return jnp.where(grp == 0, s2, jnp.where(grp == 1, s4, jnp.where(grp == 2, s8, s16)))


def _pooled(u_ext, t0):
    n = u_ext.shape[0]
    grp, cnt = _pool_consts(n, t0, POOL_HALO)
    s2 = u_ext + pltpu.roll(u_ext, 1, 0)
    s4 = s2 + pltpu.roll(s2, 2, 0)
    s8 = s4 + pltpu.roll(s4, 4, 0)
    s16 = s8 + pltpu.roll(s8, 8, 0)
    out = _pool_select(grp, s2, s4, s8, s16) / jnp.maximum(cnt, 1.0) - u_ext
    return out[POOL_HALO:, :]


def _merge_weights(l1, l4, l16):
    m = jnp.maximum(jnp.maximum(l1, l4), l16)
    e1 = jnp.exp(l1 - m)
    e4 = jnp.exp(l4 - m)
    e16 = jnp.exp(l16 - m)
    inv = 1.0 / (e1 + e4 + e16)
    return e1 * inv, e4 * inv, e16 * inv


def _out_parts(ol1_ref, ol4_ref, ol16_ref, pu_ref, puh_ref, odn_ref, z_ref, wbd_ref, i, t):
    w1, w4, w16 = _merge_weights(ol1_ref[:, 256:512], ol4_ref[:, 256:512], ol16_ref[:, 256:512])
    ya = w1 * ol1_ref[:, 0:256] + w4 * ol4_ref[:, 0:256] + w16 * ol16_ref[:, 0:256]
    halo = jnp.where(i > 0, puh_ref[...], 0.0)
    pooled = _pooled(jnp.concatenate([halo, pu_ref[...]], axis=0), i * t)
    pw = _dot(pooled.astype(BF16), wbd_ref[...])
    return ya, pooled, pw, (w1, w4, w16)


def _out_specs_common(t):
    def row(w, cb=0):
        return pl.BlockSpec((t, w), lambda i: (i, cb))

    halo = pl.BlockSpec((POOL_HALO, POOL_W),
                        lambda i: (jnp.maximum(i * (t // POOL_HALO) - 1, 0), R_PU // POOL_W))
    full = lambda a, b: pl.BlockSpec((a, b), lambda i: (0, 0))
    return [row(512), row(512), row(512), row(POOL_W, R_PU // POOL_W), halo, row(DN_W), row(DN_W, R_DZ // DN_W),
            full(POOL_W, POOL_W), full(1, POOL_W), full(1, DN_W), full(D_MODEL, D_MODEL)]


def mix_out_fwd(x, ol1, ol4, ol16, rest, odn, wbd, scale, onorm_b, wout, name):
    t = OUT_T

    def body(x_ref, ol1_ref, ol4_ref, ol16_ref, pu_ref, puh_ref, odn_ref, z_ref, wbd_ref, sc_ref, on_ref, wo_ref, o_ref):
        i = pl.program_id(0)
        ya, _pooled_v, pw, _w = _out_parts(ol1_ref, ol4_ref, ol16_ref, pu_ref, puh_ref, odn_ref, z_ref, wbd_ref, i, t)
        yb = pw * sc_ref[...]
        acc = x_ref[...] + _dot(ya.astype(BF16), wo_ref[0:256, :]) + _dot(yb.astype(BF16), wo_ref[256:512, :])
        for hd in range(DN_H):
            sl = slice(DN_E * hd, DN_E * hd + DN_E)
            oh, _r = _rms_stats(odn_ref[:, sl])
            z = z_ref[:, sl]
            yc = oh * on_ref[:, sl] * (z * _sigmoid(z))
            acc = acc + _dot(yc.astype(BF16), wo_ref[512 + DN_E * hd:512 + DN_E * hd + DN_E, :])
        o_ref[...] = acc

    return pl.pallas_call(
        body, name=name, grid=(SEQ // t,),
        in_specs=[pl.BlockSpec((t, D_MODEL), lambda i: (i, 0))] + _out_specs_common(t),
        out_specs=pl.BlockSpec((t, D_MODEL), lambda i: (i, 0)),
        out_shape=SDS((SEQ, D_MODEL), F32),
        compiler_params=_cparams(("arbitrary",)),
    )(x, ol1, ol4, ol16, rest, rest, odn, rest, wbd, scale, onorm_b, wout)


def mix_out_bwd(dxo, ol1, ol4, ol16, rest, odn, wbd, scale, onorm_b, wout, headsum, name):
    t = OUT_T

    def body(dxo_ref, ol1_ref, ol4_ref, ol16_ref, pu_ref, puh_ref, odn_ref, z_ref, wbd_ref, sc_ref, on_ref, wo_ref, hs_ref,
             dwo_ref, d1_ref, d4_ref, d16_ref, dpl_ref, dodn_ref, dz_ref, dsc_ref, don_ref, dwbd_ref):
        i = pl.program_id(0)

        @pl.when(i == 0)
        def _():
            dwo_ref[...] = jnp.zeros_like(dwo_ref)
            dsc_ref[...] = jnp.zeros_like(dsc_ref)
            don_ref[...] = jnp.zeros_like(don_ref)
            dwbd_ref[...] = jnp.zeros_like(dwbd_ref)

        ya, pooled, pw, (w1, w4, w16) = _out_parts(ol1_ref, ol4_ref, ol16_ref, pu_ref, puh_ref, odn_ref, z_ref, wbd_ref, i, t)
        sc = sc_ref[...]
        dxb = dxo_ref[...].astype(BF16)
        dwo_ref[0:256, :] += _dot_tn(ya.astype(BF16), dxb)
        dwo_ref[256:512, :] += _dot_tn((pw * sc).astype(BF16), dxb)
        dya = _dot_nt(dxb, wo_ref[0:256, :])
        o1 = ol1_ref[:, 0:256]
        o4 = ol4_ref[:, 0:256]
        o16 = ol16_ref[:, 0:256]
        hs = hs_ref[...]
        s1 = _dot(dya * o1, hs, HI)
        s4 = _dot(dya * o4, hs, HI)
        s16 = _dot(dya * o16, hs, HI)
        sbar = w1 * s1 + w4 * s4 + w16 * s16
        d1_ref[:, 0:256] = w1 * dya
        d1_ref[:, 256:512] = w1 * (s1 - sbar)
        d4_ref[:, 0:256] = w4 * dya
        d4_ref[:, 256:512] = w4 * (s4 - sbar)
        d16_ref[:, 0:256] = w16 * dya
        d16_ref[:, 256:512] = w16 * (s16 - sbar)
        dyb = _dot_nt(dxb, wo_ref[256:512, :])
        dsc_ref[...] += jnp.sum(dyb * pw, axis=0, keepdims=True)
        dpw = (dyb * sc).astype(BF16)
        dwbd_ref[...] += _dot_tn(pooled.astype(BF16), dpw)
        dpl_ref[...] = _dot_nt(dpw, wbd_ref[...])
        for hd in range(DN_H):
            sl = slice(DN_E * hd, DN_E * hd + DN_E)
            rows_w = slice(512 + DN_E * hd, 512 + DN_E * hd + DN_E)
            oh, r = _rms_stats(odn_ref[:, sl])
            z = z_ref[:, sl]
            sg = _sigmoid(z)
            sz = z * sg
            nw = on_ref[:, sl]
            on = oh * nw
            dwo_ref[rows_w, :] += _dot_tn((on * sz).astype(BF16), dxb)
            dyc = _dot_nt(dxb, wo_ref[rows_w, :])
            dz_ref[:, sl] = dyc * on * (sg * (1.0 + z * (1.0 - sg)))
            dx, dw = _rms_bwd(oh, r, nw, dyc * sz)
            dodn_ref[:, sl] = dx
            don_ref[:, sl] += dw

    row = lambda w: pl.BlockSpec((t, w), lambda i: (i, 0))
    full = lambda a, b: pl.BlockSpec((a, b), lambda i: (0, 0))
    return pl.pallas_call(
        body, name=name, grid=(SEQ // t,),
        in_specs=[row(D_MODEL)] + _out_specs_common(t) + [full(ATT_W, ATT_W)],
        out_specs=[full(D_MODEL, D_MODEL), row(512), row(512), row(512), row(POOL_W), row(DN_W), row(DN_W),
                   full(1, POOL_W), full(1, DN_W), full(POOL_W, POOL_W)],
        out_shape=[SDS((D_MODEL, D_MODEL), F32), SDS((SEQ, 512), F32), SDS((SEQ, 512), F32), SDS((SEQ, 512), F32),
                   SDS((SEQ, POOL_W), F32), SDS((SEQ, DN_W), F32), SDS((SEQ, DN_W), F32),
                   SDS((1, POOL_W), F32), SDS((1, DN_W), F32), SDS((POOL_W, POOL_W), F32)],
        compiler_params=_cparams(("arbitrary",)),
    )(dxo, ol1, ol4, ol16, rest, rest, odn, rest, wbd, scale, onorm_b, wout, headsum)


def pool_bwd(dpooled, name):
    t = 512
    nt = SEQ // t

    def body(d_ref, dn_ref, o_ref):
        i = pl.program_id(0)
        halo = jnp.where(i < nt - 1, dn_ref[...], 0.0)
        d_ext = jnp.concatenate([d_ref[...], halo], axis=0)
        n = t + POOL_HALO
        grp, cnt = _pool_consts(n, i * t, 0)
        dq = d_ext / cnt
        s2 = dq + pltpu.roll(dq, n - 1, 0)
        s4 = s2 + pltpu.roll(s2, n - 2, 0)
        s8 = s4 + pltpu.roll(s4, n - 4, 0)
        s16 = s8 + pltpu.roll(s8, n - 8, 0)
        o_ref[...] = (_pool_select(grp, s2, s4, s8, s16) - d_ext)[0:t, :]

    return pl.pallas_call(
        body, name=name, grid=(nt,),
        in_specs=[pl.BlockSpec((t, POOL_W), lambda i: (i, 0)),
                  pl.BlockSpec((POOL_HALO, POOL_W),
                               lambda i: (jnp.minimum((i + 1) * (t // POOL_HALO), SEQ // POOL_HALO - 1), 0))],
        out_specs=pl.BlockSpec((t, POOL_W), lambda i: (i, 0)),
        out_shape=SDS((SEQ, POOL_W), F32),
        compiler_params=_cparams(("arbitrary",)),
    )(dpooled, dpooled)


N_PEER = N_DEV - 1
ANY_SPEC = pl.BlockSpec(memory_space=pl.ANY)


class Exchange:
    def __init__(self, arrays, mode):
        self.arrays = list(arrays)
        self.mode = mode
        n = len(self.arrays)
        if mode == "scatter":
            self.out_shape = [SDS(a.shape, a.dtype) for a in self.arrays]
        else:
            self.out_shape = [SDS((N_DEV,) + a.shape, a.dtype) for a in self.arrays]
        self.scratch = [pltpu.SemaphoreType.DMA((n * N_PEER,)), pltpu.SemaphoreType.DMA((n * N_PEER,)),
                        pltpu.SemaphoreType.DMA((n,))]

    @staticmethod
    def _place():
        x, y, c = lax.axis_index("x"), lax.axis_index("y"), lax.axis_index("c")
        chips = [(1 - x, y), (x, 1 - y), (1 - x, 1 - y)]
        return x, y, c, chips

    @staticmethod
    def _copy(sems, a, k, src, dst, to):
        send_sems, recv_sems, _ = sems
        return pltpu.make_async_remote_copy(
            src_ref=src, dst_ref=dst, send_sem=send_sems.at[a * N_PEER + k], recv_sem=recv_sems.at[a * N_PEER + k],
            device_id=to, device_id_type=MESH)

    def _scatter_peers(self):
        x, y, c, _ = self._place()
        out = []
        for fx, fy, fc in ((0, 0, 1), (1, 0, 0), (0, 1, 0), (1, 1, 0), (1, 0, 1), (0, 1, 1), (1, 1, 1)):
            px, py, pc = x ^ fx, y ^ fy, c ^ fc
            out.append(((px, py, pc), 4 * px + 2 * py + pc))
        return 4 * x + 2 * y + c, out

    def _local(self, ins, outs, sems, a, me):
        src = ins[a].at[me] if self.mode == "scatter" else ins[a]
        return pltpu.make_async_copy(src, outs[a].at[me], sems[2].at[a])

    def start(self, ins, outs, sems):
        if self.mode == "scatter":
            me, peers = self._scatter_peers()
            for a in range(len(ins)):
                self._local(ins, outs, sems, a, me).start()
                for k, (peer, pidx) in enumerate(peers):
                    self._copy(sems, a, k, ins[a].at[pidx], outs[a].at[me], peer).start()
            return
        x, y, c, chips = self._place()
        me = 4 * x + 2 * y + c
        for a in range(len(ins)):
            self._local(ins, outs, sems, a, me).start()
            self._copy(sems, a, 0, ins[a], outs[a].at[me], (x, y, 1 - c)).start()
            for j, (cx, cy) in enumerate(chips):
                self._copy(sems, a, 1 + j, ins[a], outs[a].at[me], (cx, cy, c)).start()

    def finish(self, ins, outs, sems):
        n = len(ins)
        if self.mode == "scatter":
            me, peers = self._scatter_peers()
            for a in range(n):
                for k, (peer, pidx) in enumerate(peers):
                    self._copy(sems, a, k, ins[a].at[pidx], outs[a].at[pidx], peer).wait_recv()
            for a in range(n):
                for k, (peer, pidx) in enumerate(peers):
                    self._copy(sems, a, k, ins[a].at[pidx], outs[a].at[me], peer).wait_send()
                self._local(ins, outs, sems, a, me).wait()
            return
        x, y, c, chips = self._place()
        me = 4 * x + 2 * y + c
        sib = (x, y, 1 - c)
        for a in range(n):
            for j, (cx, cy) in enumerate(chips):
                blk = outs[a].at[4 * cx + 2 * cy + c]
                self._copy(sems, a, 1 + j, ins[a], blk, (cx, cy, c)).wait_recv()
                self._copy(sems, a, 4 + j, blk, blk, sib).start()
        for a in range(n):
            self._copy(sems, a, 0, ins[a], outs[a].at[4 * x + 2 * y + (1 - c)], sib).wait_recv()
            for j, (cx, cy) in enumerate(chips):
                blk = outs[a].at[4 * cx + 2 * cy + (1 - c)]
                self._copy(sems, a, 4 + j, blk, blk, sib).wait_recv()
        for a in range(n):
            for k in range(N_PEER):
                self._copy(sems, a, k, ins[a], outs[a].at[me], sib).wait_send()
            self._local(ins, outs, sems, a, me).wait()


def run_exchange(exch, name):
    n = len(exch.arrays)

    def body(*refs):
        ins, outs, sems = refs[:n], refs[n:2 * n], refs[2 * n:]
        exch.start(ins, outs, sems)
        exch.finish(ins, outs, sems)

    return pl.pallas_call(
        body, name=name, in_specs=[ANY_SPEC] * n, out_specs=[ANY_SPEC] * n, out_shape=exch.out_shape,
        scratch_shapes=exch.scratch,
    )(*exch.arrays)


def _call(body, *, name, grid, in_specs, out_specs, out_shape, scratch_shapes, sem, args, exch=None):
    if exch is None:
        res = pl.pallas_call(body, name=name, grid=grid, in_specs=in_specs, out_specs=out_specs, out_shape=out_shape,
                             scratch_shapes=scratch_shapes, compiler_params=_cparams(sem))(*args)
        return res, None
    single = not isinstance(out_shape, (list, tuple))
    out_specs_l = [out_specs] if single else list(out_specs)
    out_shape_l = [out_shape] if single else list(out_shape)
    n_in, n_out, n_scr, m = len(in_specs), len(out_specs_l), len(scratch_shapes), len(exch.arrays)

    def wrapped(*refs):
        p = 0
        ins = refs[p:p + n_in]; p += n_in
        xin = refs[p:p + m]; p += m
        outs = refs[p:p + n_out]; p += n_out
        xout = refs[p:p + m]; p += m
        scr = refs[p:p + n_scr]; p += n_scr
        sems = refs[p:]
        ids = [pl.program_id(ax) for ax in range(len(grid))]
        first = functools.reduce(jnp.logical_and, [i == 0 for i in ids])
        last = functools.reduce(jnp.logical_and, [i == g - 1 for i, g in zip(ids, grid)])

        @pl.when(first)
        def _():
            exch.start(xin, xout, sems)

        body(*ins, *outs, *scr)

        @pl.when(last)
        def _():
            exch.finish(xin, xout, sems)

    res = pl.pallas_call(
        wrapped, name=name, grid=grid, in_specs=list(in_specs) + [ANY_SPEC] * m,
        out_specs=out_specs_l + [ANY_SPEC] * m, out_shape=out_shape_l + exch.out_shape,
        scratch_shapes=list(scratch_shapes) + exch.scratch, compiler_params=_cparams(sem),
    )(*args, *exch.arrays)
    outs = res[:n_out]
    return (outs[0] if single else outs), res[n_out:]


def _adam_math(w, g, m, v):
    m2 = ADAM_B1 * m + (1.0 - ADAM_B1) * g
    v2 = ADAM_B2 * v + (1.0 - ADAM_B2) * (g * g)
    m_hat = m2 / (1.0 - ADAM_B1 ** ADAM_STEP)
    v_hat = v2 / (1.0 - ADAM_B2 ** ADAM_STEP)
    delta = -ADAM_LR * (m_hat / (jnp.sqrt(v_hat) + ADAM_EPS) + ADAM_WD * w)
    return delta, m2, v2


ADAM_ROW_BLOCKS = 2


def adam_shard(parts0, parts1, w, m, v, name, part_slice=None):
    _, r, c = w.shape
    sub = part_slice
    rb = r // ADAM_ROW_BLOCKS

    def body(p0_ref, p1_ref, w_ref, m_ref, v_ref, g_ref, d_ref, m2_ref, v2_ref):
        def run(p_ref):
            g = p_ref[0].astype(F32)
            for i in range(1, N_DEV):
                g = g + p_ref[i].astype(F32)
            delta, m2, v2 = _adam_math(w_ref[0], g, m_ref[0], v_ref[0])
            g_ref[0] = g
            d_ref[0] = delta
            m2_ref[0] = m2
            v2_ref[0] = v2

        @pl.when(pl.program_id(0) == 0)
        def _():
            run(p0_ref)

        @pl.when(pl.program_id(0) == 1)
        def _():
            run(p1_ref)

    def p_spec(layer):
        row = (lambda l, j: jnp.where(l == 0, j, ADAM_ROW_BLOCKS - 1)) if layer == 0 else (lambda l, j: jnp.where(l == 1, j, 0))
        if sub is None:
            return pl.BlockSpec((N_DEV, rb, c), lambda l, j: (0, row(l, j), 0))
        return pl.BlockSpec((N_DEV, None, rb, c), lambda l, j: (0, sub, row(l, j), 0))

    blk = pl.BlockSpec((1, rb, c), lambda l, j: (l, j, 0))
    return pl.pallas_call(
        body, name=name, grid=(DEPTH, ADAM_ROW_BLOCKS),
        in_specs=[p_spec(0), p_spec(1), blk, blk, blk], out_specs=[blk] * 4,
        out_shape=[SDS(w.shape, F32)] * 4,
        compiler_params=_cparams(("arbitrary", "arbitrary")),
    )(parts0, parts1, w, m, v)


def adam_small(parts, w, m, v, name):
    def body(p_ref, w_ref, m_ref, v_ref, g_ref, d_ref, m2_ref, v2_ref):
        g = p_ref[0]
        for i in range(1, N_DEV):
            g = g + p_ref[i]
        delta, m2, v2 = _adam_math(w_ref[...], g, m_ref[...], v_ref[...])
        g_ref[...] = g
        d_ref[...] = delta
        m2_ref[...] = m2
        v2_ref[...] = v2

    return pl.pallas_call(
        body, name=name, out_shape=[SDS(w.shape, F32)] * 4, compiler_params=_cparams(),
    )(parts, w, m, v)


def _rot_cols(w):
    w4 = w.reshape(w.shape[0], 4, 2, 32)
    return jnp.stack([-w4[:, :, 1], w4[:, :, 0]], axis=2).reshape(w.shape[0], ATT_W)


def _rot_cols_t(dw_rot):
    d4 = dw_rot.reshape(dw_rot.shape[0], 4, 2, 32)
    return jnp.stack([d4[:, :, 1], -d4[:, :, 0]], axis=2).reshape(dw_rot.shape[0], ATT_W)


def build_wext(w_in):
    aq, ak, av, pu = w_in[:, 0:256], w_in[:, 256:512], w_in[:, 512:768], w_in[:, 768:1024]
    dqkvz = w_in[:, 1024:3072]
    gates = jnp.repeat(w_in[:, 3072:3080], DN_E, axis=1)
    return jnp.concatenate([aq, ak, av, _rot_cols(aq), _rot_cols(ak), dqkvz, gates, pu], axis=1)


def fold_dwext(d):
    b = EXT_ATT
    aq = d[:, 0:256] + _rot_cols_t(d[:, 768:1024])
    ak = d[:, 256:512] + _rot_cols_t(d[:, 1024:1280])
    av = d[:, 512:768]
    dqkvz = d[:, b:b + 2048]
    gates = d[:, b + R_BB:b + R_BB + 1024].reshape(d.shape[0], 8, DN_E).sum(axis=-1)
    pu = d[:, b + R_PU:b + R_PU + 256]
    return jnp.concatenate([aq, ak, av, pu, dqkvz, gates], axis=1)


def _block_diag(pw):
    z = jnp.zeros((4, 64, 4, 64), pw.dtype)
    for g in range(4):
        z = z.at[g, :, g, :].set(pw[g])
    return z.reshape(POOL_W, POOL_W)


def _diag_blocks(m):
    m4 = m.reshape(4, 64, 4, 64)
    return jnp.stack([m4[g, :, g, :] for g in range(4)], axis=0)


def _lanes(v, reps):
    return jnp.repeat(v, reps)[None, :]


def layer_fwd(p, xa, cos, sin, l, host=None):
    host = host or {}

    def carried(key):
        return host[key][0] if key in host else None

    def done(key, xo):
        if key in host:
            host[key][1](xo)

    xb, xo = ffn_fwd(xa, p["n1"], p["f1gu"], p["f1d"], f"ffn1_fwd_{l}", carried("ffn1"))
    done("ffn1", xo)
    att, rest = mix_in_fwd(xb, p["nm"], p["wext"], cos, sin, f"mix_in_fwd_{l}")
    ols = [from_classes(att_fwd(to_classes(att, d), SEQ // d // ATT_BLK, f"att_fwd_{l}_{d}"), d) for d in DILATIONS]
    qkv = dn_prep_fwd(rest, p["conv"], f"dn_prep_fwd_{l}")
    (odn, states), xo = dn_chunk_fwd(qkv, rest, p["alog"], p["dtb"], f"dn_chunk_fwd_{l}", carried("dn"))
    done("dn", xo)
    xc = mix_out_fwd(xb, ols[0], ols[1], ols[2], rest, odn, p["wbd"], p["scale"], p["onorm"], p["wout"], f"mix_out_fwd_{l}")
    xd, xo = ffn_fwd(xc, p["n2"], p["f2gu"], p["f2d"], f"ffn2_fwd_{l}", carried("ffn2"))
    done("ffn2", xo)
    return xd, dict(xa=xa, xb=xb, xc=xc, att=att, rest=rest, ols=ols, qkv=qkv, odn=odn, states=states)


def layer_bwd(p, s, dx, cos, sin, headsum, l, scatter=False, carry=None):
    (dx, d_f2gu, d_f2d, d_n2), carried = ffn_bwd(s["xc"], dx, p["n2"], p["f2gu"], p["f2d"], f"ffn2_bwd_{l}", carry)
    (d_wout, dol1, dol4, dol16, dpooled, dodn, dz, dscale, donorm, dwbd) = mix_out_bwd(
        dx, s["ols"][0], s["ols"][1], s["ols"][2], s["rest"], s["odn"], p["wbd"], p["scale"], p["onorm"], p["wout"],
        headsum, f"mix_out_bwd_{l}")
    dpu = pool_bwd(dpooled, f"pool_bwd_{l}")
    f2 = [d_f2gu, d_f2d]
    (dqkv, dbb, dab, dalog, ddtb), xo = dn_chunk_bwd(
        s["qkv"], s["rest"], p["alog"], p["dtb"], s["states"], dodn, f"dn_chunk_bwd_{l}",
        Exchange(f2, "scatter") if scatter else None)
    if scatter:
        f2 = list(xo)
    d_dqkv, dconv = dn_prep_bwd(s["rest"], p["conv"], dqkv, f"dn_prep_bwd_{l}")
    datts = []
    for d, ol, dol in zip(DILATIONS, s["ols"], (dol1, dol4, dol16)):
        da = att_bwd(to_classes(s["att"], d), to_classes(ol, d), to_classes(dol, d), SEQ // d // ATT_BLK, f"att_bwd_{l}_{d}")
        datts.append(from_classes(da, d))
    dproj = assemble_dproj(datts, cos, sin, d_dqkv, dz, dbb, dab, dpu, f"assemble_dproj_{l}")
    dx, d_wext, d_nm = linear_bwd(s["xb"], dx, p["nm"], dproj, p["wext"], f"mix_in_bwd_{l}")
    d_win = fold_dwext(d_wext).reshape(D_MODEL, N_DEV, IN_BLK).transpose(1, 0, 2).astype(BF16)
    io = [d_win, d_wout.reshape(N_DEV, D_MODEL // N_DEV, D_MODEL).astype(BF16)]
    (dx, d_f1gu, d_f1d, d_n1), xo = ffn_bwd(s["xa"], dx, p["n1"], p["f1gu"], p["f1d"], f"ffn1_bwd_{l}",
                                           Exchange(io, "scatter") if scatter else None)
    if scatter:
        io = list(xo)
    big = dict(f1=[d_f1gu, d_f1d], f2=f2, io=io)
    small = dict(ffn1_norm=d_n1[0], mix_norm=d_nm[0], ffn2_norm=d_n2[0], pool_w=_diag_blocks(dwbd),
                 pool_scale=dscale[0], dn_a_log=dalog.reshape(DN_H, DN_E).sum(-1),
                 dn_dt_bias=ddtb.reshape(DN_H, DN_E).sum(-1),
                 dn_out_norm=donorm.reshape(DN_H, DN_E).sum(0), dn_conv_w=dconv)
    return dx, big, small, carried


def small_operands(l, pool_w, pool_scale, dn_out_norm, dn_a_log, dn_dt_bias, ffn1_norm, mix_norm, ffn2_norm):
    return dict(
        wbd=_block_diag(pool_w[l]).astype(BF16),
        scale=pool_scale[l][None, :],
        onorm=jnp.tile(dn_out_norm[l], DN_H)[None, :],
        alog=_lanes(dn_a_log[l], DN_E),
        dtb=_lanes(dn_dt_bias[l], DN_E),
        n1=ffn1_norm[l][None, :], nm=mix_norm[l][None, :], n2=ffn2_norm[l][None, :])


def set_mixer_weights(p, win_g, wout_g, conv_g):
    p["wext"] = build_wext(win_g.transpose(1, 0, 2).reshape(D_MODEL, IN_W))
    p["wout"] = wout_g.reshape(D_MODEL, D_MODEL)
    p["conv"] = conv_g.transpose(1, 0, 2).reshape(DN_CONV, 3 * DN_W)


def rope_tables(pos):
    inv_freq = 10000.0 ** (-jnp.arange(0, ATT_E, 2, dtype=F32) / ATT_E)
    ang = pos.astype(F32)[:, None] * inv_freq
    return jnp.tile(jnp.cos(ang), (1, 8)), jnp.tile(jnp.sin(ang), (1, 8))


def head_sum_matrix():
    return jnp.kron(jnp.eye(4, dtype=F32), jnp.ones((ATT_E, ATT_E), F32))


SMALL_NAMES = ("ffn1_norm", "mix_norm", "ffn2_norm", "pool_w", "pool_scale", "dn_a_log", "dn_dt_bias",
               "dn_out_norm", "final_norm", "dn_conv_w")


def _pack(parts):
    flat = jnp.concatenate([p.reshape(-1) for p in parts])
    n = flat.shape[0]
    rows = -(-n // 1024) * 8
    return jnp.pad(flat, (0, rows * 128 - n)).reshape(rows, 128)


def _unpack(packed, shapes):
    flat = packed.reshape(-1)
    out, off = [], 0
    for s in shapes:
        n = math.prod(s)
        out.append(flat[off:off + n].reshape(s))
        off += n
    return out


def kernel(x, positions, ffn1_norm, ffn1_w_gate, ffn1_w_up, ffn1_w_down, mix_norm, w_in, pool_w, pool_scale, dn_conv_w, dn_a_log, dn_dt_bias, dn_out_norm, w_out, ffn2_norm, ffn2_w_gate, ffn2_w_up, ffn2_w_down, final_norm, loss_target, m_ffn1_norm, m_ffn1_w_gate, m_ffn1_w_up, m_ffn1_w_down, m_mix_norm, m_w_in, m_pool_w, m_pool_scale, m_dn_conv_w, m_dn_a_log, m_dn_dt_bias, m_dn_out_norm, m_w_out, m_ffn2_norm, m_ffn2_w_gate, m_ffn2_w_up, m_ffn2_w_down, m_final_norm, v_ffn1_norm, v_ffn1_w_gate, v_ffn1_w_up, v_ffn1_w_down, v_mix_norm, v_w_in, v_pool_w, v_pool_scale, v_dn_conv_w, v_dn_a_log, v_dn_dt_bias, v_dn_out_norm, v_w_out, v_ffn2_norm, v_ffn2_w_gate, v_ffn2_w_up, v_ffn2_w_down, v_final_norm):
    me = 4 * lax.axis_index("x") + 2 * lax.axis_index("y") + lax.axis_index("c")
    x0 = x[0]
    target = loss_target[0]

    cos, sin = rope_tables(positions[0])
    headsum = head_sum_matrix()

    layers = [small_operands(l, pool_w, pool_scale, dn_out_norm, dn_a_log, dn_dt_bias, ffn1_norm, mix_norm, ffn2_norm)
              for l in range(DEPTH)]

    def ffn_shards(gate, up, down, l):
        return [jnp.stack([gate[l], up[l]]).astype(BF16), down[l].astype(BF16)]

    def gather_ffn1(l):
        def on_done(xo):
            layers[l]["f1gu"], layers[l]["f1d"] = xo
        return Exchange(ffn_shards(ffn1_w_gate, ffn1_w_up, ffn1_w_down, l), "gather"), on_done

    def gather_ffn2(l):
        def on_done(xo):
            layers[l]["f2gu"], layers[l]["f2d"] = xo
        return Exchange(ffn_shards(ffn2_w_gate, ffn2_w_up, ffn2_w_down, l), "gather"), on_done

    def gather_mixer(l):
        def on_done(xo):
            set_mixer_weights(layers[l], *xo)
        return Exchange([w_in[l].astype(BF16), w_out[l].astype(BF16), dn_conv_w[l]], "gather"), on_done

    first, on_first = gather_ffn1(0)
    on_first(run_exchange(first, "gather_ffn1_0"))
    saved = []
    xa = x0
    for l in range(DEPTH):
        host = {"ffn1": gather_mixer(l), "dn": gather_ffn2(l)}
        if l + 1 < DEPTH:
            host["ffn2"] = gather_ffn1(l + 1)
        xa, s = layer_fwd(layers[l], xa, cos, sin, l, host)
        saved.append(s)

    loss_row, dx, d_final = loss_head(xa, final_norm[None, :], target, "loss_head")
    loss = lax.psum(loss_row[0, 0], ("x", "y", "c"))

    small = {}
    big_parts = [None] * DEPTH
    carry = None
    for l in reversed(range(DEPTH)):
        dx, big, small[l], carried = layer_bwd(layers[l], saved[l], dx, cos, sin, headsum, l, True, carry)
        if carried is not None:
            big_parts[l + 1]["f1"] = list(carried)
        big_parts[l] = big
        carry = Exchange(big["f1"], "scatter")
    big_parts[0]["f1"] = list(run_exchange(carry, "scatter_ffn1_0"))
    grad_x = dx[None]

    small_shapes = {"ffn1_norm": (DEPTH, D_MODEL), "mix_norm": (DEPTH, D_MODEL), "ffn2_norm": (DEPTH, D_MODEL),
                    "pool_w": (DEPTH, 4, 64, 64), "pool_scale": (DEPTH, POOL_W), "dn_a_log": (DEPTH, DN_H),
                    "dn_dt_bias": (DEPTH, DN_H), "dn_out_norm": (DEPTH, DN_E), "final_norm": (D_MODEL,),
                    "dn_conv_w": (DEPTH, DN_CONV, 3 * DN_W)}
    g_small = {n: (d_final[0] if n == "final_norm" else jnp.stack([small[l][n] for l in range(DEPTH)]))
               for n in SMALL_NAMES}
    (small_parts,) = run_exchange(Exchange([_pack([g_small[n] for n in SMALL_NAMES])], "gather"), "gather_small_grads")

    def conv_full(a):
        return lax.dynamic_update_slice(jnp.zeros((DEPTH, DN_CONV, 3 * DN_W), F32), a, (0, 0, me * (3 * DN_W // N_DEV)))

    given = dict(ffn1_norm=(ffn1_norm, m_ffn1_norm, v_ffn1_norm), mix_norm=(mix_norm, m_mix_norm, v_mix_norm),
                 ffn2_norm=(ffn2_norm, m_ffn2_norm, v_ffn2_norm), pool_w=(pool_w, m_pool_w, v_pool_w),
                 pool_scale=(pool_scale, m_pool_scale, v_pool_scale), dn_a_log=(dn_a_log, m_dn_a_log, v_dn_a_log),
                 dn_dt_bias=(dn_dt_bias, m_dn_dt_bias, v_dn_dt_bias),
                 dn_out_norm=(dn_out_norm, m_dn_out_norm, v_dn_out_norm),
                 final_norm=(final_norm, m_final_norm, v_final_norm),
                 dn_conv_w=(conv_full(dn_conv_w), conv_full(m_dn_conv_w), conv_full(v_dn_conv_w)))
    packed_wmv = [_pack([given[n][k] for n in SMALL_NAMES]) for k in range(3)]
    small_out = adam_small(small_parts, *packed_wmv, "adam_small")
    shapes = [small_shapes[n] for n in SMALL_NAMES]
    small_res = {n: [] for n in SMALL_NAMES}
    for arr in small_out:
        for n, v_ in zip(SMALL_NAMES, _unpack(arr, shapes)):
            if n == "dn_conv_w":
                v_ = lax.dynamic_slice(v_, (0, 0, me * (3 * DN_W // N_DEV)), (DEPTH, DN_CONV, 3 * DN_W // N_DEV))
            small_res[n].append(v_)

    def parts_of(group, idx):
        return [big_parts[l][group][idx] for l in range(DEPTH)]

    big_res = dict(
        ffn1_w_gate=adam_shard(*parts_of("f1", 0), ffn1_w_gate, m_ffn1_w_gate, v_ffn1_w_gate, "adam_ffn1_gate", 0),
        ffn1_w_up=adam_shard(*parts_of("f1", 0), ffn1_w_up, m_ffn1_w_up, v_ffn1_w_up, "adam_ffn1_up", 1),
        ffn1_w_down=adam_shard(*parts_of("f1", 1), ffn1_w_down, m_ffn1_w_down, v_ffn1_w_down, "adam_ffn1_down"),
        ffn2_w_gate=adam_shard(*parts_of("f2", 0), ffn2_w_gate, m_ffn2_w_gate, v_ffn2_w_gate, "adam_ffn2_gate", 0),
        ffn2_w_up=adam_shard(*parts_of("f2", 0), ffn2_w_up, m_ffn2_w_up, v_ffn2_w_up, "adam_ffn2_up", 1),
        ffn2_w_down=adam_shard(*parts_of("f2", 1), ffn2_w_down, m_ffn2_w_down, v_ffn2_w_down, "adam_ffn2_down"),
        w_in=adam_shard(*parts_of("io", 0), w_in, m_w_in, v_w_in, "adam_w_in"),
        w_out=adam_shard(*parts_of("io", 1), w_out, m_w_out, v_w_out, "adam_w_out"),
    )

    order = ("ffn1_norm", "ffn1_w_gate", "ffn1_w_up", "ffn1_w_down", "mix_norm", "w_in", "pool_w", "pool_scale",
             "dn_conv_w", "dn_a_log", "dn_dt_bias", "dn_out_norm", "w_out", "ffn2_norm", "ffn2_w_gate", "ffn2_w_up",
             "ffn2_w_down", "final_norm")
    res = {**small_res, **big_res}
    outs = [loss, grad_x]
    for k in range(4):
        outs.extend(res[n][k] for n in order)
    return tuple(outs)
```

```python
import functools
import math

import jax
import jax.numpy as jnp
from jax import lax
from jax.experimental import pallas as pl
from jax.experimental.pallas import tpu as pltpu

F32 = jnp.float32
BF16 = jnp.bfloat16
HI = lax.Precision.HIGHEST
SDS = jax.ShapeDtypeStruct

N_DEV = 8
SEQ = 4096
D_MODEL = 1024
DEPTH = 2
D_FF = 2816
FF_BLK = D_FF // N_DEV
ATT_W = 256
ATT_E = 64
ATT_BLK = 128
DILATIONS = (1, 4, 16)
POOL_W = 256
POOL_HALO = 16
DN_W = 512
DN_H = 4
DN_E = 128
DN_C = 64
N_CHUNK = SEQ // DN_C
IN_W = 3080
IN_BLK = IN_W // N_DEV
EPS = 1e-6
EXT_ATT = 1280
EXT_REST = 3328
EXT_W = EXT_ATT + EXT_REST
R_DQKV, R_DZ, R_BB, R_AB, R_PU = 0, 1536, 2048, 2560, 3072

ADAM_LR, ADAM_B1, ADAM_B2, ADAM_EPS, ADAM_WD, ADAM_STEP = 0.001, 0.9, 0.999, 1e-08, 0.01, 10

VMEM_LIMIT = 60 * 1024 * 1024
MESH = pl.DeviceIdType.MESH


def _cparams(sem=None):
    kw = dict(vmem_limit_bytes=VMEM_LIMIT)
    if sem is not None:
        kw["dimension_semantics"] = sem
    return pltpu.CompilerParams(**kw)


def _dot(a, b, prec=None):
    return jnp.dot(a, b, preferred_element_type=F32, precision=prec)


def _dot_nt(a, b, prec=None):
    return lax.dot_general(a, b, (((1,), (1,)), ((), ())), preferred_element_type=F32, precision=prec)


def _dot_tn(a, b, prec=None):
    return lax.dot_general(a, b, (((0,), (0,)), ((), ())), preferred_element_type=F32, precision=prec)


def _sigmoid(x):
    return jax.nn.sigmoid(x)


def _rms_stats(x):
    r = lax.rsqrt(jnp.mean(x * x, axis=-1, keepdims=True) + EPS)
    return x * r, r


def _rms_bwd(xh, r, w, dh):
    dxh = dh * w
    dx = r * (dxh - xh * jnp.mean(dxh * xh, axis=-1, keepdims=True))
    return dx, jnp.sum(dh * xh, axis=0, keepdims=True)


FFN_T_FWD = 1024
FFN_T_BWD = 512
FF_TILE = 256
N_FF_TILE = D_FF // FF_TILE


def ffn_weights_from_shards(wgu_g, wd_g):
    return wgu_g.transpose(1, 2, 0, 3).reshape(2, D_MODEL, D_FF), wd_g.reshape(D_FF, D_MODEL)


def ffn_grads_to_shards(dwgu, dwd):
    return dwgu.reshape(2, D_MODEL, N_DEV, FF_BLK).transpose(2, 0, 1, 3), dwd.reshape(N_DEV, FF_BLK, D_MODEL)


def ffn_fwd(x, nw, wgu, wd, name, exch=None):
    t = FFN_T_FWD

    def body(x_ref, nw_ref, wgu_ref, wd_ref, o_ref, h_scr, acc_scr):
        k = pl.program_id(1)

        @pl.when(k == 0)
        def _():
            xh, _r = _rms_stats(x_ref[...])
            h_scr[...] = (xh * nw_ref[...]).astype(BF16)
            acc_scr[...] = jnp.zeros_like(acc_scr)

        h = h_scr[...]
        hg = _dot(h, wgu_ref[0])
        hu = _dot(h, wgu_ref[1])
        a = (hg * _sigmoid(hg) * hu).astype(BF16)
        acc_scr[...] += _dot(a, wd_ref[...])

        @pl.when(k == N_FF_TILE - 1)
        def _():
            o_ref[...] = x_ref[...] + 0.5 * acc_scr[...]

    return _call(
        body, name=name, grid=(SEQ // t, N_FF_TILE),
        in_specs=[pl.BlockSpec((t, D_MODEL), lambda i, k: (i, 0)),
                  pl.BlockSpec((1, D_MODEL), lambda i, k: (0, 0)),
                  pl.BlockSpec((2, D_MODEL, FF_TILE), lambda i, k: (0, 0, k)),
                  pl.BlockSpec((FF_TILE, D_MODEL), lambda i, k: (k, 0))],
        out_specs=pl.BlockSpec((t, D_MODEL), lambda i, k: (i, 0)),
        out_shape=SDS((SEQ, D_MODEL), F32),
        scratch_shapes=[pltpu.VMEM((t, D_MODEL), BF16), pltpu.VMEM((t, D_MODEL), F32)],
        sem=("arbitrary", "arbitrary"), args=(x, nw, wgu, wd), exch=exch)


def ffn_bwd(x, dxo, nw, wgu, wd, name, exch=None):
    t = FFN_T_BWD
    nt = SEQ // t

    def body(x_ref, dxo_ref, nw_ref, wgu_ref, wd_ref, dx_ref, dwgu_ref, dwd_ref, dnw_ref,
             dh_scr, ag_scr, au_scr, ad_scr):
        k = pl.program_id(0)
        i = pl.program_id(1)
        rows = pl.ds(pl.multiple_of(i * t, t), t)
        xh, r = _rms_stats(x_ref[...])
        nw_v = nw_ref[...]
        h = (xh * nw_v).astype(BF16)
        dxo_v = dxo_ref[...]
        dy = (0.5 * dxo_v).astype(BF16)
        wg = wgu_ref[0]
        wu = wgu_ref[1]
        hg = _dot(h, wg)
        hu = _dot(h, wu)
        sg = _sigmoid(hg)
        sil = hg * sg
        a = (sil * hu).astype(BF16)
        da = _dot_nt(dy, wd_ref[...])
        dhu = (da * sil).astype(BF16)
        dhg = (da * hu * (sg * (1.0 + hg * (1.0 - sg)))).astype(BF16)
        p_d = _dot_tn(a, dy)
        p_g = _dot_tn(h, dhg)
        p_u = _dot_tn(h, dhu)
        dh = _dot_nt(dhg, wg) + _dot_nt(dhu, wu)

        @pl.when(i == 0)
        def _():
            ad_scr[...] = p_d
            ag_scr[...] = p_g
            au_scr[...] = p_u

        @pl.when(i > 0)
        def _():
            ad_scr[...] += p_d
            ag_scr[...] += p_g
            au_scr[...] += p_u

        @pl.when(i == nt - 1)
        def _():
            dwd_ref[...] = ad_scr[...].astype(BF16)
            dwgu_ref[0] = ag_scr[...].astype(BF16)
            dwgu_ref[1] = au_scr[...].astype(BF16)

        @pl.when(k == 0)
        def _():
            dh_scr[rows, :] = dh

        @pl.when(k > 0)
        def _():
            dh_scr[rows, :] += dh

        @pl.when(jnp.logical_and(k == 0, i == 0))
        def _():
            dnw_ref[...] = jnp.zeros_like(dnw_ref)

        @pl.when(k == N_FF_TILE - 1)
        def _():
            dx, dw = _rms_bwd(xh, r, nw_v, dh_scr[rows, :])
            dx_ref[...] = dxo_v + dx
            dnw_ref[...] += dw

    last = N_FF_TILE - 1
    return _call(
        body, name=name, grid=(N_FF_TILE, nt),
        in_specs=[pl.BlockSpec((t, D_MODEL), lambda k, i: (i, 0)),
                  pl.BlockSpec((t, D_MODEL), lambda k, i: (i, 0)),
                  pl.BlockSpec((1, D_MODEL), lambda k, i: (0, 0)),
                  pl.BlockSpec((2, D_MODEL, FF_TILE), lambda k, i: (0, 0, k)),
                  pl.BlockSpec((FF_TILE, D_MODEL), lambda k, i: (k, 0))],
        out_specs=[pl.BlockSpec((t, D_MODEL), lambda k, i: (jnp.where(k == last, i, 0), 0)),
                   pl.BlockSpec((2, D_MODEL, FF_TILE), lambda k, i: (0, 0, k)),
                   pl.BlockSpec((FF_TILE, D_MODEL), lambda k, i: (k, 0)),
                   pl.BlockSpec((1, D_MODEL), lambda k, i: (0, 0))],
        out_shape=[SDS((SEQ, D_MODEL), F32), SDS((2, D_MODEL, D_FF), BF16),
                   SDS((D_FF, D_MODEL), BF16), SDS((1, D_MODEL), F32)],
        scratch_shapes=[pltpu.VMEM((SEQ, D_MODEL), F32), pltpu.VMEM((D_MODEL, FF_TILE), F32),
                        pltpu.VMEM((D_MODEL, FF_TILE), F32), pltpu.VMEM((FF_TILE, D_MODEL), F32)],
        sem=("arbitrary", "arbitrary"), args=(x, dxo, nw, wgu, wd), exch=exch)


def loss_head(x, fw, target, name):
    t = 512

    def body(x_ref, fw_ref, tg_ref, loss_ref, dx_ref, dfw_ref):
        i = pl.program_id(0)
        xh, r = _rms_stats(x_ref[...])
        w = fw_ref[...]
        err = xh * w - tg_ref[...]
        part = 0.5 * jnp.sum(jnp.sum(err * err, axis=-1, keepdims=True), axis=0, keepdims=True) / D_MODEL
        dx, dw = _rms_bwd(xh, r, w, err * (1.0 / D_MODEL))
        dx_ref[...] = dx

        @pl.when(i == 0)
        def _():
            loss_ref[...] = jnp.zeros_like(loss_ref)
            dfw_ref[...] = jnp.zeros_like(dfw_ref)

        loss_ref[...] += jnp.broadcast_to(part, loss_ref.shape)
        dfw_ref[...] += dw

    return pl.pallas_call(
        body, name=name, grid=(SEQ // t,),
        in_specs=[pl.BlockSpec((t, D_MODEL), lambda i: (i, 0)),
                  pl.BlockSpec((1, D_MODEL), lambda i: (0, 0)),
                  pl.BlockSpec((t, D_MODEL), lambda i: (i, 0))],
        out_specs=[pl.BlockSpec((1, 128), lambda i: (0, 0)),
                   pl.BlockSpec((t, D_MODEL), lambda i: (i, 0)),
                   pl.BlockSpec((1, D_MODEL), lambda i: (0, 0))],
        out_shape=[SDS((1, 128), F32), SDS((SEQ, D_MODEL), F32), SDS((1, D_MODEL), F32)],
        compiler_params=_cparams(("arbitrary",)),
    )(x, fw, target)


MIX_T = 256


def mix_in_fwd(x, nw, wext, cos, sin, name):
    t = MIX_T

    def body(x_ref, nw_ref, w_ref, cos_ref, sin_ref, att_ref, rest_ref):
        xh, _r = _rms_stats(x_ref[...])
        h = (xh * nw_ref[...]).astype(BF16)
        pa = _dot(h, w_ref[:, 0:EXT_ATT])
        c = cos_ref[...]
        s = sin_ref[...]
        att_ref[:, 0:256] = pa[:, 0:256] * c + pa[:, 768:1024] * s
        att_ref[:, 256:512] = pa[:, 256:512] * c + pa[:, 1024:1280] * s
        att_ref[:, 512:768] = pa[:, 512:768]
        for j in range(EXT_REST // 256):
            rest_ref[:, 256 * j:256 * j + 256] = _dot(h, w_ref[:, EXT_ATT + 256 * j:EXT_ATT + 256 * j + 256])

    return pl.pallas_call(
        body, name=name, grid=(SEQ // t,),
        in_specs=[pl.BlockSpec((t, D_MODEL), lambda i: (i, 0)),
                  pl.BlockSpec((1, D_MODEL), lambda i: (0, 0)),
                  pl.BlockSpec((D_MODEL, EXT_W), lambda i: (0, 0)),
                  pl.BlockSpec((t, ATT_W), lambda i: (i, 0)),
                  pl.BlockSpec((t, ATT_W), lambda i: (i, 0))],
        out_specs=[pl.BlockSpec((t, 768), lambda i: (i, 0)),
                   pl.BlockSpec((t, EXT_REST), lambda i: (i, 0))],
        out_shape=[SDS((SEQ, 768), F32), SDS((SEQ, EXT_REST), F32)],
        compiler_params=_cparams(("arbitrary",)),
    )(x, nw, wext, cos, sin)


def assemble_dproj(datts, cos, sin, d_dqkv, dz, dbb, dab, dpu, name):
    t = 512

    def body(d1_ref, d4_ref, d16_ref, cos_ref, sin_ref, dqkv_ref, dz_ref, dbb_ref, dab_ref, dpu_ref, o_ref):
        da = d1_ref[...] + d4_ref[...] + d16_ref[...]
        c = cos_ref[...]
        s = sin_ref[...]
        dq = da[:, 0:256]
        dk = da[:, 256:512]
        o_ref[:, 0:256] = (dq * c).astype(BF16)
        o_ref[:, 256:512] = (dk * c).astype(BF16)
        o_ref[:, 512:768] = da[:, 512:768].astype(BF16)
        o_ref[:, 768:1024] = (dq * s).astype(BF16)
        o_ref[:, 1024:1280] = (dk * s).astype(BF16)
        b = EXT_ATT
        o_ref[:, b + R_DQKV:b + R_DQKV + 1536] = dqkv_ref[...].astype(BF16)
        o_ref[:, b + R_DZ:b + R_DZ + 512] = dz_ref[...].astype(BF16)
        o_ref[:, b + R_BB:b + R_BB + 512] = dbb_ref[...].astype(BF16)
        o_ref[:, b + R_AB:b + R_AB + 512] = dab_ref[...].astype(BF16)
        o_ref[:, b + R_PU:b + R_PU + 256] = dpu_ref[...].astype(BF16)

    row = lambda w: pl.BlockSpec((t, w), lambda i: (i, 0))
    return pl.pallas_call(
        body, name=name, grid=(SEQ // t,),
        in_specs=[row(768), row(768), row(768), row(256), row(256), row(1536), row(512), row(512), row(512), row(256)],
        out_specs=row(EXT_W),
        out_shape=SDS((SEQ, EXT_W), BF16),
        compiler_params=_cparams(("arbitrary",)),
    )(*datts, cos, sin, d_dqkv, dz, dbb, dab, dpu)


def linear_bwd(x, dxo, nw, dy, w, name):
    t = 512
    nb = 512
    n = w.shape[1]
    nt = SEQ // t
    nn = n // nb

    def body(x_ref, dxo_ref, nw_ref, dy_ref, w_ref, dx_ref, dw_ref, dnw_ref, dh_scr):
        k = pl.program_id(0)
        i = pl.program_id(1)
        rows = pl.ds(pl.multiple_of(i * t, t), t)
        xh, r = _rms_stats(x_ref[...])
        nw_v = nw_ref[...]
        h = (xh * nw_v).astype(BF16)
        dyv = dy_ref[...]
        p_w = _dot_tn(h, dyv)
        dh = _dot_nt(dyv, w_ref[...])

        @pl.when(i == 0)
        def _():
            dw_ref[...] = p_w

        @pl.when(i > 0)
        def _():
            dw_ref[...] += p_w

        @pl.when(k == 0)
        def _():
            dh_scr[rows, :] = dh

        @pl.when(k > 0)
        def _():
            dh_scr[rows, :] += dh

        @pl.when(jnp.logical_and(k == 0, i == 0))
        def _():
            dnw_ref[...] = jnp.zeros_like(dnw_ref)

        @pl.when(k == nn - 1)
        def _():
            dx, dw = _rms_bwd(xh, r, nw_v, dh_scr[rows, :])
            dx_ref[...] = dxo_ref[...] + dx
            dnw_ref[...] += dw

    last = nn - 1
    return pl.pallas_call(
        body, name=name, grid=(nn, nt),
        in_specs=[pl.BlockSpec((t, D_MODEL), lambda k, i: (i, 0)),
                  pl.BlockSpec((t, D_MODEL), lambda k, i: (i, 0)),
                  pl.BlockSpec((1, D_MODEL), lambda k, i: (0, 0)),
                  pl.BlockSpec((t, nb), lambda k, i: (i, k)),
                  pl.BlockSpec((D_MODEL, nb), lambda k, i: (0, k))],
        out_specs=[pl.BlockSpec((t, D_MODEL), lambda k, i: (jnp.where(k == last, i, 0), 0)),
                   pl.BlockSpec((D_MODEL, nb), lambda k, i: (0, k)),
                   pl.BlockSpec((1, D_MODEL), lambda k, i: (0, 0))],
        out_shape=[SDS((SEQ, D_MODEL), F32), SDS((D_MODEL, n), F32), SDS((1, D_MODEL), F32)],
        scratch_shapes=[pltpu.VMEM((SEQ, D_MODEL), F32)],
        compiler_params=_cparams(("arbitrary", "arbitrary")),
    )(x, dxo, nw, dy, w)


def _att_masks():
    qi = lax.broadcasted_iota(jnp.int32, (ATT_BLK, ATT_BLK), 0)
    ki = lax.broadcasted_iota(jnp.int32, (ATT_BLK, ATT_BLK), 1)
    return ki <= qi, ki >= qi


NEG = -1e30


def _att_heads(ref, base):
    return jnp.stack([ref[:, base + ATT_E * hd:base + ATT_E * hd + ATT_E] for hd in range(4)], axis=0)


def att_fwd(att, blocks_per_class, name):
    nblk = SEQ // ATT_BLK

    def body(cur_ref, prev_ref, o_ref):
        i = pl.program_id(0)
        has_prev = (i % blocks_per_class) != 0
        m_d, m_p = _att_masks()
        m_p = jnp.logical_and(m_p, has_prev)
        q = _att_heads(cur_ref, 0).astype(BF16)
        kc = _att_heads(cur_ref, 256).astype(BF16)
        vc = _att_heads(cur_ref, 512).astype(BF16)
        kp = _att_heads(prev_ref, 256).astype(BF16)
        vp = _att_heads(prev_ref, 512).astype(BF16)
        sd = jnp.where(m_d, _bdot(q, kc, 2, 2) * 0.125, NEG)
        sp = jnp.where(m_p, _bdot(q, kp, 2, 2) * 0.125, NEG)
        m = jnp.maximum(jnp.max(sd, axis=-1, keepdims=True), jnp.max(sp, axis=-1, keepdims=True))
        pd = jnp.exp(sd - m)
        pp = jnp.exp(sp - m)
        den = jnp.sum(pd, axis=-1, keepdims=True) + jnp.sum(pp, axis=-1, keepdims=True)
        inv = 1.0 / den
        o = _bdot((pd * inv).astype(BF16), vc, 2, 1) + _bdot((pp * inv).astype(BF16), vp, 2, 1)
        lse = m + jnp.log(den)
        for hd in range(4):
            o_ref[:, ATT_E * hd:ATT_E * hd + ATT_E] = o[hd]
            o_ref[:, 256 + ATT_E * hd:256 + ATT_E * hd + ATT_E] = jnp.broadcast_to(lse[hd], (ATT_BLK, ATT_E))

    return pl.pallas_call(
        body, name=name, grid=(nblk,),
        in_specs=[pl.BlockSpec((ATT_BLK, 768), lambda i: (i, 0)),
                  pl.BlockSpec((ATT_BLK, 768), lambda i: (jnp.maximum(i - 1, 0), 0))],
        out_specs=pl.BlockSpec((ATT_BLK, 512), lambda i: (i, 0)),
        out_shape=SDS((SEQ, 512), F32),
        compiler_params=_cparams(("arbitrary",)),
    )(att, att)


def att_bwd(att, ol, dol, blocks_per_class, name):
    nblk = SEQ // ATT_BLK

    def body(prev_ref, cur_ref, nxt_ref, ol_c_ref, ol_n_ref, dol_c_ref, dol_n_ref, d_ref):
        i = pl.program_id(0)
        has_prev = (i % blocks_per_class) != 0
        has_next = ((i + 1) % blocks_per_class) != 0
        m_d, m_band = _att_masks()
        m_p = jnp.logical_and(m_band, has_prev)
        m_n = jnp.logical_and(m_band, has_next)

        def pair(q, k, v, lse, do, dterm, mask):
            s = jnp.where(mask, _bdot(q, k, 2, 2) * 0.125, NEG)
            p = jnp.exp(s - lse)
            dp = _bdot(do, v, 2, 2)
            ds = (p * (dp + dterm) * 0.125).astype(BF16)
            return p.astype(BF16), ds

        q_c = _att_heads(cur_ref, 0).astype(BF16)
        k_c = _att_heads(cur_ref, 256).astype(BF16)
        v_c = _att_heads(cur_ref, 512).astype(BF16)
        k_p = _att_heads(prev_ref, 256).astype(BF16)
        v_p = _att_heads(prev_ref, 512).astype(BF16)
        q_n = _att_heads(nxt_ref, 0).astype(BF16)
        o_c = _att_heads(ol_c_ref, 0)
        o_n = _att_heads(ol_n_ref, 0)
        lse_c = _att_heads(ol_c_ref, 256)[:, :, 0:1]
        lse_n = _att_heads(ol_n_ref, 256)[:, :, 0:1]
        do_c = _att_heads(dol_c_ref, 0)
        do_n = _att_heads(dol_n_ref, 0)
        t_c = _att_heads(dol_c_ref, 256)[:, :, 0:1] - jnp.sum(do_c * o_c, axis=-1, keepdims=True)
        t_n = _att_heads(dol_n_ref, 256)[:, :, 0:1] - jnp.sum(do_n * o_n, axis=-1, keepdims=True)
        do_cb = do_c.astype(BF16)
        do_nb = do_n.astype(BF16)
        p1, ds1 = pair(q_c, k_c, v_c, lse_c, do_cb, t_c, m_d)
        _p2, ds2 = pair(q_c, k_p, v_p, lse_c, do_cb, t_c, m_p)
        p3, ds3 = pair(q_n, k_c, v_c, lse_n, do_nb, t_n, m_n)
        dq = _bdot(ds1, k_c, 2, 1) + _bdot(ds2, k_p, 2, 1)
        dk = _bdot(ds1, q_c, 1, 1) + _bdot(ds3, q_n, 1, 1)
        dv = _bdot(p1, do_cb, 1, 1) + _bdot(p3, do_nb, 1, 1)
        for hd in range(4):
            a = ATT_E * hd
            d_ref[:, a:a + ATT_E] = dq[hd]
            d_ref[:, 256 + a:256 + a + ATT_E] = dk[hd]
            d_ref[:, 512 + a:512 + a + ATT_E] = dv[hd]

    prv = lambda i: (jnp.maximum(i - 1, 0), 0)
    cur = lambda i: (i, 0)
    nxt = lambda i: (jnp.minimum(i + 1, nblk - 1), 0)
    return pl.pallas_call(
        body, name=name, grid=(nblk,),
        in_specs=[pl.BlockSpec((ATT_BLK, 768), prv), pl.BlockSpec((ATT_BLK, 768), cur),
                  pl.BlockSpec((ATT_BLK, 768), nxt),
                  pl.BlockSpec((ATT_BLK, 512), cur), pl.BlockSpec((ATT_BLK, 512), nxt),
                  pl.BlockSpec((ATT_BLK, 512), cur), pl.BlockSpec((ATT_BLK, 512), nxt)],
        out_specs=pl.BlockSpec((ATT_BLK, 768), cur),
        out_shape=SDS((SEQ, 768), F32),
        compiler_params=_cparams(("arbitrary",)),
    )(att, att, att, ol, ol, dol, dol)


def to_classes(a, d):
    if d == 1:
        return a
    w = a.shape[1]
    return a.reshape(SEQ // d, d, w).transpose(1, 0, 2).reshape(SEQ, w)


def from_classes(a, d):
    if d == 1:
        return a
    w = a.shape[1]
    return a.reshape(d, SEQ // d, w).transpose(1, 0, 2).reshape(SEQ, w)


def _shift_down(x, k):
    rows = lax.broadcasted_iota(jnp.int32, x.shape, 0)
    return jnp.where(rows >= k, pltpu.roll(x, k, 0), 0.0)


def _shift_up(x, k):
    n = x.shape[0]
    rows = lax.broadcasted_iota(jnp.int32, x.shape, 0)
    return jnp.where(rows < n - k, pltpu.roll(x, n - k, 0), 0.0)


@functools.partial(jax.custom_vjp, nondiff_argnums=(1,))
def _delay(x, k):
    return _shift_down(x, k)


def _delay_fwd(x, k):
    return _shift_down(x, k), None


def _delay_bwd(k, _res, g):
    return (_shift_up(g, k),)


_delay.defvjp(_delay_fwd, _delay_bwd)

DN_CONV = 4


def _dn_prep_fn(u, w, kind):
    y = w[DN_CONV - 1:DN_CONV] * u
    for j in range(DN_CONV - 1):
        y = y + w[j:j + 1] * _delay(u, DN_CONV - 1 - j)
    y = y * _sigmoid(y)
    nrm = y * lax.rsqrt(jnp.sum(y * y, axis=-1, keepdims=True) + EPS)
    return jnp.where(kind == 0, nrm * (DN_E ** -0.5), jnp.where(kind == 1, nrm, y))


def dn_prep_fwd(rest, conv_w, name):
    def body(u_ref, w_ref, o_ref):
        j = pl.program_id(0)
        kind = (j >= DN_H).astype(jnp.int32) + (j >= 2 * DN_H).astype(jnp.int32)
        o_ref[...] = _dn_prep_fn(u_ref[...], w_ref[...], kind)

    return pl.pallas_call(
        body, name=name, grid=(3 * DN_H,),
        in_specs=[pl.BlockSpec((SEQ, DN_E), lambda j: (0, j)),
                  pl.BlockSpec((DN_CONV, DN_E), lambda j: (0, j))],
        out_specs=pl.BlockSpec((SEQ, DN_E), lambda j: (0, j)),
        out_shape=SDS((SEQ, 3 * DN_W), F32),
        compiler_params=_cparams(("arbitrary",)),
    )(rest, conv_w)


def dn_prep_bwd(rest, conv_w, dqkv, name):
    def body(u_ref, w_ref, g_ref, du_ref, dw_ref):
        j = pl.program_id(0)
        kind = (j >= DN_H).astype(jnp.int32) + (j >= 2 * DN_H).astype(jnp.int32)
        _y, vjp = jax.vjp(lambda u, w: _dn_prep_fn(u, w, kind), u_ref[...], w_ref[...])
        du, dw = vjp(g_ref[...])
        du_ref[...] = du
        dw_ref[...] = dw

    return pl.pallas_call(
        body, name=name, grid=(3 * DN_H,),
        in_specs=[pl.BlockSpec((SEQ, DN_E), lambda j: (0, j)),
                  pl.BlockSpec((DN_CONV, DN_E), lambda j: (0, j)),
                  pl.BlockSpec((SEQ, DN_E), lambda j: (0, j))],
        out_specs=[pl.BlockSpec((SEQ, DN_E), lambda j: (0, j)),
                   pl.BlockSpec((DN_CONV, DN_E), lambda j: (0, j))],
        out_shape=[SDS((SEQ, 3 * DN_W), F32), SDS((DN_CONV, 3 * DN_W), F32)],
        compiler_params=_cparams(("arbitrary",)),
    )(rest, conv_w, dqkv)


def _bdot(a, b, ca, cb, prec=None):
    return lax.dot_general(a, b, (((ca,), (cb,)), ((0,), (0,))), preferred_element_type=F32, precision=prec)


def _unit_lower_inverse(a):
    eye = (lax.broadcasted_iota(jnp.int32, (DN_C, DN_C), 0) == lax.broadcasted_iota(jnp.int32, (DN_C, DN_C), 1)).astype(F32)
    p = eye - a
    b = _bdot(a, a, 2, 1, HI)
    for lvl in range(5):
        p = p + _bdot(p, b, 2, 1, HI)
        if lvl < 4:
            b = _bdot(b, b, 2, 1, HI)
    return p


@jax.custom_vjp
def _tri_inv(a):
    return _unit_lower_inverse(a)


def _tri_inv_fwd(a):
    t = _unit_lower_inverse(a)
    return t, t


def _tri_inv_bwd(t, g):
    return (-_bdot(_bdot(t, g, 1, 1, HI), t, 2, 2, HI),)


_tri_inv.defvjp(_tri_inv_fwd, _tri_inv_bwd)


def _b16(x):
    return x.astype(BF16)


def _dn_chunk(q, k, v, bb, ab, alog, dtb, state):
    ri = lax.broadcasted_iota(jnp.int32, (DN_C, DN_C), 0)
    ci = lax.broadcasted_iota(jnp.int32, (DN_C, DN_C), 1)
    lower = ri >= ci
    strict = ri > ci
    nh = q.shape[0]
    beta = _sigmoid(bb)
    xg = ab + dtb
    softplus = jnp.maximum(xg, 0.0) + jnp.log(1.0 + jnp.exp(-jnp.abs(xg)))
    gi = -jnp.exp(alog) * softplus
    g = _bdot(jnp.broadcast_to(lower.astype(F32), (nh, DN_C, DN_C)), gi, 2, 1, HI)
    eg = jnp.exp(g)
    kb = k * beta
    vb = v * beta
    g_col = g[:, :, 0:DN_C]
    g_row = _bdot(jnp.full((nh, DN_C, DN_E), 1.0 / DN_E, F32), g, 2, 2, HI)
    decay = jnp.where(lower, jnp.exp(jnp.where(lower, g_col - g_row, 0.0)), 0.0)
    kbf = _b16(k)
    a = jnp.where(strict, _bdot(_b16(kb), kbf, 2, 2) * decay, 0.0)
    t = _tri_inv(a)
    tb = _b16(t)
    u = _bdot(tb, _b16(vb), 2, 1)
    w = _bdot(tb, _b16(kb * eg), 2, 1)
    intra = jnp.where(lower, _bdot(_b16(q), kbf, 2, 2) * decay, 0.0)
    sb = _b16(state)
    v_new = u - _bdot(_b16(w), sb, 2, 1)
    o = _bdot(_b16(q * eg), sb, 2, 1) + _bdot(_b16(intra), _b16(v_new), 2, 1)
    g_last = g[:, DN_C - 1:DN_C, :]
    k_dec = k * jnp.exp(g_last - g)
    new_state = state * jnp.exp(g_last) + _bdot(_b16(k_dec), _b16(v_new), 1, 1)
    return o, new_state


def _heads(ref, base=0):
    return jnp.stack([ref[:, base + DN_E * hd:base + DN_E * hd + DN_E] for hd in range(DN_H)], axis=0)


def _put_heads(ref, val, base=0):
    for hd in range(DN_H):
        ref[:, base + DN_E * hd:base + DN_E * hd + DN_E] = val[hd]


def _dn_args(qkv_ref, bb_ref, ab_ref, alog_ref, dtb_ref, state):
    return (_heads(qkv_ref), _heads(qkv_ref, DN_W), _heads(qkv_ref, 2 * DN_W), _heads(bb_ref), _heads(ab_ref),
            _heads(alog_ref), _heads(dtb_ref), state)


def dn_chunk_fwd(qkv, rest, alog_b, dtb_b, name, exch=None):
    def body(qkv_ref, bb_ref, ab_ref, alog_ref, dtb_ref, o_ref, st_ref, state_scr):
        n = pl.program_id(0)

        @pl.when(n == 0)
        def _():
            state_scr[...] = jnp.zeros_like(state_scr)

        st = state_scr[...]
        st_ref[0] = st
        o, ns = _dn_chunk(*_dn_args(qkv_ref, bb_ref, ab_ref, alog_ref, dtb_ref, st))
        _put_heads(o_ref, o)
        state_scr[...] = ns

    return _call(
        body, name=name, grid=(N_CHUNK,),
        in_specs=[pl.BlockSpec((DN_C, 3 * DN_W), lambda n: (n, 0)),
                  pl.BlockSpec((DN_C, DN_W), lambda n: (n, R_BB // DN_W)),
                  pl.BlockSpec((DN_C, DN_W), lambda n: (n, R_AB // DN_W)),
                  pl.BlockSpec((1, DN_W), lambda n: (0, 0)),
                  pl.BlockSpec((1, DN_W), lambda n: (0, 0))],
        out_specs=[pl.BlockSpec((DN_C, DN_W), lambda n: (n, 0)),
                   pl.BlockSpec((1, DN_H, DN_E, DN_E), lambda n: (n, 0, 0, 0))],
        out_shape=[SDS((SEQ, DN_W), F32), SDS((N_CHUNK, DN_H, DN_E, DN_E), F32)],
        scratch_shapes=[pltpu.VMEM((DN_H, DN_E, DN_E), F32)],
        sem=("arbitrary",), args=(qkv, rest, rest, alog_b, dtb_b), exch=exch)


def dn_chunk_bwd(qkv, rest, alog_b, dtb_b, states, do, name, exch=None):
    last = N_CHUNK - 1

    def body(qkv_ref, bb_ref, ab_ref, alog_ref, dtb_ref, st_ref, do_ref,
             dqkv_ref, dbb_ref, dab_ref, dalog_ref, ddtb_ref, dstate_scr):
        s = pl.program_id(0)

        @pl.when(s == 0)
        def _():
            dstate_scr[...] = jnp.zeros_like(dstate_scr)
            dalog_ref[...] = jnp.zeros_like(dalog_ref)
            ddtb_ref[...] = jnp.zeros_like(ddtb_ref)

        _out, vjp = jax.vjp(_dn_chunk, *_dn_args(qkv_ref, bb_ref, ab_ref, alog_ref, dtb_ref, st_ref[0]))
        dq, dk, dv, dbb, dab, dalog, ddtb, dst = vjp((_heads(do_ref), dstate_scr[...]))
        _put_heads(dqkv_ref, dq)
        _put_heads(dqkv_ref, dk, DN_W)
        _put_heads(dqkv_ref, dv, 2 * DN_W)
        _put_heads(dbb_ref, dbb)
        _put_heads(dab_ref, dab)
        _put_heads(dalog_ref, _heads(dalog_ref) + dalog)
        _put_heads(ddtb_ref, _heads(ddtb_ref) + ddtb)
        dstate_scr[...] = dst

    rev = lambda w: (lambda s: (last - s, w))
    return _call(
        body, name=name, grid=(N_CHUNK,),
        in_specs=[pl.BlockSpec((DN_C, 3 * DN_W), rev(0)),
                  pl.BlockSpec((DN_C, DN_W), rev(R_BB // DN_W)),
                  pl.BlockSpec((DN_C, DN_W), rev(R_AB // DN_W)),
                  pl.BlockSpec((1, DN_W), lambda s: (0, 0)),
                  pl.BlockSpec((1, DN_W), lambda s: (0, 0)),
                  pl.BlockSpec((1, DN_H, DN_E, DN_E), lambda s: (last - s, 0, 0, 0)),
                  pl.BlockSpec((DN_C, DN_W), rev(0))],
        out_specs=[pl.BlockSpec((DN_C, 3 * DN_W), rev(0)),
                   pl.BlockSpec((DN_C, DN_W), rev(0)),
                   pl.BlockSpec((DN_C, DN_W), rev(0)),
                   pl.BlockSpec((1, DN_W), lambda s: (0, 0)),
                   pl.BlockSpec((1, DN_W), lambda s: (0, 0))],
        out_shape=[SDS((SEQ, 3 * DN_W), F32), SDS((SEQ, DN_W), F32), SDS((SEQ, DN_W), F32),
                   SDS((1, DN_W), F32), SDS((1, DN_W), F32)],
        scratch_shapes=[pltpu.VMEM((DN_H, DN_E, DN_E), F32)],
        sem=("arbitrary",), args=(qkv, rest, rest, alog_b, dtb_b, states, do), exch=exch)


OUT_T = 256


def _pool_consts(rows_total, t0, halo_before):
    lane = lax.broadcasted_iota(jnp.int32, (rows_total, POOL_W), 1)
    row = lax.broadcasted_iota(jnp.int32, (rows_total, POOL_W), 0)
    grp = (lane >= 64).astype(jnp.int32) + (lane >= 128).astype(jnp.int32) + (lane >= 192).astype(jnp.int32)
    win = jnp.where(grp == 0, 2, jnp.where(grp == 1, 4, jnp.where(grp == 2, 8, 16)))
    pos = t0 + row - halo_before
    cnt = jnp.minimum(pos + 1, win).astype(F32)
    return grp, cnt


def _pool_select(grp, s2, s4, s8, s16):
    return jnp.where(grp == 0, s2, jnp.where(grp == 1, s4, jnp.where(grp == 2, s8, s16)))


def _pooled(u_ext, t0):
    n = u_ext.shape[0]
    grp, cnt = _pool_consts(n, t0, POOL_HALO)
    s2 = u_ext + pltpu.roll(u_ext, 1, 0)
    s4 = s2 + pltpu.roll(s2, 2, 0)
    s8 = s4 + pltpu.roll(s4, 4, 0)
    s16 = s8 + pltpu.roll(s8, 8, 0)
    out = _pool_select(grp, s2, s4, s8, s16) / jnp.maximum(cnt, 1.0) - u_ext
    return out[POOL_HALO:, :]


def _merge_weights(l1, l4, l16):
    m = jnp.maximum(jnp.maximum(l1, l4), l16)
    e1 = jnp.exp(l1 - m)
    e4 = jnp.exp(l4 - m)
    e16 = jnp.exp(l16 - m)
    inv = 1.0 / (e1 + e4 + e16)
    return e1 * inv, e4 * inv, e16 * inv


def _out_parts(ol1_ref, ol4_ref, ol16_ref, pu_ref, puh_ref, odn_ref, z_ref, wbd_ref, i, t):
    w1, w4, w16 = _merge_weights(ol1_ref[:, 256:512], ol4_ref[:, 256:512], ol16_ref[:, 256:512])
    ya = w1 * ol1_ref[:, 0:256] + w4 * ol4_ref[:, 0:256] + w16 * ol16_ref[:, 0:256]
    halo = jnp.where(i > 0, puh_ref[...], 0.0)
    pooled = _pooled(jnp.concatenate([halo, pu_ref[...]], axis=0), i * t)
    pw = _dot(pooled.astype(BF16), wbd_ref[...])
    return ya, pooled, pw, (w1, w4, w16)


def _out_specs_common(t):
    def row(w, cb=0):
        return pl.BlockSpec((t, w), lambda i: (i, cb))

    halo = pl.BlockSpec((POOL_HALO, POOL_W),
                        lambda i: (jnp.maximum(i * (t // POOL_HALO) - 1, 0), R_PU // POOL_W))
    full = lambda a, b: pl.BlockSpec((a, b), lambda i: (0, 0))
    return [row(512), row(512), row(512), row(POOL_W, R_PU // POOL_W), halo, row(DN_W), row(DN_W, R_DZ // DN_W),
            full(POOL_W, POOL_W), full(1, POOL_W), full(1, DN_W), full(D_MODEL, D_MODEL)]


def mix_out_fwd(x, ol1, ol4, ol16, rest, odn, wbd, scale, onorm_b, wout, name):
    t = OUT_T

    def body(x_ref, ol1_ref, ol4_ref, ol16_ref, pu_ref, puh_ref, odn_ref, z_ref, wbd_ref, sc_ref, on_ref, wo_ref, o_ref):
        i = pl.program_id(0)
        ya, _pooled_v, pw, _w = _out_parts(ol1_ref, ol4_ref, ol16_ref, pu_ref, puh_ref, odn_ref, z_ref, wbd_ref, i, t)
        yb = pw * sc_ref[...]
        acc = x_ref[...] + _dot(ya.astype(BF16), wo_ref[0:256, :]) + _dot(yb.astype(BF16), wo_ref[256:512, :])
        for hd in range(DN_H):
            sl = slice(DN_E * hd, DN_E * hd + DN_E)
            oh, _r = _rms_stats(odn_ref[:, sl])
            z = z_ref[:, sl]
            yc = oh * on_ref[:, sl] * (z * _sigmoid(z))
            acc = acc + _dot(yc.astype(BF16), wo_ref[512 + DN_E * hd:512 + DN_E * hd + DN_E, :])
        o_ref[...] = acc

    return pl.pallas_call(
        body, name=name, grid=(SEQ // t,),
        in_specs=[pl.BlockSpec((t, D_MODEL), lambda i: (i, 0))] + _out_specs_common(t),
        out_specs=pl.BlockSpec((t, D_MODEL), lambda i: (i, 0)),
        out_shape=SDS((SEQ, D_MODEL), F32),
        compiler_params=_cparams(("arbitrary",)),
    )(x, ol1, ol4, ol16, rest, rest, odn, rest, wbd, scale, onorm_b, wout)


def mix_out_bwd(dxo, ol1, ol4, ol16, rest, odn, wbd, scale, onorm_b, wout, headsum, name):
    t = OUT_T

    def body(dxo_ref, ol1_ref, ol4_ref, ol16_ref, pu_ref, puh_ref, odn_ref, z_ref, wbd_ref, sc_ref, on_ref, wo_ref, hs_ref,
             dwo_ref, d1_ref, d4_ref, d16_ref, dpl_ref, dodn_ref, dz_ref, dsc_ref, don_ref, dwbd_ref):
        i = pl.program_id(0)

        @pl.when(i == 0)
        def _():
            dwo_ref[...] = jnp.zeros_like(dwo_ref)
            dsc_ref[...] = jnp.zeros_like(dsc_ref)
            don_ref[...] = jnp.zeros_like(don_ref)
            dwbd_ref[...] = jnp.zeros_like(dwbd_ref)

        ya, pooled, pw, (w1, w4, w16) = _out_parts(ol1_ref, ol4_ref, ol16_ref, pu_ref, puh_ref, odn_ref, z_ref, wbd_ref, i, t)
        sc = sc_ref[...]
        dxb = dxo_ref[...].astype(BF16)
        dwo_ref[0:256, :] += _dot_tn(ya.astype(BF16), dxb)
        dwo_ref[256:512, :] += _dot_tn((pw * sc).astype(BF16), dxb)
        dya = _dot_nt(dxb, wo_ref[0:256, :])
        o1 = ol1_ref[:, 0:256]
        o4 = ol4_ref[:, 0:256]
        o16 = ol16_ref[:, 0:256]
        hs = hs_ref[...]
        s1 = _dot(dya * o1, hs, HI)
        s4 = _dot(dya * o4, hs, HI)
        s16 = _dot(dya * o16, hs, HI)
        sbar = w1 * s1 + w4 * s4 + w16 * s16
        d1_ref[:, 0:256] = w1 * dya
        d1_ref[:, 256:512] = w1 * (s1 - sbar)
        d4_ref[:, 0:256] = w4 * dya
        d4_ref[:, 256:512] = w4 * (s4 - sbar)
        d16_ref[:, 0:256] = w16 * dya
        d16_ref[:, 256:512] = w16 * (s16 - sbar)
        dyb = _dot_nt(dxb, wo_ref[256:512, :])
        dsc_ref[...] += jnp.sum(dyb * pw, axis=0, keepdims=True)
        dpw = (dyb * sc).astype(BF16)
        dwbd_ref[...] += _dot_tn(pooled.astype(BF16), dpw)
        dpl_ref[...] = _dot_nt(dpw, wbd_ref[...])
        for hd in range(DN_H):
            sl = slice(DN_E * hd, DN_E * hd + DN_E)
            rows_w = slice(512 + DN_E * hd, 512 + DN_E * hd + DN_E)
            oh, r = _rms_stats(odn_ref[:, sl])
            z = z_ref[:, sl]
            sg = _sigmoid(z)
            sz = z * sg
            nw = on_ref[:, sl]
            on = oh * nw
            dwo_ref[rows_w, :] += _dot_tn((on * sz).astype(BF16), dxb)
            dyc = _dot_nt(dxb, wo_ref[rows_w, :])
            dz_ref[:, sl] = dyc * on * (sg * (1.0 + z * (1.0 - sg)))
            dx, dw = _rms_bwd(oh, r, nw, dyc * sz)
            dodn_ref[:, sl] = dx
            don_ref[:, sl] += dw

    row = lambda w: pl.BlockSpec((t, w), lambda i: (i, 0))
    full = lambda a, b: pl.BlockSpec((a, b), lambda i: (0, 0))
    return pl.pallas_call(
        body, name=name, grid=(SEQ // t,),
        in_specs=[row(D_MODEL)] + _out_specs_common(t) + [full(ATT_W, ATT_W)],
        out_specs=[full(D_MODEL, D_MODEL), row(512), row(512), row(512), row(POOL_W), row(DN_W), row(DN_W),
                   full(1, POOL_W), full(1, DN_W), full(POOL_W, POOL_W)],
        out_shape=[SDS((D_MODEL, D_MODEL), F32), SDS((SEQ, 512), F32), SDS((SEQ, 512), F32), SDS((SEQ, 512), F32),
                   SDS((SEQ, POOL_W), F32), SDS((SEQ, DN_W), F32), SDS((SEQ, DN_W), F32),
                   SDS((1, POOL_W), F32), SDS((1, DN_W), F32), SDS((POOL_W, POOL_W), F32)],
        compiler_params=_cparams(("arbitrary",)),
    )(dxo, ol1, ol4, ol16, rest, rest, odn, rest, wbd, scale, onorm_b, wout, headsum)


def pool_bwd(dpooled, name):
    t = 512
    nt = SEQ // t

    def body(d_ref, dn_ref, o_ref):
        i = pl.program_id(0)
        halo = jnp.where(i < nt - 1, dn_ref[...], 0.0)
        d_ext = jnp.concatenate([d_ref[...], halo], axis=0)
        n = t + POOL_HALO
        grp, cnt = _pool_consts(n, i * t, 0)
        dq = d_ext / cnt
        s2 = dq + pltpu.roll(dq, n - 1, 0)
        s4 = s2 + pltpu.roll(s2, n - 2, 0)
        s8 = s4 + pltpu.roll(s4, n - 4, 0)
        s16 = s8 + pltpu.roll(s8, n - 8, 0)
        o_ref[...] = (_pool_select(grp, s2, s4, s8, s16) - d_ext)[0:t, :]

    return pl.pallas_call(
        body, name=name, grid=(nt,),
        in_specs=[pl.BlockSpec((t, POOL_W), lambda i: (i, 0)),
                  pl.BlockSpec((POOL_HALO, POOL_W),
                               lambda i: (jnp.minimum((i + 1) * (t // POOL_HALO), SEQ // POOL_HALO - 1), 0))],
        out_specs=pl.BlockSpec((t, POOL_W), lambda i: (i, 0)),
        out_shape=SDS((SEQ, POOL_W), F32),
        compiler_params=_cparams(("arbitrary",)),
    )(dpooled, dpooled)


N_PEER = N_DEV - 1
ANY_SPEC = pl.BlockSpec(memory_space=pl.ANY)


class Exchange:
    def __init__(self, arrays, mode):
        self.arrays = list(arrays)
        self.mode = mode
        n = len(self.arrays)
        if mode == "scatter":
            self.out_shape = [SDS(a.shape, a.dtype) for a in self.arrays]
        else:
            self.out_shape = [SDS((N_DEV,) + a.shape, a.dtype) for a in self.arrays]
        self.scratch = [pltpu.SemaphoreType.DMA((n * N_PEER,)), pltpu.SemaphoreType.DMA((n * N_PEER,)),
                        pltpu.SemaphoreType.DMA((n,))]

    @staticmethod
    def _place():
        x, y, c = lax.axis_index("x"), lax.axis_index("y"), lax.axis_index("c")
        chips = [(1 - x, y), (x, 1 - y), (1 - x, 1 - y)]
        return x, y, c, chips

    @staticmethod
    def _copy(sems, a, k, src, dst, to):
        send_sems, recv_sems, _ = sems
        return pltpu.make_async_remote_copy(
            src_ref=src, dst_ref=dst, send_sem=send_sems.at[a * N_PEER + k], recv_sem=recv_sems.at[a * N_PEER + k],
            device_id=to, device_id_type=MESH)

    def _scatter_peers(self):
        x, y, c, _ = self._place()
        out = []
        for fx, fy, fc in ((0, 0, 1), (1, 0, 0), (0, 1, 0), (1, 1, 0), (1, 0, 1), (0, 1, 1), (1, 1, 1)):
            px, py, pc = x ^ fx, y ^ fy, c ^ fc
            out.append(((px, py, pc), 4 * px + 2 * py + pc))
        return 4 * x + 2 * y + c, out

    def _local(self, ins, outs, sems, a, me):
        src = ins[a].at[me] if self.mode == "scatter" else ins[a]
        return pltpu.make_async_copy(src, outs[a].at[me], sems[2].at[a])

    def start(self, ins, outs, sems):
        if self.mode == "scatter":
            me, peers = self._scatter_peers()
            for a in range(len(ins)):
                self._local(ins, outs, sems, a, me).start()
                for k, (peer, pidx) in enumerate(peers):
                    self._copy(sems, a, k, ins[a].at[pidx], outs[a].at[me], peer).start()
            return
        x, y, c, chips = self._place()
        me = 4 * x + 2 * y + c
        for a in range(len(ins)):
            self._local(ins, outs, sems, a, me).start()
            self._copy(sems, a, 0, ins[a], outs[a].at[me], (x, y, 1 - c)).start()
            for j, (cx, cy) in enumerate(chips):
                self._copy(sems, a, 1 + j, ins[a], outs[a].at[me], (cx, cy, c)).start()

    def finish(self, ins, outs, sems):
        n = len(ins)
        if self.mode == "scatter":
            me, peers = self._scatter_peers()
            for a in range(n):
                for k, (peer, pidx) in enumerate(peers):
                    self._copy(sems, a, k, ins[a].at[pidx], outs[a].at[pidx], peer).wait_recv()
            for a in range(n):
                for k, (peer, pidx) in enumerate(peers):
                    self._copy(sems, a, k, ins[a].at[pidx], outs[a].at[me], peer).wait_send()
                self._local(ins, outs, sems, a, me).wait()
            return
        x, y, c, chips = self._place()
        me = 4 * x + 2 * y + c
        sib = (x, y, 1 - c)
        for a in range(n):
            for j, (cx, cy) in enumerate(chips):
                blk = outs[a].at[4 * cx + 2 * cy + c]
                self._copy(sems, a, 1 + j, ins[a], blk, (cx, cy, c)).wait_recv()
                self._copy(sems, a, 4 + j, blk, blk, sib).start()
        for a in range(n):
            self._copy(sems, a, 0, ins[a], outs[a].at[4 * x + 2 * y + (1 - c)], sib).wait_recv()
            for j, (cx, cy) in enumerate(chips):
                blk = outs[a].at[4 * cx + 2 * cy + (1 - c)]
                self._copy(sems, a, 4 + j, blk, blk, sib).wait_recv()
        for a in range(n):
            for k in range(N_PEER):
                self._copy(sems, a, k, ins[a], outs[a].at[me], sib).wait_send()
            self._local(ins, outs, sems, a, me).wait()


def run_exchange(exch, name):
    n = len(exch.arrays)

    def body(*refs):
        ins, outs, sems = refs[:n], refs[n:2 * n], refs[2 * n:]
        exch.start(ins, outs, sems)
        exch.finish(ins, outs, sems)

    return pl.pallas_call(
        body, name=name, in_specs=[ANY_SPEC] * n, out_specs=[ANY_SPEC] * n, out_shape=exch.out_shape,
        scratch_shapes=exch.scratch,
    )(*exch.arrays)


def _call(body, *, name, grid, in_specs, out_specs, out_shape, scratch_shapes, sem, args, exch=None):
    if exch is None:
        res = pl.pallas_call(body, name=name, grid=grid, in_specs=in_specs, out_specs=out_specs, out_shape=out_shape,
                             scratch_shapes=scratch_shapes, compiler_params=_cparams(sem))(*args)
        return res, None
    single = not isinstance(out_shape, (list, tuple))
    out_specs_l = [out_specs] if single else list(out_specs)
    out_shape_l = [out_shape] if single else list(out_shape)
    n_in, n_out, n_scr, m = len(in_specs), len(out_specs_l), len(scratch_shapes), len(exch.arrays)

    def wrapped(*refs):
        p = 0
        ins = refs[p:p + n_in]; p += n_in
        xin = refs[p:p + m]; p += m
        outs = refs[p:p + n_out]; p += n_out
        xout = refs[p:p + m]; p += m
        scr = refs[p:p + n_scr]; p += n_scr
        sems = refs[p:]
        ids = [pl.program_id(ax) for ax in range(len(grid))]
        first = functools.reduce(jnp.logical_and, [i == 0 for i in ids])
        last = functools.reduce(jnp.logical_and, [i == g - 1 for i, g in zip(ids, grid)])

        @pl.when(first)
        def _():
            exch.start(xin, xout, sems)

        body(*ins, *outs, *scr)

        @pl.when(last)
        def _():
            exch.finish(xin, xout, sems)

    res = pl.pallas_call(
        wrapped, name=name, grid=grid, in_specs=list(in_specs) + [ANY_SPEC] * m,
        out_specs=out_specs_l + [ANY_SPEC] * m, out_shape=out_shape_l + exch.out_shape,
        scratch_shapes=list(scratch_shapes) + exch.scratch, compiler_params=_cparams(sem),
    )(*args, *exch.arrays)
    outs = res[:n_out]
    return (outs[0] if single else outs), res[n_out:]


def _adam_math(w, g, m, v):
    m2 = ADAM_B1 * m + (1.0 - ADAM_B1) * g
    v2 = ADAM_B2 * v + (1.0 - ADAM_B2) * (g * g)
    m_hat = m2 / (1.0 - ADAM_B1 ** ADAM_STEP)
    v_hat = v2 / (1.0 - ADAM_B2 ** ADAM_STEP)
    delta = -ADAM_LR * (m_hat / (jnp.sqrt(v_hat) + ADAM_EPS) + ADAM_WD * w)
    return delta, m2, v2


ADAM_ROW_BLOCKS = 2


def adam_shard(parts0, parts1, w, m, v, name, part_slice=None):
    _, r, c = w.shape
    sub = part_slice
    rb = r // ADAM_ROW_BLOCKS

    def body(p0_ref, p1_ref, w_ref, m_ref, v_ref, g_ref, d_ref, m2_ref, v2_ref):
        def run(p_ref):
            g = p_ref[0].astype(F32)
            for i in range(1, N_DEV):
                g = g + p_ref[i].astype(F32)
            delta, m2, v2 = _adam_math(w_ref[0], g, m_ref[0], v_ref[0])
            g_ref[0] = g
            d_ref[0] = delta
            m2_ref[0] = m2
            v2_ref[0] = v2

        @pl.when(pl.program_id(0) == 0)
        def _():
            run(p0_ref)

        @pl.when(pl.program_id(0) == 1)
        def _():
            run(p1_ref)

    def p_spec(layer):
        row = (lambda l, j: jnp.where(l == 0, j, ADAM_ROW_BLOCKS - 1)) if layer == 0 else (lambda l, j: jnp.where(l == 1, j, 0))
        if sub is None:
            return pl.BlockSpec((N_DEV, rb, c), lambda l, j: (0, row(l, j), 0))
        return pl.BlockSpec((N_DEV, None, rb, c), lambda l, j: (0, sub, row(l, j), 0))

    blk = pl.BlockSpec((1, rb, c), lambda l, j: (l, j, 0))
    return pl.pallas_call(
        body, name=name, grid=(DEPTH, ADAM_ROW_BLOCKS),
        in_specs=[p_spec(0), p_spec(1), blk, blk, blk], out_specs=[blk] * 4,
        out_shape=[SDS(w.shape, F32)] * 4,
        compiler_params=_cparams(("arbitrary", "arbitrary")),
    )(parts0, parts1, w, m, v)


def adam_small(parts, w, m, v, name):
    def body(p_ref, w_ref, m_ref, v_ref, g_ref, d_ref, m2_ref, v2_ref):
        g = p_ref[0]
        for i in range(1, N_DEV):
            g = g + p_ref[i]
        delta, m2, v2 = _adam_math(w_ref[...], g, m_ref[...], v_ref[...])
        g_ref[...] = g
        d_ref[...] = delta
        m2_ref[...] = m2
        v2_ref[...] = v2

    return pl.pallas_call(
        body, name=name, out_shape=[SDS(w.shape, F32)] * 4, compiler_params=_cparams(),
    )(parts, w, m, v)


def _rot_cols(w):
    w4 = w.reshape(w.shape[0], 4, 2, 32)
    return jnp.stack([-w4[:, :, 1], w4[:, :, 0]], axis=2).reshape(w.shape[0], ATT_W)


def _rot_cols_t(dw_rot):
    d4 = dw_rot.reshape(dw_rot.shape[0], 4, 2, 32)
    return jnp.stack([d4[:, :, 1], -d4[:, :, 0]], axis=2).reshape(dw_rot.shape[0], ATT_W)


def build_wext(w_in):
    aq, ak, av, pu = w_in[:, 0:256], w_in[:, 256:512], w_in[:, 512:768], w_in[:, 768:1024]
    dqkvz = w_in[:, 1024:3072]
    gates = jnp.repeat(w_in[:, 3072:3080], DN_E, axis=1)
    return jnp.concatenate([aq, ak, av, _rot_cols(aq), _rot_cols(ak), dqkvz, gates, pu], axis=1)


def fold_dwext(d):
    b = EXT_ATT
    aq = d[:, 0:256] + _rot_cols_t(d[:, 768:1024])
    ak = d[:, 256:512] + _rot_cols_t(d[:, 1024:1280])
    av = d[:, 512:768]
    dqkvz = d[:, b:b + 2048]
    gates = d[:, b + R_BB:b + R_BB + 1024].reshape(d.shape[0], 8, DN_E).sum(axis=-1)
    pu = d[:, b + R_PU:b + R_PU + 256]
    return jnp.concatenate([aq, ak, av, pu, dqkvz, gates], axis=1)


def _block_diag(pw):
    z = jnp.zeros((4, 64, 4, 64), pw.dtype)
    for g in range(4):
        z = z.at[g, :, g, :].set(pw[g])
    return z.reshape(POOL_W, POOL_W)


def _diag_blocks(m):
    m4 = m.reshape(4, 64, 4, 64)
    return jnp.stack([m4[g, :, g, :] for g in range(4)], axis=0)


def _lanes(v, reps):
    return jnp.repeat(v, reps)[None, :]


def layer_fwd(p, xa, cos, sin, l, host=None):
    host = host or {}

    def carried(key):
        return host[key][0] if key in host else None

    def done(key, xo):
        if key in host:
            host[key][1](xo)

    xb, xo = ffn_fwd(xa, p["n1"], p["f1gu"], p["f1d"], f"ffn1_fwd_{l}", carried("ffn1"))
    done("ffn1", xo)
    att, rest = mix_in_fwd(xb, p["nm"], p["wext"], cos, sin, f"mix_in_fwd_{l}")
    ols = [from_classes(att_fwd(to_classes(att, d), SEQ // d // ATT_BLK, f"att_fwd_{l}_{d}"), d) for d in DILATIONS]
    qkv = dn_prep_fwd(rest, p["conv"], f"dn_prep_fwd_{l}")
    (odn, states), xo = dn_chunk_fwd(qkv, rest, p["alog"], p["dtb"], f"dn_chunk_fwd_{l}", carried("dn"))
    done("dn", xo)
    xc = mix_out_fwd(xb, ols[0], ols[1], ols[2], rest, odn, p["wbd"], p["scale"], p["onorm"], p["wout"], f"mix_out_fwd_{l}")
    xd, xo = ffn_fwd(xc, p["n2"], p["f2gu"], p["f2d"], f"ffn2_fwd_{l}", carried("ffn2"))
    done("ffn2", xo)
    return xd, dict(xa=xa, xb=xb, xc=xc, att=att, rest=rest, ols=ols, qkv=qkv, odn=odn, states=states)


def layer_bwd(p, s, dx, cos, sin, headsum, l, scatter=False, carry=None):
    (dx, d_f2gu, d_f2d, d_n2), carried = ffn_bwd(s["xc"], dx, p["n2"], p["f2gu"], p["f2d"], f"ffn2_bwd_{l}", carry)
    (d_wout, dol1, dol4, dol16, dpooled, dodn, dz, dscale, donorm, dwbd) = mix_out_bwd(
        dx, s["ols"][0], s["ols"][1], s["ols"][2], s["rest"], s["odn"], p["wbd"], p["scale"], p["onorm"], p["wout"],
        headsum, f"mix_out_bwd_{l}")
    dpu = pool_bwd(dpooled, f"pool_bwd_{l}")
    f2 = list(ffn_grads_to_shards(d_f2gu, d_f2d))
    (dqkv, dbb, dab, dalog, ddtb), xo = dn_chunk_bwd(
        s["qkv"], s["rest"], p["alog"], p["dtb"], s["states"], dodn, f"dn_chunk_bwd_{l}",
        Exchange(f2, "scatter") if scatter else None)
    if scatter:
        f2 = list(xo)
    d_dqkv, dconv = dn_prep_bwd(s["rest"], p["conv"], dqkv, f"dn_prep_bwd_{l}")
    datts = []
    for d, ol, dol in zip(DILATIONS, s["ols"], (dol1, dol4, dol16)):
        da = att_bwd(to_classes(s["att"], d), to_classes(ol, d), to_classes(dol, d), SEQ // d // ATT_BLK, f"att_bwd_{l}_{d}")
        datts.append(from_classes(da, d))
    dproj = assemble_dproj(datts, cos, sin, d_dqkv, dz, dbb, dab, dpu, f"assemble_dproj_{l}")
    dx, d_wext, d_nm = linear_bwd(s["xb"], dx, p["nm"], dproj, p["wext"], f"mix_in_bwd_{l}")
    d_win = fold_dwext(d_wext).reshape(D_MODEL, N_DEV, IN_BLK).transpose(1, 0, 2).astype(BF16)
    io = [d_win, d_wout.reshape(N_DEV, D_MODEL // N_DEV, D_MODEL).astype(BF16)]
    (dx, d_f1gu, d_f1d, d_n1), xo = ffn_bwd(s["xa"], dx, p["n1"], p["f1gu"], p["f1d"], f"ffn1_bwd_{l}",
                                           Exchange(io, "scatter") if scatter else None)
    if scatter:
        io = list(xo)
    big = dict(f1=list(ffn_grads_to_shards(d_f1gu, d_f1d)), f2=f2, io=io)
    small = dict(ffn1_norm=d_n1[0], mix_norm=d_nm[0], ffn2_norm=d_n2[0], pool_w=_diag_blocks(dwbd),
                 pool_scale=dscale[0], dn_a_log=dalog.reshape(DN_H, DN_E).sum(-1),
                 dn_dt_bias=ddtb.reshape(DN_H, DN_E).sum(-1),
                 dn_out_norm=donorm.reshape(DN_H, DN_E).sum(0), dn_conv_w=dconv)
    return dx, big, small, carried


def small_operands(l, pool_w, pool_scale, dn_out_norm, dn_a_log, dn_dt_bias, ffn1_norm, mix_norm, ffn2_norm):
    return dict(
        wbd=_block_diag(pool_w[l]).astype(BF16),
        scale=pool_scale[l][None, :],
        onorm=jnp.tile(dn_out_norm[l], DN_H)[None, :],
        alog=_lanes(dn_a_log[l], DN_E),
        dtb=_lanes(dn_dt_bias[l], DN_E),
        n1=ffn1_norm[l][None, :], nm=mix_norm[l][None, :], n2=ffn2_norm[l][None, :])


def set_mixer_weights(p, win_g, wout_g, conv_g):
    p["wext"] = build_wext(win_g.transpose(1, 0, 2).reshape(D_MODEL, IN_W))
    p["wout"] = wout_g.reshape(D_MODEL, D_MODEL)
    p["conv"] = conv_g.transpose(1, 0, 2).reshape(DN_CONV, 3 * DN_W)


def rope_tables(pos):
    inv_freq = 10000.0 ** (-jnp.arange(0, ATT_E, 2, dtype=F32) / ATT_E)
    ang = pos.astype(F32)[:, None] * inv_freq
    return jnp.tile(jnp.cos(ang), (1, 8)), jnp.tile(jnp.sin(ang), (1, 8))


def head_sum_matrix():
    return jnp.kron(jnp.eye(4, dtype=F32), jnp.ones((ATT_E, ATT_E), F32))


SMALL_NAMES = ("ffn1_norm", "mix_norm", "ffn2_norm", "pool_w", "pool_scale", "dn_a_log", "dn_dt_bias",
               "dn_out_norm", "final_norm", "dn_conv_w")


def _pack(parts):
    flat = jnp.concatenate([p.reshape(-1) for p in parts])
    n = flat.shape[0]
    rows = -(-n // 1024) * 8
    return jnp.pad(flat, (0, rows * 128 - n)).reshape(rows, 128)


def _unpack(packed, shapes):
    flat = packed.reshape(-1)
    out, off = [], 0
    for s in shapes:
        n = math.prod(s)
        out.append(flat[off:off + n].reshape(s))
        off += n
    return out


def kernel(x, positions, ffn1_norm, ffn1_w_gate, ffn1_w_up, ffn1_w_down, mix_norm, w_in, pool_w, pool_scale, dn_conv_w, dn_a_log, dn_dt_bias, dn_out_norm, w_out, ffn2_norm, ffn2_w_gate, ffn2_w_up, ffn2_w_down, final_norm, loss_target, m_ffn1_norm, m_ffn1_w_gate, m_ffn1_w_up, m_ffn1_w_down, m_mix_norm, m_w_in, m_pool_w, m_pool_scale, m_dn_conv_w, m_dn_a_log, m_dn_dt_bias, m_dn_out_norm, m_w_out, m_ffn2_norm, m_ffn2_w_gate, m_ffn2_w_up, m_ffn2_w_down, m_final_norm, v_ffn1_norm, v_ffn1_w_gate, v_ffn1_w_up, v_ffn1_w_down, v_mix_norm, v_w_in, v_pool_w, v_pool_scale, v_dn_conv_w, v_dn_a_log, v_dn_dt_bias, v_dn_out_norm, v_w_out, v_ffn2_norm, v_ffn2_w_gate, v_ffn2_w_up, v_ffn2_w_down, v_final_norm):
    me = 4 * lax.axis_index("x") + 2 * lax.axis_index("y") + lax.axis_index("c")
    x0 = x[0]
    target = loss_target[0]

    cos, sin = rope_tables(positions[0])
    headsum = head_sum_matrix()

    layers = [small_operands(l, pool_w, pool_scale, dn_out_norm, dn_a_log, dn_dt_bias, ffn1_norm, mix_norm, ffn2_norm)
              for l in range(DEPTH)]

    def ffn_shards(gate, up, down, l):
        return [jnp.stack([gate[l], up[l]]).astype(BF16), down[l].astype(BF16)]

    def gather_ffn1(l):
        def on_done(xo):
            layers[l]["f1gu"], layers[l]["f1d"] = ffn_weights_from_shards(*xo)
        return Exchange(ffn_shards(ffn1_w_gate, ffn1_w_up, ffn1_w_down, l), "gather"), on_done

    def gather_ffn2(l):
        def on_done(xo):
            layers[l]["f2gu"], layers[l]["f2d"] = ffn_weights_from_shards(*xo)
        return Exchange(ffn_shards(ffn2_w_gate, ffn2_w_up, ffn2_w_down, l), "gather"), on_done

    def gather_mixer(l):
        def on_done(xo):
            set_mixer_weights(layers[l], *xo)
        return Exchange([w_in[l].astype(BF16), w_out[l].astype(BF16), dn_conv_w[l]], "gather"), on_done

    first, on_first = gather_ffn1(0)
    on_first(run_exchange(first, "gather_ffn1_0"))
    saved = []
    xa = x0
    for l in range(DEPTH):
        host = {"ffn1": gather_mixer(l), "dn": gather_ffn2(l)}
        if l + 1 < DEPTH:
            host["ffn2"] = gather_ffn1(l + 1)
        xa, s = layer_fwd(layers[l], xa, cos, sin, l, host)
        saved.append(s)

    loss_row, dx, d_final = loss_head(xa, final_norm[None, :], target, "loss_head")
    loss = lax.psum(loss_row[0, 0], ("x", "y", "c"))

    small = {}
    big_parts = [None] * DEPTH
    carry = None
    for l in reversed(range(DEPTH)):
        dx, big, small[l], carried = layer_bwd(layers[l], saved[l], dx, cos, sin, headsum, l, True, carry)
        if carried is not None:
            big_parts[l + 1]["f1"] = list(carried)
        big_parts[l] = big
        carry = Exchange(big["f1"], "scatter")
    big_parts[0]["f1"] = list(run_exchange(carry, "scatter_ffn1_0"))
    grad_x = dx[None]

    small_shapes = {"ffn1_norm": (DEPTH, D_MODEL), "mix_norm": (DEPTH, D_MODEL), "ffn2_norm": (DEPTH, D_MODEL),
                    "pool_w": (DEPTH, 4, 64, 64), "pool_scale": (DEPTH, POOL_W), "dn_a_log": (DEPTH, DN_H),
                    "dn_dt_bias": (DEPTH, DN_H), "dn_out_norm": (DEPTH, DN_E), "final_norm": (D_MODEL,),
                    "dn_conv_w": (DEPTH, DN_CONV, 3 * DN_W)}
    g_small = {n: (d_final[0] if n == "final_norm" else jnp.stack([small[l][n] for l in range(DEPTH)]))
               for n in SMALL_NAMES}
    (small_parts,) = run_exchange(Exchange([_pack([g_small[n] for n in SMALL_NAMES])], "gather"), "gather_small_grads")

    def conv_full(a):
        return lax.dynamic_update_slice(jnp.zeros((DEPTH, DN_CONV, 3 * DN_W), F32), a, (0, 0, me * (3 * DN_W // N_DEV)))

    given = dict(ffn1_norm=(ffn1_norm, m_ffn1_norm, v_ffn1_norm), mix_norm=(mix_norm, m_mix_norm, v_mix_norm),
                 ffn2_norm=(ffn2_norm, m_ffn2_norm, v_ffn2_norm), pool_w=(pool_w, m_pool_w, v_pool_w),
                 pool_scale=(pool_scale, m_pool_scale, v_pool_scale), dn_a_log=(dn_a_log, m_dn_a_log, v_dn_a_log),
                 dn_dt_bias=(dn_dt_bias, m_dn_dt_bias, v_dn_dt_bias),
                 dn_out_norm=(dn_out_norm, m_dn_out_norm, v_dn_out_norm),
                 final_norm=(final_norm, m_final_norm, v_final_norm),
                 dn_conv_w=(conv_full(dn_conv_w), conv_full(m_dn_conv_w), conv_full(v_dn_conv_w)))
    packed_wmv = [_pack([given[n][k] for n in SMALL_NAMES]) for k in range(3)]
    small_out = adam_small(small_parts, *packed_wmv, "adam_small")
    shapes = [small_shapes[n] for n in SMALL_NAMES]
    small_res = {n: [] for n in SMALL_NAMES}
    for arr in small_out:
        for n, v_ in zip(SMALL_NAMES, _unpack(arr, shapes)):
            if n == "dn_conv_w":
                v_ = lax.dynamic_slice(v_, (0, 0, me * (3 * DN_W // N_DEV)), (DEPTH, DN_CONV, 3 * DN_W // N_DEV))
            small_res[n].append(v_)

    def parts_of(group, idx):
        return [big_parts[l][group][idx] for l in range(DEPTH)]

    big_res = dict(
        ffn1_w_gate=adam_shard(*parts_of("f1", 0), ffn1_w_gate, m_ffn1_w_gate, v_ffn1_w_gate, "adam_ffn1_gate", 0),
        ffn1_w_up=adam_shard(*parts_of("f1", 0), ffn1_w_up, m_ffn1_w_up, v_ffn1_w_up, "adam_ffn1_up", 1),
        ffn1_w_down=adam_shard(*parts_of("f1", 1), ffn1_w_down, m_ffn1_w_down, v_ffn1_w_down, "adam_ffn1_down"),
        ffn2_w_gate=adam_shard(*parts_of("f2", 0), ffn2_w_gate, m_ffn2_w_gate, v_ffn2_w_gate, "adam_ffn2_gate", 0),
        ffn2_w_up=adam_shard(*parts_of("f2", 0), ffn2_w_up, m_ffn2_w_up, v_ffn2_w_up, "adam_ffn2_up", 1),
        ffn2_w_down=adam_shard(*parts_of("f2", 1), ffn2_w_down, m_ffn2_w_down, v_ffn2_w_down, "adam_ffn2_down"),
        w_in=adam_shard(*parts_of("io", 0), w_in, m_w_in, v_w_in, "adam_w_in"),
        w_out=adam_shard(*parts_of("io", 1), w_out, m_w_out, v_w_out, "adam_w_out"),
    )

    order = ("ffn1_norm", "ffn1_w_gate", "ffn1_w_up", "ffn1_w_down", "mix_norm", "w_in", "pool_w", "pool_scale",
             "dn_conv_w", "dn_a_log", "dn_dt_bias", "dn_out_norm", "w_out", "ffn2_norm", "ffn2_w_gate", "ffn2_w_up",
             "ffn2_w_down", "final_norm")
    res = {**small_res, **big_res}
    outs = [loss, grad_x]
    for k in range(4):
        outs.extend(res[n][k] for n in order)
    return tuple(outs)
```

```python
import functools
import math

import jax
import jax.numpy as jnp
from jax import lax
from jax.experimental import pallas as pl
from jax.experimental.pallas import tpu as pltpu

F32 = jnp.float32
BF16 = jnp.bfloat16
HI = lax.Precision.HIGHEST
INV_PREC = lax.Precision.HIGH
SDS = jax.ShapeDtypeStruct

N_DEV = 8
SEQ = 4096
D_MODEL = 1024
DEPTH = 2
D_FF = 2816
FF_BLK = D_FF // N_DEV
ATT_W = 256
ATT_E = 64
ATT_BLK = 128
DILATIONS = (1, 4, 16)
POOL_W = 256
POOL_HALO = 16
DN_W = 512
DN_H = 4
DN_E = 128
DN_C = 64
N_CHUNK = SEQ // DN_C
IN_W = 3080
IN_BLK = IN_W // N_DEV
EPS = 1e-6
EXT_ATT = 1280
EXT_REST = 3328
EXT_W = EXT_ATT + EXT_REST
R_DQKV, R_DZ, R_BB, R_AB, R_PU = 0, 1536, 2048, 2560, 3072

ADAM_LR, ADAM_B1, ADAM_B2, ADAM_EPS, ADAM_WD, ADAM_STEP = 0.001, 0.9, 0.999, 1e-08, 0.01, 10

VMEM_LIMIT = 60 * 1024 * 1024
MESH = pl.DeviceIdType.MESH


def _cparams(sem=None):
    kw = dict(vmem_limit_bytes=VMEM_LIMIT)
    if sem is not None:
        kw["dimension_semantics"] = sem
    return pltpu.CompilerParams(**kw)


def _dot(a, b, prec=None):
    return jnp.dot(a, b, preferred_element_type=F32, precision=prec)


def _dot_nt(a, b, prec=None):
    return lax.dot_general(a, b, (((1,), (1,)), ((), ())), preferred_element_type=F32, precision=prec)


def _dot_tn(a, b, prec=None):
    return lax.dot_general(a, b, (((0,), (0,)), ((), ())), preferred_element_type=F32, precision=prec)


def _sigmoid(x):
    return jax.nn.sigmoid(x)


def _rms_stats(x):
    r = lax.rsqrt(jnp.mean(x * x, axis=-1, keepdims=True) + EPS)
    return x * r, r


def _rms_bwd(xh, r, w, dh):
    dxh = dh * w
    dx = r * (dxh - xh * jnp.mean(dxh * xh, axis=-1, keepdims=True))
    return dx, jnp.sum(dh * xh, axis=0, keepdims=True)


FFN_T_FWD = 1024
FFN_T_BWD = 512
FF_TILE = 256
N_FF_TILE = D_FF // FF_TILE


def ffn_weights_from_shards(wgu_g, wd_g):
    return wgu_g.transpose(1, 2, 0, 3).reshape(2, D_MODEL, D_FF), wd_g.reshape(D_FF, D_MODEL)


def ffn_grads_to_shards(dwgu, dwd):
    return dwgu.reshape(2, D_MODEL, N_DEV, FF_BLK).transpose(2, 0, 1, 3), dwd.reshape(N_DEV, FF_BLK, D_MODEL)


def ffn_fwd(x, nw, wgu, wd, name, exch=None):
    t = FFN_T_FWD

    def body(x_ref, nw_ref, wgu_ref, wd_ref, o_ref, h_scr, acc_scr):
        k = pl.program_id(1)

        @pl.when(k == 0)
        def _():
            xh, _r = _rms_stats(x_ref[...])
            h_scr[...] = (xh * nw_ref[...]).astype(BF16)
            acc_scr[...] = jnp.zeros_like(acc_scr)

        h = h_scr[...]
        hg = _dot(h, wgu_ref[0])
        hu = _dot(h, wgu_ref[1])
        a = (hg * _sigmoid(hg) * hu).astype(BF16)
        acc_scr[...] += _dot(a, wd_ref[...])

        @pl.when(k == N_FF_TILE - 1)
        def _():
            o_ref[...] = x_ref[...] + 0.5 * acc_scr[...]

    return _call(
        body, name=name, grid=(SEQ // t, N_FF_TILE),
        in_specs=[pl.BlockSpec((t, D_MODEL), lambda i, k: (i, 0)),
                  pl.BlockSpec((1, D_MODEL), lambda i, k: (0, 0)),
                  pl.BlockSpec((2, D_MODEL, FF_TILE), lambda i, k: (0, 0, k)),
                  pl.BlockSpec((FF_TILE, D_MODEL), lambda i, k: (k, 0))],
        out_specs=pl.BlockSpec((t, D_MODEL), lambda i, k: (i, 0)),
        out_shape=SDS((SEQ, D_MODEL), F32),
        scratch_shapes=[pltpu.VMEM((t, D_MODEL), BF16), pltpu.VMEM((t, D_MODEL), F32)],
        sem=("arbitrary", "arbitrary"), args=(x, nw, wgu, wd), exch=exch)


def ffn_bwd(x, dxo, nw, wgu, wd, name, exch=None):
    t = FFN_T_BWD
    nt = SEQ // t

    def body(x_ref, dxo_ref, nw_ref, wgu_ref, wd_ref, dx_ref, dwgu_ref, dwd_ref, dnw_ref,
             dh_scr, ag_scr, au_scr, ad_scr, h_scr):
        k = pl.program_id(0)
        i = pl.program_id(1)
        rows = pl.ds(pl.multiple_of(i * t, t), t)
        nw_v = nw_ref[...]

        @pl.when(k == 0)
        def _():
            xh0, _r0 = _rms_stats(x_ref[...])
            h_scr[rows, :] = (xh0 * nw_v).astype(BF16)

        h = h_scr[rows, :]
        dy = (0.5 * dxo_ref[...]).astype(BF16)
        wg = wgu_ref[0]
        wu = wgu_ref[1]
        hg = _dot(h, wg)
        hu = _dot(h, wu)
        sg = _sigmoid(hg)
        sil = hg * sg
        a = (sil * hu).astype(BF16)
        da = _dot_nt(dy, wd_ref[...])
        dhu = (da * sil).astype(BF16)
        dhg = (da * hu * (sg * (1.0 + hg * (1.0 - sg)))).astype(BF16)
        p_d = _dot_tn(a, dy)
        p_g = _dot_tn(h, dhg)
        p_u = _dot_tn(h, dhu)
        dh = _dot_nt(dhg, wg) + _dot_nt(dhu, wu)

        @pl.when(i == 0)
        def _():
            ad_scr[...] = p_d
            ag_scr[...] = p_g
            au_scr[...] = p_u

        @pl.when(i > 0)
        def _():
            ad_scr[...] += p_d
            ag_scr[...] += p_g
            au_scr[...] += p_u

        @pl.when(i == nt - 1)
        def _():
            dwd_ref[...] = ad_scr[...].astype(BF16)
            dwgu_ref[0] = ag_scr[...].astype(BF16)
            dwgu_ref[1] = au_scr[...].astype(BF16)

        @pl.when(k == 0)
        def _():
            dh_scr[rows, :] = dh

        @pl.when(k > 0)
        def _():
            dh_scr[rows, :] += dh

        @pl.when(jnp.logical_and(k == 0, i == 0))
        def _():
            dnw_ref[...] = jnp.zeros_like(dnw_ref)

        @pl.when(k == N_FF_TILE - 1)
        def _():
            xh, r = _rms_stats(x_ref[...])
            dx, dw = _rms_bwd(xh, r, nw_v, dh_scr[rows, :])
            dx_ref[...] = dxo_ref[...] + dx
            dnw_ref[...] += dw

    last = N_FF_TILE - 1
    return _call(
        body, name=name, grid=(N_FF_TILE, nt),
        in_specs=[pl.BlockSpec((t, D_MODEL), lambda k, i: (i, 0)),
                  pl.BlockSpec((t, D_MODEL), lambda k, i: (i, 0)),
                  pl.BlockSpec((1, D_MODEL), lambda k, i: (0, 0)),
                  pl.BlockSpec((2, D_MODEL, FF_TILE), lambda k, i: (0, 0, k)),
                  pl.BlockSpec((FF_TILE, D_MODEL), lambda k, i: (k, 0))],
        out_specs=[pl.BlockSpec((t, D_MODEL), lambda k, i: (jnp.where(k == last, i, 0), 0)),
                   pl.BlockSpec((2, D_MODEL, FF_TILE), lambda k, i: (0, 0, k)),
                   pl.BlockSpec((FF_TILE, D_MODEL), lambda k, i: (k, 0)),
                   pl.BlockSpec((1, D_MODEL), lambda k, i: (0, 0))],
        out_shape=[SDS((SEQ, D_MODEL), F32), SDS((2, D_MODEL, D_FF), BF16),
                   SDS((D_FF, D_MODEL), BF16), SDS((1, D_MODEL), F32)],
        scratch_shapes=[pltpu.VMEM((SEQ, D_MODEL), F32), pltpu.VMEM((D_MODEL, FF_TILE), F32),
                        pltpu.VMEM((D_MODEL, FF_TILE), F32), pltpu.VMEM((FF_TILE, D_MODEL), F32),
                        pltpu.VMEM((SEQ, D_MODEL), BF16)],
        sem=("arbitrary", "arbitrary"), args=(x, dxo, nw, wgu, wd), exch=exch)


def loss_head(x, fw, target, name):
    t = 512

    def body(x_ref, fw_ref, tg_ref, loss_ref, dx_ref, dfw_ref):
        i = pl.program_id(0)
        xh, r = _rms_stats(x_ref[...])
        w = fw_ref[...]
        err = xh * w - tg_ref[...]
        part = 0.5 * jnp.sum(jnp.sum(err * err, axis=-1, keepdims=True), axis=0, keepdims=True) / D_MODEL
        dx, dw = _rms_bwd(xh, r, w, err * (1.0 / D_MODEL))
        dx_ref[...] = dx

        @pl.when(i == 0)
        def _():
            loss_ref[...] = jnp.zeros_like(loss_ref)
            dfw_ref[...] = jnp.zeros_like(dfw_ref)

        loss_ref[...] += jnp.broadcast_to(part, loss_ref.shape)
        dfw_ref[...] += dw

    return pl.pallas_call(
        body, name=name, grid=(SEQ // t,),
        in_specs=[pl.BlockSpec((t, D_MODEL), lambda i: (i, 0)),
                  pl.BlockSpec((1, D_MODEL), lambda i: (0, 0)),
                  pl.BlockSpec((t, D_MODEL), lambda i: (i, 0))],
        out_specs=[pl.BlockSpec((1, 128), lambda i: (0, 0)),
                   pl.BlockSpec((t, D_MODEL), lambda i: (i, 0)),
                   pl.BlockSpec((1, D_MODEL), lambda i: (0, 0))],
        out_shape=[SDS((1, 128), F32), SDS((SEQ, D_MODEL), F32), SDS((1, D_MODEL), F32)],
        compiler_params=_cparams(("arbitrary",)),
    )(x, fw, target)


MIX_T = 256


def mix_in_fwd(x, nw, wext, cos, sin, name):
    t = MIX_T

    def body(x_ref, nw_ref, w_ref, cos_ref, sin_ref, att_ref, rest_ref):
        xh, _r = _rms_stats(x_ref[...])
        h = (xh * nw_ref[...]).astype(BF16)
        pa = _dot(h, w_ref[:, 0:EXT_ATT])
        c = cos_ref[...]
        s = sin_ref[...]
        att_ref[:, 0:256] = pa[:, 0:256] * c + pa[:, 768:1024] * s
        att_ref[:, 256:512] = pa[:, 256:512] * c + pa[:, 1024:1280] * s
        att_ref[:, 512:768] = pa[:, 512:768]
        for j in range(EXT_REST // 256):
            rest_ref[:, 256 * j:256 * j + 256] = _dot(h, w_ref[:, EXT_ATT + 256 * j:EXT_ATT + 256 * j + 256])

    return pl.pallas_call(
        body, name=name, grid=(SEQ // t,),
        in_specs=[pl.BlockSpec((t, D_MODEL), lambda i: (i, 0)),
                  pl.BlockSpec((1, D_MODEL), lambda i: (0, 0)),
                  pl.BlockSpec((D_MODEL, EXT_W), lambda i: (0, 0)),
                  pl.BlockSpec((t, ATT_W), lambda i: (i, 0)),
                  pl.BlockSpec((t, ATT_W), lambda i: (i, 0))],
        out_specs=[pl.BlockSpec((t, 768), lambda i: (i, 0)),
                   pl.BlockSpec((t, EXT_REST), lambda i: (i, 0))],
        out_shape=[SDS((SEQ, 768), F32), SDS((SEQ, EXT_REST), F32)],
        compiler_params=_cparams(("arbitrary",)),
    )(x, nw, wext, cos, sin)


def assemble_dproj(datts, cos, sin, d_dqkv, dz, dbb, dab, dpu, name):
    t = 512

    def body(d1_ref, d4_ref, d16_ref, cos_ref, sin_ref, dqkv_ref, dz_ref, dbb_ref, dab_ref, dpu_ref, o_ref):
        da = d1_ref[...] + d4_ref[...] + d16_ref[...]
        c = cos_ref[...]
        s = sin_ref[...]
        dq = da[:, 0:256]
        dk = da[:, 256:512]
        o_ref[:, 0:256] = (dq * c).astype(BF16)
        o_ref[:, 256:512] = (dk * c).astype(BF16)
        o_ref[:, 512:768] = da[:, 512:768].astype(BF16)
        o_ref[:, 768:1024] = (dq * s).astype(BF16)
        o_ref[:, 1024:1280] = (dk * s).astype(BF16)
        b = EXT_ATT
        o_ref[:, b + R_DQKV:b + R_DQKV + 1536] = dqkv_ref[...].astype(BF16)
        o_ref[:, b + R_DZ:b + R_DZ + 512] = dz_ref[...].astype(BF16)
        o_ref[:, b + R_BB:b + R_BB + 512] = dbb_ref[...].astype(BF16)
        o_ref[:, b + R_AB:b + R_AB + 512] = dab_ref[...].astype(BF16)
        o_ref[:, b + R_PU:b + R_PU + 256] = dpu_ref[...].astype(BF16)

    row = lambda w: pl.BlockSpec((t, w), lambda i: (i, 0))
    return pl.pallas_call(
        body, name=name, grid=(SEQ // t,),
        in_specs=[row(768), row(768), row(768), row(256), row(256), row(1536), row(512), row(512), row(512), row(256)],
        out_specs=row(EXT_W),
        out_shape=SDS((SEQ, EXT_W), BF16),
        compiler_params=_cparams(("arbitrary",)),
    )(*datts, cos, sin, d_dqkv, dz, dbb, dab, dpu)


def linear_bwd(x, dxo, nw, dy, w, name):
    t = 512
    nb = 768
    n = w.shape[1]
    nt = SEQ // t
    nn = n // nb

    def body(x_ref, dxo_ref, nw_ref, dy_ref, w_ref, dx_ref, dw_ref, dnw_ref, dh_scr, h_scr):
        k = pl.program_id(0)
        i = pl.program_id(1)
        rows = pl.ds(pl.multiple_of(i * t, t), t)
        nw_v = nw_ref[...]

        @pl.when(k == 0)
        def _():
            xh0, _r0 = _rms_stats(x_ref[...])
            h_scr[rows, :] = (xh0 * nw_v).astype(BF16)

        h = h_scr[rows, :]
        dyv = dy_ref[...]
        p_w = _dot_tn(h, dyv)
        dh = _dot_nt(dyv, w_ref[...])

        @pl.when(i == 0)
        def _():
            dw_ref[...] = p_w

        @pl.when(i > 0)
        def _():
            dw_ref[...] += p_w

        @pl.when(k == 0)
        def _():
            dh_scr[rows, :] = dh

        @pl.when(k > 0)
        def _():
            dh_scr[rows, :] += dh

        @pl.when(jnp.logical_and(k == 0, i == 0))
        def _():
            dnw_ref[...] = jnp.zeros_like(dnw_ref)

        @pl.when(k == nn - 1)
        def _():
            xh, r = _rms_stats(x_ref[...])
            dx, dw = _rms_bwd(xh, r, nw_v, dh_scr[rows, :])
            dx_ref[...] = dxo_ref[...] + dx
            dnw_ref[...] += dw

    last = nn - 1
    return pl.pallas_call(
        body, name=name, grid=(nn, nt),
        in_specs=[pl.BlockSpec((t, D_MODEL), lambda k, i: (i, 0)),
                  pl.BlockSpec((t, D_MODEL), lambda k, i: (i, 0)),
                  pl.BlockSpec((1, D_MODEL), lambda k, i: (0, 0)),
                  pl.BlockSpec((t, nb), lambda k, i: (i, k)),
                  pl.BlockSpec((D_MODEL, nb), lambda k, i: (0, k))],
        out_specs=[pl.BlockSpec((t, D_MODEL), lambda k, i: (jnp.where(k == last, i, 0), 0)),
                   pl.BlockSpec((D_MODEL, nb), lambda k, i: (0, k)),
                   pl.BlockSpec((1, D_MODEL), lambda k, i: (0, 0))],
        out_shape=[SDS((SEQ, D_MODEL), F32), SDS((D_MODEL, n), F32), SDS((1, D_MODEL), F32)],
        scratch_shapes=[pltpu.VMEM((SEQ, D_MODEL), F32), pltpu.VMEM((SEQ, D_MODEL), BF16)],
        compiler_params=_cparams(("arbitrary", "arbitrary")),
    )(x, dxo, nw, dy, w)


def _att_masks():
    qi = lax.broadcasted_iota(jnp.int32, (ATT_BLK, ATT_BLK), 0)
    ki = lax.broadcasted_iota(jnp.int32, (ATT_BLK, ATT_BLK), 1)
    return ki <= qi, ki >= qi


NEG = -1e30


def _att_heads(ref, base):
    return jnp.stack([ref[:, base + ATT_E * hd:base + ATT_E * hd + ATT_E] for hd in range(4)], axis=0)


def att_fwd(att, blocks_per_class, name):
    nblk = SEQ // ATT_BLK

    def body(cur_ref, prev_ref, o_ref):
        i = pl.program_id(0)
        has_prev = (i % blocks_per_class) != 0
        m_d, m_p = _att_masks()
        m_p = jnp.logical_and(m_p, has_prev)
        q = _att_heads(cur_ref, 0).astype(BF16)
        kc = _att_heads(cur_ref, 256).astype(BF16)
        vc = _att_heads(cur_ref, 512).astype(BF16)
        kp = _att_heads(prev_ref, 256).astype(BF16)
        vp = _att_heads(prev_ref, 512).astype(BF16)
        sd = jnp.where(m_d, _bdot(q, kc, 2, 2) * 0.125, NEG)
        sp = jnp.where(m_p, _bdot(q, kp, 2, 2) * 0.125, NEG)
        m = jnp.maximum(jnp.max(sd, axis=-1, keepdims=True), jnp.max(sp, axis=-1, keepdims=True))
        pd = jnp.exp(sd - m)
        pp = jnp.exp(sp - m)
        den = jnp.sum(pd, axis=-1, keepdims=True) + jnp.sum(pp, axis=-1, keepdims=True)
        inv = 1.0 / den
        o = _bdot((pd * inv).astype(BF16), vc, 2, 1) + _bdot((pp * inv).astype(BF16), vp, 2, 1)
        lse = m + jnp.log(den)
        for hd in range(4):
            o_ref[:, ATT_E * hd:ATT_E * hd + ATT_E] = o[hd]
            o_ref[:, 256 + ATT_E * hd:256 + ATT_E * hd + ATT_E] = jnp.broadcast_to(lse[hd], (ATT_BLK, ATT_E))

    return pl.pallas_call(
        body, name=name, grid=(nblk,),
        in_specs=[pl.BlockSpec((ATT_BLK, 768), lambda i: (i, 0)),
                  pl.BlockSpec((ATT_BLK, 768), lambda i: (jnp.maximum(i - 1, 0), 0))],
        out_specs=pl.BlockSpec((ATT_BLK, 512), lambda i: (i, 0)),
        out_shape=SDS((SEQ, 512), F32),
        compiler_params=_cparams(("arbitrary",)),
    )(att, att)


def att_bwd(att, ol, dol, blocks_per_class, name):
    nblk = SEQ // ATT_BLK

    def body(prev_ref, cur_ref, nxt_ref, ol_c_ref, ol_n_ref, dol_c_ref, dol_n_ref, d_ref):
        i = pl.program_id(0)
        has_prev = (i % blocks_per_class) != 0
        has_next = ((i + 1) % blocks_per_class) != 0
        m_d, m_band = _att_masks()
        m_p = jnp.logical_and(m_band, has_prev)
        m_n = jnp.logical_and(m_band, has_next)

        def pair(q, k, v, lse, do, dterm, mask):
            s = jnp.where(mask, _bdot(q, k, 2, 2) * 0.125, NEG)
            p = jnp.exp(s - lse)
            dp = _bdot(do, v, 2, 2)
            ds = (p * (dp + dterm) * 0.125).astype(BF16)
            return p.astype(BF16), ds

        q_c = _att_heads(cur_ref, 0).astype(BF16)
        k_c = _att_heads(cur_ref, 256).astype(BF16)
        v_c = _att_heads(cur_ref, 512).astype(BF16)
        k_p = _att_heads(prev_ref, 256).astype(BF16)
        v_p = _att_heads(prev_ref, 512).astype(BF16)
        q_n = _att_heads(nxt_ref, 0).astype(BF16)
        o_c = _att_heads(ol_c_ref, 0)
        o_n = _att_heads(ol_n_ref, 0)
        lse_c = _att_heads(ol_c_ref, 256)[:, :, 0:1]
        lse_n = _att_heads(ol_n_ref, 256)[:, :, 0:1]
        do_c = _att_heads(dol_c_ref, 0)
        do_n = _att_heads(dol_n_ref, 0)
        t_c = _att_heads(dol_c_ref, 256)[:, :, 0:1] - jnp.sum(do_c * o_c, axis=-1, keepdims=True)
        t_n = _att_heads(dol_n_ref, 256)[:, :, 0:1] - jnp.sum(do_n * o_n, axis=-1, keepdims=True)
        do_cb = do_c.astype(BF16)
        do_nb = do_n.astype(BF16)
        p1, ds1 = pair(q_c, k_c, v_c, lse_c, do_cb, t_c, m_d)
        _p2, ds2 = pair(q_c, k_p, v_p, lse_c, do_cb, t_c, m_p)
        p3, ds3 = pair(q_n, k_c, v_c, lse_n, do_nb, t_n, m_n)
        dq = _bdot(ds1, k_c, 2, 1) + _bdot(ds2, k_p, 2, 1)
        dk = _bdot(ds1, q_c, 1, 1) + _bdot(ds3, q_n, 1, 1)
        dv = _bdot(p1, do_cb, 1, 1) + _bdot(p3, do_nb, 1, 1)
        for hd in range(4):
            a = ATT_E * hd
            d_ref[:, a:a + ATT_E] = dq[hd]
            d_ref[:, 256 + a:256 + a + ATT_E] = dk[hd]
            d_ref[:, 512 + a:512 + a + ATT_E] = dv[hd]

    prv = lambda i: (jnp.maximum(i - 1, 0), 0)
    cur = lambda i: (i, 0)
    nxt = lambda i: (jnp.minimum(i + 1, nblk - 1), 0)
    return pl.pallas_call(
        body, name=name, grid=(nblk,),
        in_specs=[pl.BlockSpec((ATT_BLK, 768), prv), pl.BlockSpec((ATT_BLK, 768), cur),
                  pl.BlockSpec((ATT_BLK, 768), nxt),
                  pl.BlockSpec((ATT_BLK, 512), cur), pl.BlockSpec((ATT_BLK, 512), nxt),
                  pl.BlockSpec((ATT_BLK, 512), cur), pl.BlockSpec((ATT_BLK, 512), nxt)],
        out_specs=pl.BlockSpec((ATT_BLK, 768), cur),
        out_shape=SDS((SEQ, 768), F32),
        compiler_params=_cparams(("arbitrary",)),
    )(att, att, att, ol, ol, dol, dol)


def to_classes(a, d):
    if d == 1:
        return a
    w = a.shape[1]
    return a.reshape(SEQ // d, d, w).transpose(1, 0, 2).reshape(SEQ, w)


def from_classes(a, d):
    if d == 1:
        return a
    w = a.shape[1]
    return a.reshape(d, SEQ // d, w).transpose(1, 0, 2).reshape(SEQ, w)


def _shift_down(x, k):
    rows = lax.broadcasted_iota(jnp.int32, x.shape, 0)
    return jnp.where(rows >= k, pltpu.roll(x, k, 0), 0.0)


def _shift_up(x, k):
    n = x.shape[0]
    rows = lax.broadcasted_iota(jnp.int32, x.shape, 0)
    return jnp.where(rows < n - k, pltpu.roll(x, n - k, 0), 0.0)


@functools.partial(jax.custom_vjp, nondiff_argnums=(1,))
def _delay(x, k):
    return _shift_down(x, k)


def _delay_fwd(x, k):
    return _shift_down(x, k), None


def _delay_bwd(k, _res, g):
    return (_shift_up(g, k),)


_delay.defvjp(_delay_fwd, _delay_bwd)

DN_CONV = 4


def _dn_prep_fn(u, w, kind):
    y = w[DN_CONV - 1:DN_CONV] * u
    for j in range(DN_CONV - 1):
        y = y + w[j:j + 1] * _delay(u, DN_CONV - 1 - j)
    y = y * _sigmoid(y)
    nrm = y * lax.rsqrt(jnp.sum(y * y, axis=-1, keepdims=True) + EPS)
    return jnp.where(kind == 0, nrm * (DN_E ** -0.5), jnp.where(kind == 1, nrm, y))


def dn_prep_fwd(rest, conv_w, name):
    def body(u_ref, w_ref, o_ref):
        j = pl.program_id(0)
        kind = (j >= DN_H).astype(jnp.int32) + (j >= 2 * DN_H).astype(jnp.int32)
        o_ref[...] = _dn_prep_fn(u_ref[...], w_ref[...], kind)

    return pl.pallas_call(
        body, name=name, grid=(3 * DN_H,),
        in_specs=[pl.BlockSpec((SEQ, DN_E), lambda j: (0, j)),
                  pl.BlockSpec((DN_CONV, DN_E), lambda j: (0, j))],
        out_specs=pl.BlockSpec((SEQ, DN_E), lambda j: (0, j)),
        out_shape=SDS((SEQ, 3 * DN_W), F32),
        compiler_params=_cparams(("arbitrary",)),
    )(rest, conv_w)


def dn_prep_bwd(rest, conv_w, dqkv, name):
    def body(u_ref, w_ref, g_ref, du_ref, dw_ref):
        j = pl.program_id(0)
        kind = (j >= DN_H).astype(jnp.int32) + (j >= 2 * DN_H).astype(jnp.int32)
        _y, vjp = jax.vjp(lambda u, w: _dn_prep_fn(u, w, kind), u_ref[...], w_ref[...])
        du, dw = vjp(g_ref[...])
        du_ref[...] = du
        dw_ref[...] = dw

    return pl.pallas_call(
        body, name=name, grid=(3 * DN_H,),
        in_specs=[pl.BlockSpec((SEQ, DN_E), lambda j: (0, j)),
                  pl.BlockSpec((DN_CONV, DN_E), lambda j: (0, j)),
                  pl.BlockSpec((SEQ, DN_E), lambda j: (0, j))],
        out_specs=[pl.BlockSpec((SEQ, DN_E), lambda j: (0, j)),
                   pl.BlockSpec((DN_CONV, DN_E), lambda j: (0, j))],
        out_shape=[SDS((SEQ, 3 * DN_W), F32), SDS((DN_CONV, 3 * DN_W), F32)],
        compiler_params=_cparams(("arbitrary",)),
    )(rest, conv_w, dqkv)


def _bdot(a, b, ca, cb, prec=None):
    return lax.dot_general(a, b, (((ca,), (cb,)), ((0,), (0,))), preferred_element_type=F32, precision=prec)


def _unit_lower_inverse(a):
    eye = (lax.broadcasted_iota(jnp.int32, (DN_C, DN_C), 0) == lax.broadcasted_iota(jnp.int32, (DN_C, DN_C), 1)).astype(F32)
    p = eye - a
    b = _bdot(a, a, 2, 1, INV_PREC)
    for lvl in range(5):
        p = p + _bdot(p, b, 2, 1, INV_PREC)
        if lvl < 4:
            b = _bdot(b, b, 2, 1, INV_PREC)
    return p


@jax.custom_vjp
def _tri_inv(a):
    return _unit_lower_inverse(a)


def _tri_inv_fwd(a):
    t = _unit_lower_inverse(a)
    return t, t


def _tri_inv_bwd(t, g):
    return (-_bdot(_bdot(t, g, 1, 1, INV_PREC), t, 2, 2, INV_PREC),)


_tri_inv.defvjp(_tri_inv_fwd, _tri_inv_bwd)


def _b16(x):
    return x.astype(BF16)


def _dn_chunk(q, k, v, bb, ab, alog, dtb, state):
    ri = lax.broadcasted_iota(jnp.int32, (DN_C, DN_C), 0)
    ci = lax.broadcasted_iota(jnp.int32, (DN_C, DN_C), 1)
    lower = ri >= ci
    strict = ri > ci
    nh = q.shape[0]
    beta = _sigmoid(bb)
    xg = ab + dtb
    softplus = jnp.maximum(xg, 0.0) + jnp.log(1.0 + jnp.exp(-jnp.abs(xg)))
    gi = -jnp.exp(alog) * softplus
    g = _bdot(jnp.broadcast_to(lower.astype(F32), (nh, DN_C, DN_C)), gi, 2, 1, HI)
    eg = jnp.exp(g)
    kb = k * beta
    vb = v * beta
    g_col = g[:, :, 0:DN_C]
    g_row = _bdot(jnp.full((nh, DN_C, DN_E), 1.0 / DN_E, F32), g, 2, 2, HI)
    decay = jnp.where(lower, jnp.exp(jnp.where(lower, g_col - g_row, 0.0)), 0.0)
    kbf = _b16(k)
    a = jnp.where(strict, _bdot(_b16(kb), kbf, 2, 2) * decay, 0.0)
    t = _tri_inv(a)
    tb = _b16(t)
    u = _bdot(tb, _b16(vb), 2, 1)
    w = _bdot(tb, _b16(kb * eg), 2, 1)
    intra = jnp.where(lower, _bdot(_b16(q), kbf, 2, 2) * decay, 0.0)
    sb = _b16(state)
    v_new = u - _bdot(_b16(w), sb, 2, 1)
    o = _bdot(_b16(q * eg), sb, 2, 1) + _bdot(_b16(intra), _b16(v_new), 2, 1)
    g_last = g[:, DN_C - 1:DN_C, :]
    k_dec = k * jnp.exp(g_last - g)
    new_state = state * jnp.exp(g_last) + _bdot(_b16(k_dec), _b16(v_new), 1, 1)
    return o, new_state


def _heads(ref, base=0):
    return jnp.stack([ref[:, base + DN_E * hd:base + DN_E * hd + DN_E] for hd in range(DN_H)], axis=0)


def _put_heads(ref, val, base=0):
    for hd in range(DN_H):
        ref[:, base + DN_E * hd:base + DN_E * hd + DN_E] = val[hd]


def _dn_args(qkv_ref, bb_ref, ab_ref, alog_ref, dtb_ref, state):
    return (_heads(qkv_ref), _heads(qkv_ref, DN_W), _heads(qkv_ref, 2 * DN_W), _heads(bb_ref), _heads(ab_ref),
            _heads(alog_ref), _heads(dtb_ref), state)


def dn_chunk_fwd(qkv, rest, alog_b, dtb_b, name, exch=None):
    def body(qkv_ref, bb_ref, ab_ref, alog_ref, dtb_ref, o_ref, st_ref, state_scr):
        n = pl.program_id(0)

        @pl.when(n == 0)
        def _():
            state_scr[...] = jnp.zeros_like(state_scr)

        st = state_scr[...]
        st_ref[0] = st
        o, ns = _dn_chunk(*_dn_args(qkv_ref, bb_ref, ab_ref, alog_ref, dtb_ref, st))
        _put_heads(o_ref, o)
        state_scr[...] = ns

    return _call(
        body, name=name, grid=(N_CHUNK,),
        in_specs=[pl.BlockSpec((DN_C, 3 * DN_W), lambda n: (n, 0)),
                  pl.BlockSpec((DN_C, DN_W), lambda n: (n, R_BB // DN_W)),
                  pl.BlockSpec((DN_C, DN_W), lambda n: (n, R_AB // DN_W)),
                  pl.BlockSpec((1, DN_W), lambda n: (0, 0)),
                  pl.BlockSpec((1, DN_W), lambda n: (0, 0))],
        out_specs=[pl.BlockSpec((DN_C, DN_W), lambda n: (n, 0)),
                   pl.BlockSpec((1, DN_H, DN_E, DN_E), lambda n: (n, 0, 0, 0))],
        out_shape=[SDS((SEQ, DN_W), F32), SDS((N_CHUNK, DN_H, DN_E, DN_E), F32)],
        scratch_shapes=[pltpu.VMEM((DN_H, DN_E, DN_E), F32)],
        sem=("arbitrary",), args=(qkv, rest, rest, alog_b, dtb_b), exch=exch)


def dn_chunk_bwd(qkv, rest, alog_b, dtb_b, states, do, name, exch=None):
    last = N_CHUNK - 1

    def body(qkv_ref, bb_ref, ab_ref, alog_ref, dtb_ref, st_ref, do_ref,
             dqkv_ref, dbb_ref, dab_ref, dalog_ref, ddtb_ref, dstate_scr):
        s = pl.program_id(0)

        @pl.when(s == 0)
        def _():
            dstate_scr[...] = jnp.zeros_like(dstate_scr)
            dalog_ref[...] = jnp.zeros_like(dalog_ref)
            ddtb_ref[...] = jnp.zeros_like(ddtb_ref)

        _out, vjp = jax.vjp(_dn_chunk, *_dn_args(qkv_ref, bb_ref, ab_ref, alog_ref, dtb_ref, st_ref[0]))
        dq, dk, dv, dbb, dab, dalog, ddtb, dst = vjp((_heads(do_ref), dstate_scr[...]))
        _put_heads(dqkv_ref, dq)
        _put_heads(dqkv_ref, dk, DN_W)
        _put_heads(dqkv_ref, dv, 2 * DN_W)
        _put_heads(dbb_ref, dbb)
        _put_heads(dab_ref, dab)
        _put_heads(dalog_ref, _heads(dalog_ref) + dalog)
        _put_heads(ddtb_ref, _heads(ddtb_ref) + ddtb)
        dstate_scr[...] = dst

    rev = lambda w: (lambda s: (last - s, w))
    return _call(
        body, name=name, grid=(N_CHUNK,),
        in_specs=[pl.BlockSpec((DN_C, 3 * DN_W), rev(0)),
                  pl.BlockSpec((DN_C, DN_W), rev(R_BB // DN_W)),
                  pl.BlockSpec((DN_C, DN_W), rev(R_AB // DN_W)),
                  pl.BlockSpec((1, DN_W), lambda s: (0, 0)),
                  pl.BlockSpec((1, DN_W), lambda s: (0, 0)),
                  pl.BlockSpec((1, DN_H, DN_E, DN_E), lambda s: (last - s, 0, 0, 0)),
                  pl.BlockSpec((DN_C, DN_W), rev(0))],
        out_specs=[pl.BlockSpec((DN_C, 3 * DN_W), rev(0)),
                   pl.BlockSpec((DN_C, DN_W), rev(0)),
                   pl.BlockSpec((DN_C, DN_W), rev(0)),
                   pl.BlockSpec((1, DN_W), lambda s: (0, 0)),
                   pl.BlockSpec((1, DN_W), lambda s: (0, 0))],
        out_shape=[SDS((SEQ, 3 * DN_W), F32), SDS((SEQ, DN_W), F32), SDS((SEQ, DN_W), F32),
                   SDS((1, DN_W), F32), SDS((1, DN_W), F32)],
        scratch_shapes=[pltpu.VMEM((DN_H, DN_E, DN_E), F32)],
        sem=("arbitrary",), args=(qkv, rest, rest, alog_b, dtb_b, states, do), exch=exch)


OUT_T = 256


def _pool_consts(rows_total, t0, halo_before):
    lane = lax.broadcasted_iota(jnp.int32, (rows_total, POOL_W), 1)
    row = lax.broadcasted_iota(jnp.int32, (rows_total, POOL_W), 0)
    grp = (lane >= 64).astype(jnp.int32) + (lane >= 128).astype(jnp.int32) + (lane >= 192).astype(jnp.int32)
    win = jnp.where(grp == 0, 2, jnp.where(grp == 1, 4, jnp.where(grp == 2, 8, 16)))
    pos = t0 + row - halo_before
    cnt = jnp.minimum(pos + 1, win).astype(F32)
    return grp, cnt


def _pool_select(grp, s2, s4, s8, s16):
    return jnp.where(grp == 0, s2, jnp.where(grp == 1, s4, jnp.where(grp == 2, s8, s16)))


def _pooled(u_ext, t0):
    n = u_ext.shape[0]
    grp, cnt = _pool_consts(n, t0, POOL_HALO)
    s2 = u_ext + pltpu.roll(u_ext, 1, 0)
    s4 = s2 + pltpu.roll(s2, 2, 0)
    s8 = s4 + pltpu.roll(s4, 4, 0)
    s16 = s8 + pltpu.roll(s8, 8, 0)
    out = _pool_select(grp, s2, s4, s8, s16) / jnp.maximum(cnt, 1.0) - u_ext
    return out[POOL_HALO:, :]


def _merge_weights(l1, l4, l16):
    m = jnp.maximum(jnp.maximum(l1, l4), l16)
    e1 = jnp.exp(l1 - m)
    e4 = jnp.exp(l4 - m)
    e16 = jnp.exp(l16 - m)
    inv = 1.0 / (e1 + e4 + e16)
    return e1 * inv, e4 * inv, e16 * inv


def _out_parts(ol1_ref, ol4_ref, ol16_ref, pu_ref, puh_ref, odn_ref, z_ref, wbd_ref, i, t):
    w1, w4, w16 = _merge_weights(ol1_ref[:, 256:512], ol4_ref[:, 256:512], ol16_ref[:, 256:512])
    ya = w1 * ol1_ref[:, 0:256] + w4 * ol4_ref[:, 0:256] + w16 * ol16_ref[:, 0:256]
    halo = jnp.where(i > 0, puh_ref[...], 0.0)
    pooled = _pooled(jnp.concatenate([halo, pu_ref[...]], axis=0), i * t)
    pw = _dot(pooled.astype(BF16), wbd_ref[...])
    return ya, pooled, pw, (w1, w4, w16)


def _out_specs_common(t):
    def row(w, cb=0):
        return pl.BlockSpec((t, w), lambda i: (i, cb))

    halo = pl.BlockSpec((POOL_HALO, POOL_W),
                        lambda i: (jnp.maximum(i * (t // POOL_HALO) - 1, 0), R_PU // POOL_W))
    full = lambda a, b: pl.BlockSpec((a, b), lambda i: (0, 0))
    return [row(512), row(512), row(512), row(POOL_W, R_PU // POOL_W), halo, row(DN_W), row(DN_W, R_DZ // DN_W),
            full(POOL_W, POOL_W), full(1, POOL_W), full(1, DN_W), full(D_MODEL, D_MODEL)]


def mix_out_fwd(x, ol1, ol4, ol16, rest, odn, wbd, scale, onorm_b, wout, name):
    t = OUT_T

    def body(x_ref, ol1_ref, ol4_ref, ol16_ref, pu_ref, puh_ref, odn_ref, z_ref, wbd_ref, sc_ref, on_ref, wo_ref, o_ref):
        i = pl.program_id(0)
        ya, _pooled_v, pw, _w = _out_parts(ol1_ref, ol4_ref, ol16_ref, pu_ref, puh_ref, odn_ref, z_ref, wbd_ref, i, t)
        yb = pw * sc_ref[...]
        acc = x_ref[...] + _dot(ya.astype(BF16), wo_ref[0:256, :]) + _dot(yb.astype(BF16), wo_ref[256:512, :])
        for hd in range(DN_H):
            sl = slice(DN_E * hd, DN_E * hd + DN_E)
            oh, _r = _rms_stats(odn_ref[:, sl])
            z = z_ref[:, sl]
            yc = oh * on_ref[:, sl] * (z * _sigmoid(z))
            acc = acc + _dot(yc.astype(BF16), wo_ref[512 + DN_E * hd:512 + DN_E * hd + DN_E, :])
        o_ref[...] = acc

    return pl.pallas_call(
        body, name=name, grid=(SEQ // t,),
        in_specs=[pl.BlockSpec((t, D_MODEL), lambda i: (i, 0))] + _out_specs_common(t),
        out_specs=pl.BlockSpec((t, D_MODEL), lambda i: (i, 0)),
        out_shape=SDS((SEQ, D_MODEL), F32),
        compiler_params=_cparams(("arbitrary",)),
    )(x, ol1, ol4, ol16, rest, rest, odn, rest, wbd, scale, onorm_b, wout)


def mix_out_bwd(dxo, ol1, ol4, ol16, rest, odn, wbd, scale, onorm_b, wout, headsum, name):
    t = OUT_T

    def body(dxo_ref, ol1_ref, ol4_ref, ol16_ref, pu_ref, puh_ref, odn_ref, z_ref, wbd_ref, sc_ref, on_ref, wo_ref, hs_ref,
             dwo_ref, d1_ref, d4_ref, d16_ref, dpl_ref, dodn_ref, dz_ref, dsc_ref, don_ref, dwbd_ref):
        i = pl.program_id(0)

        @pl.when(i == 0)
        def _():
            dwo_ref[...] = jnp.zeros_like(dwo_ref)
            dsc_ref[...] = jnp.zeros_like(dsc_ref)
            don_ref[...] = jnp.zeros_like(don_ref)
            dwbd_ref[...] = jnp.zeros_like(dwbd_ref)

        ya, pooled, pw, (w1, w4, w16) = _out_parts(ol1_ref, ol4_ref, ol16_ref, pu_ref, puh_ref, odn_ref, z_ref, wbd_ref, i, t)
        sc = sc_ref[...]
        dxb = dxo_ref[...].astype(BF16)
        dwo_ref[0:256, :] += _dot_tn(ya.astype(BF16), dxb)
        dwo_ref[256:512, :] += _dot_tn((pw * sc).astype(BF16), dxb)
        dya = _dot_nt(dxb, wo_ref[0:256, :])
        o1 = ol1_ref[:, 0:256]
        o4 = ol4_ref[:, 0:256]
        o16 = ol16_ref[:, 0:256]
        hs = hs_ref[...]
        s1 = _dot(dya * o1, hs, HI)
        s4 = _dot(dya * o4, hs, HI)
        s16 = _dot(dya * o16, hs, HI)
        sbar = w1 * s1 + w4 * s4 + w16 * s16
        d1_ref[:, 0:256] = w1 * dya
        d1_ref[:, 256:512] = w1 * (s1 - sbar)
        d4_ref[:, 0:256] = w4 * dya
        d4_ref[:, 256:512] = w4 * (s4 - sbar)
        d16_ref[:, 0:256] = w16 * dya
        d16_ref[:, 256:512] = w16 * (s16 - sbar)
        dyb = _dot_nt(dxb, wo_ref[256:512, :])
        dsc_ref[...] += jnp.sum(dyb * pw, axis=0, keepdims=True)
        dpw = (dyb * sc).astype(BF16)
        dwbd_ref[...] += _dot_tn(pooled.astype(BF16), dpw)
        dpl_ref[...] = _dot_nt(dpw, wbd_ref[...])
        for hd in range(DN_H):
            sl = slice(DN_E * hd, DN_E * hd + DN_E)
            rows_w = slice(512 + DN_E * hd, 512 + DN_E * hd + DN_E)
            oh, r = _rms_stats(odn_ref[:, sl])
            z = z_ref[:, sl]
            sg = _sigmoid(z)
            sz = z * sg
            nw = on_ref[:, sl]
            on = oh * nw
            dwo_ref[rows_w, :] += _dot_tn((on * sz).astype(BF16), dxb)
            dyc = _dot_nt(dxb, wo_ref[rows_w, :])
            dz_ref[:, sl] = dyc * on * (sg * (1.0 + z * (1.0 - sg)))
            dx, dw = _rms_bwd(oh, r, nw, dyc * sz)
            dodn_ref[:, sl] = dx
            don_ref[:, sl] += dw

    row = lambda w: pl.BlockSpec((t, w), lambda i: (i, 0))
    full = lambda a, b: pl.BlockSpec((a, b), lambda i: (0, 0))
    return pl.pallas_call(
        body, name=name, grid=(SEQ // t,),
        in_specs=[row(D_MODEL)] + _out_specs_common(t) + [full(ATT_W, ATT_W)],
        out_specs=[full(D_MODEL, D_MODEL), row(512), row(512), row(512), row(POOL_W), row(DN_W), row(DN_W),
                   full(1, POOL_W), full(1, DN_W), full(POOL_W, POOL_W)],
        out_shape=[SDS((D_MODEL, D_MODEL), F32), SDS((SEQ, 512), F32), SDS((SEQ, 512), F32), SDS((SEQ, 512), F32),
                   SDS((SEQ, POOL_W), F32), SDS((SEQ, DN_W), F32), SDS((SEQ, DN_W), F32),
                   SDS((1, POOL_W), F32), SDS((1, DN_W), F32), SDS((POOL_W, POOL_W), F32)],
        compiler_params=_cparams(("arbitrary",)),
    )(dxo, ol1, ol4, ol16, rest, rest, odn, rest, wbd, scale, onorm_b, wout, headsum)


def pool_bwd(dpooled, name):
    t = 512
    nt = SEQ // t

    def body(d_ref, dn_ref, o_ref):
        i = pl.program_id(0)
        halo = jnp.where(i < nt - 1, dn_ref[...], 0.0)
        d_ext = jnp.concatenate([d_ref[...], halo], axis=0)
        n = t + POOL_HALO
        grp, cnt = _pool_consts(n, i * t, 0)
        dq = d_ext / cnt
        s2 = dq + pltpu.roll(dq, n - 1, 0)
        s4 = s2 + pltpu.roll(s2, n - 2, 0)
        s8 = s4 + pltpu.roll(s4, n - 4, 0)
        s16 = s8 + pltpu.roll(s8, n - 8, 0)
        o_ref[...] = (_pool_select(grp, s2, s4, s8, s16) - d_ext)[0:t, :]

    return pl.pallas_call(
        body, name=name, grid=(nt,),
        in_specs=[pl.BlockSpec((t, POOL_W), lambda i: (i, 0)),
                  pl.BlockSpec((POOL_HALO, POOL_W),
                               lambda i: (jnp.minimum((i + 1) * (t // POOL_HALO), SEQ // POOL_HALO - 1), 0))],
        out_specs=pl.BlockSpec((t, POOL_W), lambda i: (i, 0)),
        out_shape=SDS((SEQ, POOL_W), F32),
        compiler_params=_cparams(("arbitrary",)),
    )(dpooled, dpooled)


N_PEER = N_DEV - 1
ANY_SPEC = pl.BlockSpec(memory_space=pl.ANY)


class Exchange:
    def __init__(self, arrays, mode):
        self.arrays = list(arrays)
        self.mode = mode
        n = len(self.arrays)
        if mode == "scatter":
            self.out_shape = [SDS(a.shape, a.dtype) for a in self.arrays]
        else:
            self.out_shape = [SDS((N_DEV,) + a.shape, a.dtype) for a in self.arrays]
        self.scratch = [pltpu.SemaphoreType.DMA((n * N_PEER,)), pltpu.SemaphoreType.DMA((n * N_PEER,)),
                        pltpu.SemaphoreType.DMA((n,))]

    @staticmethod
    def _place():
        x, y, c = lax.axis_index("x"), lax.axis_index("y"), lax.axis_index("c")
        chips = [(1 - x, y), (x, 1 - y), (1 - x, 1 - y)]
        return x, y, c, chips

    @staticmethod
    def _copy(sems, a, k, src, dst, to):
        send_sems, recv_sems, _ = sems
        return pltpu.make_async_remote_copy(
            src_ref=src, dst_ref=dst, send_sem=send_sems.at[a * N_PEER + k], recv_sem=recv_sems.at[a * N_PEER + k],
            device_id=to, device_id_type=MESH)

    def _scatter_peers(self):
        x, y, c, _ = self._place()
        out = []
        for fx, fy, fc in ((0, 0, 1), (1, 0, 0), (0, 1, 0), (1, 1, 0), (1, 0, 1), (0, 1, 1), (1, 1, 1)):
            px, py, pc = x ^ fx, y ^ fy, c ^ fc
            out.append(((px, py, pc), 4 * px + 2 * py + pc))
        return 4 * x + 2 * y + c, out

    def _local(self, ins, outs, sems, a, me):
        src = ins[a].at[me] if self.mode == "scatter" else ins[a]
        return pltpu.make_async_copy(src, outs[a].at[me], sems[2].at[a])

    def start(self, ins, outs, sems):
        if self.mode == "scatter":
            me, peers = self._scatter_peers()
            for a in range(len(ins)):
                self._local(ins, outs, sems, a, me).start()
                for k, (peer, pidx) in enumerate(peers):
                    self._copy(sems, a, k, ins[a].at[pidx], outs[a].at[me], peer).start()
            return
        x, y, c, chips = self._place()
        me = 4 * x + 2 * y + c
        for a in range(len(ins)):
            self._local(ins, outs, sems, a, me).start()
            self._copy(sems, a, 0, ins[a], outs[a].at[me], (x, y, 1 - c)).start()
            for j, (cx, cy) in enumerate(chips):
                self._copy(sems, a, 1 + j, ins[a], outs[a].at[me], (cx, cy, c)).start()

    def finish(self, ins, outs, sems):
        n = len(ins)
        if self.mode == "scatter":
            me, peers = self._scatter_peers()
            for a in range(n):
                for k, (peer, pidx) in enumerate(peers):
                    self._copy(sems, a, k, ins[a].at[pidx], outs[a].at[pidx], peer).wait_recv()
            for a in range(n):
                for k, (peer, pidx) in enumerate(peers):
                    self._copy(sems, a, k, ins[a].at[pidx], outs[a].at[me], peer).wait_send()
                self._local(ins, outs, sems, a, me).wait()
            return
        x, y, c, chips = self._place()
        me = 4 * x + 2 * y + c
        sib = (x, y, 1 - c)
        for a in range(n):
            for j, (cx, cy) in enumerate(chips):
                blk = outs[a].at[4 * cx + 2 * cy + c]
                self._copy(sems, a, 1 + j, ins[a], blk, (cx, cy, c)).wait_recv()
                self._copy(sems, a, 4 + j, blk, blk, sib).start()
        for a in range(n):
            self._copy(sems, a, 0, ins[a], outs[a].at[4 * x + 2 * y + (1 - c)], sib).wait_recv()
            for j, (cx, cy) in enumerate(chips):
                blk = outs[a].at[4 * cx + 2 * cy + (1 - c)]
                self._copy(sems, a, 4 + j, blk, blk, sib).wait_recv()
        for a in range(n):
            for k in range(N_PEER):
                self._copy(sems, a, k, ins[a], outs[a].at[me], sib).wait_send()
            self._local(ins, outs, sems, a, me).wait()


def run_exchange(exch, name):
    n = len(exch.arrays)

    def body(*refs):
        ins, outs, sems = refs[:n], refs[n:2 * n], refs[2 * n:]
        exch.start(ins, outs, sems)
        exch.finish(ins, outs, sems)

    return pl.pallas_call(
        body, name=name, in_specs=[ANY_SPEC] * n, out_specs=[ANY_SPEC] * n, out_shape=exch.out_shape,
        scratch_shapes=exch.scratch,
    )(*exch.arrays)


def _call(body, *, name, grid, in_specs, out_specs, out_shape, scratch_shapes, sem, args, exch=None):
    if exch is None:
        res = pl.pallas_call(body, name=name, grid=grid, in_specs=in_specs, out_specs=out_specs, out_shape=out_shape,
                             scratch_shapes=scratch_shapes, compiler_params=_cparams(sem))(*args)
        return res, None
    single = not isinstance(out_shape, (list, tuple))
    out_specs_l = [out_specs] if single else list(out_specs)
    out_shape_l = [out_shape] if single else list(out_shape)
    n_in, n_out, n_scr, m = len(in_specs), len(out_specs_l), len(scratch_shapes), len(exch.arrays)

    def wrapped(*refs):
        p = 0
        ins = refs[p:p + n_in]; p += n_in
        xin = refs[p:p + m]; p += m
        outs = refs[p:p + n_out]; p += n_out
        xout = refs[p:p + m]; p += m
        scr = refs[p:p + n_scr]; p += n_scr
        sems = refs[p:]
        ids = [pl.program_id(ax) for ax in range(len(grid))]
        first = functools.reduce(jnp.logical_and, [i == 0 for i in ids])
        last = functools.reduce(jnp.logical_and, [i == g - 1 for i, g in zip(ids, grid)])

        @pl.when(first)
        def _():
            exch.start(xin, xout, sems)

        body(*ins, *outs, *scr)

        @pl.when(last)
        def _():
            exch.finish(xin, xout, sems)

    res = pl.pallas_call(
        wrapped, name=name, grid=grid, in_specs=list(in_specs) + [ANY_SPEC] * m,
        out_specs=out_specs_l + [ANY_SPEC] * m, out_shape=out_shape_l + exch.out_shape,
        scratch_shapes=list(scratch_shapes) + exch.scratch, compiler_params=_cparams(sem),
    )(*args, *exch.arrays)
    outs = res[:n_out]
    return (outs[0] if single else outs), res[n_out:]


def _adam_math(w, g, m, v):
    m2 = ADAM_B1 * m + (1.0 - ADAM_B1) * g
    v2 = ADAM_B2 * v + (1.0 - ADAM_B2) * (g * g)
    m_hat = m2 / (1.0 - ADAM_B1 ** ADAM_STEP)
    v_hat = v2 / (1.0 - ADAM_B2 ** ADAM_STEP)
    delta = -ADAM_LR * (m_hat / (jnp.sqrt(v_hat) + ADAM_EPS) + ADAM_WD * w)
    return delta, m2, v2


ADAM_ROW_BLOCKS = 2


def adam_shard(parts0, parts1, w, m, v, name, part_slice=None):
    _, r, c = w.shape
    sub = part_slice
    rb = r // ADAM_ROW_BLOCKS

    def body(p0_ref, p1_ref, w_ref, m_ref, v_ref, g_ref, d_ref, m2_ref, v2_ref):
        def run(p_ref):
            g = p_ref[0].astype(F32)
            for i in range(1, N_DEV):
                g = g + p_ref[i].astype(F32)
            delta, m2, v2 = _adam_math(w_ref[0], g, m_ref[0], v_ref[0])
            g_ref[0] = g
            d_ref[0] = delta
            m2_ref[0] = m2
            v2_ref[0] = v2

        @pl.when(pl.program_id(0) == 0)
        def _():
            run(p0_ref)

        @pl.when(pl.program_id(0) == 1)
        def _():
            run(p1_ref)

    def p_spec(layer):
        row = (lambda l, j: jnp.where(l == 0, j, ADAM_ROW_BLOCKS - 1)) if layer == 0 else (lambda l, j: jnp.where(l == 1, j, 0))
        if sub is None:
            return pl.BlockSpec((N_DEV, rb, c), lambda l, j: (0, row(l, j), 0))
        return pl.BlockSpec((N_DEV, None, rb, c), lambda l, j: (0, sub, row(l, j), 0))

    blk = pl.BlockSpec((1, rb, c), lambda l, j: (l, j, 0))
    return pl.pallas_call(
        body, name=name, grid=(DEPTH, ADAM_ROW_BLOCKS),
        in_specs=[p_spec(0), p_spec(1), blk, blk, blk], out_specs=[blk] * 4,
        out_shape=[SDS(w.shape, F32)] * 4,
        compiler_params=_cparams(("arbitrary", "arbitrary")),
    )(parts0, parts1, w, m, v)


def adam_small(parts, w, m, v, name):
    def body(p_ref, w_ref, m_ref, v_ref, g_ref, d_ref, m2_ref, v2_ref):
        g = p_ref[0]
        for i in range(1, N_DEV):
            g = g + p_ref[i]
        delta, m2, v2 = _adam_math(w_ref[...], g, m_ref[...], v_ref[...])
        g_ref[...] = g
        d_ref[...] = delta
        m2_ref[...] = m2
        v2_ref[...] = v2

    return pl.pallas_call(
        body, name=name, out_shape=[SDS(w.shape, F32)] * 4, compiler_params=_cparams(),
    )(parts, w, m, v)


def _rot_cols(w):
    w4 = w.reshape(w.shape[0], 4, 2, 32)
    return jnp.stack([-w4[:, :, 1], w4[:, :, 0]], axis=2).reshape(w.shape[0], ATT_W)


def _rot_cols_t(dw_rot):
    d4 = dw_rot.reshape(dw_rot.shape[0], 4, 2, 32)
    return jnp.stack([d4[:, :, 1], -d4[:, :, 0]], axis=2).reshape(dw_rot.shape[0], ATT_W)


def build_wext(w_in):
    aq, ak, av, pu = w_in[:, 0:256], w_in[:, 256:512], w_in[:, 512:768], w_in[:, 768:1024]
    dqkvz = w_in[:, 1024:3072]
    gates = jnp.repeat(w_in[:, 3072:3080], DN_E, axis=1)
    return jnp.concatenate([aq, ak, av, _rot_cols(aq), _rot_cols(ak), dqkvz, gates, pu], axis=1)


def fold_dwext(d):
    b = EXT_ATT
    aq = d[:, 0:256] + _rot_cols_t(d[:, 768:1024])
    ak = d[:, 256:512] + _rot_cols_t(d[:, 1024:1280])
    av = d[:, 512:768]
    dqkvz = d[:, b:b + 2048]
    gates = d[:, b + R_BB:b + R_BB + 1024].reshape(d.shape[0], 8, DN_E).sum(axis=-1)
    pu = d[:, b + R_PU:b + R_PU + 256]
    return jnp.concatenate([aq, ak, av, pu, dqkvz, gates], axis=1)


def _block_diag(pw):
    z = jnp.zeros((4, 64, 4, 64), pw.dtype)
    for g in range(4):
        z = z.at[g, :, g, :].set(pw[g])
    return z.reshape(POOL_W, POOL_W)


def _diag_blocks(m):
    m4 = m.reshape(4, 64, 4, 64)
    return jnp.stack([m4[g, :, g, :] for g in range(4)], axis=0)


def _lanes(v, reps):
    return jnp.repeat(v, reps)[None, :]


def layer_fwd(p, xa, cos, sin, l, host=None):
    host = host or {}

    def carried(key):
        return host[key][0] if key in host else None

    def done(key, xo):
        if key in host:
            host[key][1](xo)

    xb, xo = ffn_fwd(xa, p["n1"], p["f1gu"], p["f1d"], f"ffn1_fwd_{l}", carried("ffn1"))
    done("ffn1", xo)
    att, rest = mix_in_fwd(xb, p["nm"], p["wext"], cos, sin, f"mix_in_fwd_{l}")
    att_c = [to_classes(att, d) for d in DILATIONS]
    ol_c = [att_fwd(a, SEQ // d // ATT_BLK, f"att_fwd_{l}_{d}") for a, d in zip(att_c, DILATIONS)]
    ols = [from_classes(o, d) for o, d in zip(ol_c, DILATIONS)]
    qkv = dn_prep_fwd(rest, p["conv"], f"dn_prep_fwd_{l}")
    (odn, states), xo = dn_chunk_fwd(qkv, rest, p["alog"], p["dtb"], f"dn_chunk_fwd_{l}", carried("dn"))
    done("dn", xo)
    xc = mix_out_fwd(xb, ols[0], ols[1], ols[2], rest, odn, p["wbd"], p["scale"], p["onorm"], p["wout"], f"mix_out_fwd_{l}")
    xd, xo = ffn_fwd(xc, p["n2"], p["f2gu"], p["f2d"], f"ffn2_fwd_{l}", carried("ffn2"))
    done("ffn2", xo)
    return xd, dict(xa=xa, xb=xb, xc=xc, att_c=att_c, ol_c=ol_c, rest=rest, ols=ols, qkv=qkv, odn=odn, states=states)


def layer_bwd(p, s, dx, cos, sin, headsum, l, scatter=False, carry=None):
    (dx, d_f2gu, d_f2d, d_n2), carried = ffn_bwd(s["xc"], dx, p["n2"], p["f2gu"], p["f2d"], f"ffn2_bwd_{l}", carry)
    (d_wout, dol1, dol4, dol16, dpooled, dodn, dz, dscale, donorm, dwbd) = mix_out_bwd(
        dx, s["ols"][0], s["ols"][1], s["ols"][2], s["rest"], s["odn"], p["wbd"], p["scale"], p["onorm"], p["wout"],
        headsum, f"mix_out_bwd_{l}")
    dpu = pool_bwd(dpooled, f"pool_bwd_{l}")
    f2 = list(ffn_grads_to_shards(d_f2gu, d_f2d))
    (dqkv, dbb, dab, dalog, ddtb), xo = dn_chunk_bwd(
        s["qkv"], s["rest"], p["alog"], p["dtb"], s["states"], dodn, f"dn_chunk_bwd_{l}",
        Exchange(f2, "scatter") if scatter else None)
    if scatter:
        f2 = list(xo)
    d_dqkv, dconv = dn_prep_bwd(s["rest"], p["conv"], dqkv, f"dn_prep_bwd_{l}")
    datts = []
    for d, a_c, o_c, dol in zip(DILATIONS, s["att_c"], s["ol_c"], (dol1, dol4, dol16)):
        da = att_bwd(a_c, o_c, to_classes(dol, d), SEQ // d // ATT_BLK, f"att_bwd_{l}_{d}")
        datts.append(from_classes(da, d))
    dproj = assemble_dproj(datts, cos, sin, d_dqkv, dz, dbb, dab, dpu, f"assemble_dproj_{l}")
    dx, d_wext, d_nm = linear_bwd(s["xb"], dx, p["nm"], dproj, p["wext"], f"mix_in_bwd_{l}")
    d_win = fold_dwext(d_wext).reshape(D_MODEL, N_DEV, IN_BLK).transpose(1, 0, 2).astype(BF16)
    io = [d_win, d_wout.reshape(N_DEV, D_MODEL // N_DEV, D_MODEL).astype(BF16)]
    (dx, d_f1gu, d_f1d, d_n1), xo = ffn_bwd(s["xa"], dx, p["n1"], p["f1gu"], p["f1d"], f"ffn1_bwd_{l}",
                                           Exchange(io, "scatter") if scatter else None)
    if scatter:
        io = list(xo)
    big = dict(f1=list(ffn_grads_to_shards(d_f1gu, d_f1d)), f2=f2, io=io)
    small = dict(ffn1_norm=d_n1[0], mix_norm=d_nm[0], ffn2_norm=d_n2[0], pool_w=_diag_blocks(dwbd),
                 pool_scale=dscale[0], dn_a_log=dalog.reshape(DN_H, DN_E).sum(-1),
                 dn_dt_bias=ddtb.reshape(DN_H, DN_E).sum(-1),
                 dn_out_norm=donorm.reshape(DN_H, DN_E).sum(0), dn_conv_w=dconv)
    return dx, big, small, carried


def small_operands(l, pool_w, pool_scale, dn_out_norm, dn_a_log, dn_dt_bias, ffn1_norm, mix_norm, ffn2_norm):
    return dict(
        wbd=_block_diag(pool_w[l]).astype(BF16),
        scale=pool_scale[l][None, :],
        onorm=jnp.tile(dn_out_norm[l], DN_H)[None, :],
        alog=_lanes(dn_a_log[l], DN_E),
        dtb=_lanes(dn_dt_bias[l], DN_E),
        n1=ffn1_norm[l][None, :], nm=mix_norm[l][None, :], n2=ffn2_norm[l][None, :])


def set_mixer_weights(p, win_g, wout_g, conv_g):
    p["wext"] = build_wext(win_g.transpose(1, 0, 2).reshape(D_MODEL, IN_W))
    p["wout"] = wout_g.reshape(D_MODEL, D_MODEL)
    p["conv"] = conv_g.transpose(1, 0, 2).reshape(DN_CONV, 3 * DN_W)


def rope_tables(pos):
    inv_freq = 10000.0 ** (-jnp.arange(0, ATT_E, 2, dtype=F32) / ATT_E)
    ang = pos.astype(F32)[:, None] * inv_freq
    return jnp.tile(jnp.cos(ang), (1, 8)), jnp.tile(jnp.sin(ang), (1, 8))


def head_sum_matrix():
    return jnp.kron(jnp.eye(4, dtype=F32), jnp.ones((ATT_E, ATT_E), F32))


SMALL_NAMES = ("ffn1_norm", "mix_norm", "ffn2_norm", "pool_w", "pool_scale", "dn_a_log", "dn_dt_bias",
               "dn_out_norm", "final_norm", "dn_conv_w")


def _pack(parts):
    flat = jnp.concatenate([p.reshape(-1) for p in parts])
    n = flat.shape[0]
    rows = -(-n // 1024) * 8
    return jnp.pad(flat, (0, rows * 128 - n)).reshape(rows, 128)


def _unpack(packed, shapes):
    flat = packed.reshape(-1)
    out, off = [], 0
    for s in shapes:
        n = math.prod(s)
        out.append(flat[off:off + n].reshape(s))
        off += n
    return out


def kernel(x, positions, ffn1_norm, ffn1_w_gate, ffn1_w_up, ffn1_w_down, mix_norm, w_in, pool_w, pool_scale, dn_conv_w, dn_a_log, dn_dt_bias, dn_out_norm, w_out, ffn2_norm, ffn2_w_gate, ffn2_w_up, ffn2_w_down, final_norm, loss_target, m_ffn1_norm, m_ffn1_w_gate, m_ffn1_w_up, m_ffn1_w_down, m_mix_norm, m_w_in, m_pool_w, m_pool_scale, m_dn_conv_w, m_dn_a_log, m_dn_dt_bias, m_dn_out_norm, m_w_out, m_ffn2_norm, m_ffn2_w_gate, m_ffn2_w_up, m_ffn2_w_down, m_final_norm, v_ffn1_norm, v_ffn1_w_gate, v_ffn1_w_up, v_ffn1_w_down, v_mix_norm, v_w_in, v_pool_w, v_pool_scale, v_dn_conv_w, v_dn_a_log, v_dn_dt_bias, v_dn_out_norm, v_w_out, v_ffn2_norm, v_ffn2_w_gate, v_ffn2_w_up, v_ffn2_w_down, v_final_norm):
    me = 4 * lax.axis_index("x") + 2 * lax.axis_index("y") + lax.axis_index("c")
    x0 = x[0]
    target = loss_target[0]

    cos, sin = rope_tables(positions[0])
    headsum = head_sum_matrix()

    layers = [small_operands(l, pool_w, pool_scale, dn_out_norm, dn_a_log, dn_dt_bias, ffn1_norm, mix_norm, ffn2_norm)
              for l in range(DEPTH)]

    def ffn_shards(gate, up, down, l):
        return [jnp.stack([gate[l], up[l]]).astype(BF16), down[l].astype(BF16)]

    def gather_ffn1(l):
        def on_done(xo):
            layers[l]["f1gu"], layers[l]["f1d"] = ffn_weights_from_shards(*xo)
        return Exchange(ffn_shards(ffn1_w_gate, ffn1_w_up, ffn1_w_down, l), "gather"), on_done

    def gather_ffn2(l):
        def on_done(xo):
            layers[l]["f2gu"], layers[l]["f2d"] = ffn_weights_from_shards(*xo)
        return Exchange(ffn_shards(ffn2_w_gate, ffn2_w_up, ffn2_w_down, l), "gather"), on_done

    def gather_mixer(l):
        def on_done(xo):
            set_mixer_weights(layers[l], *xo)
        return Exchange([w_in[l].astype(BF16), w_out[l].astype(BF16), dn_conv_w[l]], "gather"), on_done

    first, on_first = gather_ffn1(0)
    on_first(run_exchange(first, "gather_ffn1_0"))
    saved = []
    xa = x0
    for l in range(DEPTH):
        host = {"ffn1": gather_mixer(l), "dn": gather_ffn2(l)}
        if l + 1 < DEPTH:
            host["ffn2"] = gather_ffn1(l + 1)
        xa, s = layer_fwd(layers[l], xa, cos, sin, l, host)
        saved.append(s)

    loss_row, dx, d_final = loss_head(xa, final_norm[None, :], target, "loss_head")
    loss = lax.psum(loss_row[0, 0], ("x", "y", "c"))

    small = {}
    big_parts = [None] * DEPTH
    carry = None
    for l in reversed(range(DEPTH)):
        dx, big, small[l], carried = layer_bwd(layers[l], saved[l], dx, cos, sin, headsum, l, True, carry)
        if carried is not None:
            big_parts[l + 1]["f1"] = list(carried)
        big_parts[l] = big
        carry = Exchange(big["f1"], "scatter")
    big_parts[0]["f1"] = list(run_exchange(carry, "scatter_ffn1_0"))
    grad_x = dx[None]

    small_shapes = {"ffn1_norm": (DEPTH, D_MODEL), "mix_norm": (DEPTH, D_MODEL), "ffn2_norm": (DEPTH, D_MODEL),
                    "pool_w": (DEPTH, 4, 64, 64), "pool_scale": (DEPTH, POOL_W), "dn_a_log": (DEPTH, DN_H),
                    "dn_dt_bias": (DEPTH, DN_H), "dn_out_norm": (DEPTH, DN_E), "final_norm": (D_MODEL,),
                    "dn_conv_w": (DEPTH, DN_CONV, 3 * DN_W)}
    g_small = {n: (d_final[0] if n == "final_norm" else jnp.stack([small[l][n] for l in range(DEPTH)]))
               for n in SMALL_NAMES}
    (small_parts,) = run_exchange(Exchange([_pack([g_small[n] for n in SMALL_NAMES])], "gather"), "gather_small_grads")

    def conv_full(a):
        return lax.dynamic_update_slice(jnp.zeros((DEPTH, DN_CONV, 3 * DN_W), F32), a, (0, 0, me * (3 * DN_W // N_DEV)))

    given = dict(ffn1_norm=(ffn1_norm, m_ffn1_norm, v_ffn1_norm), mix_norm=(mix_norm, m_mix_norm, v_mix_norm),
                 ffn2_norm=(ffn2_norm, m_ffn2_norm, v_ffn2_norm), pool_w=(pool_w, m_pool_w, v_pool_w),
                 pool_scale=(pool_scale, m_pool_scale, v_pool_scale), dn_a_log=(dn_a_log, m_dn_a_log, v_dn_a_log),
                 dn_dt_bias=(dn_dt_bias, m_dn_dt_bias, v_dn_dt_bias),
                 dn_out_norm=(dn_out_norm, m_dn_out_norm, v_dn_out_norm),
                 final_norm=(final_norm, m_final_norm, v_final_norm),
                 dn_conv_w=(conv_full(dn_conv_w), conv_full(m_dn_conv_w), conv_full(v_dn_conv_w)))
    packed_wmv = [_pack([given[n][k] for n in SMALL_NAMES]) for k in range(3)]
    small_out = adam_small(small_parts, *packed_wmv, "adam_small")
    shapes = [small_shapes[n] for n in SMALL_NAMES]
    small_res = {n: [] for n in SMALL_NAMES}
    for arr in small_out:
        for n, v_ in zip(SMALL_NAMES, _unpack(arr, shapes)):
            if n == "dn_conv_w":
                v_ = lax.dynamic_slice(v_, (0, 0, me * (3 * DN_W // N_DEV)), (DEPTH, DN_CONV, 3 * DN_W // N_DEV))
            small_res[n].append(v_)

    def parts_of(group, idx):
        return [big_parts[l][group][idx] for l in range(DEPTH)]

    big_res = dict(
        ffn1_w_gate=adam_shard(*parts_of("f1", 0), ffn1_w_gate, m_ffn1_w_gate, v_ffn1_w_gate, "adam_ffn1_gate", 0),
        ffn1_w_up=adam_shard(*parts_of("f1", 0), ffn1_w_up, m_ffn1_w_up, v_ffn1_w_up, "adam_ffn1_up", 1),
        ffn1_w_down=adam_shard(*parts_of("f1", 1), ffn1_w_down, m_ffn1_w_down, v_ffn1_w_down, "adam_ffn1_down"),
        ffn2_w_gate=adam_shard(*parts_of("f2", 0), ffn2_w_gate, m_ffn2_w_gate, v_ffn2_w_gate, "adam_ffn2_gate", 0),
        ffn2_w_up=adam_shard(*parts_of("f2", 0), ffn2_w_up, m_ffn2_w_up, v_ffn2_w_up, "adam_ffn2_up", 1),
        ffn2_w_down=adam_shard(*parts_of("f2", 1), ffn2_w_down, m_ffn2_w_down, v_ffn2_w_down, "adam_ffn2_down"),
        w_in=adam_shard(*parts_of("io", 0), w_in, m_w_in, v_w_in, "adam_w_in"),
        w_out=adam_shard(*parts_of("io", 1), w_out, m_w_out, v_w_out, "adam_w_out"),
    )

    order = ("ffn1_norm", "ffn1_w_gate", "ffn1_w_up", "ffn1_w_down", "mix_norm", "w_in", "pool_w", "pool_scale",
             "dn_conv_w", "dn_a_log", "dn_dt_bias", "dn_out_norm", "w_out", "ffn2_norm", "ffn2_w_gate", "ffn2_w_up",
             "ffn2_w_down", "final_norm")
    res = {**small_res, **big_res}
    outs = [loss, grad_x]
    for k in range(4):
        outs.extend(res[n][k] for n in order)
    return tuple(outs)
```

```python
import functools
import math

import jax
import jax.numpy as jnp
from jax import lax
from jax.experimental import pallas as pl
from jax.experimental.pallas import tpu as pltpu

F32 = jnp.float32
BF16 = jnp.bfloat16
HI = lax.Precision.HIGHEST
INV_PREC = lax.Precision.HIGH
SDS = jax.ShapeDtypeStruct

N_DEV = 8
SEQ = 4096
D_MODEL = 1024
DEPTH = 2
D_FF = 2816
FF_BLK = D_FF // N_DEV
ATT_W = 256
ATT_E = 64
ATT_BLK = 128
DILATIONS = (1, 4, 16)
POOL_W = 256
POOL_HALO = 16
DN_W = 512
DN_H = 4
DN_E = 128
DN_C = 64
N_CHUNK = SEQ // DN_C
IN_W = 3080
IN_BLK = IN_W // N_DEV
EPS = 1e-6
EXT_ATT = 1280
EXT_REST = 3328
EXT_W = EXT_ATT + EXT_REST
R_DQKV, R_DZ, R_BB, R_AB, R_PU = 0, 1536, 2048, 2560, 3072

ADAM_LR, ADAM_B1, ADAM_B2, ADAM_EPS, ADAM_WD, ADAM_STEP = 0.001, 0.9, 0.999, 1e-08, 0.01, 10

VMEM_LIMIT = 60 * 1024 * 1024
MESH = pl.DeviceIdType.MESH


def _cparams(sem=None):
    kw = dict(vmem_limit_bytes=VMEM_LIMIT)
    if sem is not None:
        kw["dimension_semantics"] = sem
    return pltpu.CompilerParams(**kw)


def _dot(a, b, prec=None):
    return jnp.dot(a, b, preferred_element_type=F32, precision=prec)


def _dot_nt(a, b, prec=None):
    return lax.dot_general(a, b, (((1,), (1,)), ((), ())), preferred_element_type=F32, precision=prec)


def _dot_tn(a, b, prec=None):
    return lax.dot_general(a, b, (((0,), (0,)), ((), ())), preferred_element_type=F32, precision=prec)


def _sigmoid(x):
    return jax.nn.sigmoid(x)


def _rms_stats(x):
    r = lax.rsqrt(jnp.mean(x * x, axis=-1, keepdims=True) + EPS)
    return x * r, r


def _rms_bwd(xh, r, w, dh):
    dxh = dh * w
    dx = r * (dxh - xh * jnp.mean(dxh * xh, axis=-1, keepdims=True))
    return dx, jnp.sum(dh * xh, axis=0, keepdims=True)


FFN_T_FWD = 1024
FFN_T_BWD = 512
FF_TILE = 256
N_FF_TILE = D_FF // FF_TILE


def ffn_weights_from_shards(wgu_g, wd_g):
    return wgu_g.transpose(1, 2, 0, 3).reshape(2, D_MODEL, D_FF), wd_g.reshape(D_FF, D_MODEL)


def ffn_grads_to_shards(dwgu, dwd):
    return dwgu.reshape(2, D_MODEL, N_DEV, FF_BLK).transpose(2, 0, 1, 3), dwd.reshape(N_DEV, FF_BLK, D_MODEL)


def ffn_fwd(x, nw, wgu, wd, name, exch=None):
    t = FFN_T_FWD

    def body(x_ref, nw_ref, wgu_ref, wd_ref, o_ref, h_scr, acc_scr):
        k = pl.program_id(1)

        @pl.when(k == 0)
        def _():
            xh, _r = _rms_stats(x_ref[...])
            h_scr[...] = (xh * nw_ref[...]).astype(BF16)
            acc_scr[...] = jnp.zeros_like(acc_scr)

        h = h_scr[...]
        hg = _dot(h, wgu_ref[0])
        hu = _dot(h, wgu_ref[1])
        a = (hg * _sigmoid(hg) * hu).astype(BF16)
        acc_scr[...] += _dot(a, wd_ref[...])

        @pl.when(k == N_FF_TILE - 1)
        def _():
            o_ref[...] = x_ref[...] + 0.5 * acc_scr[...]

    return _call(
        body, name=name, grid=(SEQ // t, N_FF_TILE),
        in_specs=[pl.BlockSpec((t, D_MODEL), lambda i, k: (i, 0)),
                  pl.BlockSpec((1, D_MODEL), lambda i, k: (0, 0)),
                  pl.BlockSpec((2, D_MODEL, FF_TILE), lambda i, k: (0, 0, k)),
                  pl.BlockSpec((FF_TILE, D_MODEL), lambda i, k: (k, 0))],
        out_specs=pl.BlockSpec((t, D_MODEL), lambda i, k: (i, 0)),
        out_shape=SDS((SEQ, D_MODEL), F32),
        scratch_shapes=[pltpu.VMEM((t, D_MODEL), BF16), pltpu.VMEM((t, D_MODEL), F32)],
        sem=("arbitrary", "arbitrary"), args=(x, nw, wgu, wd), exch=exch)


def ffn_bwd(x, dxo, nw, wgu, wd, name, exch=None):
    t = FFN_T_BWD
    nt = SEQ // t

    def body(x_ref, dxo_ref, nw_ref, wgu_ref, wd_ref, dx_ref, dwgu_ref, dwd_ref, dnw_ref,
             dh_scr, ag_scr, au_scr, ad_scr, h_scr):
        k = pl.program_id(0)
        i = pl.program_id(1)
        rows = pl.ds(pl.multiple_of(i * t, t), t)
        nw_v = nw_ref[...]

        @pl.when(k == 0)
        def _():
            xh0, _r0 = _rms_stats(x_ref[...])
            h_scr[rows, :] = (xh0 * nw_v).astype(BF16)

        h = h_scr[rows, :]
        dy = (0.5 * dxo_ref[...]).astype(BF16)
        wg = wgu_ref[0]
        wu = wgu_ref[1]
        hg = _dot(h, wg)
        hu = _dot(h, wu)
        sg = _sigmoid(hg)
        sil = hg * sg
        a = (sil * hu).astype(BF16)
        da = _dot_nt(dy, wd_ref[...])
        dhu = (da * sil).astype(BF16)
        dhg = (da * hu * (sg * (1.0 + hg * (1.0 - sg)))).astype(BF16)
        p_d = _dot_tn(a, dy)
        p_g = _dot_tn(h, dhg)
        p_u = _dot_tn(h, dhu)
        dh = _dot_nt(dhg, wg) + _dot_nt(dhu, wu)

        @pl.when(i == 0)
        def _():
            ad_scr[...] = p_d
            ag_scr[...] = p_g
            au_scr[...] = p_u

        @pl.when(i > 0)
        def _():
            ad_scr[...] += p_d
            ag_scr[...] += p_g
            au_scr[...] += p_u

        @pl.when(i == nt - 1)
        def _():
            dwd_ref[...] = ad_scr[...].astype(BF16)
            dwgu_ref[0] = ag_scr[...].astype(BF16)
            dwgu_ref[1] = au_scr[...].astype(BF16)

        @pl.when(k == 0)
        def _():
            dh_scr[rows, :] = dh

        @pl.when(k > 0)
        def _():
            dh_scr[rows, :] += dh

        @pl.when(jnp.logical_and(k == 0, i == 0))
        def _():
            dnw_ref[...] = jnp.zeros_like(dnw_ref)

        @pl.when(k == N_FF_TILE - 1)
        def _():
            xh, r = _rms_stats(x_ref[...])
            dx, dw = _rms_bwd(xh, r, nw_v, dh_scr[rows, :])
            dx_ref[...] = dxo_ref[...] + dx
            dnw_ref[...] += dw

    last = N_FF_TILE - 1
    return _call(
        body, name=name, grid=(N_FF_TILE, nt),
        in_specs=[pl.BlockSpec((t, D_MODEL), lambda k, i: (i, 0)),
                  pl.BlockSpec((t, D_MODEL), lambda k, i: (i, 0)),
                  pl.BlockSpec((1, D_MODEL), lambda k, i: (0, 0)),
                  pl.BlockSpec((2, D_MODEL, FF_TILE), lambda k, i: (0, 0, k)),
                  pl.BlockSpec((FF_TILE, D_MODEL), lambda k, i: (k, 0))],
        out_specs=[pl.BlockSpec((t, D_MODEL), lambda k, i: (jnp.where(k == last, i, 0), 0)),
                   pl.BlockSpec((2, D_MODEL, FF_TILE), lambda k, i: (0, 0, k)),
                   pl.BlockSpec((FF_TILE, D_MODEL), lambda k, i: (k, 0)),
                   pl.BlockSpec((1, D_MODEL), lambda k, i: (0, 0))],
        out_shape=[SDS((SEQ, D_MODEL), F32), SDS((2, D_MODEL, D_FF), BF16),
                   SDS((D_FF, D_MODEL), BF16), SDS((1, D_MODEL), F32)],
        scratch_shapes=[pltpu.VMEM((SEQ, D_MODEL), F32), pltpu.VMEM((D_MODEL, FF_TILE), F32),
                        pltpu.VMEM((D_MODEL, FF_TILE), F32), pltpu.VMEM((FF_TILE, D_MODEL), F32),
                        pltpu.VMEM((SEQ, D_MODEL), BF16)],
        sem=("arbitrary", "arbitrary"), args=(x, dxo, nw, wgu, wd), exch=exch)


def loss_head(x, fw, target, name):
    t = 512

    def body(x_ref, fw_ref, tg_ref, loss_ref, dx_ref, dfw_ref):
        i = pl.program_id(0)
        xh, r = _rms_stats(x_ref[...])
        w = fw_ref[...]
        err = xh * w - tg_ref[...]
        part = 0.5 * jnp.sum(jnp.sum(err * err, axis=-1, keepdims=True), axis=0, keepdims=True) / D_MODEL
        dx, dw = _rms_bwd(xh, r, w, err * (1.0 / D_MODEL))
        dx_ref[...] = dx

        @pl.when(i == 0)
        def _():
            loss_ref[...] = jnp.zeros_like(loss_ref)
            dfw_ref[...] = jnp.zeros_like(dfw_ref)

        loss_ref[...] += jnp.broadcast_to(part, loss_ref.shape)
        dfw_ref[...] += dw

    return pl.pallas_call(
        body, name=name, grid=(SEQ // t,),
        in_specs=[pl.BlockSpec((t, D_MODEL), lambda i: (i, 0)),
                  pl.BlockSpec((1, D_MODEL), lambda i: (0, 0)),
                  pl.BlockSpec((t, D_MODEL), lambda i: (i, 0))],
        out_specs=[pl.BlockSpec((1, 128), lambda i: (0, 0)),
                   pl.BlockSpec((t, D_MODEL), lambda i: (i, 0)),
                   pl.BlockSpec((1, D_MODEL), lambda i: (0, 0))],
        out_shape=[SDS((1, 128), F32), SDS((SEQ, D_MODEL), F32), SDS((1, D_MODEL), F32)],
        compiler_params=_cparams(("arbitrary",)),
    )(x, fw, target)


MIX_T = 256


def mix_in_fwd(x, nw, wext, cos, sin, name):
    t = MIX_T

    def body(x_ref, nw_ref, w_ref, cos_ref, sin_ref, att_ref, rest_ref):
        xh, _r = _rms_stats(x_ref[...])
        h = (xh * nw_ref[...]).astype(BF16)
        pa = _dot(h, w_ref[:, 0:EXT_ATT])
        c = cos_ref[...]
        s = sin_ref[...]
        att_ref[:, 0:256] = pa[:, 0:256] * c + pa[:, 768:1024] * s
        att_ref[:, 256:512] = pa[:, 256:512] * c + pa[:, 1024:1280] * s
        att_ref[:, 512:768] = pa[:, 512:768]
        for j in range(EXT_REST // 256):
            rest_ref[:, 256 * j:256 * j + 256] = _dot(h, w_ref[:, EXT_ATT + 256 * j:EXT_ATT + 256 * j + 256])

    return pl.pallas_call(
        body, name=name, grid=(SEQ // t,),
        in_specs=[pl.BlockSpec((t, D_MODEL), lambda i: (i, 0)),
                  pl.BlockSpec((1, D_MODEL), lambda i: (0, 0)),
                  pl.BlockSpec((D_MODEL, EXT_W), lambda i: (0, 0)),
                  pl.BlockSpec((t, ATT_W), lambda i: (i, 0)),
                  pl.BlockSpec((t, ATT_W), lambda i: (i, 0))],
        out_specs=[pl.BlockSpec((t, 768), lambda i: (i, 0)),
                   pl.BlockSpec((t, EXT_REST), lambda i: (i, 0))],
        out_shape=[SDS((SEQ, 768), F32), SDS((SEQ, EXT_REST), F32)],
        compiler_params=_cparams(("arbitrary",)),
    )(x, nw, wext, cos, sin)


def assemble_dproj(datts, cos, sin, d_dqkv, dz, dbb, dab, dpu, name):
    t = 512

    def body(d1_ref, d4_ref, d16_ref, cos_ref, sin_ref, dqkv_ref, dz_ref, dbb_ref, dab_ref, dpu_ref, o_ref):
        da = d1_ref[...] + d4_ref[...] + d16_ref[...]
        c = cos_ref[...]
        s = sin_ref[...]
        dq = da[:, 0:256]
        dk = da[:, 256:512]
        o_ref[:, 0:256] = (dq * c).astype(BF16)
        o_ref[:, 256:512] = (dk * c).astype(BF16)
        o_ref[:, 512:768] = da[:, 512:768].astype(BF16)
        o_ref[:, 768:1024] = (dq * s).astype(BF16)
        o_ref[:, 1024:1280] = (dk * s).astype(BF16)
        b = EXT_ATT
        o_ref[:, b + R_DQKV:b + R_DQKV + 1536] = dqkv_ref[...].astype(BF16)
        o_ref[:, b + R_DZ:b + R_DZ + 512] = dz_ref[...].astype(BF16)
        o_ref[:, b + R_BB:b + R_BB + 512] = dbb_ref[...].astype(BF16)
        o_ref[:, b + R_AB:b + R_AB + 512] = dab_ref[...].astype(BF16)
        o_ref[:, b + R_PU:b + R_PU + 256] = dpu_ref[...].astype(BF16)

    row = lambda w: pl.BlockSpec((t, w), lambda i: (i, 0))
    return pl.pallas_call(
        body, name=name, grid=(SEQ // t,),
        in_specs=[row(768), row(768), row(768), row(256), row(256), row(1536), row(512), row(512), row(512), row(256)],
        out_specs=row(EXT_W),
        out_shape=SDS((SEQ, EXT_W), BF16),
        compiler_params=_cparams(("arbitrary",)),
    )(*datts, cos, sin, d_dqkv, dz, dbb, dab, dpu)


def linear_bwd(x, dxo, nw, dy, w, name):
    t = 512
    nb = 768
    n = w.shape[1]
    nt = SEQ // t
    nn = n // nb

    def body(x_ref, dxo_ref, nw_ref, dy_ref, w_ref, dx_ref, dw_ref, dnw_ref, dh_scr, h_scr):
        k = pl.program_id(0)
        i = pl.program_id(1)
        rows = pl.ds(pl.multiple_of(i * t, t), t)
        nw_v = nw_ref[...]

        @pl.when(k == 0)
        def _():
            xh0, _r0 = _rms_stats(x_ref[...])
            h_scr[rows, :] = (xh0 * nw_v).astype(BF16)

        h = h_scr[rows, :]
        dyv = dy_ref[...]
        p_w = _dot_tn(h, dyv)
        dh = _dot_nt(dyv, w_ref[...])

        @pl.when(i == 0)
        def _():
            dw_ref[...] = p_w

        @pl.when(i > 0)
        def _():
            dw_ref[...] += p_w

        @pl.when(k == 0)
        def _():
            dh_scr[rows, :] = dh

        @pl.when(k > 0)
        def _():
            dh_scr[rows, :] += dh

        @pl.when(jnp.logical_and(k == 0, i == 0))
        def _():
            dnw_ref[...] = jnp.zeros_like(dnw_ref)

        @pl.when(k == nn - 1)
        def _():
            xh, r = _rms_stats(x_ref[...])
            dx, dw = _rms_bwd(xh, r, nw_v, dh_scr[rows, :])
            dx_ref[...] = dxo_ref[...] + dx
            dnw_ref[...] += dw

    last = nn - 1
    return pl.pallas_call(
        body, name=name, grid=(nn, nt),
        in_specs=[pl.BlockSpec((t, D_MODEL), lambda k, i: (i, 0)),
                  pl.BlockSpec((t, D_MODEL), lambda k, i: (i, 0)),
                  pl.BlockSpec((1, D_MODEL), lambda k, i: (0, 0)),
                  pl.BlockSpec((t, nb), lambda k, i: (i, k)),
                  pl.BlockSpec((D_MODEL, nb), lambda k, i: (0, k))],
        out_specs=[pl.BlockSpec((t, D_MODEL), lambda k, i: (jnp.where(k == last, i, 0), 0)),
                   pl.BlockSpec((D_MODEL, nb), lambda k, i: (0, k)),
                   pl.BlockSpec((1, D_MODEL), lambda k, i: (0, 0))],
        out_shape=[SDS((SEQ, D_MODEL), F32), SDS((D_MODEL, n), F32), SDS((1, D_MODEL), F32)],
        scratch_shapes=[pltpu.VMEM((SEQ, D_MODEL), F32), pltpu.VMEM((SEQ, D_MODEL), BF16)],
        compiler_params=_cparams(("arbitrary", "arbitrary")),
    )(x, dxo, nw, dy, w)


def _att_masks():
    qi = lax.broadcasted_iota(jnp.int32, (ATT_BLK, ATT_BLK), 0)
    ki = lax.broadcasted_iota(jnp.int32, (ATT_BLK, ATT_BLK), 1)
    return ki <= qi, ki >= qi


NEG = -1e30


def _att_heads(ref, base):
    return jnp.stack([ref[:, base + ATT_E * hd:base + ATT_E * hd + ATT_E] for hd in range(4)], axis=0)


def att_fwd(att, blocks_per_class, name):
    nblk = SEQ // ATT_BLK

    def body(cur_ref, prev_ref, o_ref):
        i = pl.program_id(0)
        has_prev = (i % blocks_per_class) != 0
        m_d, m_p = _att_masks()
        m_p = jnp.logical_and(m_p, has_prev)
        q = _att_heads(cur_ref, 0).astype(BF16)
        kc = _att_heads(cur_ref, 256).astype(BF16)
        vc = _att_heads(cur_ref, 512).astype(BF16)
        kp = _att_heads(prev_ref, 256).astype(BF16)
        vp = _att_heads(prev_ref, 512).astype(BF16)
        sd = jnp.where(m_d, _bdot(q, kc, 2, 2) * 0.125, NEG)
        sp = jnp.where(m_p, _bdot(q, kp, 2, 2) * 0.125, NEG)
        m = jnp.maximum(jnp.max(sd, axis=-1, keepdims=True), jnp.max(sp, axis=-1, keepdims=True))
        pd = jnp.exp(sd - m)
        pp = jnp.exp(sp - m)
        den = jnp.sum(pd, axis=-1, keepdims=True) + jnp.sum(pp, axis=-1, keepdims=True)
        inv = 1.0 / den
        o = _bdot((pd * inv).astype(BF16), vc, 2, 1) + _bdot((pp * inv).astype(BF16), vp, 2, 1)
        lse = m + jnp.log(den)
        for hd in range(4):
            o_ref[:, ATT_E * hd:ATT_E * hd + ATT_E] = o[hd]
            o_ref[:, 256 + ATT_E * hd:256 + ATT_E * hd + ATT_E] = jnp.broadcast_to(lse[hd], (ATT_BLK, ATT_E))

    return pl.pallas_call(
        body, name=name, grid=(nblk,),
        in_specs=[pl.BlockSpec((ATT_BLK, 768), lambda i: (i, 0)),
                  pl.BlockSpec((ATT_BLK, 768), lambda i: (jnp.maximum(i - 1, 0), 0))],
        out_specs=pl.BlockSpec((ATT_BLK, 512), lambda i: (i, 0)),
        out_shape=SDS((SEQ, 512), F32),
        compiler_params=_cparams(("arbitrary",)),
    )(att, att)


def att_bwd(att, ol, dol, blocks_per_class, name):
    nblk = SEQ // ATT_BLK

    def body(prev_ref, cur_ref, nxt_ref, ol_c_ref, ol_n_ref, dol_c_ref, dol_n_ref, d_ref):
        i = pl.program_id(0)
        has_prev = (i % blocks_per_class) != 0
        has_next = ((i + 1) % blocks_per_class) != 0
        m_d, m_band = _att_masks()
        m_p = jnp.logical_and(m_band, has_prev)
        m_n = jnp.logical_and(m_band, has_next)

        def pair(q, k, v, lse, do, dterm, mask):
            s = jnp.where(mask, _bdot(q, k, 2, 2) * 0.125, NEG)
            p = jnp.exp(s - lse)
            dp = _bdot(do, v, 2, 2)
            ds = (p * (dp + dterm) * 0.125).astype(BF16)
            return p.astype(BF16), ds

        q_c = _att_heads(cur_ref, 0).astype(BF16)
        k_c = _att_heads(cur_ref, 256).astype(BF16)
        v_c = _att_heads(cur_ref, 512).astype(BF16)
        k_p = _att_heads(prev_ref, 256).astype(BF16)
        v_p = _att_heads(prev_ref, 512).astype(BF16)
        q_n = _att_heads(nxt_ref, 0).astype(BF16)
        o_c = _att_heads(ol_c_ref, 0)
        o_n = _att_heads(ol_n_ref, 0)
        lse_c = _att_heads(ol_c_ref, 256)[:, :, 0:1]
        lse_n = _att_heads(ol_n_ref, 256)[:, :, 0:1]
        do_c = _att_heads(dol_c_ref, 0)
        do_n = _att_heads(dol_n_ref, 0)
        t_c = _att_heads(dol_c_ref, 256)[:, :, 0:1] - jnp.sum(do_c * o_c, axis=-1, keepdims=True)
        t_n = _att_heads(dol_n_ref, 256)[:, :, 0:1] - jnp.sum(do_n * o_n, axis=-1, keepdims=True)
        do_cb = do_c.astype(BF16)
        do_nb = do_n.astype(BF16)
        p1, ds1 = pair(q_c, k_c, v_c, lse_c, do_cb, t_c, m_d)
        _p2, ds2 = pair(q_c, k_p, v_p, lse_c, do_cb, t_c, m_p)
        p3, ds3 = pair(q_n, k_c, v_c, lse_n, do_nb, t_n, m_n)
        dq = _bdot(ds1, k_c, 2, 1) + _bdot(ds2, k_p, 2, 1)
        dk = _bdot(ds1, q_c, 1, 1) + _bdot(ds3, q_n, 1, 1)
        dv = _bdot(p1, do_cb, 1, 1) + _bdot(p3, do_nb, 1, 1)
        for hd in range(4):
            a = ATT_E * hd
            d_ref[:, a:a + ATT_E] = dq[hd]
            d_ref[:, 256 + a:256 + a + ATT_E] = dk[hd]
            d_ref[:, 512 + a:512 + a + ATT_E] = dv[hd]

    prv = lambda i: (jnp.maximum(i - 1, 0), 0)
    cur = lambda i: (i, 0)
    nxt = lambda i: (jnp.minimum(i + 1, nblk - 1), 0)
    return pl.pallas_call(
        body, name=name, grid=(nblk,),
        in_specs=[pl.BlockSpec((ATT_BLK, 768), prv), pl.BlockSpec((ATT_BLK, 768), cur),
                  pl.BlockSpec((ATT_BLK, 768), nxt),
                  pl.BlockSpec((ATT_BLK, 512), cur), pl.BlockSpec((ATT_BLK, 512), nxt),
                  pl.BlockSpec((ATT_BLK, 512), cur), pl.BlockSpec((ATT_BLK, 512), nxt)],
        out_specs=pl.BlockSpec((ATT_BLK, 768), cur),
        out_shape=SDS((SEQ, 768), F32),
        compiler_params=_cparams(("arbitrary",)),
    )(att, att, att, ol, ol, dol, dol)


def to_classes(a, d):
    if d == 1:
        return a
    w = a.shape[1]
    return a.reshape(SEQ // d, d, w).transpose(1, 0, 2).reshape(SEQ, w)


def from_classes(a, d):
    if d == 1:
        return a
    w = a.shape[1]
    return a.reshape(d, SEQ // d, w).transpose(1, 0, 2).reshape(SEQ, w)


def _shift_down(x, k):
    rows = lax.broadcasted_iota(jnp.int32, x.shape, 0)
    return jnp.where(rows >= k, pltpu.roll(x, k, 0), 0.0)


def _shift_up(x, k):
    n = x.shape[0]
    rows = lax.broadcasted_iota(jnp.int32, x.shape, 0)
    return jnp.where(rows < n - k, pltpu.roll(x, n - k, 0), 0.0)


@functools.partial(jax.custom_vjp, nondiff_argnums=(1,))
def _delay(x, k):
    return _shift_down(x, k)


def _delay_fwd(x, k):
    return _shift_down(x, k), None


def _delay_bwd(k, _res, g):
    return (_shift_up(g, k),)


_delay.defvjp(_delay_fwd, _delay_bwd)

DN_CONV = 4


def _dn_prep_fn(u, w, kind):
    y = w[DN_CONV - 1:DN_CONV] * u
    for j in range(DN_CONV - 1):
        y = y + w[j:j + 1] * _delay(u, DN_CONV - 1 - j)
    y = y * _sigmoid(y)
    nrm = y * lax.rsqrt(jnp.sum(y * y, axis=-1, keepdims=True) + EPS)
    return jnp.where(kind == 0, nrm * (DN_E ** -0.5), jnp.where(kind == 1, nrm, y))


def dn_prep_fwd(rest, conv_w, name):
    def body(u_ref, w_ref, o_ref):
        j = pl.program_id(0)
        kind = (j >= DN_H).astype(jnp.int32) + (j >= 2 * DN_H).astype(jnp.int32)
        o_ref[...] = _dn_prep_fn(u_ref[...], w_ref[...], kind)

    return pl.pallas_call(
        body, name=name, grid=(3 * DN_H,),
        in_specs=[pl.BlockSpec((SEQ, DN_E), lambda j: (0, j)),
                  pl.BlockSpec((DN_CONV, DN_E), lambda j: (0, j))],
        out_specs=pl.BlockSpec((SEQ, DN_E), lambda j: (0, j)),
        out_shape=SDS((SEQ, 3 * DN_W), F32),
        compiler_params=_cparams(("arbitrary",)),
    )(rest, conv_w)


def dn_prep_bwd(rest, conv_w, dqkv, name):
    def body(u_ref, w_ref, g_ref, du_ref, dw_ref):
        j = pl.program_id(0)
        kind = (j >= DN_H).astype(jnp.int32) + (j >= 2 * DN_H).astype(jnp.int32)
        _y, vjp = jax.vjp(lambda u, w: _dn_prep_fn(u, w, kind), u_ref[...], w_ref[...])
        du, dw = vjp(g_ref[...])
        du_ref[...] = du
        dw_ref[...] = dw

    return pl.pallas_call(
        body, name=name, grid=(3 * DN_H,),
        in_specs=[pl.BlockSpec((SEQ, DN_E), lambda j: (0, j)),
                  pl.BlockSpec((DN_CONV, DN_E), lambda j: (0, j)),
                  pl.BlockSpec((SEQ, DN_E), lambda j: (0, j))],
        out_specs=[pl.BlockSpec((SEQ, DN_E), lambda j: (0, j)),
                   pl.BlockSpec((DN_CONV, DN_E), lambda j: (0, j))],
        out_shape=[SDS((SEQ, 3 * DN_W), F32), SDS((DN_CONV, 3 * DN_W), F32)],
        compiler_params=_cparams(("arbitrary",)),
    )(rest, conv_w, dqkv)


def _bdot(a, b, ca, cb, prec=None):
    return lax.dot_general(a, b, (((ca,), (cb,)), ((0,), (0,))), preferred_element_type=F32, precision=prec)


def _unit_lower_inverse(a):
    eye = (lax.broadcasted_iota(jnp.int32, (DN_C, DN_C), 0) == lax.broadcasted_iota(jnp.int32, (DN_C, DN_C), 1)).astype(F32)
    p = eye - a
    b = _bdot(a, a, 2, 1, INV_PREC)
    for lvl in range(5):
        p = p + _bdot(p, b, 2, 1, INV_PREC)
        if lvl < 4:
            b = _bdot(b, b, 2, 1, INV_PREC)
    return p


@jax.custom_vjp
def _tri_inv(a):
    return _unit_lower_inverse(a)


def _tri_inv_fwd(a):
    t = _unit_lower_inverse(a)
    return t, t


def _tri_inv_bwd(t, g):
    return (-_bdot(_bdot(t, g, 1, 1, INV_PREC), t, 2, 2, INV_PREC),)


_tri_inv.defvjp(_tri_inv_fwd, _tri_inv_bwd)


def _b16(x):
    return x.astype(BF16)


def _dn_chunk(q, k, v, bb, ab, alog, dtb, state):
    ri = lax.broadcasted_iota(jnp.int32, (DN_C, DN_C), 0)
    ci = lax.broadcasted_iota(jnp.int32, (DN_C, DN_C), 1)
    lower = ri >= ci
    strict = ri > ci
    nh = q.shape[0]
    beta = _sigmoid(bb)
    xg = ab + dtb
    softplus = jnp.maximum(xg, 0.0) + jnp.log(1.0 + jnp.exp(-jnp.abs(xg)))
    gi = -jnp.exp(alog) * softplus
    g = _bdot(jnp.broadcast_to(lower.astype(F32), (nh, DN_C, DN_C)), gi, 2, 1, HI)
    eg = jnp.exp(g)
    kb = k * beta
    vb = v * beta
    g_col = g[:, :, 0:DN_C]
    g_row = _bdot(jnp.full((nh, DN_C, DN_E), 1.0 / DN_E, F32), g, 2, 2, HI)
    decay = jnp.where(lower, jnp.exp(jnp.where(lower, g_col - g_row, 0.0)), 0.0)
    kbf = _b16(k)
    a = jnp.where(strict, _bdot(_b16(kb), kbf, 2, 2) * decay, 0.0)
    t = _tri_inv(a)
    tb = _b16(t)
    u = _bdot(tb, _b16(vb), 2, 1)
    w = _bdot(tb, _b16(kb * eg), 2, 1)
    intra = jnp.where(lower, _bdot(_b16(q), kbf, 2, 2) * decay, 0.0)
    sb = _b16(state)
    v_new = u - _bdot(_b16(w), sb, 2, 1)
    o = _bdot(_b16(q * eg), sb, 2, 1) + _bdot(_b16(intra), _b16(v_new), 2, 1)
    g_last = g[:, DN_C - 1:DN_C, :]
    k_dec = k * jnp.exp(g_last - g)
    new_state = state * jnp.exp(g_last) + _bdot(_b16(k_dec), _b16(v_new), 1, 1)
    return o, new_state


def _heads(ref, base=0):
    return jnp.stack([ref[:, base + DN_E * hd:base + DN_E * hd + DN_E] for hd in range(DN_H)], axis=0)


def _put_heads(ref, val, base=0):
    for hd in range(DN_H):
        ref[:, base + DN_E * hd:base + DN_E * hd + DN_E] = val[hd]


def _dn_args(qkv_ref, bb_ref, ab_ref, alog_ref, dtb_ref, state):
    return (_heads(qkv_ref), _heads(qkv_ref, DN_W), _heads(qkv_ref, 2 * DN_W), _heads(bb_ref), _heads(ab_ref),
            _heads(alog_ref), _heads(dtb_ref), state)


def dn_chunk_fwd(qkv, rest, alog_b, dtb_b, name, exch=None):
    def body(qkv_ref, bb_ref, ab_ref, alog_ref, dtb_ref, o_ref, st_ref, state_scr):
        n = pl.program_id(0)

        @pl.when(n == 0)
        def _():
            state_scr[...] = jnp.zeros_like(state_scr)

        st = state_scr[...]
        st_ref[0] = st
        o, ns = _dn_chunk(*_dn_args(qkv_ref, bb_ref, ab_ref, alog_ref, dtb_ref, st))
        _put_heads(o_ref, o)
        state_scr[...] = ns

    return _call(
        body, name=name, grid=(N_CHUNK,),
        in_specs=[pl.BlockSpec((DN_C, 3 * DN_W), lambda n: (n, 0)),
                  pl.BlockSpec((DN_C, DN_W), lambda n: (n, R_BB // DN_W)),
                  pl.BlockSpec((DN_C, DN_W), lambda n: (n, R_AB // DN_W)),
                  pl.BlockSpec((1, DN_W), lambda n: (0, 0)),
                  pl.BlockSpec((1, DN_W), lambda n: (0, 0))],
        out_specs=[pl.BlockSpec((DN_C, DN_W), lambda n: (n, 0)),
                   pl.BlockSpec((1, DN_H, DN_E, DN_E), lambda n: (n, 0, 0, 0))],
        out_shape=[SDS((SEQ, DN_W), F32), SDS((N_CHUNK, DN_H, DN_E, DN_E), F32)],
        scratch_shapes=[pltpu.VMEM((DN_H, DN_E, DN_E), F32)],
        sem=("arbitrary",), args=(qkv, rest, rest, alog_b, dtb_b), exch=exch)


def dn_chunk_bwd(qkv, rest, alog_b, dtb_b, states, do, name, exch=None):
    last = N_CHUNK - 1

    def body(qkv_ref, bb_ref, ab_ref, alog_ref, dtb_ref, st_ref, do_ref,
             dqkv_ref, dbb_ref, dab_ref, dalog_ref, ddtb_ref, dstate_scr):
        s = pl.program_id(0)

        @pl.when(s == 0)
        def _():
            dstate_scr[...] = jnp.zeros_like(dstate_scr)
            dalog_ref[...] = jnp.zeros_like(dalog_ref)
            ddtb_ref[...] = jnp.zeros_like(ddtb_ref)

        _out, vjp = jax.vjp(_dn_chunk, *_dn_args(qkv_ref, bb_ref, ab_ref, alog_ref, dtb_ref, st_ref[0]))
        dq, dk, dv, dbb, dab, dalog, ddtb, dst = vjp((_heads(do_ref), dstate_scr[...]))
        _put_heads(dqkv_ref, dq)
        _put_heads(dqkv_ref, dk, DN_W)
        _put_heads(dqkv_ref, dv, 2 * DN_W)
        _put_heads(dbb_ref, dbb)
        _put_heads(dab_ref, dab)
        _put_heads(dalog_ref, _heads(dalog_ref) + dalog)
        _put_heads(ddtb_ref, _heads(ddtb_ref) + ddtb)
        dstate_scr[...] = dst

    rev = lambda w: (lambda s: (last - s, w))
    return _call(
        body, name=name, grid=(N_CHUNK,),
        in_specs=[pl.BlockSpec((DN_C, 3 * DN_W), rev(0)),
                  pl.BlockSpec((DN_C, DN_W), rev(R_BB // DN_W)),
                  pl.BlockSpec((DN_C, DN_W), rev(R_AB // DN_W)),
                  pl.BlockSpec((1, DN_W), lambda s: (0, 0)),
                  pl.BlockSpec((1, DN_W), lambda s: (0, 0)),
                  pl.BlockSpec((1, DN_H, DN_E, DN_E), lambda s: (last - s, 0, 0, 0)),
                  pl.BlockSpec((DN_C, DN_W), rev(0))],
        out_specs=[pl.BlockSpec((DN_C, 3 * DN_W), rev(0)),
                   pl.BlockSpec((DN_C, DN_W), rev(0)),
                   pl.BlockSpec((DN_C, DN_W), rev(0)),
                   pl.BlockSpec((1, DN_W), lambda s: (0, 0)),
                   pl.BlockSpec((1, DN_W), lambda s: (0, 0))],
        out_shape=[SDS((SEQ, 3 * DN_W), F32), SDS((SEQ, DN_W), F32), SDS((SEQ, DN_W), F32),
                   SDS((1, DN_W), F32), SDS((1, DN_W), F32)],
        scratch_shapes=[pltpu.VMEM((DN_H, DN_E, DN_E), F32)],
        sem=("arbitrary",), args=(qkv, rest, rest, alog_b, dtb_b, states, do), exch=exch)


DN_G_LOG2 = 3
DN_G = 1 << DN_G_LOG2
N_INST = DN_G * DN_H


def _dn_intra(q, k, v, bb, ab, alog, dtb):
    ri = lax.broadcasted_iota(jnp.int32, (DN_C, DN_C), 0)
    ci = lax.broadcasted_iota(jnp.int32, (DN_C, DN_C), 1)
    lower = ri >= ci
    strict = ri > ci
    nh = q.shape[0]
    beta = _sigmoid(bb)
    xg = ab + dtb
    softplus = jnp.maximum(xg, 0.0) + jnp.log(1.0 + jnp.exp(-jnp.abs(xg)))
    gi = -jnp.exp(alog) * softplus
    g = _bdot(jnp.broadcast_to(lower.astype(F32), (nh, DN_C, DN_C)), gi, 2, 1, HI)
    eg = jnp.exp(g)
    kb = k * beta
    vb = v * beta
    g_col = g[:, :, 0:DN_C]
    g_row = _bdot(jnp.full((nh, DN_C, DN_E), 1.0 / DN_E, F32), g, 2, 2, HI)
    decay = jnp.where(lower, jnp.exp(jnp.where(lower, g_col - g_row, 0.0)), 0.0)
    kbf = _b16(k)
    a = jnp.where(strict, _bdot(_b16(kb), kbf, 2, 2) * decay, 0.0)
    tb = _b16(_tri_inv(a))
    u = _bdot(tb, _b16(vb), 2, 1)
    w = _bdot(tb, _b16(kb * eg), 2, 1)
    intra = jnp.where(lower, _bdot(_b16(q), kbf, 2, 2) * decay, 0.0)
    g_last = g[:, DN_C - 1:DN_C, :]
    return u, w, q * eg, k * jnp.exp(g_last - g), intra, jnp.exp(g_last)


def _dn_inter(u, w, qg, kdec, intra, egl, state):
    sb = _b16(state)
    v_new = u - _bdot(_b16(w), sb, 2, 1)
    o = _bdot(_b16(qg), sb, 2, 1) + _bdot(_b16(intra), _b16(v_new), 2, 1)
    return o, state * egl + _bdot(_b16(kdec), _b16(v_new), 1, 1)


def _inst(ref, base=0):
    per_head = [ref[:, base + DN_E * hd:base + DN_E * hd + DN_E].reshape(DN_G, DN_C, DN_E) for hd in range(DN_H)]
    return jnp.concatenate(per_head, axis=0)


def _inst_rows(ref):
    rows = [jnp.broadcast_to(ref[:, DN_E * hd:DN_E * hd + DN_E][None], (DN_G, 1, DN_E)) for hd in range(DN_H)]
    return jnp.concatenate(rows, axis=0)


def _put_inst(ref, val, width=DN_E, base=0):
    for hd in range(DN_H):
        ref[:, base + width * hd:base + width * hd + width] = val[DN_G * hd:DN_G * hd + DN_G].reshape(DN_G * DN_C, width)


def _intra_args(qkv_ref, bb_ref, ab_ref, alog_ref, dtb_ref):
    return (_inst(qkv_ref), _inst(qkv_ref, DN_W), _inst(qkv_ref, 2 * DN_W), _inst(bb_ref), _inst(ab_ref),
            _inst_rows(alog_ref), _inst_rows(dtb_ref))


def _intra_in_specs():
    t = DN_G * DN_C
    return [pl.BlockSpec((t, 3 * DN_W), lambda n: (n, 0)),
            pl.BlockSpec((t, DN_W), lambda n: (n, R_BB // DN_W)),
            pl.BlockSpec((t, DN_W), lambda n: (n, R_AB // DN_W)),
            pl.BlockSpec((1, DN_W), lambda n: (0, 0)),
            pl.BlockSpec((1, DN_W), lambda n: (0, 0))]


def dn_intra_fwd(qkv, rest, alog_b, dtb_b, name):
    t = DN_G * DN_C

    def body(qkv_ref, bb_ref, ab_ref, alog_ref, dtb_ref, u_ref, w_ref, qg_ref, kd_ref, in_ref, egl_ref):
        u, w, qg, kdec, intra, egl = _dn_intra(*_intra_args(qkv_ref, bb_ref, ab_ref, alog_ref, dtb_ref))
        _put_inst(u_ref, u)
        _put_inst(w_ref, w.astype(BF16))
        _put_inst(qg_ref, qg.astype(BF16))
        _put_inst(kd_ref, kdec.astype(BF16))
        _put_inst(in_ref, intra.astype(BF16), DN_C)
        for hd in range(DN_H):
            egl_ref[:, DN_E * hd:DN_E * hd + DN_E] = egl[DN_G * hd:DN_G * hd + DN_G].reshape(DN_G, DN_E)

    row = lambda w_: pl.BlockSpec((t, w_), lambda n: (n, 0))
    return pl.pallas_call(
        body, name=name, grid=(N_CHUNK // DN_G,), in_specs=_intra_in_specs(),
        out_specs=[row(DN_W), row(DN_W), row(DN_W), row(DN_W), row(DN_H * DN_C),
                   pl.BlockSpec((DN_G, DN_W), lambda n: (n, 0))],
        out_shape=[SDS((SEQ, DN_W), F32), SDS((SEQ, DN_W), BF16), SDS((SEQ, DN_W), BF16), SDS((SEQ, DN_W), BF16),
                   SDS((SEQ, DN_H * DN_C), BF16), SDS((N_CHUNK, DN_W), F32)],
        compiler_params=_cparams(("arbitrary",)),
    )(qkv, rest, rest, alog_b, dtb_b)


def dn_intra_bwd(qkv, rest, alog_b, dtb_b, du, dw, dqg, dkd, dintra, degl, name):
    t = DN_G * DN_C

    def body(qkv_ref, bb_ref, ab_ref, alog_ref, dtb_ref, du_ref, dw_ref, dqg_ref, dkd_ref, din_ref, degl_ref,
             dqkv_ref, dbb_ref, dab_ref, dalog_ref, ddtb_ref):
        @pl.when(pl.program_id(0) == 0)
        def _():
            dalog_ref[...] = jnp.zeros_like(dalog_ref)
            ddtb_ref[...] = jnp.zeros_like(ddtb_ref)

        _out, vjp = jax.vjp(_dn_intra, *_intra_args(qkv_ref, bb_ref, ab_ref, alog_ref, dtb_ref))
        d_in = jnp.concatenate([din_ref[:, DN_C * hd:DN_C * hd + DN_C].reshape(DN_G, DN_C, DN_C) for hd in range(DN_H)], axis=0)
        d_egl = jnp.concatenate([degl_ref[:, DN_E * hd:DN_E * hd + DN_E].reshape(DN_G, 1, DN_E) for hd in range(DN_H)], axis=0)
        dq, dk, dv, dbb, dab, dalog, ddtb = vjp((_inst(du_ref), _inst(dw_ref), _inst(dqg_ref), _inst(dkd_ref), d_in, d_egl))
        _put_inst(dqkv_ref, dq)
        _put_inst(dqkv_ref, dk, DN_E, DN_W)
        _put_inst(dqkv_ref, dv, DN_E, 2 * DN_W)
        _put_inst(dbb_ref, dbb)
        _put_inst(dab_ref, dab)
        for hd in range(DN_H):
            sl = slice(DN_E * hd, DN_E * hd + DN_E)
            dalog_ref[:, sl] += jnp.sum(dalog[DN_G * hd:DN_G * hd + DN_G], axis=0)
            ddtb_ref[:, sl] += jnp.sum(ddtb[DN_G * hd:DN_G * hd + DN_G], axis=0)

    row = lambda w_: pl.BlockSpec((t, w_), lambda n: (n, 0))
    acc = pl.BlockSpec((1, DN_W), lambda n: (0, 0))
    return pl.pallas_call(
        body, name=name, grid=(N_CHUNK // DN_G,),
        in_specs=_intra_in_specs() + [row(DN_W), row(DN_W), row(DN_W), row(DN_W), row(DN_H * DN_C),
                                      pl.BlockSpec((DN_G, DN_W), lambda n: (n, 0))],
        out_specs=[row(3 * DN_W), row(DN_W), row(DN_W), acc, acc],
        out_shape=[SDS((SEQ, 3 * DN_W), F32), SDS((SEQ, DN_W), F32), SDS((SEQ, DN_W), F32),
                   SDS((1, DN_W), F32), SDS((1, DN_W), F32)],
        compiler_params=_cparams(("arbitrary",)),
    )(qkv, rest, rest, alog_b, dtb_b, du, dw, dqg, dkd, dintra, degl)


def _inter_args(u_ref, w_ref, qg_ref, kd_ref, in_ref, egl_ref, n, state):
    f = lambda r: _heads(r).astype(F32)
    intra = jnp.stack([in_ref[:, DN_C * hd:DN_C * hd + DN_C] for hd in range(DN_H)], axis=0).astype(F32)
    egl = _heads(egl_ref.at[pl.ds(n & (DN_G - 1), 1), :])
    return f(u_ref), f(w_ref), f(qg_ref), f(kd_ref), intra, egl, state


def dn_inter_fwd(u, w, qg, kdec, intra, egl, name):
    def body(u_ref, w_ref, qg_ref, kd_ref, in_ref, egl_ref, o_ref, st_ref, state_scr):
        n = pl.program_id(0)

        @pl.when(n == 0)
        def _():
            state_scr[...] = jnp.zeros_like(state_scr)

        st = state_scr[...]
        st_ref[0] = st
        o, ns = _dn_inter(*_inter_args(u_ref, w_ref, qg_ref, kd_ref, in_ref, egl_ref, n, st))
        _put_heads(o_ref, o)
        state_scr[...] = ns

    row = lambda w_: pl.BlockSpec((DN_C, w_), lambda n: (n, 0))
    return pl.pallas_call(
        body, name=name, grid=(N_CHUNK,),
        in_specs=[row(DN_W), row(DN_W), row(DN_W), row(DN_W), row(DN_H * DN_C),
                  pl.BlockSpec((DN_G, DN_W), lambda n: (n >> DN_G_LOG2, 0))],
        out_specs=[row(DN_W), pl.BlockSpec((1, DN_H, DN_E, DN_E), lambda n: (n, 0, 0, 0))],
        out_shape=[SDS((SEQ, DN_W), F32), SDS((N_CHUNK, DN_H, DN_E, DN_E), F32)],
        scratch_shapes=[pltpu.VMEM((DN_H, DN_E, DN_E), F32)],
        compiler_params=_cparams(("arbitrary",)),
    )(u, w, qg, kdec, intra, egl)


def dn_inter_bwd(u, w, qg, kdec, intra, egl, states, do, name):
    last = N_CHUNK - 1

    def body(u_ref, w_ref, qg_ref, kd_ref, in_ref, egl_ref, st_ref, do_ref,
             du_ref, dw_ref, dqg_ref, dkd_ref, din_ref, degl_ref, dstate_scr):
        s = pl.program_id(0)
        n = last - s

        @pl.when(s == 0)
        def _():
            dstate_scr[...] = jnp.zeros_like(dstate_scr)

        _out, vjp = jax.vjp(_dn_inter, *_inter_args(u_ref, w_ref, qg_ref, kd_ref, in_ref, egl_ref, n, st_ref[0]))
        du, dw, dqg, dkd, din, degl, dst = vjp((_heads(do_ref), dstate_scr[...]))
        _put_heads(du_ref, du)
        _put_heads(dw_ref, dw)
        _put_heads(dqg_ref, dqg)
        _put_heads(dkd_ref, dkd)
        for hd in range(DN_H):
            din_ref[:, DN_C * hd:DN_C * hd + DN_C] = din[hd]
        row = n & (DN_G - 1)

        @pl.when(row == DN_G - 1)
        def _():
            degl_ref[...] = jnp.zeros_like(degl_ref)

        new_row = jnp.concatenate([degl[hd] for hd in range(DN_H)], axis=1)
        rows = lax.broadcasted_iota(jnp.int32, (DN_G, DN_W), 0)
        degl_ref[...] = jnp.where(rows == row, jnp.broadcast_to(new_row, (DN_G, DN_W)), degl_ref[...])
        dstate_scr[...] = dst

    rev = lambda w_: pl.BlockSpec((DN_C, w_), lambda s: (last - s, 0))
    grp = pl.BlockSpec((DN_G, DN_W), lambda s: ((last - s) >> DN_G_LOG2, 0))
    return pl.pallas_call(
        body, name=name, grid=(N_CHUNK,),
        in_specs=[rev(DN_W), rev(DN_W), rev(DN_W), rev(DN_W), rev(DN_H * DN_C), grp,
                  pl.BlockSpec((1, DN_H, DN_E, DN_E), lambda s: (last - s, 0, 0, 0)), rev(DN_W)],
        out_specs=[rev(DN_W), rev(DN_W), rev(DN_W), rev(DN_W), rev(DN_H * DN_C), grp],
        out_shape=[SDS((SEQ, DN_W), F32)] * 4 + [SDS((SEQ, DN_H * DN_C), F32), SDS((N_CHUNK, DN_W), F32)],
        scratch_shapes=[pltpu.VMEM((DN_H, DN_E, DN_E), F32)],
        compiler_params=_cparams(("arbitrary",)),
    )(u, w, qg, kdec, intra, egl, states, do)


OUT_T = 256


def _pool_consts(rows_total, t0, halo_before):
    lane = lax.broadcasted_iota(jnp.int32, (rows_total, POOL_W), 1)
    row = lax.broadcasted_iota(jnp.int32, (rows_total, POOL_W), 0)
    grp = (lane >= 64).astype(jnp.int32) + (lane >= 128).astype(jnp.int32) + (lane >= 192).astype(jnp.int32)
    win = jnp.where(grp == 0, 2, jnp.where(grp == 1, 4, jnp.where(grp == 2, 8, 16)))
    pos = t0 + row - halo_before
    cnt = jnp.minimum(pos + 1, win).astype(F32)
    return grp, cnt


def _pool_select(grp, s2, s4, s8, s16):
    return jnp.where(grp == 0, s2, jnp.where(grp == 1, s4, jnp.where(grp == 2, s8, s16)))


def _pooled(u_ext, t0):
    n = u_ext.shape[0]
    grp, cnt = _pool_consts(n, t0, POOL_HALO)
    s2 = u_ext + pltpu.roll(u_ext, 1, 0)
    s4 = s2 + pltpu.roll(s2, 2, 0)
    s8 = s4 + pltpu.roll(s4, 4, 0)
    s16 = s8 + pltpu.roll(s8, 8, 0)
    out = _pool_select(grp, s2, s4, s8, s16) / jnp.maximum(cnt, 1.0) - u_ext
    return out[POOL_HALO:, :]


def _merge_weights(l1, l4, l16):
    m = jnp.maximum(jnp.maximum(l1, l4), l16)
    e1 = jnp.exp(l1 - m)
    e4 = jnp.exp(l4 - m)
    e16 = jnp.exp(l16 - m)
    inv = 1.0 / (e1 + e4 + e16)
    return e1 * inv, e4 * inv, e16 * inv


def _out_parts(ol1_ref, ol4_ref, ol16_ref, pu_ref, puh_ref, odn_ref, z_ref, wbd_ref, i, t):
    w1, w4, w16 = _merge_weights(ol1_ref[:, 256:512], ol4_ref[:, 256:512], ol16_ref[:, 256:512])
    ya = w1 * ol1_ref[:, 0:256] + w4 * ol4_ref[:, 0:256] + w16 * ol16_ref[:, 0:256]
    halo = jnp.where(i > 0, puh_ref[...], 0.0)
    pooled = _pooled(jnp.concatenate([halo, pu_ref[...]], axis=0), i * t)
    pw = _dot(pooled.astype(BF16), wbd_ref[...])
    return ya, pooled, pw, (w1, w4, w16)


def _out_specs_common(t):
    def row(w, cb=0):
        return pl.BlockSpec((t, w), lambda i: (i, cb))

    halo = pl.BlockSpec((POOL_HALO, POOL_W),
                        lambda i: (jnp.maximum(i * (t // POOL_HALO) - 1, 0), R_PU // POOL_W))
    full = lambda a, b: pl.BlockSpec((a, b), lambda i: (0, 0))
    return [row(512), row(512), row(512), row(POOL_W, R_PU // POOL_W), halo, row(DN_W), row(DN_W, R_DZ // DN_W),
            full(POOL_W, POOL_W), full(1, POOL_W), full(1, DN_W), full(D_MODEL, D_MODEL)]


def mix_out_fwd(x, ol1, ol4, ol16, rest, odn, wbd, scale, onorm_b, wout, name):
    t = OUT_T

    def body(x_ref, ol1_ref, ol4_ref, ol16_ref, pu_ref, puh_ref, odn_ref, z_ref, wbd_ref, sc_ref, on_ref, wo_ref, o_ref):
        i = pl.program_id(0)
        ya, _pooled_v, pw, _w = _out_parts(ol1_ref, ol4_ref, ol16_ref, pu_ref, puh_ref, odn_ref, z_ref, wbd_ref, i, t)
        yb = pw * sc_ref[...]
        acc = x_ref[...] + _dot(ya.astype(BF16), wo_ref[0:256, :]) + _dot(yb.astype(BF16), wo_ref[256:512, :])
        for hd in range(DN_H):
            sl = slice(DN_E * hd, DN_E * hd + DN_E)
            oh, _r = _rms_stats(odn_ref[:, sl])
            z = z_ref[:, sl]
            yc = oh * on_ref[:, sl] * (z * _sigmoid(z))
            acc = acc + _dot(yc.astype(BF16), wo_ref[512 + DN_E * hd:512 + DN_E * hd + DN_E, :])
        o_ref[...] = acc

    return pl.pallas_call(
        body, name=name, grid=(SEQ // t,),
        in_specs=[pl.BlockSpec((t, D_MODEL), lambda i: (i, 0))] + _out_specs_common(t),
        out_specs=pl.BlockSpec((t, D_MODEL), lambda i: (i, 0)),
        out_shape=SDS((SEQ, D_MODEL), F32),
        compiler_params=_cparams(("arbitrary",)),
    )(x, ol1, ol4, ol16, rest, rest, odn, rest, wbd, scale, onorm_b, wout)


def mix_out_bwd(dxo, ol1, ol4, ol16, rest, odn, wbd, scale, onorm_b, wout, headsum, name):
    t = OUT_T

    def body(dxo_ref, ol1_ref, ol4_ref, ol16_ref, pu_ref, puh_ref, odn_ref, z_ref, wbd_ref, sc_ref, on_ref, wo_ref, hs_ref,
             dwo_ref, d1_ref, d4_ref, d16_ref, dpl_ref, dodn_ref, dz_ref, dsc_ref, don_ref, dwbd_ref):
        i = pl.program_id(0)

        @pl.when(i == 0)
        def _():
            dwo_ref[...] = jnp.zeros_like(dwo_ref)
            dsc_ref[...] = jnp.zeros_like(dsc_ref)
            don_ref[...] = jnp.zeros_like(don_ref)
            dwbd_ref[...] = jnp.zeros_like(dwbd_ref)

        ya, pooled, pw, (w1, w4, w16) = _out_parts(ol1_ref, ol4_ref, ol16_ref, pu_ref, puh_ref, odn_ref, z_ref, wbd_ref, i, t)
        sc = sc_ref[...]
        dxb = dxo_ref[...].astype(BF16)
        dwo_ref[0:256, :] += _dot_tn(ya.astype(BF16), dxb)
        dwo_ref[256:512, :] += _dot_tn((pw * sc).astype(BF16), dxb)
        dya = _dot_nt(dxb, wo_ref[0:256, :])
        o1 = ol1_ref[:, 0:256]
        o4 = ol4_ref[:, 0:256]
        o16 = ol16_ref[:, 0:256]
        hs = hs_ref[...]
        s1 = _dot(dya * o1, hs, HI)
        s4 = _dot(dya * o4, hs, HI)
        s16 = _dot(dya * o16, hs, HI)
        sbar = w1 * s1 + w4 * s4 + w16 * s16
        d1_ref[:, 0:256] = w1 * dya
        d1_ref[:, 256:512] = w1 * (s1 - sbar)
        d4_ref[:, 0:256] = w4 * dya
        d4_ref[:, 256:512] = w4 * (s4 - sbar)
        d16_ref[:, 0:256] = w16 * dya
        d16_ref[:, 256:512] = w16 * (s16 - sbar)
        dyb = _dot_nt(dxb, wo_ref[256:512, :])
        dsc_ref[...] += jnp.sum(dyb * pw, axis=0, keepdims=True)
        dpw = (dyb * sc).astype(BF16)
        dwbd_ref[...] += _dot_tn(pooled.astype(BF16), dpw)
        dpl_ref[...] = _dot_nt(dpw, wbd_ref[...])
        for hd in range(DN_H):
            sl = slice(DN_E * hd, DN_E * hd + DN_E)
            rows_w = slice(512 + DN_E * hd, 512 + DN_E * hd + DN_E)
            oh, r = _rms_stats(odn_ref[:, sl])
            z = z_ref[:, sl]
            sg = _sigmoid(z)
            sz = z * sg
            nw = on_ref[:, sl]
            on = oh * nw
            dwo_ref[rows_w, :] += _dot_tn((on * sz).astype(BF16), dxb)
            dyc = _dot_nt(dxb, wo_ref[rows_w, :])
            dz_ref[:, sl] = dyc * on * (sg * (1.0 + z * (1.0 - sg)))
            dx, dw = _rms_bwd(oh, r, nw, dyc * sz)
            dodn_ref[:, sl] = dx
            don_ref[:, sl] += dw

    row = lambda w: pl.BlockSpec((t, w), lambda i: (i, 0))
    full = lambda a, b: pl.BlockSpec((a, b), lambda i: (0, 0))
    return pl.pallas_call(
        body, name=name, grid=(SEQ // t,),
        in_specs=[row(D_MODEL)] + _out_specs_common(t) + [full(ATT_W, ATT_W)],
        out_specs=[full(D_MODEL, D_MODEL), row(512), row(512), row(512), row(POOL_W), row(DN_W), row(DN_W),
                   full(1, POOL_W), full(1, DN_W), full(POOL_W, POOL_W)],
        out_shape=[SDS((D_MODEL, D_MODEL), F32), SDS((SEQ, 512), F32), SDS((SEQ, 512), F32), SDS((SEQ, 512), F32),
                   SDS((SEQ, POOL_W), F32), SDS((SEQ, DN_W), F32), SDS((SEQ, DN_W), F32),
                   SDS((1, POOL_W), F32), SDS((1, DN_W), F32), SDS((POOL_W, POOL_W), F32)],
        compiler_params=_cparams(("arbitrary",)),
    )(dxo, ol1, ol4, ol16, rest, rest, odn, rest, wbd, scale, onorm_b, wout, headsum)


def pool_bwd(dpooled, name):
    t = 512
    nt = SEQ // t

    def body(d_ref, dn_ref, o_ref):
        i = pl.program_id(0)
        halo = jnp.where(i < nt - 1, dn_ref[...], 0.0)
        d_ext = jnp.concatenate([d_ref[...], halo], axis=0)
        n = t + POOL_HALO
        grp, cnt = _pool_consts(n, i * t, 0)
        dq = d_ext / cnt
        s2 = dq + pltpu.roll(dq, n - 1, 0)
        s4 = s2 + pltpu.roll(s2, n - 2, 0)
        s8 = s4 + pltpu.roll(s4, n - 4, 0)
        s16 = s8 + pltpu.roll(s8, n - 8, 0)
        o_ref[...] = (_pool_select(grp, s2, s4, s8, s16) - d_ext)[0:t, :]

    return pl.pallas_call(
        body, name=name, grid=(nt,),
        in_specs=[pl.BlockSpec((t, POOL_W), lambda i: (i, 0)),
                  pl.BlockSpec((POOL_HALO, POOL_W),
                               lambda i: (jnp.minimum((i + 1) * (t // POOL_HALO), SEQ // POOL_HALO - 1), 0))],
        out_specs=pl.BlockSpec((t, POOL_W), lambda i: (i, 0)),
        out_shape=SDS((SEQ, POOL_W), F32),
        compiler_params=_cparams(("arbitrary",)),
    )(dpooled, dpooled)


N_PEER = N_DEV - 1
ANY_SPEC = pl.BlockSpec(memory_space=pl.ANY)


class Exchange:
    def __init__(self, arrays, mode):
        self.arrays = list(arrays)
        self.mode = mode
        n = len(self.arrays)
        if mode == "scatter":
            self.out_shape = [SDS(a.shape, a.dtype) for a in self.arrays]
        else:
            self.out_shape = [SDS((N_DEV,) + a.shape, a.dtype) for a in self.arrays]
        self.scratch = [pltpu.SemaphoreType.DMA((n * N_PEER,)), pltpu.SemaphoreType.DMA((n * N_PEER,)),
                        pltpu.SemaphoreType.DMA((n,))]

    @staticmethod
    def _place():
        x, y, c = lax.axis_index("x"), lax.axis_index("y"), lax.axis_index("c")
        chips = [(1 - x, y), (x, 1 - y), (1 - x, 1 - y)]
        return x, y, c, chips

    @staticmethod
    def _copy(sems, a, k, src, dst, to):
        send_sems, recv_sems, _ = sems
        return pltpu.make_async_remote_copy(
            src_ref=src, dst_ref=dst, send_sem=send_sems.at[a * N_PEER + k], recv_sem=recv_sems.at[a * N_PEER + k],
            device_id=to, device_id_type=MESH)

    def _scatter_peers(self):
        x, y, c, _ = self._place()
        out = []
        for fx, fy, fc in ((0, 0, 1), (1, 0, 0), (0, 1, 0), (1, 1, 0), (1, 0, 1), (0, 1, 1), (1, 1, 1)):
            px, py, pc = x ^ fx, y ^ fy, c ^ fc
            out.append(((px, py, pc), 4 * px + 2 * py + pc))
        return 4 * x + 2 * y + c, out

    def _local(self, ins, outs, sems, a, me):
        src = ins[a].at[me] if self.mode == "scatter" else ins[a]
        return pltpu.make_async_copy(src, outs[a].at[me], sems[2].at[a])

    def start(self, ins, outs, sems):
        if self.mode == "scatter":
            me, peers = self._scatter_peers()
            for a in range(len(ins)):
                self._local(ins, outs, sems, a, me).start()
                for k, (peer, pidx) in enumerate(peers):
                    self._copy(sems, a, k, ins[a].at[pidx], outs[a].at[me], peer).start()
            return
        x, y, c, chips = self._place()
        me = 4 * x + 2 * y + c
        for a in range(len(ins)):
            self._local(ins, outs, sems, a, me).start()
            self._copy(sems, a, 0, ins[a], outs[a].at[me], (x, y, 1 - c)).start()
            for j, (cx, cy) in enumerate(chips):
                self._copy(sems, a, 1 + j, ins[a], outs[a].at[me], (cx, cy, c)).start()

    def finish(self, ins, outs, sems):
        n = len(ins)
        if self.mode == "scatter":
            me, peers = self._scatter_peers()
            for a in range(n):
                for k, (peer, pidx) in enumerate(peers):
                    self._copy(sems, a, k, ins[a].at[pidx], outs[a].at[pidx], peer).wait_recv()
            for a in range(n):
                for k, (peer, pidx) in enumerate(peers):
                    self._copy(sems, a, k, ins[a].at[pidx], outs[a].at[me], peer).wait_send()
                self._local(ins, outs, sems, a, me).wait()
            return
        x, y, c, chips = self._place()
        me = 4 * x + 2 * y + c
        sib = (x, y, 1 - c)
        for a in range(n):
            for j, (cx, cy) in enumerate(chips):
                blk = outs[a].at[4 * cx + 2 * cy + c]
                self._copy(sems, a, 1 + j, ins[a], blk, (cx, cy, c)).wait_recv()
                self._copy(sems, a, 4 + j, blk, blk, sib).start()
        for a in range(n):
            self._copy(sems, a, 0, ins[a], outs[a].at[4 * x + 2 * y + (1 - c)], sib).wait_recv()
            for j, (cx, cy) in enumerate(chips):
                blk = outs[a].at[4 * cx + 2 * cy + (1 - c)]
                self._copy(sems, a, 4 + j, blk, blk, sib).wait_recv()
        for a in range(n):
            for k in range(N_PEER):
                self._copy(sems, a, k, ins[a], outs[a].at[me], sib).wait_send()
            self._local(ins, outs, sems, a, me).wait()


def run_exchange(exch, name):
    n = len(exch.arrays)

    def body(*refs):
        ins, outs, sems = refs[:n], refs[n:2 * n], refs[2 * n:]
        exch.start(ins, outs, sems)
        exch.finish(ins, outs, sems)

    return pl.pallas_call(
        body, name=name, in_specs=[ANY_SPEC] * n, out_specs=[ANY_SPEC] * n, out_shape=exch.out_shape,
        scratch_shapes=exch.scratch,
    )(*exch.arrays)


def _call(body, *, name, grid, in_specs, out_specs, out_shape, scratch_shapes, sem, args, exch=None):
    if exch is None:
        res = pl.pallas_call(body, name=name, grid=grid, in_specs=in_specs, out_specs=out_specs, out_shape=out_shape,
                             scratch_shapes=scratch_shapes, compiler_params=_cparams(sem))(*args)
        return res, None
    single = not isinstance(out_shape, (list, tuple))
    out_specs_l = [out_specs] if single else list(out_specs)
    out_shape_l = [out_shape] if single else list(out_shape)
    n_in, n_out, n_scr, m = len(in_specs), len(out_specs_l), len(scratch_shapes), len(exch.arrays)

    def wrapped(*refs):
        p = 0
        ins = refs[p:p + n_in]; p += n_in
        xin = refs[p:p + m]; p += m
        outs = refs[p:p + n_out]; p += n_out
        xout = refs[p:p + m]; p += m
        scr = refs[p:p + n_scr]; p += n_scr
        sems = refs[p:]
        ids = [pl.program_id(ax) for ax in range(len(grid))]
        first = functools.reduce(jnp.logical_and, [i == 0 for i in ids])
        last = functools.reduce(jnp.logical_and, [i == g - 1 for i, g in zip(ids, grid)])

        @pl.when(first)
        def _():
            exch.start(xin, xout, sems)

        body(*ins, *outs, *scr)

        @pl.when(last)
        def _():
            exch.finish(xin, xout, sems)

    res = pl.pallas_call(
        wrapped, name=name, grid=grid, in_specs=list(in_specs) + [ANY_SPEC] * m,
        out_specs=out_specs_l + [ANY_SPEC] * m, out_shape=out_shape_l + exch.out_shape,
        scratch_shapes=list(scratch_shapes) + exch.scratch, compiler_params=_cparams(sem),
    )(*args, *exch.arrays)
    outs = res[:n_out]
    return (outs[0] if single else outs), res[n_out:]


def _adam_math(w, g, m, v):
    m2 = ADAM_B1 * m + (1.0 - ADAM_B1) * g
    v2 = ADAM_B2 * v + (1.0 - ADAM_B2) * (g * g)
    m_hat = m2 / (1.0 - ADAM_B1 ** ADAM_STEP)
    v_hat = v2 / (1.0 - ADAM_B2 ** ADAM_STEP)
    delta = -ADAM_LR * (m_hat / (jnp.sqrt(v_hat) + ADAM_EPS) + ADAM_WD * w)
    return delta, m2, v2


ADAM_ROW_BLOCKS = 2


def adam_shard(parts0, parts1, w, m, v, name, part_slice=None):
    _, r, c = w.shape
    sub = part_slice
    rb = r // ADAM_ROW_BLOCKS

    def body(p0_ref, p1_ref, w_ref, m_ref, v_ref, g_ref, d_ref, m2_ref, v2_ref):
        def run(p_ref):
            g = p_ref[0].astype(F32)
            for i in range(1, N_DEV):
                g = g + p_ref[i].astype(F32)
            delta, m2, v2 = _adam_math(w_ref[0], g, m_ref[0], v_ref[0])
            g_ref[0] = g
            d_ref[0] = delta
            m2_ref[0] = m2
            v2_ref[0] = v2

        @pl.when(pl.program_id(0) == 0)
        def _():
            run(p0_ref)

        @pl.when(pl.program_id(0) == 1)
        def _():
            run(p1_ref)

    def p_spec(layer):
        row = (lambda l, j: jnp.where(l == 0, j, ADAM_ROW_BLOCKS - 1)) if layer == 0 else (lambda l, j: jnp.where(l == 1, j, 0))
        if sub is None:
            return pl.BlockSpec((N_DEV, rb, c), lambda l, j: (0, row(l, j), 0))
        return pl.BlockSpec((N_DEV, None, rb, c), lambda l, j: (0, sub, row(l, j), 0))

    blk = pl.BlockSpec((1, rb, c), lambda l, j: (l, j, 0))
    return pl.pallas_call(
        body, name=name, grid=(DEPTH, ADAM_ROW_BLOCKS),
        in_specs=[p_spec(0), p_spec(1), blk, blk, blk], out_specs=[blk] * 4,
        out_shape=[SDS(w.shape, F32)] * 4,
        compiler_params=_cparams(("arbitrary", "arbitrary")),
    )(parts0, parts1, w, m, v)


def adam_small(parts, w, m, v, name):
    def body(p_ref, w_ref, m_ref, v_ref, g_ref, d_ref, m2_ref, v2_ref):
        g = p_ref[0]
        for i in range(1, N_DEV):
            g = g + p_ref[i]
        delta, m2, v2 = _adam_math(w_ref[...], g, m_ref[...], v_ref[...])
        g_ref[...] = g
        d_ref[...] = delta
        m2_ref[...] = m2
        v2_ref[...] = v2

    return pl.pallas_call(
        body, name=name, out_shape=[SDS(w.shape, F32)] * 4, compiler_params=_cparams(),
    )(parts, w, m, v)


def _rot_cols(w):
    w4 = w.reshape(w.shape[0], 4, 2, 32)
    return jnp.stack([-w4[:, :, 1], w4[:, :, 0]], axis=2).reshape(w.shape[0], ATT_W)


def _rot_cols_t(dw_rot):
    d4 = dw_rot.reshape(dw_rot.shape[0], 4, 2, 32)
    return jnp.stack([d4[:, :, 1], -d4[:, :, 0]], axis=2).reshape(dw_rot.shape[0], ATT_W)


def build_wext(w_in):
    aq, ak, av, pu = w_in[:, 0:256], w_in[:, 256:512], w_in[:, 512:768], w_in[:, 768:1024]
    dqkvz = w_in[:, 1024:3072]
    gates = jnp.repeat(w_in[:, 3072:3080], DN_E, axis=1)
    return jnp.concatenate([aq, ak, av, _rot_cols(aq), _rot_cols(ak), dqkvz, gates, pu], axis=1)


def fold_dwext(d):
    b = EXT_ATT
    aq = d[:, 0:256] + _rot_cols_t(d[:, 768:1024])
    ak = d[:, 256:512] + _rot_cols_t(d[:, 1024:1280])
    av = d[:, 512:768]
    dqkvz = d[:, b:b + 2048]
    gates = d[:, b + R_BB:b + R_BB + 1024].reshape(d.shape[0], 8, DN_E).sum(axis=-1)
    pu = d[:, b + R_PU:b + R_PU + 256]
    return jnp.concatenate([aq, ak, av, pu, dqkvz, gates], axis=1)


def _block_diag(pw):
    z = jnp.zeros((4, 64, 4, 64), pw.dtype)
    for g in range(4):
        z = z.at[g, :, g, :].set(pw[g])
    return z.reshape(POOL_W, POOL_W)


def _diag_blocks(m):
    m4 = m.reshape(4, 64, 4, 64)
    return jnp.stack([m4[g, :, g, :] for g in range(4)], axis=0)


def _lanes(v, reps):
    return jnp.repeat(v, reps)[None, :]


def layer_fwd(p, xa, cos, sin, l, host=None):
    host = host or {}

    def carried(key):
        return host[key][0] if key in host else None

    def done(key, xo):
        if key in host:
            host[key][1](xo)

    xb, xo = ffn_fwd(xa, p["n1"], p["f1gu"], p["f1d"], f"ffn1_fwd_{l}", carried("ffn1"))
    done("ffn1", xo)
    att, rest = mix_in_fwd(xb, p["nm"], p["wext"], cos, sin, f"mix_in_fwd_{l}")
    att_c = [to_classes(att, d) for d in DILATIONS]
    ol_c = [att_fwd(a, SEQ // d // ATT_BLK, f"att_fwd_{l}_{d}") for a, d in zip(att_c, DILATIONS)]
    ols = [from_classes(o, d) for o, d in zip(ol_c, DILATIONS)]
    qkv = dn_prep_fwd(rest, p["conv"], f"dn_prep_fwd_{l}")
    dn = dn_intra_fwd(qkv, rest, p["alog"], p["dtb"], f"dn_intra_fwd_{l}")
    odn, states = dn_inter_fwd(*dn, f"dn_inter_fwd_{l}")
    xc = mix_out_fwd(xb, ols[0], ols[1], ols[2], rest, odn, p["wbd"], p["scale"], p["onorm"], p["wout"], f"mix_out_fwd_{l}")
    xd, xo = ffn_fwd(xc, p["n2"], p["f2gu"], p["f2d"], f"ffn2_fwd_{l}", carried("ffn2"))
    done("ffn2", xo)
    return xd, dict(xa=xa, xb=xb, xc=xc, att_c=att_c, ol_c=ol_c, rest=rest, ols=ols, qkv=qkv, dn=dn, odn=odn, states=states)


def layer_bwd(p, s, dx, cos, sin, headsum, l, scatter=False, carry=None):
    (dx, d_f2gu, d_f2d, d_n2), carried = ffn_bwd(s["xc"], dx, p["n2"], p["f2gu"], p["f2d"], f"ffn2_bwd_{l}", carry)
    (d_wout, dol1, dol4, dol16, dpooled, dodn, dz, dscale, donorm, dwbd) = mix_out_bwd(
        dx, s["ols"][0], s["ols"][1], s["ols"][2], s["rest"], s["odn"], p["wbd"], p["scale"], p["onorm"], p["wout"],
        headsum, f"mix_out_bwd_{l}")
    dpu = pool_bwd(dpooled, f"pool_bwd_{l}")
    f2 = list(ffn_grads_to_shards(d_f2gu, d_f2d))
    d_dn = dn_inter_bwd(*s["dn"], s["states"], dodn, f"dn_inter_bwd_{l}")
    dqkv, dbb, dab, dalog, ddtb = dn_intra_bwd(s["qkv"], s["rest"], p["alog"], p["dtb"], *d_dn, f"dn_intra_bwd_{l}")
    d_dqkv, dconv = dn_prep_bwd(s["rest"], p["conv"], dqkv, f"dn_prep_bwd_{l}")
    datts = []
    for d, a_c, o_c, dol in zip(DILATIONS, s["att_c"], s["ol_c"], (dol1, dol4, dol16)):
        da = att_bwd(a_c, o_c, to_classes(dol, d), SEQ // d // ATT_BLK, f"att_bwd_{l}_{d}")
        datts.append(from_classes(da, d))
    dproj = assemble_dproj(datts, cos, sin, d_dqkv, dz, dbb, dab, dpu, f"assemble_dproj_{l}")
    dx, d_wext, d_nm = linear_bwd(s["xb"], dx, p["nm"], dproj, p["wext"], f"mix_in_bwd_{l}")
    d_win = fold_dwext(d_wext).reshape(D_MODEL, N_DEV, IN_BLK).transpose(1, 0, 2).astype(BF16)
    io = [d_win, d_wout.reshape(N_DEV, D_MODEL // N_DEV, D_MODEL).astype(BF16)]
    (dx, d_f1gu, d_f1d, d_n1), xo = ffn_bwd(s["xa"], dx, p["n1"], p["f1gu"], p["f1d"], f"ffn1_bwd_{l}",
                                           Exchange(f2 + io, "scatter") if scatter else None)
    if scatter:
        f2, io = list(xo[:2]), list(xo[2:])
    big = dict(f1=list(ffn_grads_to_shards(d_f1gu, d_f1d)), f2=f2, io=io)
    small = dict(ffn1_norm=d_n1[0], mix_norm=d_nm[0], ffn2_norm=d_n2[0], pool_w=_diag_blocks(dwbd),
                 pool_scale=dscale[0], dn_a_log=dalog.reshape(DN_H, DN_E).sum(-1),
                 dn_dt_bias=ddtb.reshape(DN_H, DN_E).sum(-1),
                 dn_out_norm=donorm.reshape(DN_H, DN_E).sum(0), dn_conv_w=dconv)
    return dx, big, small, carried


def small_operands(l, pool_w, pool_scale, dn_out_norm, dn_a_log, dn_dt_bias, ffn1_norm, mix_norm, ffn2_norm):
    return dict(
        wbd=_block_diag(pool_w[l]).astype(BF16),
        scale=pool_scale[l][None, :],
        onorm=jnp.tile(dn_out_norm[l], DN_H)[None, :],
        alog=_lanes(dn_a_log[l], DN_E),
        dtb=_lanes(dn_dt_bias[l], DN_E),
        n1=ffn1_norm[l][None, :], nm=mix_norm[l][None, :], n2=ffn2_norm[l][None, :])


def set_mixer_weights(p, win_g, wout_g, conv_g):
    p["wext"] = build_wext(win_g.transpose(1, 0, 2).reshape(D_MODEL, IN_W))
    p["wout"] = wout_g.reshape(D_MODEL, D_MODEL)
    p["conv"] = conv_g.transpose(1, 0, 2).reshape(DN_CONV, 3 * DN_W)


def rope_tables(pos):
    inv_freq = 10000.0 ** (-jnp.arange(0, ATT_E, 2, dtype=F32) / ATT_E)
    ang = pos.astype(F32)[:, None] * inv_freq
    return jnp.tile(jnp.cos(ang), (1, 8)), jnp.tile(jnp.sin(ang), (1, 8))


def head_sum_matrix():
    return jnp.kron(jnp.eye(4, dtype=F32), jnp.ones((ATT_E, ATT_E), F32))


SMALL_NAMES = ("ffn1_norm", "mix_norm", "ffn2_norm", "pool_w", "pool_scale", "dn_a_log", "dn_dt_bias",
               "dn_out_norm", "final_norm", "dn_conv_w")


def _pack(parts):
    flat = jnp.concatenate([p.reshape(-1) for p in parts])
    n = flat.shape[0]
    rows = -(-n // 1024) * 8
    return jnp.pad(flat, (0, rows * 128 - n)).reshape(rows, 128)


def _unpack(packed, shapes):
    flat = packed.reshape(-1)
    out, off = [], 0
    for s in shapes:
        n = math.prod(s)
        out.append(flat[off:off + n].reshape(s))
        off += n
    return out


def kernel(x, positions, ffn1_norm, ffn1_w_gate, ffn1_w_up, ffn1_w_down, mix_norm, w_in, pool_w, pool_scale, dn_conv_w, dn_a_log, dn_dt_bias, dn_out_norm, w_out, ffn2_norm, ffn2_w_gate, ffn2_w_up, ffn2_w_down, final_norm, loss_target, m_ffn1_norm, m_ffn1_w_gate, m_ffn1_w_up, m_ffn1_w_down, m_mix_norm, m_w_in, m_pool_w, m_pool_scale, m_dn_conv_w, m_dn_a_log, m_dn_dt_bias, m_dn_out_norm, m_w_out, m_ffn2_norm, m_ffn2_w_gate, m_ffn2_w_up, m_ffn2_w_down, m_final_norm, v_ffn1_norm, v_ffn1_w_gate, v_ffn1_w_up, v_ffn1_w_down, v_mix_norm, v_w_in, v_pool_w, v_pool_scale, v_dn_conv_w, v_dn_a_log, v_dn_dt_bias, v_dn_out_norm, v_w_out, v_ffn2_norm, v_ffn2_w_gate, v_ffn2_w_up, v_ffn2_w_down, v_final_norm):
    me = 4 * lax.axis_index("x") + 2 * lax.axis_index("y") + lax.axis_index("c")
    x0 = x[0]
    target = loss_target[0]

    cos, sin = rope_tables(positions[0])
    headsum = head_sum_matrix()

    layers = [small_operands(l, pool_w, pool_scale, dn_out_norm, dn_a_log, dn_dt_bias, ffn1_norm, mix_norm, ffn2_norm)
              for l in range(DEPTH)]

    def ffn_shards(gate, up, down, l):
        return [jnp.stack([gate[l], up[l]]).astype(BF16), down[l].astype(BF16)]

    def gather_ffn1(l):
        def on_done(xo):
            layers[l]["f1gu"], layers[l]["f1d"] = ffn_weights_from_shards(*xo)
        return Exchange(ffn_shards(ffn1_w_gate, ffn1_w_up, ffn1_w_down, l), "gather"), on_done

    def gather_rest(l):
        def on_done(xo):
            set_mixer_weights(layers[l], *xo[:3])
            layers[l]["f2gu"], layers[l]["f2d"] = ffn_weights_from_shards(*xo[3:])
        return Exchange([w_in[l].astype(BF16), w_out[l].astype(BF16), dn_conv_w[l]]
                        + ffn_shards(ffn2_w_gate, ffn2_w_up, ffn2_w_down, l), "gather"), on_done

    first, on_first = gather_ffn1(0)
    on_first(run_exchange(first, "gather_ffn1_0"))
    saved = []
    xa = x0
    for l in range(DEPTH):
        host = {"ffn1": gather_rest(l)}
        if l + 1 < DEPTH:
            host["ffn2"] = gather_ffn1(l + 1)
        xa, s = layer_fwd(layers[l], xa, cos, sin, l, host)
        saved.append(s)

    loss_row, dx, d_final = loss_head(xa, final_norm[None, :], target, "loss_head")
    loss = lax.psum(loss_row[0, 0], ("x", "y", "c"))

    small = {}
    big_parts = [None] * DEPTH
    carry = None
    for l in reversed(range(DEPTH)):
        dx, big, small[l], carried = layer_bwd(layers[l], saved[l], dx, cos, sin, headsum, l, True, carry)
        if carried is not None:
            big_parts[l + 1]["f1"] = list(carried)
        big_parts[l] = big
        carry = Exchange(big["f1"], "scatter")
    big_parts[0]["f1"] = list(run_exchange(carry, "scatter_ffn1_0"))
    grad_x = dx[None]

    small_shapes = {"ffn1_norm": (DEPTH, D_MODEL), "mix_norm": (DEPTH, D_MODEL), "ffn2_norm": (DEPTH, D_MODEL),
                    "pool_w": (DEPTH, 4, 64, 64), "pool_scale": (DEPTH, POOL_W), "dn_a_log": (DEPTH, DN_H),
                    "dn_dt_bias": (DEPTH, DN_H), "dn_out_norm": (DEPTH, DN_E), "final_norm": (D_MODEL,),
                    "dn_conv_w": (DEPTH, DN_CONV, 3 * DN_W)}
    g_small = {n: (d_final[0] if n == "final_norm" else jnp.stack([small[l][n] for l in range(DEPTH)]))
               for n in SMALL_NAMES}
    (small_parts,) = run_exchange(Exchange([_pack([g_small[n] for n in SMALL_NAMES])], "gather"), "gather_small_grads")

    def conv_full(a):
        return lax.dynamic_update_slice(jnp.zeros((DEPTH, DN_CONV, 3 * DN_W), F32), a, (0, 0, me * (3 * DN_W // N_DEV)))

    given = dict(ffn1_norm=(ffn1_norm, m_ffn1_norm, v_ffn1_norm), mix_norm=(mix_norm, m_mix_norm, v_mix_norm),
                 ffn2_norm=(ffn2_norm, m_ffn2_norm, v_ffn2_norm), pool_w=(pool_w, m_pool_w, v_pool_w),
                 pool_scale=(pool_scale, m_pool_scale, v_pool_scale), dn_a_log=(dn_a_log, m_dn_a_log, v_dn_a_log),
                 dn_dt_bias=(dn_dt_bias, m_dn_dt_bias, v_dn_dt_bias),
                 dn_out_norm=(dn_out_norm, m_dn_out_norm, v_dn_out_norm),
                 final_norm=(final_norm, m_final_norm, v_final_norm),
                 dn_conv_w=(conv_full(dn_conv_w), conv_full(m_dn_conv_w), conv_full(v_dn_conv_w)))
    packed_wmv = [_pack([given[n][k] for n in SMALL_NAMES]) for k in range(3)]
    small_out = adam_small(small_parts, *packed_wmv, "adam_small")
    shapes = [small_shapes[n] for n in SMALL_NAMES]
    small_res = {n: [] for n in SMALL_NAMES}
    for arr in small_out:
        for n, v_ in zip(SMALL_NAMES, _unpack(arr, shapes)):
            if n == "dn_conv_w":
                v_ = lax.dynamic_slice(v_, (0, 0, me * (3 * DN_W // N_DEV)), (DEPTH, DN_CONV, 3 * DN_W // N_DEV))
            small_res[n].append(v_)

    def parts_of(group, idx):
        return [big_parts[l][group][idx] for l in range(DEPTH)]

    big_res = dict(
        ffn1_w_gate=adam_shard(*parts_of("f1", 0), ffn1_w_gate, m_ffn1_w_gate, v_ffn1_w_gate, "adam_ffn1_gate", 0),
        ffn1_w_up=adam_shard(*parts_of("f1", 0), ffn1_w_up, m_ffn1_w_up, v_ffn1_w_up, "adam_ffn1_up", 1),
        ffn1_w_down=adam_shard(*parts_of("f1", 1), ffn1_w_down, m_ffn1_w_down, v_ffn1_w_down, "adam_ffn1_down"),
        ffn2_w_gate=adam_shard(*parts_of("f2", 0), ffn2_w_gate, m_ffn2_w_gate, v_ffn2_w_gate, "adam_ffn2_gate", 0),
        ffn2_w_up=adam_shard(*parts_of("f2", 0), ffn2_w_up, m_ffn2_w_up, v_ffn2_w_up, "adam_ffn2_up", 1),
        ffn2_w_down=adam_shard(*parts_of("f2", 1), ffn2_w_down, m_ffn2_w_down, v_ffn2_w_down, "adam_ffn2_down"),
        w_in=adam_shard(*parts_of("io", 0), w_in, m_w_in, v_w_in, "adam_w_in"),
        w_out=adam_shard(*parts_of("io", 1), w_out, m_w_out, v_w_out, "adam_w_out"),
    )

    order = ("ffn1_norm", "ffn1_w_gate", "ffn1_w_up", "ffn1_w_down", "mix_norm", "w_in", "pool_w", "pool_scale",
             "dn_conv_w", "dn_a_log", "dn_dt_bias", "dn_out_norm", "w_out", "ffn2_norm", "ffn2_w_gate", "ffn2_w_up",
             "ffn2_w_down", "final_norm")
    res = {**small_res, **big_res}
    outs = [loss, grad_x]
    for k in range(4):
        outs.extend(res[n][k] for n in order)
    return tuple(outs)
```

```python
import functools
import math

import jax
import jax.numpy as jnp
from jax import lax
from jax.experimental import pallas as pl
from jax.experimental.pallas import tpu as pltpu

F32 = jnp.float32
BF16 = jnp.bfloat16
HI = lax.Precision.HIGHEST
INV_PREC = lax.Precision.HIGH
SDS = jax.ShapeDtypeStruct

N_DEV = 8
SEQ = 4096
D_MODEL = 1024
DEPTH = 2
D_FF = 2816
FF_BLK = D_FF // N_DEV
ATT_W = 256
ATT_E = 64
ATT_BLK = 128
DILATIONS = (1, 4, 16)
POOL_W = 256
POOL_HALO = 16
DN_W = 512
DN_H = 4
DN_E = 128
DN_C = 64
N_CHUNK = SEQ // DN_C
IN_W = 3080
IN_BLK = IN_W // N_DEV
EPS = 1e-6
EXT_ATT = 1280
EXT_REST = 3328
EXT_W = EXT_ATT + EXT_REST
R_DQKV, R_DZ, R_BB, R_AB, R_PU = 0, 1536, 2048, 2560, 3072

ADAM_LR, ADAM_B1, ADAM_B2, ADAM_EPS, ADAM_WD, ADAM_STEP = 0.001, 0.9, 0.999, 1e-08, 0.01, 10

VMEM_LIMIT = 60 * 1024 * 1024
MESH = pl.DeviceIdType.MESH


def _cparams(sem=None):
    kw = dict(vmem_limit_bytes=VMEM_LIMIT)
    if sem is not None:
        kw["dimension_semantics"] = sem
    return pltpu.CompilerParams(**kw)


def _dot(a, b, prec=None):
    return jnp.dot(a, b, preferred_element_type=F32, precision=prec)


def _dot_nt(a, b, prec=None):
    return lax.dot_general(a, b, (((1,), (1,)), ((), ())), preferred_element_type=F32, precision=prec)


def _dot_tn(a, b, prec=None):
    return lax.dot_general(a, b, (((0,), (0,)), ((), ())), preferred_element_type=F32, precision=prec)


def _sigmoid(x):
    return jax.nn.sigmoid(x)


def _rms_stats(x):
    r = lax.rsqrt(jnp.mean(x * x, axis=-1, keepdims=True) + EPS)
    return x * r, r


def _rms_bwd(xh, r, w, dh):
    dxh = dh * w
    dx = r * (dxh - xh * jnp.mean(dxh * xh, axis=-1, keepdims=True))
    return dx, jnp.sum(dh * xh, axis=0, keepdims=True)


FFN_T_FWD = 1024
FFN_T_BWD = 512
FF_TILE = 256
N_FF_TILE = D_FF // FF_TILE


def ffn_weights_from_shards(wgu_g, wd_g):
    return wgu_g.transpose(1, 2, 0, 3).reshape(2, D_MODEL, D_FF), wd_g.reshape(D_FF, D_MODEL)


def ffn_grads_to_shards(dwgu, dwd):
    return dwgu.reshape(2, D_MODEL, N_DEV, FF_BLK).transpose(2, 0, 1, 3), dwd.reshape(N_DEV, FF_BLK, D_MODEL)


def ffn_fwd(x, nw, wgu, wd, name, exch=None):
    t = FFN_T_FWD

    def body(x_ref, nw_ref, wgu_ref, wd_ref, o_ref, h_scr, acc_scr):
        k = pl.program_id(1)

        @pl.when(k == 0)
        def _():
            xh, _r = _rms_stats(x_ref[...])
            h_scr[...] = (xh * nw_ref[...]).astype(BF16)
            acc_scr[...] = jnp.zeros_like(acc_scr)

        h = h_scr[...]
        hg = _dot(h, wgu_ref[0])
        hu = _dot(h, wgu_ref[1])
        a = (hg * _sigmoid(hg) * hu).astype(BF16)
        acc_scr[...] += _dot(a, wd_ref[...])

        @pl.when(k == N_FF_TILE - 1)
        def _():
            o_ref[...] = x_ref[...] + 0.5 * acc_scr[...]

    return _call(
        body, name=name, grid=(SEQ // t, N_FF_TILE),
        in_specs=[pl.BlockSpec((t, D_MODEL), lambda i, k: (i, 0)),
                  pl.BlockSpec((1, D_MODEL), lambda i, k: (0, 0)),
                  pl.BlockSpec((2, D_MODEL, FF_TILE), lambda i, k: (0, 0, k)),
                  pl.BlockSpec((FF_TILE, D_MODEL), lambda i, k: (k, 0))],
        out_specs=pl.BlockSpec((t, D_MODEL), lambda i, k: (i, 0)),
        out_shape=SDS((SEQ, D_MODEL), F32),
        scratch_shapes=[pltpu.VMEM((t, D_MODEL), BF16), pltpu.VMEM((t, D_MODEL), F32)],
        sem=("arbitrary", "arbitrary"), args=(x, nw, wgu, wd), exch=exch)


def ffn_bwd(x, dxo, nw, wgu, wd, name, exch=None):
    t = FFN_T_BWD
    nt = SEQ // t

    def body(x_ref, dxo_ref, nw_ref, wgu_ref, wd_ref, dx_ref, dwgu_ref, dwd_ref, dnw_ref,
             dh_scr, ag_scr, au_scr, ad_scr, h_scr):
        k = pl.program_id(0)
        i = pl.program_id(1)
        rows = pl.ds(pl.multiple_of(i * t, t), t)
        nw_v = nw_ref[...]

        @pl.when(k == 0)
        def _():
            xh0, _r0 = _rms_stats(x_ref[...])
            h_scr[rows, :] = (xh0 * nw_v).astype(BF16)

        h = h_scr[rows, :]
        dy = (0.5 * dxo_ref[...]).astype(BF16)
        wg = wgu_ref[0]
        wu = wgu_ref[1]
        hg = _dot(h, wg)
        hu = _dot(h, wu)
        sg = _sigmoid(hg)
        sil = hg * sg
        a = (sil * hu).astype(BF16)
        da = _dot_nt(dy, wd_ref[...])
        dhu = (da * sil).astype(BF16)
        dhg = (da * hu * (sg * (1.0 + hg * (1.0 - sg)))).astype(BF16)
        p_d = _dot_tn(a, dy)
        p_g = _dot_tn(h, dhg)
        p_u = _dot_tn(h, dhu)
        dh = _dot_nt(dhg, wg) + _dot_nt(dhu, wu)

        @pl.when(i == 0)
        def _():
            ad_scr[...] = p_d
            ag_scr[...] = p_g
            au_scr[...] = p_u

        @pl.when(i > 0)
        def _():
            ad_scr[...] += p_d
            ag_scr[...] += p_g
            au_scr[...] += p_u

        @pl.when(i == nt - 1)
        def _():
            dwd_ref[...] = ad_scr[...].astype(BF16)
            dwgu_ref[0] = ag_scr[...].astype(BF16)
            dwgu_ref[1] = au_scr[...].astype(BF16)

        @pl.when(k == 0)
        def _():
            dh_scr[rows, :] = dh

        @pl.when(k > 0)
        def _():
            dh_scr[rows, :] += dh

        @pl.when(jnp.logical_and(k == 0, i == 0))
        def _():
            dnw_ref[...] = jnp.zeros_like(dnw_ref)

        @pl.when(k == N_FF_TILE - 1)
        def _():
            xh, r = _rms_stats(x_ref[...])
            dx, dw = _rms_bwd(xh, r, nw_v, dh_scr[rows, :])
            dx_ref[...] = dxo_ref[...] + dx
            dnw_ref[...] += dw

    last = N_FF_TILE - 1
    return _call(
        body, name=name, grid=(N_FF_TILE, nt),
        in_specs=[pl.BlockSpec((t, D_MODEL), lambda k, i: (i, 0)),
                  pl.BlockSpec((t, D_MODEL), lambda k, i: (i, 0)),
                  pl.BlockSpec((1, D_MODEL), lambda k, i: (0, 0)),
                  pl.BlockSpec((2, D_MODEL, FF_TILE), lambda k, i: (0, 0, k)),
                  pl.BlockSpec((FF_TILE, D_MODEL), lambda k, i: (k, 0))],
        out_specs=[pl.BlockSpec((t, D_MODEL), lambda k, i: (jnp.where(k == last, i, 0), 0)),
                   pl.BlockSpec((2, D_MODEL, FF_TILE), lambda k, i: (0, 0, k)),
                   pl.BlockSpec((FF_TILE, D_MODEL), lambda k, i: (k, 0)),
                   pl.BlockSpec((1, D_MODEL), lambda k, i: (0, 0))],
        out_shape=[SDS((SEQ, D_MODEL), F32), SDS((2, D_MODEL, D_FF), BF16),
                   SDS((D_FF, D_MODEL), BF16), SDS((1, D_MODEL), F32)],
        scratch_shapes=[pltpu.VMEM((SEQ, D_MODEL), F32), pltpu.VMEM((D_MODEL, FF_TILE), F32),
                        pltpu.VMEM((D_MODEL, FF_TILE), F32), pltpu.VMEM((FF_TILE, D_MODEL), F32),
                        pltpu.VMEM((SEQ, D_MODEL), BF16)],
        sem=("arbitrary", "arbitrary"), args=(x, dxo, nw, wgu, wd), exch=exch)


def loss_head(x, fw, target, name):
    t = 512

    def body(x_ref, fw_ref, tg_ref, loss_ref, dx_ref, dfw_ref):
        i = pl.program_id(0)
        xh, r = _rms_stats(x_ref[...])
        w = fw_ref[...]
        err = xh * w - tg_ref[...]
        part = 0.5 * jnp.sum(jnp.sum(err * err, axis=-1, keepdims=True), axis=0, keepdims=True) / D_MODEL
        dx, dw = _rms_bwd(xh, r, w, err * (1.0 / D_MODEL))
        dx_ref[...] = dx

        @pl.when(i == 0)
        def _():
            loss_ref[...] = jnp.zeros_like(loss_ref)
            dfw_ref[...] = jnp.zeros_like(dfw_ref)

        loss_ref[...] += jnp.broadcast_to(part, loss_ref.shape)
        dfw_ref[...] += dw

    return pl.pallas_call(
        body, name=name, grid=(SEQ // t,),
        in_specs=[pl.BlockSpec((t, D_MODEL), lambda i: (i, 0)),
                  pl.BlockSpec((1, D_MODEL), lambda i: (0, 0)),
                  pl.BlockSpec((t, D_MODEL), lambda i: (i, 0))],
        out_specs=[pl.BlockSpec((1, 128), lambda i: (0, 0)),
                   pl.BlockSpec((t, D_MODEL), lambda i: (i, 0)),
                   pl.BlockSpec((1, D_MODEL), lambda i: (0, 0))],
        out_shape=[SDS((1, 128), F32), SDS((SEQ, D_MODEL), F32), SDS((1, D_MODEL), F32)],
        compiler_params=_cparams(("arbitrary",)),
    )(x, fw, target)


MIX_T = 256


def _slabs_load(ref, first, n):
    return jnp.concatenate([ref[first + j] for j in range(n)], axis=1)


def _slabs_store(ref, first, val):
    for j in range(val.shape[1] // 128):
        ref[first + j] = val[:, 128 * j:128 * j + 128]


def _slab_spec(k, t):
    return pl.BlockSpec((k, t, 128), lambda i: (0, i, 0))


def mix_in_fwd(x, nw, wext, cos, sin, name):
    t = MIX_T

    def body(x_ref, nw_ref, w_ref, cos_ref, sin_ref, att_ref, rest_ref):
        xh, _r = _rms_stats(x_ref[...])
        h = (xh * nw_ref[...]).astype(BF16)
        pa = _dot(h, w_ref[:, 0:EXT_ATT])
        c = cos_ref[...]
        s = sin_ref[...]
        _slabs_store(att_ref, 0, pa[:, 0:256] * c + pa[:, 768:1024] * s)
        _slabs_store(att_ref, 2, pa[:, 256:512] * c + pa[:, 1024:1280] * s)
        _slabs_store(att_ref, 4, pa[:, 512:768])
        for j in range(EXT_REST // 256):
            rest_ref[:, 256 * j:256 * j + 256] = _dot(h, w_ref[:, EXT_ATT + 256 * j:EXT_ATT + 256 * j + 256])

    return pl.pallas_call(
        body, name=name, grid=(SEQ // t,),
        in_specs=[pl.BlockSpec((t, D_MODEL), lambda i: (i, 0)),
                  pl.BlockSpec((1, D_MODEL), lambda i: (0, 0)),
                  pl.BlockSpec((D_MODEL, EXT_W), lambda i: (0, 0)),
                  pl.BlockSpec((t, ATT_W), lambda i: (i, 0)),
                  pl.BlockSpec((t, ATT_W), lambda i: (i, 0))],
        out_specs=[_slab_spec(6, t),
                   pl.BlockSpec((t, EXT_REST), lambda i: (i, 0))],
        out_shape=[SDS((6, SEQ, 128), F32), SDS((SEQ, EXT_REST), F32)],
        compiler_params=_cparams(("arbitrary",)),
    )(x, nw, wext, cos, sin)


def assemble_dproj(datts, cos, sin, d_dqkv, dz, dbb, dab, dpu, name):
    t = 512

    def body(d1_ref, d4_ref, d16_ref, cos_ref, sin_ref, dqkv_ref, dz_ref, dbb_ref, dab_ref, dpu_ref, o_ref):
        da6 = d1_ref[...] + d4_ref[...] + d16_ref[...]
        da = jnp.concatenate([da6[j] for j in range(6)], axis=1)
        c = cos_ref[...]
        s = sin_ref[...]
        dq = da[:, 0:256]
        dk = da[:, 256:512]
        o_ref[:, 0:256] = (dq * c).astype(BF16)
        o_ref[:, 256:512] = (dk * c).astype(BF16)
        o_ref[:, 512:768] = da[:, 512:768].astype(BF16)
        o_ref[:, 768:1024] = (dq * s).astype(BF16)
        o_ref[:, 1024:1280] = (dk * s).astype(BF16)
        b = EXT_ATT
        o_ref[:, b + R_DQKV:b + R_DQKV + 1536] = dqkv_ref[...].astype(BF16)
        o_ref[:, b + R_DZ:b + R_DZ + 512] = dz_ref[...].astype(BF16)
        o_ref[:, b + R_BB:b + R_BB + 512] = dbb_ref[...].astype(BF16)
        o_ref[:, b + R_AB:b + R_AB + 512] = dab_ref[...].astype(BF16)
        o_ref[:, b + R_PU:b + R_PU + 256] = dpu_ref[...].astype(BF16)

    row = lambda w: pl.BlockSpec((t, w), lambda i: (i, 0))
    return pl.pallas_call(
        body, name=name, grid=(SEQ // t,),
        in_specs=[_slab_spec(6, t), _slab_spec(6, t), _slab_spec(6, t),
                  row(256), row(256), row(1536), row(512), row(512), row(512), row(256)],
        out_specs=row(EXT_W),
        out_shape=SDS((SEQ, EXT_W), BF16),
        compiler_params=_cparams(("arbitrary",)),
    )(*datts, cos, sin, d_dqkv, dz, dbb, dab, dpu)


def linear_bwd(x, dxo, nw, dy, w, name):
    t = 512
    nb = 768
    n = w.shape[1]
    nt = SEQ // t
    nn = n // nb

    def body(x_ref, dxo_ref, nw_ref, dy_ref, w_ref, dx_ref, dw_ref, dnw_ref, dh_scr, h_scr):
        k = pl.program_id(0)
        i = pl.program_id(1)
        rows = pl.ds(pl.multiple_of(i * t, t), t)
        nw_v = nw_ref[...]

        @pl.when(k == 0)
        def _():
            xh0, _r0 = _rms_stats(x_ref[...])
            h_scr[rows, :] = (xh0 * nw_v).astype(BF16)

        h = h_scr[rows, :]
        dyv = dy_ref[...]
        p_w = _dot_tn(h, dyv)
        dh = _dot_nt(dyv, w_ref[...])

        @pl.when(i == 0)
        def _():
            dw_ref[...] = p_w

        @pl.when(i > 0)
        def _():
            dw_ref[...] += p_w

        @pl.when(k == 0)
        def _():
            dh_scr[rows, :] = dh

        @pl.when(k > 0)
        def _():
            dh_scr[rows, :] += dh

        @pl.when(jnp.logical_and(k == 0, i == 0))
        def _():
            dnw_ref[...] = jnp.zeros_like(dnw_ref)

        @pl.when(k == nn - 1)
        def _():
            xh, r = _rms_stats(x_ref[...])
            dx, dw = _rms_bwd(xh, r, nw_v, dh_scr[rows, :])
            dx_ref[...] = dxo_ref[...] + dx
            dnw_ref[...] += dw

    last = nn - 1
    return pl.pallas_call(
        body, name=name, grid=(nn, nt),
        in_specs=[pl.BlockSpec((t, D_MODEL), lambda k, i: (i, 0)),
                  pl.BlockSpec((t, D_MODEL), lambda k, i: (i, 0)),
                  pl.BlockSpec((1, D_MODEL), lambda k, i: (0, 0)),
                  pl.BlockSpec((t, nb), lambda k, i: (i, k)),
                  pl.BlockSpec((D_MODEL, nb), lambda k, i: (0, k))],
        out_specs=[pl.BlockSpec((t, D_MODEL), lambda k, i: (jnp.where(k == last, i, 0), 0)),
                   pl.BlockSpec((D_MODEL, nb), lambda k, i: (0, k)),
                   pl.BlockSpec((1, D_MODEL), lambda k, i: (0, 0))],
        out_shape=[SDS((SEQ, D_MODEL), F32), SDS((D_MODEL, n), F32), SDS((1, D_MODEL), F32)],
        scratch_shapes=[pltpu.VMEM((SEQ, D_MODEL), F32), pltpu.VMEM((SEQ, D_MODEL), BF16)],
        compiler_params=_cparams(("arbitrary", "arbitrary")),
    )(x, dxo, nw, dy, w)


def _att_masks():
    qi = lax.broadcasted_iota(jnp.int32, (ATT_BLK, ATT_BLK), 0)
    ki = lax.broadcasted_iota(jnp.int32, (ATT_BLK, ATT_BLK), 1)
    return ki <= qi, ki >= qi


NEG = -1e30


def _att_heads(ref, base):
    return jnp.stack([ref[:, base + ATT_E * hd:base + ATT_E * hd + ATT_E] for hd in range(4)], axis=0)


def att_fwd(att, blocks_per_class, name):
    nblk = SEQ // ATT_BLK

    def body(cur_ref, prev_ref, o_ref):
        i = pl.program_id(0)
        has_prev = (i % blocks_per_class) != 0
        m_d, m_p = _att_masks()
        m_p = jnp.logical_and(m_p, has_prev)
        q = _att_heads(cur_ref, 0).astype(BF16)
        kc = _att_heads(cur_ref, 256).astype(BF16)
        vc = _att_heads(cur_ref, 512).astype(BF16)
        kp = _att_heads(prev_ref, 256).astype(BF16)
        vp = _att_heads(prev_ref, 512).astype(BF16)
        sd = jnp.where(m_d, _bdot(q, kc, 2, 2) * 0.125, NEG)
        sp = jnp.where(m_p, _bdot(q, kp, 2, 2) * 0.125, NEG)
        m = jnp.maximum(jnp.max(sd, axis=-1, keepdims=True), jnp.max(sp, axis=-1, keepdims=True))
        pd = jnp.exp(sd - m)
        pp = jnp.exp(sp - m)
        den = jnp.sum(pd, axis=-1, keepdims=True) + jnp.sum(pp, axis=-1, keepdims=True)
        inv = 1.0 / den
        o = _bdot((pd * inv).astype(BF16), vc, 2, 1) + _bdot((pp * inv).astype(BF16), vp, 2, 1)
        lse = m + jnp.log(den)
        for hd in range(4):
            o_ref[:, ATT_E * hd:ATT_E * hd + ATT_E] = o[hd]
            o_ref[:, 256 + ATT_E * hd:256 + ATT_E * hd + ATT_E] = jnp.broadcast_to(lse[hd], (ATT_BLK, ATT_E))

    return pl.pallas_call(
        body, name=name, grid=(nblk,),
        in_specs=[pl.BlockSpec((ATT_BLK, 768), lambda i: (i, 0)),
                  pl.BlockSpec((ATT_BLK, 768), lambda i: (jnp.maximum(i - 1, 0), 0))],
        out_specs=pl.BlockSpec((ATT_BLK, 512), lambda i: (i, 0)),
        out_shape=SDS((SEQ, 512), F32),
        compiler_params=_cparams(("arbitrary",)),
    )(att, att)


def att_bwd(att, ol, dol, blocks_per_class, name):
    nblk = SEQ // ATT_BLK

    def body(prev_ref, cur_ref, nxt_ref, ol_c_ref, ol_n_ref, dol_c_ref, dol_n_ref, d_ref):
        i = pl.program_id(0)
        has_prev = (i % blocks_per_class) != 0
        has_next = ((i + 1) % blocks_per_class) != 0
        m_d, m_band = _att_masks()
        m_p = jnp.logical_and(m_band, has_prev)
        m_n = jnp.logical_and(m_band, has_next)

        def pair(q, k, v, lse, do, dterm, mask):
            s = jnp.where(mask, _bdot(q, k, 2, 2) * 0.125, NEG)
            p = jnp.exp(s - lse)
            dp = _bdot(do, v, 2, 2)
            ds = (p * (dp + dterm) * 0.125).astype(BF16)
            return p.astype(BF16), ds

        q_c = _att_heads(cur_ref, 0).astype(BF16)
        k_c = _att_heads(cur_ref, 256).astype(BF16)
        v_c = _att_heads(cur_ref, 512).astype(BF16)
        k_p = _att_heads(prev_ref, 256).astype(BF16)
        v_p = _att_heads(prev_ref, 512).astype(BF16)
        q_n = _att_heads(nxt_ref, 0).astype(BF16)
        o_c = _att_heads(ol_c_ref, 0)
        o_n = _att_heads(ol_n_ref, 0)
        lse_c = _att_heads(ol_c_ref, 256)[:, :, 0:1]
        lse_n = _att_heads(ol_n_ref, 256)[:, :, 0:1]
        do_c = _att_heads(dol_c_ref, 0)
        do_n = _att_heads(dol_n_ref, 0)
        t_c = _att_heads(dol_c_ref, 256)[:, :, 0:1] - jnp.sum(do_c * o_c, axis=-1, keepdims=True)
        t_n = _att_heads(dol_n_ref, 256)[:, :, 0:1] - jnp.sum(do_n * o_n, axis=-1, keepdims=True)
        do_cb = do_c.astype(BF16)
        do_nb = do_n.astype(BF16)
        p1, ds1 = pair(q_c, k_c, v_c, lse_c, do_cb, t_c, m_d)
        _p2, ds2 = pair(q_c, k_p, v_p, lse_c, do_cb, t_c, m_p)
        p3, ds3 = pair(q_n, k_c, v_c, lse_n, do_nb, t_n, m_n)
        dq = _bdot(ds1, k_c, 2, 1) + _bdot(ds2, k_p, 2, 1)
        dk = _bdot(ds1, q_c, 1, 1) + _bdot(ds3, q_n, 1, 1)
        dv = _bdot(p1, do_cb, 1, 1) + _bdot(p3, do_nb, 1, 1)
        for hd in range(4):
            a = ATT_E * hd
            d_ref[:, a:a + ATT_E] = dq[hd]
            d_ref[:, 256 + a:256 + a + ATT_E] = dk[hd]
            d_ref[:, 512 + a:512 + a + ATT_E] = dv[hd]

    prv = lambda i: (jnp.maximum(i - 1, 0), 0)
    cur = lambda i: (i, 0)
    nxt = lambda i: (jnp.minimum(i + 1, nblk - 1), 0)
    return pl.pallas_call(
        body, name=name, grid=(nblk,),
        in_specs=[pl.BlockSpec((ATT_BLK, 768), prv), pl.BlockSpec((ATT_BLK, 768), cur),
                  pl.BlockSpec((ATT_BLK, 768), nxt),
                  pl.BlockSpec((ATT_BLK, 512), cur), pl.BlockSpec((ATT_BLK, 512), nxt),
                  pl.BlockSpec((ATT_BLK, 512), cur), pl.BlockSpec((ATT_BLK, 512), nxt)],
        out_specs=pl.BlockSpec((ATT_BLK, 768), cur),
        out_shape=SDS((SEQ, 768), F32),
        compiler_params=_cparams(("arbitrary",)),
    )(att, att, att, ol, ol, dol, dol)


def to_classes(a, d):
    if d == 1:
        return a
    w = a.shape[1]
    return a.reshape(SEQ // d, d, w).transpose(1, 0, 2).reshape(SEQ, w)


def from_classes(a, d):
    if d == 1:
        return a
    w = a.shape[1]
    return a.reshape(d, SEQ // d, w).transpose(1, 0, 2).reshape(SEQ, w)


N_ATT_BLK = SEQ // ATT_BLK


def _class_rows(i, d):
    per_class = N_ATT_BLK // d
    shift = per_class.bit_length() - 1
    r = i >> shift
    j = i & (per_class - 1)
    span = ATT_BLK * d
    start = r + span * j
    prev = jnp.where(j == 0, start, start - span)
    nxt = jnp.where(j == per_class - 1, start, start + span)

    def rows(s0):
        if d == 1:
            return pl.ds(pl.multiple_of(s0, ATT_BLK), ATT_BLK)
        return pl.ds(s0, ATT_BLK, stride=d)

    return rows(start), rows(prev), rows(nxt), j != 0, j != per_class - 1


def _slab_heads(ref, slab, rows):
    x0 = ref[pl.ds(slab, 1), rows, :][0]
    x1 = ref[pl.ds(slab + 1, 1), rows, :][0]
    return jnp.stack([x0[:, 0:ATT_E], x0[:, ATT_E:], x1[:, 0:ATT_E], x1[:, ATT_E:]], axis=0)


def _put_slab_heads(ref, slab, rows, val):
    ref[pl.ds(slab, 1), rows, :] = jnp.concatenate([val[0], val[1]], axis=1)[None]
    ref[pl.ds(slab + 1, 1), rows, :] = jnp.concatenate([val[2], val[3]], axis=1)[None]


def _resident_call(body, ins, out_slabs, d, name):
    n_in = len(ins)

    def wrapped(*refs):
        hbm_in, hbm_out = refs[:n_in], refs[n_in]
        vm_in, vm_out, sem = refs[n_in + 1:2 * n_in + 1], refs[2 * n_in + 1], refs[2 * n_in + 2]
        i = pl.program_id(0)

        @pl.when(i == 0)
        def _():
            copies = [pltpu.make_async_copy(h, v, sem.at[k]) for k, (h, v) in enumerate(zip(hbm_in, vm_in))]
            for cp in copies:
                cp.start()
            for cp in copies:
                cp.wait()

        body(i, *vm_in, vm_out)

        @pl.when(i == N_ATT_BLK - 1)
        def _():
            cp = pltpu.make_async_copy(vm_out, hbm_out, sem.at[n_in])
            cp.start()
            cp.wait()

    return pl.pallas_call(
        wrapped, name=name, grid=(N_ATT_BLK,),
        in_specs=[ANY_SPEC] * n_in, out_specs=ANY_SPEC, out_shape=SDS((out_slabs, SEQ, 128), F32),
        scratch_shapes=[pltpu.VMEM(a.shape, a.dtype) for a in ins] + [pltpu.VMEM((out_slabs, SEQ, 128), F32),
                                                                      pltpu.SemaphoreType.DMA((n_in + 1,))],
        compiler_params=_cparams(("arbitrary",)),
    )(*ins)


def att_fwd_s(att, d, name):
    def body(i, att_ref, o_ref):
        cur, prev, _nxt, has_prev, _has_next = _class_rows(i, d)
        m_d, m_p = _att_masks()
        m_p = jnp.logical_and(m_p, has_prev)
        q = _slab_heads(att_ref, 0, cur).astype(BF16)
        kc = _slab_heads(att_ref, 2, cur).astype(BF16)
        vc = _slab_heads(att_ref, 4, cur).astype(BF16)
        kp = _slab_heads(att_ref, 2, prev).astype(BF16)
        vp = _slab_heads(att_ref, 4, prev).astype(BF16)
        sd = jnp.where(m_d, _bdot(q, kc, 2, 2) * 0.125, NEG)
        sp = jnp.where(m_p, _bdot(q, kp, 2, 2) * 0.125, NEG)
        m = jnp.maximum(jnp.max(sd, axis=-1, keepdims=True), jnp.max(sp, axis=-1, keepdims=True))
        pd = jnp.exp(sd - m)
        pp = jnp.exp(sp - m)
        den = jnp.sum(pd, axis=-1, keepdims=True) + jnp.sum(pp, axis=-1, keepdims=True)
        inv = 1.0 / den
        o = _bdot((pd * inv).astype(BF16), vc, 2, 1) + _bdot((pp * inv).astype(BF16), vp, 2, 1)
        _put_slab_heads(o_ref, 0, cur, o)
        _put_slab_heads(o_ref, 2, cur, jnp.broadcast_to(m + jnp.log(den), (4, ATT_BLK, ATT_E)))

    return _resident_call(body, [att], 4, d, name)


def att_bwd_s(att, ol, dol, d, name):
    def body(i, att_ref, ol_ref, dol_ref, d_ref):
        cur, prev, nxt, has_prev, has_next = _class_rows(i, d)
        m_d, m_band = _att_masks()
        m_p = jnp.logical_and(m_band, has_prev)
        m_n = jnp.logical_and(m_band, has_next)

        def pair(q, k, v, lse, do, dterm, mask):
            s = jnp.where(mask, _bdot(q, k, 2, 2) * 0.125, NEG)
            p = jnp.exp(s - lse)
            dp = _bdot(do, v, 2, 2)
            ds = (p * (dp + dterm) * 0.125).astype(BF16)
            return p.astype(BF16), ds

        q_c = _slab_heads(att_ref, 0, cur).astype(BF16)
        k_c = _slab_heads(att_ref, 2, cur).astype(BF16)
        v_c = _slab_heads(att_ref, 4, cur).astype(BF16)
        k_p = _slab_heads(att_ref, 2, prev).astype(BF16)
        v_p = _slab_heads(att_ref, 4, prev).astype(BF16)
        q_n = _slab_heads(att_ref, 0, nxt).astype(BF16)
        o_c = _slab_heads(ol_ref, 0, cur)
        o_n = _slab_heads(ol_ref, 0, nxt)
        lse_c = _slab_heads(ol_ref, 2, cur)[:, :, 0:1]
        lse_n = _slab_heads(ol_ref, 2, nxt)[:, :, 0:1]
        do_c = _slab_heads(dol_ref, 0, cur)
        do_n = _slab_heads(dol_ref, 0, nxt)
        t_c = _slab_heads(dol_ref, 2, cur)[:, :, 0:1] - jnp.sum(do_c * o_c, axis=-1, keepdims=True)
        t_n = _slab_heads(dol_ref, 2, nxt)[:, :, 0:1] - jnp.sum(do_n * o_n, axis=-1, keepdims=True)
        do_cb = do_c.astype(BF16)
        do_nb = do_n.astype(BF16)
        p1, ds1 = pair(q_c, k_c, v_c, lse_c, do_cb, t_c, m_d)
        _p2, ds2 = pair(q_c, k_p, v_p, lse_c, do_cb, t_c, m_p)
        p3, ds3 = pair(q_n, k_c, v_c, lse_n, do_nb, t_n, m_n)
        _put_slab_heads(d_ref, 0, cur, _bdot(ds1, k_c, 2, 1) + _bdot(ds2, k_p, 2, 1))
        _put_slab_heads(d_ref, 2, cur, _bdot(ds1, q_c, 1, 1) + _bdot(ds3, q_n, 1, 1))
        _put_slab_heads(d_ref, 4, cur, _bdot(p1, do_cb, 1, 1) + _bdot(p3, do_nb, 1, 1))

    return _resident_call(body, [att, ol, dol], 6, d, name)


def _shift_down(x, k):
    rows = lax.broadcasted_iota(jnp.int32, x.shape, 0)
    return jnp.where(rows >= k, pltpu.roll(x, k, 0), 0.0)


def _shift_up(x, k):
    n = x.shape[0]
    rows = lax.broadcasted_iota(jnp.int32, x.shape, 0)
    return jnp.where(rows < n - k, pltpu.roll(x, n - k, 0), 0.0)


@functools.partial(jax.custom_vjp, nondiff_argnums=(1,))
def _delay(x, k):
    return _shift_down(x, k)


def _delay_fwd(x, k):
    return _shift_down(x, k), None


def _delay_bwd(k, _res, g):
    return (_shift_up(g, k),)


_delay.defvjp(_delay_fwd, _delay_bwd)

DN_CONV = 4


def _dn_prep_fn(u, w, kind):
    y = w[DN_CONV - 1:DN_CONV] * u
    for j in range(DN_CONV - 1):
        y = y + w[j:j + 1] * _delay(u, DN_CONV - 1 - j)
    y = y * _sigmoid(y)
    nrm = y * lax.rsqrt(jnp.sum(y * y, axis=-1, keepdims=True) + EPS)
    return jnp.where(kind == 0, nrm * (DN_E ** -0.5), jnp.where(kind == 1, nrm, y))


def dn_prep_fwd(rest, conv_w, name):
    def body(u_ref, w_ref, o_ref):
        j = pl.program_id(0)
        kind = (j >= DN_H).astype(jnp.int32) + (j >= 2 * DN_H).astype(jnp.int32)
        o_ref[...] = _dn_prep_fn(u_ref[...], w_ref[...], kind)

    return pl.pallas_call(
        body, name=name, grid=(3 * DN_H,),
        in_specs=[pl.BlockSpec((SEQ, DN_E), lambda j: (0, j)),
                  pl.BlockSpec((DN_CONV, DN_E), lambda j: (0, j))],
        out_specs=pl.BlockSpec((SEQ, DN_E), lambda j: (0, j)),
        out_shape=SDS((SEQ, 3 * DN_W), F32),
        compiler_params=_cparams(("arbitrary",)),
    )(rest, conv_w)


def dn_prep_bwd(rest, conv_w, dqkv, name):
    def body(u_ref, w_ref, g_ref, du_ref, dw_ref):
        j = pl.program_id(0)
        kind = (j >= DN_H).astype(jnp.int32) + (j >= 2 * DN_H).astype(jnp.int32)
        _y, vjp = jax.vjp(lambda u, w: _dn_prep_fn(u, w, kind), u_ref[...], w_ref[...])
        du, dw = vjp(g_ref[...])
        du_ref[...] = du
        dw_ref[...] = dw

    return pl.pallas_call(
        body, name=name, grid=(3 * DN_H,),
        in_specs=[pl.BlockSpec((SEQ, DN_E), lambda j: (0, j)),
                  pl.BlockSpec((DN_CONV, DN_E), lambda j: (0, j)),
                  pl.BlockSpec((SEQ, DN_E), lambda j: (0, j))],
        out_specs=[pl.BlockSpec((SEQ, DN_E), lambda j: (0, j)),
                   pl.BlockSpec((DN_CONV, DN_E), lambda j: (0, j))],
        out_shape=[SDS((SEQ, 3 * DN_W), F32), SDS((DN_CONV, 3 * DN_W), F32)],
        compiler_params=_cparams(("arbitrary",)),
    )(rest, conv_w, dqkv)


def _bdot(a, b, ca, cb, prec=None):
    return lax.dot_general(a, b, (((ca,), (cb,)), ((0,), (0,))), preferred_element_type=F32, precision=prec)


def _unit_lower_inverse(a):
    eye = (lax.broadcasted_iota(jnp.int32, (DN_C, DN_C), 0) == lax.broadcasted_iota(jnp.int32, (DN_C, DN_C), 1)).astype(F32)
    p = eye - a
    b = _bdot(a, a, 2, 1, INV_PREC)
    for lvl in range(5):
        p = p + _bdot(p, b, 2, 1, INV_PREC)
        if lvl < 4:
            b = _bdot(b, b, 2, 1, INV_PREC)
    return p


@jax.custom_vjp
def _tri_inv(a):
    return _unit_lower_inverse(a)


def _tri_inv_fwd(a):
    t = _unit_lower_inverse(a)
    return t, t


def _tri_inv_bwd(t, g):
    return (-_bdot(_bdot(t, g, 1, 1, INV_PREC), t, 2, 2, INV_PREC),)


_tri_inv.defvjp(_tri_inv_fwd, _tri_inv_bwd)


def _b16(x):
    return x.astype(BF16)


def _dn_chunk(q, k, v, bb, ab, alog, dtb, state):
    ri = lax.broadcasted_iota(jnp.int32, (DN_C, DN_C), 0)
    ci = lax.broadcasted_iota(jnp.int32, (DN_C, DN_C), 1)
    lower = ri >= ci
    strict = ri > ci
    nh = q.shape[0]
    beta = _sigmoid(bb)
    xg = ab + dtb
    softplus = jnp.maximum(xg, 0.0) + jnp.log(1.0 + jnp.exp(-jnp.abs(xg)))
    gi = -jnp.exp(alog) * softplus
    g = _bdot(jnp.broadcast_to(lower.astype(F32), (nh, DN_C, DN_C)), gi, 2, 1, HI)
    eg = jnp.exp(g)
    kb = k * beta
    vb = v * beta
    g_col = g[:, :, 0:DN_C]
    g_row = _bdot(jnp.full((nh, DN_C, DN_E), 1.0 / DN_E, F32), g, 2, 2, HI)
    decay = jnp.where(lower, jnp.exp(jnp.where(lower, g_col - g_row, 0.0)), 0.0)
    kbf = _b16(k)
    a = jnp.where(strict, _bdot(_b16(kb), kbf, 2, 2) * decay, 0.0)
    t = _tri_inv(a)
    tb = _b16(t)
    u = _bdot(tb, _b16(vb), 2, 1)
    w = _bdot(tb, _b16(kb * eg), 2, 1)
    intra = jnp.where(lower, _bdot(_b16(q), kbf, 2, 2) * decay, 0.0)
    sb = _b16(state)
    v_new = u - _bdot(_b16(w), sb, 2, 1)
    o = _bdot(_b16(q * eg), sb, 2, 1) + _bdot(_b16(intra), _b16(v_new), 2, 1)
    g_last = g[:, DN_C - 1:DN_C, :]
    k_dec = k * jnp.exp(g_last - g)
    new_state = state * jnp.exp(g_last) + _bdot(_b16(k_dec), _b16(v_new), 1, 1)
    return o, new_state


def _heads(ref, base=0):
    return jnp.stack([ref[:, base + DN_E * hd:base + DN_E * hd + DN_E] for hd in range(DN_H)], axis=0)


def _put_heads(ref, val, base=0):
    for hd in range(DN_H):
        ref[:, base + DN_E * hd:base + DN_E * hd + DN_E] = val[hd]


def _dn_args(qkv_ref, bb_ref, ab_ref, alog_ref, dtb_ref, state):
    return (_heads(qkv_ref), _heads(qkv_ref, DN_W), _heads(qkv_ref, 2 * DN_W), _heads(bb_ref), _heads(ab_ref),
            _heads(alog_ref), _heads(dtb_ref), state)


def dn_chunk_fwd(qkv, rest, alog_b, dtb_b, name, exch=None):
    def body(qkv_ref, bb_ref, ab_ref, alog_ref, dtb_ref, o_ref, st_ref, state_scr):
        n = pl.program_id(0)

        @pl.when(n == 0)
        def _():
            state_scr[...] = jnp.zeros_like(state_scr)

        st = state_scr[...]
        st_ref[0] = st
        o, ns = _dn_chunk(*_dn_args(qkv_ref, bb_ref, ab_ref, alog_ref, dtb_ref, st))
        _put_heads(o_ref, o)
        state_scr[...] = ns

    return _call(
        body, name=name, grid=(N_CHUNK,),
        in_specs=[pl.BlockSpec((DN_C, 3 * DN_W), lambda n: (n, 0)),
                  pl.BlockSpec((DN_C, DN_W), lambda n: (n, R_BB // DN_W)),
                  pl.BlockSpec((DN_C, DN_W), lambda n: (n, R_AB // DN_W)),
                  pl.BlockSpec((1, DN_W), lambda n: (0, 0)),
                  pl.BlockSpec((1, DN_W), lambda n: (0, 0))],
        out_specs=[pl.BlockSpec((DN_C, DN_W), lambda n: (n, 0)),
                   pl.BlockSpec((1, DN_H, DN_E, DN_E), lambda n: (n, 0, 0, 0))],
        out_shape=[SDS((SEQ, DN_W), F32), SDS((N_CHUNK, DN_H, DN_E, DN_E), F32)],
        scratch_shapes=[pltpu.VMEM((DN_H, DN_E, DN_E), F32)],
        sem=("arbitrary",), args=(qkv, rest, rest, alog_b, dtb_b), exch=exch)


def dn_chunk_bwd(qkv, rest, alog_b, dtb_b, states, do, name, exch=None):
    last = N_CHUNK - 1

    def body(qkv_ref, bb_ref, ab_ref, alog_ref, dtb_ref, st_ref, do_ref,
             dqkv_ref, dbb_ref, dab_ref, dalog_ref, ddtb_ref, dstate_scr):
        s = pl.program_id(0)

        @pl.when(s == 0)
        def _():
            dstate_scr[...] = jnp.zeros_like(dstate_scr)
            dalog_ref[...] = jnp.zeros_like(dalog_ref)
            ddtb_ref[...] = jnp.zeros_like(ddtb_ref)

        _out, vjp = jax.vjp(_dn_chunk, *_dn_args(qkv_ref, bb_ref, ab_ref, alog_ref, dtb_ref, st_ref[0]))
        dq, dk, dv, dbb, dab, dalog, ddtb, dst = vjp((_heads(do_ref), dstate_scr[...]))
        _put_heads(dqkv_ref, dq)
        _put_heads(dqkv_ref, dk, DN_W)
        _put_heads(dqkv_ref, dv, 2 * DN_W)
        _put_heads(dbb_ref, dbb)
        _put_heads(dab_ref, dab)
        _put_heads(dalog_ref, _heads(dalog_ref) + dalog)
        _put_heads(ddtb_ref, _heads(ddtb_ref) + ddtb)
        dstate_scr[...] = dst

    rev = lambda w: (lambda s: (last - s, w))
    return _call(
        body, name=name, grid=(N_CHUNK,),
        in_specs=[pl.BlockSpec((DN_C, 3 * DN_W), rev(0)),
                  pl.BlockSpec((DN_C, DN_W), rev(R_BB // DN_W)),
                  pl.BlockSpec((DN_C, DN_W), rev(R_AB // DN_W)),
                  pl.BlockSpec((1, DN_W), lambda s: (0, 0)),
                  pl.BlockSpec((1, DN_W), lambda s: (0, 0)),
                  pl.BlockSpec((1, DN_H, DN_E, DN_E), lambda s: (last - s, 0, 0, 0)),
                  pl.BlockSpec((DN_C, DN_W), rev(0))],
        out_specs=[pl.BlockSpec((DN_C, 3 * DN_W), rev(0)),
                   pl.BlockSpec((DN_C, DN_W), rev(0)),
                   pl.BlockSpec((DN_C, DN_W), rev(0)),
                   pl.BlockSpec((1, DN_W), lambda s: (0, 0)),
                   pl.BlockSpec((1, DN_W), lambda s: (0, 0))],
        out_shape=[SDS((SEQ, 3 * DN_W), F32), SDS((SEQ, DN_W), F32), SDS((SEQ, DN_W), F32),
                   SDS((1, DN_W), F32), SDS((1, DN_W), F32)],
        scratch_shapes=[pltpu.VMEM((DN_H, DN_E, DN_E), F32)],
        sem=("arbitrary",), args=(qkv, rest, rest, alog_b, dtb_b, states, do), exch=exch)


DN_G_LOG2 = 3
DN_G = 1 << DN_G_LOG2
N_INST = DN_G * DN_H


def _dn_intra(q, k, v, bb, ab, alog, dtb):
    ri = lax.broadcasted_iota(jnp.int32, (DN_C, DN_C), 0)
    ci = lax.broadcasted_iota(jnp.int32, (DN_C, DN_C), 1)
    lower = ri >= ci
    strict = ri > ci
    nh = q.shape[0]
    beta = _sigmoid(bb)
    xg = ab + dtb
    softplus = jnp.maximum(xg, 0.0) + jnp.log(1.0 + jnp.exp(-jnp.abs(xg)))
    gi = -jnp.exp(alog) * softplus
    g = _bdot(jnp.broadcast_to(lower.astype(F32), (nh, DN_C, DN_C)), gi, 2, 1, HI)
    eg = jnp.exp(g)
    kb = k * beta
    vb = v * beta
    g_col = g[:, :, 0:DN_C]
    g_row = _bdot(jnp.full((nh, DN_C, DN_E), 1.0 / DN_E, F32), g, 2, 2, HI)
    decay = jnp.where(lower, jnp.exp(jnp.where(lower, g_col - g_row, 0.0)), 0.0)
    kbf = _b16(k)
    a = jnp.where(strict, _bdot(_b16(kb), kbf, 2, 2) * decay, 0.0)
    tb = _b16(_tri_inv(a))
    u = _bdot(tb, _b16(vb), 2, 1)
    w = _bdot(tb, _b16(kb * eg), 2, 1)
    intra = jnp.where(lower, _bdot(_b16(q), kbf, 2, 2) * decay, 0.0)
    g_last = g[:, DN_C - 1:DN_C, :]
    return u, w, q * eg, k * jnp.exp(g_last - g), intra, jnp.exp(g_last)


def _dn_inter(u, w, qg, kdec, intra, egl, state):
    sb = _b16(state)
    v_new = u - _bdot(_b16(w), sb, 2, 1)
    o = _bdot(_b16(qg), sb, 2, 1) + _bdot(_b16(intra), _b16(v_new), 2, 1)
    return o, state * egl + _bdot(_b16(kdec), _b16(v_new), 1, 1)


def _inst(ref, base=0):
    per_head = [ref[:, base + DN_E * hd:base + DN_E * hd + DN_E].reshape(DN_G, DN_C, DN_E) for hd in range(DN_H)]
    return jnp.concatenate(per_head, axis=0)


def _inst_rows(ref):
    rows = [jnp.broadcast_to(ref[:, DN_E * hd:DN_E * hd + DN_E][None], (DN_G, 1, DN_E)) for hd in range(DN_H)]
    return jnp.concatenate(rows, axis=0)


def _put_inst(ref, val, width=DN_E, base=0):
    for hd in range(DN_H):
        ref[:, base + width * hd:base + width * hd + width] = val[DN_G * hd:DN_G * hd + DN_G].reshape(DN_G * DN_C, width)


def _intra_args(qkv_ref, bb_ref, ab_ref, alog_ref, dtb_ref):
    return (_inst(qkv_ref), _inst(qkv_ref, DN_W), _inst(qkv_ref, 2 * DN_W), _inst(bb_ref), _inst(ab_ref),
            _inst_rows(alog_ref), _inst_rows(dtb_ref))


def _intra_in_specs():
    t = DN_G * DN_C
    return [pl.BlockSpec((t, 3 * DN_W), lambda n: (n, 0)),
            pl.BlockSpec((t, DN_W), lambda n: (n, R_BB // DN_W)),
            pl.BlockSpec((t, DN_W), lambda n: (n, R_AB // DN_W)),
            pl.BlockSpec((1, DN_W), lambda n: (0, 0)),
            pl.BlockSpec((1, DN_W), lambda n: (0, 0))]


def dn_intra_fwd(qkv, rest, alog_b, dtb_b, name, exch=None):
    t = DN_G * DN_C

    def body(qkv_ref, bb_ref, ab_ref, alog_ref, dtb_ref, u_ref, w_ref, qg_ref, kd_ref, in_ref, egl_ref):
        u, w, qg, kdec, intra, egl = _dn_intra(*_intra_args(qkv_ref, bb_ref, ab_ref, alog_ref, dtb_ref))
        _put_inst(u_ref, u)
        _put_inst(w_ref, w.astype(BF16))
        _put_inst(qg_ref, qg.astype(BF16))
        _put_inst(kd_ref, kdec.astype(BF16))
        _put_inst(in_ref, intra.astype(BF16), DN_C)
        for hd in range(DN_H):
            egl_ref[:, DN_E * hd:DN_E * hd + DN_E] = egl[DN_G * hd:DN_G * hd + DN_G].reshape(DN_G, DN_E)

    row = lambda w_: pl.BlockSpec((t, w_), lambda n: (n, 0))
    return _call(
        body, name=name, grid=(N_CHUNK // DN_G,), in_specs=_intra_in_specs(),
        out_specs=[row(DN_W), row(DN_W), row(DN_W), row(DN_W), row(DN_H * DN_C),
                   pl.BlockSpec((DN_G, DN_W), lambda n: (n, 0))],
        out_shape=[SDS((SEQ, DN_W), F32), SDS((SEQ, DN_W), BF16), SDS((SEQ, DN_W), BF16), SDS((SEQ, DN_W), BF16),
                   SDS((SEQ, DN_H * DN_C), BF16), SDS((N_CHUNK, DN_W), F32)],
        scratch_shapes=[], sem=("arbitrary",), args=(qkv, rest, rest, alog_b, dtb_b), exch=exch)


def dn_intra_bwd(qkv, rest, alog_b, dtb_b, du, dw, dqg, dkd, dintra, degl, name):
    t = DN_G * DN_C

    def body(qkv_ref, bb_ref, ab_ref, alog_ref, dtb_ref, du_ref, dw_ref, dqg_ref, dkd_ref, din_ref, degl_ref,
             dqkv_ref, dbb_ref, dab_ref, dalog_ref, ddtb_ref):
        @pl.when(pl.program_id(0) == 0)
        def _():
            dalog_ref[...] = jnp.zeros_like(dalog_ref)
            ddtb_ref[...] = jnp.zeros_like(ddtb_ref)

        _out, vjp = jax.vjp(_dn_intra, *_intra_args(qkv_ref, bb_ref, ab_ref, alog_ref, dtb_ref))
        d_in = jnp.concatenate([din_ref[:, DN_C * hd:DN_C * hd + DN_C].reshape(DN_G, DN_C, DN_C) for hd in range(DN_H)], axis=0)
        d_egl = jnp.concatenate([degl_ref[:, DN_E * hd:DN_E * hd + DN_E].reshape(DN_G, 1, DN_E) for hd in range(DN_H)], axis=0)
        dq, dk, dv, dbb, dab, dalog, ddtb = vjp((_inst(du_ref), _inst(dw_ref), _inst(dqg_ref), _inst(dkd_ref), d_in, d_egl))
        _put_inst(dqkv_ref, dq)
        _put_inst(dqkv_ref, dk, DN_E, DN_W)
        _put_inst(dqkv_ref, dv, DN_E, 2 * DN_W)
        _put_inst(dbb_ref, dbb)
        _put_inst(dab_ref, dab)
        for hd in range(DN_H):
            sl = slice(DN_E * hd, DN_E * hd + DN_E)
            dalog_ref[:, sl] += jnp.sum(dalog[DN_G * hd:DN_G * hd + DN_G], axis=0)
            ddtb_ref[:, sl] += jnp.sum(ddtb[DN_G * hd:DN_G * hd + DN_G], axis=0)

    row = lambda w_: pl.BlockSpec((t, w_), lambda n: (n, 0))
    acc = pl.BlockSpec((1, DN_W), lambda n: (0, 0))
    return pl.pallas_call(
        body, name=name, grid=(N_CHUNK // DN_G,),
        in_specs=_intra_in_specs() + [row(DN_W), row(DN_W), row(DN_W), row(DN_W), row(DN_H * DN_C),
                                      pl.BlockSpec((DN_G, DN_W), lambda n: (n, 0))],
        out_specs=[row(3 * DN_W), row(DN_W), row(DN_W), acc, acc],
        out_shape=[SDS((SEQ, 3 * DN_W), F32), SDS((SEQ, DN_W), F32), SDS((SEQ, DN_W), F32),
                   SDS((1, DN_W), F32), SDS((1, DN_W), F32)],
        compiler_params=_cparams(("arbitrary",)),
    )(qkv, rest, rest, alog_b, dtb_b, du, dw, dqg, dkd, dintra, degl)


def _inter_args(u_ref, w_ref, qg_ref, kd_ref, in_ref, egl_ref, n, state):
    f = lambda r: _heads(r).astype(F32)
    intra = jnp.stack([in_ref[:, DN_C * hd:DN_C * hd + DN_C] for hd in range(DN_H)], axis=0).astype(F32)
    egl = _heads(egl_ref.at[pl.ds(n & (DN_G - 1), 1), :])
    return f(u_ref), f(w_ref), f(qg_ref), f(kd_ref), intra, egl, state


def dn_inter_fwd(u, w, qg, kdec, intra, egl, name, exch=None):
    def body(u_ref, w_ref, qg_ref, kd_ref, in_ref, egl_ref, o_ref, st_ref, state_scr):
        n = pl.program_id(0)

        @pl.when(n == 0)
        def _():
            state_scr[...] = jnp.zeros_like(state_scr)

        st = state_scr[...]
        st_ref[0] = st
        o, ns = _dn_inter(*_inter_args(u_ref, w_ref, qg_ref, kd_ref, in_ref, egl_ref, n, st))
        _put_heads(o_ref, o)
        state_scr[...] = ns

    row = lambda w_: pl.BlockSpec((DN_C, w_), lambda n: (n, 0))
    return _call(
        body, name=name, grid=(N_CHUNK,),
        in_specs=[row(DN_W), row(DN_W), row(DN_W), row(DN_W), row(DN_H * DN_C),
                  pl.BlockSpec((DN_G, DN_W), lambda n: (n >> DN_G_LOG2, 0))],
        out_specs=[row(DN_W), pl.BlockSpec((1, DN_H, DN_E, DN_E), lambda n: (n, 0, 0, 0))],
        out_shape=[SDS((SEQ, DN_W), F32), SDS((N_CHUNK, DN_H, DN_E, DN_E), F32)],
        scratch_shapes=[pltpu.VMEM((DN_H, DN_E, DN_E), F32)],
        sem=("arbitrary",), args=(u, w, qg, kdec, intra, egl), exch=exch)


def dn_inter_bwd(u, w, qg, kdec, intra, egl, states, do, name):
    last = N_CHUNK - 1

    def body(u_ref, w_ref, qg_ref, kd_ref, in_ref, egl_ref, st_ref, do_ref,
             du_ref, dw_ref, dqg_ref, dkd_ref, din_ref, degl_ref, dstate_scr):
        s = pl.program_id(0)
        n = last - s

        @pl.when(s == 0)
        def _():
            dstate_scr[...] = jnp.zeros_like(dstate_scr)

        _out, vjp = jax.vjp(_dn_inter, *_inter_args(u_ref, w_ref, qg_ref, kd_ref, in_ref, egl_ref, n, st_ref[0]))
        du, dw, dqg, dkd, din, degl, dst = vjp((_heads(do_ref), dstate_scr[...]))
        _put_heads(du_ref, du)
        _put_heads(dw_ref, dw)
        _put_heads(dqg_ref, dqg)
        _put_heads(dkd_ref, dkd)
        for hd in range(DN_H):
            din_ref[:, DN_C * hd:DN_C * hd + DN_C] = din[hd]
        row = n & (DN_G - 1)

        @pl.when(row == DN_G - 1)
        def _():
            degl_ref[...] = jnp.zeros_like(degl_ref)

        new_row = jnp.concatenate([degl[hd] for hd in range(DN_H)], axis=1)
        rows = lax.broadcasted_iota(jnp.int32, (DN_G, DN_W), 0)
        degl_ref[...] = jnp.where(rows == row, jnp.broadcast_to(new_row, (DN_G, DN_W)), degl_ref[...])
        dstate_scr[...] = dst

    rev = lambda w_: pl.BlockSpec((DN_C, w_), lambda s: (last - s, 0))
    grp = pl.BlockSpec((DN_G, DN_W), lambda s: ((last - s) >> DN_G_LOG2, 0))
    return pl.pallas_call(
        body, name=name, grid=(N_CHUNK,),
        in_specs=[rev(DN_W), rev(DN_W), rev(DN_W), rev(DN_W), rev(DN_H * DN_C), grp,
                  pl.BlockSpec((1, DN_H, DN_E, DN_E), lambda s: (last - s, 0, 0, 0)), rev(DN_W)],
        out_specs=[rev(DN_W), rev(DN_W), rev(DN_W), rev(DN_W), rev(DN_H * DN_C), grp],
        out_shape=[SDS((SEQ, DN_W), F32)] * 4 + [SDS((SEQ, DN_H * DN_C), F32), SDS((N_CHUNK, DN_W), F32)],
        scratch_shapes=[pltpu.VMEM((DN_H, DN_E, DN_E), F32)],
        compiler_params=_cparams(("arbitrary",)),
    )(u, w, qg, kdec, intra, egl, states, do)


OUT_T = 256


def _pool_consts(rows_total, t0, halo_before):
    lane = lax.broadcasted_iota(jnp.int32, (rows_total, POOL_W), 1)
    row = lax.broadcasted_iota(jnp.int32, (rows_total, POOL_W), 0)
    grp = (lane >= 64).astype(jnp.int32) + (lane >= 128).astype(jnp.int32) + (lane >= 192).astype(jnp.int32)
    win = jnp.where(grp == 0, 2, jnp.where(grp == 1, 4, jnp.where(grp == 2, 8, 16)))
    pos = t0 + row - halo_before
    cnt = jnp.minimum(pos + 1, win).astype(F32)
    return grp, cnt


def _pool_select(grp, s2, s4, s8, s16):
    return jnp.where(grp == 0, s2, jnp.where(grp == 1, s4, jnp.where(grp == 2, s8, s16)))


def _pooled(u_ext, t0):
    n = u_ext.shape[0]
    grp, cnt = _pool_consts(n, t0, POOL_HALO)
    s2 = u_ext + pltpu.roll(u_ext, 1, 0)
    s4 = s2 + pltpu.roll(s2, 2, 0)
    s8 = s4 + pltpu.roll(s4, 4, 0)
    s16 = s8 + pltpu.roll(s8, 8, 0)
    out = _pool_select(grp, s2, s4, s8, s16) / jnp.maximum(cnt, 1.0) - u_ext
    return out[POOL_HALO:, :]


def _merge_weights(l1, l4, l16):
    m = jnp.maximum(jnp.maximum(l1, l4), l16)
    e1 = jnp.exp(l1 - m)
    e4 = jnp.exp(l4 - m)
    e16 = jnp.exp(l16 - m)
    inv = 1.0 / (e1 + e4 + e16)
    return e1 * inv, e4 * inv, e16 * inv


def _out_parts(ol1_ref, ol4_ref, ol16_ref, pu_ref, puh_ref, odn_ref, z_ref, wbd_ref, i, t):
    w1, w4, w16 = _merge_weights(_slabs_load(ol1_ref, 2, 2), _slabs_load(ol4_ref, 2, 2), _slabs_load(ol16_ref, 2, 2))
    ya = w1 * _slabs_load(ol1_ref, 0, 2) + w4 * _slabs_load(ol4_ref, 0, 2) + w16 * _slabs_load(ol16_ref, 0, 2)
    halo = jnp.where(i > 0, puh_ref[...], 0.0)
    pooled = _pooled(jnp.concatenate([halo, pu_ref[...]], axis=0), i * t)
    pw = _dot(pooled.astype(BF16), wbd_ref[...])
    return ya, pooled, pw, (w1, w4, w16)


def _out_specs_common(t):
    def row(w, cb=0):
        return pl.BlockSpec((t, w), lambda i: (i, cb))

    halo = pl.BlockSpec((POOL_HALO, POOL_W),
                        lambda i: (jnp.maximum(i * (t // POOL_HALO) - 1, 0), R_PU // POOL_W))
    full = lambda a, b: pl.BlockSpec((a, b), lambda i: (0, 0))
    return [_slab_spec(4, t), _slab_spec(4, t), _slab_spec(4, t), row(POOL_W, R_PU // POOL_W), halo, row(DN_W), row(DN_W, R_DZ // DN_W),
            full(POOL_W, POOL_W), full(1, POOL_W), full(1, DN_W), full(D_MODEL, D_MODEL)]


def mix_out_fwd(x, ol1, ol4, ol16, rest, odn, wbd, scale, onorm_b, wout, name):
    t = OUT_T

    def body(x_ref, ol1_ref, ol4_ref, ol16_ref, pu_ref, puh_ref, odn_ref, z_ref, wbd_ref, sc_ref, on_ref, wo_ref, o_ref):
        i = pl.program_id(0)
        ya, _pooled_v, pw, _w = _out_parts(ol1_ref, ol4_ref, ol16_ref, pu_ref, puh_ref, odn_ref, z_ref, wbd_ref, i, t)
        yb = pw * sc_ref[...]
        acc = x_ref[...] + _dot(ya.astype(BF16), wo_ref[0:256, :]) + _dot(yb.astype(BF16), wo_ref[256:512, :])
        for hd in range(DN_H):
            sl = slice(DN_E * hd, DN_E * hd + DN_E)
            oh, _r = _rms_stats(odn_ref[:, sl])
            z = z_ref[:, sl]
            yc = oh * on_ref[:, sl] * (z * _sigmoid(z))
            acc = acc + _dot(yc.astype(BF16), wo_ref[512 + DN_E * hd:512 + DN_E * hd + DN_E, :])
        o_ref[...] = acc

    return pl.pallas_call(
        body, name=name, grid=(SEQ // t,),
        in_specs=[pl.BlockSpec((t, D_MODEL), lambda i: (i, 0))] + _out_specs_common(t),
        out_specs=pl.BlockSpec((t, D_MODEL), lambda i: (i, 0)),
        out_shape=SDS((SEQ, D_MODEL), F32),
        compiler_params=_cparams(("arbitrary",)),
    )(x, ol1, ol4, ol16, rest, rest, odn, rest, wbd, scale, onorm_b, wout)


def mix_out_bwd(dxo, ol1, ol4, ol16, rest, odn, wbd, scale, onorm_b, wout, headsum, name):
    t = OUT_T

    def body(dxo_ref, ol1_ref, ol4_ref, ol16_ref, pu_ref, puh_ref, odn_ref, z_ref, wbd_ref, sc_ref, on_ref, wo_ref, hs_ref,
             dwo_ref, d1_ref, d4_ref, d16_ref, dpl_ref, dodn_ref, dz_ref, dsc_ref, don_ref, dwbd_ref):
        i = pl.program_id(0)

        @pl.when(i == 0)
        def _():
            dwo_ref[...] = jnp.zeros_like(dwo_ref)
            dsc_ref[...] = jnp.zeros_like(dsc_ref)
            don_ref[...] = jnp.zeros_like(don_ref)
            dwbd_ref[...] = jnp.zeros_like(dwbd_ref)

        ya, pooled, pw, (w1, w4, w16) = _out_parts(ol1_ref, ol4_ref, ol16_ref, pu_ref, puh_ref, odn_ref, z_ref, wbd_ref, i, t)
        sc = sc_ref[...]
        dxb = dxo_ref[...].astype(BF16)
        dwo_ref[0:256, :] += _dot_tn(ya.astype(BF16), dxb)
        dwo_ref[256:512, :] += _dot_tn((pw * sc).astype(BF16), dxb)
        dya = _dot_nt(dxb, wo_ref[0:256, :])
        o1 = _slabs_load(ol1_ref, 0, 2)
        o4 = _slabs_load(ol4_ref, 0, 2)
        o16 = _slabs_load(ol16_ref, 0, 2)
        hs = hs_ref[...]
        s1 = _dot(dya * o1, hs, HI)
        s4 = _dot(dya * o4, hs, HI)
        s16 = _dot(dya * o16, hs, HI)
        sbar = w1 * s1 + w4 * s4 + w16 * s16
        _slabs_store(d1_ref, 0, w1 * dya)
        _slabs_store(d1_ref, 2, w1 * (s1 - sbar))
        _slabs_store(d4_ref, 0, w4 * dya)
        _slabs_store(d4_ref, 2, w4 * (s4 - sbar))
        _slabs_store(d16_ref, 0, w16 * dya)
        _slabs_store(d16_ref, 2, w16 * (s16 - sbar))
        dyb = _dot_nt(dxb, wo_ref[256:512, :])
        dsc_ref[...] += jnp.sum(dyb * pw, axis=0, keepdims=True)
        dpw = (dyb * sc).astype(BF16)
        dwbd_ref[...] += _dot_tn(pooled.astype(BF16), dpw)
        dpl_ref[...] = _dot_nt(dpw, wbd_ref[...])
        for hd in range(DN_H):
            sl = slice(DN_E * hd, DN_E * hd + DN_E)
            rows_w = slice(512 + DN_E * hd, 512 + DN_E * hd + DN_E)
            oh, r = _rms_stats(odn_ref[:, sl])
            z = z_ref[:, sl]
            sg = _sigmoid(z)
            sz = z * sg
            nw = on_ref[:, sl]
            on = oh * nw
            dwo_ref[rows_w, :] += _dot_tn((on * sz).astype(BF16), dxb)
            dyc = _dot_nt(dxb, wo_ref[rows_w, :])
            dz_ref[:, sl] = dyc * on * (sg * (1.0 + z * (1.0 - sg)))
            dx, dw = _rms_bwd(oh, r, nw, dyc * sz)
            dodn_ref[:, sl] = dx
            don_ref[:, sl] += dw

    row = lambda w: pl.BlockSpec((t, w), lambda i: (i, 0))
    full = lambda a, b: pl.BlockSpec((a, b), lambda i: (0, 0))
    return pl.pallas_call(
        body, name=name, grid=(SEQ // t,),
        in_specs=[row(D_MODEL)] + _out_specs_common(t) + [full(ATT_W, ATT_W)],
        out_specs=[full(D_MODEL, D_MODEL), _slab_spec(4, t), _slab_spec(4, t), _slab_spec(4, t), row(POOL_W), row(DN_W), row(DN_W),
                   full(1, POOL_W), full(1, DN_W), full(POOL_W, POOL_W)],
        out_shape=[SDS((D_MODEL, D_MODEL), F32), SDS((4, SEQ, 128), F32), SDS((4, SEQ, 128), F32), SDS((4, SEQ, 128), F32),
                   SDS((SEQ, POOL_W), F32), SDS((SEQ, DN_W), F32), SDS((SEQ, DN_W), F32),
                   SDS((1, POOL_W), F32), SDS((1, DN_W), F32), SDS((POOL_W, POOL_W), F32)],
        compiler_params=_cparams(("arbitrary",)),
    )(dxo, ol1, ol4, ol16, rest, rest, odn, rest, wbd, scale, onorm_b, wout, headsum)


def pool_bwd(dpooled, name):
    t = 512
    nt = SEQ // t

    def body(d_ref, dn_ref, o_ref):
        i = pl.program_id(0)
        halo = jnp.where(i < nt - 1, dn_ref[...], 0.0)
        d_ext = jnp.concatenate([d_ref[...], halo], axis=0)
        n = t + POOL_HALO
        grp, cnt = _pool_consts(n, i * t, 0)
        dq = d_ext / cnt
        s2 = dq + pltpu.roll(dq, n - 1, 0)
        s4 = s2 + pltpu.roll(s2, n - 2, 0)
        s8 = s4 + pltpu.roll(s4, n - 4, 0)
        s16 = s8 + pltpu.roll(s8, n - 8, 0)
        o_ref[...] = (_pool_select(grp, s2, s4, s8, s16) - d_ext)[0:t, :]

    return pl.pallas_call(
        body, name=name, grid=(nt,),
        in_specs=[pl.BlockSpec((t, POOL_W), lambda i: (i, 0)),
                  pl.BlockSpec((POOL_HALO, POOL_W),
                               lambda i: (jnp.minimum((i + 1) * (t // POOL_HALO), SEQ // POOL_HALO - 1), 0))],
        out_specs=pl.BlockSpec((t, POOL_W), lambda i: (i, 0)),
        out_shape=SDS((SEQ, POOL_W), F32),
        compiler_params=_cparams(("arbitrary",)),
    )(dpooled, dpooled)


N_PEER = N_DEV - 1
ANY_SPEC = pl.BlockSpec(memory_space=pl.ANY)


class Exchange:
    def __init__(self, arrays, mode):
        self.arrays = list(arrays)
        self.mode = mode
        n = len(self.arrays)
        if mode == "scatter":
            self.out_shape = [SDS(a.shape, a.dtype) for a in self.arrays]
        else:
            self.out_shape = [SDS((N_DEV,) + a.shape, a.dtype) for a in self.arrays]
        self.scratch = [pltpu.SemaphoreType.DMA((n * N_PEER,)), pltpu.SemaphoreType.DMA((n * N_PEER,)),
                        pltpu.SemaphoreType.DMA((n,))]

    @staticmethod
    def _place():
        x, y, c = lax.axis_index("x"), lax.axis_index("y"), lax.axis_index("c")
        chips = [(1 - x, y), (x, 1 - y), (1 - x, 1 - y)]
        return x, y, c, chips

    @staticmethod
    def _copy(sems, a, k, src, dst, to):
        send_sems, recv_sems, _ = sems
        return pltpu.make_async_remote_copy(
            src_ref=src, dst_ref=dst, send_sem=send_sems.at[a * N_PEER + k], recv_sem=recv_sems.at[a * N_PEER + k],
            device_id=to, device_id_type=MESH)

    def _scatter_peers(self):
        x, y, c, _ = self._place()
        out = []
        for fx, fy, fc in ((0, 0, 1), (1, 0, 0), (0, 1, 0), (1, 1, 0), (1, 0, 1), (0, 1, 1), (1, 1, 1)):
            px, py, pc = x ^ fx, y ^ fy, c ^ fc
            out.append(((px, py, pc), 4 * px + 2 * py + pc))
        return 4 * x + 2 * y + c, out

    def _local(self, ins, outs, sems, a, me):
        src = ins[a].at[me] if self.mode == "scatter" else ins[a]
        return pltpu.make_async_copy(src, outs[a].at[me], sems[2].at[a])

    def start(self, ins, outs, sems):
        if self.mode == "scatter":
            me, peers = self._scatter_peers()
            for a in range(len(ins)):
                self._local(ins, outs, sems, a, me).start()
                for k, (peer, pidx) in enumerate(peers):
                    self._copy(sems, a, k, ins[a].at[pidx], outs[a].at[me], peer).start()
            return
        x, y, c, chips = self._place()
        me = 4 * x + 2 * y + c
        for a in range(len(ins)):
            self._local(ins, outs, sems, a, me).start()
            self._copy(sems, a, 0, ins[a], outs[a].at[me], (x, y, 1 - c)).start()
            for j, (cx, cy) in enumerate(chips):
                self._copy(sems, a, 1 + j, ins[a], outs[a].at[me], (cx, cy, c)).start()

    def finish(self, ins, outs, sems):
        n = len(ins)
        if self.mode == "scatter":
            me, peers = self._scatter_peers()
            for a in range(n):
                for k, (peer, pidx) in enumerate(peers):
                    self._copy(sems, a, k, ins[a].at[pidx], outs[a].at[pidx], peer).wait_recv()
            for a in range(n):
                for k, (peer, pidx) in enumerate(peers):
                    self._copy(sems, a, k, ins[a].at[pidx], outs[a].at[me], peer).wait_send()
                self._local(ins, outs, sems, a, me).wait()
            return
        x, y, c, chips = self._place()
        me = 4 * x + 2 * y + c
        sib = (x, y, 1 - c)
        for a in range(n):
            for j, (cx, cy) in enumerate(chips):
                blk = outs[a].at[4 * cx + 2 * cy + c]
                self._copy(sems, a, 1 + j, ins[a], blk, (cx, cy, c)).wait_recv()
                self._copy(sems, a, 4 + j, blk, blk, sib).start()
        for a in range(n):
            self._copy(sems, a, 0, ins[a], outs[a].at[4 * x + 2 * y + (1 - c)], sib).wait_recv()
            for j, (cx, cy) in enumerate(chips):
                blk = outs[a].at[4 * cx + 2 * cy + (1 - c)]
                self._copy(sems, a, 4 + j, blk, blk, sib).wait_recv()
        for a in range(n):
            for k in range(N_PEER):
                self._copy(sems, a, k, ins[a], outs[a].at[me], sib).wait_send()
            self._local(ins, outs, sems, a, me).wait()


def run_exchange(exch, name):
    n = len(exch.arrays)

    def body(*refs):
        ins, outs, sems = refs[:n], refs[n:2 * n], refs[2 * n:]
        exch.start(ins, outs, sems)
        exch.finish(ins, outs, sems)

    return pl.pallas_call(
        body, name=name, in_specs=[ANY_SPEC] * n, out_specs=[ANY_SPEC] * n, out_shape=exch.out_shape,
        scratch_shapes=exch.scratch,
    )(*exch.arrays)


def _call(body, *, name, grid, in_specs, out_specs, out_shape, scratch_shapes, sem, args, exch=None):
    if exch is None:
        res = pl.pallas_call(body, name=name, grid=grid, in_specs=in_specs, out_specs=out_specs, out_shape=out_shape,
                             scratch_shapes=scratch_shapes, compiler_params=_cparams(sem))(*args)
        return res, None
    single = not isinstance(out_shape, (list, tuple))
    out_specs_l = [out_specs] if single else list(out_specs)
    out_shape_l = [out_shape] if single else list(out_shape)
    n_in, n_out, n_scr, m = len(in_specs), len(out_specs_l), len(scratch_shapes), len(exch.arrays)

    def wrapped(*refs):
        p = 0
        ins = refs[p:p + n_in]; p += n_in
        xin = refs[p:p + m]; p += m
        outs = refs[p:p + n_out]; p += n_out
        xout = refs[p:p + m]; p += m
        scr = refs[p:p + n_scr]; p += n_scr
        sems = refs[p:]
        ids = [pl.program_id(ax) for ax in range(len(grid))]
        first = functools.reduce(jnp.logical_and, [i == 0 for i in ids])
        last = functools.reduce(jnp.logical_and, [i == g - 1 for i, g in zip(ids, grid)])

        @pl.when(first)
        def _():
            exch.start(xin, xout, sems)

        body(*ins, *outs, *scr)

        @pl.when(last)
        def _():
            exch.finish(xin, xout, sems)

    res = pl.pallas_call(
        wrapped, name=name, grid=grid, in_specs=list(in_specs) + [ANY_SPEC] * m,
        out_specs=out_specs_l + [ANY_SPEC] * m, out_shape=out_shape_l + exch.out_shape,
        scratch_shapes=list(scratch_shapes) + exch.scratch, compiler_params=_cparams(sem),
    )(*args, *exch.arrays)
    outs = res[:n_out]
    return (outs[0] if single else outs), res[n_out:]


def _adam_math(w, g, m, v):
    m2 = ADAM_B1 * m + (1.0 - ADAM_B1) * g
    v2 = ADAM_B2 * v + (1.0 - ADAM_B2) * (g * g)
    m_hat = m2 / (1.0 - ADAM_B1 ** ADAM_STEP)
    v_hat = v2 / (1.0 - ADAM_B2 ** ADAM_STEP)
    delta = -ADAM_LR * (m_hat / (jnp.sqrt(v_hat) + ADAM_EPS) + ADAM_WD * w)
    return delta, m2, v2


ADAM_ROW_BLOCKS = 2


def adam_shard(parts0, parts1, w, m, v, name, part_slice=None):
    _, r, c = w.shape
    sub = part_slice
    rb = r // ADAM_ROW_BLOCKS

    def body(p0_ref, p1_ref, w_ref, m_ref, v_ref, g_ref, d_ref, m2_ref, v2_ref):
        def run(p_ref):
            g = p_ref[0].astype(F32)
            for i in range(1, N_DEV):
                g = g + p_ref[i].astype(F32)
            delta, m2, v2 = _adam_math(w_ref[0], g, m_ref[0], v_ref[0])
            g_ref[0] = g
            d_ref[0] = delta
            m2_ref[0] = m2
            v2_ref[0] = v2

        @pl.when(pl.program_id(0) == 0)
        def _():
            run(p0_ref)

        @pl.when(pl.program_id(0) == 1)
        def _():
            run(p1_ref)

    def p_spec(layer):
        row = (lambda l, j: jnp.where(l == 0, j, ADAM_ROW_BLOCKS - 1)) if layer == 0 else (lambda l, j: jnp.where(l == 1, j, 0))
        if sub is None:
            return pl.BlockSpec((N_DEV, rb, c), lambda l, j: (0, row(l, j), 0))
        return pl.BlockSpec((N_DEV, None, rb, c), lambda l, j: (0, sub, row(l, j), 0))

    blk = pl.BlockSpec((1, rb, c), lambda l, j: (l, j, 0))
    return pl.pallas_call(
        body, name=name, grid=(DEPTH, ADAM_ROW_BLOCKS),
        in_specs=[p_spec(0), p_spec(1), blk, blk, blk], out_specs=[blk] * 4,
        out_shape=[SDS(w.shape, F32)] * 4,
        compiler_params=_cparams(("arbitrary", "arbitrary")),
    )(parts0, parts1, w, m, v)


def adam_small(parts, w, m, v, name):
    def body(p_ref, w_ref, m_ref, v_ref, g_ref, d_ref, m2_ref, v2_ref):
        g = p_ref[0]
        for i in range(1, N_DEV):
            g = g + p_ref[i]
        delta, m2, v2 = _adam_math(w_ref[...], g, m_ref[...], v_ref[...])
        g_ref[...] = g
        d_ref[...] = delta
        m2_ref[...] = m2
        v2_ref[...] = v2

    return pl.pallas_call(
        body, name=name, out_shape=[SDS(w.shape, F32)] * 4, compiler_params=_cparams(),
    )(parts, w, m, v)


def _rot_cols(w):
    w4 = w.reshape(w.shape[0], 4, 2, 32)
    return jnp.stack([-w4[:, :, 1], w4[:, :, 0]], axis=2).reshape(w.shape[0], ATT_W)


def _rot_cols_t(dw_rot):
    d4 = dw_rot.reshape(dw_rot.shape[0], 4, 2, 32)
    return jnp.stack([d4[:, :, 1], -d4[:, :, 0]], axis=2).reshape(dw_rot.shape[0], ATT_W)


def build_wext(w_in):
    aq, ak, av, pu = w_in[:, 0:256], w_in[:, 256:512], w_in[:, 512:768], w_in[:, 768:1024]
    dqkvz = w_in[:, 1024:3072]
    gates = jnp.repeat(w_in[:, 3072:3080], DN_E, axis=1)
    return jnp.concatenate([aq, ak, av, _rot_cols(aq), _rot_cols(ak), dqkvz, gates, pu], axis=1)


def fold_dwext(d):
    b = EXT_ATT
    aq = d[:, 0:256] + _rot_cols_t(d[:, 768:1024])
    ak = d[:, 256:512] + _rot_cols_t(d[:, 1024:1280])
    av = d[:, 512:768]
    dqkvz = d[:, b:b + 2048]
    gates = d[:, b + R_BB:b + R_BB + 1024].reshape(d.shape[0], 8, DN_E).sum(axis=-1)
    pu = d[:, b + R_PU:b + R_PU + 256]
    return jnp.concatenate([aq, ak, av, pu, dqkvz, gates], axis=1)


def _block_diag(pw):
    z = jnp.zeros((4, 64, 4, 64), pw.dtype)
    for g in range(4):
        z = z.at[g, :, g, :].set(pw[g])
    return z.reshape(POOL_W, POOL_W)


def _diag_blocks(m):
    m4 = m.reshape(4, 64, 4, 64)
    return jnp.stack([m4[g, :, g, :] for g in range(4)], axis=0)


def _lanes(v, reps):
    return jnp.repeat(v, reps)[None, :]


def layer_fwd(p, xa, cos, sin, l, host=None):
    host = host or {}

    def carried(key):
        return host[key][0] if key in host else None

    def done(key, xo):
        if key in host:
            host[key][1](xo)

    xb, xo = ffn_fwd(xa, p["n1"], p["f1gu"], p["f1d"], f"ffn1_fwd_{l}", carried("ffn1"))
    done("ffn1", xo)
    att, rest = mix_in_fwd(xb, p["nm"], p["wext"], cos, sin, f"mix_in_fwd_{l}")
    ols = [att_fwd_s(att, d, f"att_fwd_{l}_{d}") for d in DILATIONS]
    qkv = dn_prep_fwd(rest, p["conv"], f"dn_prep_fwd_{l}")
    dn, xo = dn_intra_fwd(qkv, rest, p["alog"], p["dtb"], f"dn_intra_fwd_{l}", carried("dn_intra"))
    done("dn_intra", xo)
    (odn, states), xo = dn_inter_fwd(*dn, f"dn_inter_fwd_{l}", carried("dn_inter"))
    done("dn_inter", xo)
    xc = mix_out_fwd(xb, ols[0], ols[1], ols[2], rest, odn, p["wbd"], p["scale"], p["onorm"], p["wout"], f"mix_out_fwd_{l}")
    xd, xo = ffn_fwd(xc, p["n2"], p["f2gu"], p["f2d"], f"ffn2_fwd_{l}", carried("ffn2"))
    done("ffn2", xo)
    return xd, dict(xa=xa, xb=xb, xc=xc, att=att, rest=rest, ols=ols, qkv=qkv, dn=dn, odn=odn, states=states)


def layer_bwd(p, s, dx, cos, sin, headsum, l, scatter=False, carry=None):
    (dx, d_f2gu, d_f2d, d_n2), carried = ffn_bwd(s["xc"], dx, p["n2"], p["f2gu"], p["f2d"], f"ffn2_bwd_{l}", carry)
    (d_wout, dol1, dol4, dol16, dpooled, dodn, dz, dscale, donorm, dwbd) = mix_out_bwd(
        dx, s["ols"][0], s["ols"][1], s["ols"][2], s["rest"], s["odn"], p["wbd"], p["scale"], p["onorm"], p["wout"],
        headsum, f"mix_out_bwd_{l}")
    dpu = pool_bwd(dpooled, f"pool_bwd_{l}")
    f2 = list(ffn_grads_to_shards(d_f2gu, d_f2d))
    d_dn = dn_inter_bwd(*s["dn"], s["states"], dodn, f"dn_inter_bwd_{l}")
    dqkv, dbb, dab, dalog, ddtb = dn_intra_bwd(s["qkv"], s["rest"], p["alog"], p["dtb"], *d_dn, f"dn_intra_bwd_{l}")
    d_dqkv, dconv = dn_prep_bwd(s["rest"], p["conv"], dqkv, f"dn_prep_bwd_{l}")
    datts = [att_bwd_s(s["att"], ol, dol, d, f"att_bwd_{l}_{d}")
             for d, ol, dol in zip(DILATIONS, s["ols"], (dol1, dol4, dol16))]
    dproj = assemble_dproj(datts, cos, sin, d_dqkv, dz, dbb, dab, dpu, f"assemble_dproj_{l}")
    dx, d_wext, d_nm = linear_bwd(s["xb"], dx, p["nm"], dproj, p["wext"], f"mix_in_bwd_{l}")
    d_win = fold_dwext(d_wext).reshape(D_MODEL, N_DEV, IN_BLK).transpose(1, 0, 2).astype(BF16)
    io = [d_win, d_wout.reshape(N_DEV, D_MODEL // N_DEV, D_MODEL).astype(BF16)]
    (dx, d_f1gu, d_f1d, d_n1), xo = ffn_bwd(s["xa"], dx, p["n1"], p["f1gu"], p["f1d"], f"ffn1_bwd_{l}",
                                           Exchange(f2 + io, "scatter") if scatter else None)
    if scatter:
        f2, io = list(xo[:2]), list(xo[2:])
    big = dict(f1=list(ffn_grads_to_shards(d_f1gu, d_f1d)), f2=f2, io=io)
    small = dict(ffn1_norm=d_n1[0], mix_norm=d_nm[0], ffn2_norm=d_n2[0], pool_w=_diag_blocks(dwbd),
                 pool_scale=dscale[0], dn_a_log=dalog.reshape(DN_H, DN_E).sum(-1),
                 dn_dt_bias=ddtb.reshape(DN_H, DN_E).sum(-1),
                 dn_out_norm=donorm.reshape(DN_H, DN_E).sum(0), dn_conv_w=dconv)
    return dx, big, small, carried


def small_operands(l, pool_w, pool_scale, dn_out_norm, dn_a_log, dn_dt_bias, ffn1_norm, mix_norm, ffn2_norm):
    return dict(
        wbd=_block_diag(pool_w[l]).astype(BF16),
        scale=pool_scale[l][None, :],
        onorm=jnp.tile(dn_out_norm[l], DN_H)[None, :],
        alog=_lanes(dn_a_log[l], DN_E),
        dtb=_lanes(dn_dt_bias[l], DN_E),
        n1=ffn1_norm[l][None, :], nm=mix_norm[l][None, :], n2=ffn2_norm[l][None, :])


def set_mixer_weights(p, win_g, wout_g, conv_g):
    p["wext"] = build_wext(win_g.transpose(1, 0, 2).reshape(D_MODEL, IN_W))
    p["wout"] = wout_g.reshape(D_MODEL, D_MODEL)
    p["conv"] = conv_g.transpose(1, 0, 2).reshape(DN_CONV, 3 * DN_W)


def rope_tables(pos):
    inv_freq = 10000.0 ** (-jnp.arange(0, ATT_E, 2, dtype=F32) / ATT_E)
    ang = pos.astype(F32)[:, None] * inv_freq
    return jnp.tile(jnp.cos(ang), (1, 8)), jnp.tile(jnp.sin(ang), (1, 8))


def head_sum_matrix():
    return jnp.kron(jnp.eye(4, dtype=F32), jnp.ones((ATT_E, ATT_E), F32))


SMALL_NAMES = ("ffn1_norm", "mix_norm", "ffn2_norm", "pool_w", "pool_scale", "dn_a_log", "dn_dt_bias",
               "dn_out_norm", "final_norm", "dn_conv_w")


def _pack(parts):
    flat = jnp.concatenate([p.reshape(-1) for p in parts])
    n = flat.shape[0]
    rows = -(-n // 1024) * 8
    return jnp.pad(flat, (0, rows * 128 - n)).reshape(rows, 128)


def _unpack(packed, shapes):
    flat = packed.reshape(-1)
    out, off = [], 0
    for s in shapes:
        n = math.prod(s)
        out.append(flat[off:off + n].reshape(s))
        off += n
    return out


def kernel(x, positions, ffn1_norm, ffn1_w_gate, ffn1_w_up, ffn1_w_down, mix_norm, w_in, pool_w, pool_scale, dn_conv_w, dn_a_log, dn_dt_bias, dn_out_norm, w_out, ffn2_norm, ffn2_w_gate, ffn2_w_up, ffn2_w_down, final_norm, loss_target, m_ffn1_norm, m_ffn1_w_gate, m_ffn1_w_up, m_ffn1_w_down, m_mix_norm, m_w_in, m_pool_w, m_pool_scale, m_dn_conv_w, m_dn_a_log, m_dn_dt_bias, m_dn_out_norm, m_w_out, m_ffn2_norm, m_ffn2_w_gate, m_ffn2_w_up, m_ffn2_w_down, m_final_norm, v_ffn1_norm, v_ffn1_w_gate, v_ffn1_w_up, v_ffn1_w_down, v_mix_norm, v_w_in, v_pool_w, v_pool_scale, v_dn_conv_w, v_dn_a_log, v_dn_dt_bias, v_dn_out_norm, v_w_out, v_ffn2_norm, v_ffn2_w_gate, v_ffn2_w_up, v_ffn2_w_down, v_final_norm):
    me = 4 * lax.axis_index("x") + 2 * lax.axis_index("y") + lax.axis_index("c")
    x0 = x[0]
    target = loss_target[0]

    cos, sin = rope_tables(positions[0])
    headsum = head_sum_matrix()

    layers = [small_operands(l, pool_w, pool_scale, dn_out_norm, dn_a_log, dn_dt_bias, ffn1_norm, mix_norm, ffn2_norm)
              for l in range(DEPTH)]

    def ffn_shards(gate, up, down, l):
        return [jnp.stack([gate[l], up[l]]).astype(BF16), down[l].astype(BF16)]

    def gather_ffn1(l):
        def on_done(xo):
            layers[l]["f1gu"], layers[l]["f1d"] = ffn_weights_from_shards(*xo)
        return Exchange(ffn_shards(ffn1_w_gate, ffn1_w_up, ffn1_w_down, l), "gather"), on_done

    def gather_mixer(l):
        def on_done(xo):
            set_mixer_weights(layers[l], *xo)
        return Exchange([w_in[l].astype(BF16), w_out[l].astype(BF16), dn_conv_w[l]], "gather"), on_done

    gathered_f2 = {}

    def gather_ffn2_part(l, part):
        def on_done(xo):
            gathered_f2[(l, part)] = xo[0]
            if (l, 0) in gathered_f2 and (l, 1) in gathered_f2:
                layers[l]["f2gu"], layers[l]["f2d"] = ffn_weights_from_shards(gathered_f2[(l, 0)], gathered_f2[(l, 1)])
        return Exchange([ffn_shards(ffn2_w_gate, ffn2_w_up, ffn2_w_down, l)[part]], "gather"), on_done

    first, on_first = gather_ffn1(0)
    on_first(run_exchange(first, "gather_ffn1_0"))
    saved = []
    xa = x0
    for l in range(DEPTH):
        host = {"ffn1": gather_mixer(l), "dn_intra": gather_ffn2_part(l, 0), "dn_inter": gather_ffn2_part(l, 1)}
        if l + 1 < DEPTH:
            host["ffn2"] = gather_ffn1(l + 1)
        xa, s = layer_fwd(layers[l], xa, cos, sin, l, host)
        saved.append(s)

    loss_row, dx, d_final = loss_head(xa, final_norm[None, :], target, "loss_head")
    loss = lax.psum(loss_row[0, 0], ("x", "y", "c"))

    small = {}
    big_parts = [None] * DEPTH
    carry = None
    for l in reversed(range(DEPTH)):
        dx, big, small[l], carried = layer_bwd(layers[l], saved[l], dx, cos, sin, headsum, l, True, carry)
        if carried is not None:
            big_parts[l + 1]["f1"] = list(carried)
        big_parts[l] = big
        carry = Exchange(big["f1"], "scatter")
    big_parts[0]["f1"] = list(run_exchange(carry, "scatter_ffn1_0"))
    grad_x = dx[None]

    small_shapes = {"ffn1_norm": (DEPTH, D_MODEL), "mix_norm": (DEPTH, D_MODEL), "ffn2_norm": (DEPTH, D_MODEL),
                    "pool_w": (DEPTH, 4, 64, 64), "pool_scale": (DEPTH, POOL_W), "dn_a_log": (DEPTH, DN_H),
                    "dn_dt_bias": (DEPTH, DN_H), "dn_out_norm": (DEPTH, DN_E), "final_norm": (D_MODEL,),
                    "dn_conv_w": (DEPTH, DN_CONV, 3 * DN_W)}
    g_small = {n: (d_final[0] if n == "final_norm" else jnp.stack([small[l][n] for l in range(DEPTH)]))
               for n in SMALL_NAMES}
    (small_parts,) = run_exchange(Exchange([_pack([g_small[n] for n in SMALL_NAMES])], "gather"), "gather_small_grads")

    def conv_full(a):
        return lax.dynamic_update_slice(jnp.zeros((DEPTH, DN_CONV, 3 * DN_W), F32), a, (0, 0, me * (3 * DN_W // N_DEV)))

    given = dict(ffn1_norm=(ffn1_norm, m_ffn1_norm, v_ffn1_norm), mix_norm=(mix_norm, m_mix_norm, v_mix_norm),
                 ffn2_norm=(ffn2_norm, m_ffn2_norm, v_ffn2_norm), pool_w=(pool_w, m_pool_w, v_pool_w),
                 pool_scale=(pool_scale, m_pool_scale, v_pool_scale), dn_a_log=(dn_a_log, m_dn_a_log, v_dn_a_log),
                 dn_dt_bias=(dn_dt_bias, m_dn_dt_bias, v_dn_dt_bias),
                 dn_out_norm=(dn_out_norm, m_dn_out_norm, v_dn_out_norm),
                 final_norm=(final_norm, m_final_norm, v_final_norm),
                 dn_conv_w=(conv_full(dn_conv_w), conv_full(m_dn_conv_w), conv_full(v_dn_conv_w)))
    packed_wmv = [_pack([given[n][k] for n in SMALL_NAMES]) for k in range(3)]
    small_out = adam_small(small_parts, *packed_wmv, "adam_small")
    shapes = [small_shapes[n] for n in SMALL_NAMES]
    small_res = {n: [] for n in SMALL_NAMES}
    for arr in small_out:
        for n, v_ in zip(SMALL_NAMES, _unpack(arr, shapes)):
            if n == "dn_conv_w":
                v_ = lax.dynamic_slice(v_, (0, 0, me * (3 * DN_W // N_DEV)), (DEPTH, DN_CONV, 3 * DN_W // N_DEV))
            small_res[n].append(v_)

    def parts_of(group, idx):
        return [big_parts[l][group][idx] for l in range(DEPTH)]

    big_res = dict(
        ffn1_w_gate=adam_shard(*parts_of("f1", 0), ffn1_w_gate, m_ffn1_w_gate, v_ffn1_w_gate, "adam_ffn1_gate", 0),
        ffn1_w_up=adam_shard(*parts_of("f1", 0), ffn1_w_up, m_ffn1_w_up, v_ffn1_w_up, "adam_ffn1_up", 1),
        ffn1_w_down=adam_shard(*parts_of("f1", 1), ffn1_w_down, m_ffn1_w_down, v_ffn1_w_down, "adam_ffn1_down"),
        ffn2_w_gate=adam_shard(*parts_of("f2", 0), ffn2_w_gate, m_ffn2_w_gate, v_ffn2_w_gate, "adam_ffn2_gate", 0),
        ffn2_w_up=adam_shard(*parts_of("f2", 0), ffn2_w_up, m_ffn2_w_up, v_ffn2_w_up, "adam_ffn2_up", 1),
        ffn2_w_down=adam_shard(*parts_of("f2", 1), ffn2_w_down, m_ffn2_w_down, v_ffn2_w_down, "adam_ffn2_down"),
        w_in=adam_shard(*parts_of("io", 0), w_in, m_w_in, v_w_in, "adam_w_in"),
        w_out=adam_shard(*parts_of("io", 1), w_out, m_w_out, v_w_out, "adam_w_out"),
    )

    order = ("ffn1_norm", "ffn1_w_gate", "ffn1_w_up", "ffn1_w_down", "mix_norm", "w_in", "pool_w", "pool_scale",
             "dn_conv_w", "dn_a_log", "dn_dt_bias", "dn_out_norm", "w_out", "ffn2_norm", "ffn2_w_gate", "ffn2_w_up",
             "ffn2_w_down", "final_norm")
    res = {**small_res, **big_res}
    outs = [loss, grad_x]
    for k in range(4):
        outs.extend(res[n][k] for n in order)
    return tuple(outs)
```

```python
import functools
import math

import jax
import jax.numpy as jnp
from jax import lax
from jax.experimental import pallas as pl
from jax.experimental.pallas import tpu as pltpu

F32 = jnp.float32
BF16 = jnp.bfloat16
HI = lax.Precision.HIGHEST
INV_PREC = lax.Precision.HIGH
SDS = jax.ShapeDtypeStruct

N_DEV = 8
SEQ = 4096
D_MODEL = 1024
DEPTH = 2
D_FF = 2816
FF_BLK = D_FF // N_DEV
ATT_W = 256
ATT_E = 64
ATT_BLK = 128
DILATIONS = (1, 4, 16)
POOL_W = 256
POOL_HALO = 16
DN_W = 512
DN_H = 4
DN_E = 128
DN_C = 64
N_CHUNK = SEQ // DN_C
IN_W = 3080
IN_BLK = IN_W // N_DEV
EPS = 1e-6
EXT_ATT = 1280
EXT_REST = 3328
EXT_W = EXT_ATT + EXT_REST
R_DQKV, R_DZ, R_BB, R_AB, R_PU = 0, 1536, 2048, 2560, 3072

ADAM_LR, ADAM_B1, ADAM_B2, ADAM_EPS, ADAM_WD, ADAM_STEP = 0.001, 0.9, 0.999, 1e-08, 0.01, 10

VMEM_LIMIT = 60 * 1024 * 1024
MESH = pl.DeviceIdType.MESH


def _cparams(sem=None):
    kw = dict(vmem_limit_bytes=VMEM_LIMIT)
    if sem is not None:
        kw["dimension_semantics"] = sem
    return pltpu.CompilerParams(**kw)


def _dot(a, b, prec=None):
    return jnp.dot(a, b, preferred_element_type=F32, precision=prec)


def _dot_nt(a, b, prec=None):
    return lax.dot_general(a, b, (((1,), (1,)), ((), ())), preferred_element_type=F32, precision=prec)


def _dot_tn(a, b, prec=None):
    return lax.dot_general(a, b, (((0,), (0,)), ((), ())), preferred_element_type=F32, precision=prec)


def _sigmoid(x):
    return jax.nn.sigmoid(x)


def _rms_stats(x):
    r = lax.rsqrt(jnp.mean(x * x, axis=-1, keepdims=True) + EPS)
    return x * r, r


def _rms_bwd(xh, r, w, dh):
    dxh = dh * w
    dx = r * (dxh - xh * jnp.mean(dxh * xh, axis=-1, keepdims=True))
    return dx, jnp.sum(dh * xh, axis=0, keepdims=True)


FFN_T_FWD = 2048
FFN_T_BWD = 512
FF_TILE = 256
N_FF_TILE = D_FF // FF_TILE


def ffn_weights_from_shards(wgu_g, wd_g):
    return wgu_g.transpose(1, 2, 0, 3).reshape(2, D_MODEL, D_FF), wd_g.reshape(D_FF, D_MODEL)


def ffn_grads_to_shards(dwgu, dwd):
    return dwgu.reshape(2, D_MODEL, N_DEV, FF_BLK).transpose(2, 0, 1, 3), dwd.reshape(N_DEV, FF_BLK, D_MODEL)


def ffn_fwd(x, nw, wgu, wd, name, exch=None):
    t = FFN_T_FWD

    def body(x_ref, nw_ref, wgu_ref, wd_ref, o_ref, h_scr, acc_scr):
        k = pl.program_id(1)

        @pl.when(k == 0)
        def _():
            xh, _r = _rms_stats(x_ref[...])
            h_scr[...] = (xh * nw_ref[...]).astype(BF16)
            acc_scr[...] = jnp.zeros_like(acc_scr)

        h = h_scr[...]
        hg = _dot(h, wgu_ref[0])
        hu = _dot(h, wgu_ref[1])
        a = (hg * _sigmoid(hg) * hu).astype(BF16)
        acc_scr[...] += _dot(a, wd_ref[...])

        @pl.when(k == N_FF_TILE - 1)
        def _():
            o_ref[...] = x_ref[...] + 0.5 * acc_scr[...]

    return _call(
        body, name=name, grid=(SEQ // t, N_FF_TILE),
        in_specs=[pl.BlockSpec((t, D_MODEL), lambda i, k: (i, 0)),
                  pl.BlockSpec((1, D_MODEL), lambda i, k: (0, 0)),
                  pl.BlockSpec((2, D_MODEL, FF_TILE), lambda i, k: (0, 0, k)),
                  pl.BlockSpec((FF_TILE, D_MODEL), lambda i, k: (k, 0))],
        out_specs=pl.BlockSpec((t, D_MODEL), lambda i, k: (i, 0)),
        out_shape=SDS((SEQ, D_MODEL), F32),
        scratch_shapes=[pltpu.VMEM((t, D_MODEL), BF16), pltpu.VMEM((t, D_MODEL), F32)],
        sem=("arbitrary", "arbitrary"), args=(x, nw, wgu, wd), exch=exch)


def ffn_bwd(x, dxo, nw, wgu, wd, name, exch=None):
    t = FFN_T_BWD
    nt = SEQ // t

    def body(x_ref, dxo_ref, nw_ref, wgu_ref, wd_ref, dx_ref, dwgu_ref, dwd_ref, dnw_ref,
             dh_scr, ag_scr, au_scr, ad_scr, h_scr):
        k = pl.program_id(0)
        i = pl.program_id(1)
        rows = pl.ds(pl.multiple_of(i * t, t), t)
        nw_v = nw_ref[...]

        @pl.when(k == 0)
        def _():
            xh0, _r0 = _rms_stats(x_ref[...])
            h_scr[rows, :] = (xh0 * nw_v).astype(BF16)

        h = h_scr[rows, :]
        dy = (0.5 * dxo_ref[...]).astype(BF16)
        wg = wgu_ref[0]
        wu = wgu_ref[1]
        hg = _dot(h, wg)
        hu = _dot(h, wu)
        sg = _sigmoid(hg)
        sil = hg * sg
        a = (sil * hu).astype(BF16)
        da = _dot_nt(dy, wd_ref[...])
        dhu = (da * sil).astype(BF16)
        dhg = (da * hu * (sg * (1.0 + hg * (1.0 - sg)))).astype(BF16)
        p_d = _dot_tn(a, dy)
        p_g = _dot_tn(h, dhg)
        p_u = _dot_tn(h, dhu)
        dh = _dot_nt(dhg, wg) + _dot_nt(dhu, wu)

        @pl.when(i == 0)
        def _():
            ad_scr[...] = p_d
            ag_scr[...] = p_g
            au_scr[...] = p_u

        @pl.when(i > 0)
        def _():
            ad_scr[...] += p_d
            ag_scr[...] += p_g
            au_scr[...] += p_u

        @pl.when(i == nt - 1)
        def _():
            dwd_ref[...] = ad_scr[...].astype(BF16)
            dwgu_ref[0] = ag_scr[...].astype(BF16)
            dwgu_ref[1] = au_scr[...].astype(BF16)

        @pl.when(k == 0)
        def _():
            dh_scr[rows, :] = dh

        @pl.when(k > 0)
        def _():
            dh_scr[rows, :] += dh

        @pl.when(jnp.logical_and(k == 0, i == 0))
        def _():
            dnw_ref[...] = jnp.zeros_like(dnw_ref)

        @pl.when(k == N_FF_TILE - 1)
        def _():
            xh, r = _rms_stats(x_ref[...])
            dx, dw = _rms_bwd(xh, r, nw_v, dh_scr[rows, :])
            dx_ref[...] = dxo_ref[...] + dx
            dnw_ref[...] += dw

    last = N_FF_TILE - 1
    return _call(
        body, name=name, grid=(N_FF_TILE, nt),
        in_specs=[pl.BlockSpec((t, D_MODEL), lambda k, i: (i, 0)),
                  pl.BlockSpec((t, D_MODEL), lambda k, i: (i, 0)),
                  pl.BlockSpec((1, D_MODEL), lambda k, i: (0, 0)),
                  pl.BlockSpec((2, D_MODEL, FF_TILE), lambda k, i: (0, 0, k)),
                  pl.BlockSpec((FF_TILE, D_MODEL), lambda k, i: (k, 0))],
        out_specs=[pl.BlockSpec((t, D_MODEL), lambda k, i: (jnp.where(k == last, i, 0), 0)),
                   pl.BlockSpec((2, D_MODEL, FF_TILE), lambda k, i: (0, 0, k)),
                   pl.BlockSpec((FF_TILE, D_MODEL), lambda k, i: (k, 0)),
                   pl.BlockSpec((1, D_MODEL), lambda k, i: (0, 0))],
        out_shape=[SDS((SEQ, D_MODEL), F32), SDS((2, D_MODEL, D_FF), BF16),
                   SDS((D_FF, D_MODEL), BF16), SDS((1, D_MODEL), F32)],
        scratch_shapes=[pltpu.VMEM((SEQ, D_MODEL), F32), pltpu.VMEM((D_MODEL, FF_TILE), F32),
                        pltpu.VMEM((D_MODEL, FF_TILE), F32), pltpu.VMEM((FF_TILE, D_MODEL), F32),
                        pltpu.VMEM((SEQ, D_MODEL), BF16)],
        sem=("arbitrary", "arbitrary"), args=(x, dxo, nw, wgu, wd), exch=exch)


def loss_head(x, fw, target, name):
    t = 512

    def body(x_ref, fw_ref, tg_ref, loss_ref, dx_ref, dfw_ref):
        i = pl.program_id(0)
        xh, r = _rms_stats(x_ref[...])
        w = fw_ref[...]
        err = xh * w - tg_ref[...]
        part = 0.5 * jnp.sum(jnp.sum(err * err, axis=-1, keepdims=True), axis=0, keepdims=True) / D_MODEL
        dx, dw = _rms_bwd(xh, r, w, err * (1.0 / D_MODEL))
        dx_ref[...] = dx

        @pl.when(i == 0)
        def _():
            loss_ref[...] = jnp.zeros_like(loss_ref)
            dfw_ref[...] = jnp.zeros_like(dfw_ref)

        loss_ref[...] += jnp.broadcast_to(part, loss_ref.shape)
        dfw_ref[...] += dw

    return pl.pallas_call(
        body, name=name, grid=(SEQ // t,),
        in_specs=[pl.BlockSpec((t, D_MODEL), lambda i: (i, 0)),
                  pl.BlockSpec((1, D_MODEL), lambda i: (0, 0)),
                  pl.BlockSpec((t, D_MODEL), lambda i: (i, 0))],
        out_specs=[pl.BlockSpec((1, 128), lambda i: (0, 0)),
                   pl.BlockSpec((t, D_MODEL), lambda i: (i, 0)),
                   pl.BlockSpec((1, D_MODEL), lambda i: (0, 0))],
        out_shape=[SDS((1, 128), F32), SDS((SEQ, D_MODEL), F32), SDS((1, D_MODEL), F32)],
        compiler_params=_cparams(("arbitrary",)),
    )(x, fw, target)


MIX_T = 256


def _slabs_load(ref, first, n):
    return jnp.concatenate([ref[first + j] for j in range(n)], axis=1)


def _slabs_store(ref, first, val):
    for j in range(val.shape[1] // 128):
        ref[first + j] = val[:, 128 * j:128 * j + 128]


def _slab_spec(k, t):
    return pl.BlockSpec((k, t, 128), lambda i: (0, i, 0))


def mix_in_fwd(x, nw, wext, cos, sin, name):
    t = MIX_T

    def body(x_ref, nw_ref, w_ref, cos_ref, sin_ref, att_ref, rest_ref):
        xh, _r = _rms_stats(x_ref[...])
        h = (xh * nw_ref[...]).astype(BF16)
        pa = _dot(h, w_ref[:, 0:EXT_ATT])
        c = cos_ref[...]
        s = sin_ref[...]
        _slabs_store(att_ref, 0, pa[:, 0:256] * c + pa[:, 768:1024] * s)
        _slabs_store(att_ref, 2, pa[:, 256:512] * c + pa[:, 1024:1280] * s)
        _slabs_store(att_ref, 4, pa[:, 512:768])
        for j in range(EXT_REST // 256):
            rest_ref[:, 256 * j:256 * j + 256] = _dot(h, w_ref[:, EXT_ATT + 256 * j:EXT_ATT + 256 * j + 256])

    return pl.pallas_call(
        body, name=name, grid=(SEQ // t,),
        in_specs=[pl.BlockSpec((t, D_MODEL), lambda i: (i, 0)),
                  pl.BlockSpec((1, D_MODEL), lambda i: (0, 0)),
                  pl.BlockSpec((D_MODEL, EXT_W), lambda i: (0, 0)),
                  pl.BlockSpec((t, ATT_W), lambda i: (i, 0)),
                  pl.BlockSpec((t, ATT_W), lambda i: (i, 0))],
        out_specs=[_slab_spec(6, t),
                   pl.BlockSpec((t, EXT_REST), lambda i: (i, 0))],
        out_shape=[SDS((6, SEQ, 128), F32), SDS((SEQ, EXT_REST), F32)],
        compiler_params=_cparams(("arbitrary",)),
    )(x, nw, wext, cos, sin)


def assemble_dproj(datts, cos, sin, d_dqkv, dz, dbb, dab, dpu, name):
    t = 512

    def body(d1_ref, d4_ref, d16_ref, cos_ref, sin_ref, dqkv_ref, dz_ref, dbb_ref, dab_ref, dpu_ref, o_ref):
        da6 = d1_ref[...] + d4_ref[...] + d16_ref[...]
        da = jnp.concatenate([da6[j] for j in range(6)], axis=1)
        c = cos_ref[...]
        s = sin_ref[...]
        dq = da[:, 0:256]
        dk = da[:, 256:512]
        o_ref[:, 0:256] = (dq * c).astype(BF16)
        o_ref[:, 256:512] = (dk * c).astype(BF16)
        o_ref[:, 512:768] = da[:, 512:768].astype(BF16)
        o_ref[:, 768:1024] = (dq * s).astype(BF16)
        o_ref[:, 1024:1280] = (dk * s).astype(BF16)
        b = EXT_ATT
        o_ref[:, b + R_DQKV:b + R_DQKV + 1536] = dqkv_ref[...].astype(BF16)
        o_ref[:, b + R_DZ:b + R_DZ + 512] = dz_ref[...].astype(BF16)
        o_ref[:, b + R_BB:b + R_BB + 512] = dbb_ref[...].astype(BF16)
        o_ref[:, b + R_AB:b + R_AB + 512] = dab_ref[...].astype(BF16)
        o_ref[:, b + R_PU:b + R_PU + 256] = dpu_ref[...].astype(BF16)

    row = lambda w: pl.BlockSpec((t, w), lambda i: (i, 0))
    return pl.pallas_call(
        body, name=name, grid=(SEQ // t,),
        in_specs=[_slab_spec(6, t), _slab_spec(6, t), _slab_spec(6, t),
                  row(256), row(256), row(1536), row(512), row(512), row(512), row(256)],
        out_specs=row(EXT_W),
        out_shape=SDS((SEQ, EXT_W), BF16),
        compiler_params=_cparams(("arbitrary",)),
    )(*datts, cos, sin, d_dqkv, dz, dbb, dab, dpu)


def linear_bwd(x, dxo, nw, dy, w, name):
    t = 512
    nb = 768
    n = w.shape[1]
    nt = SEQ // t
    nn = n // nb

    def body(x_ref, dxo_ref, nw_ref, dy_ref, w_ref, dx_ref, dw_ref, dnw_ref, dh_scr, h_scr):
        k = pl.program_id(0)
        i = pl.program_id(1)
        rows = pl.ds(pl.multiple_of(i * t, t), t)
        nw_v = nw_ref[...]

        @pl.when(k == 0)
        def _():
            xh0, _r0 = _rms_stats(x_ref[...])
            h_scr[rows, :] = (xh0 * nw_v).astype(BF16)

        h = h_scr[rows, :]
        dyv = dy_ref[...]
        p_w = _dot_tn(h, dyv)
        dh = _dot_nt(dyv, w_ref[...])

        @pl.when(i == 0)
        def _():
            dw_ref[...] = p_w

        @pl.when(i > 0)
        def _():
            dw_ref[...] += p_w

        @pl.when(k == 0)
        def _():
            dh_scr[rows, :] = dh

        @pl.when(k > 0)
        def _():
            dh_scr[rows, :] += dh

        @pl.when(jnp.logical_and(k == 0, i == 0))
        def _():
            dnw_ref[...] = jnp.zeros_like(dnw_ref)

        @pl.when(k == nn - 1)
        def _():
            xh, r = _rms_stats(x_ref[...])
            dx, dw = _rms_bwd(xh, r, nw_v, dh_scr[rows, :])
            dx_ref[...] = dxo_ref[...] + dx
            dnw_ref[...] += dw

    last = nn - 1
    return pl.pallas_call(
        body, name=name, grid=(nn, nt),
        in_specs=[pl.BlockSpec((t, D_MODEL), lambda k, i: (i, 0)),
                  pl.BlockSpec((t, D_MODEL), lambda k, i: (i, 0)),
                  pl.BlockSpec((1, D_MODEL), lambda k, i: (0, 0)),
                  pl.BlockSpec((t, nb), lambda k, i: (i, k)),
                  pl.BlockSpec((D_MODEL, nb), lambda k, i: (0, k))],
        out_specs=[pl.BlockSpec((t, D_MODEL), lambda k, i: (jnp.where(k == last, i, 0), 0)),
                   pl.BlockSpec((D_MODEL, nb), lambda k, i: (0, k)),
                   pl.BlockSpec((1, D_MODEL), lambda k, i: (0, 0))],
        out_shape=[SDS((SEQ, D_MODEL), F32), SDS((D_MODEL, n), F32), SDS((1, D_MODEL), F32)],
        scratch_shapes=[pltpu.VMEM((SEQ, D_MODEL), F32), pltpu.VMEM((SEQ, D_MODEL), BF16)],
        compiler_params=_cparams(("arbitrary", "arbitrary")),
    )(x, dxo, nw, dy, w)


def _att_masks():
    qi = lax.broadcasted_iota(jnp.int32, (ATT_BLK, ATT_BLK), 0)
    ki = lax.broadcasted_iota(jnp.int32, (ATT_BLK, ATT_BLK), 1)
    return ki <= qi, ki >= qi


NEG = -1e30


N_ATT_BLK = SEQ // ATT_BLK


def _class_rows(i, d):
    per_class = N_ATT_BLK // d
    shift = per_class.bit_length() - 1
    r = i >> shift
    j = i & (per_class - 1)
    span = ATT_BLK * d
    start = r + span * j
    prev = jnp.where(j == 0, start, start - span)
    nxt = jnp.where(j == per_class - 1, start, start + span)

    def rows(s0):
        if d == 1:
            return pl.ds(pl.multiple_of(s0, ATT_BLK), ATT_BLK)
        return pl.ds(s0, ATT_BLK, stride=d)

    return rows(start), rows(prev), rows(nxt), j != 0, j != per_class - 1


def _slab_heads(ref, slab, rows):
    x0 = ref[pl.ds(slab, 1), rows, :][0]
    x1 = ref[pl.ds(slab + 1, 1), rows, :][0]
    return jnp.stack([x0[:, 0:ATT_E], x0[:, ATT_E:], x1[:, 0:ATT_E], x1[:, ATT_E:]], axis=0)


def _put_slab_heads(ref, slab, rows, val):
    ref[pl.ds(slab, 1), rows, :] = jnp.concatenate([val[0], val[1]], axis=1)[None]
    ref[pl.ds(slab + 1, 1), rows, :] = jnp.concatenate([val[2], val[3]], axis=1)[None]


def _resident_call(body, ins, out_slabs, d, name):
    n_in = len(ins)

    def wrapped(*refs):
        hbm_in, hbm_out = refs[:n_in], refs[n_in]
        vm_in, vm_out, sem = refs[n_in + 1:2 * n_in + 1], refs[2 * n_in + 1], refs[2 * n_in + 2]
        i = pl.program_id(0)

        @pl.when(i == 0)
        def _():
            copies = [pltpu.make_async_copy(h, v, sem.at[k]) for k, (h, v) in enumerate(zip(hbm_in, vm_in))]
            for cp in copies:
                cp.start()
            for cp in copies:
                cp.wait()

        body(i, *vm_in, vm_out)

        @pl.when(i == N_ATT_BLK - 1)
        def _():
            cp = pltpu.make_async_copy(vm_out, hbm_out, sem.at[n_in])
            cp.start()
            cp.wait()

    return pl.pallas_call(
        wrapped, name=name, grid=(N_ATT_BLK,),
        in_specs=[ANY_SPEC] * n_in, out_specs=ANY_SPEC, out_shape=SDS((out_slabs, SEQ, 128), F32),
        scratch_shapes=[pltpu.VMEM(a.shape, a.dtype) for a in ins] + [pltpu.VMEM((out_slabs, SEQ, 128), F32),
                                                                      pltpu.SemaphoreType.DMA((n_in + 1,))],
        compiler_params=_cparams(("arbitrary",)),
    )(*ins)


def _att_fwd_math(ld, has_prev):
    m_d, m_p = _att_masks()
    m_p = jnp.logical_and(m_p, has_prev)
    q = ld("att", 0, "cur").astype(BF16)
    kc = ld("att", 2, "cur").astype(BF16)
    vc = ld("att", 4, "cur").astype(BF16)
    kp = ld("att", 2, "prev").astype(BF16)
    vp = ld("att", 4, "prev").astype(BF16)
    sd = jnp.where(m_d, _bdot(q, kc, 2, 2) * 0.125, NEG)
    sp = jnp.where(m_p, _bdot(q, kp, 2, 2) * 0.125, NEG)
    m = jnp.maximum(jnp.max(sd, axis=-1, keepdims=True), jnp.max(sp, axis=-1, keepdims=True))
    pd = jnp.exp(sd - m)
    pp = jnp.exp(sp - m)
    den = jnp.sum(pd, axis=-1, keepdims=True) + jnp.sum(pp, axis=-1, keepdims=True)
    inv = 1.0 / den
    o = _bdot((pd * inv).astype(BF16), vc, 2, 1) + _bdot((pp * inv).astype(BF16), vp, 2, 1)
    return o, jnp.broadcast_to(m + jnp.log(den), (4, ATT_BLK, ATT_E))


def _att_bwd_math(ld, has_prev, has_next):
    m_d, m_band = _att_masks()
    m_p = jnp.logical_and(m_band, has_prev)
    m_n = jnp.logical_and(m_band, has_next)

    def pair(q, k, v, lse, do, dterm, mask):
        s = jnp.where(mask, _bdot(q, k, 2, 2) * 0.125, NEG)
        p = jnp.exp(s - lse)
        dp = _bdot(do, v, 2, 2)
        ds = (p * (dp + dterm) * 0.125).astype(BF16)
        return p.astype(BF16), ds

    q_c = ld("att", 0, "cur").astype(BF16)
    k_c = ld("att", 2, "cur").astype(BF16)
    v_c = ld("att", 4, "cur").astype(BF16)
    k_p = ld("att", 2, "prev").astype(BF16)
    v_p = ld("att", 4, "prev").astype(BF16)
    q_n = ld("att", 0, "next").astype(BF16)
    o_c = ld("ol", 0, "cur")
    o_n = ld("ol", 0, "next")
    lse_c = ld("ol", 2, "cur")[:, :, 0:1]
    lse_n = ld("ol", 2, "next")[:, :, 0:1]
    do_c = ld("dol", 0, "cur")
    do_n = ld("dol", 0, "next")
    t_c = ld("dol", 2, "cur")[:, :, 0:1] - jnp.sum(do_c * o_c, axis=-1, keepdims=True)
    t_n = ld("dol", 2, "next")[:, :, 0:1] - jnp.sum(do_n * o_n, axis=-1, keepdims=True)
    do_cb = do_c.astype(BF16)
    do_nb = do_n.astype(BF16)
    p1, ds1 = pair(q_c, k_c, v_c, lse_c, do_cb, t_c, m_d)
    _p2, ds2 = pair(q_c, k_p, v_p, lse_c, do_cb, t_c, m_p)
    p3, ds3 = pair(q_n, k_c, v_c, lse_n, do_nb, t_n, m_n)
    return (_bdot(ds1, k_c, 2, 1) + _bdot(ds2, k_p, 2, 1), _bdot(ds1, q_c, 1, 1) + _bdot(ds3, q_n, 1, 1),
            _bdot(p1, do_cb, 1, 1) + _bdot(p3, do_nb, 1, 1))


ALL_ROWS = pl.ds(0, ATT_BLK)


def att_fwd_s(att, d, name):
    if d == 1:
        def body1(cur_ref, prev_ref, o_ref):
            refs = {"cur": cur_ref, "prev": prev_ref}
            o, lse = _att_fwd_math(lambda _a, slab, where: _slab_heads(refs[where], slab, ALL_ROWS), pl.program_id(0) != 0)
            _put_slab_heads(o_ref, 0, ALL_ROWS, o)
            _put_slab_heads(o_ref, 2, ALL_ROWS, lse)

        return pl.pallas_call(
            body1, name=name, grid=(N_ATT_BLK,),
            in_specs=[pl.BlockSpec((6, ATT_BLK, 128), lambda i: (0, i, 0)),
                      pl.BlockSpec((6, ATT_BLK, 128), lambda i: (0, jnp.maximum(i - 1, 0), 0))],
            out_specs=pl.BlockSpec((4, ATT_BLK, 128), lambda i: (0, i, 0)),
            out_shape=SDS((4, SEQ, 128), F32), compiler_params=_cparams(("arbitrary",)),
        )(att, att)

    def body(i, att_ref, o_ref):
        cur, prev, _nxt, has_prev, _has_next = _class_rows(i, d)
        rows = {"cur": cur, "prev": prev}
        o, lse = _att_fwd_math(lambda _a, slab, where: _slab_heads(att_ref, slab, rows[where]), has_prev)
        _put_slab_heads(o_ref, 0, cur, o)
        _put_slab_heads(o_ref, 2, cur, lse)

    return _resident_call(body, [att], 4, d, name)


def att_bwd_s(att, ol, dol, d, name):
    if d == 1:
        def body1(a_p, a_c, a_n, ol_c, ol_n, dol_c, dol_n, d_ref):
            i = pl.program_id(0)
            refs = {("att", "prev"): a_p, ("att", "cur"): a_c, ("att", "next"): a_n, ("ol", "cur"): ol_c,
                    ("ol", "next"): ol_n, ("dol", "cur"): dol_c, ("dol", "next"): dol_n}
            dq, dk, dv = _att_bwd_math(lambda a, slab, where: _slab_heads(refs[(a, where)], slab, ALL_ROWS),
                                       i != 0, i != N_ATT_BLK - 1)
            _put_slab_heads(d_ref, 0, ALL_ROWS, dq)
            _put_slab_heads(d_ref, 2, ALL_ROWS, dk)
            _put_slab_heads(d_ref, 4, ALL_ROWS, dv)

        def blk(k, f):
            return pl.BlockSpec((k, ATT_BLK, 128), lambda i: (0, f(i), 0))

        prv = lambda i: jnp.maximum(i - 1, 0)
        cur = lambda i: i
        nxt = lambda i: jnp.minimum(i + 1, N_ATT_BLK - 1)
        return pl.pallas_call(
            body1, name=name, grid=(N_ATT_BLK,),
            in_specs=[blk(6, prv), blk(6, cur), blk(6, nxt), blk(4, cur), blk(4, nxt), blk(4, cur), blk(4, nxt)],
            out_specs=blk(6, cur), out_shape=SDS((6, SEQ, 128), F32), compiler_params=_cparams(("arbitrary",)),
        )(att, att, att, ol, ol, dol, dol)

    def body(i, att_ref, ol_ref, dol_ref, d_ref):
        cur, prev, nxt, has_prev, has_next = _class_rows(i, d)
        rows = {"cur": cur, "prev": prev, "next": nxt}
        refs = {"att": att_ref, "ol": ol_ref, "dol": dol_ref}
        dq, dk, dv = _att_bwd_math(lambda a, slab, where: _slab_heads(refs[a], slab, rows[where]), has_prev, has_next)
        _put_slab_heads(d_ref, 0, cur, dq)
        _put_slab_heads(d_ref, 2, cur, dk)
        _put_slab_heads(d_ref, 4, cur, dv)

    return _resident_call(body, [att, ol, dol], 6, d, name)


def _shift_down(x, k):
    rows = lax.broadcasted_iota(jnp.int32, x.shape, 0)
    return jnp.where(rows >= k, pltpu.roll(x, k, 0), 0.0)


def _shift_up(x, k):
    n = x.shape[0]
    rows = lax.broadcasted_iota(jnp.int32, x.shape, 0)
    return jnp.where(rows < n - k, pltpu.roll(x, n - k, 0), 0.0)


@functools.partial(jax.custom_vjp, nondiff_argnums=(1,))
def _delay(x, k):
    return _shift_down(x, k)


def _delay_fwd(x, k):
    return _shift_down(x, k), None


def _delay_bwd(k, _res, g):
    return (_shift_up(g, k),)


_delay.defvjp(_delay_fwd, _delay_bwd)

DN_CONV = 4


def _dn_prep_fn(u, w, kind):
    y = w[DN_CONV - 1:DN_CONV] * u
    for j in range(DN_CONV - 1):
        y = y + w[j:j + 1] * _delay(u, DN_CONV - 1 - j)
    y = y * _sigmoid(y)
    nrm = y * lax.rsqrt(jnp.sum(y * y, axis=-1, keepdims=True) + EPS)
    return jnp.where(kind == 0, nrm * (DN_E ** -0.5), jnp.where(kind == 1, nrm, y))


def dn_prep_fwd(rest, conv_w, name):
    def body(u_ref, w_ref, o_ref):
        j = pl.program_id(0)
        kind = (j >= DN_H).astype(jnp.int32) + (j >= 2 * DN_H).astype(jnp.int32)
        o_ref[...] = _dn_prep_fn(u_ref[...], w_ref[...], kind)

    return pl.pallas_call(
        body, name=name, grid=(3 * DN_H,),
        in_specs=[pl.BlockSpec((SEQ, DN_E), lambda j: (0, j)),
                  pl.BlockSpec((DN_CONV, DN_E), lambda j: (0, j))],
        out_specs=pl.BlockSpec((SEQ, DN_E), lambda j: (0, j)),
        out_shape=SDS((SEQ, 3 * DN_W), F32),
        compiler_params=_cparams(("arbitrary",)),
    )(rest, conv_w)


def dn_prep_bwd(rest, conv_w, dqkv, name):
    def body(u_ref, w_ref, g_ref, du_ref, dw_ref):
        j = pl.program_id(0)
        kind = (j >= DN_H).astype(jnp.int32) + (j >= 2 * DN_H).astype(jnp.int32)
        _y, vjp = jax.vjp(lambda u, w: _dn_prep_fn(u, w, kind), u_ref[...], w_ref[...])
        du, dw = vjp(g_ref[...])
        du_ref[...] = du
        dw_ref[...] = dw

    return pl.pallas_call(
        body, name=name, grid=(3 * DN_H,),
        in_specs=[pl.BlockSpec((SEQ, DN_E), lambda j: (0, j)),
                  pl.BlockSpec((DN_CONV, DN_E), lambda j: (0, j)),
                  pl.BlockSpec((SEQ, DN_E), lambda j: (0, j))],
        out_specs=[pl.BlockSpec((SEQ, DN_E), lambda j: (0, j)),
                   pl.BlockSpec((DN_CONV, DN_E), lambda j: (0, j))],
        out_shape=[SDS((SEQ, 3 * DN_W), F32), SDS((DN_CONV, 3 * DN_W), F32)],
        compiler_params=_cparams(("arbitrary",)),
    )(rest, conv_w, dqkv)


def _bdot(a, b, ca, cb, prec=None):
    return lax.dot_general(a, b, (((ca,), (cb,)), ((0,), (0,))), preferred_element_type=F32, precision=prec)


def _unit_lower_inverse(a):
    eye = (lax.broadcasted_iota(jnp.int32, (DN_C, DN_C), 0) == lax.broadcasted_iota(jnp.int32, (DN_C, DN_C), 1)).astype(F32)
    p = eye - a
    b = _bdot(a, a, 2, 1, INV_PREC)
    for lvl in range(5):
        p = p + _bdot(p, b, 2, 1, INV_PREC)
        if lvl < 4:
            b = _bdot(b, b, 2, 1, INV_PREC)
    return p


@jax.custom_vjp
def _tri_inv(a):
    return _unit_lower_inverse(a)


def _tri_inv_fwd(a):
    t = _unit_lower_inverse(a)
    return t, t


def _tri_inv_bwd(t, g):
    return (-_bdot(_bdot(t, g, 1, 1, INV_PREC), t, 2, 2, INV_PREC),)


_tri_inv.defvjp(_tri_inv_fwd, _tri_inv_bwd)


def _b16(x):
    return x.astype(BF16)


def _heads(ref, base=0):
    return jnp.stack([ref[:, base + DN_E * hd:base + DN_E * hd + DN_E] for hd in range(DN_H)], axis=0)


def _put_heads(ref, val, base=0):
    for hd in range(DN_H):
        ref[:, base + DN_E * hd:base + DN_E * hd + DN_E] = val[hd]


DN_G_LOG2 = 3
DN_G = 1 << DN_G_LOG2
N_INST = DN_G * DN_H


def _dn_intra(q, k, v, bb, ab, alog, dtb):
    ri = lax.broadcasted_iota(jnp.int32, (DN_C, DN_C), 0)
    ci = lax.broadcasted_iota(jnp.int32, (DN_C, DN_C), 1)
    lower = ri >= ci
    strict = ri > ci
    nh = q.shape[0]
    beta = _sigmoid(bb)
    xg = ab + dtb
    softplus = jnp.maximum(xg, 0.0) + jnp.log(1.0 + jnp.exp(-jnp.abs(xg)))
    gi = -jnp.exp(alog) * softplus
    g = _bdot(jnp.broadcast_to(lower.astype(F32), (nh, DN_C, DN_C)), gi, 2, 1, HI)
    eg = jnp.exp(g)
    kb = k * beta
    vb = v * beta
    g_col = g[:, :, 0:DN_C]
    g_row = _bdot(jnp.full((nh, DN_C, DN_E), 1.0 / DN_E, F32), g, 2, 2, HI)
    decay = jnp.where(lower, jnp.exp(jnp.where(lower, g_col - g_row, 0.0)), 0.0)
    kbf = _b16(k)
    a = jnp.where(strict, _bdot(_b16(kb), kbf, 2, 2) * decay, 0.0)
    tb = _b16(_tri_inv(a))
    u = _bdot(tb, _b16(vb), 2, 1)
    w = _bdot(tb, _b16(kb * eg), 2, 1)
    intra = jnp.where(lower, _bdot(_b16(q), kbf, 2, 2) * decay, 0.0)
    g_last = g[:, DN_C - 1:DN_C, :]
    return u, w, q * eg, k * jnp.exp(g_last - g), intra, jnp.exp(g_last)


def _dn_inter(u, w, qg, kdec, intra, egl, state):
    sb = _b16(state)
    v_new = u - _bdot(_b16(w), sb, 2, 1)
    o = _bdot(_b16(qg), sb, 2, 1) + _bdot(_b16(intra), _b16(v_new), 2, 1)
    return o, state * egl + _bdot(_b16(kdec), _b16(v_new), 1, 1)


def _inst(ref, base=0):
    per_head = [ref[:, base + DN_E * hd:base + DN_E * hd + DN_E].reshape(DN_G, DN_C, DN_E) for hd in range(DN_H)]
    return jnp.concatenate(per_head, axis=0)


def _inst_rows(ref):
    rows = [jnp.broadcast_to(ref[:, DN_E * hd:DN_E * hd + DN_E][None], (DN_G, 1, DN_E)) for hd in range(DN_H)]
    return jnp.concatenate(rows, axis=0)


def _put_inst(ref, val, width=DN_E, base=0):
    for hd in range(DN_H):
        ref[:, base + width * hd:base + width * hd + width] = val[DN_G * hd:DN_G * hd + DN_G].reshape(DN_G * DN_C, width)


def _intra_args(qkv_ref, bb_ref, ab_ref, alog_ref, dtb_ref):
    return (_inst(qkv_ref), _inst(qkv_ref, DN_W), _inst(qkv_ref, 2 * DN_W), _inst(bb_ref), _inst(ab_ref),
            _inst_rows(alog_ref), _inst_rows(dtb_ref))


def _intra_in_specs():
    t = DN_G * DN_C
    return [pl.BlockSpec((t, 3 * DN_W), lambda n: (n, 0)),
            pl.BlockSpec((t, DN_W), lambda n: (n, R_BB // DN_W)),
            pl.BlockSpec((t, DN_W), lambda n: (n, R_AB // DN_W)),
            pl.BlockSpec((1, DN_W), lambda n: (0, 0)),
            pl.BlockSpec((1, DN_W), lambda n: (0, 0))]


def dn_intra_fwd(qkv, rest, alog_b, dtb_b, name, exch=None):
    t = DN_G * DN_C

    def body(qkv_ref, bb_ref, ab_ref, alog_ref, dtb_ref, u_ref, w_ref, qg_ref, kd_ref, in_ref, egl_ref):
        u, w, qg, kdec, intra, egl = _dn_intra(*_intra_args(qkv_ref, bb_ref, ab_ref, alog_ref, dtb_ref))
        _put_inst(u_ref, u)
        _put_inst(w_ref, w.astype(BF16))
        _put_inst(qg_ref, qg.astype(BF16))
        _put_inst(kd_ref, kdec.astype(BF16))
        _put_inst(in_ref, intra.astype(BF16), DN_C)
        for hd in range(DN_H):
            egl_ref[:, DN_E * hd:DN_E * hd + DN_E] = egl[DN_G * hd:DN_G * hd + DN_G].reshape(DN_G, DN_E)

    row = lambda w_: pl.BlockSpec((t, w_), lambda n: (n, 0))
    return _call(
        body, name=name, grid=(N_CHUNK // DN_G,), in_specs=_intra_in_specs(),
        out_specs=[row(DN_W), row(DN_W), row(DN_W), row(DN_W), row(DN_H * DN_C),
                   pl.BlockSpec((DN_G, DN_W), lambda n: (n, 0))],
        out_shape=[SDS((SEQ, DN_W), F32), SDS((SEQ, DN_W), BF16), SDS((SEQ, DN_W), BF16), SDS((SEQ, DN_W), BF16),
                   SDS((SEQ, DN_H * DN_C), BF16), SDS((N_CHUNK, DN_W), F32)],
        scratch_shapes=[], sem=("arbitrary",), args=(qkv, rest, rest, alog_b, dtb_b), exch=exch)


def dn_intra_bwd(qkv, rest, alog_b, dtb_b, du, dw, dqg, dkd, dintra, degl, name):
    t = DN_G * DN_C

    def body(qkv_ref, bb_ref, ab_ref, alog_ref, dtb_ref, du_ref, dw_ref, dqg_ref, dkd_ref, din_ref, degl_ref,
             dqkv_ref, dbb_ref, dab_ref, dalog_ref, ddtb_ref):
        @pl.when(pl.program_id(0) == 0)
        def _():
            dalog_ref[...] = jnp.zeros_like(dalog_ref)
            ddtb_ref[...] = jnp.zeros_like(ddtb_ref)

        _out, vjp = jax.vjp(_dn_intra, *_intra_args(qkv_ref, bb_ref, ab_ref, alog_ref, dtb_ref))
        d_in = jnp.concatenate([din_ref[:, DN_C * hd:DN_C * hd + DN_C].reshape(DN_G, DN_C, DN_C) for hd in range(DN_H)], axis=0)
        d_egl = jnp.concatenate([degl_ref[:, DN_E * hd:DN_E * hd + DN_E].reshape(DN_G, 1, DN_E) for hd in range(DN_H)], axis=0)
        dq, dk, dv, dbb, dab, dalog, ddtb = vjp((_inst(du_ref), _inst(dw_ref), _inst(dqg_ref), _inst(dkd_ref), d_in, d_egl))
        _put_inst(dqkv_ref, dq)
        _put_inst(dqkv_ref, dk, DN_E, DN_W)
        _put_inst(dqkv_ref, dv, DN_E, 2 * DN_W)
        _put_inst(dbb_ref, dbb)
        _put_inst(dab_ref, dab)
        for hd in range(DN_H):
            sl = slice(DN_E * hd, DN_E * hd + DN_E)
            dalog_ref[:, sl] += jnp.sum(dalog[DN_G * hd:DN_G * hd + DN_G], axis=0)
            ddtb_ref[:, sl] += jnp.sum(ddtb[DN_G * hd:DN_G * hd + DN_G], axis=0)

    row = lambda w_: pl.BlockSpec((t, w_), lambda n: (n, 0))
    acc = pl.BlockSpec((1, DN_W), lambda n: (0, 0))
    return pl.pallas_call(
        body, name=name, grid=(N_CHUNK // DN_G,),
        in_specs=_intra_in_specs() + [row(DN_W), row(DN_W), row(DN_W), row(DN_W), row(DN_H * DN_C),
                                      pl.BlockSpec((DN_G, DN_W), lambda n: (n, 0))],
        out_specs=[row(3 * DN_W), row(DN_W), row(DN_W), acc, acc],
        out_shape=[SDS((SEQ, 3 * DN_W), F32), SDS((SEQ, DN_W), F32), SDS((SEQ, DN_W), F32),
                   SDS((1, DN_W), F32), SDS((1, DN_W), F32)],
        compiler_params=_cparams(("arbitrary",)),
    )(qkv, rest, rest, alog_b, dtb_b, du, dw, dqg, dkd, dintra, degl)


def _inter_args(u_ref, w_ref, qg_ref, kd_ref, in_ref, egl_ref, n, state):
    f = lambda r: _heads(r).astype(F32)
    intra = jnp.stack([in_ref[:, DN_C * hd:DN_C * hd + DN_C] for hd in range(DN_H)], axis=0).astype(F32)
    egl = _heads(egl_ref.at[pl.ds(n & (DN_G - 1), 1), :])
    return f(u_ref), f(w_ref), f(qg_ref), f(kd_ref), intra, egl, state


def dn_inter_fwd(u, w, qg, kdec, intra, egl, name, exch=None):
    def body(u_ref, w_ref, qg_ref, kd_ref, in_ref, egl_ref, o_ref, st_ref, state_scr):
        n = pl.program_id(0)

        @pl.when(n == 0)
        def _():
            state_scr[...] = jnp.zeros_like(state_scr)

        st = state_scr[...]
        st_ref[0] = st
        o, ns = _dn_inter(*_inter_args(u_ref, w_ref, qg_ref, kd_ref, in_ref, egl_ref, n, st))
        _put_heads(o_ref, o)
        state_scr[...] = ns

    row = lambda w_: pl.BlockSpec((DN_C, w_), lambda n: (n, 0))
    return _call(
        body, name=name, grid=(N_CHUNK,),
        in_specs=[row(DN_W), row(DN_W), row(DN_W), row(DN_W), row(DN_H * DN_C),
                  pl.BlockSpec((DN_G, DN_W), lambda n: (n >> DN_G_LOG2, 0))],
        out_specs=[row(DN_W), pl.BlockSpec((1, DN_H, DN_E, DN_E), lambda n: (n, 0, 0, 0))],
        out_shape=[SDS((SEQ, DN_W), F32), SDS((N_CHUNK, DN_H, DN_E, DN_E), F32)],
        scratch_shapes=[pltpu.VMEM((DN_H, DN_E, DN_E), F32)],
        sem=("arbitrary",), args=(u, w, qg, kdec, intra, egl), exch=exch)


def dn_inter_bwd(u, w, qg, kdec, intra, egl, states, do, name):
    last = N_CHUNK - 1

    def body(u_ref, w_ref, qg_ref, kd_ref, in_ref, egl_ref, st_ref, do_ref,
             du_ref, dw_ref, dqg_ref, dkd_ref, din_ref, degl_ref, dstate_scr):
        s = pl.program_id(0)
        n = last - s

        @pl.when(s == 0)
        def _():
            dstate_scr[...] = jnp.zeros_like(dstate_scr)

        _out, vjp = jax.vjp(_dn_inter, *_inter_args(u_ref, w_ref, qg_ref, kd_ref, in_ref, egl_ref, n, st_ref[0]))
        du, dw, dqg, dkd, din, degl, dst = vjp((_heads(do_ref), dstate_scr[...]))
        _put_heads(du_ref, du)
        _put_heads(dw_ref, dw)
        _put_heads(dqg_ref, dqg)
        _put_heads(dkd_ref, dkd)
        for hd in range(DN_H):
            din_ref[:, DN_C * hd:DN_C * hd + DN_C] = din[hd]
        row = n & (DN_G - 1)

        @pl.when(row == DN_G - 1)
        def _():
            degl_ref[...] = jnp.zeros_like(degl_ref)

        new_row = jnp.concatenate([degl[hd] for hd in range(DN_H)], axis=1)
        rows = lax.broadcasted_iota(jnp.int32, (DN_G, DN_W), 0)
        degl_ref[...] = jnp.where(rows == row, jnp.broadcast_to(new_row, (DN_G, DN_W)), degl_ref[...])
        dstate_scr[...] = dst

    rev = lambda w_: pl.BlockSpec((DN_C, w_), lambda s: (last - s, 0))
    grp = pl.BlockSpec((DN_G, DN_W), lambda s: ((last - s) >> DN_G_LOG2, 0))
    return pl.pallas_call(
        body, name=name, grid=(N_CHUNK,),
        in_specs=[rev(DN_W), rev(DN_W), rev(DN_W), rev(DN_W), rev(DN_H * DN_C), grp,
                  pl.BlockSpec((1, DN_H, DN_E, DN_E), lambda s: (last - s, 0, 0, 0)), rev(DN_W)],
        out_specs=[rev(DN_W), rev(DN_W), rev(DN_W), rev(DN_W), rev(DN_H * DN_C), grp],
        out_shape=[SDS((SEQ, DN_W), F32)] * 4 + [SDS((SEQ, DN_H * DN_C), F32), SDS((N_CHUNK, DN_W), F32)],
        scratch_shapes=[pltpu.VMEM((DN_H, DN_E, DN_E), F32)],
        compiler_params=_cparams(("arbitrary",)),
    )(u, w, qg, kdec, intra, egl, states, do)


OUT_T = 256


def _pool_consts(rows_total, t0, halo_before):
    lane = lax.broadcasted_iota(jnp.int32, (rows_total, POOL_W), 1)
    row = lax.broadcasted_iota(jnp.int32, (rows_total, POOL_W), 0)
    grp = (lane >= 64).astype(jnp.int32) + (lane >= 128).astype(jnp.int32) + (lane >= 192).astype(jnp.int32)
    win = jnp.where(grp == 0, 2, jnp.where(grp == 1, 4, jnp.where(grp == 2, 8, 16)))
    pos = t0 + row - halo_before
    cnt = jnp.minimum(pos + 1, win).astype(F32)
    return grp, cnt


def _pool_select(grp, s2, s4, s8, s16):
    return jnp.where(grp == 0, s2, jnp.where(grp == 1, s4, jnp.where(grp == 2, s8, s16)))


def _pooled(u_ext, t0):
    n = u_ext.shape[0]
    grp, cnt = _pool_consts(n, t0, POOL_HALO)
    s2 = u_ext + pltpu.roll(u_ext, 1, 0)
    s4 = s2 + pltpu.roll(s2, 2, 0)
    s8 = s4 + pltpu.roll(s4, 4, 0)
    s16 = s8 + pltpu.roll(s8, 8, 0)
    out = _pool_select(grp, s2, s4, s8, s16) / jnp.maximum(cnt, 1.0) - u_ext
    return out[POOL_HALO:, :]


def _merge_weights(l1, l4, l16):
    m = jnp.maximum(jnp.maximum(l1, l4), l16)
    e1 = jnp.exp(l1 - m)
    e4 = jnp.exp(l4 - m)
    e16 = jnp.exp(l16 - m)
    inv = 1.0 / (e1 + e4 + e16)
    return e1 * inv, e4 * inv, e16 * inv


def _out_parts(ol1_ref, ol4_ref, ol16_ref, pu_ref, puh_ref, odn_ref, z_ref, wbd_ref, i, t):
    w1, w4, w16 = _merge_weights(_slabs_load(ol1_ref, 2, 2), _slabs_load(ol4_ref, 2, 2), _slabs_load(ol16_ref, 2, 2))
    ya = w1 * _slabs_load(ol1_ref, 0, 2) + w4 * _slabs_load(ol4_ref, 0, 2) + w16 * _slabs_load(ol16_ref, 0, 2)
    halo = jnp.where(i > 0, puh_ref[...], 0.0)
    pooled = _pooled(jnp.concatenate([halo, pu_ref[...]], axis=0), i * t)
    pw = _dot(pooled.astype(BF16), wbd_ref[...])
    return ya, pooled, pw, (w1, w4, w16)


def _out_specs_common(t):
    def row(w, cb=0):
        return pl.BlockSpec((t, w), lambda i: (i, cb))

    halo = pl.BlockSpec((POOL_HALO, POOL_W),
                        lambda i: (jnp.maximum(i * (t // POOL_HALO) - 1, 0), R_PU // POOL_W))
    full = lambda a, b: pl.BlockSpec((a, b), lambda i: (0, 0))
    return [_slab_spec(4, t), _slab_spec(4, t), _slab_spec(4, t), row(POOL_W, R_PU // POOL_W), halo, row(DN_W), row(DN_W, R_DZ // DN_W),
            full(POOL_W, POOL_W), full(1, POOL_W), full(1, DN_W), full(D_MODEL, D_MODEL)]


def mix_out_fwd(x, ol1, ol4, ol16, rest, odn, wbd, scale, onorm_b, wout, name):
    t = OUT_T

    def body(x_ref, ol1_ref, ol4_ref, ol16_ref, pu_ref, puh_ref, odn_ref, z_ref, wbd_ref, sc_ref, on_ref, wo_ref, o_ref):
        i = pl.program_id(0)
        ya, _pooled_v, pw, _w = _out_parts(ol1_ref, ol4_ref, ol16_ref, pu_ref, puh_ref, odn_ref, z_ref, wbd_ref, i, t)
        yb = pw * sc_ref[...]
        acc = x_ref[...] + _dot(ya.astype(BF16), wo_ref[0:256, :]) + _dot(yb.astype(BF16), wo_ref[256:512, :])
        for hd in range(DN_H):
            sl = slice(DN_E * hd, DN_E * hd + DN_E)
            oh, _r = _rms_stats(odn_ref[:, sl])
            z = z_ref[:, sl]
            yc = oh * on_ref[:, sl] * (z * _sigmoid(z))
            acc = acc + _dot(yc.astype(BF16), wo_ref[512 + DN_E * hd:512 + DN_E * hd + DN_E, :])
        o_ref[...] = acc

    return pl.pallas_call(
        body, name=name, grid=(SEQ // t,),
        in_specs=[pl.BlockSpec((t, D_MODEL), lambda i: (i, 0))] + _out_specs_common(t),
        out_specs=pl.BlockSpec((t, D_MODEL), lambda i: (i, 0)),
        out_shape=SDS((SEQ, D_MODEL), F32),
        compiler_params=_cparams(("arbitrary",)),
    )(x, ol1, ol4, ol16, rest, rest, odn, rest, wbd, scale, onorm_b, wout)


def mix_out_bwd(dxo, ol1, ol4, ol16, rest, odn, wbd, scale, onorm_b, wout, headsum, name):
    t = OUT_T

    def body(dxo_ref, ol1_ref, ol4_ref, ol16_ref, pu_ref, puh_ref, odn_ref, z_ref, wbd_ref, sc_ref, on_ref, wo_ref, hs_ref,
             dwo_ref, d1_ref, d4_ref, d16_ref, dpl_ref, dodn_ref, dz_ref, dsc_ref, don_ref, dwbd_ref):
        i = pl.program_id(0)

        @pl.when(i == 0)
        def _():
            dwo_ref[...] = jnp.zeros_like(dwo_ref)
            dsc_ref[...] = jnp.zeros_like(dsc_ref)
            don_ref[...] = jnp.zeros_like(don_ref)
            dwbd_ref[...] = jnp.zeros_like(dwbd_ref)

        ya, pooled, pw, (w1, w4, w16) = _out_parts(ol1_ref, ol4_ref, ol16_ref, pu_ref, puh_ref, odn_ref, z_ref, wbd_ref, i, t)
        sc = sc_ref[...]
        dxb = dxo_ref[...].astype(BF16)
        dwo_ref[0:256, :] += _dot_tn(ya.astype(BF16), dxb)
        dwo_ref[256:512, :] += _dot_tn((pw * sc).astype(BF16), dxb)
        dya = _dot_nt(dxb, wo_ref[0:256, :])
        o1 = _slabs_load(ol1_ref, 0, 2)
        o4 = _slabs_load(ol4_ref, 0, 2)
        o16 = _slabs_load(ol16_ref, 0, 2)
        hs = hs_ref[...]
        s1 = _dot(dya * o1, hs, HI)
        s4 = _dot(dya * o4, hs, HI)
        s16 = _dot(dya * o16, hs, HI)
        sbar = w1 * s1 + w4 * s4 + w16 * s16
        _slabs_store(d1_ref, 0, w1 * dya)
        _slabs_store(d1_ref, 2, w1 * (s1 - sbar))
        _slabs_store(d4_ref, 0, w4 * dya)
        _slabs_store(d4_ref, 2, w4 * (s4 - sbar))
        _slabs_store(d16_ref, 0, w16 * dya)
        _slabs_store(d16_ref, 2, w16 * (s16 - sbar))
        dyb = _dot_nt(dxb, wo_ref[256:512, :])
        dsc_ref[...] += jnp.sum(dyb * pw, axis=0, keepdims=True)
        dpw = (dyb * sc).astype(BF16)
        dwbd_ref[...] += _dot_tn(pooled.astype(BF16), dpw)
        dpl_ref[...] = _dot_nt(dpw, wbd_ref[...])
        for hd in range(DN_H):
            sl = slice(DN_E * hd, DN_E * hd + DN_E)
            rows_w = slice(512 + DN_E * hd, 512 + DN_E * hd + DN_E)
            oh, r = _rms_stats(odn_ref[:, sl])
            z = z_ref[:, sl]
            sg = _sigmoid(z)
            sz = z * sg
            nw = on_ref[:, sl]
            on = oh * nw
            dwo_ref[rows_w, :] += _dot_tn((on * sz).astype(BF16), dxb)
            dyc = _dot_nt(dxb, wo_ref[rows_w, :])
            dz_ref[:, sl] = dyc * on * (sg * (1.0 + z * (1.0 - sg)))
            dx, dw = _rms_bwd(oh, r, nw, dyc * sz)
            dodn_ref[:, sl] = dx
            don_ref[:, sl] += dw

    row = lambda w: pl.BlockSpec((t, w), lambda i: (i, 0))
    full = lambda a, b: pl.BlockSpec((a, b), lambda i: (0, 0))
    return pl.pallas_call(
        body, name=name, grid=(SEQ // t,),
        in_specs=[row(D_MODEL)] + _out_specs_common(t) + [full(ATT_W, ATT_W)],
        out_specs=[full(D_MODEL, D_MODEL), _slab_spec(4, t), _slab_spec(4, t), _slab_spec(4, t), row(POOL_W), row(DN_W), row(DN_W),
                   full(1, POOL_W), full(1, DN_W), full(POOL_W, POOL_W)],
        out_shape=[SDS((D_MODEL, D_MODEL), F32), SDS((4, SEQ, 128), F32), SDS((4, SEQ, 128), F32), SDS((4, SEQ, 128), F32),
                   SDS((SEQ, POOL_W), F32), SDS((SEQ, DN_W), F32), SDS((SEQ, DN_W), F32),
                   SDS((1, POOL_W), F32), SDS((1, DN_W), F32), SDS((POOL_W, POOL_W), F32)],
        compiler_params=_cparams(("arbitrary",)),
    )(dxo, ol1, ol4, ol16, rest, rest, odn, rest, wbd, scale, onorm_b, wout, headsum)


def pool_bwd(dpooled, name):
    t = 512
    nt = SEQ // t

    def body(d_ref, dn_ref, o_ref):
        i = pl.program_id(0)
        halo = jnp.where(i < nt - 1, dn_ref[...], 0.0)
        d_ext = jnp.concatenate([d_ref[...], halo], axis=0)
        n = t + POOL_HALO
        grp, cnt = _pool_consts(n, i * t, 0)
        dq = d_ext / cnt
        s2 = dq + pltpu.roll(dq, n - 1, 0)
        s4 = s2 + pltpu.roll(s2, n - 2, 0)
        s8 = s4 + pltpu.roll(s4, n - 4, 0)
        s16 = s8 + pltpu.roll(s8, n - 8, 0)
        o_ref[...] = (_pool_select(grp, s2, s4, s8, s16) - d_ext)[0:t, :]

    return pl.pallas_call(
        body, name=name, grid=(nt,),
        in_specs=[pl.BlockSpec((t, POOL_W), lambda i: (i, 0)),
                  pl.BlockSpec((POOL_HALO, POOL_W),
                               lambda i: (jnp.minimum((i + 1) * (t // POOL_HALO), SEQ // POOL_HALO - 1), 0))],
        out_specs=pl.BlockSpec((t, POOL_W), lambda i: (i, 0)),
        out_shape=SDS((SEQ, POOL_W), F32),
        compiler_params=_cparams(("arbitrary",)),
    )(dpooled, dpooled)


N_PEER = N_DEV - 1
ANY_SPEC = pl.BlockSpec(memory_space=pl.ANY)


class Exchange:
    def __init__(self, arrays, mode):
        self.arrays = list(arrays)
        self.mode = mode
        n = len(self.arrays)
        if mode == "scatter":
            self.out_shape = [SDS(a.shape, a.dtype) for a in self.arrays]
        else:
            self.out_shape = [SDS((N_DEV,) + a.shape, a.dtype) for a in self.arrays]
        self.scratch = [pltpu.SemaphoreType.DMA((n * N_PEER,)), pltpu.SemaphoreType.DMA((n * N_PEER,)),
                        pltpu.SemaphoreType.DMA((n,))]

    @staticmethod
    def _place():
        x, y, c = lax.axis_index("x"), lax.axis_index("y"), lax.axis_index("c")
        chips = [(1 - x, y), (x, 1 - y), (1 - x, 1 - y)]
        return x, y, c, chips

    @staticmethod
    def _copy(sems, a, k, src, dst, to):
        send_sems, recv_sems, _ = sems
        return pltpu.make_async_remote_copy(
            src_ref=src, dst_ref=dst, send_sem=send_sems.at[a * N_PEER + k], recv_sem=recv_sems.at[a * N_PEER + k],
            device_id=to, device_id_type=MESH)

    def _scatter_peers(self):
        x, y, c, _ = self._place()
        out = []
        for fx, fy, fc in ((0, 0, 1), (1, 0, 0), (0, 1, 0), (1, 1, 0), (1, 0, 1), (0, 1, 1), (1, 1, 1)):
            px, py, pc = x ^ fx, y ^ fy, c ^ fc
            out.append(((px, py, pc), 4 * px + 2 * py + pc))
        return 4 * x + 2 * y + c, out

    def _local(self, ins, outs, sems, a, me):
        src = ins[a].at[me] if self.mode == "scatter" else ins[a]
        return pltpu.make_async_copy(src, outs[a].at[me], sems[2].at[a])

    def start(self, ins, outs, sems):
        if self.mode == "scatter":
            me, peers = self._scatter_peers()
            for a in range(len(ins)):
                self._local(ins, outs, sems, a, me).start()
                for k, (peer, pidx) in enumerate(peers):
                    self._copy(sems, a, k, ins[a].at[pidx], outs[a].at[me], peer).start()
            return
        x, y, c, chips = self._place()
        me = 4 * x + 2 * y + c
        for a in range(len(ins)):
            self._local(ins, outs, sems, a, me).start()
            self._copy(sems, a, 0, ins[a], outs[a].at[me], (x, y, 1 - c)).start()
            for j, (cx, cy) in enumerate(chips):
                self._copy(sems, a, 1 + j, ins[a], outs[a].at[me], (cx, cy, c)).start()

    def finish(self, ins, outs, sems):
        n = len(ins)
        if self.mode == "scatter":
            me, peers = self._scatter_peers()
            for a in range(n):
                for k, (peer, pidx) in enumerate(peers):
                    self._copy(sems, a, k, ins[a].at[pidx], outs[a].at[pidx], peer).wait_recv()
            for a in range(n):
                for k, (peer, pidx) in enumerate(peers):
                    self._copy(sems, a, k, ins[a].at[pidx], outs[a].at[me], peer).wait_send()
                self._local(ins, outs, sems, a, me).wait()
            return
        x, y, c, chips = self._place()
        me = 4 * x + 2 * y + c
        sib = (x, y, 1 - c)
        for a in range(n):
            for j, (cx, cy) in enumerate(chips):
                blk = outs[a].at[4 * cx + 2 * cy + c]
                self._copy(sems, a, 1 + j, ins[a], blk, (cx, cy, c)).wait_recv()
                self._copy(sems, a, 4 + j, blk, blk, sib).start()
        for a in range(n):
            self._copy(sems, a, 0, ins[a], outs[a].at[4 * x + 2 * y + (1 - c)], sib).wait_recv()
            for j, (cx, cy) in enumerate(chips):
                blk = outs[a].at[4 * cx + 2 * cy + (1 - c)]
                self._copy(sems, a, 4 + j, blk, blk, sib).wait_recv()
        for a in range(n):
            for k in range(N_PEER):
                self._copy(sems, a, k, ins[a], outs[a].at[me], sib).wait_send()
            self._local(ins, outs, sems, a, me).wait()


def run_exchange(exch, name):
    n = len(exch.arrays)

    def body(*refs):
        ins, outs, sems = refs[:n], refs[n:2 * n], refs[2 * n:]
        exch.start(ins, outs, sems)
        exch.finish(ins, outs, sems)

    return pl.pallas_call(
        body, name=name, in_specs=[ANY_SPEC] * n, out_specs=[ANY_SPEC] * n, out_shape=exch.out_shape,
        scratch_shapes=exch.scratch,
    )(*exch.arrays)


def _call(body, *, name, grid, in_specs, out_specs, out_shape, scratch_shapes, sem, args, exch=None):
    if exch is None:
        res = pl.pallas_call(body, name=name, grid=grid, in_specs=in_specs, out_specs=out_specs, out_shape=out_shape,
                             scratch_shapes=scratch_shapes, compiler_params=_cparams(sem))(*args)
        return res, None
    single = not isinstance(out_shape, (list, tuple))
    out_specs_l = [out_specs] if single else list(out_specs)
    out_shape_l = [out_shape] if single else list(out_shape)
    n_in, n_out, n_scr, m = len(in_specs), len(out_specs_l), len(scratch_shapes), len(exch.arrays)

    def wrapped(*refs):
        p = 0
        ins = refs[p:p + n_in]; p += n_in
        xin = refs[p:p + m]; p += m
        outs = refs[p:p + n_out]; p += n_out
        xout = refs[p:p + m]; p += m
        scr = refs[p:p + n_scr]; p += n_scr
        sems = refs[p:]
        ids = [pl.program_id(ax) for ax in range(len(grid))]
        first = functools.reduce(jnp.logical_and, [i == 0 for i in ids])
        last = functools.reduce(jnp.logical_and, [i == g - 1 for i, g in zip(ids, grid)])

        @pl.when(first)
        def _():
            exch.start(xin, xout, sems)

        body(*ins, *outs, *scr)

        @pl.when(last)
        def _():
            exch.finish(xin, xout, sems)

    res = pl.pallas_call(
        wrapped, name=name, grid=grid, in_specs=list(in_specs) + [ANY_SPEC] * m,
        out_specs=out_specs_l + [ANY_SPEC] * m, out_shape=out_shape_l + exch.out_shape,
        scratch_shapes=list(scratch_shapes) + exch.scratch, compiler_params=_cparams(sem),
    )(*args, *exch.arrays)
    outs = res[:n_out]
    return (outs[0] if single else outs), res[n_out:]


def _adam_math(w, g, m, v):
    m2 = ADAM_B1 * m + (1.0 - ADAM_B1) * g
    v2 = ADAM_B2 * v + (1.0 - ADAM_B2) * (g * g)
    m_hat = m2 / (1.0 - ADAM_B1 ** ADAM_STEP)
    v_hat = v2 / (1.0 - ADAM_B2 ** ADAM_STEP)
    delta = -ADAM_LR * (m_hat / (jnp.sqrt(v_hat) + ADAM_EPS) + ADAM_WD * w)
    return delta, m2, v2


ADAM_ROW_BLOCKS = 2


def adam_shard(parts0, parts1, w, m, v, name, part_slice=None):
    _, r, c = w.shape
    sub = part_slice
    rb = r // ADAM_ROW_BLOCKS

    def body(p0_ref, p1_ref, w_ref, m_ref, v_ref, g_ref, d_ref, m2_ref, v2_ref):
        def run(p_ref):
            g = p_ref[0].astype(F32)
            for i in range(1, N_DEV):
                g = g + p_ref[i].astype(F32)
            delta, m2, v2 = _adam_math(w_ref[0], g, m_ref[0], v_ref[0])
            g_ref[0] = g
            d_ref[0] = delta
            m2_ref[0] = m2
            v2_ref[0] = v2

        @pl.when(pl.program_id(0) == 0)
        def _():
            run(p0_ref)

        @pl.when(pl.program_id(0) == 1)
        def _():
            run(p1_ref)

    def p_spec(layer):
        row = (lambda l, j: jnp.where(l == 0, j, ADAM_ROW_BLOCKS - 1)) if layer == 0 else (lambda l, j: jnp.where(l == 1, j, 0))
        if sub is None:
            return pl.BlockSpec((N_DEV, rb, c), lambda l, j: (0, row(l, j), 0))
        return pl.BlockSpec((N_DEV, None, rb, c), lambda l, j: (0, sub, row(l, j), 0))

    blk = pl.BlockSpec((1, rb, c), lambda l, j: (l, j, 0))
    return pl.pallas_call(
        body, name=name, grid=(DEPTH, ADAM_ROW_BLOCKS),
        in_specs=[p_spec(0), p_spec(1), blk, blk, blk], out_specs=[blk] * 4,
        out_shape=[SDS(w.shape, F32)] * 4,
        compiler_params=_cparams(("arbitrary", "arbitrary")),
    )(parts0, parts1, w, m, v)


def adam_small(parts, w, m, v, name):
    def body(p_ref, w_ref, m_ref, v_ref, g_ref, d_ref, m2_ref, v2_ref):
        g = p_ref[0]
        for i in range(1, N_DEV):
            g = g + p_ref[i]
        delta, m2, v2 = _adam_math(w_ref[...], g, m_ref[...], v_ref[...])
        g_ref[...] = g
        d_ref[...] = delta
        m2_ref[...] = m2
        v2_ref[...] = v2

    return pl.pallas_call(
        body, name=name, out_shape=[SDS(w.shape, F32)] * 4, compiler_params=_cparams(),
    )(parts, w, m, v)


def _rot_cols(w):
    w4 = w.reshape(w.shape[0], 4, 2, 32)
    return jnp.stack([-w4[:, :, 1], w4[:, :, 0]], axis=2).reshape(w.shape[0], ATT_W)


def _rot_cols_t(dw_rot):
    d4 = dw_rot.reshape(dw_rot.shape[0], 4, 2, 32)
    return jnp.stack([d4[:, :, 1], -d4[:, :, 0]], axis=2).reshape(dw_rot.shape[0], ATT_W)


def build_wext(w_in):
    aq, ak, av, pu = w_in[:, 0:256], w_in[:, 256:512], w_in[:, 512:768], w_in[:, 768:1024]
    dqkvz = w_in[:, 1024:3072]
    gates = jnp.repeat(w_in[:, 3072:3080], DN_E, axis=1)
    return jnp.concatenate([aq, ak, av, _rot_cols(aq), _rot_cols(ak), dqkvz, gates, pu], axis=1)


def fold_dwext(d):
    b = EXT_ATT
    aq = d[:, 0:256] + _rot_cols_t(d[:, 768:1024])
    ak = d[:, 256:512] + _rot_cols_t(d[:, 1024:1280])
    av = d[:, 512:768]
    dqkvz = d[:, b:b + 2048]
    gates = d[:, b + R_BB:b + R_BB + 1024].reshape(d.shape[0], 8, DN_E).sum(axis=-1)
    pu = d[:, b + R_PU:b + R_PU + 256]
    return jnp.concatenate([aq, ak, av, pu, dqkvz, gates], axis=1)


def _block_diag(pw):
    z = jnp.zeros((4, 64, 4, 64), pw.dtype)
    for g in range(4):
        z = z.at[g, :, g, :].set(pw[g])
    return z.reshape(POOL_W, POOL_W)


def _diag_blocks(m):
    m4 = m.reshape(4, 64, 4, 64)
    return jnp.stack([m4[g, :, g, :] for g in range(4)], axis=0)


def _lanes(v, reps):
    return jnp.repeat(v, reps)[None, :]


def layer_fwd(p, xa, cos, sin, l, host=None):
    host = host or {}

    def carried(key):
        return host[key][0] if key in host else None

    def done(key, xo):
        if key in host:
            host[key][1](xo)

    xb, xo = ffn_fwd(xa, p["n1"], p["f1gu"], p["f1d"], f"ffn1_fwd_{l}", carried("ffn1"))
    done("ffn1", xo)
    att, rest = mix_in_fwd(xb, p["nm"], p["wext"], cos, sin, f"mix_in_fwd_{l}")
    ols = [att_fwd_s(att, d, f"att_fwd_{l}_{d}") for d in DILATIONS]
    qkv = dn_prep_fwd(rest, p["conv"], f"dn_prep_fwd_{l}")
    dn, xo = dn_intra_fwd(qkv, rest, p["alog"], p["dtb"], f"dn_intra_fwd_{l}", carried("dn_intra"))
    done("dn_intra", xo)
    (odn, states), xo = dn_inter_fwd(*dn, f"dn_inter_fwd_{l}", carried("dn_inter"))
    done("dn_inter", xo)
    xc = mix_out_fwd(xb, ols[0], ols[1], ols[2], rest, odn, p["wbd"], p["scale"], p["onorm"], p["wout"], f"mix_out_fwd_{l}")
    xd, xo = ffn_fwd(xc, p["n2"], p["f2gu"], p["f2d"], f"ffn2_fwd_{l}", carried("ffn2"))
    done("ffn2", xo)
    return xd, dict(xa=xa, xb=xb, xc=xc, att=att, rest=rest, ols=ols, qkv=qkv, dn=dn, odn=odn, states=states)


def layer_bwd(p, s, dx, cos, sin, headsum, l, scatter=False, carry=None):
    (dx, d_f2gu, d_f2d, d_n2), carried = ffn_bwd(s["xc"], dx, p["n2"], p["f2gu"], p["f2d"], f"ffn2_bwd_{l}", carry)
    (d_wout, dol1, dol4, dol16, dpooled, dodn, dz, dscale, donorm, dwbd) = mix_out_bwd(
        dx, s["ols"][0], s["ols"][1], s["ols"][2], s["rest"], s["odn"], p["wbd"], p["scale"], p["onorm"], p["wout"],
        headsum, f"mix_out_bwd_{l}")
    dpu = pool_bwd(dpooled, f"pool_bwd_{l}")
    f2 = list(ffn_grads_to_shards(d_f2gu, d_f2d))
    d_dn = dn_inter_bwd(*s["dn"], s["states"], dodn, f"dn_inter_bwd_{l}")
    dqkv, dbb, dab, dalog, ddtb = dn_intra_bwd(s["qkv"], s["rest"], p["alog"], p["dtb"], *d_dn, f"dn_intra_bwd_{l}")
    d_dqkv, dconv = dn_prep_bwd(s["rest"], p["conv"], dqkv, f"dn_prep_bwd_{l}")
    datts = [att_bwd_s(s["att"], ol, dol, d, f"att_bwd_{l}_{d}")
             for d, ol, dol in zip(DILATIONS, s["ols"], (dol1, dol4, dol16))]
    dproj = assemble_dproj(datts, cos, sin, d_dqkv, dz, dbb, dab, dpu, f"assemble_dproj_{l}")
    dx, d_wext, d_nm = linear_bwd(s["xb"], dx, p["nm"], dproj, p["wext"], f"mix_in_bwd_{l}")
    d_win = fold_dwext(d_wext).reshape(D_MODEL, N_DEV, IN_BLK).transpose(1, 0, 2).astype(BF16)
    io = [d_win, d_wout.reshape(N_DEV, D_MODEL // N_DEV, D_MODEL).astype(BF16)]
    (dx, d_f1gu, d_f1d, d_n1), xo = ffn_bwd(s["xa"], dx, p["n1"], p["f1gu"], p["f1d"], f"ffn1_bwd_{l}",
                                           Exchange(f2 + io, "scatter") if scatter else None)
    if scatter:
        f2, io = list(xo[:2]), list(xo[2:])
    big = dict(f1=list(ffn_grads_to_shards(d_f1gu, d_f1d)), f2=f2, io=io)
    small = dict(ffn1_norm=d_n1[0], mix_norm=d_nm[0], ffn2_norm=d_n2[0], pool_w=_diag_blocks(dwbd),
                 pool_scale=dscale[0], dn_a_log=dalog.reshape(DN_H, DN_E).sum(-1),
                 dn_dt_bias=ddtb.reshape(DN_H, DN_E).sum(-1),
                 dn_out_norm=donorm.reshape(DN_H, DN_E).sum(0), dn_conv_w=dconv)
    return dx, big, small, carried


def small_operands(l, pool_w, pool_scale, dn_out_norm, dn_a_log, dn_dt_bias, ffn1_norm, mix_norm, ffn2_norm):
    return dict(
        wbd=_block_diag(pool_w[l]).astype(BF16),
        scale=pool_scale[l][None, :],
        onorm=jnp.tile(dn_out_norm[l], DN_H)[None, :],
        alog=_lanes(dn_a_log[l], DN_E),
        dtb=_lanes(dn_dt_bias[l], DN_E),
        n1=ffn1_norm[l][None, :], nm=mix_norm[l][None, :], n2=ffn2_norm[l][None, :])


def set_mixer_weights(p, win_g, wout_g, conv_g):
    p["wext"] = build_wext(win_g.transpose(1, 0, 2).reshape(D_MODEL, IN_W))
    p["wout"] = wout_g.reshape(D_MODEL, D_MODEL)
    p["conv"] = conv_g.transpose(1, 0, 2).reshape(DN_CONV, 3 * DN_W)


def rope_tables(pos):
    inv_freq = 10000.0 ** (-jnp.arange(0, ATT_E, 2, dtype=F32) / ATT_E)
    ang = pos.astype(F32)[:, None] * inv_freq
    return jnp.tile(jnp.cos(ang), (1, 8)), jnp.tile(jnp.sin(ang), (1, 8))


def head_sum_matrix():
    return jnp.kron(jnp.eye(4, dtype=F32), jnp.ones((ATT_E, ATT_E), F32))


SMALL_NAMES = ("ffn1_norm", "mix_norm", "ffn2_norm", "pool_w", "pool_scale", "dn_a_log", "dn_dt_bias",
               "dn_out_norm", "final_norm", "dn_conv_w")


PACK_UNIT = 8 * 128


def _pack_rows(n):
    return -(-n // PACK_UNIT) * 8


def _pack(parts):
    rows = []
    for p in parts:
        flat = p.reshape(-1)
        r = _pack_rows(flat.shape[0])
        rows.append(jnp.pad(flat, (0, r * 128 - flat.shape[0])).reshape(r, 128))
    return jnp.concatenate(rows, axis=0)


def _unpack(packed, shapes):
    out, row = [], 0
    for s in shapes:
        n = math.prod(s)
        r = _pack_rows(n)
        out.append(packed[row:row + r].reshape(-1)[:n].reshape(s))
        row += r
    return out


def kernel(x, positions, ffn1_norm, ffn1_w_gate, ffn1_w_up, ffn1_w_down, mix_norm, w_in, pool_w, pool_scale, dn_conv_w, dn_a_log, dn_dt_bias, dn_out_norm, w_out, ffn2_norm, ffn2_w_gate, ffn2_w_up, ffn2_w_down, final_norm, loss_target, m_ffn1_norm, m_ffn1_w_gate, m_ffn1_w_up, m_ffn1_w_down, m_mix_norm, m_w_in, m_pool_w, m_pool_scale, m_dn_conv_w, m_dn_a_log, m_dn_dt_bias, m_dn_out_norm, m_w_out, m_ffn2_norm, m_ffn2_w_gate, m_ffn2_w_up, m_ffn2_w_down, m_final_norm, v_ffn1_norm, v_ffn1_w_gate, v_ffn1_w_up, v_ffn1_w_down, v_mix_norm, v_w_in, v_pool_w, v_pool_scale, v_dn_conv_w, v_dn_a_log, v_dn_dt_bias, v_dn_out_norm, v_w_out, v_ffn2_norm, v_ffn2_w_gate, v_ffn2_w_up, v_ffn2_w_down, v_final_norm):
    me = 4 * lax.axis_index("x") + 2 * lax.axis_index("y") + lax.axis_index("c")
    x0 = x[0]
    target = loss_target[0]

    cos, sin = rope_tables(positions[0])
    headsum = head_sum_matrix()

    layers = [small_operands(l, pool_w, pool_scale, dn_out_norm, dn_a_log, dn_dt_bias, ffn1_norm, mix_norm, ffn2_norm)
              for l in range(DEPTH)]

    def ffn_shards(gate, up, down, l):
        return [jnp.stack([gate[l], up[l]]).astype(BF16), down[l].astype(BF16)]

    def gather_ffn1(l):
        def on_done(xo):
            layers[l]["f1gu"], layers[l]["f1d"] = ffn_weights_from_shards(*xo)
        return Exchange(ffn_shards(ffn1_w_gate, ffn1_w_up, ffn1_w_down, l), "gather"), on_done

    def gather_mixer(l):
        def on_done(xo):
            set_mixer_weights(layers[l], *xo)
        return Exchange([w_in[l].astype(BF16), w_out[l].astype(BF16), dn_conv_w[l]], "gather"), on_done

    gathered_f2 = {}

    def gather_ffn2_part(l, part):
        def on_done(xo):
            gathered_f2[(l, part)] = xo[0]
            if (l, 0) in gathered_f2 and (l, 1) in gathered_f2:
                layers[l]["f2gu"], layers[l]["f2d"] = ffn_weights_from_shards(gathered_f2[(l, 0)], gathered_f2[(l, 1)])
        return Exchange([ffn_shards(ffn2_w_gate, ffn2_w_up, ffn2_w_down, l)[part]], "gather"), on_done

    first, on_first = gather_ffn1(0)
    on_first(run_exchange(first, "gather_ffn1_0"))
    saved = []
    xa = x0
    for l in range(DEPTH):
        host = {"ffn1": gather_mixer(l), "dn_intra": gather_ffn2_part(l, 0), "dn_inter": gather_ffn2_part(l, 1)}
        if l + 1 < DEPTH:
            host["ffn2"] = gather_ffn1(l + 1)
        xa, s = layer_fwd(layers[l], xa, cos, sin, l, host)
        saved.append(s)

    loss_row, dx, d_final = loss_head(xa, final_norm[None, :], target, "loss_head")
    loss = lax.psum(loss_row[0, 0], ("x", "y", "c"))

    small = {}
    big_parts = [None] * DEPTH
    carry = None
    for l in reversed(range(DEPTH)):
        dx, big, small[l], carried = layer_bwd(layers[l], saved[l], dx, cos, sin, headsum, l, True, carry)
        if carried is not None:
            big_parts[l + 1]["f1"] = list(carried)
        big_parts[l] = big
        carry = Exchange(big["f1"], "scatter")
    big_parts[0]["f1"] = list(run_exchange(carry, "scatter_ffn1_0"))
    grad_x = dx[None]

    small_shapes = {"ffn1_norm": (DEPTH, D_MODEL), "mix_norm": (DEPTH, D_MODEL), "ffn2_norm": (DEPTH, D_MODEL),
                    "pool_w": (DEPTH, 4, 64, 64), "pool_scale": (DEPTH, POOL_W), "dn_a_log": (DEPTH, DN_H),
                    "dn_dt_bias": (DEPTH, DN_H), "dn_out_norm": (DEPTH, DN_E), "final_norm": (D_MODEL,),
                    "dn_conv_w": (DEPTH, DN_CONV, 3 * DN_W)}
    g_small = {n: (d_final[0] if n == "final_norm" else jnp.stack([small[l][n] for l in range(DEPTH)]))
               for n in SMALL_NAMES}
    (small_parts,) = run_exchange(Exchange([_pack([g_small[n] for n in SMALL_NAMES])], "gather"), "gather_small_grads")

    def conv_full(a):
        return lax.dynamic_update_slice(jnp.zeros((DEPTH, DN_CONV, 3 * DN_W), F32), a, (0, 0, me * (3 * DN_W // N_DEV)))

    given = dict(ffn1_norm=(ffn1_norm, m_ffn1_norm, v_ffn1_norm), mix_norm=(mix_norm, m_mix_norm, v_mix_norm),
                 ffn2_norm=(ffn2_norm, m_ffn2_norm, v_ffn2_norm), pool_w=(pool_w, m_pool_w, v_pool_w),
                 pool_scale=(pool_scale, m_pool_scale, v_pool_scale), dn_a_log=(dn_a_log, m_dn_a_log, v_dn_a_log),
                 dn_dt_bias=(dn_dt_bias, m_dn_dt_bias, v_dn_dt_bias),
                 dn_out_norm=(dn_out_norm, m_dn_out_norm, v_dn_out_norm),
                 final_norm=(final_norm, m_final_norm, v_final_norm),
                 dn_conv_w=(conv_full(dn_conv_w), conv_full(m_dn_conv_w), conv_full(v_dn_conv_w)))
    packed_wmv = [_pack([given[n][k] for n in SMALL_NAMES]) for k in range(3)]
    small_out = adam_small(small_parts, *packed_wmv, "adam_small")
    shapes = [small_shapes[n] for n in SMALL_NAMES]
    small_res = {n: [] for n in SMALL_NAMES}
    for arr in small_out:
        for n, v_ in zip(SMALL_NAMES, _unpack(arr, shapes)):
            if n == "dn_conv_w":
                v_ = lax.dynamic_slice(v_, (0, 0, me * (3 * DN_W // N_DEV)), (DEPTH, DN_CONV, 3 * DN_W // N_DEV))
            small_res[n].append(v_)

    def parts_of(group, idx):
        return [big_parts[l][group][idx] for l in range(DEPTH)]

    big_res = dict(
        ffn1_w_gate=adam_shard(*parts_of("f1", 0), ffn1_w_gate, m_ffn1_w_gate, v_ffn1_w_gate, "adam_ffn1_gate", 0),
        ffn1_w_up=adam_shard(*parts_of("f1", 0), ffn1_w_up, m_ffn1_w_up, v_ffn1_w_up, "adam_ffn1_up", 1),
        ffn1_w_down=adam_shard(*parts_of("f1", 1), ffn1_w_down, m_ffn1_w_down, v_ffn1_w_down, "adam_ffn1_down"),
        ffn2_w_gate=adam_shard(*parts_of("f2", 0), ffn2_w_gate, m_ffn2_w_gate, v_ffn2_w_gate, "adam_ffn2_gate", 0),
        ffn2_w_up=adam_shard(*parts_of("f2", 0), ffn2_w_up, m_ffn2_w_up, v_ffn2_w_up, "adam_ffn2_up", 1),
        ffn2_w_down=adam_shard(*parts_of("f2", 1), ffn2_w_down, m_ffn2_w_down, v_ffn2_w_down, "adam_ffn2_down"),
        w_in=adam_shard(*parts_of("io", 0), w_in, m_w_in, v_w_in, "adam_w_in"),
        w_out=adam_shard(*parts_of("io", 1), w_out, m_w_out, v_w_out, "adam_w_out"),
    )

    order = ("ffn1_norm", "ffn1_w_gate", "ffn1_w_up", "ffn1_w_down", "mix_norm", "w_in", "pool_w", "pool_scale",
             "dn_conv_w", "dn_a_log", "dn_dt_bias", "dn_out_norm", "w_out", "ffn2_norm", "ffn2_w_gate", "ffn2_w_up",
             "ffn2_w_down", "final_norm")
    res = {**small_res, **big_res}
    outs = [loss, grad_x]
    for k in range(4):
        outs.extend(res[n][k] for n in order)
    return tuple(outs)
```

```python
import functools
import math

import jax
import jax.numpy as jnp
from jax import lax
from jax.experimental import pallas as pl
from jax.experimental.pallas import tpu as pltpu

F32 = jnp.float32
BF16 = jnp.bfloat16
HI = lax.Precision.HIGHEST
INV_PREC = lax.Precision.HIGH
SDS = jax.ShapeDtypeStruct

N_DEV = 8
SEQ = 4096
D_MODEL = 1024
DEPTH = 2
D_FF = 2816
FF_BLK = D_FF // N_DEV
ATT_W = 256
ATT_E = 64
ATT_BLK = 128
DILATIONS = (1, 4, 16)
POOL_W = 256
POOL_HALO = 16
DN_W = 512
DN_H = 4
DN_E = 128
DN_C = 64
N_CHUNK = SEQ // DN_C
IN_W = 3080
IN_BLK = IN_W // N_DEV
EPS = 1e-6
EXT_ATT = 1280
EXT_REST = 3328
EXT_W = EXT_ATT + EXT_REST
R_DQKV, R_DZ, R_BB, R_AB, R_PU = 0, 1536, 2048, 2560, 3072

ADAM_LR, ADAM_B1, ADAM_B2, ADAM_EPS, ADAM_WD, ADAM_STEP = 0.001, 0.9, 0.999, 1e-08, 0.01, 10

VMEM_LIMIT = 60 * 1024 * 1024
MESH = pl.DeviceIdType.MESH


def _cparams(sem=None):
    kw = dict(vmem_limit_bytes=VMEM_LIMIT)
    if sem is not None:
        kw["dimension_semantics"] = sem
    return pltpu.CompilerParams(**kw)


def _dot(a, b, prec=None):
    return jnp.dot(a, b, preferred_element_type=F32, precision=prec)


def _dot_nt(a, b, prec=None):
    return lax.dot_general(a, b, (((1,), (1,)), ((), ())), preferred_element_type=F32, precision=prec)


def _dot_tn(a, b, prec=None):
    return lax.dot_general(a, b, (((0,), (0,)), ((), ())), preferred_element_type=F32, precision=prec)


def _sigmoid(x):
    return jax.nn.sigmoid(x)


def _rms_stats(x):
    r = lax.rsqrt(jnp.mean(x * x, axis=-1, keepdims=True) + EPS)
    return x * r, r


def _rms_bwd(xh, r, w, dh):
    dxh = dh * w
    dx = r * (dxh - xh * jnp.mean(dxh * xh, axis=-1, keepdims=True))
    return dx, jnp.sum(dh * xh, axis=0, keepdims=True)


FFN_T_FWD = 2048
FFN_T_BWD = 512
FF_TILE = 256
N_FF_TILE = D_FF // FF_TILE


def ffn_shard_operands(gate, up, down):
    return [gate.T.astype(BF16), up.T.astype(BF16), down.astype(BF16)]


def ffn_fwd(x, nw, wgt, wut, wd, name, exch=None):
    t = FFN_T_FWD

    def body(x_ref, nw_ref, wgt_ref, wut_ref, wd_ref, o_ref, h_scr, acc_scr):
        k = pl.program_id(1)

        @pl.when(k == 0)
        def _():
            xh, _r = _rms_stats(x_ref[...])
            h_scr[...] = (xh * nw_ref[...]).astype(BF16)
            acc_scr[...] = jnp.zeros_like(acc_scr)

        h = h_scr[...]
        hg = _dot_nt(h, wgt_ref[...])
        hu = _dot_nt(h, wut_ref[...])
        a = (hg * _sigmoid(hg) * hu).astype(BF16)
        acc_scr[...] += _dot(a, wd_ref[...])

        @pl.when(k == N_FF_TILE - 1)
        def _():
            o_ref[...] = x_ref[...] + 0.5 * acc_scr[...]

    w_spec = pl.BlockSpec((FF_TILE, D_MODEL), lambda i, k: (k, 0))
    return _call(
        body, name=name, grid=(SEQ // t, N_FF_TILE),
        in_specs=[pl.BlockSpec((t, D_MODEL), lambda i, k: (i, 0)),
                  pl.BlockSpec((1, D_MODEL), lambda i, k: (0, 0)), w_spec, w_spec, w_spec],
        out_specs=pl.BlockSpec((t, D_MODEL), lambda i, k: (i, 0)),
        out_shape=SDS((SEQ, D_MODEL), F32),
        scratch_shapes=[pltpu.VMEM((t, D_MODEL), BF16), pltpu.VMEM((t, D_MODEL), F32)],
        sem=("arbitrary", "arbitrary"), args=(x, nw, wgt, wut, wd), exch=exch)


def ffn_bwd(x, dxo, nw, wgt, wut, wd, name, exch=None):
    t = FFN_T_BWD
    nt = SEQ // t

    def body(x_ref, dxo_ref, nw_ref, wgt_ref, wut_ref, wd_ref, dx_ref, dwgt_ref, dwut_ref, dwd_ref, dnw_ref,
             dh_scr, ag_scr, au_scr, ad_scr, h_scr):
        k = pl.program_id(0)
        i = pl.program_id(1)
        rows = pl.ds(pl.multiple_of(i * t, t), t)
        nw_v = nw_ref[...]

        @pl.when(k == 0)
        def _():
            xh0, _r0 = _rms_stats(x_ref[...])
            h_scr[rows, :] = (xh0 * nw_v).astype(BF16)

        h = h_scr[rows, :]
        dy = (0.5 * dxo_ref[...]).astype(BF16)
        wgt = wgt_ref[...]
        wut = wut_ref[...]
        hg = _dot_nt(h, wgt)
        hu = _dot_nt(h, wut)
        sg = _sigmoid(hg)
        sil = hg * sg
        a = (sil * hu).astype(BF16)
        da = _dot_nt(dy, wd_ref[...])
        dhu = (da * sil).astype(BF16)
        dhg = (da * hu * (sg * (1.0 + hg * (1.0 - sg)))).astype(BF16)
        p_d = _dot_tn(a, dy)
        p_g = _dot_tn(dhg, h)
        p_u = _dot_tn(dhu, h)
        dh = _dot(dhg, wgt) + _dot(dhu, wut)

        @pl.when(i == 0)
        def _():
            ad_scr[...] = p_d
            ag_scr[...] = p_g
            au_scr[...] = p_u

        @pl.when(i > 0)
        def _():
            ad_scr[...] += p_d
            ag_scr[...] += p_g
            au_scr[...] += p_u

        @pl.when(i == nt - 1)
        def _():
            dwd_ref[...] = ad_scr[...].astype(BF16)
            dwgt_ref[...] = ag_scr[...].astype(BF16)
            dwut_ref[...] = au_scr[...].astype(BF16)

        @pl.when(k == 0)
        def _():
            dh_scr[rows, :] = dh

        @pl.when(k > 0)
        def _():
            dh_scr[rows, :] += dh

        @pl.when(jnp.logical_and(k == 0, i == 0))
        def _():
            dnw_ref[...] = jnp.zeros_like(dnw_ref)

        @pl.when(k == N_FF_TILE - 1)
        def _():
            xh, r = _rms_stats(x_ref[...])
            dx, dw = _rms_bwd(xh, r, nw_v, dh_scr[rows, :])
            dx_ref[...] = dxo_ref[...] + dx
            dnw_ref[...] += dw

    last = N_FF_TILE - 1
    w_spec = pl.BlockSpec((FF_TILE, D_MODEL), lambda k, i: (k, 0))
    return _call(
        body, name=name, grid=(N_FF_TILE, nt),
        in_specs=[pl.BlockSpec((t, D_MODEL), lambda k, i: (i, 0)),
                  pl.BlockSpec((t, D_MODEL), lambda k, i: (i, 0)),
                  pl.BlockSpec((1, D_MODEL), lambda k, i: (0, 0)), w_spec, w_spec, w_spec],
        out_specs=[pl.BlockSpec((t, D_MODEL), lambda k, i: (jnp.where(k == last, i, 0), 0)),
                   w_spec, w_spec, w_spec, pl.BlockSpec((1, D_MODEL), lambda k, i: (0, 0))],
        out_shape=[SDS((SEQ, D_MODEL), F32), SDS((D_FF, D_MODEL), BF16), SDS((D_FF, D_MODEL), BF16),
                   SDS((D_FF, D_MODEL), BF16), SDS((1, D_MODEL), F32)],
        scratch_shapes=[pltpu.VMEM((SEQ, D_MODEL), F32), pltpu.VMEM((FF_TILE, D_MODEL), F32),
                        pltpu.VMEM((FF_TILE, D_MODEL), F32), pltpu.VMEM((FF_TILE, D_MODEL), F32),
                        pltpu.VMEM((SEQ, D_MODEL), BF16)],
        sem=("arbitrary", "arbitrary"), args=(x, dxo, nw, wgt, wut, wd), exch=exch)


def loss_head(x, fw, target, name):
    t = 512

    def body(x_ref, fw_ref, tg_ref, loss_ref, dx_ref, dfw_ref):
        i = pl.program_id(0)
        xh, r = _rms_stats(x_ref[...])
        w = fw_ref[...]
        err = xh * w - tg_ref[...]
        part = 0.5 * jnp.sum(jnp.sum(err * err, axis=-1, keepdims=True), axis=0, keepdims=True) / D_MODEL
        dx, dw = _rms_bwd(xh, r, w, err * (1.0 / D_MODEL))
        dx_ref[...] = dx

        @pl.when(i == 0)
        def _():
            loss_ref[...] = jnp.zeros_like(loss_ref)
            dfw_ref[...] = jnp.zeros_like(dfw_ref)

        loss_ref[...] += jnp.broadcast_to(part, loss_ref.shape)
        dfw_ref[...] += dw

    return pl.pallas_call(
        body, name=name, grid=(SEQ // t,),
        in_specs=[pl.BlockSpec((t, D_MODEL), lambda i: (i, 0)),
                  pl.BlockSpec((1, D_MODEL), lambda i: (0, 0)),
                  pl.BlockSpec((t, D_MODEL), lambda i: (i, 0))],
        out_specs=[pl.BlockSpec((1, 128), lambda i: (0, 0)),
                   pl.BlockSpec((t, D_MODEL), lambda i: (i, 0)),
                   pl.BlockSpec((1, D_MODEL), lambda i: (0, 0))],
        out_shape=[SDS((1, 128), F32), SDS((SEQ, D_MODEL), F32), SDS((1, D_MODEL), F32)],
        compiler_params=_cparams(("arbitrary",)),
    )(x, fw, target)


MIX_T = 256


def _slabs_load(ref, first, n):
    return jnp.concatenate([ref[first + j] for j in range(n)], axis=1)


def _slabs_store(ref, first, val):
    for j in range(val.shape[1] // 128):
        ref[first + j] = val[:, 128 * j:128 * j + 128]


def _slab_spec(k, t):
    return pl.BlockSpec((k, t, 128), lambda i: (0, i, 0))


def mix_in_fwd(x, nw, wext, cos, sin, name):
    t = MIX_T

    def body(x_ref, nw_ref, w_ref, cos_ref, sin_ref, att_ref, rest_ref):
        xh, _r = _rms_stats(x_ref[...])
        h = (xh * nw_ref[...]).astype(BF16)
        pa = _dot(h, w_ref[:, 0:EXT_ATT])
        c = cos_ref[...]
        s = sin_ref[...]
        _slabs_store(att_ref, 0, pa[:, 0:256] * c + pa[:, 768:1024] * s)
        _slabs_store(att_ref, 2, pa[:, 256:512] * c + pa[:, 1024:1280] * s)
        _slabs_store(att_ref, 4, pa[:, 512:768])
        for j in range(EXT_REST // 256):
            rest_ref[:, 256 * j:256 * j + 256] = _dot(h, w_ref[:, EXT_ATT + 256 * j:EXT_ATT + 256 * j + 256])

    return pl.pallas_call(
        body, name=name, grid=(SEQ // t,),
        in_specs=[pl.BlockSpec((t, D_MODEL), lambda i: (i, 0)),
                  pl.BlockSpec((1, D_MODEL), lambda i: (0, 0)),
                  pl.BlockSpec((D_MODEL, EXT_W), lambda i: (0, 0)),
                  pl.BlockSpec((t, ATT_W), lambda i: (i, 0)),
                  pl.BlockSpec((t, ATT_W), lambda i: (i, 0))],
        out_specs=[_slab_spec(6, t),
                   pl.BlockSpec((t, EXT_REST), lambda i: (i, 0))],
        out_shape=[SDS((6, SEQ, 128), F32), SDS((SEQ, EXT_REST), F32)],
        compiler_params=_cparams(("arbitrary",)),
    )(x, nw, wext, cos, sin)


def assemble_dproj(datts, cos, sin, d_dqkv, dz, dbb, dab, dpu, name):
    t = 512

    def body(d1_ref, d4_ref, d16_ref, cos_ref, sin_ref, dqkv_ref, dz_ref, dbb_ref, dab_ref, dpu_ref, o_ref):
        da6 = d1_ref[...] + d4_ref[...] + d16_ref[...]
        da = jnp.concatenate([da6[j] for j in range(6)], axis=1)
        c = cos_ref[...]
        s = sin_ref[...]
        dq = da[:, 0:256]
        dk = da[:, 256:512]
        o_ref[:, 0:256] = (dq * c).astype(BF16)
        o_ref[:, 256:512] = (dk * c).astype(BF16)
        o_ref[:, 512:768] = da[:, 512:768].astype(BF16)
        o_ref[:, 768:1024] = (dq * s).astype(BF16)
        o_ref[:, 1024:1280] = (dk * s).astype(BF16)
        b = EXT_ATT
        o_ref[:, b + R_DQKV:b + R_DQKV + 1536] = dqkv_ref[...].astype(BF16)
        o_ref[:, b + R_DZ:b + R_DZ + 512] = dz_ref[...].astype(BF16)
        o_ref[:, b + R_BB:b + R_BB + 512] = dbb_ref[...].astype(BF16)
        o_ref[:, b + R_AB:b + R_AB + 512] = dab_ref[...].astype(BF16)
        o_ref[:, b + R_PU:b + R_PU + 256] = dpu_ref[...].astype(BF16)

    row = lambda w: pl.BlockSpec((t, w), lambda i: (i, 0))
    return pl.pallas_call(
        body, name=name, grid=(SEQ // t,),
        in_specs=[_slab_spec(6, t), _slab_spec(6, t), _slab_spec(6, t),
                  row(256), row(256), row(1536), row(512), row(512), row(512), row(256)],
        out_specs=row(EXT_W),
        out_shape=SDS((SEQ, EXT_W), BF16),
        compiler_params=_cparams(("arbitrary",)),
    )(*datts, cos, sin, d_dqkv, dz, dbb, dab, dpu)


def linear_bwd(x, dxo, nw, dy, w, name):
    t = 512
    nb = 768
    n = w.shape[1]
    nt = SEQ // t
    nn = n // nb

    def body(x_ref, dxo_ref, nw_ref, dy_ref, w_ref, dx_ref, dw_ref, dnw_ref, dh_scr, h_scr):
        k = pl.program_id(0)
        i = pl.program_id(1)
        rows = pl.ds(pl.multiple_of(i * t, t), t)
        nw_v = nw_ref[...]

        @pl.when(k == 0)
        def _():
            xh0, _r0 = _rms_stats(x_ref[...])
            h_scr[rows, :] = (xh0 * nw_v).astype(BF16)

        h = h_scr[rows, :]
        dyv = dy_ref[...]
        p_w = _dot_tn(h, dyv)
        dh = _dot_nt(dyv, w_ref[...])

        @pl.when(i == 0)
        def _():
            dw_ref[...] = p_w

        @pl.when(i > 0)
        def _():
            dw_ref[...] += p_w

        @pl.when(k == 0)
        def _():
            dh_scr[rows, :] = dh

        @pl.when(k > 0)
        def _():
            dh_scr[rows, :] += dh

        @pl.when(jnp.logical_and(k == 0, i == 0))
        def _():
            dnw_ref[...] = jnp.zeros_like(dnw_ref)

        @pl.when(k == nn - 1)
        def _():
            xh, r = _rms_stats(x_ref[...])
            dx, dw = _rms_bwd(xh, r, nw_v, dh_scr[rows, :])
            dx_ref[...] = dxo_ref[...] + dx
            dnw_ref[...] += dw

    last = nn - 1
    return pl.pallas_call(
        body, name=name, grid=(nn, nt),
        in_specs=[pl.BlockSpec((t, D_MODEL), lambda k, i: (i, 0)),
                  pl.BlockSpec((t, D_MODEL), lambda k, i: (i, 0)),
                  pl.BlockSpec((1, D_MODEL), lambda k, i: (0, 0)),
                  pl.BlockSpec((t, nb), lambda k, i: (i, k)),
                  pl.BlockSpec((D_MODEL, nb), lambda k, i: (0, k))],
        out_specs=[pl.BlockSpec((t, D_MODEL), lambda k, i: (jnp.where(k == last, i, 0), 0)),
                   pl.BlockSpec((D_MODEL, nb), lambda k, i: (0, k)),
                   pl.BlockSpec((1, D_MODEL), lambda k, i: (0, 0))],
        out_shape=[SDS((SEQ, D_MODEL), F32), SDS((D_MODEL, n), F32), SDS((1, D_MODEL), F32)],
        scratch_shapes=[pltpu.VMEM((SEQ, D_MODEL), F32), pltpu.VMEM((SEQ, D_MODEL), BF16)],
        compiler_params=_cparams(("arbitrary", "arbitrary")),
    )(x, dxo, nw, dy, w)


def _att_masks():
    qi = lax.broadcasted_iota(jnp.int32, (ATT_BLK, ATT_BLK), 0)
    ki = lax.broadcasted_iota(jnp.int32, (ATT_BLK, ATT_BLK), 1)
    return ki <= qi, ki >= qi


NEG = -1e30


N_ATT_BLK = SEQ // ATT_BLK


def _class_rows(i, d):
    per_class = N_ATT_BLK // d
    shift = per_class.bit_length() - 1
    r = i >> shift
    j = i & (per_class - 1)
    span = ATT_BLK * d
    start = r + span * j
    prev = jnp.where(j == 0, start, start - span)
    nxt = jnp.where(j == per_class - 1, start, start + span)

    def rows(s0):
        if d == 1:
            return pl.ds(pl.multiple_of(s0, ATT_BLK), ATT_BLK)
        return pl.ds(s0, ATT_BLK, stride=d)

    return rows(start), rows(prev), rows(nxt), j != 0, j != per_class - 1


def _slab_heads(ref, slab, rows):
    x0 = ref[pl.ds(slab, 1), rows, :][0]
    x1 = ref[pl.ds(slab + 1, 1), rows, :][0]
    return jnp.stack([x0[:, 0:ATT_E], x0[:, ATT_E:], x1[:, 0:ATT_E], x1[:, ATT_E:]], axis=0)


def _put_slab_heads(ref, slab, rows, val):
    ref[pl.ds(slab, 1), rows, :] = jnp.concatenate([val[0], val[1]], axis=1)[None]
    ref[pl.ds(slab + 1, 1), rows, :] = jnp.concatenate([val[2], val[3]], axis=1)[None]


def _resident_call(body, ins, out_slabs, d, name):
    n_in = len(ins)

    def wrapped(*refs):
        hbm_in, hbm_out = refs[:n_in], refs[n_in]
        vm_in, vm_out, sem = refs[n_in + 1:2 * n_in + 1], refs[2 * n_in + 1], refs[2 * n_in + 2]
        i = pl.program_id(0)

        @pl.when(i == 0)
        def _():
            copies = [pltpu.make_async_copy(h, v, sem.at[k]) for k, (h, v) in enumerate(zip(hbm_in, vm_in))]
            for cp in copies:
                cp.start()
            for cp in copies:
                cp.wait()

        body(i, *vm_in, vm_out)

        @pl.when(i == N_ATT_BLK - 1)
        def _():
            cp = pltpu.make_async_copy(vm_out, hbm_out, sem.at[n_in])
            cp.start()
            cp.wait()

    return pl.pallas_call(
        wrapped, name=name, grid=(N_ATT_BLK,),
        in_specs=[ANY_SPEC] * n_in, out_specs=ANY_SPEC, out_shape=SDS((out_slabs, SEQ, 128), F32),
        scratch_shapes=[pltpu.VMEM(a.shape, a.dtype) for a in ins] + [pltpu.VMEM((out_slabs, SEQ, 128), F32),
                                                                      pltpu.SemaphoreType.DMA((n_in + 1,))],
        compiler_params=_cparams(("arbitrary",)),
    )(*ins)


def _att_fwd_math(ld, has_prev):
    m_d, m_p = _att_masks()
    m_p = jnp.logical_and(m_p, has_prev)
    q = ld("att", 0, "cur").astype(BF16)
    kc = ld("att", 2, "cur").astype(BF16)
    vc = ld("att", 4, "cur").astype(BF16)
    kp = ld("att", 2, "prev").astype(BF16)
    vp = ld("att", 4, "prev").astype(BF16)
    sd = jnp.where(m_d, _bdot(q, kc, 2, 2) * 0.125, NEG)
    sp = jnp.where(m_p, _bdot(q, kp, 2, 2) * 0.125, NEG)
    m = jnp.maximum(jnp.max(sd, axis=-1, keepdims=True), jnp.max(sp, axis=-1, keepdims=True))
    pd = jnp.exp(sd - m)
    pp = jnp.exp(sp - m)
    den = jnp.sum(pd, axis=-1, keepdims=True) + jnp.sum(pp, axis=-1, keepdims=True)
    inv = 1.0 / den
    o = _bdot((pd * inv).astype(BF16), vc, 2, 1) + _bdot((pp * inv).astype(BF16), vp, 2, 1)
    return o, jnp.broadcast_to(m + jnp.log(den), (4, ATT_BLK, ATT_E))


def _att_bwd_math(ld, has_prev, has_next):
    m_d, m_band = _att_masks()
    m_p = jnp.logical_and(m_band, has_prev)
    m_n = jnp.logical_and(m_band, has_next)

    def pair(q, k, v, lse, do, dterm, mask):
        s = jnp.where(mask, _bdot(q, k, 2, 2) * 0.125, NEG)
        p = jnp.exp(s - lse)
        dp = _bdot(do, v, 2, 2)
        ds = (p * (dp + dterm) * 0.125).astype(BF16)
        return p.astype(BF16), ds

    q_c = ld("att", 0, "cur").astype(BF16)
    k_c = ld("att", 2, "cur").astype(BF16)
    v_c = ld("att", 4, "cur").astype(BF16)
    k_p = ld("att", 2, "prev").astype(BF16)
    v_p = ld("att", 4, "prev").astype(BF16)
    q_n = ld("att", 0, "next").astype(BF16)
    o_c = ld("ol", 0, "cur")
    o_n = ld("ol", 0, "next")
    lse_c = ld("ol", 2, "cur")[:, :, 0:1]
    lse_n = ld("ol", 2, "next")[:, :, 0:1]
    do_c = ld("dol", 0, "cur")
    do_n = ld("dol", 0, "next")
    t_c = ld("dol", 2, "cur")[:, :, 0:1] - jnp.sum(do_c * o_c, axis=-1, keepdims=True)
    t_n = ld("dol", 2, "next")[:, :, 0:1] - jnp.sum(do_n * o_n, axis=-1, keepdims=True)
    do_cb = do_c.astype(BF16)
    do_nb = do_n.astype(BF16)
    p1, ds1 = pair(q_c, k_c, v_c, lse_c, do_cb, t_c, m_d)
    _p2, ds2 = pair(q_c, k_p, v_p, lse_c, do_cb, t_c, m_p)
    p3, ds3 = pair(q_n, k_c, v_c, lse_n, do_nb, t_n, m_n)
    return (_bdot(ds1, k_c, 2, 1) + _bdot(ds2, k_p, 2, 1), _bdot(ds1, q_c, 1, 1) + _bdot(ds3, q_n, 1, 1),
            _bdot(p1, do_cb, 1, 1) + _bdot(p3, do_nb, 1, 1))


ALL_ROWS = pl.ds(0, ATT_BLK)


def att_fwd_s(att, d, name):
    if d == 1:
        def body1(cur_ref, prev_ref, o_ref):
            refs = {"cur": cur_ref, "prev": prev_ref}
            o, lse = _att_fwd_math(lambda _a, slab, where: _slab_heads(refs[where], slab, ALL_ROWS), pl.program_id(0) != 0)
            _put_slab_heads(o_ref, 0, ALL_ROWS, o)
            _put_slab_heads(o_ref, 2, ALL_ROWS, lse)

        return pl.pallas_call(
            body1, name=name, grid=(N_ATT_BLK,),
            in_specs=[pl.BlockSpec((6, ATT_BLK, 128), lambda i: (0, i, 0)),
                      pl.BlockSpec((6, ATT_BLK, 128), lambda i: (0, jnp.maximum(i - 1, 0), 0))],
            out_specs=pl.BlockSpec((4, ATT_BLK, 128), lambda i: (0, i, 0)),
            out_shape=SDS((4, SEQ, 128), F32), compiler_params=_cparams(("arbitrary",)),
        )(att, att)

    def body(i, att_ref, o_ref):
        cur, prev, _nxt, has_prev, _has_next = _class_rows(i, d)
        rows = {"cur": cur, "prev": prev}
        o, lse = _att_fwd_math(lambda _a, slab, where: _slab_heads(att_ref, slab, rows[where]), has_prev)
        _put_slab_heads(o_ref, 0, cur, o)
        _put_slab_heads(o_ref, 2, cur, lse)

    return _resident_call(body, [att], 4, d, name)


def att_bwd_s(att, ol, dol, d, name):
    if d == 1:
        def body1(a_p, a_c, a_n, ol_c, ol_n, dol_c, dol_n, d_ref):
            i = pl.program_id(0)
            refs = {("att", "prev"): a_p, ("att", "cur"): a_c, ("att", "next"): a_n, ("ol", "cur"): ol_c,
                    ("ol", "next"): ol_n, ("dol", "cur"): dol_c, ("dol", "next"): dol_n}
            dq, dk, dv = _att_bwd_math(lambda a, slab, where: _slab_heads(refs[(a, where)], slab, ALL_ROWS),
                                       i != 0, i != N_ATT_BLK - 1)
            _put_slab_heads(d_ref, 0, ALL_ROWS, dq)
            _put_slab_heads(d_ref, 2, ALL_ROWS, dk)
            _put_slab_heads(d_ref, 4, ALL_ROWS, dv)

        def blk(k, f):
            return pl.BlockSpec((k, ATT_BLK, 128), lambda i: (0, f(i), 0))

        prv = lambda i: jnp.maximum(i - 1, 0)
        cur = lambda i: i
        nxt = lambda i: jnp.minimum(i + 1, N_ATT_BLK - 1)
        return pl.pallas_call(
            body1, name=name, grid=(N_ATT_BLK,),
            in_specs=[blk(6, prv), blk(6, cur), blk(6, nxt), blk(4, cur), blk(4, nxt), blk(4, cur), blk(4, nxt)],
            out_specs=blk(6, cur), out_shape=SDS((6, SEQ, 128), F32), compiler_params=_cparams(("arbitrary",)),
        )(att, att, att, ol, ol, dol, dol)

    def body(i, att_ref, ol_ref, dol_ref, d_ref):
        cur, prev, nxt, has_prev, has_next = _class_rows(i, d)
        rows = {"cur": cur, "prev": prev, "next": nxt}
        refs = {"att": att_ref, "ol": ol_ref, "dol": dol_ref}
        dq, dk, dv = _att_bwd_math(lambda a, slab, where: _slab_heads(refs[a], slab, rows[where]), has_prev, has_next)
        _put_slab_heads(d_ref, 0, cur, dq)
        _put_slab_heads(d_ref, 2, cur, dk)
        _put_slab_heads(d_ref, 4, cur, dv)

    return _resident_call(body, [att, ol, dol], 6, d, name)


def _shift_down(x, k):
    rows = lax.broadcasted_iota(jnp.int32, x.shape, 0)
    return jnp.where(rows >= k, pltpu.roll(x, k, 0), 0.0)


def _shift_up(x, k):
    n = x.shape[0]
    rows = lax.broadcasted_iota(jnp.int32, x.shape, 0)
    return jnp.where(rows < n - k, pltpu.roll(x, n - k, 0), 0.0)


@functools.partial(jax.custom_vjp, nondiff_argnums=(1,))
def _delay(x, k):
    return _shift_down(x, k)


def _delay_fwd(x, k):
    return _shift_down(x, k), None


def _delay_bwd(k, _res, g):
    return (_shift_up(g, k),)


_delay.defvjp(_delay_fwd, _delay_bwd)

DN_CONV = 4


def _dn_prep_fn(u, w, kind):
    y = w[DN_CONV - 1:DN_CONV] * u
    for j in range(DN_CONV - 1):
        y = y + w[j:j + 1] * _delay(u, DN_CONV - 1 - j)
    y = y * _sigmoid(y)
    nrm = y * lax.rsqrt(jnp.sum(y * y, axis=-1, keepdims=True) + EPS)
    return jnp.where(kind == 0, nrm * (DN_E ** -0.5), jnp.where(kind == 1, nrm, y))


def dn_prep_fwd(rest, conv_w, name):
    def body(u_ref, w_ref, o_ref):
        j = pl.program_id(0)
        kind = (j >= DN_H).astype(jnp.int32) + (j >= 2 * DN_H).astype(jnp.int32)
        o_ref[...] = _dn_prep_fn(u_ref[...], w_ref[...], kind)

    return pl.pallas_call(
        body, name=name, grid=(3 * DN_H,),
        in_specs=[pl.BlockSpec((SEQ, DN_E), lambda j: (0, j)),
                  pl.BlockSpec((DN_CONV, DN_E), lambda j: (0, j))],
        out_specs=pl.BlockSpec((SEQ, DN_E), lambda j: (0, j)),
        out_shape=SDS((SEQ, 3 * DN_W), F32),
        compiler_params=_cparams(("arbitrary",)),
    )(rest, conv_w)


def dn_prep_bwd(rest, conv_w, dqkv, name):
    def body(u_ref, w_ref, g_ref, du_ref, dw_ref):
        j = pl.program_id(0)
        kind = (j >= DN_H).astype(jnp.int32) + (j >= 2 * DN_H).astype(jnp.int32)
        _y, vjp = jax.vjp(lambda u, w: _dn_prep_fn(u, w, kind), u_ref[...], w_ref[...])
        du, dw = vjp(g_ref[...])
        du_ref[...] = du
        dw_ref[...] = dw

    return pl.pallas_call(
        body, name=name, grid=(3 * DN_H,),
        in_specs=[pl.BlockSpec((SEQ, DN_E), lambda j: (0, j)),
                  pl.BlockSpec((DN_CONV, DN_E), lambda j: (0, j)),
                  pl.BlockSpec((SEQ, DN_E), lambda j: (0, j))],
        out_specs=[pl.BlockSpec((SEQ, DN_E), lambda j: (0, j)),
                   pl.BlockSpec((DN_CONV, DN_E), lambda j: (0, j))],
        out_shape=[SDS((SEQ, 3 * DN_W), F32), SDS((DN_CONV, 3 * DN_W), F32)],
        compiler_params=_cparams(("arbitrary",)),
    )(rest, conv_w, dqkv)


def _bdot(a, b, ca, cb, prec=None):
    return lax.dot_general(a, b, (((ca,), (cb,)), ((0,), (0,))), preferred_element_type=F32, precision=prec)


def _unit_lower_inverse(a):
    eye = (lax.broadcasted_iota(jnp.int32, (DN_C, DN_C), 0) == lax.broadcasted_iota(jnp.int32, (DN_C, DN_C), 1)).astype(F32)
    p = eye - a
    b = _bdot(a, a, 2, 1, INV_PREC)
    for lvl in range(5):
        p = p + _bdot(p, b, 2, 1, INV_PREC)
        if lvl < 4:
            b = _bdot(b, b, 2, 1, INV_PREC)
    return p


@jax.custom_vjp
def _tri_inv(a):
    return _unit_lower_inverse(a)


def _tri_inv_fwd(a):
    t = _unit_lower_inverse(a)
    return t, t


def _tri_inv_bwd(t, g):
    return (-_bdot(_bdot(t, g, 1, 1, INV_PREC), t, 2, 2, INV_PREC),)


_tri_inv.defvjp(_tri_inv_fwd, _tri_inv_bwd)


def _b16(x):
    return x.astype(BF16)


def _heads(ref, base=0):
    return jnp.stack([ref[:, base + DN_E * hd:base + DN_E * hd + DN_E] for hd in range(DN_H)], axis=0)


def _put_heads(ref, val, base=0):
    for hd in range(DN_H):
        ref[:, base + DN_E * hd:base + DN_E * hd + DN_E] = val[hd]


DN_G_LOG2 = 3
DN_G = 1 << DN_G_LOG2
N_INST = DN_G * DN_H


def _dn_intra(q, k, v, bb, ab, alog, dtb):
    ri = lax.broadcasted_iota(jnp.int32, (DN_C, DN_C), 0)
    ci = lax.broadcasted_iota(jnp.int32, (DN_C, DN_C), 1)
    lower = ri >= ci
    strict = ri > ci
    nh = q.shape[0]
    beta = _sigmoid(bb)
    xg = ab + dtb
    softplus = jnp.maximum(xg, 0.0) + jnp.log(1.0 + jnp.exp(-jnp.abs(xg)))
    gi = -jnp.exp(alog) * softplus
    g = _bdot(jnp.broadcast_to(lower.astype(F32), (nh, DN_C, DN_C)), gi, 2, 1, HI)
    eg = jnp.exp(g)
    kb = k * beta
    vb = v * beta
    g_col = g[:, :, 0:DN_C]
    g_row = _bdot(jnp.full((nh, DN_C, DN_E), 1.0 / DN_E, F32), g, 2, 2, HI)
    decay = jnp.where(lower, jnp.exp(jnp.where(lower, g_col - g_row, 0.0)), 0.0)
    kbf = _b16(k)
    a = jnp.where(strict, _bdot(_b16(kb), kbf, 2, 2) * decay, 0.0)
    tb = _b16(_tri_inv(a))
    u = _bdot(tb, _b16(vb), 2, 1)
    w = _bdot(tb, _b16(kb * eg), 2, 1)
    intra = jnp.where(lower, _bdot(_b16(q), kbf, 2, 2) * decay, 0.0)
    g_last = g[:, DN_C - 1:DN_C, :]
    return u, w, q * eg, k * jnp.exp(g_last - g), intra, jnp.exp(g_last)


def _dn_inter(u, w, qg, kdec, intra, egl, state):
    sb = _b16(state)
    v_new = u - _bdot(_b16(w), sb, 2, 1)
    o = _bdot(_b16(qg), sb, 2, 1) + _bdot(_b16(intra), _b16(v_new), 2, 1)
    return o, state * egl + _bdot(_b16(kdec), _b16(v_new), 1, 1)


def _inst(ref, base=0):
    per_head = [ref[:, base + DN_E * hd:base + DN_E * hd + DN_E].reshape(DN_G, DN_C, DN_E) for hd in range(DN_H)]
    return jnp.concatenate(per_head, axis=0)


def _inst_rows(ref):
    rows = [jnp.broadcast_to(ref[:, DN_E * hd:DN_E * hd + DN_E][None], (DN_G, 1, DN_E)) for hd in range(DN_H)]
    return jnp.concatenate(rows, axis=0)


def _put_inst(ref, val, width=DN_E, base=0):
    for hd in range(DN_H):
        ref[:, base + width * hd:base + width * hd + width] = val[DN_G * hd:DN_G * hd + DN_G].reshape(DN_G * DN_C, width)


def _intra_args(qkv_ref, bb_ref, ab_ref, alog_ref, dtb_ref):
    return (_inst(qkv_ref), _inst(qkv_ref, DN_W), _inst(qkv_ref, 2 * DN_W), _inst(bb_ref), _inst(ab_ref),
            _inst_rows(alog_ref), _inst_rows(dtb_ref))


def _intra_in_specs():
    t = DN_G * DN_C
    return [pl.BlockSpec((t, 3 * DN_W), lambda n: (n, 0)),
            pl.BlockSpec((t, DN_W), lambda n: (n, R_BB // DN_W)),
            pl.BlockSpec((t, DN_W), lambda n: (n, R_AB // DN_W)),
            pl.BlockSpec((1, DN_W), lambda n: (0, 0)),
            pl.BlockSpec((1, DN_W), lambda n: (0, 0))]


def dn_intra_fwd(qkv, rest, alog_b, dtb_b, name, exch=None):
    t = DN_G * DN_C

    def body(qkv_ref, bb_ref, ab_ref, alog_ref, dtb_ref, u_ref, w_ref, qg_ref, kd_ref, in_ref, egl_ref):
        u, w, qg, kdec, intra, egl = _dn_intra(*_intra_args(qkv_ref, bb_ref, ab_ref, alog_ref, dtb_ref))
        _put_inst(u_ref, u)
        _put_inst(w_ref, w.astype(BF16))
        _put_inst(qg_ref, qg.astype(BF16))
        _put_inst(kd_ref, kdec.astype(BF16))
        _put_inst(in_ref, intra.astype(BF16), DN_C)
        for hd in range(DN_H):
            egl_ref[:, DN_E * hd:DN_E * hd + DN_E] = egl[DN_G * hd:DN_G * hd + DN_G].reshape(DN_G, DN_E)

    row = lambda w_: pl.BlockSpec((t, w_), lambda n: (n, 0))
    return _call(
        body, name=name, grid=(N_CHUNK // DN_G,), in_specs=_intra_in_specs(),
        out_specs=[row(DN_W), row(DN_W), row(DN_W), row(DN_W), row(DN_H * DN_C),
                   pl.BlockSpec((DN_G, DN_W), lambda n: (n, 0))],
        out_shape=[SDS((SEQ, DN_W), F32), SDS((SEQ, DN_W), BF16), SDS((SEQ, DN_W), BF16), SDS((SEQ, DN_W), BF16),
                   SDS((SEQ, DN_H * DN_C), BF16), SDS((N_CHUNK, DN_W), F32)],
        scratch_shapes=[], sem=("arbitrary",), args=(qkv, rest, rest, alog_b, dtb_b), exch=exch)


def dn_intra_bwd(qkv, rest, alog_b, dtb_b, du, dw, dqg, dkd, dintra, degl, name):
    t = DN_G * DN_C

    def body(qkv_ref, bb_ref, ab_ref, alog_ref, dtb_ref, du_ref, dw_ref, dqg_ref, dkd_ref, din_ref, degl_ref,
             dqkv_ref, dbb_ref, dab_ref, dalog_ref, ddtb_ref):
        @pl.when(pl.program_id(0) == 0)
        def _():
            dalog_ref[...] = jnp.zeros_like(dalog_ref)
            ddtb_ref[...] = jnp.zeros_like(ddtb_ref)

        _out, vjp = jax.vjp(_dn_intra, *_intra_args(qkv_ref, bb_ref, ab_ref, alog_ref, dtb_ref))
        d_in = jnp.concatenate([din_ref[:, DN_C * hd:DN_C * hd + DN_C].reshape(DN_G, DN_C, DN_C) for hd in range(DN_H)], axis=0)
        d_egl = jnp.concatenate([degl_ref[:, DN_E * hd:DN_E * hd + DN_E].reshape(DN_G, 1, DN_E) for hd in range(DN_H)], axis=0)
        dq, dk, dv, dbb, dab, dalog, ddtb = vjp((_inst(du_ref), _inst(dw_ref), _inst(dqg_ref), _inst(dkd_ref), d_in, d_egl))
        _put_inst(dqkv_ref, dq)
        _put_inst(dqkv_ref, dk, DN_E, DN_W)
        _put_inst(dqkv_ref, dv, DN_E, 2 * DN_W)
        _put_inst(dbb_ref, dbb)
        _put_inst(dab_ref, dab)
        for hd in range(DN_H):
            sl = slice(DN_E * hd, DN_E * hd + DN_E)
            dalog_ref[:, sl] += jnp.sum(dalog[DN_G * hd:DN_G * hd + DN_G], axis=0)
            ddtb_ref[:, sl] += jnp.sum(ddtb[DN_G * hd:DN_G * hd + DN_G], axis=0)

    row = lambda w_: pl.BlockSpec((t, w_), lambda n: (n, 0))
    acc = pl.BlockSpec((1, DN_W), lambda n: (0, 0))
    return pl.pallas_call(
        body, name=name, grid=(N_CHUNK // DN_G,),
        in_specs=_intra_in_specs() + [row(DN_W), row(DN_W), row(DN_W), row(DN_W), row(DN_H * DN_C),
                                      pl.BlockSpec((DN_G, DN_W), lambda n: (n, 0))],
        out_specs=[row(3 * DN_W), row(DN_W), row(DN_W), acc, acc],
        out_shape=[SDS((SEQ, 3 * DN_W), F32), SDS((SEQ, DN_W), F32), SDS((SEQ, DN_W), F32),
                   SDS((1, DN_W), F32), SDS((1, DN_W), F32)],
        compiler_params=_cparams(("arbitrary",)),
    )(qkv, rest, rest, alog_b, dtb_b, du, dw, dqg, dkd, dintra, degl)


def _inter_args(u_ref, w_ref, qg_ref, kd_ref, in_ref, egl_ref, n, state):
    f = lambda r: _heads(r).astype(F32)
    intra = jnp.stack([in_ref[:, DN_C * hd:DN_C * hd + DN_C] for hd in range(DN_H)], axis=0).astype(F32)
    egl = _heads(egl_ref.at[pl.ds(n & (DN_G - 1), 1), :])
    return f(u_ref), f(w_ref), f(qg_ref), f(kd_ref), intra, egl, state


def dn_inter_fwd(u, w, qg, kdec, intra, egl, name, exch=None):
    def body(u_ref, w_ref, qg_ref, kd_ref, in_ref, egl_ref, o_ref, st_ref, state_scr):
        n = pl.program_id(0)

        @pl.when(n == 0)
        def _():
            state_scr[...] = jnp.zeros_like(state_scr)

        st = state_scr[...]
        st_ref[0] = st
        o, ns = _dn_inter(*_inter_args(u_ref, w_ref, qg_ref, kd_ref, in_ref, egl_ref, n, st))
        _put_heads(o_ref, o)
        state_scr[...] = ns

    row = lambda w_: pl.BlockSpec((DN_C, w_), lambda n: (n, 0))
    return _call(
        body, name=name, grid=(N_CHUNK,),
        in_specs=[row(DN_W), row(DN_W), row(DN_W), row(DN_W), row(DN_H * DN_C),
                  pl.BlockSpec((DN_G, DN_W), lambda n: (n >> DN_G_LOG2, 0))],
        out_specs=[row(DN_W), pl.BlockSpec((1, DN_H, DN_E, DN_E), lambda n: (n, 0, 0, 0))],
        out_shape=[SDS((SEQ, DN_W), F32), SDS((N_CHUNK, DN_H, DN_E, DN_E), F32)],
        scratch_shapes=[pltpu.VMEM((DN_H, DN_E, DN_E), F32)],
        sem=("arbitrary",), args=(u, w, qg, kdec, intra, egl), exch=exch)


def dn_inter_bwd(u, w, qg, kdec, intra, egl, states, do, name):
    last = N_CHUNK - 1

    def body(u_ref, w_ref, qg_ref, kd_ref, in_ref, egl_ref, st_ref, do_ref,
             du_ref, dw_ref, dqg_ref, dkd_ref, din_ref, degl_ref, dstate_scr):
        s = pl.program_id(0)
        n = last - s

        @pl.when(s == 0)
        def _():
            dstate_scr[...] = jnp.zeros_like(dstate_scr)

        _out, vjp = jax.vjp(_dn_inter, *_inter_args(u_ref, w_ref, qg_ref, kd_ref, in_ref, egl_ref, n, st_ref[0]))
        du, dw, dqg, dkd, din, degl, dst = vjp((_heads(do_ref), dstate_scr[...]))
        _put_heads(du_ref, du)
        _put_heads(dw_ref, dw)
        _put_heads(dqg_ref, dqg)
        _put_heads(dkd_ref, dkd)
        for hd in range(DN_H):
            din_ref[:, DN_C * hd:DN_C * hd + DN_C] = din[hd]
        row = n & (DN_G - 1)

        @pl.when(row == DN_G - 1)
        def _():
            degl_ref[...] = jnp.zeros_like(degl_ref)

        new_row = jnp.concatenate([degl[hd] for hd in range(DN_H)], axis=1)
        rows = lax.broadcasted_iota(jnp.int32, (DN_G, DN_W), 0)
        degl_ref[...] = jnp.where(rows == row, jnp.broadcast_to(new_row, (DN_G, DN_W)), degl_ref[...])
        dstate_scr[...] = dst

    rev = lambda w_: pl.BlockSpec((DN_C, w_), lambda s: (last - s, 0))
    grp = pl.BlockSpec((DN_G, DN_W), lambda s: ((last - s) >> DN_G_LOG2, 0))
    return pl.pallas_call(
        body, name=name, grid=(N_CHUNK,),
        in_specs=[rev(DN_W), rev(DN_W), rev(DN_W), rev(DN_W), rev(DN_H * DN_C), grp,
                  pl.BlockSpec((1, DN_H, DN_E, DN_E), lambda s: (last - s, 0, 0, 0)), rev(DN_W)],
        out_specs=[rev(DN_W), rev(DN_W), rev(DN_W), rev(DN_W), rev(DN_H * DN_C), grp],
        out_shape=[SDS((SEQ, DN_W), F32)] * 4 + [SDS((SEQ, DN_H * DN_C), F32), SDS((N_CHUNK, DN_W), F32)],
        scratch_shapes=[pltpu.VMEM((DN_H, DN_E, DN_E), F32)],
        compiler_params=_cparams(("arbitrary",)),
    )(u, w, qg, kdec, intra, egl, states, do)


OUT_T = 256


def _pool_consts(rows_total, t0, halo_before):
    lane = lax.broadcasted_iota(jnp.int32, (rows_total, POOL_W), 1)
    row = lax.broadcasted_iota(jnp.int32, (rows_total, POOL_W), 0)
    grp = (lane >= 64).astype(jnp.int32) + (lane >= 128).astype(jnp.int32) + (lane >= 192).astype(jnp.int32)
    win = jnp.where(grp == 0, 2, jnp.where(grp == 1, 4, jnp.where(grp == 2, 8, 16)))
    pos = t0 + row - halo_before
    cnt = jnp.minimum(pos + 1, win).astype(F32)
    return grp, cnt


def _pool_select(grp, s2, s4, s8, s16):
    return jnp.where(grp == 0, s2, jnp.where(grp == 1, s4, jnp.where(grp == 2, s8, s16)))


def _pooled(u_ext, t0):
    n = u_ext.shape[0]
    grp, cnt = _pool_consts(n, t0, POOL_HALO)
    s2 = u_ext + pltpu.roll(u_ext, 1, 0)
    s4 = s2 + pltpu.roll(s2, 2, 0)
    s8 = s4 + pltpu.roll(s4, 4, 0)
    s16 = s8 + pltpu.roll(s8, 8, 0)
    out = _pool_select(grp, s2, s4, s8, s16) / jnp.maximum(cnt, 1.0) - u_ext
    return out[POOL_HALO:, :]


def _merge_weights(l1, l4, l16):
    m = jnp.maximum(jnp.maximum(l1, l4), l16)
    e1 = jnp.exp(l1 - m)
    e4 = jnp.exp(l4 - m)
    e16 = jnp.exp(l16 - m)
    inv = 1.0 / (e1 + e4 + e16)
    return e1 * inv, e4 * inv, e16 * inv


def _out_parts(ol1_ref, ol4_ref, ol16_ref, pu_ref, puh_ref, odn_ref, z_ref, wbd_ref, i, t):
    w1, w4, w16 = _merge_weights(_slabs_load(ol1_ref, 2, 2), _slabs_load(ol4_ref, 2, 2), _slabs_load(ol16_ref, 2, 2))
    ya = w1 * _slabs_load(ol1_ref, 0, 2) + w4 * _slabs_load(ol4_ref, 0, 2) + w16 * _slabs_load(ol16_ref, 0, 2)
    halo = jnp.where(i > 0, puh_ref[...], 0.0)
    pooled = _pooled(jnp.concatenate([halo, pu_ref[...]], axis=0), i * t)
    pw = _dot(pooled.astype(BF16), wbd_ref[...])
    return ya, pooled, pw, (w1, w4, w16)


def _out_specs_common(t):
    def row(w, cb=0):
        return pl.BlockSpec((t, w), lambda i: (i, cb))

    halo = pl.BlockSpec((POOL_HALO, POOL_W),
                        lambda i: (jnp.maximum(i * (t // POOL_HALO) - 1, 0), R_PU // POOL_W))
    full = lambda a, b: pl.BlockSpec((a, b), lambda i: (0, 0))
    return [_slab_spec(4, t), _slab_spec(4, t), _slab_spec(4, t), row(POOL_W, R_PU // POOL_W), halo, row(DN_W), row(DN_W, R_DZ // DN_W),
            full(POOL_W, POOL_W), full(1, POOL_W), full(1, DN_W), full(D_MODEL, D_MODEL)]


def mix_out_fwd(x, ol1, ol4, ol16, rest, odn, wbd, scale, onorm_b, wout, name):
    t = OUT_T

    def body(x_ref, ol1_ref, ol4_ref, ol16_ref, pu_ref, puh_ref, odn_ref, z_ref, wbd_ref, sc_ref, on_ref, wo_ref, o_ref):
        i = pl.program_id(0)
        ya, _pooled_v, pw, _w = _out_parts(ol1_ref, ol4_ref, ol16_ref, pu_ref, puh_ref, odn_ref, z_ref, wbd_ref, i, t)
        yb = pw * sc_ref[...]
        acc = x_ref[...] + _dot(ya.astype(BF16), wo_ref[0:256, :]) + _dot(yb.astype(BF16), wo_ref[256:512, :])
        for hd in range(DN_H):
            sl = slice(DN_E * hd, DN_E * hd + DN_E)
            oh, _r = _rms_stats(odn_ref[:, sl])
            z = z_ref[:, sl]
            yc = oh * on_ref[:, sl] * (z * _sigmoid(z))
            acc = acc + _dot(yc.astype(BF16), wo_ref[512 + DN_E * hd:512 + DN_E * hd + DN_E, :])
        o_ref[...] = acc

    return pl.pallas_call(
        body, name=name, grid=(SEQ // t,),
        in_specs=[pl.BlockSpec((t, D_MODEL), lambda i: (i, 0))] + _out_specs_common(t),
        out_specs=pl.BlockSpec((t, D_MODEL), lambda i: (i, 0)),
        out_shape=SDS((SEQ, D_MODEL), F32),
        compiler_params=_cparams(("arbitrary",)),
    )(x, ol1, ol4, ol16, rest, rest, odn, rest, wbd, scale, onorm_b, wout)


def mix_out_bwd(dxo, ol1, ol4, ol16, rest, odn, wbd, scale, onorm_b, wout, headsum, name):
    t = OUT_T

    def body(dxo_ref, ol1_ref, ol4_ref, ol16_ref, pu_ref, puh_ref, odn_ref, z_ref, wbd_ref, sc_ref, on_ref, wo_ref, hs_ref,
             dwo_ref, d1_ref, d4_ref, d16_ref, dpl_ref, dodn_ref, dz_ref, dsc_ref, don_ref, dwbd_ref):
        i = pl.program_id(0)

        @pl.when(i == 0)
        def _():
            dwo_ref[...] = jnp.zeros_like(dwo_ref)
            dsc_ref[...] = jnp.zeros_like(dsc_ref)
            don_ref[...] = jnp.zeros_like(don_ref)
            dwbd_ref[...] = jnp.zeros_like(dwbd_ref)

        ya, pooled, pw, (w1, w4, w16) = _out_parts(ol1_ref, ol4_ref, ol16_ref, pu_ref, puh_ref, odn_ref, z_ref, wbd_ref, i, t)
        sc = sc_ref[...]
        dxb = dxo_ref[...].astype(BF16)
        dwo_ref[0:256, :] += _dot_tn(ya.astype(BF16), dxb)
        dwo_ref[256:512, :] += _dot_tn((pw * sc).astype(BF16), dxb)
        dya = _dot_nt(dxb, wo_ref[0:256, :])
        o1 = _slabs_load(ol1_ref, 0, 2)
        o4 = _slabs_load(ol4_ref, 0, 2)
        o16 = _slabs_load(ol16_ref, 0, 2)
        hs = hs_ref[...]
        s1 = _dot(dya * o1, hs, HI)
        s4 = _dot(dya * o4, hs, HI)
        s16 = _dot(dya * o16, hs, HI)
        sbar = w1 * s1 + w4 * s4 + w16 * s16
        _slabs_store(d1_ref, 0, w1 * dya)
        _slabs_store(d1_ref, 2, w1 * (s1 - sbar))
        _slabs_store(d4_ref, 0, w4 * dya)
        _slabs_store(d4_ref, 2, w4 * (s4 - sbar))
        _slabs_store(d16_ref, 0, w16 * dya)
        _slabs_store(d16_ref, 2, w16 * (s16 - sbar))
        dyb = _dot_nt(dxb, wo_ref[256:512, :])
        dsc_ref[...] += jnp.sum(dyb * pw, axis=0, keepdims=True)
        dpw = (dyb * sc).astype(BF16)
        dwbd_ref[...] += _dot_tn(pooled.astype(BF16), dpw)
        dpl_ref[...] = _dot_nt(dpw, wbd_ref[...])
        for hd in range(DN_H):
            sl = slice(DN_E * hd, DN_E * hd + DN_E)
            rows_w = slice(512 + DN_E * hd, 512 + DN_E * hd + DN_E)
            oh, r = _rms_stats(odn_ref[:, sl])
            z = z_ref[:, sl]
            sg = _sigmoid(z)
            sz = z * sg
            nw = on_ref[:, sl]
            on = oh * nw
            dwo_ref[rows_w, :] += _dot_tn((on * sz).astype(BF16), dxb)
            dyc = _dot_nt(dxb, wo_ref[rows_w, :])
            dz_ref[:, sl] = dyc * on * (sg * (1.0 + z * (1.0 - sg)))
            dx, dw = _rms_bwd(oh, r, nw, dyc * sz)
            dodn_ref[:, sl] = dx
            don_ref[:, sl] += dw

    row = lambda w: pl.BlockSpec((t, w), lambda i: (i, 0))
    full = lambda a, b: pl.BlockSpec((a, b), lambda i: (0, 0))
    return pl.pallas_call(
        body, name=name, grid=(SEQ // t,),
        in_specs=[row(D_MODEL)] + _out_specs_common(t) + [full(ATT_W, ATT_W)],
        out_specs=[full(D_MODEL, D_MODEL), _slab_spec(4, t), _slab_spec(4, t), _slab_spec(4, t), row(POOL_W), row(DN_W), row(DN_W),
                   full(1, POOL_W), full(1, DN_W), full(POOL_W, POOL_W)],
        out_shape=[SDS((D_MODEL, D_MODEL), F32), SDS((4, SEQ, 128), F32), SDS((4, SEQ, 128), F32), SDS((4, SEQ, 128), F32),
                   SDS((SEQ, POOL_W), F32), SDS((SEQ, DN_W), F32), SDS((SEQ, DN_W), F32),
                   SDS((1, POOL_W), F32), SDS((1, DN_W), F32), SDS((POOL_W, POOL_W), F32)],
        compiler_params=_cparams(("arbitrary",)),
    )(dxo, ol1, ol4, ol16, rest, rest, odn, rest, wbd, scale, onorm_b, wout, headsum)


def pool_bwd(dpooled, name):
    t = 512
    nt = SEQ // t

    def body(d_ref, dn_ref, o_ref):
        i = pl.program_id(0)
        halo = jnp.where(i < nt - 1, dn_ref[...], 0.0)
        d_ext = jnp.concatenate([d_ref[...], halo], axis=0)
        n = t + POOL_HALO
        grp, cnt = _pool_consts(n, i * t, 0)
        dq = d_ext / cnt
        s2 = dq + pltpu.roll(dq, n - 1, 0)
        s4 = s2 + pltpu.roll(s2, n - 2, 0)
        s8 = s4 + pltpu.roll(s4, n - 4, 0)
        s16 = s8 + pltpu.roll(s8, n - 8, 0)
        o_ref[...] = (_pool_select(grp, s2, s4, s8, s16) - d_ext)[0:t, :]

    return pl.pallas_call(
        body, name=name, grid=(nt,),
        in_specs=[pl.BlockSpec((t, POOL_W), lambda i: (i, 0)),
                  pl.BlockSpec((POOL_HALO, POOL_W),
                               lambda i: (jnp.minimum((i + 1) * (t // POOL_HALO), SEQ // POOL_HALO - 1), 0))],
        out_specs=pl.BlockSpec((t, POOL_W), lambda i: (i, 0)),
        out_shape=SDS((SEQ, POOL_W), F32),
        compiler_params=_cparams(("arbitrary",)),
    )(dpooled, dpooled)


N_PEER = N_DEV - 1
ANY_SPEC = pl.BlockSpec(memory_space=pl.ANY)


class Exchange:
    def __init__(self, arrays, mode):
        self.arrays = list(arrays)
        self.mode = mode
        n = len(self.arrays)
        if mode == "scatter":
            self.out_shape = [SDS(a.shape, a.dtype) for a in self.arrays]
        else:
            self.out_shape = [SDS((N_DEV,) + a.shape, a.dtype) for a in self.arrays]
        self.scratch = [pltpu.SemaphoreType.DMA((n * N_PEER,)), pltpu.SemaphoreType.DMA((n * N_PEER,)),
                        pltpu.SemaphoreType.DMA((n,))]

    @staticmethod
    def _place():
        x, y, c = lax.axis_index("x"), lax.axis_index("y"), lax.axis_index("c")
        chips = [(1 - x, y), (x, 1 - y), (1 - x, 1 - y)]
        return x, y, c, chips

    @staticmethod
    def _copy(sems, a, k, src, dst, to):
        send_sems, recv_sems, _ = sems
        return pltpu.make_async_remote_copy(
            src_ref=src, dst_ref=dst, send_sem=send_sems.at[a * N_PEER + k], recv_sem=recv_sems.at[a * N_PEER + k],
            device_id=to, device_id_type=MESH)

    def _scatter_peers(self):
        x, y, c, _ = self._place()
        out = []
        for fx, fy, fc in ((0, 0, 1), (1, 0, 0), (0, 1, 0), (1, 1, 0), (1, 0, 1), (0, 1, 1), (1, 1, 1)):
            px, py, pc = x ^ fx, y ^ fy, c ^ fc
            out.append(((px, py, pc), 4 * px + 2 * py + pc))
        return 4 * x + 2 * y + c, out

    def _local(self, ins, outs, sems, a, me):
        src = ins[a].at[me] if self.mode == "scatter" else ins[a]
        return pltpu.make_async_copy(src, outs[a].at[me], sems[2].at[a])

    def start(self, ins, outs, sems):
        if self.mode == "scatter":
            me, peers = self._scatter_peers()
            for a in range(len(ins)):
                self._local(ins, outs, sems, a, me).start()
                for k, (peer, pidx) in enumerate(peers):
                    self._copy(sems, a, k, ins[a].at[pidx], outs[a].at[me], peer).start()
            return
        x, y, c, chips = self._place()
        me = 4 * x + 2 * y + c
        for a in range(len(ins)):
            self._local(ins, outs, sems, a, me).start()
            self._copy(sems, a, 0, ins[a], outs[a].at[me], (x, y, 1 - c)).start()
            for j, (cx, cy) in enumerate(chips):
                self._copy(sems, a, 1 + j, ins[a], outs[a].at[me], (cx, cy, c)).start()

    def finish(self, ins, outs, sems):
        n = len(ins)
        if self.mode == "scatter":
            me, peers = self._scatter_peers()
            for a in range(n):
                for k, (peer, pidx) in enumerate(peers):
                    self._copy(sems, a, k, ins[a].at[pidx], outs[a].at[pidx], peer).wait_recv()
            for a in range(n):
                for k, (peer, pidx) in enumerate(peers):
                    self._copy(sems, a, k, ins[a].at[pidx], outs[a].at[me], peer).wait_send()
                self._local(ins, outs, sems, a, me).wait()
            return
        x, y, c, chips = self._place()
        me = 4 * x + 2 * y + c
        sib = (x, y, 1 - c)
        for a in range(n):
            for j, (cx, cy) in enumerate(chips):
                blk = outs[a].at[4 * cx + 2 * cy + c]
                self._copy(sems, a, 1 + j, ins[a], blk, (cx, cy, c)).wait_recv()
                self._copy(sems, a, 4 + j, blk, blk, sib).start()
        for a in range(n):
            self._copy(sems, a, 0, ins[a], outs[a].at[4 * x + 2 * y + (1 - c)], sib).wait_recv()
            for j, (cx, cy) in enumerate(chips):
                blk = outs[a].at[4 * cx + 2 * cy + (1 - c)]
                self._copy(sems, a, 4 + j, blk, blk, sib).wait_recv()
        for a in range(n):
            for k in range(N_PEER):
                self._copy(sems, a, k, ins[a], outs[a].at[me], sib).wait_send()
            self._local(ins, outs, sems, a, me).wait()


def run_exchange(exch, name):
    n = len(exch.arrays)

    def body(*refs):
        ins, outs, sems = refs[:n], refs[n:2 * n], refs[2 * n:]
        exch.start(ins, outs, sems)
        exch.finish(ins, outs, sems)

    return pl.pallas_call(
        body, name=name, in_specs=[ANY_SPEC] * n, out_specs=[ANY_SPEC] * n, out_shape=exch.out_shape,
        scratch_shapes=exch.scratch,
    )(*exch.arrays)


def _call(body, *, name, grid, in_specs, out_specs, out_shape, scratch_shapes, sem, args, exch=None):
    if exch is None:
        res = pl.pallas_call(body, name=name, grid=grid, in_specs=in_specs, out_specs=out_specs, out_shape=out_shape,
                             scratch_shapes=scratch_shapes, compiler_params=_cparams(sem))(*args)
        return res, None
    single = not isinstance(out_shape, (list, tuple))
    out_specs_l = [out_specs] if single else list(out_specs)
    out_shape_l = [out_shape] if single else list(out_shape)
    n_in, n_out, n_scr, m = len(in_specs), len(out_specs_l), len(scratch_shapes), len(exch.arrays)

    def wrapped(*refs):
        p = 0
        ins = refs[p:p + n_in]; p += n_in
        xin = refs[p:p + m]; p += m
        outs = refs[p:p + n_out]; p += n_out
        xout = refs[p:p + m]; p += m
        scr = refs[p:p + n_scr]; p += n_scr
        sems = refs[p:]
        ids = [pl.program_id(ax) for ax in range(len(grid))]
        first = functools.reduce(jnp.logical_and, [i == 0 for i in ids])
        last = functools.reduce(jnp.logical_and, [i == g - 1 for i, g in zip(ids, grid)])

        @pl.when(first)
        def _():
            exch.start(xin, xout, sems)

        body(*ins, *outs, *scr)

        @pl.when(last)
        def _():
            exch.finish(xin, xout, sems)

    res = pl.pallas_call(
        wrapped, name=name, grid=grid, in_specs=list(in_specs) + [ANY_SPEC] * m,
        out_specs=out_specs_l + [ANY_SPEC] * m, out_shape=out_shape_l + exch.out_shape,
        scratch_shapes=list(scratch_shapes) + exch.scratch, compiler_params=_cparams(sem),
    )(*args, *exch.arrays)
    outs = res[:n_out]
    return (outs[0] if single else outs), res[n_out:]


def _adam_math(w, g, m, v):
    m2 = ADAM_B1 * m + (1.0 - ADAM_B1) * g
    v2 = ADAM_B2 * v + (1.0 - ADAM_B2) * (g * g)
    m_hat = m2 / (1.0 - ADAM_B1 ** ADAM_STEP)
    v_hat = v2 / (1.0 - ADAM_B2 ** ADAM_STEP)
    delta = -ADAM_LR * (m_hat / (jnp.sqrt(v_hat) + ADAM_EPS) + ADAM_WD * w)
    return delta, m2, v2


ADAM_ROW_BLOCKS = 2


def adam_shard(parts0, parts1, w, m, v, name):
    _, r, c = w.shape
    rb = r // ADAM_ROW_BLOCKS

    def body(p0_ref, p1_ref, w_ref, m_ref, v_ref, g_ref, d_ref, m2_ref, v2_ref):
        def run(p_ref):
            g = p_ref[0].astype(F32)
            for i in range(1, N_DEV):
                g = g + p_ref[i].astype(F32)
            delta, m2, v2 = _adam_math(w_ref[0], g, m_ref[0], v_ref[0])
            g_ref[0] = g
            d_ref[0] = delta
            m2_ref[0] = m2
            v2_ref[0] = v2

        @pl.when(pl.program_id(0) == 0)
        def _():
            run(p0_ref)

        @pl.when(pl.program_id(0) == 1)
        def _():
            run(p1_ref)

    def p_spec(layer):
        row = (lambda l, j: jnp.where(l == 0, j, ADAM_ROW_BLOCKS - 1)) if layer == 0 else (lambda l, j: jnp.where(l == 1, j, 0))
        return pl.BlockSpec((N_DEV, rb, c), lambda l, j: (0, row(l, j), 0))

    blk = pl.BlockSpec((1, rb, c), lambda l, j: (l, j, 0))
    return pl.pallas_call(
        body, name=name, grid=(DEPTH, ADAM_ROW_BLOCKS),
        in_specs=[p_spec(0), p_spec(1), blk, blk, blk], out_specs=[blk] * 4,
        out_shape=[SDS(w.shape, F32)] * 4,
        compiler_params=_cparams(("arbitrary", "arbitrary")),
    )(parts0, parts1, w, m, v)


def parts_sum(parts0, parts1, name):
    _, r, c = parts0.shape

    def body(p0_ref, p1_ref, g_ref):
        def run(p_ref):
            g = p_ref[0].astype(F32)
            for i in range(1, N_DEV):
                g = g + p_ref[i].astype(F32)
            g_ref[0] = g

        @pl.when(pl.program_id(0) == 0)
        def _():
            run(p0_ref)

        @pl.when(pl.program_id(0) == 1)
        def _():
            run(p1_ref)

    full = pl.BlockSpec((N_DEV, r, c), lambda l: (0, 0, 0))
    return pl.pallas_call(
        body, name=name, grid=(DEPTH,), in_specs=[full, full],
        out_specs=pl.BlockSpec((1, r, c), lambda l: (l, 0, 0)), out_shape=SDS((DEPTH, r, c), F32),
        compiler_params=_cparams(("arbitrary",)),
    )(parts0, parts1)


def adam_given(g, w, m, v, name):
    _, r, c = w.shape

    def body(g_ref, w_ref, m_ref, v_ref, d_ref, m2_ref, v2_ref):
        delta, m2, v2 = _adam_math(w_ref[0], g_ref[0], m_ref[0], v_ref[0])
        d_ref[0] = delta
        m2_ref[0] = m2
        v2_ref[0] = v2

    blk = pl.BlockSpec((1, r, c), lambda l: (l, 0, 0))
    return pl.pallas_call(
        body, name=name, grid=(DEPTH,), in_specs=[blk] * 4, out_specs=[blk] * 3, out_shape=[SDS(w.shape, F32)] * 3,
        compiler_params=_cparams(("arbitrary",)),
    )(g, w, m, v)


def adam_small(parts, w, m, v, name):
    def body(p_ref, w_ref, m_ref, v_ref, g_ref, d_ref, m2_ref, v2_ref):
        g = p_ref[0]
        for i in range(1, N_DEV):
            g = g + p_ref[i]
        delta, m2, v2 = _adam_math(w_ref[...], g, m_ref[...], v_ref[...])
        g_ref[...] = g
        d_ref[...] = delta
        m2_ref[...] = m2
        v2_ref[...] = v2

    return pl.pallas_call(
        body, name=name, out_shape=[SDS(w.shape, F32)] * 4, compiler_params=_cparams(),
    )(parts, w, m, v)


def _rot_cols(w):
    w4 = w.reshape(w.shape[0], 4, 2, 32)
    return jnp.stack([-w4[:, :, 1], w4[:, :, 0]], axis=2).reshape(w.shape[0], ATT_W)


def _rot_cols_t(dw_rot):
    d4 = dw_rot.reshape(dw_rot.shape[0], 4, 2, 32)
    return jnp.stack([d4[:, :, 1], -d4[:, :, 0]], axis=2).reshape(dw_rot.shape[0], ATT_W)


def build_wext(w_in):
    aq, ak, av, pu = w_in[:, 0:256], w_in[:, 256:512], w_in[:, 512:768], w_in[:, 768:1024]
    dqkvz = w_in[:, 1024:3072]
    gates = jnp.repeat(w_in[:, 3072:3080], DN_E, axis=1)
    return jnp.concatenate([aq, ak, av, _rot_cols(aq), _rot_cols(ak), dqkvz, gates, pu], axis=1)


def fold_dwext(d):
    b = EXT_ATT
    aq = d[:, 0:256] + _rot_cols_t(d[:, 768:1024])
    ak = d[:, 256:512] + _rot_cols_t(d[:, 1024:1280])
    av = d[:, 512:768]
    dqkvz = d[:, b:b + 2048]
    gates = d[:, b + R_BB:b + R_BB + 1024].reshape(d.shape[0], 8, DN_E).sum(axis=-1)
    pu = d[:, b + R_PU:b + R_PU + 256]
    return jnp.concatenate([aq, ak, av, pu, dqkvz, gates], axis=1)


def _block_diag(pw):
    z = jnp.zeros((4, 64, 4, 64), pw.dtype)
    for g in range(4):
        z = z.at[g, :, g, :].set(pw[g])
    return z.reshape(POOL_W, POOL_W)


def _diag_blocks(m):
    m4 = m.reshape(4, 64, 4, 64)
    return jnp.stack([m4[g, :, g, :] for g in range(4)], axis=0)


def _lanes(v, reps):
    return jnp.repeat(v, reps)[None, :]


def layer_fwd(p, xa, cos, sin, l, host=None):
    host = host or {}

    def carried(key):
        return host[key][0] if key in host else None

    def done(key, xo):
        if key in host:
            host[key][1](xo)

    xb, xo = ffn_fwd(xa, p["n1"], *p["f1"], f"ffn1_fwd_{l}", carried("ffn1"))
    done("ffn1", xo)
    att, rest = mix_in_fwd(xb, p["nm"], p["wext"], cos, sin, f"mix_in_fwd_{l}")
    ols = [att_fwd_s(att, d, f"att_fwd_{l}_{d}") for d in DILATIONS]
    qkv = dn_prep_fwd(rest, p["conv"], f"dn_prep_fwd_{l}")
    dn, xo = dn_intra_fwd(qkv, rest, p["alog"], p["dtb"], f"dn_intra_fwd_{l}", carried("dn_intra"))
    done("dn_intra", xo)
    (odn, states), xo = dn_inter_fwd(*dn, f"dn_inter_fwd_{l}", carried("dn_inter"))
    done("dn_inter", xo)
    xc = mix_out_fwd(xb, ols[0], ols[1], ols[2], rest, odn, p["wbd"], p["scale"], p["onorm"], p["wout"], f"mix_out_fwd_{l}")
    xd, xo = ffn_fwd(xc, p["n2"], *p["f2"], f"ffn2_fwd_{l}", carried("ffn2"))
    done("ffn2", xo)
    return xd, dict(xa=xa, xb=xb, xc=xc, att=att, rest=rest, ols=ols, qkv=qkv, dn=dn, odn=odn, states=states)


def layer_bwd(p, s, dx, cos, sin, headsum, l, scatter=False, carry=None):
    blocks = lambda ws: [w_.reshape(N_DEV, FF_BLK, D_MODEL) for w_ in ws]
    (dx, *d_f2, d_n2), carried = ffn_bwd(s["xc"], dx, p["n2"], *p["f2"], f"ffn2_bwd_{l}", carry)
    (d_wout, dol1, dol4, dol16, dpooled, dodn, dz, dscale, donorm, dwbd) = mix_out_bwd(
        dx, s["ols"][0], s["ols"][1], s["ols"][2], s["rest"], s["odn"], p["wbd"], p["scale"], p["onorm"], p["wout"],
        headsum, f"mix_out_bwd_{l}")
    dpu = pool_bwd(dpooled, f"pool_bwd_{l}")
    f2 = blocks(d_f2)
    d_dn = dn_inter_bwd(*s["dn"], s["states"], dodn, f"dn_inter_bwd_{l}")
    dqkv, dbb, dab, dalog, ddtb = dn_intra_bwd(s["qkv"], s["rest"], p["alog"], p["dtb"], *d_dn, f"dn_intra_bwd_{l}")
    d_dqkv, dconv = dn_prep_bwd(s["rest"], p["conv"], dqkv, f"dn_prep_bwd_{l}")
    datts = [att_bwd_s(s["att"], ol, dol, d, f"att_bwd_{l}_{d}")
             for d, ol, dol in zip(DILATIONS, s["ols"], (dol1, dol4, dol16))]
    dproj = assemble_dproj(datts, cos, sin, d_dqkv, dz, dbb, dab, dpu, f"assemble_dproj_{l}")
    dx, d_wext, d_nm = linear_bwd(s["xb"], dx, p["nm"], dproj, p["wext"], f"mix_in_bwd_{l}")
    d_win = fold_dwext(d_wext).reshape(D_MODEL, N_DEV, IN_BLK).transpose(1, 0, 2).astype(BF16)
    io = [d_win, d_wout.reshape(N_DEV, D_MODEL // N_DEV, D_MODEL).astype(BF16)]
    (dx, *d_f1, d_n1), xo = ffn_bwd(s["xa"], dx, p["n1"], *p["f1"], f"ffn1_bwd_{l}",
                                    Exchange(f2 + io, "scatter") if scatter else None)
    if scatter:
        f2, io = list(xo[:3]), list(xo[3:])
    big = dict(f1=blocks(d_f1), f2=f2, io=io)
    small = dict(ffn1_norm=d_n1[0], mix_norm=d_nm[0], ffn2_norm=d_n2[0], pool_w=_diag_blocks(dwbd),
                 pool_scale=dscale[0], dn_a_log=dalog.reshape(DN_H, DN_E).sum(-1),
                 dn_dt_bias=ddtb.reshape(DN_H, DN_E).sum(-1),
                 dn_out_norm=donorm.reshape(DN_H, DN_E).sum(0), dn_conv_w=dconv)
    return dx, big, small, carried


def small_operands(l, pool_w, pool_scale, dn_out_norm, dn_a_log, dn_dt_bias, ffn1_norm, mix_norm, ffn2_norm):
    return dict(
        wbd=_block_diag(pool_w[l]).astype(BF16),
        scale=pool_scale[l][None, :],
        onorm=jnp.tile(dn_out_norm[l], DN_H)[None, :],
        alog=_lanes(dn_a_log[l], DN_E),
        dtb=_lanes(dn_dt_bias[l], DN_E),
        n1=ffn1_norm[l][None, :], nm=mix_norm[l][None, :], n2=ffn2_norm[l][None, :])


def set_mixer_weights(p, win_g, wout_g, conv_g):
    p["wext"] = build_wext(win_g.transpose(1, 0, 2).reshape(D_MODEL, IN_W))
    p["wout"] = wout_g.reshape(D_MODEL, D_MODEL)
    p["conv"] = conv_g.transpose(1, 0, 2).reshape(DN_CONV, 3 * DN_W)


def rope_tables(pos):
    inv_freq = 10000.0 ** (-jnp.arange(0, ATT_E, 2, dtype=F32) / ATT_E)
    ang = pos.astype(F32)[:, None] * inv_freq
    return jnp.tile(jnp.cos(ang), (1, 8)), jnp.tile(jnp.sin(ang), (1, 8))


def head_sum_matrix():
    return jnp.kron(jnp.eye(4, dtype=F32), jnp.ones((ATT_E, ATT_E), F32))


SMALL_NAMES = ("ffn1_norm", "mix_norm", "ffn2_norm", "pool_w", "pool_scale", "dn_a_log", "dn_dt_bias",
               "dn_out_norm", "final_norm", "dn_conv_w")


PACK_UNIT = 8 * 128


def _pack_rows(n):
    return -(-n // PACK_UNIT) * 8


def _pack(parts):
    rows = []
    for p in parts:
        flat = p.reshape(-1)
        r = _pack_rows(flat.shape[0])
        rows.append(jnp.pad(flat, (0, r * 128 - flat.shape[0])).reshape(r, 128))
    return jnp.concatenate(rows, axis=0)


def _unpack(packed, shapes):
    out, row = [], 0
    for s in shapes:
        n = math.prod(s)
        r = _pack_rows(n)
        out.append(packed[row:row + r].reshape(-1)[:n].reshape(s))
        row += r
    return out


def kernel(x, positions, ffn1_norm, ffn1_w_gate, ffn1_w_up, ffn1_w_down, mix_norm, w_in, pool_w, pool_scale, dn_conv_w, dn_a_log, dn_dt_bias, dn_out_norm, w_out, ffn2_norm, ffn2_w_gate, ffn2_w_up, ffn2_w_down, final_norm, loss_target, m_ffn1_norm, m_ffn1_w_gate, m_ffn1_w_up, m_ffn1_w_down, m_mix_norm, m_w_in, m_pool_w, m_pool_scale, m_dn_conv_w, m_dn_a_log, m_dn_dt_bias, m_dn_out_norm, m_w_out, m_ffn2_norm, m_ffn2_w_gate, m_ffn2_w_up, m_ffn2_w_down, m_final_norm, v_ffn1_norm, v_ffn1_w_gate, v_ffn1_w_up, v_ffn1_w_down, v_mix_norm, v_w_in, v_pool_w, v_pool_scale, v_dn_conv_w, v_dn_a_log, v_dn_dt_bias, v_dn_out_norm, v_w_out, v_ffn2_norm, v_ffn2_w_gate, v_ffn2_w_up, v_ffn2_w_down, v_final_norm):
    me = 4 * lax.axis_index("x") + 2 * lax.axis_index("y") + lax.axis_index("c")
    x0 = x[0]
    target = loss_target[0]

    cos, sin = rope_tables(positions[0])
    headsum = head_sum_matrix()

    layers = [small_operands(l, pool_w, pool_scale, dn_out_norm, dn_a_log, dn_dt_bias, ffn1_norm, mix_norm, ffn2_norm)
              for l in range(DEPTH)]

    def whole(gathered):
        return gathered.reshape(D_FF, D_MODEL)

    def gather_ffn1(l):
        def on_done(xo):
            layers[l]["f1"] = tuple(whole(g) for g in xo)
        return Exchange(ffn_shard_operands(ffn1_w_gate[l], ffn1_w_up[l], ffn1_w_down[l]), "gather"), on_done

    def gather_mixer(l):
        def on_done(xo):
            set_mixer_weights(layers[l], *xo)
        return Exchange([w_in[l].astype(BF16), w_out[l].astype(BF16), dn_conv_w[l]], "gather"), on_done

    gathered_f2 = {}

    def gather_ffn2_part(l, part):
        def on_done(xo):
            gathered_f2[(l, part)] = [whole(g) for g in xo]
            if (l, 0) in gathered_f2 and (l, 1) in gathered_f2:
                layers[l]["f2"] = tuple(gathered_f2[(l, 0)] + gathered_f2[(l, 1)])
        ops = ffn_shard_operands(ffn2_w_gate[l], ffn2_w_up[l], ffn2_w_down[l])
        return Exchange(ops[:2] if part == 0 else ops[2:], "gather"), on_done

    first, on_first = gather_ffn1(0)
    on_first(run_exchange(first, "gather_ffn1_0"))
    saved = []
    xa = x0
    for l in range(DEPTH):
        host = {"ffn1": gather_mixer(l), "dn_intra": gather_ffn2_part(l, 0), "dn_inter": gather_ffn2_part(l, 1)}
        if l + 1 < DEPTH:
            host["ffn2"] = gather_ffn1(l + 1)
        xa, s = layer_fwd(layers[l], xa, cos, sin, l, host)
        saved.append(s)

    loss_row, dx, d_final = loss_head(xa, final_norm[None, :], target, "loss_head")
    loss = lax.psum(loss_row[0, 0], ("x", "y", "c"))

    small = {}
    big_parts = [None] * DEPTH
    carry = None
    for l in reversed(range(DEPTH)):
        dx, big, small[l], carried = layer_bwd(layers[l], saved[l], dx, cos, sin, headsum, l, True, carry)
        if carried is not None:
            big_parts[l + 1]["f1"] = list(carried)
        big_parts[l] = big
        carry = Exchange(big["f1"], "scatter")
    big_parts[0]["f1"] = list(run_exchange(carry, "scatter_ffn1_0"))
    grad_x = dx[None]

    small_shapes = {"ffn1_norm": (DEPTH, D_MODEL), "mix_norm": (DEPTH, D_MODEL), "ffn2_norm": (DEPTH, D_MODEL),
                    "pool_w": (DEPTH, 4, 64, 64), "pool_scale": (DEPTH, POOL_W), "dn_a_log": (DEPTH, DN_H),
                    "dn_dt_bias": (DEPTH, DN_H), "dn_out_norm": (DEPTH, DN_E), "final_norm": (D_MODEL,),
                    "dn_conv_w": (DEPTH, DN_CONV, 3 * DN_W)}
    g_small = {n: (d_final[0] if n == "final_norm" else jnp.stack([small[l][n] for l in range(DEPTH)]))
               for n in SMALL_NAMES}
    (small_parts,) = run_exchange(Exchange([_pack([g_small[n] for n in SMALL_NAMES])], "gather"), "gather_small_grads")

    def conv_full(a):
        return lax.dynamic_update_slice(jnp.zeros((DEPTH, DN_CONV, 3 * DN_W), F32), a, (0, 0, me * (3 * DN_W // N_DEV)))

    given = dict(ffn1_norm=(ffn1_norm, m_ffn1_norm, v_ffn1_norm), mix_norm=(mix_norm, m_mix_norm, v_mix_norm),
                 ffn2_norm=(ffn2_norm, m_ffn2_norm, v_ffn2_norm), pool_w=(pool_w, m_pool_w, v_pool_w),
                 pool_scale=(pool_scale, m_pool_scale, v_pool_scale), dn_a_log=(dn_a_log, m_dn_a_log, v_dn_a_log),
                 dn_dt_bias=(dn_dt_bias, m_dn_dt_bias, v_dn_dt_bias),
                 dn_out_norm=(dn_out_norm, m_dn_out_norm, v_dn_out_norm),
                 final_norm=(final_norm, m_final_norm, v_final_norm),
                 dn_conv_w=(conv_full(dn_conv_w), conv_full(m_dn_conv_w), conv_full(v_dn_conv_w)))
    packed_wmv = [_pack([given[n][k] for n in SMALL_NAMES]) for k in range(3)]
    small_out = adam_small(small_parts, *packed_wmv, "adam_small")
    shapes = [small_shapes[n] for n in SMALL_NAMES]
    small_res = {n: [] for n in SMALL_NAMES}
    for arr in small_out:
        for n, v_ in zip(SMALL_NAMES, _unpack(arr, shapes)):
            if n == "dn_conv_w":
                v_ = lax.dynamic_slice(v_, (0, 0, me * (3 * DN_W // N_DEV)), (DEPTH, DN_CONV, 3 * DN_W // N_DEV))
            small_res[n].append(v_)

    def parts_of(group, idx):
        return [big_parts[l][group][idx] for l in range(DEPTH)]

    def adam_transposed(group, idx, w, m, v, name):
        g = parts_sum(*parts_of(group, idx), f"sum_{name}").transpose(0, 2, 1)
        return [g] + list(adam_given(g, w, m, v, f"adam_{name}"))

    big_res = dict(
        ffn1_w_gate=adam_transposed("f1", 0, ffn1_w_gate, m_ffn1_w_gate, v_ffn1_w_gate, "ffn1_gate"),
        ffn1_w_up=adam_transposed("f1", 1, ffn1_w_up, m_ffn1_w_up, v_ffn1_w_up, "ffn1_up"),
        ffn1_w_down=adam_shard(*parts_of("f1", 2), ffn1_w_down, m_ffn1_w_down, v_ffn1_w_down, "adam_ffn1_down"),
        ffn2_w_gate=adam_transposed("f2", 0, ffn2_w_gate, m_ffn2_w_gate, v_ffn2_w_gate, "ffn2_gate"),
        ffn2_w_up=adam_transposed("f2", 1, ffn2_w_up, m_ffn2_w_up, v_ffn2_w_up, "ffn2_up"),
        ffn2_w_down=adam_shard(*parts_of("f2", 2), ffn2_w_down, m_ffn2_w_down, v_ffn2_w_down, "adam_ffn2_down"),
        w_in=adam_shard(*parts_of("io", 0), w_in, m_w_in, v_w_in, "adam_w_in"),
        w_out=adam_shard(*parts_of("io", 1), w_out, m_w_out, v_w_out, "adam_w_out"),
    )

    order = ("ffn1_norm", "ffn1_w_gate", "ffn1_w_up", "ffn1_w_down", "mix_norm", "w_in", "pool_w", "pool_scale",
             "dn_conv_w", "dn_a_log", "dn_dt_bias", "dn_out_norm", "w_out", "ffn2_norm", "ffn2_w_gate", "ffn2_w_up",
             "ffn2_w_down", "final_norm")
    res = {**small_res, **big_res}
    outs = [loss, grad_x]
    for k in range(4):
        outs.extend(res[n][k] for n in order)
    return tuple(outs)
```

```python
import functools
import math

import jax
import jax.numpy as jnp
from jax import lax
from jax.experimental import pallas as pl
from jax.experimental.pallas import tpu as pltpu

F32 = jnp.float32
BF16 = jnp.bfloat16
HI = lax.Precision.HIGHEST
INV_PREC = lax.Precision.HIGH
SDS = jax.ShapeDtypeStruct

N_DEV = 8
SEQ = 4096
D_MODEL = 1024
DEPTH = 2
D_FF = 2816
FF_BLK = D_FF // N_DEV
ATT_W = 256
ATT_E = 64
ATT_BLK = 128
DILATIONS = (1, 4, 16)
POOL_W = 256
POOL_HALO = 16
DN_W = 512
DN_H = 4
DN_E = 128
DN_C = 64
N_CHUNK = SEQ // DN_C
IN_W = 3080
IN_BLK = IN_W // N_DEV
EPS = 1e-6
EXT_ATT = 1280
EXT_REST = 3328
EXT_W = EXT_ATT + EXT_REST
R_DQKV, R_DZ, R_BB, R_AB, R_PU = 0, 1536, 2048, 2560, 3072

ADAM_LR, ADAM_B1, ADAM_B2, ADAM_EPS, ADAM_WD, ADAM_STEP = 0.001, 0.9, 0.999, 1e-08, 0.01, 10

VMEM_LIMIT = 60 * 1024 * 1024
MESH = pl.DeviceIdType.MESH


def _cparams(sem=None):
    kw = dict(vmem_limit_bytes=VMEM_LIMIT)
    if sem is not None:
        kw["dimension_semantics"] = sem
    return pltpu.CompilerParams(**kw)


def _dot(a, b, prec=None):
    return jnp.dot(a, b, preferred_element_type=F32, precision=prec)


def _dot_nt(a, b, prec=None):
    return lax.dot_general(a, b, (((1,), (1,)), ((), ())), preferred_element_type=F32, precision=prec)


def _dot_tn(a, b, prec=None):
    return lax.dot_general(a, b, (((0,), (0,)), ((), ())), preferred_element_type=F32, precision=prec)


def _sigmoid(x):
    return jax.nn.sigmoid(x)


def _rms_stats(x):
    r = lax.rsqrt(jnp.mean(x * x, axis=-1, keepdims=True) + EPS)
    return x * r, r


def _rms_bwd(xh, r, w, dh):
    dxh = dh * w
    dx = r * (dxh - xh * jnp.mean(dxh * xh, axis=-1, keepdims=True))
    return dx, jnp.sum(dh * xh, axis=0, keepdims=True)


FFN_T_FWD = 2048
FFN_T_BWD = 512
FF_TILE = 256
N_FF_TILE = D_FF // FF_TILE


def ffn_shard_operands(gate, up, down):
    return [gate.T.astype(BF16), up.T.astype(BF16), down.astype(BF16)]


def ffn_fwd(x, nw, wgt, wut, wd, name, exch=None):
    t = FFN_T_FWD

    def body(x_ref, nw_ref, wgt_ref, wut_ref, wd_ref, o_ref, h_scr, acc_scr):
        k = pl.program_id(1)

        @pl.when(k == 0)
        def _():
            xh, _r = _rms_stats(x_ref[...])
            h_scr[...] = (xh * nw_ref[...]).astype(BF16)
            acc_scr[...] = jnp.zeros_like(acc_scr)

        h = h_scr[...]
        hg = _dot_nt(h, wgt_ref[...])
        hu = _dot_nt(h, wut_ref[...])
        a = (hg * _sigmoid(hg) * hu).astype(BF16)
        acc_scr[...] += _dot(a, wd_ref[...])

        @pl.when(k == N_FF_TILE - 1)
        def _():
            o_ref[...] = x_ref[...] + 0.5 * acc_scr[...]

    w_spec = pl.BlockSpec((FF_TILE, D_MODEL), lambda i, k: (k, 0))
    return _call(
        body, name=name, grid=(SEQ // t, N_FF_TILE),
        in_specs=[pl.BlockSpec((t, D_MODEL), lambda i, k: (i, 0)),
                  pl.BlockSpec((1, D_MODEL), lambda i, k: (0, 0)), w_spec, w_spec, w_spec],
        out_specs=pl.BlockSpec((t, D_MODEL), lambda i, k: (i, 0)),
        out_shape=SDS((SEQ, D_MODEL), F32),
        scratch_shapes=[pltpu.VMEM((t, D_MODEL), BF16), pltpu.VMEM((t, D_MODEL), F32)],
        sem=("arbitrary", "arbitrary"), args=(x, nw, wgt, wut, wd), exch=exch)


def ffn_bwd(x, dxo, nw, wgt, wut, wd, name, exch=None):
    t = FFN_T_BWD
    nt = SEQ // t

    def body(x_ref, dxo_ref, nw_ref, wgt_ref, wut_ref, wd_ref, dx_ref, dwgt_ref, dwut_ref, dwd_ref, dnw_ref,
             dh_scr, ag_scr, au_scr, ad_scr, h_scr):
        k = pl.program_id(0)
        i = pl.program_id(1)
        rows = pl.ds(pl.multiple_of(i * t, t), t)
        nw_v = nw_ref[...]

        @pl.when(k == 0)
        def _():
            xh0, _r0 = _rms_stats(x_ref[...])
            h_scr[rows, :] = (xh0 * nw_v).astype(BF16)

        h = h_scr[rows, :]
        dy = (0.5 * dxo_ref[...]).astype(BF16)
        wgt = wgt_ref[...]
        wut = wut_ref[...]
        hg = _dot_nt(h, wgt)
        hu = _dot_nt(h, wut)
        sg = _sigmoid(hg)
        sil = hg * sg
        a = (sil * hu).astype(BF16)
        da = _dot_nt(dy, wd_ref[...])
        dhu = (da * sil).astype(BF16)
        dhg = (da * hu * (sg * (1.0 + hg * (1.0 - sg)))).astype(BF16)
        p_d = _dot_tn(a, dy)
        p_g = _dot_tn(dhg, h)
        p_u = _dot_tn(dhu, h)
        dh = _dot(dhg, wgt) + _dot(dhu, wut)

        @pl.when(i == 0)
        def _():
            ad_scr[...] = p_d
            ag_scr[...] = p_g
            au_scr[...] = p_u

        @pl.when(i > 0)
        def _():
            ad_scr[...] += p_d
            ag_scr[...] += p_g
            au_scr[...] += p_u

        @pl.when(i == nt - 1)
        def _():
            dwd_ref[...] = ad_scr[...].astype(BF16)
            dwgt_ref[...] = ag_scr[...].astype(BF16)
            dwut_ref[...] = au_scr[...].astype(BF16)

        @pl.when(k == 0)
        def _():
            dh_scr[rows, :] = dh

        @pl.when(k > 0)
        def _():
            dh_scr[rows, :] += dh

        @pl.when(jnp.logical_and(k == 0, i == 0))
        def _():
            dnw_ref[...] = jnp.zeros_like(dnw_ref)

        @pl.when(k == N_FF_TILE - 1)
        def _():
            xh, r = _rms_stats(x_ref[...])
            dx, dw = _rms_bwd(xh, r, nw_v, dh_scr[rows, :])
            dx_ref[...] = dxo_ref[...] + dx
            dnw_ref[...] += dw

    last = N_FF_TILE - 1
    w_spec = pl.BlockSpec((FF_TILE, D_MODEL), lambda k, i: (k, 0))
    return _call(
        body, name=name, grid=(N_FF_TILE, nt),
        in_specs=[pl.BlockSpec((t, D_MODEL), lambda k, i: (i, 0)),
                  pl.BlockSpec((t, D_MODEL), lambda k, i: (i, 0)),
                  pl.BlockSpec((1, D_MODEL), lambda k, i: (0, 0)), w_spec, w_spec, w_spec],
        out_specs=[pl.BlockSpec((t, D_MODEL), lambda k, i: (jnp.where(k == last, i, 0), 0)),
                   w_spec, w_spec, w_spec, pl.BlockSpec((1, D_MODEL), lambda k, i: (0, 0))],
        out_shape=[SDS((SEQ, D_MODEL), F32), SDS((D_FF, D_MODEL), BF16), SDS((D_FF, D_MODEL), BF16),
                   SDS((D_FF, D_MODEL), BF16), SDS((1, D_MODEL), F32)],
        scratch_shapes=[pltpu.VMEM((SEQ, D_MODEL), F32), pltpu.VMEM((FF_TILE, D_MODEL), F32),
                        pltpu.VMEM((FF_TILE, D_MODEL), F32), pltpu.VMEM((FF_TILE, D_MODEL), F32),
                        pltpu.VMEM((SEQ, D_MODEL), BF16)],
        sem=("arbitrary", "arbitrary"), args=(x, dxo, nw, wgt, wut, wd), exch=exch)


def loss_head(x, fw, target, name):
    t = 512

    def body(x_ref, fw_ref, tg_ref, loss_ref, dx_ref, dfw_ref):
        i = pl.program_id(0)
        xh, r = _rms_stats(x_ref[...])
        w = fw_ref[...]
        err = xh * w - tg_ref[...]
        part = 0.5 * jnp.sum(jnp.sum(err * err, axis=-1, keepdims=True), axis=0, keepdims=True) / D_MODEL
        dx, dw = _rms_bwd(xh, r, w, err * (1.0 / D_MODEL))
        dx_ref[...] = dx

        @pl.when(i == 0)
        def _():
            loss_ref[...] = jnp.zeros_like(loss_ref)
            dfw_ref[...] = jnp.zeros_like(dfw_ref)

        loss_ref[...] += jnp.broadcast_to(part, loss_ref.shape)
        dfw_ref[...] += dw

    return pl.pallas_call(
        body, name=name, grid=(SEQ // t,),
        in_specs=[pl.BlockSpec((t, D_MODEL), lambda i: (i, 0)),
                  pl.BlockSpec((1, D_MODEL), lambda i: (0, 0)),
                  pl.BlockSpec((t, D_MODEL), lambda i: (i, 0))],
        out_specs=[pl.BlockSpec((1, 128), lambda i: (0, 0)),
                   pl.BlockSpec((t, D_MODEL), lambda i: (i, 0)),
                   pl.BlockSpec((1, D_MODEL), lambda i: (0, 0))],
        out_shape=[SDS((1, 128), F32), SDS((SEQ, D_MODEL), F32), SDS((1, D_MODEL), F32)],
        compiler_params=_cparams(("arbitrary",)),
    )(x, fw, target)


MIX_T = 256


def _slabs_load(ref, first, n):
    return jnp.concatenate([ref[first + j] for j in range(n)], axis=1)


def _slabs_store(ref, first, val):
    for j in range(val.shape[1] // 128):
        ref[first + j] = val[:, 128 * j:128 * j + 128]


def _slab_spec(k, t):
    return pl.BlockSpec((k, t, 128), lambda i: (0, i, 0))


def mix_in_fwd(x, nw, wext, cos, sin, name):
    t = MIX_T

    def body(x_ref, nw_ref, w_ref, cos_ref, sin_ref, att_ref, rest_ref):
        xh, _r = _rms_stats(x_ref[...])
        h = (xh * nw_ref[...]).astype(BF16)
        pa = _dot(h, w_ref[:, 0:EXT_ATT])
        c = cos_ref[...]
        s = sin_ref[...]
        _slabs_store(att_ref, 0, pa[:, 0:256] * c + pa[:, 768:1024] * s)
        _slabs_store(att_ref, 2, pa[:, 256:512] * c + pa[:, 1024:1280] * s)
        _slabs_store(att_ref, 4, pa[:, 512:768])
        for j in range(EXT_REST // 256):
            rest_ref[:, 256 * j:256 * j + 256] = _dot(h, w_ref[:, EXT_ATT + 256 * j:EXT_ATT + 256 * j + 256])

    return pl.pallas_call(
        body, name=name, grid=(SEQ // t,),
        in_specs=[pl.BlockSpec((t, D_MODEL), lambda i: (i, 0)),
                  pl.BlockSpec((1, D_MODEL), lambda i: (0, 0)),
                  pl.BlockSpec((D_MODEL, EXT_W), lambda i: (0, 0)),
                  pl.BlockSpec((t, ATT_W), lambda i: (i, 0)),
                  pl.BlockSpec((t, ATT_W), lambda i: (i, 0))],
        out_specs=[_slab_spec(6, t),
                   pl.BlockSpec((t, EXT_REST), lambda i: (i, 0))],
        out_shape=[SDS((6, SEQ, 128), F32), SDS((SEQ, EXT_REST), F32)],
        compiler_params=_cparams(("arbitrary",)),
    )(x, nw, wext, cos, sin)


def assemble_dproj(datts, cos, sin, d_dqkv, dz, dbb, dab, dpu, name):
    t = 512

    def body(d1_ref, d4_ref, d16_ref, cos_ref, sin_ref, dqkv_ref, dz_ref, dbb_ref, dab_ref, dpu_ref, o_ref):
        da6 = d1_ref[...] + d4_ref[...] + d16_ref[...]
        da = jnp.concatenate([da6[j] for j in range(6)], axis=1)
        c = cos_ref[...]
        s = sin_ref[...]
        dq = da[:, 0:256]
        dk = da[:, 256:512]
        o_ref[:, 0:256] = (dq * c).astype(BF16)
        o_ref[:, 256:512] = (dk * c).astype(BF16)
        o_ref[:, 512:768] = da[:, 512:768].astype(BF16)
        o_ref[:, 768:1024] = (dq * s).astype(BF16)
        o_ref[:, 1024:1280] = (dk * s).astype(BF16)
        b = EXT_ATT
        o_ref[:, b + R_DQKV:b + R_DQKV + 1536] = dqkv_ref[...].astype(BF16)
        o_ref[:, b + R_DZ:b + R_DZ + 512] = dz_ref[...].astype(BF16)
        o_ref[:, b + R_BB:b + R_BB + 512] = dbb_ref[...].astype(BF16)
        o_ref[:, b + R_AB:b + R_AB + 512] = dab_ref[...].astype(BF16)
        o_ref[:, b + R_PU:b + R_PU + 256] = dpu_ref[...].astype(BF16)

    row = lambda w: pl.BlockSpec((t, w), lambda i: (i, 0))
    return pl.pallas_call(
        body, name=name, grid=(SEQ // t,),
        in_specs=[_slab_spec(6, t), _slab_spec(6, t), _slab_spec(6, t),
                  row(256), row(256), row(1536), row(512), row(512), row(512), row(256)],
        out_specs=row(EXT_W),
        out_shape=SDS((SEQ, EXT_W), BF16),
        compiler_params=_cparams(("arbitrary",)),
    )(*datts, cos, sin, d_dqkv, dz, dbb, dab, dpu)


def linear_bwd(x, dxo, nw, dy, w, name):
    t = 512
    nb = 768
    n = w.shape[1]
    nt = SEQ // t
    nn = n // nb

    def body(x_ref, dxo_ref, nw_ref, dy_ref, w_ref, dx_ref, dw_ref, dnw_ref, dh_scr, h_scr):
        k = pl.program_id(0)
        i = pl.program_id(1)
        rows = pl.ds(pl.multiple_of(i * t, t), t)
        nw_v = nw_ref[...]

        @pl.when(k == 0)
        def _():
            xh0, _r0 = _rms_stats(x_ref[...])
            h_scr[rows, :] = (xh0 * nw_v).astype(BF16)

        h = h_scr[rows, :]
        dyv = dy_ref[...]
        p_w = _dot_tn(h, dyv)
        dh = _dot_nt(dyv, w_ref[...])

        @pl.when(i == 0)
        def _():
            dw_ref[...] = p_w

        @pl.when(i > 0)
        def _():
            dw_ref[...] += p_w

        @pl.when(k == 0)
        def _():
            dh_scr[rows, :] = dh

        @pl.when(k > 0)
        def _():
            dh_scr[rows, :] += dh

        @pl.when(jnp.logical_and(k == 0, i == 0))
        def _():
            dnw_ref[...] = jnp.zeros_like(dnw_ref)

        @pl.when(k == nn - 1)
        def _():
            xh, r = _rms_stats(x_ref[...])
            dx, dw = _rms_bwd(xh, r, nw_v, dh_scr[rows, :])
            dx_ref[...] = dxo_ref[...] + dx
            dnw_ref[...] += dw

    last = nn - 1
    return pl.pallas_call(
        body, name=name, grid=(nn, nt),
        in_specs=[pl.BlockSpec((t, D_MODEL), lambda k, i: (i, 0)),
                  pl.BlockSpec((t, D_MODEL), lambda k, i: (i, 0)),
                  pl.BlockSpec((1, D_MODEL), lambda k, i: (0, 0)),
                  pl.BlockSpec((t, nb), lambda k, i: (i, k)),
                  pl.BlockSpec((D_MODEL, nb), lambda k, i: (0, k))],
        out_specs=[pl.BlockSpec((t, D_MODEL), lambda k, i: (jnp.where(k == last, i, 0), 0)),
                   pl.BlockSpec((D_MODEL, nb), lambda k, i: (0, k)),
                   pl.BlockSpec((1, D_MODEL), lambda k, i: (0, 0))],
        out_shape=[SDS((SEQ, D_MODEL), F32), SDS((D_MODEL, n), F32), SDS((1, D_MODEL), F32)],
        scratch_shapes=[pltpu.VMEM((SEQ, D_MODEL), F32), pltpu.VMEM((SEQ, D_MODEL), BF16)],
        compiler_params=_cparams(("arbitrary", "arbitrary")),
    )(x, dxo, nw, dy, w)


def _att_masks():
    qi = lax.broadcasted_iota(jnp.int32, (ATT_BLK, ATT_BLK), 0)
    ki = lax.broadcasted_iota(jnp.int32, (ATT_BLK, ATT_BLK), 1)
    return ki <= qi, ki >= qi


NEG = -1e30


N_ATT_BLK = SEQ // ATT_BLK


def _class_rows(i, d):
    per_class = N_ATT_BLK // d
    shift = per_class.bit_length() - 1
    r = i >> shift
    j = i & (per_class - 1)
    span = ATT_BLK * d
    start = r + span * j
    prev = jnp.where(j == 0, start, start - span)
    nxt = jnp.where(j == per_class - 1, start, start + span)

    def rows(s0):
        if d == 1:
            return pl.ds(pl.multiple_of(s0, ATT_BLK), ATT_BLK)
        return pl.ds(s0, ATT_BLK, stride=d)

    return rows(start), rows(prev), rows(nxt), j != 0, j != per_class - 1


def _slab_heads(ref, slab, rows):
    x0 = ref[pl.ds(slab, 1), rows, :][0]
    x1 = ref[pl.ds(slab + 1, 1), rows, :][0]
    return jnp.stack([x0[:, 0:ATT_E], x0[:, ATT_E:], x1[:, 0:ATT_E], x1[:, ATT_E:]], axis=0)


def _put_slab_heads(ref, slab, rows, val):
    ref[pl.ds(slab, 1), rows, :] = jnp.concatenate([val[0], val[1]], axis=1)[None]
    ref[pl.ds(slab + 1, 1), rows, :] = jnp.concatenate([val[2], val[3]], axis=1)[None]


ATT_BLOCKS_PER_STEP = 2


def _resident_call(body, ins, out_slabs, name):
    n_in = len(ins)
    steps = N_ATT_BLK // ATT_BLOCKS_PER_STEP

    def wrapped(*refs):
        hbm_in, hbm_out = refs[:n_in], refs[n_in]
        vm_in, vm_out, sem = refs[n_in + 1:2 * n_in + 1], refs[2 * n_in + 1], refs[2 * n_in + 2]
        i = pl.program_id(0)

        @pl.when(i == 0)
        def _():
            copies = [pltpu.make_async_copy(h, v, sem.at[k]) for k, (h, v) in enumerate(zip(hbm_in, vm_in))]
            for cp in copies:
                cp.start()
            for cp in copies:
                cp.wait()

        for b in range(ATT_BLOCKS_PER_STEP):
            body(ATT_BLOCKS_PER_STEP * i + b, *vm_in, vm_out)

        @pl.when(i == steps - 1)
        def _():
            cp = pltpu.make_async_copy(vm_out, hbm_out, sem.at[n_in])
            cp.start()
            cp.wait()

    return pl.pallas_call(
        wrapped, name=name, grid=(steps,),
        in_specs=[ANY_SPEC] * n_in, out_specs=ANY_SPEC, out_shape=SDS((out_slabs, SEQ, 128), F32),
        scratch_shapes=[pltpu.VMEM(a.shape, a.dtype) for a in ins] + [pltpu.VMEM((out_slabs, SEQ, 128), F32),
                                                                      pltpu.SemaphoreType.DMA((n_in + 1,))],
        compiler_params=_cparams(("arbitrary",)),
    )(*ins)


def _att_fwd_math(ld, has_prev):
    m_d, m_p = _att_masks()
    m_p = jnp.logical_and(m_p, has_prev)
    q = ld("att", 0, "cur").astype(BF16)
    kc = ld("att", 2, "cur").astype(BF16)
    vc = ld("att", 4, "cur").astype(BF16)
    kp = ld("att", 2, "prev").astype(BF16)
    vp = ld("att", 4, "prev").astype(BF16)
    sd = jnp.where(m_d, _bdot(q, kc, 2, 2) * 0.125, NEG)
    sp = jnp.where(m_p, _bdot(q, kp, 2, 2) * 0.125, NEG)
    m = jnp.maximum(jnp.max(sd, axis=-1, keepdims=True), jnp.max(sp, axis=-1, keepdims=True))
    pd = jnp.exp(sd - m)
    pp = jnp.exp(sp - m)
    den = jnp.sum(pd, axis=-1, keepdims=True) + jnp.sum(pp, axis=-1, keepdims=True)
    inv = 1.0 / den
    o = _bdot((pd * inv).astype(BF16), vc, 2, 1) + _bdot((pp * inv).astype(BF16), vp, 2, 1)
    return o, jnp.broadcast_to(m + jnp.log(den), (4, ATT_BLK, ATT_E))


def _att_bwd_math(ld, has_prev, has_next):
    m_d, m_band = _att_masks()
    m_p = jnp.logical_and(m_band, has_prev)
    m_n = jnp.logical_and(m_band, has_next)

    def pair(q, k, v, lse, do, dterm, mask):
        s = jnp.where(mask, _bdot(q, k, 2, 2) * 0.125, NEG)
        p = jnp.exp(s - lse)
        dp = _bdot(do, v, 2, 2)
        ds = (p * (dp + dterm) * 0.125).astype(BF16)
        return p.astype(BF16), ds

    q_c = ld("att", 0, "cur").astype(BF16)
    k_c = ld("att", 2, "cur").astype(BF16)
    v_c = ld("att", 4, "cur").astype(BF16)
    k_p = ld("att", 2, "prev").astype(BF16)
    v_p = ld("att", 4, "prev").astype(BF16)
    q_n = ld("att", 0, "next").astype(BF16)
    o_c = ld("ol", 0, "cur")
    o_n = ld("ol", 0, "next")
    lse_c = ld("ol", 2, "cur")[:, :, 0:1]
    lse_n = ld("ol", 2, "next")[:, :, 0:1]
    do_c = ld("dol", 0, "cur")
    do_n = ld("dol", 0, "next")
    t_c = ld("dol", 2, "cur")[:, :, 0:1] - jnp.sum(do_c * o_c, axis=-1, keepdims=True)
    t_n = ld("dol", 2, "next")[:, :, 0:1] - jnp.sum(do_n * o_n, axis=-1, keepdims=True)
    do_cb = do_c.astype(BF16)
    do_nb = do_n.astype(BF16)
    p1, ds1 = pair(q_c, k_c, v_c, lse_c, do_cb, t_c, m_d)
    _p2, ds2 = pair(q_c, k_p, v_p, lse_c, do_cb, t_c, m_p)
    p3, ds3 = pair(q_n, k_c, v_c, lse_n, do_nb, t_n, m_n)
    return (_bdot(ds1, k_c, 2, 1) + _bdot(ds2, k_p, 2, 1), _bdot(ds1, q_c, 1, 1) + _bdot(ds3, q_n, 1, 1),
            _bdot(p1, do_cb, 1, 1) + _bdot(p3, do_nb, 1, 1))


ROWS_A = pl.ds(0, ATT_BLK)
ROWS_B = pl.ds(ATT_BLK, ATT_BLK)
N_ATT_PAIR = N_ATT_BLK // 2


def _pair_spec(k):
    return pl.BlockSpec((k, 2 * ATT_BLK, 128), lambda i: (0, i, 0))


def _before_pair_spec(k):
    return pl.BlockSpec((k, ATT_BLK, 128), lambda i: (0, jnp.maximum(2 * i - 1, 0), 0))


def _after_pair_spec(k):
    return pl.BlockSpec((k, ATT_BLK, 128), lambda i: (0, jnp.minimum(2 * i + 2, N_ATT_BLK - 1), 0))


def att_fwd_s(att, d, name):
    if d == 1:
        def body1(cur_ref, prev_ref, o_ref):
            i = pl.program_id(0)
            for rows, views, has_prev in (
                    (ROWS_A, {"cur": (cur_ref, ROWS_A), "prev": (prev_ref, ROWS_A)}, i != 0),
                    (ROWS_B, {"cur": (cur_ref, ROWS_B), "prev": (cur_ref, ROWS_A)}, True)):
                o, lse = _att_fwd_math(lambda _a, slab, where, v=views: _slab_heads(v[where][0], slab, v[where][1]), has_prev)
                _put_slab_heads(o_ref, 0, rows, o)
                _put_slab_heads(o_ref, 2, rows, lse)

        return pl.pallas_call(
            body1, name=name, grid=(N_ATT_PAIR,),
            in_specs=[_pair_spec(6), _before_pair_spec(6)], out_specs=_pair_spec(4),
            out_shape=SDS((4, SEQ, 128), F32), compiler_params=_cparams(("arbitrary",)),
        )(att, att)

    def body(i, att_ref, o_ref):
        cur, prev, _nxt, has_prev, _has_next = _class_rows(i, d)
        rows = {"cur": cur, "prev": prev}
        o, lse = _att_fwd_math(lambda _a, slab, where: _slab_heads(att_ref, slab, rows[where]), has_prev)
        _put_slab_heads(o_ref, 0, cur, o)
        _put_slab_heads(o_ref, 2, cur, lse)

    return _resident_call(body, [att], 4, name)


def att_bwd_s(att, ol, dol, d, name):
    if d == 1:
        def body1(a_p, a_c, a_n, ol_c, ol_n, dol_c, dol_n, d_ref):
            i = pl.program_id(0)
            first = {("att", "prev"): (a_p, ROWS_A), ("att", "cur"): (a_c, ROWS_A), ("att", "next"): (a_c, ROWS_B),
                     ("ol", "cur"): (ol_c, ROWS_A), ("ol", "next"): (ol_c, ROWS_B),
                     ("dol", "cur"): (dol_c, ROWS_A), ("dol", "next"): (dol_c, ROWS_B)}
            second = {("att", "prev"): (a_c, ROWS_A), ("att", "cur"): (a_c, ROWS_B), ("att", "next"): (a_n, ROWS_A),
                      ("ol", "cur"): (ol_c, ROWS_B), ("ol", "next"): (ol_n, ROWS_A),
                      ("dol", "cur"): (dol_c, ROWS_B), ("dol", "next"): (dol_n, ROWS_A)}
            for rows, views, has_prev, has_next in ((ROWS_A, first, i != 0, True),
                                                    (ROWS_B, second, True, i != N_ATT_PAIR - 1)):
                dq, dk, dv = _att_bwd_math(
                    lambda a, slab, where, v=views: _slab_heads(v[(a, where)][0], slab, v[(a, where)][1]), has_prev, has_next)
                _put_slab_heads(d_ref, 0, rows, dq)
                _put_slab_heads(d_ref, 2, rows, dk)
                _put_slab_heads(d_ref, 4, rows, dv)

        return pl.pallas_call(
            body1, name=name, grid=(N_ATT_PAIR,),
            in_specs=[_before_pair_spec(6), _pair_spec(6), _after_pair_spec(6), _pair_spec(4), _after_pair_spec(4),
                      _pair_spec(4), _after_pair_spec(4)],
            out_specs=_pair_spec(6), out_shape=SDS((6, SEQ, 128), F32), compiler_params=_cparams(("arbitrary",)),
        )(att, att, att, ol, ol, dol, dol)

    def body(i, att_ref, ol_ref, dol_ref, d_ref):
        cur, prev, nxt, has_prev, has_next = _class_rows(i, d)
        rows = {"cur": cur, "prev": prev, "next": nxt}
        refs = {"att": att_ref, "ol": ol_ref, "dol": dol_ref}
        dq, dk, dv = _att_bwd_math(lambda a, slab, where: _slab_heads(refs[a], slab, rows[where]), has_prev, has_next)
        _put_slab_heads(d_ref, 0, cur, dq)
        _put_slab_heads(d_ref, 2, cur, dk)
        _put_slab_heads(d_ref, 4, cur, dv)

    return _resident_call(body, [att, ol, dol], 6, name)


def _shift_down(x, k):
    rows = lax.broadcasted_iota(jnp.int32, x.shape, 0)
    return jnp.where(rows >= k, pltpu.roll(x, k, 0), 0.0)


def _shift_up(x, k):
    n = x.shape[0]
    rows = lax.broadcasted_iota(jnp.int32, x.shape, 0)
    return jnp.where(rows < n - k, pltpu.roll(x, n - k, 0), 0.0)


@functools.partial(jax.custom_vjp, nondiff_argnums=(1,))
def _delay(x, k):
    return _shift_down(x, k)


def _delay_fwd(x, k):
    return _shift_down(x, k), None


def _delay_bwd(k, _res, g):
    return (_shift_up(g, k),)


_delay.defvjp(_delay_fwd, _delay_bwd)

DN_CONV = 4


def _dn_prep_fn(u, w, kind):
    y = w[DN_CONV - 1:DN_CONV] * u
    for j in range(DN_CONV - 1):
        y = y + w[j:j + 1] * _delay(u, DN_CONV - 1 - j)
    y = y * _sigmoid(y)
    nrm = y * lax.rsqrt(jnp.sum(y * y, axis=-1, keepdims=True) + EPS)
    return jnp.where(kind == 0, nrm * (DN_E ** -0.5), jnp.where(kind == 1, nrm, y))


def dn_prep_fwd(rest, conv_w, name):
    def body(u_ref, w_ref, o_ref):
        j = pl.program_id(0)
        kind = (j >= DN_H).astype(jnp.int32) + (j >= 2 * DN_H).astype(jnp.int32)
        o_ref[...] = _dn_prep_fn(u_ref[...], w_ref[...], kind)

    return pl.pallas_call(
        body, name=name, grid=(3 * DN_H,),
        in_specs=[pl.BlockSpec((SEQ, DN_E), lambda j: (0, j)),
                  pl.BlockSpec((DN_CONV, DN_E), lambda j: (0, j))],
        out_specs=pl.BlockSpec((SEQ, DN_E), lambda j: (0, j)),
        out_shape=SDS((SEQ, 3 * DN_W), F32),
        compiler_params=_cparams(("arbitrary",)),
    )(rest, conv_w)


def dn_prep_bwd(rest, conv_w, dqkv, name):
    def body(u_ref, w_ref, g_ref, du_ref, dw_ref):
        j = pl.program_id(0)
        kind = (j >= DN_H).astype(jnp.int32) + (j >= 2 * DN_H).astype(jnp.int32)
        _y, vjp = jax.vjp(lambda u, w: _dn_prep_fn(u, w, kind), u_ref[...], w_ref[...])
        du, dw = vjp(g_ref[...])
        du_ref[...] = du
        dw_ref[...] = dw

    return pl.pallas_call(
        body, name=name, grid=(3 * DN_H,),
        in_specs=[pl.BlockSpec((SEQ, DN_E), lambda j: (0, j)),
                  pl.BlockSpec((DN_CONV, DN_E), lambda j: (0, j)),
                  pl.BlockSpec((SEQ, DN_E), lambda j: (0, j))],
        out_specs=[pl.BlockSpec((SEQ, DN_E), lambda j: (0, j)),
                   pl.BlockSpec((DN_CONV, DN_E), lambda j: (0, j))],
        out_shape=[SDS((SEQ, 3 * DN_W), F32), SDS((DN_CONV, 3 * DN_W), F32)],
        compiler_params=_cparams(("arbitrary",)),
    )(rest, conv_w, dqkv)


def _bdot(a, b, ca, cb, prec=None):
    return lax.dot_general(a, b, (((ca,), (cb,)), ((0,), (0,))), preferred_element_type=F32, precision=prec)


def _unit_lower_inverse(a):
    eye = (lax.broadcasted_iota(jnp.int32, (DN_C, DN_C), 0) == lax.broadcasted_iota(jnp.int32, (DN_C, DN_C), 1)).astype(F32)
    p = eye - a
    b = _bdot(a, a, 2, 1, INV_PREC)
    for lvl in range(5):
        p = p + _bdot(p, b, 2, 1, INV_PREC)
        if lvl < 4:
            b = _bdot(b, b, 2, 1, INV_PREC)
    return p


@jax.custom_vjp
def _tri_inv(a):
    return _unit_lower_inverse(a)


def _tri_inv_fwd(a):
    t = _unit_lower_inverse(a)
    return t, t


def _tri_inv_bwd(t, g):
    return (-_bdot(_bdot(t, g, 1, 1, INV_PREC), t, 2, 2, INV_PREC),)


_tri_inv.defvjp(_tri_inv_fwd, _tri_inv_bwd)


def _b16(x):
    return x.astype(BF16)


def _heads(ref, base=0):
    return jnp.stack([ref[:, base + DN_E * hd:base + DN_E * hd + DN_E] for hd in range(DN_H)], axis=0)


def _put_heads(ref, val, base=0):
    for hd in range(DN_H):
        ref[:, base + DN_E * hd:base + DN_E * hd + DN_E] = val[hd]


DN_G_LOG2 = 3
DN_G = 1 << DN_G_LOG2
N_INST = DN_G * DN_H


def _dn_intra(q, k, v, bb, ab, alog, dtb):
    ri = lax.broadcasted_iota(jnp.int32, (DN_C, DN_C), 0)
    ci = lax.broadcasted_iota(jnp.int32, (DN_C, DN_C), 1)
    lower = ri >= ci
    strict = ri > ci
    nh = q.shape[0]
    beta = _sigmoid(bb)
    xg = ab + dtb
    softplus = jnp.maximum(xg, 0.0) + jnp.log(1.0 + jnp.exp(-jnp.abs(xg)))
    gi = -jnp.exp(alog) * softplus
    g = _bdot(jnp.broadcast_to(lower.astype(F32), (nh, DN_C, DN_C)), gi, 2, 1, HI)
    eg = jnp.exp(g)
    kb = k * beta
    vb = v * beta
    g_col = g[:, :, 0:DN_C]
    g_row = _bdot(jnp.full((nh, DN_C, DN_E), 1.0 / DN_E, F32), g, 2, 2, HI)
    decay = jnp.where(lower, jnp.exp(jnp.where(lower, g_col - g_row, 0.0)), 0.0)
    kbf = _b16(k)
    a = jnp.where(strict, _bdot(_b16(kb), kbf, 2, 2) * decay, 0.0)
    tb = _b16(_tri_inv(a))
    u = _bdot(tb, _b16(vb), 2, 1)
    w = _bdot(tb, _b16(kb * eg), 2, 1)
    intra = jnp.where(lower, _bdot(_b16(q), kbf, 2, 2) * decay, 0.0)
    g_last = g[:, DN_C - 1:DN_C, :]
    return u, w, q * eg, k * jnp.exp(g_last - g), intra, jnp.exp(g_last)


def _dn_inter(u, w, qg, kdec, intra, egl, state):
    sb = _b16(state)
    v_new = u - _bdot(_b16(w), sb, 2, 1)
    o = _bdot(_b16(qg), sb, 2, 1) + _bdot(_b16(intra), _b16(v_new), 2, 1)
    return o, state * egl + _bdot(_b16(kdec), _b16(v_new), 1, 1)


def _inst(ref, base=0):
    per_head = [ref[:, base + DN_E * hd:base + DN_E * hd + DN_E].reshape(DN_G, DN_C, DN_E) for hd in range(DN_H)]
    return jnp.concatenate(per_head, axis=0)


def _inst_rows(ref):
    rows = [jnp.broadcast_to(ref[:, DN_E * hd:DN_E * hd + DN_E][None], (DN_G, 1, DN_E)) for hd in range(DN_H)]
    return jnp.concatenate(rows, axis=0)


def _put_inst(ref, val, width=DN_E, base=0):
    for hd in range(DN_H):
        ref[:, base + width * hd:base + width * hd + width] = val[DN_G * hd:DN_G * hd + DN_G].reshape(DN_G * DN_C, width)


def _intra_args(qkv_ref, bb_ref, ab_ref, alog_ref, dtb_ref):
    return (_inst(qkv_ref), _inst(qkv_ref, DN_W), _inst(qkv_ref, 2 * DN_W), _inst(bb_ref), _inst(ab_ref),
            _inst_rows(alog_ref), _inst_rows(dtb_ref))


def _intra_in_specs():
    t = DN_G * DN_C
    return [pl.BlockSpec((t, 3 * DN_W), lambda n: (n, 0)),
            pl.BlockSpec((t, DN_W), lambda n: (n, R_BB // DN_W)),
            pl.BlockSpec((t, DN_W), lambda n: (n, R_AB // DN_W)),
            pl.BlockSpec((1, DN_W), lambda n: (0, 0)),
            pl.BlockSpec((1, DN_W), lambda n: (0, 0))]


def dn_intra_fwd(qkv, rest, alog_b, dtb_b, name, exch=None):
    t = DN_G * DN_C

    def body(qkv_ref, bb_ref, ab_ref, alog_ref, dtb_ref, u_ref, w_ref, qg_ref, kd_ref, in_ref, egl_ref):
        u, w, qg, kdec, intra, egl = _dn_intra(*_intra_args(qkv_ref, bb_ref, ab_ref, alog_ref, dtb_ref))
        _put_inst(u_ref, u)
        _put_inst(w_ref, w.astype(BF16))
        _put_inst(qg_ref, qg.astype(BF16))
        _put_inst(kd_ref, kdec.astype(BF16))
        _put_inst(in_ref, intra.astype(BF16), DN_C)
        for hd in range(DN_H):
            egl_ref[:, DN_E * hd:DN_E * hd + DN_E] = egl[DN_G * hd:DN_G * hd + DN_G].reshape(DN_G, DN_E)

    row = lambda w_: pl.BlockSpec((t, w_), lambda n: (n, 0))
    return _call(
        body, name=name, grid=(N_CHUNK // DN_G,), in_specs=_intra_in_specs(),
        out_specs=[row(DN_W), row(DN_W), row(DN_W), row(DN_W), row(DN_H * DN_C),
                   pl.BlockSpec((DN_G, DN_W), lambda n: (n, 0))],
        out_shape=[SDS((SEQ, DN_W), F32), SDS((SEQ, DN_W), BF16), SDS((SEQ, DN_W), BF16), SDS((SEQ, DN_W), BF16),
                   SDS((SEQ, DN_H * DN_C), BF16), SDS((N_CHUNK, DN_W), F32)],
        scratch_shapes=[], sem=("arbitrary",), args=(qkv, rest, rest, alog_b, dtb_b), exch=exch)


def dn_intra_bwd(qkv, rest, alog_b, dtb_b, du, dw, dqg, dkd, dintra, degl, name):
    t = DN_G * DN_C

    def body(qkv_ref, bb_ref, ab_ref, alog_ref, dtb_ref, du_ref, dw_ref, dqg_ref, dkd_ref, din_ref, degl_ref,
             dqkv_ref, dbb_ref, dab_ref, dalog_ref, ddtb_ref):
        @pl.when(pl.program_id(0) == 0)
        def _():
            dalog_ref[...] = jnp.zeros_like(dalog_ref)
            ddtb_ref[...] = jnp.zeros_like(ddtb_ref)

        _out, vjp = jax.vjp(_dn_intra, *_intra_args(qkv_ref, bb_ref, ab_ref, alog_ref, dtb_ref))
        d_in = jnp.concatenate([din_ref[:, DN_C * hd:DN_C * hd + DN_C].reshape(DN_G, DN_C, DN_C) for hd in range(DN_H)], axis=0)
        d_egl = jnp.concatenate([degl_ref[:, DN_E * hd:DN_E * hd + DN_E].reshape(DN_G, 1, DN_E) for hd in range(DN_H)], axis=0)
        dq, dk, dv, dbb, dab, dalog, ddtb = vjp((_inst(du_ref), _inst(dw_ref), _inst(dqg_ref), _inst(dkd_ref), d_in, d_egl))
        _put_inst(dqkv_ref, dq)
        _put_inst(dqkv_ref, dk, DN_E, DN_W)
        _put_inst(dqkv_ref, dv, DN_E, 2 * DN_W)
        _put_inst(dbb_ref, dbb)
        _put_inst(dab_ref, dab)
        for hd in range(DN_H):
            sl = slice(DN_E * hd, DN_E * hd + DN_E)
            dalog_ref[:, sl] += jnp.sum(dalog[DN_G * hd:DN_G * hd + DN_G], axis=0)
            ddtb_ref[:, sl] += jnp.sum(ddtb[DN_G * hd:DN_G * hd + DN_G], axis=0)

    row = lambda w_: pl.BlockSpec((t, w_), lambda n: (n, 0))
    acc = pl.BlockSpec((1, DN_W), lambda n: (0, 0))
    return pl.pallas_call(
        body, name=name, grid=(N_CHUNK // DN_G,),
        in_specs=_intra_in_specs() + [row(DN_W), row(DN_W), row(DN_W), row(DN_W), row(DN_H * DN_C),
                                      pl.BlockSpec((DN_G, DN_W), lambda n: (n, 0))],
        out_specs=[row(3 * DN_W), row(DN_W), row(DN_W), acc, acc],
        out_shape=[SDS((SEQ, 3 * DN_W), F32), SDS((SEQ, DN_W), F32), SDS((SEQ, DN_W), F32),
                   SDS((1, DN_W), F32), SDS((1, DN_W), F32)],
        compiler_params=_cparams(("arbitrary",)),
    )(qkv, rest, rest, alog_b, dtb_b, du, dw, dqg, dkd, dintra, degl)


def _inter_args(u_ref, w_ref, qg_ref, kd_ref, in_ref, egl_ref, n, state):
    f = lambda r: _heads(r).astype(F32)
    intra = jnp.stack([in_ref[:, DN_C * hd:DN_C * hd + DN_C] for hd in range(DN_H)], axis=0).astype(F32)
    egl = _heads(egl_ref.at[pl.ds(n & (DN_G - 1), 1), :])
    return f(u_ref), f(w_ref), f(qg_ref), f(kd_ref), intra, egl, state


def dn_inter_fwd(u, w, qg, kdec, intra, egl, name, exch=None):
    def body(u_ref, w_ref, qg_ref, kd_ref, in_ref, egl_ref, o_ref, st_ref, state_scr):
        n = pl.program_id(0)

        @pl.when(n == 0)
        def _():
            state_scr[...] = jnp.zeros_like(state_scr)

        st = state_scr[...]
        st_ref[0] = st
        o, ns = _dn_inter(*_inter_args(u_ref, w_ref, qg_ref, kd_ref, in_ref, egl_ref, n, st))
        _put_heads(o_ref, o)
        state_scr[...] = ns

    row = lambda w_: pl.BlockSpec((DN_C, w_), lambda n: (n, 0))
    return _call(
        body, name=name, grid=(N_CHUNK,),
        in_specs=[row(DN_W), row(DN_W), row(DN_W), row(DN_W), row(DN_H * DN_C),
                  pl.BlockSpec((DN_G, DN_W), lambda n: (n >> DN_G_LOG2, 0))],
        out_specs=[row(DN_W), pl.BlockSpec((1, DN_H, DN_E, DN_E), lambda n: (n, 0, 0, 0))],
        out_shape=[SDS((SEQ, DN_W), F32), SDS((N_CHUNK, DN_H, DN_E, DN_E), F32)],
        scratch_shapes=[pltpu.VMEM((DN_H, DN_E, DN_E), F32)],
        sem=("arbitrary",), args=(u, w, qg, kdec, intra, egl), exch=exch)


def dn_inter_bwd(u, w, qg, kdec, intra, egl, states, do, name):
    last = N_CHUNK - 1

    def body(u_ref, w_ref, qg_ref, kd_ref, in_ref, egl_ref, st_ref, do_ref,
             du_ref, dw_ref, dqg_ref, dkd_ref, din_ref, degl_ref, dstate_scr):
        s = pl.program_id(0)
        n = last - s

        @pl.when(s == 0)
        def _():
            dstate_scr[...] = jnp.zeros_like(dstate_scr)

        _out, vjp = jax.vjp(_dn_inter, *_inter_args(u_ref, w_ref, qg_ref, kd_ref, in_ref, egl_ref, n, st_ref[0]))
        du, dw, dqg, dkd, din, degl, dst = vjp((_heads(do_ref), dstate_scr[...]))
        _put_heads(du_ref, du)
        _put_heads(dw_ref, dw)
        _put_heads(dqg_ref, dqg)
        _put_heads(dkd_ref, dkd)
        for hd in range(DN_H):
            din_ref[:, DN_C * hd:DN_C * hd + DN_C] = din[hd]
        row = n & (DN_G - 1)

        @pl.when(row == DN_G - 1)
        def _():
            degl_ref[...] = jnp.zeros_like(degl_ref)

        new_row = jnp.concatenate([degl[hd] for hd in range(DN_H)], axis=1)
        rows = lax.broadcasted_iota(jnp.int32, (DN_G, DN_W), 0)
        degl_ref[...] = jnp.where(rows == row, jnp.broadcast_to(new_row, (DN_G, DN_W)), degl_ref[...])
        dstate_scr[...] = dst

    rev = lambda w_: pl.BlockSpec((DN_C, w_), lambda s: (last - s, 0))
    grp = pl.BlockSpec((DN_G, DN_W), lambda s: ((last - s) >> DN_G_LOG2, 0))
    return pl.pallas_call(
        body, name=name, grid=(N_CHUNK,),
        in_specs=[rev(DN_W), rev(DN_W), rev(DN_W), rev(DN_W), rev(DN_H * DN_C), grp,
                  pl.BlockSpec((1, DN_H, DN_E, DN_E), lambda s: (last - s, 0, 0, 0)), rev(DN_W)],
        out_specs=[rev(DN_W), rev(DN_W), rev(DN_W), rev(DN_W), rev(DN_H * DN_C), grp],
        out_shape=[SDS((SEQ, DN_W), F32)] * 4 + [SDS((SEQ, DN_H * DN_C), F32), SDS((N_CHUNK, DN_W), F32)],
        scratch_shapes=[pltpu.VMEM((DN_H, DN_E, DN_E), F32)],
        compiler_params=_cparams(("arbitrary",)),
    )(u, w, qg, kdec, intra, egl, states, do)


OUT_T = 256


def _pool_consts(rows_total, t0, halo_before):
    lane = lax.broadcasted_iota(jnp.int32, (rows_total, POOL_W), 1)
    row = lax.broadcasted_iota(jnp.int32, (rows_total, POOL_W), 0)
    grp = (lane >= 64).astype(jnp.int32) + (lane >= 128).astype(jnp.int32) + (lane >= 192).astype(jnp.int32)
    win = jnp.where(grp == 0, 2, jnp.where(grp == 1, 4, jnp.where(grp == 2, 8, 16)))
    pos = t0 + row - halo_before
    cnt = jnp.minimum(pos + 1, win).astype(F32)
    return grp, cnt


def _pool_select(grp, s2, s4, s8, s16):
    return jnp.where(grp == 0, s2, jnp.where(grp == 1, s4, jnp.where(grp == 2, s8, s16)))


def _pooled(u_ext, t0):
    n = u_ext.shape[0]
    grp, cnt = _pool_consts(n, t0, POOL_HALO)
    s2 = u_ext + pltpu.roll(u_ext, 1, 0)
    s4 = s2 + pltpu.roll(s2, 2, 0)
    s8 = s4 + pltpu.roll(s4, 4, 0)
    s16 = s8 + pltpu.roll(s8, 8, 0)
    out = _pool_select(grp, s2, s4, s8, s16) / jnp.maximum(cnt, 1.0) - u_ext
    return out[POOL_HALO:, :]


def _merge_weights(l1, l4, l16):
    m = jnp.maximum(jnp.maximum(l1, l4), l16)
    e1 = jnp.exp(l1 - m)
    e4 = jnp.exp(l4 - m)
    e16 = jnp.exp(l16 - m)
    inv = 1.0 / (e1 + e4 + e16)
    return e1 * inv, e4 * inv, e16 * inv


def _out_parts(ol1_ref, ol4_ref, ol16_ref, pu_ref, puh_ref, odn_ref, z_ref, wbd_ref, i, t):
    w1, w4, w16 = _merge_weights(_slabs_load(ol1_ref, 2, 2), _slabs_load(ol4_ref, 2, 2), _slabs_load(ol16_ref, 2, 2))
    ya = w1 * _slabs_load(ol1_ref, 0, 2) + w4 * _slabs_load(ol4_ref, 0, 2) + w16 * _slabs_load(ol16_ref, 0, 2)
    halo = jnp.where(i > 0, puh_ref[...], 0.0)
    pooled = _pooled(jnp.concatenate([halo, pu_ref[...]], axis=0), i * t)
    pw = _dot(pooled.astype(BF16), wbd_ref[...])
    return ya, pooled, pw, (w1, w4, w16)


def _out_specs_common(t):
    def row(w, cb=0):
        return pl.BlockSpec((t, w), lambda i: (i, cb))

    halo = pl.BlockSpec((POOL_HALO, POOL_W),
                        lambda i: (jnp.maximum(i * (t // POOL_HALO) - 1, 0), R_PU // POOL_W))
    full = lambda a, b: pl.BlockSpec((a, b), lambda i: (0, 0))
    return [_slab_spec(4, t), _slab_spec(4, t), _slab_spec(4, t), row(POOL_W, R_PU // POOL_W), halo, row(DN_W), row(DN_W, R_DZ // DN_W),
            full(POOL_W, POOL_W), full(1, POOL_W), full(1, DN_W), full(D_MODEL, D_MODEL)]


def mix_out_fwd(x, ol1, ol4, ol16, rest, odn, wbd, scale, onorm_b, wout, name):
    t = OUT_T

    def body(x_ref, ol1_ref, ol4_ref, ol16_ref, pu_ref, puh_ref, odn_ref, z_ref, wbd_ref, sc_ref, on_ref, wo_ref, o_ref):
        i = pl.program_id(0)
        ya, _pooled_v, pw, _w = _out_parts(ol1_ref, ol4_ref, ol16_ref, pu_ref, puh_ref, odn_ref, z_ref, wbd_ref, i, t)
        yb = pw * sc_ref[...]
        acc = x_ref[...] + _dot(ya.astype(BF16), wo_ref[0:256, :]) + _dot(yb.astype(BF16), wo_ref[256:512, :])
        for hd in range(DN_H):
            sl = slice(DN_E * hd, DN_E * hd + DN_E)
            oh, _r = _rms_stats(odn_ref[:, sl])
            z = z_ref[:, sl]
            yc = oh * on_ref[:, sl] * (z * _sigmoid(z))
            acc = acc + _dot(yc.astype(BF16), wo_ref[512 + DN_E * hd:512 + DN_E * hd + DN_E, :])
        o_ref[...] = acc

    return pl.pallas_call(
        body, name=name, grid=(SEQ // t,),
        in_specs=[pl.BlockSpec((t, D_MODEL), lambda i: (i, 0))] + _out_specs_common(t),
        out_specs=pl.BlockSpec((t, D_MODEL), lambda i: (i, 0)),
        out_shape=SDS((SEQ, D_MODEL), F32),
        compiler_params=_cparams(("arbitrary",)),
    )(x, ol1, ol4, ol16, rest, rest, odn, rest, wbd, scale, onorm_b, wout)


def mix_out_bwd(dxo, ol1, ol4, ol16, rest, odn, wbd, scale, onorm_b, wout, headsum, name):
    t = OUT_T

    def body(dxo_ref, ol1_ref, ol4_ref, ol16_ref, pu_ref, puh_ref, odn_ref, z_ref, wbd_ref, sc_ref, on_ref, wo_ref, hs_ref,
             dwo_ref, d1_ref, d4_ref, d16_ref, dpl_ref, dodn_ref, dz_ref, dsc_ref, don_ref, dwbd_ref):
        i = pl.program_id(0)

        @pl.when(i == 0)
        def _():
            dwo_ref[...] = jnp.zeros_like(dwo_ref)
            dsc_ref[...] = jnp.zeros_like(dsc_ref)
            don_ref[...] = jnp.zeros_like(don_ref)
            dwbd_ref[...] = jnp.zeros_like(dwbd_ref)

        ya, pooled, pw, (w1, w4, w16) = _out_parts(ol1_ref, ol4_ref, ol16_ref, pu_ref, puh_ref, odn_ref, z_ref, wbd_ref, i, t)
        sc = sc_ref[...]
        dxb = dxo_ref[...].astype(BF16)
        dwo_ref[0:256, :] += _dot_tn(ya.astype(BF16), dxb)
        dwo_ref[256:512, :] += _dot_tn((pw * sc).astype(BF16), dxb)
        dya = _dot_nt(dxb, wo_ref[0:256, :])
        o1 = _slabs_load(ol1_ref, 0, 2)
        o4 = _slabs_load(ol4_ref, 0, 2)
        o16 = _slabs_load(ol16_ref, 0, 2)
        hs = hs_ref[...]
        s1 = _dot(dya * o1, hs, HI)
        s4 = _dot(dya * o4, hs, HI)
        s16 = _dot(dya * o16, hs, HI)
        sbar = w1 * s1 + w4 * s4 + w16 * s16
        _slabs_store(d1_ref, 0, w1 * dya)
        _slabs_store(d1_ref, 2, w1 * (s1 - sbar))
        _slabs_store(d4_ref, 0, w4 * dya)
        _slabs_store(d4_ref, 2, w4 * (s4 - sbar))
        _slabs_store(d16_ref, 0, w16 * dya)
        _slabs_store(d16_ref, 2, w16 * (s16 - sbar))
        dyb = _dot_nt(dxb, wo_ref[256:512, :])
        dsc_ref[...] += jnp.sum(dyb * pw, axis=0, keepdims=True)
        dpw = (dyb * sc).astype(BF16)
        dwbd_ref[...] += _dot_tn(pooled.astype(BF16), dpw)
        dpl_ref[...] = _dot_nt(dpw, wbd_ref[...])
        for hd in range(DN_H):
            sl = slice(DN_E * hd, DN_E * hd + DN_E)
            rows_w = slice(512 + DN_E * hd, 512 + DN_E * hd + DN_E)
            oh, r = _rms_stats(odn_ref[:, sl])
            z = z_ref[:, sl]
            sg = _sigmoid(z)
            sz = z * sg
            nw = on_ref[:, sl]
            on = oh * nw
            dwo_ref[rows_w, :] += _dot_tn((on * sz).astype(BF16), dxb)
            dyc = _dot_nt(dxb, wo_ref[rows_w, :])
            dz_ref[:, sl] = dyc * on * (sg * (1.0 + z * (1.0 - sg)))
            dx, dw = _rms_bwd(oh, r, nw, dyc * sz)
            dodn_ref[:, sl] = dx
            don_ref[:, sl] += dw

    row = lambda w: pl.BlockSpec((t, w), lambda i: (i, 0))
    full = lambda a, b: pl.BlockSpec((a, b), lambda i: (0, 0))
    return pl.pallas_call(
        body, name=name, grid=(SEQ // t,),
        in_specs=[row(D_MODEL)] + _out_specs_common(t) + [full(ATT_W, ATT_W)],
        out_specs=[full(D_MODEL, D_MODEL), _slab_spec(4, t), _slab_spec(4, t), _slab_spec(4, t), row(POOL_W), row(DN_W), row(DN_W),
                   full(1, POOL_W), full(1, DN_W), full(POOL_W, POOL_W)],
        out_shape=[SDS((D_MODEL, D_MODEL), F32), SDS((4, SEQ, 128), F32), SDS((4, SEQ, 128), F32), SDS((4, SEQ, 128), F32),
                   SDS((SEQ, POOL_W), F32), SDS((SEQ, DN_W), F32), SDS((SEQ, DN_W), F32),
                   SDS((1, POOL_W), F32), SDS((1, DN_W), F32), SDS((POOL_W, POOL_W), F32)],
        compiler_params=_cparams(("arbitrary",)),
    )(dxo, ol1, ol4, ol16, rest, rest, odn, rest, wbd, scale, onorm_b, wout, headsum)


def pool_bwd(dpooled, name):
    t = 512
    nt = SEQ // t

    def body(d_ref, dn_ref, o_ref):
        i = pl.program_id(0)
        halo = jnp.where(i < nt - 1, dn_ref[...], 0.0)
        d_ext = jnp.concatenate([d_ref[...], halo], axis=0)
        n = t + POOL_HALO
        grp, cnt = _pool_consts(n, i * t, 0)
        dq = d_ext / cnt
        s2 = dq + pltpu.roll(dq, n - 1, 0)
        s4 = s2 + pltpu.roll(s2, n - 2, 0)
        s8 = s4 + pltpu.roll(s4, n - 4, 0)
        s16 = s8 + pltpu.roll(s8, n - 8, 0)
        o_ref[...] = (_pool_select(grp, s2, s4, s8, s16) - d_ext)[0:t, :]

    return pl.pallas_call(
        body, name=name, grid=(nt,),
        in_specs=[pl.BlockSpec((t, POOL_W), lambda i: (i, 0)),
                  pl.BlockSpec((POOL_HALO, POOL_W),
                               lambda i: (jnp.minimum((i + 1) * (t // POOL_HALO), SEQ // POOL_HALO - 1), 0))],
        out_specs=pl.BlockSpec((t, POOL_W), lambda i: (i, 0)),
        out_shape=SDS((SEQ, POOL_W), F32),
        compiler_params=_cparams(("arbitrary",)),
    )(dpooled, dpooled)


N_PEER = N_DEV - 1
ANY_SPEC = pl.BlockSpec(memory_space=pl.ANY)


class Exchange:
    def __init__(self, arrays, mode):
        self.arrays = list(arrays)
        self.mode = mode
        n = len(self.arrays)
        if mode == "scatter":
            self.out_shape = [SDS(a.shape, a.dtype) for a in self.arrays]
        else:
            self.out_shape = [SDS((N_DEV,) + a.shape, a.dtype) for a in self.arrays]
        self.scratch = [pltpu.SemaphoreType.DMA((n * N_PEER,)), pltpu.SemaphoreType.DMA((n * N_PEER,)),
                        pltpu.SemaphoreType.DMA((n,))]

    @staticmethod
    def _place():
        x, y, c = lax.axis_index("x"), lax.axis_index("y"), lax.axis_index("c")
        chips = [(1 - x, y), (x, 1 - y), (1 - x, 1 - y)]
        return x, y, c, chips

    @staticmethod
    def _copy(sems, a, k, src, dst, to):
        send_sems, recv_sems, _ = sems
        return pltpu.make_async_remote_copy(
            src_ref=src, dst_ref=dst, send_sem=send_sems.at[a * N_PEER + k], recv_sem=recv_sems.at[a * N_PEER + k],
            device_id=to, device_id_type=MESH)

    def _scatter_peers(self):
        x, y, c, _ = self._place()
        out = []
        for fx, fy, fc in ((0, 0, 1), (1, 0, 0), (0, 1, 0), (1, 1, 0), (1, 0, 1), (0, 1, 1), (1, 1, 1)):
            px, py, pc = x ^ fx, y ^ fy, c ^ fc
            out.append(((px, py, pc), 4 * px + 2 * py + pc))
        return 4 * x + 2 * y + c, out

    def _local(self, ins, outs, sems, a, me):
        src = ins[a].at[me] if self.mode == "scatter" else ins[a]
        return pltpu.make_async_copy(src, outs[a].at[me], sems[2].at[a])

    def start(self, ins, outs, sems):
        if self.mode == "scatter":
            me, peers = self._scatter_peers()
            for a in range(len(ins)):
                self._local(ins, outs, sems, a, me).start()
                for k, (peer, pidx) in enumerate(peers):
                    self._copy(sems, a, k, ins[a].at[pidx], outs[a].at[me], peer).start()
            return
        x, y, c, chips = self._place()
        me = 4 * x + 2 * y + c
        for a in range(len(ins)):
            self._local(ins, outs, sems, a, me).start()
            self._copy(sems, a, 0, ins[a], outs[a].at[me], (x, y, 1 - c)).start()
            for j, (cx, cy) in enumerate(chips):
                self._copy(sems, a, 1 + j, ins[a], outs[a].at[me], (cx, cy, c)).start()

    def finish(self, ins, outs, sems):
        n = len(ins)
        if self.mode == "scatter":
            me, peers = self._scatter_peers()
            for a in range(n):
                for k, (peer, pidx) in enumerate(peers):
                    self._copy(sems, a, k, ins[a].at[pidx], outs[a].at[pidx], peer).wait_recv()
            for a in range(n):
                for k, (peer, pidx) in enumerate(peers):
                    self._copy(sems, a, k, ins[a].at[pidx], outs[a].at[me], peer).wait_send()
                self._local(ins, outs, sems, a, me).wait()
            return
        x, y, c, chips = self._place()
        me = 4 * x + 2 * y + c
        sib = (x, y, 1 - c)
        for a in range(n):
            for j, (cx, cy) in enumerate(chips):
                blk = outs[a].at[4 * cx + 2 * cy + c]
                self._copy(sems, a, 1 + j, ins[a], blk, (cx, cy, c)).wait_recv()
                self._copy(sems, a, 4 + j, blk, blk, sib).start()
        for a in range(n):
            self._copy(sems, a, 0, ins[a], outs[a].at[4 * x + 2 * y + (1 - c)], sib).wait_recv()
            for j, (cx, cy) in enumerate(chips):
                blk = outs[a].at[4 * cx + 2 * cy + (1 - c)]
                self._copy(sems, a, 4 + j, blk, blk, sib).wait_recv()
        for a in range(n):
            for k in range(N_PEER):
                self._copy(sems, a, k, ins[a], outs[a].at[me], sib).wait_send()
            self._local(ins, outs, sems, a, me).wait()


def run_exchange(exch, name):
    n = len(exch.arrays)

    def body(*refs):
        ins, outs, sems = refs[:n], refs[n:2 * n], refs[2 * n:]
        exch.start(ins, outs, sems)
        exch.finish(ins, outs, sems)

    return pl.pallas_call(
        body, name=name, in_specs=[ANY_SPEC] * n, out_specs=[ANY_SPEC] * n, out_shape=exch.out_shape,
        scratch_shapes=exch.scratch,
    )(*exch.arrays)


def _call(body, *, name, grid, in_specs, out_specs, out_shape, scratch_shapes, sem, args, exch=None):
    if exch is None:
        res = pl.pallas_call(body, name=name, grid=grid, in_specs=in_specs, out_specs=out_specs, out_shape=out_shape,
                             scratch_shapes=scratch_shapes, compiler_params=_cparams(sem))(*args)
        return res, None
    single = not isinstance(out_shape, (list, tuple))
    out_specs_l = [out_specs] if single else list(out_specs)
    out_shape_l = [out_shape] if single else list(out_shape)
    n_in, n_out, n_scr, m = len(in_specs), len(out_specs_l), len(scratch_shapes), len(exch.arrays)

    def wrapped(*refs):
        p = 0
        ins = refs[p:p + n_in]; p += n_in
        xin = refs[p:p + m]; p += m
        outs = refs[p:p + n_out]; p += n_out
        xout = refs[p:p + m]; p += m
        scr = refs[p:p + n_scr]; p += n_scr
        sems = refs[p:]
        ids = [pl.program_id(ax) for ax in range(len(grid))]
        first = functools.reduce(jnp.logical_and, [i == 0 for i in ids])
        last = functools.reduce(jnp.logical_and, [i == g - 1 for i, g in zip(ids, grid)])

        @pl.when(first)
        def _():
            exch.start(xin, xout, sems)

        body(*ins, *outs, *scr)

        @pl.when(last)
        def _():
            exch.finish(xin, xout, sems)

    res = pl.pallas_call(
        wrapped, name=name, grid=grid, in_specs=list(in_specs) + [ANY_SPEC] * m,
        out_specs=out_specs_l + [ANY_SPEC] * m, out_shape=out_shape_l + exch.out_shape,
        scratch_shapes=list(scratch_shapes) + exch.scratch, compiler_params=_cparams(sem),
    )(*args, *exch.arrays)
    outs = res[:n_out]
    return (outs[0] if single else outs), res[n_out:]


def _adam_math(w, g, m, v):
    m2 = ADAM_B1 * m + (1.0 - ADAM_B1) * g
    v2 = ADAM_B2 * v + (1.0 - ADAM_B2) * (g * g)
    m_hat = m2 / (1.0 - ADAM_B1 ** ADAM_STEP)
    v_hat = v2 / (1.0 - ADAM_B2 ** ADAM_STEP)
    delta = -ADAM_LR * (m_hat / (jnp.sqrt(v_hat) + ADAM_EPS) + ADAM_WD * w)
    return delta, m2, v2


ADAM_ROW_BLOCKS = 2


def adam_shard(parts0, parts1, w, m, v, name):
    _, r, c = w.shape
    rb = r // ADAM_ROW_BLOCKS

    def body(p0_ref, p1_ref, w_ref, m_ref, v_ref, g_ref, d_ref, m2_ref, v2_ref):
        def run(p_ref):
            g = p_ref[0].astype(F32)
            for i in range(1, N_DEV):
                g = g + p_ref[i].astype(F32)
            delta, m2, v2 = _adam_math(w_ref[0], g, m_ref[0], v_ref[0])
            g_ref[0] = g
            d_ref[0] = delta
            m2_ref[0] = m2
            v2_ref[0] = v2

        @pl.when(pl.program_id(0) == 0)
        def _():
            run(p0_ref)

        @pl.when(pl.program_id(0) == 1)
        def _():
            run(p1_ref)

    def p_spec(layer):
        row = (lambda l, j: jnp.where(l == 0, j, ADAM_ROW_BLOCKS - 1)) if layer == 0 else (lambda l, j: jnp.where(l == 1, j, 0))
        return pl.BlockSpec((N_DEV, rb, c), lambda l, j: (0, row(l, j), 0))

    blk = pl.BlockSpec((1, rb, c), lambda l, j: (l, j, 0))
    return pl.pallas_call(
        body, name=name, grid=(DEPTH, ADAM_ROW_BLOCKS),
        in_specs=[p_spec(0), p_spec(1), blk, blk, blk], out_specs=[blk] * 4,
        out_shape=[SDS(w.shape, F32)] * 4,
        compiler_params=_cparams(("arbitrary", "arbitrary")),
    )(parts0, parts1, w, m, v)


def parts_sum(parts0, parts1, name):
    _, r, c = parts0.shape

    def body(p0_ref, p1_ref, g_ref):
        def run(p_ref):
            g = p_ref[0].astype(F32)
            for i in range(1, N_DEV):
                g = g + p_ref[i].astype(F32)
            g_ref[0] = g

        @pl.when(pl.program_id(0) == 0)
        def _():
            run(p0_ref)

        @pl.when(pl.program_id(0) == 1)
        def _():
            run(p1_ref)

    full = pl.BlockSpec((N_DEV, r, c), lambda l: (0, 0, 0))
    return pl.pallas_call(
        body, name=name, grid=(DEPTH,), in_specs=[full, full],
        out_specs=pl.BlockSpec((1, r, c), lambda l: (l, 0, 0)), out_shape=SDS((DEPTH, r, c), F32),
        compiler_params=_cparams(("arbitrary",)),
    )(parts0, parts1)


def adam_given(g, w, m, v, name):
    _, r, c = w.shape

    def body(g_ref, w_ref, m_ref, v_ref, d_ref, m2_ref, v2_ref):
        delta, m2, v2 = _adam_math(w_ref[0], g_ref[0], m_ref[0], v_ref[0])
        d_ref[0] = delta
        m2_ref[0] = m2
        v2_ref[0] = v2

    blk = pl.BlockSpec((1, r, c), lambda l: (l, 0, 0))
    return pl.pallas_call(
        body, name=name, grid=(DEPTH,), in_specs=[blk] * 4, out_specs=[blk] * 3, out_shape=[SDS(w.shape, F32)] * 3,
        compiler_params=_cparams(("arbitrary",)),
    )(g, w, m, v)


def adam_small(parts, w, m, v, name):
    def body(p_ref, w_ref, m_ref, v_ref, g_ref, d_ref, m2_ref, v2_ref):
        g = p_ref[0]
        for i in range(1, N_DEV):
            g = g + p_ref[i]
        delta, m2, v2 = _adam_math(w_ref[...], g, m_ref[...], v_ref[...])
        g_ref[...] = g
        d_ref[...] = delta
        m2_ref[...] = m2
        v2_ref[...] = v2

    return pl.pallas_call(
        body, name=name, out_shape=[SDS(w.shape, F32)] * 4, compiler_params=_cparams(),
    )(parts, w, m, v)


def _rot_cols(w):
    w4 = w.reshape(w.shape[0], 4, 2, 32)
    return jnp.stack([-w4[:, :, 1], w4[:, :, 0]], axis=2).reshape(w.shape[0], ATT_W)


def _rot_cols_t(dw_rot):
    d4 = dw_rot.reshape(dw_rot.shape[0], 4, 2, 32)
    return jnp.stack([d4[:, :, 1], -d4[:, :, 0]], axis=2).reshape(dw_rot.shape[0], ATT_W)


def build_wext(w_in):
    aq, ak, av, pu = w_in[:, 0:256], w_in[:, 256:512], w_in[:, 512:768], w_in[:, 768:1024]
    dqkvz = w_in[:, 1024:3072]
    gates = jnp.repeat(w_in[:, 3072:3080], DN_E, axis=1)
    return jnp.concatenate([aq, ak, av, _rot_cols(aq), _rot_cols(ak), dqkvz, gates, pu], axis=1)


def fold_dwext(d):
    b = EXT_ATT
    aq = d[:, 0:256] + _rot_cols_t(d[:, 768:1024])
    ak = d[:, 256:512] + _rot_cols_t(d[:, 1024:1280])
    av = d[:, 512:768]
    dqkvz = d[:, b:b + 2048]
    gates = d[:, b + R_BB:b + R_BB + 1024].reshape(d.shape[0], 8, DN_E).sum(axis=-1)
    pu = d[:, b + R_PU:b + R_PU + 256]
    return jnp.concatenate([aq, ak, av, pu, dqkvz, gates], axis=1)


def _block_diag(pw):
    z = jnp.zeros((4, 64, 4, 64), pw.dtype)
    for g in range(4):
        z = z.at[g, :, g, :].set(pw[g])
    return z.reshape(POOL_W, POOL_W)


def _diag_blocks(m):
    m4 = m.reshape(4, 64, 4, 64)
    return jnp.stack([m4[g, :, g, :] for g in range(4)], axis=0)


def _lanes(v, reps):
    return jnp.repeat(v, reps)[None, :]


def layer_fwd(p, xa, cos, sin, l, host=None):
    host = host or {}

    def carried(key):
        return host[key][0] if key in host else None

    def done(key, xo):
        if key in host:
            host[key][1](xo)

    xb, xo = ffn_fwd(xa, p["n1"], *p["f1"], f"ffn1_fwd_{l}", carried("ffn1"))
    done("ffn1", xo)
    att, rest = mix_in_fwd(xb, p["nm"], p["wext"], cos, sin, f"mix_in_fwd_{l}")
    ols = [att_fwd_s(att, d, f"att_fwd_{l}_{d}") for d in DILATIONS]
    qkv = dn_prep_fwd(rest, p["conv"], f"dn_prep_fwd_{l}")
    dn, xo = dn_intra_fwd(qkv, rest, p["alog"], p["dtb"], f"dn_intra_fwd_{l}", carried("dn_intra"))
    done("dn_intra", xo)
    (odn, states), xo = dn_inter_fwd(*dn, f"dn_inter_fwd_{l}", carried("dn_inter"))
    done("dn_inter", xo)
    xc = mix_out_fwd(xb, ols[0], ols[1], ols[2], rest, odn, p["wbd"], p["scale"], p["onorm"], p["wout"], f"mix_out_fwd_{l}")
    xd, xo = ffn_fwd(xc, p["n2"], *p["f2"], f"ffn2_fwd_{l}", carried("ffn2"))
    done("ffn2", xo)
    return xd, dict(xa=xa, xb=xb, xc=xc, att=att, rest=rest, ols=ols, qkv=qkv, dn=dn, odn=odn, states=states)


def layer_bwd(p, s, dx, cos, sin, headsum, l, scatter=False, carry=None):
    blocks = lambda ws: [w_.reshape(N_DEV, FF_BLK, D_MODEL) for w_ in ws]
    (dx, *d_f2, d_n2), carried = ffn_bwd(s["xc"], dx, p["n2"], *p["f2"], f"ffn2_bwd_{l}", carry)
    (d_wout, dol1, dol4, dol16, dpooled, dodn, dz, dscale, donorm, dwbd) = mix_out_bwd(
        dx, s["ols"][0], s["ols"][1], s["ols"][2], s["rest"], s["odn"], p["wbd"], p["scale"], p["onorm"], p["wout"],
        headsum, f"mix_out_bwd_{l}")
    dpu = pool_bwd(dpooled, f"pool_bwd_{l}")
    f2 = blocks(d_f2)
    d_dn = dn_inter_bwd(*s["dn"], s["states"], dodn, f"dn_inter_bwd_{l}")
    dqkv, dbb, dab, dalog, ddtb = dn_intra_bwd(s["qkv"], s["rest"], p["alog"], p["dtb"], *d_dn, f"dn_intra_bwd_{l}")
    d_dqkv, dconv = dn_prep_bwd(s["rest"], p["conv"], dqkv, f"dn_prep_bwd_{l}")
    datts = [att_bwd_s(s["att"], ol, dol, d, f"att_bwd_{l}_{d}")
             for d, ol, dol in zip(DILATIONS, s["ols"], (dol1, dol4, dol16))]
    dproj = assemble_dproj(datts, cos, sin, d_dqkv, dz, dbb, dab, dpu, f"assemble_dproj_{l}")
    dx, d_wext, d_nm = linear_bwd(s["xb"], dx, p["nm"], dproj, p["wext"], f"mix_in_bwd_{l}")
    d_win = fold_dwext(d_wext).reshape(D_MODEL, N_DEV, IN_BLK).transpose(1, 0, 2).astype(BF16)
    io = [d_win, d_wout.reshape(N_DEV, D_MODEL // N_DEV, D_MODEL).astype(BF16)]
    (dx, *d_f1, d_n1), xo = ffn_bwd(s["xa"], dx, p["n1"], *p["f1"], f"ffn1_bwd_{l}",
                                    Exchange(f2 + io, "scatter") if scatter else None)
    if scatter:
        f2, io = list(xo[:3]), list(xo[3:])
    big = dict(f1=blocks(d_f1), f2=f2, io=io)
    small = dict(ffn1_norm=d_n1[0], mix_norm=d_nm[0], ffn2_norm=d_n2[0], pool_w=_diag_blocks(dwbd),
                 pool_scale=dscale[0], dn_a_log=dalog.reshape(DN_H, DN_E).sum(-1),
                 dn_dt_bias=ddtb.reshape(DN_H, DN_E).sum(-1),
                 dn_out_norm=donorm.reshape(DN_H, DN_E).sum(0), dn_conv_w=dconv)
    return dx, big, small, carried


def small_operands(l, pool_w, pool_scale, dn_out_norm, dn_a_log, dn_dt_bias, ffn1_norm, mix_norm, ffn2_norm):
    return dict(
        wbd=_block_diag(pool_w[l]).astype(BF16),
        scale=pool_scale[l][None, :],
        onorm=jnp.tile(dn_out_norm[l], DN_H)[None, :],
        alog=_lanes(dn_a_log[l], DN_E),
        dtb=_lanes(dn_dt_bias[l], DN_E),
        n1=ffn1_norm[l][None, :], nm=mix_norm[l][None, :], n2=ffn2_norm[l][None, :])


def set_mixer_weights(p, win_g, wout_g, conv_g):
    p["wext"] = build_wext(win_g.transpose(1, 0, 2).reshape(D_MODEL, IN_W))
    p["wout"] = wout_g.reshape(D_MODEL, D_MODEL)
    p["conv"] = conv_g.transpose(1, 0, 2).reshape(DN_CONV, 3 * DN_W)


def rope_tables(pos):
    inv_freq = 10000.0 ** (-jnp.arange(0, ATT_E, 2, dtype=F32) / ATT_E)
    ang = pos.astype(F32)[:, None] * inv_freq
    return jnp.tile(jnp.cos(ang), (1, 8)), jnp.tile(jnp.sin(ang), (1, 8))


def head_sum_matrix():
    return jnp.kron(jnp.eye(4, dtype=F32), jnp.ones((ATT_E, ATT_E), F32))


SMALL_NAMES = ("ffn1_norm", "mix_norm", "ffn2_norm", "pool_w", "pool_scale", "dn_a_log", "dn_dt_bias",
               "dn_out_norm", "final_norm", "dn_conv_w")


PACK_UNIT = 8 * 128


def _pack_rows(n):
    return -(-n // PACK_UNIT) * 8


def _pack(parts):
    rows = []
    for p in parts:
        flat = p.reshape(-1)
        r = _pack_rows(flat.shape[0])
        rows.append(jnp.pad(flat, (0, r * 128 - flat.shape[0])).reshape(r, 128))
    return jnp.concatenate(rows, axis=0)


def _unpack(packed, shapes):
    out, row = [], 0
    for s in shapes:
        n = math.prod(s)
        r = _pack_rows(n)
        out.append(packed[row:row + r].reshape(-1)[:n].reshape(s))
        row += r
    return out


def kernel(x, positions, ffn1_norm, ffn1_w_gate, ffn1_w_up, ffn1_w_down, mix_norm, w_in, pool_w, pool_scale, dn_conv_w, dn_a_log, dn_dt_bias, dn_out_norm, w_out, ffn2_norm, ffn2_w_gate, ffn2_w_up, ffn2_w_down, final_norm, loss_target, m_ffn1_norm, m_ffn1_w_gate, m_ffn1_w_up, m_ffn1_w_down, m_mix_norm, m_w_in, m_pool_w, m_pool_scale, m_dn_conv_w, m_dn_a_log, m_dn_dt_bias, m_dn_out_norm, m_w_out, m_ffn2_norm, m_ffn2_w_gate, m_ffn2_w_up, m_ffn2_w_down, m_final_norm, v_ffn1_norm, v_ffn1_w_gate, v_ffn1_w_up, v_ffn1_w_down, v_mix_norm, v_w_in, v_pool_w, v_pool_scale, v_dn_conv_w, v_dn_a_log, v_dn_dt_bias, v_dn_out_norm, v_w_out, v_ffn2_norm, v_ffn2_w_gate, v_ffn2_w_up, v_ffn2_w_down, v_final_norm):
    me = 4 * lax.axis_index("x") + 2 * lax.axis_index("y") + lax.axis_index("c")
    x0 = x[0]
    target = loss_target[0]

    cos, sin = rope_tables(positions[0])
    headsum = head_sum_matrix()

    layers = [small_operands(l, pool_w, pool_scale, dn_out_norm, dn_a_log, dn_dt_bias, ffn1_norm, mix_norm, ffn2_norm)
              for l in range(DEPTH)]

    def whole(gathered):
        return gathered.reshape(D_FF, D_MODEL)

    def gather_ffn1(l):
        def on_done(xo):
            layers[l]["f1"] = tuple(whole(g) for g in xo)
        return Exchange(ffn_shard_operands(ffn1_w_gate[l], ffn1_w_up[l], ffn1_w_down[l]), "gather"), on_done

    def gather_mixer(l):
        def on_done(xo):
            set_mixer_weights(layers[l], *xo)
        return Exchange([w_in[l].astype(BF16), w_out[l].astype(BF16), dn_conv_w[l]], "gather"), on_done

    gathered_f2 = {}

    def gather_ffn2_part(l, part):
        def on_done(xo):
            gathered_f2[(l, part)] = [whole(g) for g in xo]
            if (l, 0) in gathered_f2 and (l, 1) in gathered_f2:
                layers[l]["f2"] = tuple(gathered_f2[(l, 0)] + gathered_f2[(l, 1)])
        ops = ffn_shard_operands(ffn2_w_gate[l], ffn2_w_up[l], ffn2_w_down[l])
        return Exchange(ops[:2] if part == 0 else ops[2:], "gather"), on_done

    first, on_first = gather_ffn1(0)
    on_first(run_exchange(first, "gather_ffn1_0"))
    saved = []
    xa = x0
    for l in range(DEPTH):
        host = {"ffn1": gather_mixer(l), "dn_intra": gather_ffn2_part(l, 0), "dn_inter": gather_ffn2_part(l, 1)}
        if l + 1 < DEPTH:
            host["ffn2"] = gather_ffn1(l + 1)
        xa, s = layer_fwd(layers[l], xa, cos, sin, l, host)
        saved.append(s)

    loss_row, dx, d_final = loss_head(xa, final_norm[None, :], target, "loss_head")
    loss = lax.psum(loss_row[0, 0], ("x", "y", "c"))

    small = {}
    big_parts = [None] * DEPTH
    carry = None
    for l in reversed(range(DEPTH)):
        dx, big, small[l], carried = layer_bwd(layers[l], saved[l], dx, cos, sin, headsum, l, True, carry)
        if carried is not None:
            big_parts[l + 1]["f1"] = list(carried)
        big_parts[l] = big
        carry = Exchange(big["f1"], "scatter")
    big_parts[0]["f1"] = list(run_exchange(carry, "scatter_ffn1_0"))
    grad_x = dx[None]

    small_shapes = {"ffn1_norm": (DEPTH, D_MODEL), "mix_norm": (DEPTH, D_MODEL), "ffn2_norm": (DEPTH, D_MODEL),
                    "pool_w": (DEPTH, 4, 64, 64), "pool_scale": (DEPTH, POOL_W), "dn_a_log": (DEPTH, DN_H),
                    "dn_dt_bias": (DEPTH, DN_H), "dn_out_norm": (DEPTH, DN_E), "final_norm": (D_MODEL,),
                    "dn_conv_w": (DEPTH, DN_CONV, 3 * DN_W)}
    g_small = {n: (d_final[0] if n == "final_norm" else jnp.stack([small[l][n] for l in range(DEPTH)]))
               for n in SMALL_NAMES}
    (small_parts,) = run_exchange(Exchange([_pack([g_small[n] for n in SMALL_NAMES])], "gather"), "gather_small_grads")

    def conv_full(a):
        return lax.dynamic_update_slice(jnp.zeros((DEPTH, DN_CONV, 3 * DN_W), F32), a, (0, 0, me * (3 * DN_W // N_DEV)))

    given = dict(ffn1_norm=(ffn1_norm, m_ffn1_norm, v_ffn1_norm), mix_norm=(mix_norm, m_mix_norm, v_mix_norm),
                 ffn2_norm=(ffn2_norm, m_ffn2_norm, v_ffn2_norm), pool_w=(pool_w, m_pool_w, v_pool_w),
                 pool_scale=(pool_scale, m_pool_scale, v_pool_scale), dn_a_log=(dn_a_log, m_dn_a_log, v_dn_a_log),
                 dn_dt_bias=(dn_dt_bias, m_dn_dt_bias, v_dn_dt_bias),
                 dn_out_norm=(dn_out_norm, m_dn_out_norm, v_dn_out_norm),
                 final_norm=(final_norm, m_final_norm, v_final_norm),
                 dn_conv_w=(conv_full(dn_conv_w), conv_full(m_dn_conv_w), conv_full(v_dn_conv_w)))
    packed_wmv = [_pack([given[n][k] for n in SMALL_NAMES]) for k in range(3)]
    small_out = adam_small(small_parts, *packed_wmv, "adam_small")
    shapes = [small_shapes[n] for n in SMALL_NAMES]
    small_res = {n: [] for n in SMALL_NAMES}
    for arr in small_out:
        for n, v_ in zip(SMALL_NAMES, _unpack(arr, shapes)):
            if n == "dn_conv_w":
                v_ = lax.dynamic_slice(v_, (0, 0, me * (3 * DN_W // N_DEV)), (DEPTH, DN_CONV, 3 * DN_W // N_DEV))
            small_res[n].append(v_)

    def parts_of(group, idx):
        return [big_parts[l][group][idx] for l in range(DEPTH)]

    def adam_transposed(group, idx, w, m, v, name):
        g = parts_sum(*parts_of(group, idx), f"sum_{name}").transpose(0, 2, 1)
        return [g] + list(adam_given(g, w, m, v, f"adam_{name}"))

    big_res = dict(
        ffn1_w_gate=adam_transposed("f1", 0, ffn1_w_gate, m_ffn1_w_gate, v_ffn1_w_gate, "ffn1_gate"),
        ffn1_w_up=adam_transposed("f1", 1, ffn1_w_up, m_ffn1_w_up, v_ffn1_w_up, "ffn1_up"),
        ffn1_w_down=adam_shard(*parts_of("f1", 2), ffn1_w_down, m_ffn1_w_down, v_ffn1_w_down, "adam_ffn1_down"),
        ffn2_w_gate=adam_transposed("f2", 0, ffn2_w_gate, m_ffn2_w_gate, v_ffn2_w_gate, "ffn2_gate"),
        ffn2_w_up=adam_transposed("f2", 1, ffn2_w_up, m_ffn2_w_up, v_ffn2_w_up, "ffn2_up"),
        ffn2_w_down=adam_shard(*parts_of("f2", 2), ffn2_w_down, m_ffn2_w_down, v_ffn2_w_down, "adam_ffn2_down"),
        w_in=adam_shard(*parts_of("io", 0), w_in, m_w_in, v_w_in, "adam_w_in"),
        w_out=adam_shard(*parts_of("io", 1), w_out, m_w_out, v_w_out, "adam_w_out"),
    )

    order = ("ffn1_norm", "ffn1_w_gate", "ffn1_w_up", "ffn1_w_down", "mix_norm", "w_in", "pool_w", "pool_scale",
             "dn_conv_w", "dn_a_log", "dn_dt_bias", "dn_out_norm", "w_out", "ffn2_norm", "ffn2_w_gate", "ffn2_w_up",
             "ffn2_w_down", "final_norm")
    res = {**small_res, **big_res}
    outs = [loss, grad_x]
    for k in range(4):
        outs.extend(res[n][k] for n in order)
    return tuple(outs)
```

```python
import functools
import math

import jax
import jax.numpy as jnp
from jax import lax
from jax.experimental import pallas as pl
from jax.experimental.pallas import tpu as pltpu

F32 = jnp.float32
BF16 = jnp.bfloat16
HI = lax.Precision.HIGHEST
INV_PREC = lax.Precision.HIGH
SDS = jax.ShapeDtypeStruct

N_DEV = 8
SEQ = 4096
D_MODEL = 1024
DEPTH = 2
D_FF = 2816
FF_BLK = D_FF // N_DEV
ATT_W = 256
ATT_E = 64
ATT_BLK = 128
DILATIONS = (1, 4, 16)
POOL_W = 256
POOL_HALO = 16
DN_W = 512
DN_H = 4
DN_E = 128
DN_C = 64
N_CHUNK = SEQ // DN_C
IN_W = 3080
IN_BLK = IN_W // N_DEV
EPS = 1e-6
EXT_ATT = 1280
GATE_W = 256
EXT_REST = 4 * DN_W + GATE_W + POOL_W
EXT_W = EXT_ATT + EXT_REST
R_DQKV, R_DZ, R_G, R_PU = 0, 1536, 2048, 2304

ADAM_LR, ADAM_B1, ADAM_B2, ADAM_EPS, ADAM_WD, ADAM_STEP = 0.001, 0.9, 0.999, 1e-08, 0.01, 10

VMEM_LIMIT = 60 * 1024 * 1024
MESH = pl.DeviceIdType.MESH


def _cparams(sem=None):
    kw = dict(vmem_limit_bytes=VMEM_LIMIT)
    if sem is not None:
        kw["dimension_semantics"] = sem
    return pltpu.CompilerParams(**kw)


def _dot(a, b, prec=None):
    return jnp.dot(a, b, preferred_element_type=F32, precision=prec)


def _dot_nt(a, b, prec=None):
    return lax.dot_general(a, b, (((1,), (1,)), ((), ())), preferred_element_type=F32, precision=prec)


def _dot_tn(a, b, prec=None):
    return lax.dot_general(a, b, (((0,), (0,)), ((), ())), preferred_element_type=F32, precision=prec)


def _sigmoid(x):
    return jax.nn.sigmoid(x)


def _rms_stats(x):
    r = lax.rsqrt(jnp.mean(x * x, axis=-1, keepdims=True) + EPS)
    return x * r, r


def _rms_bwd(xh, r, w, dh):
    dxh = dh * w
    dx = r * (dxh - xh * jnp.mean(dxh * xh, axis=-1, keepdims=True))
    return dx, jnp.sum(dh * xh, axis=0, keepdims=True)


FFN_T_FWD = 2048
FFN_T_BWD = 512
FF_TILE = 256
N_FF_TILE = D_FF // FF_TILE


def ffn_shard_operands(gate, up, down):
    return [gate.T.astype(BF16), up.T.astype(BF16), down.astype(BF16)]


def ffn_fwd(x, nw, wgt, wut, wd, name, exch=None):
    t = FFN_T_FWD

    def body(x_ref, nw_ref, wgt_ref, wut_ref, wd_ref, o_ref, h_scr, acc_scr):
        k = pl.program_id(1)

        @pl.when(k == 0)
        def _():
            xh, _r = _rms_stats(x_ref[...])
            h_scr[...] = (xh * nw_ref[...]).astype(BF16)
            acc_scr[...] = jnp.zeros_like(acc_scr)

        h = h_scr[...]
        hg = _dot_nt(h, wgt_ref[...])
        hu = _dot_nt(h, wut_ref[...])
        a = (hg * _sigmoid(hg) * hu).astype(BF16)
        acc_scr[...] += _dot(a, wd_ref[...])

        @pl.when(k == N_FF_TILE - 1)
        def _():
            o_ref[...] = x_ref[...] + 0.5 * acc_scr[...]

    w_spec = pl.BlockSpec((FF_TILE, D_MODEL), lambda i, k: (k, 0))
    return _call(
        body, name=name, grid=(SEQ // t, N_FF_TILE),
        in_specs=[pl.BlockSpec((t, D_MODEL), lambda i, k: (i, 0)),
                  pl.BlockSpec((1, D_MODEL), lambda i, k: (0, 0)), w_spec, w_spec, w_spec],
        out_specs=pl.BlockSpec((t, D_MODEL), lambda i, k: (i, 0)),
        out_shape=SDS((SEQ, D_MODEL), F32),
        scratch_shapes=[pltpu.VMEM((t, D_MODEL), BF16), pltpu.VMEM((t, D_MODEL), F32)],
        sem=("arbitrary", "arbitrary"), args=(x, nw, wgt, wut, wd), exch=exch)


def ffn_bwd(x, dxo, nw, wgt, wut, wd, name, exch=None):
    t = FFN_T_BWD
    nt = SEQ // t

    def body(x_ref, dxo_ref, nw_ref, wgt_ref, wut_ref, wd_ref, dx_ref, dwgt_ref, dwut_ref, dwd_ref, dnw_ref,
             dh_scr, ag_scr, au_scr, ad_scr, h_scr):
        k = pl.program_id(0)
        i = pl.program_id(1)
        rows = pl.ds(pl.multiple_of(i * t, t), t)
        nw_v = nw_ref[...]

        @pl.when(k == 0)
        def _():
            xh0, _r0 = _rms_stats(x_ref[...])
            h_scr[rows, :] = (xh0 * nw_v).astype(BF16)

        h = h_scr[rows, :]
        dy = (0.5 * dxo_ref[...]).astype(BF16)
        wgt = wgt_ref[...]
        wut = wut_ref[...]
        hg = _dot_nt(h, wgt)
        hu = _dot_nt(h, wut)
        sg = _sigmoid(hg)
        sil = hg * sg
        a = (sil * hu).astype(BF16)
        da = _dot_nt(dy, wd_ref[...])
        dhu = (da * sil).astype(BF16)
        dhg = (da * hu * (sg * (1.0 + hg * (1.0 - sg)))).astype(BF16)
        p_d = _dot_tn(a, dy)
        p_g = _dot_tn(dhg, h)
        p_u = _dot_tn(dhu, h)
        dh = _dot(dhg, wgt) + _dot(dhu, wut)

        @pl.when(i == 0)
        def _():
            ad_scr[...] = p_d
            ag_scr[...] = p_g
            au_scr[...] = p_u

        @pl.when(i > 0)
        def _():
            ad_scr[...] += p_d
            ag_scr[...] += p_g
            au_scr[...] += p_u

        @pl.when(i == nt - 1)
        def _():
            dwd_ref[...] = ad_scr[...].astype(BF16)
            dwgt_ref[...] = ag_scr[...].astype(BF16)
            dwut_ref[...] = au_scr[...].astype(BF16)

        @pl.when(k == 0)
        def _():
            dh_scr[rows, :] = dh

        @pl.when(k > 0)
        def _():
            dh_scr[rows, :] += dh

        @pl.when(jnp.logical_and(k == 0, i == 0))
        def _():
            dnw_ref[...] = jnp.zeros_like(dnw_ref)

        @pl.when(k == N_FF_TILE - 1)
        def _():
            xh, r = _rms_stats(x_ref[...])
            dx, dw = _rms_bwd(xh, r, nw_v, dh_scr[rows, :])
            dx_ref[...] = dxo_ref[...] + dx
            dnw_ref[...] += dw

    last = N_FF_TILE - 1
    w_spec = pl.BlockSpec((FF_TILE, D_MODEL), lambda k, i: (k, 0))
    return _call(
        body, name=name, grid=(N_FF_TILE, nt),
        in_specs=[pl.BlockSpec((t, D_MODEL), lambda k, i: (i, 0)),
                  pl.BlockSpec((t, D_MODEL), lambda k, i: (i, 0)),
                  pl.BlockSpec((1, D_MODEL), lambda k, i: (0, 0)), w_spec, w_spec, w_spec],
        out_specs=[pl.BlockSpec((t, D_MODEL), lambda k, i: (jnp.where(k == last, i, 0), 0)),
                   w_spec, w_spec, w_spec, pl.BlockSpec((1, D_MODEL), lambda k, i: (0, 0))],
        out_shape=[SDS((SEQ, D_MODEL), F32), SDS((D_FF, D_MODEL), BF16), SDS((D_FF, D_MODEL), BF16),
                   SDS((D_FF, D_MODEL), BF16), SDS((1, D_MODEL), F32)],
        scratch_shapes=[pltpu.VMEM((SEQ, D_MODEL), F32), pltpu.VMEM((FF_TILE, D_MODEL), F32),
                        pltpu.VMEM((FF_TILE, D_MODEL), F32), pltpu.VMEM((FF_TILE, D_MODEL), F32),
                        pltpu.VMEM((SEQ, D_MODEL), BF16)],
        sem=("arbitrary", "arbitrary"), args=(x, dxo, nw, wgt, wut, wd), exch=exch)


def loss_head(x, fw, target, name):
    t = 512

    def body(x_ref, fw_ref, tg_ref, loss_ref, dx_ref, dfw_ref):
        i = pl.program_id(0)
        xh, r = _rms_stats(x_ref[...])
        w = fw_ref[...]
        err = xh * w - tg_ref[...]
        part = 0.5 * jnp.sum(jnp.sum(err * err, axis=-1, keepdims=True), axis=0, keepdims=True) / D_MODEL
        dx, dw = _rms_bwd(xh, r, w, err * (1.0 / D_MODEL))
        dx_ref[...] = dx

        @pl.when(i == 0)
        def _():
            loss_ref[...] = jnp.zeros_like(loss_ref)
            dfw_ref[...] = jnp.zeros_like(dfw_ref)

        loss_ref[...] += jnp.broadcast_to(part, loss_ref.shape)
        dfw_ref[...] += dw

    return pl.pallas_call(
        body, name=name, grid=(SEQ // t,),
        in_specs=[pl.BlockSpec((t, D_MODEL), lambda i: (i, 0)),
                  pl.BlockSpec((1, D_MODEL), lambda i: (0, 0)),
                  pl.BlockSpec((t, D_MODEL), lambda i: (i, 0))],
        out_specs=[pl.BlockSpec((1, 128), lambda i: (0, 0)),
                   pl.BlockSpec((t, D_MODEL), lambda i: (i, 0)),
                   pl.BlockSpec((1, D_MODEL), lambda i: (0, 0))],
        out_shape=[SDS((1, 128), F32), SDS((SEQ, D_MODEL), F32), SDS((1, D_MODEL), F32)],
        compiler_params=_cparams(("arbitrary",)),
    )(x, fw, target)


MIX_T = 256


def _slabs_load(ref, first, n):
    return jnp.concatenate([ref[first + j] for j in range(n)], axis=1)


def _slabs_store(ref, first, val):
    for j in range(val.shape[1] // 128):
        ref[first + j] = val[:, 128 * j:128 * j + 128]


def _slab_spec(k, t):
    return pl.BlockSpec((k, t, 128), lambda i: (0, i, 0))


def mix_in_fwd(x, nw, wext, cos, sin, name):
    t = MIX_T

    def body(x_ref, nw_ref, w_ref, cos_ref, sin_ref, att_ref, rest_ref):
        xh, _r = _rms_stats(x_ref[...])
        h = (xh * nw_ref[...]).astype(BF16)
        pa = _dot(h, w_ref[:, 0:EXT_ATT])
        c = cos_ref[...]
        s = sin_ref[...]
        _slabs_store(att_ref, 0, pa[:, 0:256] * c + pa[:, 768:1024] * s)
        _slabs_store(att_ref, 2, pa[:, 256:512] * c + pa[:, 1024:1280] * s)
        _slabs_store(att_ref, 4, pa[:, 512:768])
        for j in range(EXT_REST // 256):
            rest_ref[:, 256 * j:256 * j + 256] = _dot(h, w_ref[:, EXT_ATT + 256 * j:EXT_ATT + 256 * j + 256])

    return pl.pallas_call(
        body, name=name, grid=(SEQ // t,),
        in_specs=[pl.BlockSpec((t, D_MODEL), lambda i: (i, 0)),
                  pl.BlockSpec((1, D_MODEL), lambda i: (0, 0)),
                  pl.BlockSpec((D_MODEL, EXT_W), lambda i: (0, 0)),
                  pl.BlockSpec((t, ATT_W), lambda i: (i, 0)),
                  pl.BlockSpec((t, ATT_W), lambda i: (i, 0))],
        out_specs=[_slab_spec(6, t),
                   pl.BlockSpec((t, EXT_REST), lambda i: (i, 0))],
        out_shape=[SDS((6, SEQ, 128), F32), SDS((SEQ, EXT_REST), F32)],
        compiler_params=_cparams(("arbitrary",)),
    )(x, nw, wext, cos, sin)


def assemble_dproj(datts, cos, sin, d_dqkv, dz, dg, dpu, name):
    t = 512

    def body(d1_ref, d4_ref, d16_ref, cos_ref, sin_ref, dqkv_ref, dz_ref, dg_ref, dpu_ref, o_ref):
        da6 = d1_ref[...] + d4_ref[...] + d16_ref[...]
        da = jnp.concatenate([da6[j] for j in range(6)], axis=1)
        c = cos_ref[...]
        s = sin_ref[...]
        dq = da[:, 0:256]
        dk = da[:, 256:512]
        o_ref[:, 0:256] = (dq * c).astype(BF16)
        o_ref[:, 256:512] = (dk * c).astype(BF16)
        o_ref[:, 512:768] = da[:, 512:768].astype(BF16)
        o_ref[:, 768:1024] = (dq * s).astype(BF16)
        o_ref[:, 1024:1280] = (dk * s).astype(BF16)
        b = EXT_ATT
        o_ref[:, b + R_DQKV:b + R_DQKV + 1536] = dqkv_ref[...].astype(BF16)
        o_ref[:, b + R_DZ:b + R_DZ + 512] = dz_ref[...].astype(BF16)
        o_ref[:, b + R_G:b + R_G + GATE_W] = dg_ref[...].astype(BF16)
        o_ref[:, b + R_PU:b + R_PU + 256] = dpu_ref[...].astype(BF16)

    row = lambda w: pl.BlockSpec((t, w), lambda i: (i, 0))
    return pl.pallas_call(
        body, name=name, grid=(SEQ // t,),
        in_specs=[_slab_spec(6, t), _slab_spec(6, t), _slab_spec(6, t),
                  row(256), row(256), row(1536), row(512), row(GATE_W), row(256)],
        out_specs=row(EXT_W),
        out_shape=SDS((SEQ, EXT_W), BF16),
        compiler_params=_cparams(("arbitrary",)),
    )(*datts, cos, sin, d_dqkv, dz, dg, dpu)


def linear_bwd(x, dxo, nw, dy, w, name):
    t = 512
    nb = 768
    n = w.shape[1]
    nt = SEQ // t
    nn = n // nb

    def body(x_ref, dxo_ref, nw_ref, dy_ref, w_ref, dx_ref, dw_ref, dnw_ref, dh_scr, h_scr):
        k = pl.program_id(0)
        i = pl.program_id(1)
        rows = pl.ds(pl.multiple_of(i * t, t), t)
        nw_v = nw_ref[...]

        @pl.when(k == 0)
        def _():
            xh0, _r0 = _rms_stats(x_ref[...])
            h_scr[rows, :] = (xh0 * nw_v).astype(BF16)

        h = h_scr[rows, :]
        dyv = dy_ref[...]
        p_w = _dot_tn(h, dyv)
        dh = _dot_nt(dyv, w_ref[...])

        @pl.when(i == 0)
        def _():
            dw_ref[...] = p_w

        @pl.when(i > 0)
        def _():
            dw_ref[...] += p_w

        @pl.when(k == 0)
        def _():
            dh_scr[rows, :] = dh

        @pl.when(k > 0)
        def _():
            dh_scr[rows, :] += dh

        @pl.when(jnp.logical_and(k == 0, i == 0))
        def _():
            dnw_ref[...] = jnp.zeros_like(dnw_ref)

        @pl.when(k == nn - 1)
        def _():
            xh, r = _rms_stats(x_ref[...])
            dx, dw = _rms_bwd(xh, r, nw_v, dh_scr[rows, :])
            dx_ref[...] = dxo_ref[...] + dx
            dnw_ref[...] += dw

    last = nn - 1
    return pl.pallas_call(
        body, name=name, grid=(nn, nt),
        in_specs=[pl.BlockSpec((t, D_MODEL), lambda k, i: (i, 0)),
                  pl.BlockSpec((t, D_MODEL), lambda k, i: (i, 0)),
                  pl.BlockSpec((1, D_MODEL), lambda k, i: (0, 0)),
                  pl.BlockSpec((t, nb), lambda k, i: (i, k)),
                  pl.BlockSpec((D_MODEL, nb), lambda k, i: (0, k))],
        out_specs=[pl.BlockSpec((t, D_MODEL), lambda k, i: (jnp.where(k == last, i, 0), 0)),
                   pl.BlockSpec((D_MODEL, nb), lambda k, i: (0, k)),
                   pl.BlockSpec((1, D_MODEL), lambda k, i: (0, 0))],
        out_shape=[SDS((SEQ, D_MODEL), F32), SDS((D_MODEL, n), F32), SDS((1, D_MODEL), F32)],
        scratch_shapes=[pltpu.VMEM((SEQ, D_MODEL), F32), pltpu.VMEM((SEQ, D_MODEL), BF16)],
        compiler_params=_cparams(("arbitrary", "arbitrary")),
    )(x, dxo, nw, dy, w)


def _att_masks():
    qi = lax.broadcasted_iota(jnp.int32, (ATT_BLK, ATT_BLK), 0)
    ki = lax.broadcasted_iota(jnp.int32, (ATT_BLK, ATT_BLK), 1)
    return ki <= qi, ki >= qi


NEG = -1e30


N_ATT_BLK = SEQ // ATT_BLK


def _class_rows(i, d):
    per_class = N_ATT_BLK // d
    shift = per_class.bit_length() - 1
    r = i >> shift
    j = i & (per_class - 1)
    span = ATT_BLK * d
    start = r + span * j
    prev = jnp.where(j == 0, start, start - span)
    nxt = jnp.where(j == per_class - 1, start, start + span)

    def rows(s0):
        if d == 1:
            return pl.ds(pl.multiple_of(s0, ATT_BLK), ATT_BLK)
        return pl.ds(s0, ATT_BLK, stride=d)

    return rows(start), rows(prev), rows(nxt), j != 0, j != per_class - 1


def _slab_heads(ref, slab, rows):
    x0 = ref[pl.ds(slab, 1), rows, :][0]
    x1 = ref[pl.ds(slab + 1, 1), rows, :][0]
    return jnp.stack([x0[:, 0:ATT_E], x0[:, ATT_E:], x1[:, 0:ATT_E], x1[:, ATT_E:]], axis=0)


def _put_slab_heads(ref, slab, rows, val):
    ref[pl.ds(slab, 1), rows, :] = jnp.concatenate([val[0], val[1]], axis=1)[None]
    ref[pl.ds(slab + 1, 1), rows, :] = jnp.concatenate([val[2], val[3]], axis=1)[None]


ATT_BLOCKS_PER_STEP = 2


def _resident_call(body, ins, out_slabs, name):
    n_in = len(ins)
    steps = N_ATT_BLK // ATT_BLOCKS_PER_STEP

    def wrapped(*refs):
        hbm_in, hbm_out = refs[:n_in], refs[n_in]
        vm_in, vm_out, sem = refs[n_in + 1:2 * n_in + 1], refs[2 * n_in + 1], refs[2 * n_in + 2]
        i = pl.program_id(0)

        @pl.when(i == 0)
        def _():
            copies = [pltpu.make_async_copy(h, v, sem.at[k]) for k, (h, v) in enumerate(zip(hbm_in, vm_in))]
            for cp in copies:
                cp.start()
            for cp in copies:
                cp.wait()

        for b in range(ATT_BLOCKS_PER_STEP):
            body(ATT_BLOCKS_PER_STEP * i + b, *vm_in, vm_out)

        @pl.when(i == steps - 1)
        def _():
            cp = pltpu.make_async_copy(vm_out, hbm_out, sem.at[n_in])
            cp.start()
            cp.wait()

    return pl.pallas_call(
        wrapped, name=name, grid=(steps,),
        in_specs=[ANY_SPEC] * n_in, out_specs=ANY_SPEC, out_shape=SDS((out_slabs, SEQ, 128), F32),
        scratch_shapes=[pltpu.VMEM(a.shape, a.dtype) for a in ins] + [pltpu.VMEM((out_slabs, SEQ, 128), F32),
                                                                      pltpu.SemaphoreType.DMA((n_in + 1,))],
        compiler_params=_cparams(("arbitrary",)),
    )(*ins)


def _att_fwd_math(ld, has_prev):
    m_d, m_p = _att_masks()
    m_p = jnp.logical_and(m_p, has_prev)
    q = ld("att", 0, "cur").astype(BF16)
    kc = ld("att", 2, "cur").astype(BF16)
    vc = ld("att", 4, "cur").astype(BF16)
    kp = ld("att", 2, "prev").astype(BF16)
    vp = ld("att", 4, "prev").astype(BF16)
    sd = jnp.where(m_d, _bdot(q, kc, 2, 2) * 0.125, NEG)
    sp = jnp.where(m_p, _bdot(q, kp, 2, 2) * 0.125, NEG)
    m = jnp.maximum(jnp.max(sd, axis=-1, keepdims=True), jnp.max(sp, axis=-1, keepdims=True))
    pd = jnp.exp(sd - m)
    pp = jnp.exp(sp - m)
    den = jnp.sum(pd, axis=-1, keepdims=True) + jnp.sum(pp, axis=-1, keepdims=True)
    inv = 1.0 / den
    o = _bdot((pd * inv).astype(BF16), vc, 2, 1) + _bdot((pp * inv).astype(BF16), vp, 2, 1)
    return o, jnp.broadcast_to(m + jnp.log(den), (4, ATT_BLK, ATT_E))


def _att_bwd_math(ld, has_prev, has_next):
    m_d, m_band = _att_masks()
    m_p = jnp.logical_and(m_band, has_prev)
    m_n = jnp.logical_and(m_band, has_next)

    def pair(q, k, v, lse, do, dterm, mask):
        s = jnp.where(mask, _bdot(q, k, 2, 2) * 0.125, NEG)
        p = jnp.exp(s - lse)
        dp = _bdot(do, v, 2, 2)
        ds = (p * (dp + dterm) * 0.125).astype(BF16)
        return p.astype(BF16), ds

    q_c = ld("att", 0, "cur").astype(BF16)
    k_c = ld("att", 2, "cur").astype(BF16)
    v_c = ld("att", 4, "cur").astype(BF16)
    k_p = ld("att", 2, "prev").astype(BF16)
    v_p = ld("att", 4, "prev").astype(BF16)
    q_n = ld("att", 0, "next").astype(BF16)
    o_c = ld("ol", 0, "cur")
    o_n = ld("ol", 0, "next")
    lse_c = ld("ol", 2, "cur")[:, :, 0:1]
    lse_n = ld("ol", 2, "next")[:, :, 0:1]
    do_c = ld("dol", 0, "cur")
    do_n = ld("dol", 0, "next")
    t_c = ld("dol", 2, "cur")[:, :, 0:1] - jnp.sum(do_c * o_c, axis=-1, keepdims=True)
    t_n = ld("dol", 2, "next")[:, :, 0:1] - jnp.sum(do_n * o_n, axis=-1, keepdims=True)
    do_cb = do_c.astype(BF16)
    do_nb = do_n.astype(BF16)
    p1, ds1 = pair(q_c, k_c, v_c, lse_c, do_cb, t_c, m_d)
    _p2, ds2 = pair(q_c, k_p, v_p, lse_c, do_cb, t_c, m_p)
    p3, ds3 = pair(q_n, k_c, v_c, lse_n, do_nb, t_n, m_n)
    return (_bdot(ds1, k_c, 2, 1) + _bdot(ds2, k_p, 2, 1), _bdot(ds1, q_c, 1, 1) + _bdot(ds3, q_n, 1, 1),
            _bdot(p1, do_cb, 1, 1) + _bdot(p3, do_nb, 1, 1))


ROWS_A = pl.ds(0, ATT_BLK)
ROWS_B = pl.ds(ATT_BLK, ATT_BLK)
N_ATT_PAIR = N_ATT_BLK // 2


def _pair_spec(k):
    return pl.BlockSpec((k, 2 * ATT_BLK, 128), lambda i: (0, i, 0))


def _before_pair_spec(k):
    return pl.BlockSpec((k, ATT_BLK, 128), lambda i: (0, jnp.maximum(2 * i - 1, 0), 0))


def _after_pair_spec(k):
    return pl.BlockSpec((k, ATT_BLK, 128), lambda i: (0, jnp.minimum(2 * i + 2, N_ATT_BLK - 1), 0))


def att_fwd_s(att, d, name):
    if d == 1:
        def body1(cur_ref, prev_ref, o_ref):
            i = pl.program_id(0)
            for rows, views, has_prev in (
                    (ROWS_A, {"cur": (cur_ref, ROWS_A), "prev": (prev_ref, ROWS_A)}, i != 0),
                    (ROWS_B, {"cur": (cur_ref, ROWS_B), "prev": (cur_ref, ROWS_A)}, True)):
                o, lse = _att_fwd_math(lambda _a, slab, where, v=views: _slab_heads(v[where][0], slab, v[where][1]), has_prev)
                _put_slab_heads(o_ref, 0, rows, o)
                _put_slab_heads(o_ref, 2, rows, lse)

        return pl.pallas_call(
            body1, name=name, grid=(N_ATT_PAIR,),
            in_specs=[_pair_spec(6), _before_pair_spec(6)], out_specs=_pair_spec(4),
            out_shape=SDS((4, SEQ, 128), F32), compiler_params=_cparams(("arbitrary",)),
        )(att, att)

    def body(i, att_ref, o_ref):
        cur, prev, _nxt, has_prev, _has_next = _class_rows(i, d)
        rows = {"cur": cur, "prev": prev}
        o, lse = _att_fwd_math(lambda _a, slab, where: _slab_heads(att_ref, slab, rows[where]), has_prev)
        _put_slab_heads(o_ref, 0, cur, o)
        _put_slab_heads(o_ref, 2, cur, lse)

    return _resident_call(body, [att], 4, name)


def att_bwd_s(att, ol, dol, d, name):
    if d == 1:
        def body1(a_p, a_c, a_n, ol_c, ol_n, dol_c, dol_n, d_ref):
            i = pl.program_id(0)
            first = {("att", "prev"): (a_p, ROWS_A), ("att", "cur"): (a_c, ROWS_A), ("att", "next"): (a_c, ROWS_B),
                     ("ol", "cur"): (ol_c, ROWS_A), ("ol", "next"): (ol_c, ROWS_B),
                     ("dol", "cur"): (dol_c, ROWS_A), ("dol", "next"): (dol_c, ROWS_B)}
            second = {("att", "prev"): (a_c, ROWS_A), ("att", "cur"): (a_c, ROWS_B), ("att", "next"): (a_n, ROWS_A),
                      ("ol", "cur"): (ol_c, ROWS_B), ("ol", "next"): (ol_n, ROWS_A),
                      ("dol", "cur"): (dol_c, ROWS_B), ("dol", "next"): (dol_n, ROWS_A)}
            for rows, views, has_prev, has_next in ((ROWS_A, first, i != 0, True),
                                                    (ROWS_B, second, True, i != N_ATT_PAIR - 1)):
                dq, dk, dv = _att_bwd_math(
                    lambda a, slab, where, v=views: _slab_heads(v[(a, where)][0], slab, v[(a, where)][1]), has_prev, has_next)
                _put_slab_heads(d_ref, 0, rows, dq)
                _put_slab_heads(d_ref, 2, rows, dk)
                _put_slab_heads(d_ref, 4, rows, dv)

        return pl.pallas_call(
            body1, name=name, grid=(N_ATT_PAIR,),
            in_specs=[_before_pair_spec(6), _pair_spec(6), _after_pair_spec(6), _pair_spec(4), _after_pair_spec(4),
                      _pair_spec(4), _after_pair_spec(4)],
            out_specs=_pair_spec(6), out_shape=SDS((6, SEQ, 128), F32), compiler_params=_cparams(("arbitrary",)),
        )(att, att, att, ol, ol, dol, dol)

    def body(i, att_ref, ol_ref, dol_ref, d_ref):
        cur, prev, nxt, has_prev, has_next = _class_rows(i, d)
        rows = {"cur": cur, "prev": prev, "next": nxt}
        refs = {"att": att_ref, "ol": ol_ref, "dol": dol_ref}
        dq, dk, dv = _att_bwd_math(lambda a, slab, where: _slab_heads(refs[a], slab, rows[where]), has_prev, has_next)
        _put_slab_heads(d_ref, 0, cur, dq)
        _put_slab_heads(d_ref, 2, cur, dk)
        _put_slab_heads(d_ref, 4, cur, dv)

    return _resident_call(body, [att, ol, dol], 6, name)


def _shift_down(x, k):
    rows = lax.broadcasted_iota(jnp.int32, x.shape, 0)
    return jnp.where(rows >= k, pltpu.roll(x, k, 0), 0.0)


def _shift_up(x, k):
    n = x.shape[0]
    rows = lax.broadcasted_iota(jnp.int32, x.shape, 0)
    return jnp.where(rows < n - k, pltpu.roll(x, n - k, 0), 0.0)


@functools.partial(jax.custom_vjp, nondiff_argnums=(1,))
def _delay(x, k):
    return _shift_down(x, k)


def _delay_fwd(x, k):
    return _shift_down(x, k), None


def _delay_bwd(k, _res, g):
    return (_shift_up(g, k),)


_delay.defvjp(_delay_fwd, _delay_bwd)

DN_CONV = 4


def _dn_prep_fn(u, w, kind):
    y = w[DN_CONV - 1:DN_CONV] * u
    for j in range(DN_CONV - 1):
        y = y + w[j:j + 1] * _delay(u, DN_CONV - 1 - j)
    y = y * _sigmoid(y)
    nrm = y * lax.rsqrt(jnp.sum(y * y, axis=-1, keepdims=True) + EPS)
    return jnp.where(kind == 0, nrm * (DN_E ** -0.5), jnp.where(kind == 1, nrm, y))


def dn_prep_fwd(rest, conv_w, name):
    def body(u_ref, w_ref, o_ref):
        j = pl.program_id(0)
        kind = (j >= DN_H).astype(jnp.int32) + (j >= 2 * DN_H).astype(jnp.int32)
        o_ref[...] = _dn_prep_fn(u_ref[...], w_ref[...], kind)

    return pl.pallas_call(
        body, name=name, grid=(3 * DN_H,),
        in_specs=[pl.BlockSpec((SEQ, DN_E), lambda j: (0, j)),
                  pl.BlockSpec((DN_CONV, DN_E), lambda j: (0, j))],
        out_specs=pl.BlockSpec((SEQ, DN_E), lambda j: (0, j)),
        out_shape=SDS((SEQ, 3 * DN_W), F32),
        compiler_params=_cparams(("arbitrary",)),
    )(rest, conv_w)


def dn_prep_bwd(rest, conv_w, dqkv, name):
    def body(u_ref, w_ref, g_ref, du_ref, dw_ref):
        j = pl.program_id(0)
        kind = (j >= DN_H).astype(jnp.int32) + (j >= 2 * DN_H).astype(jnp.int32)
        _y, vjp = jax.vjp(lambda u, w: _dn_prep_fn(u, w, kind), u_ref[...], w_ref[...])
        du, dw = vjp(g_ref[...])
        du_ref[...] = du
        dw_ref[...] = dw

    return pl.pallas_call(
        body, name=name, grid=(3 * DN_H,),
        in_specs=[pl.BlockSpec((SEQ, DN_E), lambda j: (0, j)),
                  pl.BlockSpec((DN_CONV, DN_E), lambda j: (0, j)),
                  pl.BlockSpec((SEQ, DN_E), lambda j: (0, j))],
        out_specs=[pl.BlockSpec((SEQ, DN_E), lambda j: (0, j)),
                   pl.BlockSpec((DN_CONV, DN_E), lambda j: (0, j))],
        out_shape=[SDS((SEQ, 3 * DN_W), F32), SDS((DN_CONV, 3 * DN_W), F32)],
        compiler_params=_cparams(("arbitrary",)),
    )(rest, conv_w, dqkv)


def _bdot(a, b, ca, cb, prec=None):
    return lax.dot_general(a, b, (((ca,), (cb,)), ((0,), (0,))), preferred_element_type=F32, precision=prec)


def _unit_lower_inverse(a):
    eye = (lax.broadcasted_iota(jnp.int32, (DN_C, DN_C), 0) == lax.broadcasted_iota(jnp.int32, (DN_C, DN_C), 1)).astype(F32)
    p = eye - a
    b = _bdot(a, a, 2, 1, INV_PREC)
    for lvl in range(5):
        p = p + _bdot(p, b, 2, 1, INV_PREC)
        if lvl < 4:
            b = _bdot(b, b, 2, 1, INV_PREC)
    return p


@jax.custom_vjp
def _tri_inv(a):
    return _unit_lower_inverse(a)


def _tri_inv_fwd(a):
    t = _unit_lower_inverse(a)
    return t, t


def _tri_inv_bwd(t, g):
    return (-_bdot(_bdot(t, g, 1, 1, INV_PREC), t, 2, 2, INV_PREC),)


_tri_inv.defvjp(_tri_inv_fwd, _tri_inv_bwd)


def _b16(x):
    return x.astype(BF16)


def _heads(ref, base=0):
    return jnp.stack([ref[:, base + DN_E * hd:base + DN_E * hd + DN_E] for hd in range(DN_H)], axis=0)


def _put_heads(ref, val, base=0):
    for hd in range(DN_H):
        ref[:, base + DN_E * hd:base + DN_E * hd + DN_E] = val[hd]


DN_G_LOG2 = 3
DN_G = 1 << DN_G_LOG2
N_INST = DN_G * DN_H


def _dn_intra(q, k, v, bb, ab, alog, dtb):
    ri = lax.broadcasted_iota(jnp.int32, (DN_C, DN_C), 0)
    ci = lax.broadcasted_iota(jnp.int32, (DN_C, DN_C), 1)
    lower = ri >= ci
    strict = ri > ci
    nh = q.shape[0]
    beta = _sigmoid(bb)
    xg = ab + dtb
    softplus = jnp.maximum(xg, 0.0) + jnp.log(1.0 + jnp.exp(-jnp.abs(xg)))
    gi = -jnp.exp(alog) * softplus
    g = _bdot(jnp.broadcast_to(lower.astype(F32), (nh, DN_C, DN_C)), gi, 2, 1, HI)
    eg = jnp.exp(g)
    kb = k * beta
    vb = v * beta
    g_col = g[:, :, 0:DN_C]
    g_row = _bdot(jnp.full((nh, DN_C, DN_E), 1.0 / DN_E, F32), g, 2, 2, HI)
    decay = jnp.where(lower, jnp.exp(jnp.where(lower, g_col - g_row, 0.0)), 0.0)
    kbf = _b16(k)
    a = jnp.where(strict, _bdot(_b16(kb), kbf, 2, 2) * decay, 0.0)
    tb = _b16(_tri_inv(a))
    u = _bdot(tb, _b16(vb), 2, 1)
    w = _bdot(tb, _b16(kb * eg), 2, 1)
    intra = jnp.where(lower, _bdot(_b16(q), kbf, 2, 2) * decay, 0.0)
    g_last = g[:, DN_C - 1:DN_C, :]
    return u, w, q * eg, k * jnp.exp(g_last - g), intra, jnp.exp(g_last)


def _dn_inter(u, w, qg, kdec, intra, egl, state):
    sb = _b16(state)
    v_new = u - _bdot(_b16(w), sb, 2, 1)
    o = _bdot(_b16(qg), sb, 2, 1) + _bdot(_b16(intra), _b16(v_new), 2, 1)
    return o, state * egl + _bdot(_b16(kdec), _b16(v_new), 1, 1)


def _inst(ref, base=0):
    per_head = [ref[:, base + DN_E * hd:base + DN_E * hd + DN_E].reshape(DN_G, DN_C, DN_E) for hd in range(DN_H)]
    return jnp.concatenate(per_head, axis=0)


def _inst_rows(ref):
    rows = [jnp.broadcast_to(ref[:, DN_E * hd:DN_E * hd + DN_E][None], (DN_G, 1, DN_E)) for hd in range(DN_H)]
    return jnp.concatenate(rows, axis=0)


def _put_inst(ref, val, width=DN_E, base=0):
    for hd in range(DN_H):
        ref[:, base + width * hd:base + width * hd + width] = val[DN_G * hd:DN_G * hd + DN_G].reshape(DN_G * DN_C, width)


def gate_spread_matrix():
    return jnp.pad(jnp.repeat(jnp.eye(2 * DN_H, dtype=F32), DN_E, axis=1), ((0, GATE_W - 2 * DN_H), (0, 0)))


def _dn_intra_from_gates(spread, q, k, v, gates, alog, dtb):
    wide = _dot(gates, spread, HI)
    inst = lambda base: jnp.concatenate(
        [wide[:, base + DN_E * hd:base + DN_E * hd + DN_E].reshape(DN_G, DN_C, DN_E) for hd in range(DN_H)], axis=0)
    return _dn_intra(q, k, v, inst(0), inst(DN_W), alog, dtb)


def _intra_args(qkv_ref, g_ref, alog_ref, dtb_ref):
    return (_inst(qkv_ref), _inst(qkv_ref, DN_W), _inst(qkv_ref, 2 * DN_W), g_ref[...],
            _inst_rows(alog_ref), _inst_rows(dtb_ref))


def _intra_in_specs():
    t = DN_G * DN_C
    return [pl.BlockSpec((t, 3 * DN_W), lambda n: (n, 0)),
            pl.BlockSpec((t, GATE_W), lambda n: (n, R_G // GATE_W)),
            pl.BlockSpec((GATE_W, 2 * DN_W), lambda n: (0, 0)),
            pl.BlockSpec((1, DN_W), lambda n: (0, 0)),
            pl.BlockSpec((1, DN_W), lambda n: (0, 0))]


def dn_intra_fwd(qkv, rest, spread, alog_b, dtb_b, name, exch=None):
    t = DN_G * DN_C

    def body(qkv_ref, g_ref, sp_ref, alog_ref, dtb_ref, u_ref, w_ref, qg_ref, kd_ref, in_ref, egl_ref):
        u, w, qg, kdec, intra, egl = _dn_intra_from_gates(sp_ref[...], *_intra_args(qkv_ref, g_ref, alog_ref, dtb_ref))
        _put_inst(u_ref, u)
        _put_inst(w_ref, w.astype(BF16))
        _put_inst(qg_ref, qg.astype(BF16))
        _put_inst(kd_ref, kdec.astype(BF16))
        _put_inst(in_ref, intra.astype(BF16), DN_C)
        for hd in range(DN_H):
            egl_ref[:, DN_E * hd:DN_E * hd + DN_E] = egl[DN_G * hd:DN_G * hd + DN_G].reshape(DN_G, DN_E)

    row = lambda w_: pl.BlockSpec((t, w_), lambda n: (n, 0))
    return _call(
        body, name=name, grid=(N_CHUNK // DN_G,), in_specs=_intra_in_specs(),
        out_specs=[row(DN_W), row(DN_W), row(DN_W), row(DN_W), row(DN_H * DN_C),
                   pl.BlockSpec((DN_G, DN_W), lambda n: (n, 0))],
        out_shape=[SDS((SEQ, DN_W), F32), SDS((SEQ, DN_W), BF16), SDS((SEQ, DN_W), BF16), SDS((SEQ, DN_W), BF16),
                   SDS((SEQ, DN_H * DN_C), BF16), SDS((N_CHUNK, DN_W), F32)],
        scratch_shapes=[], sem=("arbitrary",), args=(qkv, rest, spread, alog_b, dtb_b), exch=exch)


def dn_intra_bwd(qkv, rest, spread, alog_b, dtb_b, du, dw, dqg, dkd, dintra, degl, name):
    t = DN_G * DN_C

    def body(qkv_ref, g_ref, sp_ref, alog_ref, dtb_ref, du_ref, dw_ref, dqg_ref, dkd_ref, din_ref, degl_ref,
             dqkv_ref, dg_ref, dalog_ref, ddtb_ref):
        @pl.when(pl.program_id(0) == 0)
        def _():
            dalog_ref[...] = jnp.zeros_like(dalog_ref)
            ddtb_ref[...] = jnp.zeros_like(ddtb_ref)

        _out, vjp = jax.vjp(functools.partial(_dn_intra_from_gates, sp_ref[...]),
                            *_intra_args(qkv_ref, g_ref, alog_ref, dtb_ref))
        d_in = jnp.concatenate([din_ref[:, DN_C * hd:DN_C * hd + DN_C].reshape(DN_G, DN_C, DN_C) for hd in range(DN_H)], axis=0)
        d_egl = jnp.concatenate([degl_ref[:, DN_E * hd:DN_E * hd + DN_E].reshape(DN_G, 1, DN_E) for hd in range(DN_H)], axis=0)
        dq, dk, dv, dg, dalog, ddtb = vjp((_inst(du_ref), _inst(dw_ref), _inst(dqg_ref), _inst(dkd_ref), d_in, d_egl))
        _put_inst(dqkv_ref, dq)
        _put_inst(dqkv_ref, dk, DN_E, DN_W)
        _put_inst(dqkv_ref, dv, DN_E, 2 * DN_W)
        dg_ref[...] = dg
        for hd in range(DN_H):
            sl = slice(DN_E * hd, DN_E * hd + DN_E)
            dalog_ref[:, sl] += jnp.sum(dalog[DN_G * hd:DN_G * hd + DN_G], axis=0)
            ddtb_ref[:, sl] += jnp.sum(ddtb[DN_G * hd:DN_G * hd + DN_G], axis=0)

    row = lambda w_: pl.BlockSpec((t, w_), lambda n: (n, 0))
    acc = pl.BlockSpec((1, DN_W), lambda n: (0, 0))
    return pl.pallas_call(
        body, name=name, grid=(N_CHUNK // DN_G,),
        in_specs=_intra_in_specs() + [row(DN_W), row(DN_W), row(DN_W), row(DN_W), row(DN_H * DN_C),
                                      pl.BlockSpec((DN_G, DN_W), lambda n: (n, 0))],
        out_specs=[row(3 * DN_W), row(GATE_W), acc, acc],
        out_shape=[SDS((SEQ, 3 * DN_W), F32), SDS((SEQ, GATE_W), F32), SDS((1, DN_W), F32), SDS((1, DN_W), F32)],
        compiler_params=_cparams(("arbitrary",)),
    )(qkv, rest, spread, alog_b, dtb_b, du, dw, dqg, dkd, dintra, degl)


def _inter_args(u_ref, w_ref, qg_ref, kd_ref, in_ref, egl_ref, n, state):
    f = lambda r: _heads(r).astype(F32)
    intra = jnp.stack([in_ref[:, DN_C * hd:DN_C * hd + DN_C] for hd in range(DN_H)], axis=0).astype(F32)
    egl = _heads(egl_ref.at[pl.ds(n & (DN_G - 1), 1), :])
    return f(u_ref), f(w_ref), f(qg_ref), f(kd_ref), intra, egl, state


def dn_inter_fwd(u, w, qg, kdec, intra, egl, name, exch=None):
    def body(u_ref, w_ref, qg_ref, kd_ref, in_ref, egl_ref, o_ref, st_ref, state_scr):
        n = pl.program_id(0)

        @pl.when(n == 0)
        def _():
            state_scr[...] = jnp.zeros_like(state_scr)

        st = state_scr[...]
        st_ref[0] = st
        o, ns = _dn_inter(*_inter_args(u_ref, w_ref, qg_ref, kd_ref, in_ref, egl_ref, n, st))
        _put_heads(o_ref, o)
        state_scr[...] = ns

    row = lambda w_: pl.BlockSpec((DN_C, w_), lambda n: (n, 0))
    return _call(
        body, name=name, grid=(N_CHUNK,),
        in_specs=[row(DN_W), row(DN_W), row(DN_W), row(DN_W), row(DN_H * DN_C),
                  pl.BlockSpec((DN_G, DN_W), lambda n: (n >> DN_G_LOG2, 0))],
        out_specs=[row(DN_W), pl.BlockSpec((1, DN_H, DN_E, DN_E), lambda n: (n, 0, 0, 0))],
        out_shape=[SDS((SEQ, DN_W), F32), SDS((N_CHUNK, DN_H, DN_E, DN_E), F32)],
        scratch_shapes=[pltpu.VMEM((DN_H, DN_E, DN_E), F32)],
        sem=("arbitrary",), args=(u, w, qg, kdec, intra, egl), exch=exch)


def dn_inter_bwd(u, w, qg, kdec, intra, egl, states, do, name):
    last = N_CHUNK - 1

    def body(u_ref, w_ref, qg_ref, kd_ref, in_ref, egl_ref, st_ref, do_ref,
             du_ref, dw_ref, dqg_ref, dkd_ref, din_ref, degl_ref, dstate_scr):
        s = pl.program_id(0)
        n = last - s

        @pl.when(s == 0)
        def _():
            dstate_scr[...] = jnp.zeros_like(dstate_scr)

        _out, vjp = jax.vjp(_dn_inter, *_inter_args(u_ref, w_ref, qg_ref, kd_ref, in_ref, egl_ref, n, st_ref[0]))
        du, dw, dqg, dkd, din, degl, dst = vjp((_heads(do_ref), dstate_scr[...]))
        _put_heads(du_ref, du)
        _put_heads(dw_ref, dw)
        _put_heads(dqg_ref, dqg)
        _put_heads(dkd_ref, dkd)
        for hd in range(DN_H):
            din_ref[:, DN_C * hd:DN_C * hd + DN_C] = din[hd]
        row = n & (DN_G - 1)

        @pl.when(row == DN_G - 1)
        def _():
            degl_ref[...] = jnp.zeros_like(degl_ref)

        new_row = jnp.concatenate([degl[hd] for hd in range(DN_H)], axis=1)
        rows = lax.broadcasted_iota(jnp.int32, (DN_G, DN_W), 0)
        degl_ref[...] = jnp.where(rows == row, jnp.broadcast_to(new_row, (DN_G, DN_W)), degl_ref[...])
        dstate_scr[...] = dst

    rev = lambda w_: pl.BlockSpec((DN_C, w_), lambda s: (last - s, 0))
    grp = pl.BlockSpec((DN_G, DN_W), lambda s: ((last - s) >> DN_G_LOG2, 0))
    return pl.pallas_call(
        body, name=name, grid=(N_CHUNK,),
        in_specs=[rev(DN_W), rev(DN_W), rev(DN_W), rev(DN_W), rev(DN_H * DN_C), grp,
                  pl.BlockSpec((1, DN_H, DN_E, DN_E), lambda s: (last - s, 0, 0, 0)), rev(DN_W)],
        out_specs=[rev(DN_W), rev(DN_W), rev(DN_W), rev(DN_W), rev(DN_H * DN_C), grp],
        out_shape=[SDS((SEQ, DN_W), F32)] * 4 + [SDS((SEQ, DN_H * DN_C), F32), SDS((N_CHUNK, DN_W), F32)],
        scratch_shapes=[pltpu.VMEM((DN_H, DN_E, DN_E), F32)],
        compiler_params=_cparams(("arbitrary",)),
    )(u, w, qg, kdec, intra, egl, states, do)


OUT_T = 256


def _pool_consts(rows_total, t0, halo_before):
    lane = lax.broadcasted_iota(jnp.int32, (rows_total, POOL_W), 1)
    row = lax.broadcasted_iota(jnp.int32, (rows_total, POOL_W), 0)
    grp = (lane >= 64).astype(jnp.int32) + (lane >= 128).astype(jnp.int32) + (lane >= 192).astype(jnp.int32)
    win = jnp.where(grp == 0, 2, jnp.where(grp == 1, 4, jnp.where(grp == 2, 8, 16)))
    pos = t0 + row - halo_before
    cnt = jnp.minimum(pos + 1, win).astype(F32)
    return grp, cnt


def _pool_select(grp, s2, s4, s8, s16):
    return jnp.where(grp == 0, s2, jnp.where(grp == 1, s4, jnp.where(grp == 2, s8, s16)))


def _pooled(u_ext, t0):
    n = u_ext.shape[0]
    grp, cnt = _pool_consts(n, t0, POOL_HALO)
    s2 = u_ext + pltpu.roll(u_ext, 1, 0)
    s4 = s2 + pltpu.roll(s2, 2, 0)
    s8 = s4 + pltpu.roll(s4, 4, 0)
    s16 = s8 + pltpu.roll(s8, 8, 0)
    out = _pool_select(grp, s2, s4, s8, s16) / jnp.maximum(cnt, 1.0) - u_ext
    return out[POOL_HALO:, :]


def _merge_weights(l1, l4, l16):
    m = jnp.maximum(jnp.maximum(l1, l4), l16)
    e1 = jnp.exp(l1 - m)
    e4 = jnp.exp(l4 - m)
    e16 = jnp.exp(l16 - m)
    inv = 1.0 / (e1 + e4 + e16)
    return e1 * inv, e4 * inv, e16 * inv


def _out_parts(ol1_ref, ol4_ref, ol16_ref, pu_ref, puh_ref, odn_ref, z_ref, wbd_ref, i, t):
    w1, w4, w16 = _merge_weights(_slabs_load(ol1_ref, 2, 2), _slabs_load(ol4_ref, 2, 2), _slabs_load(ol16_ref, 2, 2))
    ya = w1 * _slabs_load(ol1_ref, 0, 2) + w4 * _slabs_load(ol4_ref, 0, 2) + w16 * _slabs_load(ol16_ref, 0, 2)
    halo = jnp.where(i > 0, puh_ref[...], 0.0)
    pooled = _pooled(jnp.concatenate([halo, pu_ref[...]], axis=0), i * t)
    pw = _dot(pooled.astype(BF16), wbd_ref[...])
    return ya, pooled, pw, (w1, w4, w16)


def _out_specs_common(t):
    def row(w, cb=0):
        return pl.BlockSpec((t, w), lambda i: (i, cb))

    halo = pl.BlockSpec((POOL_HALO, POOL_W),
                        lambda i: (jnp.maximum(i * (t // POOL_HALO) - 1, 0), R_PU // POOL_W))
    full = lambda a, b: pl.BlockSpec((a, b), lambda i: (0, 0))
    return [_slab_spec(4, t), _slab_spec(4, t), _slab_spec(4, t), row(POOL_W, R_PU // POOL_W), halo, row(DN_W), row(DN_W, R_DZ // DN_W),
            full(POOL_W, POOL_W), full(1, POOL_W), full(1, DN_W), full(D_MODEL, D_MODEL)]


def mix_out_fwd(x, ol1, ol4, ol16, rest, odn, wbd, scale, onorm_b, wout, name):
    t = OUT_T

    def body(x_ref, ol1_ref, ol4_ref, ol16_ref, pu_ref, puh_ref, odn_ref, z_ref, wbd_ref, sc_ref, on_ref, wo_ref, o_ref):
        i = pl.program_id(0)
        ya, _pooled_v, pw, _w = _out_parts(ol1_ref, ol4_ref, ol16_ref, pu_ref, puh_ref, odn_ref, z_ref, wbd_ref, i, t)
        yb = pw * sc_ref[...]
        acc = x_ref[...] + _dot(ya.astype(BF16), wo_ref[0:256, :]) + _dot(yb.astype(BF16), wo_ref[256:512, :])
        for hd in range(DN_H):
            sl = slice(DN_E * hd, DN_E * hd + DN_E)
            oh, _r = _rms_stats(odn_ref[:, sl])
            z = z_ref[:, sl]
            yc = oh * on_ref[:, sl] * (z * _sigmoid(z))
            acc = acc + _dot(yc.astype(BF16), wo_ref[512 + DN_E * hd:512 + DN_E * hd + DN_E, :])
        o_ref[...] = acc

    return pl.pallas_call(
        body, name=name, grid=(SEQ // t,),
        in_specs=[pl.BlockSpec((t, D_MODEL), lambda i: (i, 0))] + _out_specs_common(t),
        out_specs=pl.BlockSpec((t, D_MODEL), lambda i: (i, 0)),
        out_shape=SDS((SEQ, D_MODEL), F32),
        compiler_params=_cparams(("arbitrary",)),
    )(x, ol1, ol4, ol16, rest, rest, odn, rest, wbd, scale, onorm_b, wout)


def mix_out_bwd(dxo, ol1, ol4, ol16, rest, odn, wbd, scale, onorm_b, wout, headsum, name):
    t = OUT_T

    def body(dxo_ref, ol1_ref, ol4_ref, ol16_ref, pu_ref, puh_ref, odn_ref, z_ref, wbd_ref, sc_ref, on_ref, wo_ref, hs_ref,
             dwo_ref, d1_ref, d4_ref, d16_ref, dpl_ref, dodn_ref, dz_ref, dsc_ref, don_ref, dwbd_ref):
        i = pl.program_id(0)

        @pl.when(i == 0)
        def _():
            dwo_ref[...] = jnp.zeros_like(dwo_ref)
            dsc_ref[...] = jnp.zeros_like(dsc_ref)
            don_ref[...] = jnp.zeros_like(don_ref)
            dwbd_ref[...] = jnp.zeros_like(dwbd_ref)

        ya, pooled, pw, (w1, w4, w16) = _out_parts(ol1_ref, ol4_ref, ol16_ref, pu_ref, puh_ref, odn_ref, z_ref, wbd_ref, i, t)
        sc = sc_ref[...]
        dxb = dxo_ref[...].astype(BF16)
        dwo_ref[0:256, :] += _dot_tn(ya.astype(BF16), dxb)
        dwo_ref[256:512, :] += _dot_tn((pw * sc).astype(BF16), dxb)
        dya = _dot_nt(dxb, wo_ref[0:256, :])
        o1 = _slabs_load(ol1_ref, 0, 2)
        o4 = _slabs_load(ol4_ref, 0, 2)
        o16 = _slabs_load(ol16_ref, 0, 2)
        hs = hs_ref[...]
        s1 = _dot(dya * o1, hs, HI)
        s4 = _dot(dya * o4, hs, HI)
        s16 = _dot(dya * o16, hs, HI)
        sbar = w1 * s1 + w4 * s4 + w16 * s16
        _slabs_store(d1_ref, 0, w1 * dya)
        _slabs_store(d1_ref, 2, w1 * (s1 - sbar))
        _slabs_store(d4_ref, 0, w4 * dya)
        _slabs_store(d4_ref, 2, w4 * (s4 - sbar))
        _slabs_store(d16_ref, 0, w16 * dya)
        _slabs_store(d16_ref, 2, w16 * (s16 - sbar))
        dyb = _dot_nt(dxb, wo_ref[256:512, :])
        dsc_ref[...] += jnp.sum(dyb * pw, axis=0, keepdims=True)
        dpw = (dyb * sc).astype(BF16)
        dwbd_ref[...] += _dot_tn(pooled.astype(BF16), dpw)
        dpl_ref[...] = _dot_nt(dpw, wbd_ref[...])
        for hd in range(DN_H):
            sl = slice(DN_E * hd, DN_E * hd + DN_E)
            rows_w = slice(512 + DN_E * hd, 512 + DN_E * hd + DN_E)
            oh, r = _rms_stats(odn_ref[:, sl])
            z = z_ref[:, sl]
            sg = _sigmoid(z)
            sz = z * sg
            nw = on_ref[:, sl]
            on = oh * nw
            dwo_ref[rows_w, :] += _dot_tn((on * sz).astype(BF16), dxb)
            dyc = _dot_nt(dxb, wo_ref[rows_w, :])
            dz_ref[:, sl] = dyc * on * (sg * (1.0 + z * (1.0 - sg)))
            dx, dw = _rms_bwd(oh, r, nw, dyc * sz)
            dodn_ref[:, sl] = dx
            don_ref[:, sl] += dw

    row = lambda w: pl.BlockSpec((t, w), lambda i: (i, 0))
    full = lambda a, b: pl.BlockSpec((a, b), lambda i: (0, 0))
    return pl.pallas_call(
        body, name=name, grid=(SEQ // t,),
        in_specs=[row(D_MODEL)] + _out_specs_common(t) + [full(ATT_W, ATT_W)],
        out_specs=[full(D_MODEL, D_MODEL), _slab_spec(4, t), _slab_spec(4, t), _slab_spec(4, t), row(POOL_W), row(DN_W), row(DN_W),
                   full(1, POOL_W), full(1, DN_W), full(POOL_W, POOL_W)],
        out_shape=[SDS((D_MODEL, D_MODEL), F32), SDS((4, SEQ, 128), F32), SDS((4, SEQ, 128), F32), SDS((4, SEQ, 128), F32),
                   SDS((SEQ, POOL_W), F32), SDS((SEQ, DN_W), F32), SDS((SEQ, DN_W), F32),
                   SDS((1, POOL_W), F32), SDS((1, DN_W), F32), SDS((POOL_W, POOL_W), F32)],
        compiler_params=_cparams(("arbitrary",)),
    )(dxo, ol1, ol4, ol16, rest, rest, odn, rest, wbd, scale, onorm_b, wout, headsum)


def pool_bwd(dpooled, name):
    t = 512
    nt = SEQ // t

    def body(d_ref, dn_ref, o_ref):
        i = pl.program_id(0)
        halo = jnp.where(i < nt - 1, dn_ref[...], 0.0)
        d_ext = jnp.concatenate([d_ref[...], halo], axis=0)
        n = t + POOL_HALO
        grp, cnt = _pool_consts(n, i * t, 0)
        dq = d_ext / cnt
        s2 = dq + pltpu.roll(dq, n - 1, 0)
        s4 = s2 + pltpu.roll(s2, n - 2, 0)
        s8 = s4 + pltpu.roll(s4, n - 4, 0)
        s16 = s8 + pltpu.roll(s8, n - 8, 0)
        o_ref[...] = (_pool_select(grp, s2, s4, s8, s16) - d_ext)[0:t, :]

    return pl.pallas_call(
        body, name=name, grid=(nt,),
        in_specs=[pl.BlockSpec((t, POOL_W), lambda i: (i, 0)),
                  pl.BlockSpec((POOL_HALO, POOL_W),
                               lambda i: (jnp.minimum((i + 1) * (t // POOL_HALO), SEQ // POOL_HALO - 1), 0))],
        out_specs=pl.BlockSpec((t, POOL_W), lambda i: (i, 0)),
        out_shape=SDS((SEQ, POOL_W), F32),
        compiler_params=_cparams(("arbitrary",)),
    )(dpooled, dpooled)


N_PEER = N_DEV - 1
ANY_SPEC = pl.BlockSpec(memory_space=pl.ANY)


class Exchange:
    def __init__(self, arrays, mode):
        self.arrays = list(arrays)
        self.mode = mode
        n = len(self.arrays)
        if mode == "scatter":
            self.out_shape = [SDS(a.shape, a.dtype) for a in self.arrays]
        else:
            self.out_shape = [SDS((N_DEV,) + a.shape, a.dtype) for a in self.arrays]
        self.scratch = [pltpu.SemaphoreType.DMA((n * N_PEER,)), pltpu.SemaphoreType.DMA((n * N_PEER,)),
                        pltpu.SemaphoreType.DMA((n,))]

    @staticmethod
    def _place():
        x, y, c = lax.axis_index("x"), lax.axis_index("y"), lax.axis_index("c")
        chips = [(1 - x, y), (x, 1 - y), (1 - x, 1 - y)]
        return x, y, c, chips

    @staticmethod
    def _copy(sems, a, k, src, dst, to):
        send_sems, recv_sems, _ = sems
        return pltpu.make_async_remote_copy(
            src_ref=src, dst_ref=dst, send_sem=send_sems.at[a * N_PEER + k], recv_sem=recv_sems.at[a * N_PEER + k],
            device_id=to, device_id_type=MESH)

    def _scatter_peers(self):
        x, y, c, _ = self._place()
        out = []
        for fx, fy, fc in ((0, 0, 1), (1, 0, 0), (0, 1, 0), (1, 1, 0), (1, 0, 1), (0, 1, 1), (1, 1, 1)):
            px, py, pc = x ^ fx, y ^ fy, c ^ fc
            out.append(((px, py, pc), 4 * px + 2 * py + pc))
        return 4 * x + 2 * y + c, out

    def _local(self, ins, outs, sems, a, me):
        src = ins[a].at[me] if self.mode == "scatter" else ins[a]
        return pltpu.make_async_copy(src, outs[a].at[me], sems[2].at[a])

    def start(self, ins, outs, sems):
        if self.mode == "scatter":
            me, peers = self._scatter_peers()
            for a in range(len(ins)):
                self._local(ins, outs, sems, a, me).start()
                for k, (peer, pidx) in enumerate(peers):
                    self._copy(sems, a, k, ins[a].at[pidx], outs[a].at[me], peer).start()
            return
        x, y, c, chips = self._place()
        me = 4 * x + 2 * y + c
        for a in range(len(ins)):
            self._local(ins, outs, sems, a, me).start()
            self._copy(sems, a, 0, ins[a], outs[a].at[me], (x, y, 1 - c)).start()
            for j, (cx, cy) in enumerate(chips):
                self._copy(sems, a, 1 + j, ins[a], outs[a].at[me], (cx, cy, c)).start()

    def finish(self, ins, outs, sems):
        n = len(ins)
        if self.mode == "scatter":
            me, peers = self._scatter_peers()
            for a in range(n):
                for k, (peer, pidx) in enumerate(peers):
                    self._copy(sems, a, k, ins[a].at[pidx], outs[a].at[pidx], peer).wait_recv()
            for a in range(n):
                for k, (peer, pidx) in enumerate(peers):
                    self._copy(sems, a, k, ins[a].at[pidx], outs[a].at[me], peer).wait_send()
                self._local(ins, outs, sems, a, me).wait()
            return
        x, y, c, chips = self._place()
        me = 4 * x + 2 * y + c
        sib = (x, y, 1 - c)
        for a in range(n):
            for j, (cx, cy) in enumerate(chips):
                blk = outs[a].at[4 * cx + 2 * cy + c]
                self._copy(sems, a, 1 + j, ins[a], blk, (cx, cy, c)).wait_recv()
                self._copy(sems, a, 4 + j, blk, blk, sib).start()
        for a in range(n):
            self._copy(sems, a, 0, ins[a], outs[a].at[4 * x + 2 * y + (1 - c)], sib).wait_recv()
            for j, (cx, cy) in enumerate(chips):
                blk = outs[a].at[4 * cx + 2 * cy + (1 - c)]
                self._copy(sems, a, 4 + j, blk, blk, sib).wait_recv()
        for a in range(n):
            for k in range(N_PEER):
                self._copy(sems, a, k, ins[a], outs[a].at[me], sib).wait_send()
            self._local(ins, outs, sems, a, me).wait()


def run_exchange(exch, name):
    n = len(exch.arrays)

    def body(*refs):
        ins, outs, sems = refs[:n], refs[n:2 * n], refs[2 * n:]
        exch.start(ins, outs, sems)
        exch.finish(ins, outs, sems)

    return pl.pallas_call(
        body, name=name, in_specs=[ANY_SPEC] * n, out_specs=[ANY_SPEC] * n, out_shape=exch.out_shape,
        scratch_shapes=exch.scratch,
    )(*exch.arrays)


def _call(body, *, name, grid, in_specs, out_specs, out_shape, scratch_shapes, sem, args, exch=None):
    if exch is None:
        res = pl.pallas_call(body, name=name, grid=grid, in_specs=in_specs, out_specs=out_specs, out_shape=out_shape,
                             scratch_shapes=scratch_shapes, compiler_params=_cparams(sem))(*args)
        return res, None
    single = not isinstance(out_shape, (list, tuple))
    out_specs_l = [out_specs] if single else list(out_specs)
    out_shape_l = [out_shape] if single else list(out_shape)
    n_in, n_out, n_scr, m = len(in_specs), len(out_specs_l), len(scratch_shapes), len(exch.arrays)

    def wrapped(*refs):
        p = 0
        ins = refs[p:p + n_in]; p += n_in
        xin = refs[p:p + m]; p += m
        outs = refs[p:p + n_out]; p += n_out
        xout = refs[p:p + m]; p += m
        scr = refs[p:p + n_scr]; p += n_scr
        sems = refs[p:]
        ids = [pl.program_id(ax) for ax in range(len(grid))]
        first = functools.reduce(jnp.logical_and, [i == 0 for i in ids])
        last = functools.reduce(jnp.logical_and, [i == g - 1 for i, g in zip(ids, grid)])

        @pl.when(first)
        def _():
            exch.start(xin, xout, sems)

        body(*ins, *outs, *scr)

        @pl.when(last)
        def _():
            exch.finish(xin, xout, sems)

    res = pl.pallas_call(
        wrapped, name=name, grid=grid, in_specs=list(in_specs) + [ANY_SPEC] * m,
        out_specs=out_specs_l + [ANY_SPEC] * m, out_shape=out_shape_l + exch.out_shape,
        scratch_shapes=list(scratch_shapes) + exch.scratch, compiler_params=_cparams(sem),
    )(*args, *exch.arrays)
    outs = res[:n_out]
    return (outs[0] if single else outs), res[n_out:]


def _adam_math(w, g, m, v):
    m2 = ADAM_B1 * m + (1.0 - ADAM_B1) * g
    v2 = ADAM_B2 * v + (1.0 - ADAM_B2) * (g * g)
    m_hat = m2 / (1.0 - ADAM_B1 ** ADAM_STEP)
    v_hat = v2 / (1.0 - ADAM_B2 ** ADAM_STEP)
    delta = -ADAM_LR * (m_hat / (jnp.sqrt(v_hat) + ADAM_EPS) + ADAM_WD * w)
    return delta, m2, v2


ADAM_ROW_BLOCKS = 2


def adam_shard(parts0, parts1, w, m, v, name):
    _, r, c = w.shape
    rb = r // ADAM_ROW_BLOCKS

    def body(p0_ref, p1_ref, w_ref, m_ref, v_ref, g_ref, d_ref, m2_ref, v2_ref):
        def run(p_ref):
            g = p_ref[0].astype(F32)
            for i in range(1, N_DEV):
                g = g + p_ref[i].astype(F32)
            delta, m2, v2 = _adam_math(w_ref[0], g, m_ref[0], v_ref[0])
            g_ref[0] = g
            d_ref[0] = delta
            m2_ref[0] = m2
            v2_ref[0] = v2

        @pl.when(pl.program_id(0) == 0)
        def _():
            run(p0_ref)

        @pl.when(pl.program_id(0) == 1)
        def _():
            run(p1_ref)

    def p_spec(layer):
        row = (lambda l, j: jnp.where(l == 0, j, ADAM_ROW_BLOCKS - 1)) if layer == 0 else (lambda l, j: jnp.where(l == 1, j, 0))
        return pl.BlockSpec((N_DEV, rb, c), lambda l, j: (0, row(l, j), 0))

    blk = pl.BlockSpec((1, rb, c), lambda l, j: (l, j, 0))
    return pl.pallas_call(
        body, name=name, grid=(DEPTH, ADAM_ROW_BLOCKS),
        in_specs=[p_spec(0), p_spec(1), blk, blk, blk], out_specs=[blk] * 4,
        out_shape=[SDS(w.shape, F32)] * 4,
        compiler_params=_cparams(("arbitrary", "arbitrary")),
    )(parts0, parts1, w, m, v)


def parts_sum(parts0, parts1, name):
    _, r, c = parts0.shape

    def body(p0_ref, p1_ref, g_ref):
        def run(p_ref):
            g = p_ref[0].astype(F32)
            for i in range(1, N_DEV):
                g = g + p_ref[i].astype(F32)
            g_ref[0] = g

        @pl.when(pl.program_id(0) == 0)
        def _():
            run(p0_ref)

        @pl.when(pl.program_id(0) == 1)
        def _():
            run(p1_ref)

    full = pl.BlockSpec((N_DEV, r, c), lambda l: (0, 0, 0))
    return pl.pallas_call(
        body, name=name, grid=(DEPTH,), in_specs=[full, full],
        out_specs=pl.BlockSpec((1, r, c), lambda l: (l, 0, 0)), out_shape=SDS((DEPTH, r, c), F32),
        compiler_params=_cparams(("arbitrary",)),
    )(parts0, parts1)


def adam_given(g, w, m, v, name):
    _, r, c = w.shape

    def body(g_ref, w_ref, m_ref, v_ref, d_ref, m2_ref, v2_ref):
        delta, m2, v2 = _adam_math(w_ref[0], g_ref[0], m_ref[0], v_ref[0])
        d_ref[0] = delta
        m2_ref[0] = m2
        v2_ref[0] = v2

    blk = pl.BlockSpec((1, r, c), lambda l: (l, 0, 0))
    return pl.pallas_call(
        body, name=name, grid=(DEPTH,), in_specs=[blk] * 4, out_specs=[blk] * 3, out_shape=[SDS(w.shape, F32)] * 3,
        compiler_params=_cparams(("arbitrary",)),
    )(g, w, m, v)


def adam_small(parts, w, m, v, name):
    def body(p_ref, w_ref, m_ref, v_ref, g_ref, d_ref, m2_ref, v2_ref):
        g = p_ref[0]
        for i in range(1, N_DEV):
            g = g + p_ref[i]
        delta, m2, v2 = _adam_math(w_ref[...], g, m_ref[...], v_ref[...])
        g_ref[...] = g
        d_ref[...] = delta
        m2_ref[...] = m2
        v2_ref[...] = v2

    return pl.pallas_call(
        body, name=name, out_shape=[SDS(w.shape, F32)] * 4, compiler_params=_cparams(),
    )(parts, w, m, v)


def _rot_cols(w):
    w4 = w.reshape(w.shape[0], 4, 2, 32)
    return jnp.stack([-w4[:, :, 1], w4[:, :, 0]], axis=2).reshape(w.shape[0], ATT_W)


def _rot_cols_t(dw_rot):
    d4 = dw_rot.reshape(dw_rot.shape[0], 4, 2, 32)
    return jnp.stack([d4[:, :, 1], -d4[:, :, 0]], axis=2).reshape(dw_rot.shape[0], ATT_W)


def build_wext(w_in):
    aq, ak, av, pu = w_in[:, 0:256], w_in[:, 256:512], w_in[:, 512:768], w_in[:, 768:1024]
    dqkvz = w_in[:, 1024:3072]
    gates = jnp.pad(w_in[:, 3072:3080], ((0, 0), (0, GATE_W - 2 * DN_H)))
    return jnp.concatenate([aq, ak, av, _rot_cols(aq), _rot_cols(ak), dqkvz, gates, pu], axis=1)


def fold_dwext(d):
    b = EXT_ATT
    aq = d[:, 0:256] + _rot_cols_t(d[:, 768:1024])
    ak = d[:, 256:512] + _rot_cols_t(d[:, 1024:1280])
    av = d[:, 512:768]
    dqkvz = d[:, b:b + 2048]
    gates = d[:, b + R_G:b + R_G + 2 * DN_H]
    pu = d[:, b + R_PU:b + R_PU + 256]
    return jnp.concatenate([aq, ak, av, pu, dqkvz, gates], axis=1)


def _block_diag(pw):
    z = jnp.zeros((4, 64, 4, 64), pw.dtype)
    for g in range(4):
        z = z.at[g, :, g, :].set(pw[g])
    return z.reshape(POOL_W, POOL_W)


def _diag_blocks(m):
    m4 = m.reshape(4, 64, 4, 64)
    return jnp.stack([m4[g, :, g, :] for g in range(4)], axis=0)


def _lanes(v, reps):
    return jnp.repeat(v, reps)[None, :]


def layer_fwd(p, xa, cos, sin, l, host=None):
    host = host or {}

    def carried(key):
        return host[key][0] if key in host else None

    def done(key, xo):
        if key in host:
            host[key][1](xo)

    xb, xo = ffn_fwd(xa, p["n1"], *p["f1"], f"ffn1_fwd_{l}", carried("ffn1"))
    done("ffn1", xo)
    att, rest = mix_in_fwd(xb, p["nm"], p["wext"], cos, sin, f"mix_in_fwd_{l}")
    ols = [att_fwd_s(att, d, f"att_fwd_{l}_{d}") for d in DILATIONS]
    qkv = dn_prep_fwd(rest, p["conv"], f"dn_prep_fwd_{l}")
    dn, xo = dn_intra_fwd(qkv, rest, p["spread"], p["alog"], p["dtb"], f"dn_intra_fwd_{l}", carried("dn_intra"))
    done("dn_intra", xo)
    (odn, states), xo = dn_inter_fwd(*dn, f"dn_inter_fwd_{l}", carried("dn_inter"))
    done("dn_inter", xo)
    xc = mix_out_fwd(xb, ols[0], ols[1], ols[2], rest, odn, p["wbd"], p["scale"], p["onorm"], p["wout"], f"mix_out_fwd_{l}")
    xd, xo = ffn_fwd(xc, p["n2"], *p["f2"], f"ffn2_fwd_{l}", carried("ffn2"))
    done("ffn2", xo)
    return xd, dict(xa=xa, xb=xb, xc=xc, att=att, rest=rest, ols=ols, qkv=qkv, dn=dn, odn=odn, states=states)


def layer_bwd(p, s, dx, cos, sin, headsum, l, scatter=False, carry=None):
    blocks = lambda ws: [w_.reshape(N_DEV, FF_BLK, D_MODEL) for w_ in ws]
    (dx, *d_f2, d_n2), carried = ffn_bwd(s["xc"], dx, p["n2"], *p["f2"], f"ffn2_bwd_{l}", carry)
    (d_wout, dol1, dol4, dol16, dpooled, dodn, dz, dscale, donorm, dwbd) = mix_out_bwd(
        dx, s["ols"][0], s["ols"][1], s["ols"][2], s["rest"], s["odn"], p["wbd"], p["scale"], p["onorm"], p["wout"],
        headsum, f"mix_out_bwd_{l}")
    dpu = pool_bwd(dpooled, f"pool_bwd_{l}")
    f2 = blocks(d_f2)
    d_dn = dn_inter_bwd(*s["dn"], s["states"], dodn, f"dn_inter_bwd_{l}")
    dqkv, dg, dalog, ddtb = dn_intra_bwd(s["qkv"], s["rest"], p["spread"], p["alog"], p["dtb"], *d_dn, f"dn_intra_bwd_{l}")
    d_dqkv, dconv = dn_prep_bwd(s["rest"], p["conv"], dqkv, f"dn_prep_bwd_{l}")
    datts = [att_bwd_s(s["att"], ol, dol, d, f"att_bwd_{l}_{d}")
             for d, ol, dol in zip(DILATIONS, s["ols"], (dol1, dol4, dol16))]
    dproj = assemble_dproj(datts, cos, sin, d_dqkv, dz, dg, dpu, f"assemble_dproj_{l}")
    dx, d_wext, d_nm = linear_bwd(s["xb"], dx, p["nm"], dproj, p["wext"], f"mix_in_bwd_{l}")
    d_win = fold_dwext(d_wext).reshape(D_MODEL, N_DEV, IN_BLK).transpose(1, 0, 2).astype(BF16)
    io = [d_win, d_wout.reshape(N_DEV, D_MODEL // N_DEV, D_MODEL).astype(BF16)]
    (dx, *d_f1, d_n1), xo = ffn_bwd(s["xa"], dx, p["n1"], *p["f1"], f"ffn1_bwd_{l}",
                                    Exchange(f2 + io, "scatter") if scatter else None)
    if scatter:
        f2, io = list(xo[:3]), list(xo[3:])
    big = dict(f1=blocks(d_f1), f2=f2, io=io)
    small = dict(ffn1_norm=d_n1[0], mix_norm=d_nm[0], ffn2_norm=d_n2[0], pool_w=_diag_blocks(dwbd),
                 pool_scale=dscale[0], dn_a_log=dalog.reshape(DN_H, DN_E).sum(-1),
                 dn_dt_bias=ddtb.reshape(DN_H, DN_E).sum(-1),
                 dn_out_norm=donorm.reshape(DN_H, DN_E).sum(0), dn_conv_w=dconv)
    return dx, big, small, carried


def small_operands(l, pool_w, pool_scale, dn_out_norm, dn_a_log, dn_dt_bias, ffn1_norm, mix_norm, ffn2_norm):
    return dict(
        wbd=_block_diag(pool_w[l]).astype(BF16),
        scale=pool_scale[l][None, :],
        onorm=jnp.tile(dn_out_norm[l], DN_H)[None, :],
        spread=gate_spread_matrix(),
        alog=_lanes(dn_a_log[l], DN_E),
        dtb=_lanes(dn_dt_bias[l], DN_E),
        n1=ffn1_norm[l][None, :], nm=mix_norm[l][None, :], n2=ffn2_norm[l][None, :])


def set_mixer_weights(p, win_g, wout_g, conv_g):
    p["wext"] = build_wext(win_g.transpose(1, 0, 2).reshape(D_MODEL, IN_W))
    p["wout"] = wout_g.reshape(D_MODEL, D_MODEL)
    p["conv"] = conv_g.transpose(1, 0, 2).reshape(DN_CONV, 3 * DN_W)


def rope_tables(pos):
    inv_freq = 10000.0 ** (-jnp.arange(0, ATT_E, 2, dtype=F32) / ATT_E)
    ang = pos.astype(F32)[:, None] * inv_freq
    return jnp.tile(jnp.cos(ang), (1, 8)), jnp.tile(jnp.sin(ang), (1, 8))


def head_sum_matrix():
    return jnp.kron(jnp.eye(4, dtype=F32), jnp.ones((ATT_E, ATT_E), F32))


SMALL_NAMES = ("ffn1_norm", "mix_norm", "ffn2_norm", "pool_w", "pool_scale", "dn_a_log", "dn_dt_bias",
               "dn_out_norm", "final_norm", "dn_conv_w")


PACK_UNIT = 8 * 128


def _pack_rows(n):
    return -(-n // PACK_UNIT) * 8


def _pack(parts):
    rows = []
    for p in parts:
        flat = p.reshape(-1)
        r = _pack_rows(flat.shape[0])
        rows.append(jnp.pad(flat, (0, r * 128 - flat.shape[0])).reshape(r, 128))
    return jnp.concatenate(rows, axis=0)


def _unpack(packed, shapes):
    out, row = [], 0
    for s in shapes:
        n = math.prod(s)
        r = _pack_rows(n)
        out.append(packed[row:row + r].reshape(-1)[:n].reshape(s))
        row += r
    return out


def kernel(x, positions, ffn1_norm, ffn1_w_gate, ffn1_w_up, ffn1_w_down, mix_norm, w_in, pool_w, pool_scale, dn_conv_w, dn_a_log, dn_dt_bias, dn_out_norm, w_out, ffn2_norm, ffn2_w_gate, ffn2_w_up, ffn2_w_down, final_norm, loss_target, m_ffn1_norm, m_ffn1_w_gate, m_ffn1_w_up, m_ffn1_w_down, m_mix_norm, m_w_in, m_pool_w, m_pool_scale, m_dn_conv_w, m_dn_a_log, m_dn_dt_bias, m_dn_out_norm, m_w_out, m_ffn2_norm, m_ffn2_w_gate, m_ffn2_w_up, m_ffn2_w_down, m_final_norm, v_ffn1_norm, v_ffn1_w_gate, v_ffn1_w_up, v_ffn1_w_down, v_mix_norm, v_w_in, v_pool_w, v_pool_scale, v_dn_conv_w, v_dn_a_log, v_dn_dt_bias, v_dn_out_norm, v_w_out, v_ffn2_norm, v_ffn2_w_gate, v_ffn2_w_up, v_ffn2_w_down, v_final_norm):
    me = 4 * lax.axis_index("x") + 2 * lax.axis_index("y") + lax.axis_index("c")
    x0 = x[0]
    target = loss_target[0]

    cos, sin = rope_tables(positions[0])
    headsum = head_sum_matrix()

    layers = [small_operands(l, pool_w, pool_scale, dn_out_norm, dn_a_log, dn_dt_bias, ffn1_norm, mix_norm, ffn2_norm)
              for l in range(DEPTH)]

    def whole(gathered):
        return gathered.reshape(D_FF, D_MODEL)

    def gather_ffn1(l):
        def on_done(xo):
            layers[l]["f1"] = tuple(whole(g) for g in xo)
        return Exchange(ffn_shard_operands(ffn1_w_gate[l], ffn1_w_up[l], ffn1_w_down[l]), "gather"), on_done

    def gather_mixer(l):
        def on_done(xo):
            set_mixer_weights(layers[l], *xo)
        return Exchange([w_in[l].astype(BF16), w_out[l].astype(BF16), dn_conv_w[l]], "gather"), on_done

    gathered_f2 = {}

    def gather_ffn2_part(l, part):
        def on_done(xo):
            gathered_f2[(l, part)] = [whole(g) for g in xo]
            if (l, 0) in gathered_f2 and (l, 1) in gathered_f2:
                layers[l]["f2"] = tuple(gathered_f2[(l, 0)] + gathered_f2[(l, 1)])
        ops = ffn_shard_operands(ffn2_w_gate[l], ffn2_w_up[l], ffn2_w_down[l])
        return Exchange(ops[:2] if part == 0 else ops[2:], "gather"), on_done

    first, on_first = gather_ffn1(0)
    on_first(run_exchange(first, "gather_ffn1_0"))
    saved = []
    xa = x0
    for l in range(DEPTH):
        host = {"ffn1": gather_mixer(l), "dn_intra": gather_ffn2_part(l, 0), "dn_inter": gather_ffn2_part(l, 1)}
        if l + 1 < DEPTH:
            host["ffn2"] = gather_ffn1(l + 1)
        xa, s = layer_fwd(layers[l], xa, cos, sin, l, host)
        saved.append(s)

    loss_row, dx, d_final = loss_head(xa, final_norm[None, :], target, "loss_head")
    loss = lax.psum(loss_row[0, 0], ("x", "y", "c"))

    small = {}
    big_parts = [None] * DEPTH
    carry = None
    for l in reversed(range(DEPTH)):
        dx, big, small[l], carried = layer_bwd(layers[l], saved[l], dx, cos, sin, headsum, l, True, carry)
        if carried is not None:
            big_parts[l + 1]["f1"] = list(carried)
        big_parts[l] = big
        carry = Exchange(big["f1"], "scatter")
    big_parts[0]["f1"] = list(run_exchange(carry, "scatter_ffn1_0"))
    grad_x = dx[None]

    small_shapes = {"ffn1_norm": (DEPTH, D_MODEL), "mix_norm": (DEPTH, D_MODEL), "ffn2_norm": (DEPTH, D_MODEL),
                    "pool_w": (DEPTH, 4, 64, 64), "pool_scale": (DEPTH, POOL_W), "dn_a_log": (DEPTH, DN_H),
                    "dn_dt_bias": (DEPTH, DN_H), "dn_out_norm": (DEPTH, DN_E), "final_norm": (D_MODEL,),
                    "dn_conv_w": (DEPTH, DN_CONV, 3 * DN_W)}
    g_small = {n: (d_final[0] if n == "final_norm" else jnp.stack([small[l][n] for l in range(DEPTH)]))
               for n in SMALL_NAMES}
    (small_parts,) = run_exchange(Exchange([_pack([g_small[n] for n in SMALL_NAMES])], "gather"), "gather_small_grads")

    def conv_full(a):
        return lax.dynamic_update_slice(jnp.zeros((DEPTH, DN_CONV, 3 * DN_W), F32), a, (0, 0, me * (3 * DN_W // N_DEV)))

    given = dict(ffn1_norm=(ffn1_norm, m_ffn1_norm, v_ffn1_norm), mix_norm=(mix_norm, m_mix_norm, v_mix_norm),
                 ffn2_norm=(ffn2_norm, m_ffn2_norm, v_ffn2_norm), pool_w=(pool_w, m_pool_w, v_pool_w),
                 pool_scale=(pool_scale, m_pool_scale, v_pool_scale), dn_a_log=(dn_a_log, m_dn_a_log, v_dn_a_log),
                 dn_dt_bias=(dn_dt_bias, m_dn_dt_bias, v_dn_dt_bias),
                 dn_out_norm=(dn_out_norm, m_dn_out_norm, v_dn_out_norm),
                 final_norm=(final_norm, m_final_norm, v_final_norm),
                 dn_conv_w=(conv_full(dn_conv_w), conv_full(m_dn_conv_w), conv_full(v_dn_conv_w)))
    packed_wmv = [_pack([given[n][k] for n in SMALL_NAMES]) for k in range(3)]
    small_out = adam_small(small_parts, *packed_wmv, "adam_small")
    shapes = [small_shapes[n] for n in SMALL_NAMES]
    small_res = {n: [] for n in SMALL_NAMES}
    for arr in small_out:
        for n, v_ in zip(SMALL_NAMES, _unpack(arr, shapes)):
            if n == "dn_conv_w":
                v_ = lax.dynamic_slice(v_, (0, 0, me * (3 * DN_W // N_DEV)), (DEPTH, DN_CONV, 3 * DN_W // N_DEV))
            small_res[n].append(v_)

    def parts_of(group, idx):
        return [big_parts[l][group][idx] for l in range(DEPTH)]

    def adam_transposed(group, idx, w, m, v, name):
        g = parts_sum(*parts_of(group, idx), f"sum_{name}").transpose(0, 2, 1)
        return [g] + list(adam_given(g, w, m, v, f"adam_{name}"))

    big_res = dict(
        ffn1_w_gate=adam_transposed("f1", 0, ffn1_w_gate, m_ffn1_w_gate, v_ffn1_w_gate, "ffn1_gate"),
        ffn1_w_up=adam_transposed("f1", 1, ffn1_w_up, m_ffn1_w_up, v_ffn1_w_up, "ffn1_up"),
        ffn1_w_down=adam_shard(*parts_of("f1", 2), ffn1_w_down, m_ffn1_w_down, v_ffn1_w_down, "adam_ffn1_down"),
        ffn2_w_gate=adam_transposed("f2", 0, ffn2_w_gate, m_ffn2_w_gate, v_ffn2_w_gate, "ffn2_gate"),
        ffn2_w_up=adam_transposed("f2", 1, ffn2_w_up, m_ffn2_w_up, v_ffn2_w_up, "ffn2_up"),
        ffn2_w_down=adam_shard(*parts_of("f2", 2), ffn2_w_down, m_ffn2_w_down, v_ffn2_w_down, "adam_ffn2_down"),
        w_in=adam_shard(*parts_of("io", 0), w_in, m_w_in, v_w_in, "adam_w_in"),
        w_out=adam_shard(*parts_of("io", 1), w_out, m_w_out, v_w_out, "adam_w_out"),
    )

    order = ("ffn1_norm", "ffn1_w_gate", "ffn1_w_up", "ffn1_w_down", "mix_norm", "w_in", "pool_w", "pool_scale",
             "dn_conv_w", "dn_a_log", "dn_dt_bias", "dn_out_norm", "w_out", "ffn2_norm", "ffn2_w_gate", "ffn2_w_up",
             "ffn2_w_down", "final_norm")
    res = {**small_res, **big_res}
    outs = [loss, grad_x]
    for k in range(4):
        outs.extend(res[n][k] for n in order)
    return tuple(outs)
```

```python
import functools
import math

import jax
import jax.numpy as jnp
from jax import lax
from jax.experimental import pallas as pl
from jax.experimental.pallas import tpu as pltpu

F32 = jnp.float32
BF16 = jnp.bfloat16
HI = lax.Precision.HIGHEST
INV_PREC = lax.Precision.HIGH
SDS = jax.ShapeDtypeStruct

N_DEV = 8
SEQ = 4096
D_MODEL = 1024
DEPTH = 2
D_FF = 2816
FF_BLK = D_FF // N_DEV
ATT_W = 256
ATT_E = 64
ATT_BLK = 128
DILATIONS = (1, 4, 16)
POOL_W = 256
POOL_HALO = 16
DN_W = 512
DN_H = 4
DN_E = 128
DN_C = 64
N_CHUNK = SEQ // DN_C
IN_W = 3080
IN_BLK = IN_W // N_DEV
EPS = 1e-6
EXT_ATT = 1280
GATE_W = 256
EXT_REST = 4 * DN_W + GATE_W + POOL_W
EXT_W = EXT_ATT + EXT_REST
R_DQKV, R_DZ, R_G, R_PU = 0, 1536, 2048, 2304

ADAM_LR, ADAM_B1, ADAM_B2, ADAM_EPS, ADAM_WD, ADAM_STEP = 0.001, 0.9, 0.999, 1e-08, 0.01, 10

VMEM_LIMIT = 60 * 1024 * 1024
MESH = pl.DeviceIdType.MESH


def _cparams(sem=None):
    kw = dict(vmem_limit_bytes=VMEM_LIMIT)
    if sem is not None:
        kw["dimension_semantics"] = sem
    return pltpu.CompilerParams(**kw)


def _dot(a, b, prec=None):
    return jnp.dot(a, b, preferred_element_type=F32, precision=prec)


def _dot_nt(a, b, prec=None):
    return lax.dot_general(a, b, (((1,), (1,)), ((), ())), preferred_element_type=F32, precision=prec)


def _dot_tn(a, b, prec=None):
    return lax.dot_general(a, b, (((0,), (0,)), ((), ())), preferred_element_type=F32, precision=prec)


def _sigmoid(x):
    return jax.nn.sigmoid(x)


def _rms_stats(x):
    r = lax.rsqrt(jnp.mean(x * x, axis=-1, keepdims=True) + EPS)
    return x * r, r


def _rms_bwd(xh, r, w, dh):
    dxh = dh * w
    dx = r * (dxh - xh * jnp.mean(dxh * xh, axis=-1, keepdims=True))
    return dx, jnp.sum(dh * xh, axis=0, keepdims=True)


FFN_T_FWD = 2048
FFN_T_BWD = 512
FF_TILE = 256
N_FF_TILE = D_FF // FF_TILE


def ffn_shard_operands(gate, up, down):
    return [gate.T.astype(BF16), up.T.astype(BF16), down.astype(BF16)]


def ffn_fwd(x, nw, wgt, wut, wd, name, exch=None):
    t = FFN_T_FWD

    def body(x_ref, nw_ref, wgt_ref, wut_ref, wd_ref, o_ref, h_scr, acc_scr):
        k = pl.program_id(1)

        @pl.when(k == 0)
        def _():
            xh, _r = _rms_stats(x_ref[...])
            h_scr[...] = (xh * nw_ref[...]).astype(BF16)
            acc_scr[...] = jnp.zeros_like(acc_scr)

        h = h_scr[...]
        hg = _dot_nt(h, wgt_ref[...])
        hu = _dot_nt(h, wut_ref[...])
        a = (hg * _sigmoid(hg) * hu).astype(BF16)
        acc_scr[...] += _dot(a, wd_ref[...])

        @pl.when(k == N_FF_TILE - 1)
        def _():
            o_ref[...] = x_ref[...] + 0.5 * acc_scr[...]

    w_spec = pl.BlockSpec((FF_TILE, D_MODEL), lambda i, k: (k, 0))
    return _call(
        body, name=name, grid=(SEQ // t, N_FF_TILE),
        in_specs=[pl.BlockSpec((t, D_MODEL), lambda i, k: (i, 0)),
                  pl.BlockSpec((1, D_MODEL), lambda i, k: (0, 0)), w_spec, w_spec, w_spec],
        out_specs=pl.BlockSpec((t, D_MODEL), lambda i, k: (i, 0)),
        out_shape=SDS((SEQ, D_MODEL), F32),
        scratch_shapes=[pltpu.VMEM((t, D_MODEL), BF16), pltpu.VMEM((t, D_MODEL), F32)],
        sem=("arbitrary", "arbitrary"), args=(x, nw, wgt, wut, wd), exch=exch)


def ffn_bwd(x, dxo, nw, wgt, wut, wd, name, exch=None):
    t = FFN_T_BWD
    nt = SEQ // t

    def body(x_ref, dxo_ref, nw_ref, wgt_ref, wut_ref, wd_ref, dx_ref, dwgt_ref, dwut_ref, dwd_ref, dnw_ref,
             dh_scr, ag_scr, au_scr, ad_scr, h_scr):
        k = pl.program_id(0)
        i = pl.program_id(1)
        rows = pl.ds(pl.multiple_of(i * t, t), t)
        nw_v = nw_ref[...]

        @pl.when(k == 0)
        def _():
            xh0, _r0 = _rms_stats(x_ref[...])
            h_scr[rows, :] = (xh0 * nw_v).astype(BF16)

        h = h_scr[rows, :]
        dy = (0.5 * dxo_ref[...]).astype(BF16)
        wgt = wgt_ref[...]
        wut = wut_ref[...]
        hg = _dot_nt(h, wgt)
        hu = _dot_nt(h, wut)
        sg = _sigmoid(hg)
        sil = hg * sg
        a = (sil * hu).astype(BF16)
        da = _dot_nt(dy, wd_ref[...])
        dhu = (da * sil).astype(BF16)
        dhg = (da * hu * (sg * (1.0 + hg * (1.0 - sg)))).astype(BF16)
        p_d = _dot_tn(a, dy)
        p_g = _dot_tn(dhg, h)
        p_u = _dot_tn(dhu, h)
        dh = _dot(dhg, wgt) + _dot(dhu, wut)

        @pl.when(i == 0)
        def _():
            ad_scr[...] = p_d
            ag_scr[...] = p_g
            au_scr[...] = p_u

        @pl.when(i > 0)
        def _():
            ad_scr[...] += p_d
            ag_scr[...] += p_g
            au_scr[...] += p_u

        @pl.when(i == nt - 1)
        def _():
            dwd_ref[...] = ad_scr[...].astype(BF16)
            dwgt_ref[...] = ag_scr[...].astype(BF16)
            dwut_ref[...] = au_scr[...].astype(BF16)

        @pl.when(k == 0)
        def _():
            dh_scr[rows, :] = dh

        @pl.when(k > 0)
        def _():
            dh_scr[rows, :] += dh

        @pl.when(jnp.logical_and(k == 0, i == 0))
        def _():
            dnw_ref[...] = jnp.zeros_like(dnw_ref)

        @pl.when(k == N_FF_TILE - 1)
        def _():
            xh, r = _rms_stats(x_ref[...])
            dx, dw = _rms_bwd(xh, r, nw_v, dh_scr[rows, :])
            dx_ref[...] = dxo_ref[...] + dx
            dnw_ref[...] += dw

    last = N_FF_TILE - 1
    w_spec = pl.BlockSpec((FF_TILE, D_MODEL), lambda k, i: (k, 0))
    return _call(
        body, name=name, grid=(N_FF_TILE, nt),
        in_specs=[pl.BlockSpec((t, D_MODEL), lambda k, i: (i, 0)),
                  pl.BlockSpec((t, D_MODEL), lambda k, i: (i, 0)),
                  pl.BlockSpec((1, D_MODEL), lambda k, i: (0, 0)), w_spec, w_spec, w_spec],
        out_specs=[pl.BlockSpec((t, D_MODEL), lambda k, i: (jnp.where(k == last, i, 0), 0)),
                   w_spec, w_spec, w_spec, pl.BlockSpec((1, D_MODEL), lambda k, i: (0, 0))],
        out_shape=[SDS((SEQ, D_MODEL), F32), SDS((D_FF, D_MODEL), BF16), SDS((D_FF, D_MODEL), BF16),
                   SDS((D_FF, D_MODEL), BF16), SDS((1, D_MODEL), F32)],
        scratch_shapes=[pltpu.VMEM((SEQ, D_MODEL), F32), pltpu.VMEM((FF_TILE, D_MODEL), F32),
                        pltpu.VMEM((FF_TILE, D_MODEL), F32), pltpu.VMEM((FF_TILE, D_MODEL), F32),
                        pltpu.VMEM((SEQ, D_MODEL), BF16)],
        sem=("arbitrary", "arbitrary"), args=(x, dxo, nw, wgt, wut, wd), exch=exch)


def loss_head(x, fw, target, name):
    t = 512

    def body(x_ref, fw_ref, tg_ref, loss_ref, dx_ref, dfw_ref):
        i = pl.program_id(0)
        xh, r = _rms_stats(x_ref[...])
        w = fw_ref[...]
        err = xh * w - tg_ref[...]
        part = 0.5 * jnp.sum(jnp.sum(err * err, axis=-1, keepdims=True), axis=0, keepdims=True) / D_MODEL
        dx, dw = _rms_bwd(xh, r, w, err * (1.0 / D_MODEL))
        dx_ref[...] = dx

        @pl.when(i == 0)
        def _():
            loss_ref[...] = jnp.zeros_like(loss_ref)
            dfw_ref[...] = jnp.zeros_like(dfw_ref)

        loss_ref[...] += jnp.broadcast_to(part, loss_ref.shape)
        dfw_ref[...] += dw

    return pl.pallas_call(
        body, name=name, grid=(SEQ // t,),
        in_specs=[pl.BlockSpec((t, D_MODEL), lambda i: (i, 0)),
                  pl.BlockSpec((1, D_MODEL), lambda i: (0, 0)),
                  pl.BlockSpec((t, D_MODEL), lambda i: (i, 0))],
        out_specs=[pl.BlockSpec((1, 128), lambda i: (0, 0)),
                   pl.BlockSpec((t, D_MODEL), lambda i: (i, 0)),
                   pl.BlockSpec((1, D_MODEL), lambda i: (0, 0))],
        out_shape=[SDS((1, 128), F32), SDS((SEQ, D_MODEL), F32), SDS((1, D_MODEL), F32)],
        compiler_params=_cparams(("arbitrary",)),
    )(x, fw, target)


MIX_T = 256


def _slabs_load(ref, first, n):
    return jnp.concatenate([ref[first + j] for j in range(n)], axis=1)


def _slabs_store(ref, first, val):
    for j in range(val.shape[1] // 128):
        ref[first + j] = val[:, 128 * j:128 * j + 128]


def _slab_spec(k, t):
    return pl.BlockSpec((k, t, 128), lambda i: (0, i, 0))


def mix_in_fwd(x, nw, wext, cos, sin, name):
    t = MIX_T

    def body(x_ref, nw_ref, w_ref, cos_ref, sin_ref, att_ref, rest_ref):
        xh, _r = _rms_stats(x_ref[...])
        h = (xh * nw_ref[...]).astype(BF16)
        pa = _dot(h, w_ref[:, 0:EXT_ATT])
        c = cos_ref[...]
        s = sin_ref[...]
        _slabs_store(att_ref, 0, pa[:, 0:256] * c + pa[:, 768:1024] * s)
        _slabs_store(att_ref, 2, pa[:, 256:512] * c + pa[:, 1024:1280] * s)
        _slabs_store(att_ref, 4, pa[:, 512:768])
        for j in range(EXT_REST // 256):
            rest_ref[:, 256 * j:256 * j + 256] = _dot(h, w_ref[:, EXT_ATT + 256 * j:EXT_ATT + 256 * j + 256])

    return pl.pallas_call(
        body, name=name, grid=(SEQ // t,),
        in_specs=[pl.BlockSpec((t, D_MODEL), lambda i: (i, 0)),
                  pl.BlockSpec((1, D_MODEL), lambda i: (0, 0)),
                  pl.BlockSpec((D_MODEL, EXT_W), lambda i: (0, 0)),
                  pl.BlockSpec((t, ATT_W), lambda i: (i, 0)),
                  pl.BlockSpec((t, ATT_W), lambda i: (i, 0))],
        out_specs=[_slab_spec(6, t),
                   pl.BlockSpec((t, EXT_REST), lambda i: (i, 0))],
        out_shape=[SDS((6, SEQ, 128), F32), SDS((SEQ, EXT_REST), F32)],
        compiler_params=_cparams(("arbitrary",)),
    )(x, nw, wext, cos, sin)


def assemble_dproj(datts, cos, sin, d_dqkv, dz, dg, dpu, name):
    t = 512

    def body(d1_ref, d4_ref, d16_ref, cos_ref, sin_ref, dqkv_ref, dz_ref, dg_ref, dpu_ref, o_ref):
        da6 = d1_ref[...] + d4_ref[...] + d16_ref[...]
        da = jnp.concatenate([da6[j] for j in range(6)], axis=1)
        c = cos_ref[...]
        s = sin_ref[...]
        dq = da[:, 0:256]
        dk = da[:, 256:512]
        o_ref[:, 0:256] = (dq * c).astype(BF16)
        o_ref[:, 256:512] = (dk * c).astype(BF16)
        o_ref[:, 512:768] = da[:, 512:768].astype(BF16)
        o_ref[:, 768:1024] = (dq * s).astype(BF16)
        o_ref[:, 1024:1280] = (dk * s).astype(BF16)
        b = EXT_ATT
        o_ref[:, b + R_DQKV:b + R_DQKV + 1536] = dqkv_ref[...].astype(BF16)
        o_ref[:, b + R_DZ:b + R_DZ + 512] = dz_ref[...].astype(BF16)
        o_ref[:, b + R_G:b + R_G + GATE_W] = dg_ref[...].astype(BF16)
        o_ref[:, b + R_PU:b + R_PU + 256] = dpu_ref[...].astype(BF16)

    row = lambda w: pl.BlockSpec((t, w), lambda i: (i, 0))
    return pl.pallas_call(
        body, name=name, grid=(SEQ // t,),
        in_specs=[_slab_spec(6, t), _slab_spec(6, t), _slab_spec(6, t),
                  row(256), row(256), row(1536), row(512), row(GATE_W), row(256)],
        out_specs=row(EXT_W),
        out_shape=SDS((SEQ, EXT_W), BF16),
        compiler_params=_cparams(("arbitrary",)),
    )(*datts, cos, sin, d_dqkv, dz, dg, dpu)


def linear_bwd(x, dxo, nw, dy, w, name):
    t = 512
    nb = 768
    n = w.shape[1]
    nt = SEQ // t
    nn = n // nb

    def body(x_ref, dxo_ref, nw_ref, dy_ref, w_ref, dx_ref, dw_ref, dnw_ref, dh_scr, h_scr):
        k = pl.program_id(0)
        i = pl.program_id(1)
        rows = pl.ds(pl.multiple_of(i * t, t), t)
        nw_v = nw_ref[...]

        @pl.when(k == 0)
        def _():
            xh0, _r0 = _rms_stats(x_ref[...])
            h_scr[rows, :] = (xh0 * nw_v).astype(BF16)

        h = h_scr[rows, :]
        dyv = dy_ref[...]
        p_w = _dot_tn(h, dyv)
        dh = _dot_nt(dyv, w_ref[...])

        @pl.when(i == 0)
        def _():
            dw_ref[...] = p_w

        @pl.when(i > 0)
        def _():
            dw_ref[...] += p_w

        @pl.when(k == 0)
        def _():
            dh_scr[rows, :] = dh

        @pl.when(k > 0)
        def _():
            dh_scr[rows, :] += dh

        @pl.when(jnp.logical_and(k == 0, i == 0))
        def _():
            dnw_ref[...] = jnp.zeros_like(dnw_ref)

        @pl.when(k == nn - 1)
        def _():
            xh, r = _rms_stats(x_ref[...])
            dx, dw = _rms_bwd(xh, r, nw_v, dh_scr[rows, :])
            dx_ref[...] = dxo_ref[...] + dx
            dnw_ref[...] += dw

    last = nn - 1
    return pl.pallas_call(
        body, name=name, grid=(nn, nt),
        in_specs=[pl.BlockSpec((t, D_MODEL), lambda k, i: (i, 0)),
                  pl.BlockSpec((t, D_MODEL), lambda k, i: (i, 0)),
                  pl.BlockSpec((1, D_MODEL), lambda k, i: (0, 0)),
                  pl.BlockSpec((t, nb), lambda k, i: (i, k)),
                  pl.BlockSpec((D_MODEL, nb), lambda k, i: (0, k))],
        out_specs=[pl.BlockSpec((t, D_MODEL), lambda k, i: (jnp.where(k == last, i, 0), 0)),
                   pl.BlockSpec((D_MODEL, nb), lambda k, i: (0, k)),
                   pl.BlockSpec((1, D_MODEL), lambda k, i: (0, 0))],
        out_shape=[SDS((SEQ, D_MODEL), F32), SDS((D_MODEL, n), F32), SDS((1, D_MODEL), F32)],
        scratch_shapes=[pltpu.VMEM((SEQ, D_MODEL), F32), pltpu.VMEM((SEQ, D_MODEL), BF16)],
        compiler_params=_cparams(("arbitrary", "arbitrary")),
    )(x, dxo, nw, dy, w)


def _att_masks():
    qi = lax.broadcasted_iota(jnp.int32, (ATT_BLK, ATT_BLK), 0)
    ki = lax.broadcasted_iota(jnp.int32, (ATT_BLK, ATT_BLK), 1)
    return ki <= qi, ki >= qi


NEG = -1e30


N_ATT_BLK = SEQ // ATT_BLK


def _class_rows(i, d):
    per_class = N_ATT_BLK // d
    shift = per_class.bit_length() - 1
    r = i >> shift
    j = i & (per_class - 1)
    span = ATT_BLK * d
    start = r + span * j
    prev = jnp.where(j == 0, start, start - span)
    nxt = jnp.where(j == per_class - 1, start, start + span)

    def rows(s0):
        if d == 1:
            return pl.ds(pl.multiple_of(s0, ATT_BLK), ATT_BLK)
        return pl.ds(s0, ATT_BLK, stride=d)

    return rows(start), rows(prev), rows(nxt), j != 0, j != per_class - 1


def _slab_heads(ref, slab, rows):
    x0 = ref[pl.ds(slab, 1), rows, :][0]
    x1 = ref[pl.ds(slab + 1, 1), rows, :][0]
    return jnp.stack([x0[:, 0:ATT_E], x0[:, ATT_E:], x1[:, 0:ATT_E], x1[:, ATT_E:]], axis=0)


def _put_slab_heads(ref, slab, rows, val):
    ref[pl.ds(slab, 1), rows, :] = jnp.concatenate([val[0], val[1]], axis=1)[None]
    ref[pl.ds(slab + 1, 1), rows, :] = jnp.concatenate([val[2], val[3]], axis=1)[None]


ATT_BLOCKS_PER_STEP = 2


def _resident_call(body, ins, out_slabs, name):
    n_in = len(ins)
    steps = N_ATT_BLK // ATT_BLOCKS_PER_STEP

    def wrapped(*refs):
        hbm_in, hbm_out = refs[:n_in], refs[n_in]
        vm_in, vm_out, sem = refs[n_in + 1:2 * n_in + 1], refs[2 * n_in + 1], refs[2 * n_in + 2]
        i = pl.program_id(0)

        @pl.when(i == 0)
        def _():
            copies = [pltpu.make_async_copy(h, v, sem.at[k]) for k, (h, v) in enumerate(zip(hbm_in, vm_in))]
            for cp in copies:
                cp.start()
            for cp in copies:
                cp.wait()

        for b in range(ATT_BLOCKS_PER_STEP):
            body(ATT_BLOCKS_PER_STEP * i + b, *vm_in, vm_out)

        @pl.when(i == steps - 1)
        def _():
            cp = pltpu.make_async_copy(vm_out, hbm_out, sem.at[n_in])
            cp.start()
            cp.wait()

    return pl.pallas_call(
        wrapped, name=name, grid=(steps,),
        in_specs=[ANY_SPEC] * n_in, out_specs=ANY_SPEC, out_shape=SDS((out_slabs, SEQ, 128), F32),
        scratch_shapes=[pltpu.VMEM(a.shape, a.dtype) for a in ins] + [pltpu.VMEM((out_slabs, SEQ, 128), F32),
                                                                      pltpu.SemaphoreType.DMA((n_in + 1,))],
        compiler_params=_cparams(("arbitrary",)),
    )(*ins)


def _att_fwd_math(ld, has_prev):
    m_d, m_p = _att_masks()
    m_p = jnp.logical_and(m_p, has_prev)
    q = ld("att", 0, "cur").astype(BF16)
    kc = ld("att", 2, "cur").astype(BF16)
    vc = ld("att", 4, "cur").astype(BF16)
    kp = ld("att", 2, "prev").astype(BF16)
    vp = ld("att", 4, "prev").astype(BF16)
    sd = jnp.where(m_d, _bdot(q, kc, 2, 2) * 0.125, NEG)
    sp = jnp.where(m_p, _bdot(q, kp, 2, 2) * 0.125, NEG)
    m = jnp.maximum(jnp.max(sd, axis=-1, keepdims=True), jnp.max(sp, axis=-1, keepdims=True))
    pd = jnp.exp(sd - m)
    pp = jnp.exp(sp - m)
    den = jnp.sum(pd, axis=-1, keepdims=True) + jnp.sum(pp, axis=-1, keepdims=True)
    inv = 1.0 / den
    o = _bdot((pd * inv).astype(BF16), vc, 2, 1) + _bdot((pp * inv).astype(BF16), vp, 2, 1)
    return o, jnp.broadcast_to(m + jnp.log(den), (4, ATT_BLK, ATT_E))


def _att_bwd_math(ld, has_prev, has_next):
    m_d, m_band = _att_masks()
    m_p = jnp.logical_and(m_band, has_prev)
    m_n = jnp.logical_and(m_band, has_next)

    def pair(q, k, v, lse, do, dterm, mask):
        s = jnp.where(mask, _bdot(q, k, 2, 2) * 0.125, NEG)
        p = jnp.exp(s - lse)
        dp = _bdot(do, v, 2, 2)
        ds = (p * (dp + dterm) * 0.125).astype(BF16)
        return p.astype(BF16), ds

    q_c = ld("att", 0, "cur").astype(BF16)
    k_c = ld("att", 2, "cur").astype(BF16)
    v_c = ld("att", 4, "cur").astype(BF16)
    k_p = ld("att", 2, "prev").astype(BF16)
    v_p = ld("att", 4, "prev").astype(BF16)
    q_n = ld("att", 0, "next").astype(BF16)
    o_c = ld("ol", 0, "cur")
    o_n = ld("ol", 0, "next")
    lse_c = ld("ol", 2, "cur")[:, :, 0:1]
    lse_n = ld("ol", 2, "next")[:, :, 0:1]
    do_c = ld("dol", 0, "cur")
    do_n = ld("dol", 0, "next")
    t_c = ld("dol", 2, "cur")[:, :, 0:1] - jnp.sum(do_c * o_c, axis=-1, keepdims=True)
    t_n = ld("dol", 2, "next")[:, :, 0:1] - jnp.sum(do_n * o_n, axis=-1, keepdims=True)
    do_cb = do_c.astype(BF16)
    do_nb = do_n.astype(BF16)
    p1, ds1 = pair(q_c, k_c, v_c, lse_c, do_cb, t_c, m_d)
    _p2, ds2 = pair(q_c, k_p, v_p, lse_c, do_cb, t_c, m_p)
    p3, ds3 = pair(q_n, k_c, v_c, lse_n, do_nb, t_n, m_n)
    return (_bdot(ds1, k_c, 2, 1) + _bdot(ds2, k_p, 2, 1), _bdot(ds1, q_c, 1, 1) + _bdot(ds3, q_n, 1, 1),
            _bdot(p1, do_cb, 1, 1) + _bdot(p3, do_nb, 1, 1))


ROWS_A = pl.ds(0, ATT_BLK)
ROWS_B = pl.ds(ATT_BLK, ATT_BLK)
N_ATT_PAIR = N_ATT_BLK // 2


def _pair_spec(k):
    return pl.BlockSpec((k, 2 * ATT_BLK, 128), lambda i: (0, i, 0))


def _before_pair_spec(k):
    return pl.BlockSpec((k, ATT_BLK, 128), lambda i: (0, jnp.maximum(2 * i - 1, 0), 0))


def _after_pair_spec(k):
    return pl.BlockSpec((k, ATT_BLK, 128), lambda i: (0, jnp.minimum(2 * i + 2, N_ATT_BLK - 1), 0))


def att_fwd_s(att, d, name):
    if d == 1:
        def body1(cur_ref, prev_ref, o_ref):
            i = pl.program_id(0)
            for rows, views, has_prev in (
                    (ROWS_A, {"cur": (cur_ref, ROWS_A), "prev": (prev_ref, ROWS_A)}, i != 0),
                    (ROWS_B, {"cur": (cur_ref, ROWS_B), "prev": (cur_ref, ROWS_A)}, True)):
                o, lse = _att_fwd_math(lambda _a, slab, where, v=views: _slab_heads(v[where][0], slab, v[where][1]), has_prev)
                _put_slab_heads(o_ref, 0, rows, o)
                _put_slab_heads(o_ref, 2, rows, lse)

        return pl.pallas_call(
            body1, name=name, grid=(N_ATT_PAIR,),
            in_specs=[_pair_spec(6), _before_pair_spec(6)], out_specs=_pair_spec(4),
            out_shape=SDS((4, SEQ, 128), F32), compiler_params=_cparams(("arbitrary",)),
        )(att, att)

    def body(i, att_ref, o_ref):
        cur, prev, _nxt, has_prev, _has_next = _class_rows(i, d)
        rows = {"cur": cur, "prev": prev}
        o, lse = _att_fwd_math(lambda _a, slab, where: _slab_heads(att_ref, slab, rows[where]), has_prev)
        _put_slab_heads(o_ref, 0, cur, o)
        _put_slab_heads(o_ref, 2, cur, lse)

    return _resident_call(body, [att], 4, name)


def att_bwd_s(att, ol, dol, d, name):
    if d == 1:
        def body1(a_p, a_c, a_n, ol_c, ol_n, dol_c, dol_n, d_ref):
            i = pl.program_id(0)
            first = {("att", "prev"): (a_p, ROWS_A), ("att", "cur"): (a_c, ROWS_A), ("att", "next"): (a_c, ROWS_B),
                     ("ol", "cur"): (ol_c, ROWS_A), ("ol", "next"): (ol_c, ROWS_B),
                     ("dol", "cur"): (dol_c, ROWS_A), ("dol", "next"): (dol_c, ROWS_B)}
            second = {("att", "prev"): (a_c, ROWS_A), ("att", "cur"): (a_c, ROWS_B), ("att", "next"): (a_n, ROWS_A),
                      ("ol", "cur"): (ol_c, ROWS_B), ("ol", "next"): (ol_n, ROWS_A),
                      ("dol", "cur"): (dol_c, ROWS_B), ("dol", "next"): (dol_n, ROWS_A)}
            for rows, views, has_prev, has_next in ((ROWS_A, first, i != 0, True),
                                                    (ROWS_B, second, True, i != N_ATT_PAIR - 1)):
                dq, dk, dv = _att_bwd_math(
                    lambda a, slab, where, v=views: _slab_heads(v[(a, where)][0], slab, v[(a, where)][1]), has_prev, has_next)
                _put_slab_heads(d_ref, 0, rows, dq)
                _put_slab_heads(d_ref, 2, rows, dk)
                _put_slab_heads(d_ref, 4, rows, dv)

        return pl.pallas_call(
            body1, name=name, grid=(N_ATT_PAIR,),
            in_specs=[_before_pair_spec(6), _pair_spec(6), _after_pair_spec(6), _pair_spec(4), _after_pair_spec(4),
                      _pair_spec(4), _after_pair_spec(4)],
            out_specs=_pair_spec(6), out_shape=SDS((6, SEQ, 128), F32), compiler_params=_cparams(("arbitrary",)),
        )(att, att, att, ol, ol, dol, dol)

    def body(i, att_ref, ol_ref, dol_ref, d_ref):
        cur, prev, nxt, has_prev, has_next = _class_rows(i, d)
        rows = {"cur": cur, "prev": prev, "next": nxt}
        refs = {"att": att_ref, "ol": ol_ref, "dol": dol_ref}
        dq, dk, dv = _att_bwd_math(lambda a, slab, where: _slab_heads(refs[a], slab, rows[where]), has_prev, has_next)
        _put_slab_heads(d_ref, 0, cur, dq)
        _put_slab_heads(d_ref, 2, cur, dk)
        _put_slab_heads(d_ref, 4, cur, dv)

    return _resident_call(body, [att, ol, dol], 6, name)


def _shift_down(x, k):
    rows = lax.broadcasted_iota(jnp.int32, x.shape, 0)
    return jnp.where(rows >= k, pltpu.roll(x, k, 0), 0.0)


def _shift_up(x, k):
    n = x.shape[0]
    rows = lax.broadcasted_iota(jnp.int32, x.shape, 0)
    return jnp.where(rows < n - k, pltpu.roll(x, n - k, 0), 0.0)


@functools.partial(jax.custom_vjp, nondiff_argnums=(1,))
def _delay(x, k):
    return _shift_down(x, k)


def _delay_fwd(x, k):
    return _shift_down(x, k), None


def _delay_bwd(k, _res, g):
    return (_shift_up(g, k),)


_delay.defvjp(_delay_fwd, _delay_bwd)

DN_CONV = 4


def _dn_prep_fn(u, w, kind):
    y = w[DN_CONV - 1:DN_CONV] * u
    for j in range(DN_CONV - 1):
        y = y + w[j:j + 1] * _delay(u, DN_CONV - 1 - j)
    y = y * _sigmoid(y)
    nrm = y * lax.rsqrt(jnp.sum(y * y, axis=-1, keepdims=True) + EPS)
    return jnp.where(kind == 0, nrm * (DN_E ** -0.5), jnp.where(kind == 1, nrm, y))


def dn_prep_fwd(rest, conv_w, name):
    def body(u_ref, w_ref, o_ref):
        j = pl.program_id(0)
        kind = (j >= DN_H).astype(jnp.int32) + (j >= 2 * DN_H).astype(jnp.int32)
        o_ref[...] = _dn_prep_fn(u_ref[...], w_ref[...], kind)

    return pl.pallas_call(
        body, name=name, grid=(3 * DN_H,),
        in_specs=[pl.BlockSpec((SEQ, DN_E), lambda j: (0, j)),
                  pl.BlockSpec((DN_CONV, DN_E), lambda j: (0, j))],
        out_specs=pl.BlockSpec((SEQ, DN_E), lambda j: (0, j)),
        out_shape=SDS((SEQ, 3 * DN_W), F32),
        compiler_params=_cparams(("arbitrary",)),
    )(rest, conv_w)


def dn_prep_bwd(rest, conv_w, dqkv, name):
    def body(u_ref, w_ref, g_ref, du_ref, dw_ref):
        j = pl.program_id(0)
        kind = (j >= DN_H).astype(jnp.int32) + (j >= 2 * DN_H).astype(jnp.int32)
        _y, vjp = jax.vjp(lambda u, w: _dn_prep_fn(u, w, kind), u_ref[...], w_ref[...])
        du, dw = vjp(g_ref[...])
        du_ref[...] = du
        dw_ref[...] = dw

    return pl.pallas_call(
        body, name=name, grid=(3 * DN_H,),
        in_specs=[pl.BlockSpec((SEQ, DN_E), lambda j: (0, j)),
                  pl.BlockSpec((DN_CONV, DN_E), lambda j: (0, j)),
                  pl.BlockSpec((SEQ, DN_E), lambda j: (0, j))],
        out_specs=[pl.BlockSpec((SEQ, DN_E), lambda j: (0, j)),
                   pl.BlockSpec((DN_CONV, DN_E), lambda j: (0, j))],
        out_shape=[SDS((SEQ, 3 * DN_W), F32), SDS((DN_CONV, 3 * DN_W), F32)],
        compiler_params=_cparams(("arbitrary",)),
    )(rest, conv_w, dqkv)


def _bdot(a, b, ca, cb, prec=None):
    return lax.dot_general(a, b, (((ca,), (cb,)), ((0,), (0,))), preferred_element_type=F32, precision=prec)


def _unit_lower_inverse(a):
    eye = (lax.broadcasted_iota(jnp.int32, (DN_C, DN_C), 0) == lax.broadcasted_iota(jnp.int32, (DN_C, DN_C), 1)).astype(F32)
    p = eye - a
    b = _bdot(a, a, 2, 1, INV_PREC)
    for lvl in range(5):
        p = p + _bdot(p, b, 2, 1, INV_PREC)
        if lvl < 4:
            b = _bdot(b, b, 2, 1, INV_PREC)
    return p


@jax.custom_vjp
def _tri_inv(a):
    return _unit_lower_inverse(a)


def _tri_inv_fwd(a):
    t = _unit_lower_inverse(a)
    return t, t


def _tri_inv_bwd(t, g):
    return (-_bdot(_bdot(t, g, 1, 1, INV_PREC), t, 2, 2, INV_PREC),)


_tri_inv.defvjp(_tri_inv_fwd, _tri_inv_bwd)


def _b16(x):
    return x.astype(BF16)


def _heads(ref, base=0):
    return jnp.stack([ref[:, base + DN_E * hd:base + DN_E * hd + DN_E] for hd in range(DN_H)], axis=0)


def _put_heads(ref, val, base=0):
    for hd in range(DN_H):
        ref[:, base + DN_E * hd:base + DN_E * hd + DN_E] = val[hd]


DN_G_LOG2 = 3
DN_G = 1 << DN_G_LOG2
N_INST = DN_G * DN_H


def _dn_intra(q, k, v, bb, ab, alog, dtb):
    ri = lax.broadcasted_iota(jnp.int32, (DN_C, DN_C), 0)
    ci = lax.broadcasted_iota(jnp.int32, (DN_C, DN_C), 1)
    lower = ri >= ci
    strict = ri > ci
    nh = q.shape[0]
    beta = _sigmoid(bb)
    xg = ab + dtb
    softplus = jnp.maximum(xg, 0.0) + jnp.log(1.0 + jnp.exp(-jnp.abs(xg)))
    gi = -jnp.exp(alog) * softplus
    g = _bdot(jnp.broadcast_to(lower.astype(F32), (nh, DN_C, DN_C)), gi, 2, 1, HI)
    eg = jnp.exp(g)
    kb = k * beta
    vb = v * beta
    g_col = g[:, :, 0:DN_C]
    g_row = _bdot(jnp.full((nh, DN_C, DN_E), 1.0 / DN_E, F32), g, 2, 2, HI)
    decay = jnp.where(lower, jnp.exp(jnp.where(lower, g_col - g_row, 0.0)), 0.0)
    kbf = _b16(k)
    a = jnp.where(strict, _bdot(_b16(kb), kbf, 2, 2) * decay, 0.0)
    tb = _b16(_tri_inv(a))
    u = _bdot(tb, _b16(vb), 2, 1)
    w = _bdot(tb, _b16(kb * eg), 2, 1)
    intra = jnp.where(lower, _bdot(_b16(q), kbf, 2, 2) * decay, 0.0)
    g_last = g[:, DN_C - 1:DN_C, :]
    return u, w, q * eg, k * jnp.exp(g_last - g), intra, jnp.exp(g_last)


def _dn_inter(u, w, qg, kdec, intra, egl, state):
    sb = _b16(state)
    v_new = u - _bdot(_b16(w), sb, 2, 1)
    o = _bdot(_b16(qg), sb, 2, 1) + _bdot(_b16(intra), _b16(v_new), 2, 1)
    return o, state * egl + _bdot(_b16(kdec), _b16(v_new), 1, 1)


def _inst(ref, base=0):
    per_head = [ref[:, base + DN_E * hd:base + DN_E * hd + DN_E].reshape(DN_G, DN_C, DN_E) for hd in range(DN_H)]
    return jnp.concatenate(per_head, axis=0)


def _inst_rows(ref):
    rows = [jnp.broadcast_to(ref[:, DN_E * hd:DN_E * hd + DN_E][None], (DN_G, 1, DN_E)) for hd in range(DN_H)]
    return jnp.concatenate(rows, axis=0)


def _put_inst(ref, val, width=DN_E, base=0):
    for hd in range(DN_H):
        ref[:, base + width * hd:base + width * hd + width] = val[DN_G * hd:DN_G * hd + DN_G].reshape(DN_G * DN_C, width)


@jax.custom_vjp
def _spread_gates(gates):
    t = gates.shape[0]
    return jnp.concatenate([jnp.broadcast_to(gates[:, j:j + 1], (t, DN_E)) for j in range(2 * DN_H)], axis=1)


def _spread_gates_fwd(gates):
    return _spread_gates(gates), None


def _spread_gates_bwd(_res, g):
    t = g.shape[0]
    lane = lax.broadcasted_iota(jnp.int32, (t, GATE_W), 1)
    out = jnp.zeros((t, GATE_W), F32)
    for j in range(2 * DN_H):
        s = jnp.sum(g[:, DN_E * j:DN_E * j + DN_E], axis=-1, keepdims=True)
        out = jnp.where(lane == j, s, out)
    return (out,)


_spread_gates.defvjp(_spread_gates_fwd, _spread_gates_bwd)


def _dn_intra_from_gates(q, k, v, gates, alog, dtb):
    wide = _spread_gates(gates)
    inst = lambda base: jnp.concatenate(
        [wide[:, base + DN_E * hd:base + DN_E * hd + DN_E].reshape(DN_G, DN_C, DN_E) for hd in range(DN_H)], axis=0)
    return _dn_intra(q, k, v, inst(0), inst(DN_W), alog, dtb)


def _intra_args(qkv_ref, g_ref, alog_ref, dtb_ref):
    return (_inst(qkv_ref), _inst(qkv_ref, DN_W), _inst(qkv_ref, 2 * DN_W), g_ref[...],
            _inst_rows(alog_ref), _inst_rows(dtb_ref))


def _intra_in_specs():
    t = DN_G * DN_C
    return [pl.BlockSpec((t, 3 * DN_W), lambda n: (n, 0)),
            pl.BlockSpec((t, GATE_W), lambda n: (n, R_G // GATE_W)),
            pl.BlockSpec((1, DN_W), lambda n: (0, 0)),
            pl.BlockSpec((1, DN_W), lambda n: (0, 0))]


def dn_intra_fwd(qkv, rest, alog_b, dtb_b, name, exch=None):
    t = DN_G * DN_C

    def body(qkv_ref, g_ref, alog_ref, dtb_ref, u_ref, w_ref, qg_ref, kd_ref, in_ref, egl_ref):
        u, w, qg, kdec, intra, egl = _dn_intra_from_gates(*_intra_args(qkv_ref, g_ref, alog_ref, dtb_ref))
        _put_inst(u_ref, u)
        _put_inst(w_ref, w.astype(BF16))
        _put_inst(qg_ref, qg.astype(BF16))
        _put_inst(kd_ref, kdec.astype(BF16))
        _put_inst(in_ref, intra.astype(BF16), DN_C)
        for hd in range(DN_H):
            egl_ref[:, DN_E * hd:DN_E * hd + DN_E] = egl[DN_G * hd:DN_G * hd + DN_G].reshape(DN_G, DN_E)

    row = lambda w_: pl.BlockSpec((t, w_), lambda n: (n, 0))
    return _call(
        body, name=name, grid=(N_CHUNK // DN_G,), in_specs=_intra_in_specs(),
        out_specs=[row(DN_W), row(DN_W), row(DN_W), row(DN_W), row(DN_H * DN_C),
                   pl.BlockSpec((DN_G, DN_W), lambda n: (n, 0))],
        out_shape=[SDS((SEQ, DN_W), F32), SDS((SEQ, DN_W), BF16), SDS((SEQ, DN_W), BF16), SDS((SEQ, DN_W), BF16),
                   SDS((SEQ, DN_H * DN_C), BF16), SDS((N_CHUNK, DN_W), F32)],
        scratch_shapes=[], sem=("arbitrary",), args=(qkv, rest, alog_b, dtb_b), exch=exch)


def dn_intra_bwd(qkv, rest, alog_b, dtb_b, du, dw, dqg, dkd, dintra, degl, name):
    t = DN_G * DN_C

    def body(qkv_ref, g_ref, alog_ref, dtb_ref, du_ref, dw_ref, dqg_ref, dkd_ref, din_ref, degl_ref,
             dqkv_ref, dg_ref, dalog_ref, ddtb_ref):
        @pl.when(pl.program_id(0) == 0)
        def _():
            dalog_ref[...] = jnp.zeros_like(dalog_ref)
            ddtb_ref[...] = jnp.zeros_like(ddtb_ref)

        _out, vjp = jax.vjp(_dn_intra_from_gates, *_intra_args(qkv_ref, g_ref, alog_ref, dtb_ref))
        d_in = jnp.concatenate([din_ref[:, DN_C * hd:DN_C * hd + DN_C].reshape(DN_G, DN_C, DN_C) for hd in range(DN_H)], axis=0)
        d_egl = jnp.concatenate([degl_ref[:, DN_E * hd:DN_E * hd + DN_E].reshape(DN_G, 1, DN_E) for hd in range(DN_H)], axis=0)
        dq, dk, dv, dg, dalog, ddtb = vjp((_inst(du_ref), _inst(dw_ref), _inst(dqg_ref), _inst(dkd_ref), d_in, d_egl))
        _put_inst(dqkv_ref, dq)
        _put_inst(dqkv_ref, dk, DN_E, DN_W)
        _put_inst(dqkv_ref, dv, DN_E, 2 * DN_W)
        dg_ref[...] = dg
        for hd in range(DN_H):
            sl = slice(DN_E * hd, DN_E * hd + DN_E)
            dalog_ref[:, sl] += jnp.sum(dalog[DN_G * hd:DN_G * hd + DN_G], axis=0)
            ddtb_ref[:, sl] += jnp.sum(ddtb[DN_G * hd:DN_G * hd + DN_G], axis=0)

    row = lambda w_: pl.BlockSpec((t, w_), lambda n: (n, 0))
    acc = pl.BlockSpec((1, DN_W), lambda n: (0, 0))
    return pl.pallas_call(
        body, name=name, grid=(N_CHUNK // DN_G,),
        in_specs=_intra_in_specs() + [row(DN_W), row(DN_W), row(DN_W), row(DN_W), row(DN_H * DN_C),
                                      pl.BlockSpec((DN_G, DN_W), lambda n: (n, 0))],
        out_specs=[row(3 * DN_W), row(GATE_W), acc, acc],
        out_shape=[SDS((SEQ, 3 * DN_W), F32), SDS((SEQ, GATE_W), F32), SDS((1, DN_W), F32), SDS((1, DN_W), F32)],
        compiler_params=_cparams(("arbitrary",)),
    )(qkv, rest, alog_b, dtb_b, du, dw, dqg, dkd, dintra, degl)


def _inter_args(u_ref, w_ref, qg_ref, kd_ref, in_ref, egl_ref, n, state):
    f = lambda r: _heads(r).astype(F32)
    intra = jnp.stack([in_ref[:, DN_C * hd:DN_C * hd + DN_C] for hd in range(DN_H)], axis=0).astype(F32)
    egl = _heads(egl_ref.at[pl.ds(n & (DN_G - 1), 1), :])
    return f(u_ref), f(w_ref), f(qg_ref), f(kd_ref), intra, egl, state


def dn_inter_fwd(u, w, qg, kdec, intra, egl, name, exch=None):
    def body(u_ref, w_ref, qg_ref, kd_ref, in_ref, egl_ref, o_ref, st_ref, state_scr):
        n = pl.program_id(0)

        @pl.when(n == 0)
        def _():
            state_scr[...] = jnp.zeros_like(state_scr)

        st = state_scr[...]
        st_ref[0] = st
        o, ns = _dn_inter(*_inter_args(u_ref, w_ref, qg_ref, kd_ref, in_ref, egl_ref, n, st))
        _put_heads(o_ref, o)
        state_scr[...] = ns

    row = lambda w_: pl.BlockSpec((DN_C, w_), lambda n: (n, 0))
    return _call(
        body, name=name, grid=(N_CHUNK,),
        in_specs=[row(DN_W), row(DN_W), row(DN_W), row(DN_W), row(DN_H * DN_C),
                  pl.BlockSpec((DN_G, DN_W), lambda n: (n >> DN_G_LOG2, 0))],
        out_specs=[row(DN_W), pl.BlockSpec((1, DN_H, DN_E, DN_E), lambda n: (n, 0, 0, 0))],
        out_shape=[SDS((SEQ, DN_W), F32), SDS((N_CHUNK, DN_H, DN_E, DN_E), F32)],
        scratch_shapes=[pltpu.VMEM((DN_H, DN_E, DN_E), F32)],
        sem=("arbitrary",), args=(u, w, qg, kdec, intra, egl), exch=exch)


def dn_inter_bwd(u, w, qg, kdec, intra, egl, states, do, name):
    last = N_CHUNK - 1

    def body(u_ref, w_ref, qg_ref, kd_ref, in_ref, egl_ref, st_ref, do_ref,
             du_ref, dw_ref, dqg_ref, dkd_ref, din_ref, degl_ref, dstate_scr):
        s = pl.program_id(0)
        n = last - s

        @pl.when(s == 0)
        def _():
            dstate_scr[...] = jnp.zeros_like(dstate_scr)

        _out, vjp = jax.vjp(_dn_inter, *_inter_args(u_ref, w_ref, qg_ref, kd_ref, in_ref, egl_ref, n, st_ref[0]))
        du, dw, dqg, dkd, din, degl, dst = vjp((_heads(do_ref), dstate_scr[...]))
        _put_heads(du_ref, du)
        _put_heads(dw_ref, dw)
        _put_heads(dqg_ref, dqg)
        _put_heads(dkd_ref, dkd)
        for hd in range(DN_H):
            din_ref[:, DN_C * hd:DN_C * hd + DN_C] = din[hd]
        row = n & (DN_G - 1)

        @pl.when(row == DN_G - 1)
        def _():
            degl_ref[...] = jnp.zeros_like(degl_ref)

        new_row = jnp.concatenate([degl[hd] for hd in range(DN_H)], axis=1)
        rows = lax.broadcasted_iota(jnp.int32, (DN_G, DN_W), 0)
        degl_ref[...] = jnp.where(rows == row, jnp.broadcast_to(new_row, (DN_G, DN_W)), degl_ref[...])
        dstate_scr[...] = dst

    rev = lambda w_: pl.BlockSpec((DN_C, w_), lambda s: (last - s, 0))
    grp = pl.BlockSpec((DN_G, DN_W), lambda s: ((last - s) >> DN_G_LOG2, 0))
    return pl.pallas_call(
        body, name=name, grid=(N_CHUNK,),
        in_specs=[rev(DN_W), rev(DN_W), rev(DN_W), rev(DN_W), rev(DN_H * DN_C), grp,
                  pl.BlockSpec((1, DN_H, DN_E, DN_E), lambda s: (last - s, 0, 0, 0)), rev(DN_W)],
        out_specs=[rev(DN_W), rev(DN_W), rev(DN_W), rev(DN_W), rev(DN_H * DN_C), grp],
        out_shape=[SDS((SEQ, DN_W), F32)] * 4 + [SDS((SEQ, DN_H * DN_C), F32), SDS((N_CHUNK, DN_W), F32)],
        scratch_shapes=[pltpu.VMEM((DN_H, DN_E, DN_E), F32)],
        compiler_params=_cparams(("arbitrary",)),
    )(u, w, qg, kdec, intra, egl, states, do)


OUT_T = 256


def _pool_consts(rows_total, t0, halo_before):
    lane = lax.broadcasted_iota(jnp.int32, (rows_total, POOL_W), 1)
    row = lax.broadcasted_iota(jnp.int32, (rows_total, POOL_W), 0)
    grp = (lane >= 64).astype(jnp.int32) + (lane >= 128).astype(jnp.int32) + (lane >= 192).astype(jnp.int32)
    win = jnp.where(grp == 0, 2, jnp.where(grp == 1, 4, jnp.where(grp == 2, 8, 16)))
    pos = t0 + row - halo_before
    cnt = jnp.minimum(pos + 1, win).astype(F32)
    return grp, cnt


def _pool_select(grp, s2, s4, s8, s16):
    return jnp.where(grp == 0, s2, jnp.where(grp == 1, s4, jnp.where(grp == 2, s8, s16)))


def _pooled(u_ext, t0):
    n = u_ext.shape[0]
    grp, cnt = _pool_consts(n, t0, POOL_HALO)
    s2 = u_ext + pltpu.roll(u_ext, 1, 0)
    s4 = s2 + pltpu.roll(s2, 2, 0)
    s8 = s4 + pltpu.roll(s4, 4, 0)
    s16 = s8 + pltpu.roll(s8, 8, 0)
    out = _pool_select(grp, s2, s4, s8, s16) / jnp.maximum(cnt, 1.0) - u_ext
    return out[POOL_HALO:, :]


def _merge_weights(l1, l4, l16):
    m = jnp.maximum(jnp.maximum(l1, l4), l16)
    e1 = jnp.exp(l1 - m)
    e4 = jnp.exp(l4 - m)
    e16 = jnp.exp(l16 - m)
    inv = 1.0 / (e1 + e4 + e16)
    return e1 * inv, e4 * inv, e16 * inv


def _out_parts(ol1_ref, ol4_ref, ol16_ref, pu_ref, puh_ref, odn_ref, z_ref, wbd_ref, i, t):
    w1, w4, w16 = _merge_weights(_slabs_load(ol1_ref, 2, 2), _slabs_load(ol4_ref, 2, 2), _slabs_load(ol16_ref, 2, 2))
    ya = w1 * _slabs_load(ol1_ref, 0, 2) + w4 * _slabs_load(ol4_ref, 0, 2) + w16 * _slabs_load(ol16_ref, 0, 2)
    halo = jnp.where(i > 0, puh_ref[...], 0.0)
    pooled = _pooled(jnp.concatenate([halo, pu_ref[...]], axis=0), i * t)
    pw = _dot(pooled.astype(BF16), wbd_ref[...])
    return ya, pooled, pw, (w1, w4, w16)


def _out_specs_common(t):
    def row(w, cb=0):
        return pl.BlockSpec((t, w), lambda i: (i, cb))

    halo = pl.BlockSpec((POOL_HALO, POOL_W),
                        lambda i: (jnp.maximum(i * (t // POOL_HALO) - 1, 0), R_PU // POOL_W))
    full = lambda a, b: pl.BlockSpec((a, b), lambda i: (0, 0))
    return [_slab_spec(4, t), _slab_spec(4, t), _slab_spec(4, t), row(POOL_W, R_PU // POOL_W), halo, row(DN_W), row(DN_W, R_DZ // DN_W),
            full(POOL_W, POOL_W), full(1, POOL_W), full(1, DN_W), full(D_MODEL, D_MODEL)]


def mix_out_fwd(x, ol1, ol4, ol16, rest, odn, wbd, scale, onorm_b, wout, name):
    t = OUT_T

    def body(x_ref, ol1_ref, ol4_ref, ol16_ref, pu_ref, puh_ref, odn_ref, z_ref, wbd_ref, sc_ref, on_ref, wo_ref, o_ref):
        i = pl.program_id(0)
        ya, _pooled_v, pw, _w = _out_parts(ol1_ref, ol4_ref, ol16_ref, pu_ref, puh_ref, odn_ref, z_ref, wbd_ref, i, t)
        yb = pw * sc_ref[...]
        acc = x_ref[...] + _dot(ya.astype(BF16), wo_ref[0:256, :]) + _dot(yb.astype(BF16), wo_ref[256:512, :])
        for hd in range(DN_H):
            sl = slice(DN_E * hd, DN_E * hd + DN_E)
            oh, _r = _rms_stats(odn_ref[:, sl])
            z = z_ref[:, sl]
            yc = oh * on_ref[:, sl] * (z * _sigmoid(z))
            acc = acc + _dot(yc.astype(BF16), wo_ref[512 + DN_E * hd:512 + DN_E * hd + DN_E, :])
        o_ref[...] = acc

    return pl.pallas_call(
        body, name=name, grid=(SEQ // t,),
        in_specs=[pl.BlockSpec((t, D_MODEL), lambda i: (i, 0))] + _out_specs_common(t),
        out_specs=pl.BlockSpec((t, D_MODEL), lambda i: (i, 0)),
        out_shape=SDS((SEQ, D_MODEL), F32),
        compiler_params=_cparams(("arbitrary",)),
    )(x, ol1, ol4, ol16, rest, rest, odn, rest, wbd, scale, onorm_b, wout)


def mix_out_bwd(dxo, ol1, ol4, ol16, rest, odn, wbd, scale, onorm_b, wout, headsum, name):
    t = OUT_T

    def body(dxo_ref, ol1_ref, ol4_ref, ol16_ref, pu_ref, puh_ref, odn_ref, z_ref, wbd_ref, sc_ref, on_ref, wo_ref, hs_ref,
             dwo_ref, d1_ref, d4_ref, d16_ref, dpl_ref, dodn_ref, dz_ref, dsc_ref, don_ref, dwbd_ref):
        i = pl.program_id(0)

        @pl.when(i == 0)
        def _():
            dwo_ref[...] = jnp.zeros_like(dwo_ref)
            dsc_ref[...] = jnp.zeros_like(dsc_ref)
            don_ref[...] = jnp.zeros_like(don_ref)
            dwbd_ref[...] = jnp.zeros_like(dwbd_ref)

        ya, pooled, pw, (w1, w4, w16) = _out_parts(ol1_ref, ol4_ref, ol16_ref, pu_ref, puh_ref, odn_ref, z_ref, wbd_ref, i, t)
        sc = sc_ref[...]
        dxb = dxo_ref[...].astype(BF16)
        dwo_ref[0:256, :] += _dot_tn(ya.astype(BF16), dxb)
        dwo_ref[256:512, :] += _dot_tn((pw * sc).astype(BF16), dxb)
        dya = _dot_nt(dxb, wo_ref[0:256, :])
        o1 = _slabs_load(ol1_ref, 0, 2)
        o4 = _slabs_load(ol4_ref, 0, 2)
        o16 = _slabs_load(ol16_ref, 0, 2)
        hs = hs_ref[...]
        s1 = _dot(dya * o1, hs, HI)
        s4 = _dot(dya * o4, hs, HI)
        s16 = _dot(dya * o16, hs, HI)
        sbar = w1 * s1 + w4 * s4 + w16 * s16
        _slabs_store(d1_ref, 0, w1 * dya)
        _slabs_store(d1_ref, 2, w1 * (s1 - sbar))
        _slabs_store(d4_ref, 0, w4 * dya)
        _slabs_store(d4_ref, 2, w4 * (s4 - sbar))
        _slabs_store(d16_ref, 0, w16 * dya)
        _slabs_store(d16_ref, 2, w16 * (s16 - sbar))
        dyb = _dot_nt(dxb, wo_ref[256:512, :])
        dsc_ref[...] += jnp.sum(dyb * pw, axis=0, keepdims=True)
        dpw = (dyb * sc).astype(BF16)
        dwbd_ref[...] += _dot_tn(pooled.astype(BF16), dpw)
        dpl_ref[...] = _dot_nt(dpw, wbd_ref[...])
        for hd in range(DN_H):
            sl = slice(DN_E * hd, DN_E * hd + DN_E)
            rows_w = slice(512 + DN_E * hd, 512 + DN_E * hd + DN_E)
            oh, r = _rms_stats(odn_ref[:, sl])
            z = z_ref[:, sl]
            sg = _sigmoid(z)
            sz = z * sg
            nw = on_ref[:, sl]
            on = oh * nw
            dwo_ref[rows_w, :] += _dot_tn((on * sz).astype(BF16), dxb)
            dyc = _dot_nt(dxb, wo_ref[rows_w, :])
            dz_ref[:, sl] = dyc * on * (sg * (1.0 + z * (1.0 - sg)))
            dx, dw = _rms_bwd(oh, r, nw, dyc * sz)
            dodn_ref[:, sl] = dx
            don_ref[:, sl] += dw

    row = lambda w: pl.BlockSpec((t, w), lambda i: (i, 0))
    full = lambda a, b: pl.BlockSpec((a, b), lambda i: (0, 0))
    return pl.pallas_call(
        body, name=name, grid=(SEQ // t,),
        in_specs=[row(D_MODEL)] + _out_specs_common(t) + [full(ATT_W, ATT_W)],
        out_specs=[full(D_MODEL, D_MODEL), _slab_spec(4, t), _slab_spec(4, t), _slab_spec(4, t), row(POOL_W), row(DN_W), row(DN_W),
                   full(1, POOL_W), full(1, DN_W), full(POOL_W, POOL_W)],
        out_shape=[SDS((D_MODEL, D_MODEL), F32), SDS((4, SEQ, 128), F32), SDS((4, SEQ, 128), F32), SDS((4, SEQ, 128), F32),
                   SDS((SEQ, POOL_W), F32), SDS((SEQ, DN_W), F32), SDS((SEQ, DN_W), F32),
                   SDS((1, POOL_W), F32), SDS((1, DN_W), F32), SDS((POOL_W, POOL_W), F32)],
        compiler_params=_cparams(("arbitrary",)),
    )(dxo, ol1, ol4, ol16, rest, rest, odn, rest, wbd, scale, onorm_b, wout, headsum)


def pool_bwd(dpooled, name):
    t = 512
    nt = SEQ // t

    def body(d_ref, dn_ref, o_ref):
        i = pl.program_id(0)
        halo = jnp.where(i < nt - 1, dn_ref[...], 0.0)
        d_ext = jnp.concatenate([d_ref[...], halo], axis=0)
        n = t + POOL_HALO
        grp, cnt = _pool_consts(n, i * t, 0)
        dq = d_ext / cnt
        s2 = dq + pltpu.roll(dq, n - 1, 0)
        s4 = s2 + pltpu.roll(s2, n - 2, 0)
        s8 = s4 + pltpu.roll(s4, n - 4, 0)
        s16 = s8 + pltpu.roll(s8, n - 8, 0)
        o_ref[...] = (_pool_select(grp, s2, s4, s8, s16) - d_ext)[0:t, :]

    return pl.pallas_call(
        body, name=name, grid=(nt,),
        in_specs=[pl.BlockSpec((t, POOL_W), lambda i: (i, 0)),
                  pl.BlockSpec((POOL_HALO, POOL_W),
                               lambda i: (jnp.minimum((i + 1) * (t // POOL_HALO), SEQ // POOL_HALO - 1), 0))],
        out_specs=pl.BlockSpec((t, POOL_W), lambda i: (i, 0)),
        out_shape=SDS((SEQ, POOL_W), F32),
        compiler_params=_cparams(("arbitrary",)),
    )(dpooled, dpooled)


N_PEER = N_DEV - 1
ANY_SPEC = pl.BlockSpec(memory_space=pl.ANY)


class Exchange:
    def __init__(self, arrays, mode):
        self.arrays = list(arrays)
        self.mode = mode
        n = len(self.arrays)
        if mode == "scatter":
            self.out_shape = [SDS(a.shape, a.dtype) for a in self.arrays]
        else:
            self.out_shape = [SDS((N_DEV,) + a.shape, a.dtype) for a in self.arrays]
        self.scratch = [pltpu.SemaphoreType.DMA((n * N_PEER,)), pltpu.SemaphoreType.DMA((n * N_PEER,)),
                        pltpu.SemaphoreType.DMA((n,))]

    @staticmethod
    def _place():
        x, y, c = lax.axis_index("x"), lax.axis_index("y"), lax.axis_index("c")
        chips = [(1 - x, y), (x, 1 - y), (1 - x, 1 - y)]
        return x, y, c, chips

    @staticmethod
    def _copy(sems, a, k, src, dst, to):
        send_sems, recv_sems, _ = sems
        return pltpu.make_async_remote_copy(
            src_ref=src, dst_ref=dst, send_sem=send_sems.at[a * N_PEER + k], recv_sem=recv_sems.at[a * N_PEER + k],
            device_id=to, device_id_type=MESH)

    def _scatter_peers(self):
        x, y, c, _ = self._place()
        out = []
        for fx, fy, fc in ((0, 0, 1), (1, 0, 0), (0, 1, 0), (1, 1, 0), (1, 0, 1), (0, 1, 1), (1, 1, 1)):
            px, py, pc = x ^ fx, y ^ fy, c ^ fc
            out.append(((px, py, pc), 4 * px + 2 * py + pc))
        return 4 * x + 2 * y + c, out

    def _local(self, ins, outs, sems, a, me):
        src = ins[a].at[me] if self.mode == "scatter" else ins[a]
        return pltpu.make_async_copy(src, outs[a].at[me], sems[2].at[a])

    def start(self, ins, outs, sems):
        if self.mode == "scatter":
            me, peers = self._scatter_peers()
            for a in range(len(ins)):
                self._local(ins, outs, sems, a, me).start()
                for k, (peer, pidx) in enumerate(peers):
                    self._copy(sems, a, k, ins[a].at[pidx], outs[a].at[me], peer).start()
            return
        x, y, c, chips = self._place()
        me = 4 * x + 2 * y + c
        for a in range(len(ins)):
            self._local(ins, outs, sems, a, me).start()
            self._copy(sems, a, 0, ins[a], outs[a].at[me], (x, y, 1 - c)).start()
            for j, (cx, cy) in enumerate(chips):
                self._copy(sems, a, 1 + j, ins[a], outs[a].at[me], (cx, cy, c)).start()

    def finish(self, ins, outs, sems):
        n = len(ins)
        if self.mode == "scatter":
            me, peers = self._scatter_peers()
            for a in range(n):
                for k, (peer, pidx) in enumerate(peers):
                    self._copy(sems, a, k, ins[a].at[pidx], outs[a].at[pidx], peer).wait_recv()
            for a in range(n):
                for k, (peer, pidx) in enumerate(peers):
                    self._copy(sems, a, k, ins[a].at[pidx], outs[a].at[me], peer).wait_send()
                self._local(ins, outs, sems, a, me).wait()
            return
        x, y, c, chips = self._place()
        me = 4 * x + 2 * y + c
        sib = (x, y, 1 - c)
        for a in range(n):
            for j, (cx, cy) in enumerate(chips):
                blk = outs[a].at[4 * cx + 2 * cy + c]
                self._copy(sems, a, 1 + j, ins[a], blk, (cx, cy, c)).wait_recv()
                self._copy(sems, a, 4 + j, blk, blk, sib).start()
        for a in range(n):
            self._copy(sems, a, 0, ins[a], outs[a].at[4 * x + 2 * y + (1 - c)], sib).wait_recv()
            for j, (cx, cy) in enumerate(chips):
                blk = outs[a].at[4 * cx + 2 * cy + (1 - c)]
                self._copy(sems, a, 4 + j, blk, blk, sib).wait_recv()
        for a in range(n):
            for k in range(N_PEER):
                self._copy(sems, a, k, ins[a], outs[a].at[me], sib).wait_send()
            self._local(ins, outs, sems, a, me).wait()


def run_exchange(exch, name):
    n = len(exch.arrays)

    def body(*refs):
        ins, outs, sems = refs[:n], refs[n:2 * n], refs[2 * n:]
        exch.start(ins, outs, sems)
        exch.finish(ins, outs, sems)

    return pl.pallas_call(
        body, name=name, in_specs=[ANY_SPEC] * n, out_specs=[ANY_SPEC] * n, out_shape=exch.out_shape,
        scratch_shapes=exch.scratch,
    )(*exch.arrays)


def _call(body, *, name, grid, in_specs, out_specs, out_shape, scratch_shapes, sem, args, exch=None):
    if exch is None:
        res = pl.pallas_call(body, name=name, grid=grid, in_specs=in_specs, out_specs=out_specs, out_shape=out_shape,
                             scratch_shapes=scratch_shapes, compiler_params=_cparams(sem))(*args)
        return res, None
    single = not isinstance(out_shape, (list, tuple))
    out_specs_l = [out_specs] if single else list(out_specs)
    out_shape_l = [out_shape] if single else list(out_shape)
    n_in, n_out, n_scr, m = len(in_specs), len(out_specs_l), len(scratch_shapes), len(exch.arrays)

    def wrapped(*refs):
        p = 0
        ins = refs[p:p + n_in]; p += n_in
        xin = refs[p:p + m]; p += m
        outs = refs[p:p + n_out]; p += n_out
        xout = refs[p:p + m]; p += m
        scr = refs[p:p + n_scr]; p += n_scr
        sems = refs[p:]
        ids = [pl.program_id(ax) for ax in range(len(grid))]
        first = functools.reduce(jnp.logical_and, [i == 0 for i in ids])
        last = functools.reduce(jnp.logical_and, [i == g - 1 for i, g in zip(ids, grid)])

        @pl.when(first)
        def _():
            exch.start(xin, xout, sems)

        body(*ins, *outs, *scr)

        @pl.when(last)
        def _():
            exch.finish(xin, xout, sems)

    res = pl.pallas_call(
        wrapped, name=name, grid=grid, in_specs=list(in_specs) + [ANY_SPEC] * m,
        out_specs=out_specs_l + [ANY_SPEC] * m, out_shape=out_shape_l + exch.out_shape,
        scratch_shapes=list(scratch_shapes) + exch.scratch, compiler_params=_cparams(sem),
    )(*args, *exch.arrays)
    outs = res[:n_out]
    return (outs[0] if single else outs), res[n_out:]


def _adam_math(w, g, m, v):
    m2 = ADAM_B1 * m + (1.0 - ADAM_B1) * g
    v2 = ADAM_B2 * v + (1.0 - ADAM_B2) * (g * g)
    m_hat = m2 / (1.0 - ADAM_B1 ** ADAM_STEP)
    v_hat = v2 / (1.0 - ADAM_B2 ** ADAM_STEP)
    delta = -ADAM_LR * (m_hat / (jnp.sqrt(v_hat) + ADAM_EPS) + ADAM_WD * w)
    return delta, m2, v2


ADAM_ROW_BLOCKS = 2


def adam_shard(parts0, parts1, w, m, v, name):
    _, r, c = w.shape
    rb = r // ADAM_ROW_BLOCKS

    def body(p0_ref, p1_ref, w_ref, m_ref, v_ref, g_ref, d_ref, m2_ref, v2_ref):
        def run(p_ref):
            g = p_ref[0].astype(F32)
            for i in range(1, N_DEV):
                g = g + p_ref[i].astype(F32)
            delta, m2, v2 = _adam_math(w_ref[0], g, m_ref[0], v_ref[0])
            g_ref[0] = g
            d_ref[0] = delta
            m2_ref[0] = m2
            v2_ref[0] = v2

        @pl.when(pl.program_id(0) == 0)
        def _():
            run(p0_ref)

        @pl.when(pl.program_id(0) == 1)
        def _():
            run(p1_ref)

    def p_spec(layer):
        row = (lambda l, j: jnp.where(l == 0, j, ADAM_ROW_BLOCKS - 1)) if layer == 0 else (lambda l, j: jnp.where(l == 1, j, 0))
        return pl.BlockSpec((N_DEV, rb, c), lambda l, j: (0, row(l, j), 0))

    blk = pl.BlockSpec((1, rb, c), lambda l, j: (l, j, 0))
    return pl.pallas_call(
        body, name=name, grid=(DEPTH, ADAM_ROW_BLOCKS),
        in_specs=[p_spec(0), p_spec(1), blk, blk, blk], out_specs=[blk] * 4,
        out_shape=[SDS(w.shape, F32)] * 4,
        compiler_params=_cparams(("arbitrary", "arbitrary")),
    )(parts0, parts1, w, m, v)


def parts_sum(parts0, parts1, name):
    _, r, c = parts0.shape

    def body(p0_ref, p1_ref, g_ref):
        def run(p_ref):
            g = p_ref[0].astype(F32)
            for i in range(1, N_DEV):
                g = g + p_ref[i].astype(F32)
            g_ref[0] = g

        @pl.when(pl.program_id(0) == 0)
        def _():
            run(p0_ref)

        @pl.when(pl.program_id(0) == 1)
        def _():
            run(p1_ref)

    full = pl.BlockSpec((N_DEV, r, c), lambda l: (0, 0, 0))
    return pl.pallas_call(
        body, name=name, grid=(DEPTH,), in_specs=[full, full],
        out_specs=pl.BlockSpec((1, r, c), lambda l: (l, 0, 0)), out_shape=SDS((DEPTH, r, c), F32),
        compiler_params=_cparams(("arbitrary",)),
    )(parts0, parts1)


def adam_given(g, w, m, v, name):
    _, r, c = w.shape

    def body(g_ref, w_ref, m_ref, v_ref, d_ref, m2_ref, v2_ref):
        delta, m2, v2 = _adam_math(w_ref[0], g_ref[0], m_ref[0], v_ref[0])
        d_ref[0] = delta
        m2_ref[0] = m2
        v2_ref[0] = v2

    blk = pl.BlockSpec((1, r, c), lambda l: (l, 0, 0))
    return pl.pallas_call(
        body, name=name, grid=(DEPTH,), in_specs=[blk] * 4, out_specs=[blk] * 3, out_shape=[SDS(w.shape, F32)] * 3,
        compiler_params=_cparams(("arbitrary",)),
    )(g, w, m, v)


def adam_small(parts, w, m, v, name):
    def body(p_ref, w_ref, m_ref, v_ref, g_ref, d_ref, m2_ref, v2_ref):
        g = p_ref[0]
        for i in range(1, N_DEV):
            g = g + p_ref[i]
        delta, m2, v2 = _adam_math(w_ref[...], g, m_ref[...], v_ref[...])
        g_ref[...] = g
        d_ref[...] = delta
        m2_ref[...] = m2
        v2_ref[...] = v2

    return pl.pallas_call(
        body, name=name, out_shape=[SDS(w.shape, F32)] * 4, compiler_params=_cparams(),
    )(parts, w, m, v)


def _rot_cols(w):
    w4 = w.reshape(w.shape[0], 4, 2, 32)
    return jnp.stack([-w4[:, :, 1], w4[:, :, 0]], axis=2).reshape(w.shape[0], ATT_W)


def _rot_cols_t(dw_rot):
    d4 = dw_rot.reshape(dw_rot.shape[0], 4, 2, 32)
    return jnp.stack([d4[:, :, 1], -d4[:, :, 0]], axis=2).reshape(dw_rot.shape[0], ATT_W)


def build_wext(w_in):
    aq, ak, av, pu = w_in[:, 0:256], w_in[:, 256:512], w_in[:, 512:768], w_in[:, 768:1024]
    dqkvz = w_in[:, 1024:3072]
    gates = jnp.pad(w_in[:, 3072:3080], ((0, 0), (0, GATE_W - 2 * DN_H)))
    return jnp.concatenate([aq, ak, av, _rot_cols(aq), _rot_cols(ak), dqkvz, gates, pu], axis=1)


def fold_dwext(d):
    b = EXT_ATT
    aq = d[:, 0:256] + _rot_cols_t(d[:, 768:1024])
    ak = d[:, 256:512] + _rot_cols_t(d[:, 1024:1280])
    av = d[:, 512:768]
    dqkvz = d[:, b:b + 2048]
    gates = d[:, b + R_G:b + R_G + 2 * DN_H]
    pu = d[:, b + R_PU:b + R_PU + 256]
    return jnp.concatenate([aq, ak, av, pu, dqkvz, gates], axis=1)


def _block_diag(pw):
    z = jnp.zeros((4, 64, 4, 64), pw.dtype)
    for g in range(4):
        z = z.at[g, :, g, :].set(pw[g])
    return z.reshape(POOL_W, POOL_W)


def _diag_blocks(m):
    m4 = m.reshape(4, 64, 4, 64)
    return jnp.stack([m4[g, :, g, :] for g in range(4)], axis=0)


def _lanes(v, reps):
    return jnp.repeat(v, reps)[None, :]


def layer_fwd(p, xa, cos, sin, l, host=None):
    host = host or {}

    def carried(key):
        return host[key][0] if key in host else None

    def done(key, xo):
        if key in host:
            host[key][1](xo)

    xb, xo = ffn_fwd(xa, p["n1"], *p["f1"], f"ffn1_fwd_{l}", carried("ffn1"))
    done("ffn1", xo)
    att, rest = mix_in_fwd(xb, p["nm"], p["wext"], cos, sin, f"mix_in_fwd_{l}")
    ols = [att_fwd_s(att, d, f"att_fwd_{l}_{d}") for d in DILATIONS]
    qkv = dn_prep_fwd(rest, p["conv"], f"dn_prep_fwd_{l}")
    dn, xo = dn_intra_fwd(qkv, rest, p["alog"], p["dtb"], f"dn_intra_fwd_{l}", carried("dn_intra"))
    done("dn_intra", xo)
    (odn, states), xo = dn_inter_fwd(*dn, f"dn_inter_fwd_{l}", carried("dn_inter"))
    done("dn_inter", xo)
    xc = mix_out_fwd(xb, ols[0], ols[1], ols[2], rest, odn, p["wbd"], p["scale"], p["onorm"], p["wout"], f"mix_out_fwd_{l}")
    xd, xo = ffn_fwd(xc, p["n2"], *p["f2"], f"ffn2_fwd_{l}", carried("ffn2"))
    done("ffn2", xo)
    return xd, dict(xa=xa, xb=xb, xc=xc, att=att, rest=rest, ols=ols, qkv=qkv, dn=dn, odn=odn, states=states)


def layer_bwd(p, s, dx, cos, sin, headsum, l, scatter=False, carry=None):
    blocks = lambda ws: [w_.reshape(N_DEV, FF_BLK, D_MODEL) for w_ in ws]
    (dx, *d_f2, d_n2), carried = ffn_bwd(s["xc"], dx, p["n2"], *p["f2"], f"ffn2_bwd_{l}", carry)
    (d_wout, dol1, dol4, dol16, dpooled, dodn, dz, dscale, donorm, dwbd) = mix_out_bwd(
        dx, s["ols"][0], s["ols"][1], s["ols"][2], s["rest"], s["odn"], p["wbd"], p["scale"], p["onorm"], p["wout"],
        headsum, f"mix_out_bwd_{l}")
    dpu = pool_bwd(dpooled, f"pool_bwd_{l}")
    f2 = blocks(d_f2)
    d_dn = dn_inter_bwd(*s["dn"], s["states"], dodn, f"dn_inter_bwd_{l}")
    dqkv, dg, dalog, ddtb = dn_intra_bwd(s["qkv"], s["rest"], p["alog"], p["dtb"], *d_dn, f"dn_intra_bwd_{l}")
    d_dqkv, dconv = dn_prep_bwd(s["rest"], p["conv"], dqkv, f"dn_prep_bwd_{l}")
    datts = [att_bwd_s(s["att"], ol, dol, d, f"att_bwd_{l}_{d}")
             for d, ol, dol in zip(DILATIONS, s["ols"], (dol1, dol4, dol16))]
    dproj = assemble_dproj(datts, cos, sin, d_dqkv, dz, dg, dpu, f"assemble_dproj_{l}")
    dx, d_wext, d_nm = linear_bwd(s["xb"], dx, p["nm"], dproj, p["wext"], f"mix_in_bwd_{l}")
    d_win = fold_dwext(d_wext).reshape(D_MODEL, N_DEV, IN_BLK).transpose(1, 0, 2).astype(BF16)
    io = [d_win, d_wout.reshape(N_DEV, D_MODEL // N_DEV, D_MODEL).astype(BF16)]
    (dx, *d_f1, d_n1), xo = ffn_bwd(s["xa"], dx, p["n1"], *p["f1"], f"ffn1_bwd_{l}",
                                    Exchange(f2 + io, "scatter") if scatter else None)
    if scatter:
        f2, io = list(xo[:3]), list(xo[3:])
    big = dict(f1=blocks(d_f1), f2=f2, io=io)
    small = dict(ffn1_norm=d_n1[0], mix_norm=d_nm[0], ffn2_norm=d_n2[0], pool_w=_diag_blocks(dwbd),
                 pool_scale=dscale[0], dn_a_log=dalog.reshape(DN_H, DN_E).sum(-1),
                 dn_dt_bias=ddtb.reshape(DN_H, DN_E).sum(-1),
                 dn_out_norm=donorm.reshape(DN_H, DN_E).sum(0), dn_conv_w=dconv)
    return dx, big, small, carried


def small_operands(l, pool_w, pool_scale, dn_out_norm, dn_a_log, dn_dt_bias, ffn1_norm, mix_norm, ffn2_norm):
    return dict(
        wbd=_block_diag(pool_w[l]).astype(BF16),
        scale=pool_scale[l][None, :],
        onorm=jnp.tile(dn_out_norm[l], DN_H)[None, :],
        alog=_lanes(dn_a_log[l], DN_E),
        dtb=_lanes(dn_dt_bias[l], DN_E),
        n1=ffn1_norm[l][None, :], nm=mix_norm[l][None, :], n2=ffn2_norm[l][None, :])


def set_mixer_weights(p, win_g, wout_g, conv_g):
    p["wext"] = build_wext(win_g.transpose(1, 0, 2).reshape(D_MODEL, IN_W))
    p["wout"] = wout_g.reshape(D_MODEL, D_MODEL)
    p["conv"] = conv_g.transpose(1, 0, 2).reshape(DN_CONV, 3 * DN_W)


def rope_tables(pos):
    inv_freq = 10000.0 ** (-jnp.arange(0, ATT_E, 2, dtype=F32) / ATT_E)
    ang = pos.astype(F32)[:, None] * inv_freq
    return jnp.tile(jnp.cos(ang), (1, 8)), jnp.tile(jnp.sin(ang), (1, 8))


def head_sum_matrix():
    return jnp.kron(jnp.eye(4, dtype=F32), jnp.ones((ATT_E, ATT_E), F32))


SMALL_NAMES = ("ffn1_norm", "mix_norm", "ffn2_norm", "pool_w", "pool_scale", "dn_a_log", "dn_dt_bias",
               "dn_out_norm", "final_norm", "dn_conv_w")


PACK_UNIT = 8 * 128


def _pack_rows(n):
    return -(-n // PACK_UNIT) * 8


def _pack(parts):
    rows = []
    for p in parts:
        flat = p.reshape(-1)
        r = _pack_rows(flat.shape[0])
        rows.append(jnp.pad(flat, (0, r * 128 - flat.shape[0])).reshape(r, 128))
    return jnp.concatenate(rows, axis=0)


def _unpack(packed, shapes):
    out, row = [], 0
    for s in shapes:
        n = math.prod(s)
        r = _pack_rows(n)
        out.append(packed[row:row + r].reshape(-1)[:n].reshape(s))
        row += r
    return out


def kernel(x, positions, ffn1_norm, ffn1_w_gate, ffn1_w_up, ffn1_w_down, mix_norm, w_in, pool_w, pool_scale, dn_conv_w, dn_a_log, dn_dt_bias, dn_out_norm, w_out, ffn2_norm, ffn2_w_gate, ffn2_w_up, ffn2_w_down, final_norm, loss_target, m_ffn1_norm, m_ffn1_w_gate, m_ffn1_w_up, m_ffn1_w_down, m_mix_norm, m_w_in, m_pool_w, m_pool_scale, m_dn_conv_w, m_dn_a_log, m_dn_dt_bias, m_dn_out_norm, m_w_out, m_ffn2_norm, m_ffn2_w_gate, m_ffn2_w_up, m_ffn2_w_down, m_final_norm, v_ffn1_norm, v_ffn1_w_gate, v_ffn1_w_up, v_ffn1_w_down, v_mix_norm, v_w_in, v_pool_w, v_pool_scale, v_dn_conv_w, v_dn_a_log, v_dn_dt_bias, v_dn_out_norm, v_w_out, v_ffn2_norm, v_ffn2_w_gate, v_ffn2_w_up, v_ffn2_w_down, v_final_norm):
    me = 4 * lax.axis_index("x") + 2 * lax.axis_index("y") + lax.axis_index("c")
    x0 = x[0]
    target = loss_target[0]

    cos, sin = rope_tables(positions[0])
    headsum = head_sum_matrix()

    layers = [small_operands(l, pool_w, pool_scale, dn_out_norm, dn_a_log, dn_dt_bias, ffn1_norm, mix_norm, ffn2_norm)
              for l in range(DEPTH)]

    def whole(gathered):
        return gathered.reshape(D_FF, D_MODEL)

    def gather_ffn1(l):
        def on_done(xo):
            layers[l]["f1"] = tuple(whole(g) for g in xo)
        return Exchange(ffn_shard_operands(ffn1_w_gate[l], ffn1_w_up[l], ffn1_w_down[l]), "gather"), on_done

    def gather_mixer(l):
        def on_done(xo):
            set_mixer_weights(layers[l], *xo)
        return Exchange([w_in[l].astype(BF16), w_out[l].astype(BF16), dn_conv_w[l]], "gather"), on_done

    gathered_f2 = {}

    def gather_ffn2_part(l, part):
        def on_done(xo):
            gathered_f2[(l, part)] = [whole(g) for g in xo]
            if (l, 0) in gathered_f2 and (l, 1) in gathered_f2:
                layers[l]["f2"] = tuple(gathered_f2[(l, 0)] + gathered_f2[(l, 1)])
        ops = ffn_shard_operands(ffn2_w_gate[l], ffn2_w_up[l], ffn2_w_down[l])
        return Exchange(ops[:2] if part == 0 else ops[2:], "gather"), on_done

    first, on_first = gather_ffn1(0)
    on_first(run_exchange(first, "gather_ffn1_0"))
    saved = []
    xa = x0
    for l in range(DEPTH):
        host = {"ffn1": gather_mixer(l), "dn_intra": gather_ffn2_part(l, 0), "dn_inter": gather_ffn2_part(l, 1)}
        if l + 1 < DEPTH:
            host["ffn2"] = gather_ffn1(l + 1)
        xa, s = layer_fwd(layers[l], xa, cos, sin, l, host)
        saved.append(s)

    loss_row, dx, d_final = loss_head(xa, final_norm[None, :], target, "loss_head")
    loss = lax.psum(loss_row[0, 0], ("x", "y", "c"))

    small = {}
    big_parts = [None] * DEPTH
    carry = None
    for l in reversed(range(DEPTH)):
        dx, big, small[l], carried = layer_bwd(layers[l], saved[l], dx, cos, sin, headsum, l, True, carry)
        if carried is not None:
            big_parts[l + 1]["f1"] = list(carried)
        big_parts[l] = big
        carry = Exchange(big["f1"], "scatter")
    big_parts[0]["f1"] = list(run_exchange(carry, "scatter_ffn1_0"))
    grad_x = dx[None]

    small_shapes = {"ffn1_norm": (DEPTH, D_MODEL), "mix_norm": (DEPTH, D_MODEL), "ffn2_norm": (DEPTH, D_MODEL),
                    "pool_w": (DEPTH, 4, 64, 64), "pool_scale": (DEPTH, POOL_W), "dn_a_log": (DEPTH, DN_H),
                    "dn_dt_bias": (DEPTH, DN_H), "dn_out_norm": (DEPTH, DN_E), "final_norm": (D_MODEL,),
                    "dn_conv_w": (DEPTH, DN_CONV, 3 * DN_W)}
    g_small = {n: (d_final[0] if n == "final_norm" else jnp.stack([small[l][n] for l in range(DEPTH)]))
               for n in SMALL_NAMES}
    (small_parts,) = run_exchange(Exchange([_pack([g_small[n] for n in SMALL_NAMES])], "gather"), "gather_small_grads")

    def conv_full(a):
        return lax.dynamic_update_slice(jnp.zeros((DEPTH, DN_CONV, 3 * DN_W), F32), a, (0, 0, me * (3 * DN_W // N_DEV)))

    given = dict(ffn1_norm=(ffn1_norm, m_ffn1_norm, v_ffn1_norm), mix_norm=(mix_norm, m_mix_norm, v_mix_norm),
                 ffn2_norm=(ffn2_norm, m_ffn2_norm, v_ffn2_norm), pool_w=(pool_w, m_pool_w, v_pool_w),
                 pool_scale=(pool_scale, m_pool_scale, v_pool_scale), dn_a_log=(dn_a_log, m_dn_a_log, v_dn_a_log),
                 dn_dt_bias=(dn_dt_bias, m_dn_dt_bias, v_dn_dt_bias),
                 dn_out_norm=(dn_out_norm, m_dn_out_norm, v_dn_out_norm),
                 final_norm=(final_norm, m_final_norm, v_final_norm),
                 dn_conv_w=(conv_full(dn_conv_w), conv_full(m_dn_conv_w), conv_full(v_dn_conv_w)))
    packed_wmv = [_pack([given[n][k] for n in SMALL_NAMES]) for k in range(3)]
    small_out = adam_small(small_parts, *packed_wmv, "adam_small")
    shapes = [small_shapes[n] for n in SMALL_NAMES]
    small_res = {n: [] for n in SMALL_NAMES}
    for arr in small_out:
        for n, v_ in zip(SMALL_NAMES, _unpack(arr, shapes)):
            if n == "dn_conv_w":
                v_ = lax.dynamic_slice(v_, (0, 0, me * (3 * DN_W // N_DEV)), (DEPTH, DN_CONV, 3 * DN_W // N_DEV))
            small_res[n].append(v_)

    def parts_of(group, idx):
        return [big_parts[l][group][idx] for l in range(DEPTH)]

    def adam_transposed(group, idx, w, m, v, name):
        g = parts_sum(*parts_of(group, idx), f"sum_{name}").transpose(0, 2, 1)
        return [g] + list(adam_given(g, w, m, v, f"adam_{name}"))

    big_res = dict(
        ffn1_w_gate=adam_transposed("f1", 0, ffn1_w_gate, m_ffn1_w_gate, v_ffn1_w_gate, "ffn1_gate"),
        ffn1_w_up=adam_transposed("f1", 1, ffn1_w_up, m_ffn1_w_up, v_ffn1_w_up, "ffn1_up"),
        ffn1_w_down=adam_shard(*parts_of("f1", 2), ffn1_w_down, m_ffn1_w_down, v_ffn1_w_down, "adam_ffn1_down"),
        ffn2_w_gate=adam_transposed("f2", 0, ffn2_w_gate, m_ffn2_w_gate, v_ffn2_w_gate, "ffn2_gate"),
        ffn2_w_up=adam_transposed("f2", 1, ffn2_w_up, m_ffn2_w_up, v_ffn2_w_up, "ffn2_up"),
        ffn2_w_down=adam_shard(*parts_of("f2", 2), ffn2_w_down, m_ffn2_w_down, v_ffn2_w_down, "adam_ffn2_down"),
        w_in=adam_shard(*parts_of("io", 0), w_in, m_w_in, v_w_in, "adam_w_in"),
        w_out=adam_shard(*parts_of("io", 1), w_out, m_w_out, v_w_out, "adam_w_out"),
    )

    order = ("ffn1_norm", "ffn1_w_gate", "ffn1_w_up", "ffn1_w_down", "mix_norm", "w_in", "pool_w", "pool_scale",
             "dn_conv_w", "dn_a_log", "dn_dt_bias", "dn_out_norm", "w_out", "ffn2_norm", "ffn2_w_gate", "ffn2_w_up",
             "ffn2_w_down", "final_norm")
    res = {**small_res, **big_res}
    outs = [loss, grad_x]
    for k in range(4):
        outs.extend(res[n][k] for n in order)
    return tuple(outs)
```

```python
import functools
import math

import jax
import jax.numpy as jnp
from jax import lax
from jax.experimental import pallas as pl
from jax.experimental.pallas import tpu as pltpu

F32 = jnp.float32
BF16 = jnp.bfloat16
HI = lax.Precision.HIGHEST
INV_PREC = lax.Precision.HIGH
SDS = jax.ShapeDtypeStruct

N_DEV = 8
SEQ = 4096
D_MODEL = 1024
DEPTH = 2
D_FF = 2816
FF_BLK = D_FF // N_DEV
ATT_W = 256
ATT_E = 64
ATT_BLK = 128
DILATIONS = (1, 4, 16)
POOL_W = 256
POOL_HALO = 16
DN_W = 512
DN_H = 4
DN_E = 128
DN_C = 64
N_CHUNK = SEQ // DN_C
IN_W = 3080
IN_BLK = IN_W // N_DEV
EPS = 1e-6
EXT_ATT = 1280
GATE_W = 256
EXT_REST = 4 * DN_W + GATE_W + POOL_W
EXT_W = EXT_ATT + EXT_REST
R_DQKV, R_DZ, R_G, R_PU = 0, 1536, 2048, 2304

ADAM_LR, ADAM_B1, ADAM_B2, ADAM_EPS, ADAM_WD, ADAM_STEP = 0.001, 0.9, 0.999, 1e-08, 0.01, 10

VMEM_LIMIT = 60 * 1024 * 1024
MESH = pl.DeviceIdType.MESH


def _cparams(sem=None):
    kw = dict(vmem_limit_bytes=VMEM_LIMIT)
    if sem is not None:
        kw["dimension_semantics"] = sem
    return pltpu.CompilerParams(**kw)


def _dot(a, b, prec=None):
    return jnp.dot(a, b, preferred_element_type=F32, precision=prec)


def _dot_nt(a, b, prec=None):
    return lax.dot_general(a, b, (((1,), (1,)), ((), ())), preferred_element_type=F32, precision=prec)


def _dot_tn(a, b, prec=None):
    return lax.dot_general(a, b, (((0,), (0,)), ((), ())), preferred_element_type=F32, precision=prec)


def _sigmoid(x):
    return jax.nn.sigmoid(x)


def _rms_stats(x):
    r = lax.rsqrt(jnp.mean(x * x, axis=-1, keepdims=True) + EPS)
    return x * r, r


def _rms_bwd(xh, r, w, dh):
    dxh = dh * w
    dx = r * (dxh - xh * jnp.mean(dxh * xh, axis=-1, keepdims=True))
    return dx, jnp.sum(dh * xh, axis=0, keepdims=True)


FFN_T_FWD = 2048
FFN_T_BWD = 512
FF_TILE = 256
N_FF_TILE = D_FF // FF_TILE


def ffn_shard_operands(gate, up, down):
    return [gate.T.astype(BF16), up.T.astype(BF16), down.astype(BF16)]


def ffn_fwd(x, nw, wgt, wut, wd, name, exch=None):
    t = FFN_T_FWD

    def body(x_ref, nw_ref, wgt_ref, wut_ref, wd_ref, o_ref, h_scr, acc_scr):
        k = pl.program_id(1)

        @pl.when(k == 0)
        def _():
            xh, _r = _rms_stats(x_ref[...])
            h_scr[...] = (xh * nw_ref[...]).astype(BF16)
            acc_scr[...] = jnp.zeros_like(acc_scr)

        h = h_scr[...]
        hg = _dot_nt(h, wgt_ref[...])
        hu = _dot_nt(h, wut_ref[...])
        a = (hg * _sigmoid(hg) * hu).astype(BF16)
        acc_scr[...] += _dot(a, wd_ref[...])

        @pl.when(k == N_FF_TILE - 1)
        def _():
            o_ref[...] = x_ref[...] + 0.5 * acc_scr[...]

    w_spec = pl.BlockSpec((FF_TILE, D_MODEL), lambda i, k: (k, 0))
    return _call(
        body, name=name, grid=(SEQ // t, N_FF_TILE),
        in_specs=[pl.BlockSpec((t, D_MODEL), lambda i, k: (i, 0)),
                  pl.BlockSpec((1, D_MODEL), lambda i, k: (0, 0)), w_spec, w_spec, w_spec],
        out_specs=pl.BlockSpec((t, D_MODEL), lambda i, k: (i, 0)),
        out_shape=SDS((SEQ, D_MODEL), F32),
        scratch_shapes=[pltpu.VMEM((t, D_MODEL), BF16), pltpu.VMEM((t, D_MODEL), F32)],
        sem=("arbitrary", "arbitrary"), args=(x, nw, wgt, wut, wd), exch=exch)


def ffn_bwd(x, dxo, nw, wgt, wut, wd, name, exch=None):
    t = FFN_T_BWD
    nt = SEQ // t

    def body(x_ref, dxo_ref, nw_ref, wgt_ref, wut_ref, wd_ref, dx_ref, dwgt_ref, dwut_ref, dwd_ref, dnw_ref,
             dh_scr, ag_scr, au_scr, ad_scr, h_scr):
        k = pl.program_id(0)
        i = pl.program_id(1)
        rows = pl.ds(pl.multiple_of(i * t, t), t)
        nw_v = nw_ref[...]

        @pl.when(k == 0)
        def _():
            xh0, _r0 = _rms_stats(x_ref[...])
            h_scr[rows, :] = (xh0 * nw_v).astype(BF16)

        h = h_scr[rows, :]
        dy = (0.5 * dxo_ref[...]).astype(BF16)
        wgt = wgt_ref[...]
        wut = wut_ref[...]
        hg = _dot_nt(h, wgt)
        hu = _dot_nt(h, wut)
        sg = _sigmoid(hg)
        sil = hg * sg
        a = (sil * hu).astype(BF16)
        da = _dot_nt(dy, wd_ref[...])
        dhu = (da * sil).astype(BF16)
        dhg = (da * hu * (sg * (1.0 + hg * (1.0 - sg)))).astype(BF16)
        p_d = _dot_tn(a, dy)
        p_g = _dot_tn(dhg, h)
        p_u = _dot_tn(dhu, h)
        dh = _dot(dhg, wgt) + _dot(dhu, wut)

        @pl.when(i == 0)
        def _():
            ad_scr[...] = p_d
            ag_scr[...] = p_g
            au_scr[...] = p_u

        @pl.when(i > 0)
        def _():
            ad_scr[...] += p_d
            ag_scr[...] += p_g
            au_scr[...] += p_u

        @pl.when(i == nt - 1)
        def _():
            dwd_ref[...] = ad_scr[...].astype(BF16)
            dwgt_ref[...] = ag_scr[...].astype(BF16)
            dwut_ref[...] = au_scr[...].astype(BF16)

        @pl.when(k == 0)
        def _():
            dh_scr[rows, :] = dh

        @pl.when(k > 0)
        def _():
            dh_scr[rows, :] += dh

        @pl.when(jnp.logical_and(k == 0, i == 0))
        def _():
            dnw_ref[...] = jnp.zeros_like(dnw_ref)

        @pl.when(k == N_FF_TILE - 1)
        def _():
            xh, r = _rms_stats(x_ref[...])
            dx, dw = _rms_bwd(xh, r, nw_v, dh_scr[rows, :])
            dx_ref[...] = dxo_ref[...] + dx
            dnw_ref[...] += dw

    last = N_FF_TILE - 1
    w_spec = pl.BlockSpec((FF_TILE, D_MODEL), lambda k, i: (k, 0))
    return _call(
        body, name=name, grid=(N_FF_TILE, nt),
        in_specs=[pl.BlockSpec((t, D_MODEL), lambda k, i: (i, 0)),
                  pl.BlockSpec((t, D_MODEL), lambda k, i: (i, 0)),
                  pl.BlockSpec((1, D_MODEL), lambda k, i: (0, 0)), w_spec, w_spec, w_spec],
        out_specs=[pl.BlockSpec((t, D_MODEL), lambda k, i: (jnp.where(k == last, i, 0), 0)),
                   w_spec, w_spec, w_spec, pl.BlockSpec((1, D_MODEL), lambda k, i: (0, 0))],
        out_shape=[SDS((SEQ, D_MODEL), F32), SDS((D_FF, D_MODEL), BF16), SDS((D_FF, D_MODEL), BF16),
                   SDS((D_FF, D_MODEL), BF16), SDS((1, D_MODEL), F32)],
        scratch_shapes=[pltpu.VMEM((SEQ, D_MODEL), F32), pltpu.VMEM((FF_TILE, D_MODEL), F32),
                        pltpu.VMEM((FF_TILE, D_MODEL), F32), pltpu.VMEM((FF_TILE, D_MODEL), F32),
                        pltpu.VMEM((SEQ, D_MODEL), BF16)],
        sem=("arbitrary", "arbitrary"), args=(x, dxo, nw, wgt, wut, wd), exch=exch)


def loss_head(x, fw, target, name):
    t = 512

    def body(x_ref, fw_ref, tg_ref, loss_ref, dx_ref, dfw_ref):
        i = pl.program_id(0)
        xh, r = _rms_stats(x_ref[...])
        w = fw_ref[...]
        err = xh * w - tg_ref[...]
        part = 0.5 * jnp.sum(jnp.sum(err * err, axis=-1, keepdims=True), axis=0, keepdims=True) / D_MODEL
        dx, dw = _rms_bwd(xh, r, w, err * (1.0 / D_MODEL))
        dx_ref[...] = dx

        @pl.when(i == 0)
        def _():
            loss_ref[...] = jnp.zeros_like(loss_ref)
            dfw_ref[...] = jnp.zeros_like(dfw_ref)

        loss_ref[...] += jnp.broadcast_to(part, loss_ref.shape)
        dfw_ref[...] += dw

    return pl.pallas_call(
        body, name=name, grid=(SEQ // t,),
        in_specs=[pl.BlockSpec((t, D_MODEL), lambda i: (i, 0)),
                  pl.BlockSpec((1, D_MODEL), lambda i: (0, 0)),
                  pl.BlockSpec((t, D_MODEL), lambda i: (i, 0))],
        out_specs=[pl.BlockSpec((1, 128), lambda i: (0, 0)),
                   pl.BlockSpec((t, D_MODEL), lambda i: (i, 0)),
                   pl.BlockSpec((1, D_MODEL), lambda i: (0, 0))],
        out_shape=[SDS((1, 128), F32), SDS((SEQ, D_MODEL), F32), SDS((1, D_MODEL), F32)],
        compiler_params=_cparams(("arbitrary",)),
    )(x, fw, target)


MIX_T = 256


def _slabs_load(ref, first, n):
    return jnp.concatenate([ref[first + j] for j in range(n)], axis=1)


def _slabs_store(ref, first, val):
    for j in range(val.shape[1] // 128):
        ref[first + j] = val[:, 128 * j:128 * j + 128]


def _slab_spec(k, t):
    return pl.BlockSpec((k, t, 128), lambda i: (0, i, 0))


def mix_in_fwd(x, nw, wext, cos, sin, name):
    t = MIX_T

    def body(x_ref, nw_ref, w_ref, cos_ref, sin_ref, att_ref, rest_ref):
        xh, _r = _rms_stats(x_ref[...])
        h = (xh * nw_ref[...]).astype(BF16)
        pa = _dot(h, w_ref[:, 0:EXT_ATT])
        c = cos_ref[...]
        s = sin_ref[...]
        _slabs_store(att_ref, 0, pa[:, 0:256] * c + pa[:, 768:1024] * s)
        _slabs_store(att_ref, 2, pa[:, 256:512] * c + pa[:, 1024:1280] * s)
        _slabs_store(att_ref, 4, pa[:, 512:768])
        for j in range(EXT_REST // 256):
            rest_ref[:, 256 * j:256 * j + 256] = _dot(h, w_ref[:, EXT_ATT + 256 * j:EXT_ATT + 256 * j + 256])

    return pl.pallas_call(
        body, name=name, grid=(SEQ // t,),
        in_specs=[pl.BlockSpec((t, D_MODEL), lambda i: (i, 0)),
                  pl.BlockSpec((1, D_MODEL), lambda i: (0, 0)),
                  pl.BlockSpec((D_MODEL, EXT_W), lambda i: (0, 0)),
                  pl.BlockSpec((t, ATT_W), lambda i: (i, 0)),
                  pl.BlockSpec((t, ATT_W), lambda i: (i, 0))],
        out_specs=[_slab_spec(6, t),
                   pl.BlockSpec((t, EXT_REST), lambda i: (i, 0))],
        out_shape=[SDS((6, SEQ, 128), F32), SDS((SEQ, EXT_REST), F32)],
        compiler_params=_cparams(("arbitrary",)),
    )(x, nw, wext, cos, sin)


def assemble_dproj(datts, cos, sin, d_dqkv, dz, dg, dpu, name):
    t = 512

    def body(d1_ref, d4_ref, d16_ref, cos_ref, sin_ref, dqkv_ref, dz_ref, dg_ref, dpu_ref, o_ref):
        da6 = d1_ref[...] + d4_ref[...] + d16_ref[...]
        da = jnp.concatenate([da6[j] for j in range(6)], axis=1)
        c = cos_ref[...]
        s = sin_ref[...]
        dq = da[:, 0:256]
        dk = da[:, 256:512]
        o_ref[:, 0:256] = (dq * c).astype(BF16)
        o_ref[:, 256:512] = (dk * c).astype(BF16)
        o_ref[:, 512:768] = da[:, 512:768].astype(BF16)
        o_ref[:, 768:1024] = (dq * s).astype(BF16)
        o_ref[:, 1024:1280] = (dk * s).astype(BF16)
        b = EXT_ATT
        o_ref[:, b + R_DQKV:b + R_DQKV + 1536] = dqkv_ref[...].astype(BF16)
        o_ref[:, b + R_DZ:b + R_DZ + 512] = dz_ref[...].astype(BF16)
        o_ref[:, b + R_G:b + R_G + GATE_W] = dg_ref[...].astype(BF16)
        o_ref[:, b + R_PU:b + R_PU + 256] = dpu_ref[...].astype(BF16)

    row = lambda w: pl.BlockSpec((t, w), lambda i: (i, 0))
    return pl.pallas_call(
        body, name=name, grid=(SEQ // t,),
        in_specs=[_slab_spec(6, t), _slab_spec(6, t), _slab_spec(6, t),
                  row(256), row(256), row(1536), row(512), row(GATE_W), row(256)],
        out_specs=row(EXT_W),
        out_shape=SDS((SEQ, EXT_W), BF16),
        compiler_params=_cparams(("arbitrary",)),
    )(*datts, cos, sin, d_dqkv, dz, dg, dpu)


def linear_bwd(x, dxo, nw, dy, w, name):
    t = 512
    nb = 768
    n = w.shape[1]
    nt = SEQ // t
    nn = n // nb

    def body(x_ref, dxo_ref, nw_ref, dy_ref, w_ref, dx_ref, dw_ref, dnw_ref, dh_scr, h_scr):
        k = pl.program_id(0)
        i = pl.program_id(1)
        rows = pl.ds(pl.multiple_of(i * t, t), t)
        nw_v = nw_ref[...]

        @pl.when(k == 0)
        def _():
            xh0, _r0 = _rms_stats(x_ref[...])
            h_scr[rows, :] = (xh0 * nw_v).astype(BF16)

        h = h_scr[rows, :]
        dyv = dy_ref[...]
        p_w = _dot_tn(h, dyv)
        dh = _dot_nt(dyv, w_ref[...])

        @pl.when(i == 0)
        def _():
            dw_ref[...] = p_w

        @pl.when(i > 0)
        def _():
            dw_ref[...] += p_w

        @pl.when(k == 0)
        def _():
            dh_scr[rows, :] = dh

        @pl.when(k > 0)
        def _():
            dh_scr[rows, :] += dh

        @pl.when(jnp.logical_and(k == 0, i == 0))
        def _():
            dnw_ref[...] = jnp.zeros_like(dnw_ref)

        @pl.when(k == nn - 1)
        def _():
            xh, r = _rms_stats(x_ref[...])
            dx, dw = _rms_bwd(xh, r, nw_v, dh_scr[rows, :])
            dx_ref[...] = dxo_ref[...] + dx
            dnw_ref[...] += dw

    last = nn - 1
    return pl.pallas_call(
        body, name=name, grid=(nn, nt),
        in_specs=[pl.BlockSpec((t, D_MODEL), lambda k, i: (i, 0)),
                  pl.BlockSpec((t, D_MODEL), lambda k, i: (i, 0)),
                  pl.BlockSpec((1, D_MODEL), lambda k, i: (0, 0)),
                  pl.BlockSpec((t, nb), lambda k, i: (i, k)),
                  pl.BlockSpec((D_MODEL, nb), lambda k, i: (0, k))],
        out_specs=[pl.BlockSpec((t, D_MODEL), lambda k, i: (jnp.where(k == last, i, 0), 0)),
                   pl.BlockSpec((D_MODEL, nb), lambda k, i: (0, k)),
                   pl.BlockSpec((1, D_MODEL), lambda k, i: (0, 0))],
        out_shape=[SDS((SEQ, D_MODEL), F32), SDS((D_MODEL, n), F32), SDS((1, D_MODEL), F32)],
        scratch_shapes=[pltpu.VMEM((SEQ, D_MODEL), F32), pltpu.VMEM((SEQ, D_MODEL), BF16)],
        compiler_params=_cparams(("arbitrary", "arbitrary")),
    )(x, dxo, nw, dy, w)


def _att_masks():
    qi = lax.broadcasted_iota(jnp.int32, (ATT_BLK, ATT_BLK), 0)
    ki = lax.broadcasted_iota(jnp.int32, (ATT_BLK, ATT_BLK), 1)
    return ki <= qi, ki >= qi


NEG = -1e30


N_ATT_BLK = SEQ // ATT_BLK


def _class_rows(i, d):
    per_class = N_ATT_BLK // d
    shift = per_class.bit_length() - 1
    r = i >> shift
    j = i & (per_class - 1)
    span = ATT_BLK * d
    start = r + span * j
    prev = jnp.where(j == 0, start, start - span)
    nxt = jnp.where(j == per_class - 1, start, start + span)

    def rows(s0):
        if d == 1:
            return pl.ds(pl.multiple_of(s0, ATT_BLK), ATT_BLK)
        return pl.ds(s0, ATT_BLK, stride=d)

    return rows(start), rows(prev), rows(nxt), j != 0, j != per_class - 1


def _slab_heads(ref, slab, rows):
    x0 = ref[pl.ds(slab, 1), rows, :][0]
    x1 = ref[pl.ds(slab + 1, 1), rows, :][0]
    return jnp.stack([x0[:, 0:ATT_E], x0[:, ATT_E:], x1[:, 0:ATT_E], x1[:, ATT_E:]], axis=0)


def _put_slab_heads(ref, slab, rows, val):
    ref[pl.ds(slab, 1), rows, :] = jnp.concatenate([val[0], val[1]], axis=1)[None]
    ref[pl.ds(slab + 1, 1), rows, :] = jnp.concatenate([val[2], val[3]], axis=1)[None]


ATT_BLOCKS_PER_STEP = 2


def _resident_call(body, ins, out_slabs, name):
    n_in = len(ins)
    steps = N_ATT_BLK // ATT_BLOCKS_PER_STEP

    def wrapped(*refs):
        hbm_in, hbm_out = refs[:n_in], refs[n_in]
        vm_in, vm_out, sem = refs[n_in + 1:2 * n_in + 1], refs[2 * n_in + 1], refs[2 * n_in + 2]
        i = pl.program_id(0)

        @pl.when(i == 0)
        def _():
            copies = [pltpu.make_async_copy(h, v, sem.at[k]) for k, (h, v) in enumerate(zip(hbm_in, vm_in))]
            for cp in copies:
                cp.start()
            for cp in copies:
                cp.wait()

        for b in range(ATT_BLOCKS_PER_STEP):
            body(ATT_BLOCKS_PER_STEP * i + b, *vm_in, vm_out)

        @pl.when(i == steps - 1)
        def _():
            cp = pltpu.make_async_copy(vm_out, hbm_out, sem.at[n_in])
            cp.start()
            cp.wait()

    return pl.pallas_call(
        wrapped, name=name, grid=(steps,),
        in_specs=[ANY_SPEC] * n_in, out_specs=ANY_SPEC, out_shape=SDS((out_slabs, SEQ, 128), F32),
        scratch_shapes=[pltpu.VMEM(a.shape, a.dtype) for a in ins] + [pltpu.VMEM((out_slabs, SEQ, 128), F32),
                                                                      pltpu.SemaphoreType.DMA((n_in + 1,))],
        compiler_params=_cparams(("arbitrary",)),
    )(*ins)


def _att_fwd_math(ld, has_prev):
    m_d, m_p = _att_masks()
    m_p = jnp.logical_and(m_p, has_prev)
    q = ld("att", 0, "cur").astype(BF16)
    kc = ld("att", 2, "cur").astype(BF16)
    vc = ld("att", 4, "cur").astype(BF16)
    kp = ld("att", 2, "prev").astype(BF16)
    vp = ld("att", 4, "prev").astype(BF16)
    sd = jnp.where(m_d, _bdot(q, kc, 2, 2) * 0.125, NEG)
    sp = jnp.where(m_p, _bdot(q, kp, 2, 2) * 0.125, NEG)
    m = jnp.maximum(jnp.max(sd, axis=-1, keepdims=True), jnp.max(sp, axis=-1, keepdims=True))
    pd = jnp.exp(sd - m)
    pp = jnp.exp(sp - m)
    den = jnp.sum(pd, axis=-1, keepdims=True) + jnp.sum(pp, axis=-1, keepdims=True)
    inv = 1.0 / den
    o = _bdot((pd * inv).astype(BF16), vc, 2, 1) + _bdot((pp * inv).astype(BF16), vp, 2, 1)
    return o, jnp.broadcast_to(m + jnp.log(den), (4, ATT_BLK, ATT_E))


def _att_bwd_math(ld, has_prev, has_next):
    m_d, m_band = _att_masks()
    m_p = jnp.logical_and(m_band, has_prev)
    m_n = jnp.logical_and(m_band, has_next)

    def pair(q, k, v, lse, do, dterm, mask):
        s = jnp.where(mask, _bdot(q, k, 2, 2) * 0.125, NEG)
        p = jnp.exp(s - lse)
        dp = _bdot(do, v, 2, 2)
        ds = (p * (dp + dterm) * 0.125).astype(BF16)
        return p.astype(BF16), ds

    q_c = ld("att", 0, "cur").astype(BF16)
    k_c = ld("att", 2, "cur").astype(BF16)
    v_c = ld("att", 4, "cur").astype(BF16)
    k_p = ld("att", 2, "prev").astype(BF16)
    v_p = ld("att", 4, "prev").astype(BF16)
    q_n = ld("att", 0, "next").astype(BF16)
    o_c = ld("ol", 0, "cur")
    o_n = ld("ol", 0, "next")
    lse_c = ld("ol", 2, "cur")[:, :, 0:1]
    lse_n = ld("ol", 2, "next")[:, :, 0:1]
    do_c = ld("dol", 0, "cur")
    do_n = ld("dol", 0, "next")
    t_c = ld("dol", 2, "cur")[:, :, 0:1] - jnp.sum(do_c * o_c, axis=-1, keepdims=True)
    t_n = ld("dol", 2, "next")[:, :, 0:1] - jnp.sum(do_n * o_n, axis=-1, keepdims=True)
    do_cb = do_c.astype(BF16)
    do_nb = do_n.astype(BF16)
    p1, ds1 = pair(q_c, k_c, v_c, lse_c, do_cb, t_c, m_d)
    _p2, ds2 = pair(q_c, k_p, v_p, lse_c, do_cb, t_c, m_p)
    p3, ds3 = pair(q_n, k_c, v_c, lse_n, do_nb, t_n, m_n)
    return (_bdot(ds1, k_c, 2, 1) + _bdot(ds2, k_p, 2, 1), _bdot(ds1, q_c, 1, 1) + _bdot(ds3, q_n, 1, 1),
            _bdot(p1, do_cb, 1, 1) + _bdot(p3, do_nb, 1, 1))


ROWS_A = pl.ds(0, ATT_BLK)
ROWS_B = pl.ds(ATT_BLK, ATT_BLK)
N_ATT_PAIR = N_ATT_BLK // 2


def _pair_spec(k):
    return pl.BlockSpec((k, 2 * ATT_BLK, 128), lambda i: (0, i, 0))


def _before_pair_spec(k):
    return pl.BlockSpec((k, ATT_BLK, 128), lambda i: (0, jnp.maximum(2 * i - 1, 0), 0))


def _after_pair_spec(k):
    return pl.BlockSpec((k, ATT_BLK, 128), lambda i: (0, jnp.minimum(2 * i + 2, N_ATT_BLK - 1), 0))


def att_fwd_s(att, d, name):
    if d == 1:
        def body1(cur_ref, prev_ref, o_ref):
            i = pl.program_id(0)
            for rows, views, has_prev in (
                    (ROWS_A, {"cur": (cur_ref, ROWS_A), "prev": (prev_ref, ROWS_A)}, i != 0),
                    (ROWS_B, {"cur": (cur_ref, ROWS_B), "prev": (cur_ref, ROWS_A)}, True)):
                o, lse = _att_fwd_math(lambda _a, slab, where, v=views: _slab_heads(v[where][0], slab, v[where][1]), has_prev)
                _put_slab_heads(o_ref, 0, rows, o)
                _put_slab_heads(o_ref, 2, rows, lse)

        return pl.pallas_call(
            body1, name=name, grid=(N_ATT_PAIR,),
            in_specs=[_pair_spec(6), _before_pair_spec(6)], out_specs=_pair_spec(4),
            out_shape=SDS((4, SEQ, 128), F32), compiler_params=_cparams(("arbitrary",)),
        )(att, att)

    def body(i, att_ref, o_ref):
        cur, prev, _nxt, has_prev, _has_next = _class_rows(i, d)
        rows = {"cur": cur, "prev": prev}
        o, lse = _att_fwd_math(lambda _a, slab, where: _slab_heads(att_ref, slab, rows[where]), has_prev)
        _put_slab_heads(o_ref, 0, cur, o)
        _put_slab_heads(o_ref, 2, cur, lse)

    return _resident_call(body, [att], 4, name)


def att_bwd_s(att, ol, dol, d, name):
    if d == 1:
        def body1(a_p, a_c, a_n, ol_c, ol_n, dol_c, dol_n, d_ref):
            i = pl.program_id(0)
            first = {("att", "prev"): (a_p, ROWS_A), ("att", "cur"): (a_c, ROWS_A), ("att", "next"): (a_c, ROWS_B),
                     ("ol", "cur"): (ol_c, ROWS_A), ("ol", "next"): (ol_c, ROWS_B),
                     ("dol", "cur"): (dol_c, ROWS_A), ("dol", "next"): (dol_c, ROWS_B)}
            second = {("att", "prev"): (a_c, ROWS_A), ("att", "cur"): (a_c, ROWS_B), ("att", "next"): (a_n, ROWS_A),
                      ("ol", "cur"): (ol_c, ROWS_B), ("ol", "next"): (ol_n, ROWS_A),
                      ("dol", "cur"): (dol_c, ROWS_B), ("dol", "next"): (dol_n, ROWS_A)}
            for rows, views, has_prev, has_next in ((ROWS_A, first, i != 0, True),
                                                    (ROWS_B, second, True, i != N_ATT_PAIR - 1)):
                dq, dk, dv = _att_bwd_math(
                    lambda a, slab, where, v=views: _slab_heads(v[(a, where)][0], slab, v[(a, where)][1]), has_prev, has_next)
                _put_slab_heads(d_ref, 0, rows, dq)
                _put_slab_heads(d_ref, 2, rows, dk)
                _put_slab_heads(d_ref, 4, rows, dv)

        return pl.pallas_call(
            body1, name=name, grid=(N_ATT_PAIR,),
            in_specs=[_before_pair_spec(6), _pair_spec(6), _after_pair_spec(6), _pair_spec(4), _after_pair_spec(4),
                      _pair_spec(4), _after_pair_spec(4)],
            out_specs=_pair_spec(6), out_shape=SDS((6, SEQ, 128), F32), compiler_params=_cparams(("arbitrary",)),
        )(att, att, att, ol, ol, dol, dol)

    def body(i, att_ref, ol_ref, dol_ref, d_ref):
        cur, prev, nxt, has_prev, has_next = _class_rows(i, d)
        rows = {"cur": cur, "prev": prev, "next": nxt}
        refs = {"att": att_ref, "ol": ol_ref, "dol": dol_ref}
        dq, dk, dv = _att_bwd_math(lambda a, slab, where: _slab_heads(refs[a], slab, rows[where]), has_prev, has_next)
        _put_slab_heads(d_ref, 0, cur, dq)
        _put_slab_heads(d_ref, 2, cur, dk)
        _put_slab_heads(d_ref, 4, cur, dv)

    return _resident_call(body, [att, ol, dol], 6, name)


def _shift_down(x, k):
    rows = lax.broadcasted_iota(jnp.int32, x.shape, 0)
    return jnp.where(rows >= k, pltpu.roll(x, k, 0), 0.0)


def _shift_up(x, k):
    n = x.shape[0]
    rows = lax.broadcasted_iota(jnp.int32, x.shape, 0)
    return jnp.where(rows < n - k, pltpu.roll(x, n - k, 0), 0.0)


@functools.partial(jax.custom_vjp, nondiff_argnums=(1,))
def _delay(x, k):
    return _shift_down(x, k)


def _delay_fwd(x, k):
    return _shift_down(x, k), None


def _delay_bwd(k, _res, g):
    return (_shift_up(g, k),)


_delay.defvjp(_delay_fwd, _delay_bwd)

DN_CONV = 4


def _dn_prep_fn(u, w, kind):
    y = w[DN_CONV - 1:DN_CONV] * u
    for j in range(DN_CONV - 1):
        y = y + w[j:j + 1] * _delay(u, DN_CONV - 1 - j)
    y = y * _sigmoid(y)
    nrm = y * lax.rsqrt(jnp.sum(y * y, axis=-1, keepdims=True) + EPS)
    return jnp.where(kind == 0, nrm * (DN_E ** -0.5), jnp.where(kind == 1, nrm, y))


def dn_prep_fwd(rest, conv_w, name):
    def body(u_ref, w_ref, o_ref):
        j = pl.program_id(0)
        kind = (j >= DN_H).astype(jnp.int32) + (j >= 2 * DN_H).astype(jnp.int32)
        o_ref[...] = _dn_prep_fn(u_ref[...], w_ref[...], kind)

    return pl.pallas_call(
        body, name=name, grid=(3 * DN_H,),
        in_specs=[pl.BlockSpec((SEQ, DN_E), lambda j: (0, j)),
                  pl.BlockSpec((DN_CONV, DN_E), lambda j: (0, j))],
        out_specs=pl.BlockSpec((SEQ, DN_E), lambda j: (0, j)),
        out_shape=SDS((SEQ, 3 * DN_W), F32),
        compiler_params=_cparams(("arbitrary",)),
    )(rest, conv_w)


def dn_prep_bwd(rest, conv_w, dqkv, name):
    def body(u_ref, w_ref, g_ref, du_ref, dw_ref):
        j = pl.program_id(0)
        kind = (j >= DN_H).astype(jnp.int32) + (j >= 2 * DN_H).astype(jnp.int32)
        _y, vjp = jax.vjp(lambda u, w: _dn_prep_fn(u, w, kind), u_ref[...], w_ref[...])
        du, dw = vjp(g_ref[...])
        du_ref[...] = du
        dw_ref[...] = dw

    return pl.pallas_call(
        body, name=name, grid=(3 * DN_H,),
        in_specs=[pl.BlockSpec((SEQ, DN_E), lambda j: (0, j)),
                  pl.BlockSpec((DN_CONV, DN_E), lambda j: (0, j)),
                  pl.BlockSpec((SEQ, DN_E), lambda j: (0, j))],
        out_specs=[pl.BlockSpec((SEQ, DN_E), lambda j: (0, j)),
                   pl.BlockSpec((DN_CONV, DN_E), lambda j: (0, j))],
        out_shape=[SDS((SEQ, 3 * DN_W), F32), SDS((DN_CONV, 3 * DN_W), F32)],
        compiler_params=_cparams(("arbitrary",)),
    )(rest, conv_w, dqkv)


def _bdot(a, b, ca, cb, prec=None):
    return lax.dot_general(a, b, (((ca,), (cb,)), ((0,), (0,))), preferred_element_type=F32, precision=prec)


def _unit_lower_inverse(a):
    eye = (lax.broadcasted_iota(jnp.int32, (DN_C, DN_C), 0) == lax.broadcasted_iota(jnp.int32, (DN_C, DN_C), 1)).astype(F32)
    p = eye - a
    b = _bdot(a, a, 2, 1, INV_PREC)
    for lvl in range(5):
        p = p + _bdot(p, b, 2, 1, INV_PREC)
        if lvl < 4:
            b = _bdot(b, b, 2, 1, INV_PREC)
    return p


@jax.custom_vjp
def _tri_inv(a):
    return _unit_lower_inverse(a)


def _tri_inv_fwd(a):
    t = _unit_lower_inverse(a)
    return t, t


def _tri_inv_bwd(t, g):
    return (-_bdot(_bdot(t, g, 1, 1, INV_PREC), t, 2, 2, INV_PREC),)


_tri_inv.defvjp(_tri_inv_fwd, _tri_inv_bwd)


def _b16(x):
    return x.astype(BF16)


def _heads(ref, base=0):
    return jnp.stack([ref[:, base + DN_E * hd:base + DN_E * hd + DN_E] for hd in range(DN_H)], axis=0)


def _put_heads(ref, val, base=0):
    for hd in range(DN_H):
        ref[:, base + DN_E * hd:base + DN_E * hd + DN_E] = val[hd]


DN_G_LOG2 = 3
DN_G = 1 << DN_G_LOG2
N_INST = DN_G * DN_H


def _dn_intra(q, k, v, bb, ab, alog, dtb):
    ri = lax.broadcasted_iota(jnp.int32, (DN_C, DN_C), 0)
    ci = lax.broadcasted_iota(jnp.int32, (DN_C, DN_C), 1)
    lower = ri >= ci
    strict = ri > ci
    nh = q.shape[0]
    beta = _sigmoid(bb)
    xg = ab + dtb
    softplus = jnp.maximum(xg, 0.0) + jnp.log(1.0 + jnp.exp(-jnp.abs(xg)))
    gi = -jnp.exp(alog) * softplus
    g = _bdot(jnp.broadcast_to(lower.astype(F32), (nh, DN_C, DN_C)), gi, 2, 1, HI)
    eg = jnp.exp(g)
    kb = k * beta
    vb = v * beta
    g_col = g[:, :, 0:DN_C]
    g_row = _bdot(jnp.full((nh, DN_C, DN_E), 1.0 / DN_E, F32), g, 2, 2, HI)
    decay = jnp.where(lower, jnp.exp(jnp.where(lower, g_col - g_row, 0.0)), 0.0)
    kbf = _b16(k)
    a = jnp.where(strict, _bdot(_b16(kb), kbf, 2, 2) * decay, 0.0)
    tb = _b16(_tri_inv(a))
    u = _bdot(tb, _b16(vb), 2, 1)
    w = _bdot(tb, _b16(kb * eg), 2, 1)
    intra = jnp.where(lower, _bdot(_b16(q), kbf, 2, 2) * decay, 0.0)
    g_last = g[:, DN_C - 1:DN_C, :]
    return u, w, q * eg, k * jnp.exp(g_last - g), intra, jnp.exp(g_last)


def _dn_inter(u, w, qg, kdec, intra, egl, state):
    sb = _b16(state)
    v_new = u - _bdot(_b16(w), sb, 2, 1)
    o = _bdot(_b16(qg), sb, 2, 1) + _bdot(_b16(intra), _b16(v_new), 2, 1)
    return o, state * egl + _bdot(_b16(kdec), _b16(v_new), 1, 1)


def _inst(ref, base=0):
    per_head = [ref[:, base + DN_E * hd:base + DN_E * hd + DN_E].reshape(DN_G, DN_C, DN_E) for hd in range(DN_H)]
    return jnp.concatenate(per_head, axis=0)


def _inst_rows(ref):
    rows = [jnp.broadcast_to(ref[:, DN_E * hd:DN_E * hd + DN_E][None], (DN_G, 1, DN_E)) for hd in range(DN_H)]
    return jnp.concatenate(rows, axis=0)


def _put_inst(ref, val, width=DN_E, base=0):
    for hd in range(DN_H):
        ref[:, base + width * hd:base + width * hd + width] = val[DN_G * hd:DN_G * hd + DN_G].reshape(DN_G * DN_C, width)


@jax.custom_vjp
def _spread_gates(gates):
    t = gates.shape[0]
    return jnp.concatenate([jnp.broadcast_to(gates[:, j:j + 1], (t, DN_E)) for j in range(2 * DN_H)], axis=1)


def _spread_gates_fwd(gates):
    return _spread_gates(gates), None


def _spread_gates_bwd(_res, g):
    t = g.shape[0]
    lane = lax.broadcasted_iota(jnp.int32, (t, GATE_W), 1)
    out = jnp.zeros((t, GATE_W), F32)
    for j in range(2 * DN_H):
        s = jnp.sum(g[:, DN_E * j:DN_E * j + DN_E], axis=-1, keepdims=True)
        out = jnp.where(lane == j, s, out)
    return (out,)


_spread_gates.defvjp(_spread_gates_fwd, _spread_gates_bwd)


def _dn_intra_from_gates(q, k, v, gates, alog, dtb):
    wide = _spread_gates(gates)
    inst = lambda base: jnp.concatenate(
        [wide[:, base + DN_E * hd:base + DN_E * hd + DN_E].reshape(DN_G, DN_C, DN_E) for hd in range(DN_H)], axis=0)
    return _dn_intra(q, k, v, inst(0), inst(DN_W), alog, dtb)


def _intra_args(qkv_ref, g_ref, alog_ref, dtb_ref):
    return (_inst(qkv_ref), _inst(qkv_ref, DN_W), _inst(qkv_ref, 2 * DN_W), g_ref[...],
            _inst_rows(alog_ref), _inst_rows(dtb_ref))


def _intra_in_specs():
    t = DN_G * DN_C
    return [pl.BlockSpec((t, 3 * DN_W), lambda n: (n, 0)),
            pl.BlockSpec((t, GATE_W), lambda n: (n, R_G // GATE_W)),
            pl.BlockSpec((1, DN_W), lambda n: (0, 0)),
            pl.BlockSpec((1, DN_W), lambda n: (0, 0))]


def dn_intra_fwd(qkv, rest, alog_b, dtb_b, name, exch=None):
    t = DN_G * DN_C

    def body(qkv_ref, g_ref, alog_ref, dtb_ref, u_ref, w_ref, qg_ref, kd_ref, in_ref, egl_ref):
        u, w, qg, kdec, intra, egl = _dn_intra_from_gates(*_intra_args(qkv_ref, g_ref, alog_ref, dtb_ref))
        _put_inst(u_ref, u)
        _put_inst(w_ref, w.astype(BF16))
        _put_inst(qg_ref, qg.astype(BF16))
        _put_inst(kd_ref, kdec.astype(BF16))
        _put_inst(in_ref, intra.astype(BF16), DN_C)
        for hd in range(DN_H):
            egl_ref[:, DN_E * hd:DN_E * hd + DN_E] = egl[DN_G * hd:DN_G * hd + DN_G].reshape(DN_G, DN_E)

    row = lambda w_: pl.BlockSpec((t, w_), lambda n: (n, 0))
    return _call(
        body, name=name, grid=(N_CHUNK // DN_G,), in_specs=_intra_in_specs(),
        out_specs=[row(DN_W), row(DN_W), row(DN_W), row(DN_W), row(DN_H * DN_C),
                   pl.BlockSpec((DN_G, DN_W), lambda n: (n, 0))],
        out_shape=[SDS((SEQ, DN_W), F32), SDS((SEQ, DN_W), BF16), SDS((SEQ, DN_W), BF16), SDS((SEQ, DN_W), BF16),
                   SDS((SEQ, DN_H * DN_C), BF16), SDS((N_CHUNK, DN_W), F32)],
        scratch_shapes=[], sem=("arbitrary",), args=(qkv, rest, alog_b, dtb_b), exch=exch)


def dn_intra_bwd(qkv, rest, alog_b, dtb_b, du, dw, dqg, dkd, dintra, degl, name):
    t = DN_G * DN_C

    def body(qkv_ref, g_ref, alog_ref, dtb_ref, du_ref, dw_ref, dqg_ref, dkd_ref, din_ref, degl_ref,
             dqkv_ref, dg_ref, dalog_ref, ddtb_ref):
        @pl.when(pl.program_id(0) == 0)
        def _():
            dalog_ref[...] = jnp.zeros_like(dalog_ref)
            ddtb_ref[...] = jnp.zeros_like(ddtb_ref)

        _out, vjp = jax.vjp(_dn_intra_from_gates, *_intra_args(qkv_ref, g_ref, alog_ref, dtb_ref))
        d_in = jnp.concatenate([din_ref[:, DN_C * hd:DN_C * hd + DN_C].reshape(DN_G, DN_C, DN_C) for hd in range(DN_H)], axis=0)
        d_egl = jnp.concatenate([degl_ref[:, DN_E * hd:DN_E * hd + DN_E].reshape(DN_G, 1, DN_E) for hd in range(DN_H)], axis=0)
        dq, dk, dv, dg, dalog, ddtb = vjp((_inst(du_ref), _inst(dw_ref), _inst(dqg_ref), _inst(dkd_ref), d_in, d_egl))
        _put_inst(dqkv_ref, dq)
        _put_inst(dqkv_ref, dk, DN_E, DN_W)
        _put_inst(dqkv_ref, dv, DN_E, 2 * DN_W)
        dg_ref[...] = dg
        for hd in range(DN_H):
            sl = slice(DN_E * hd, DN_E * hd + DN_E)
            dalog_ref[:, sl] += jnp.sum(dalog[DN_G * hd:DN_G * hd + DN_G], axis=0)
            ddtb_ref[:, sl] += jnp.sum(ddtb[DN_G * hd:DN_G * hd + DN_G], axis=0)

    row = lambda w_: pl.BlockSpec((t, w_), lambda n: (n, 0))
    acc = pl.BlockSpec((1, DN_W), lambda n: (0, 0))
    return pl.pallas_call(
        body, name=name, grid=(N_CHUNK // DN_G,),
        in_specs=_intra_in_specs() + [row(DN_W), row(DN_W), row(DN_W), row(DN_W), row(DN_H * DN_C),
                                      pl.BlockSpec((DN_G, DN_W), lambda n: (n, 0))],
        out_specs=[row(3 * DN_W), row(GATE_W), acc, acc],
        out_shape=[SDS((SEQ, 3 * DN_W), F32), SDS((SEQ, GATE_W), F32), SDS((1, DN_W), F32), SDS((1, DN_W), F32)],
        compiler_params=_cparams(("arbitrary",)),
    )(qkv, rest, alog_b, dtb_b, du, dw, dqg, dkd, dintra, degl)


DN_PAIR_LOG2 = 1
DN_PAIR = 1 << DN_PAIR_LOG2


def _chunk_rows(ref, c):
    return ref.at[pl.ds(DN_C * c, DN_C), :]


def _inter_args(u_ref, w_ref, qg_ref, kd_ref, in_ref, egl_ref, n, state):
    f = lambda r: _heads(r).astype(F32)
    intra = jnp.stack([in_ref[:, DN_C * hd:DN_C * hd + DN_C] for hd in range(DN_H)], axis=0).astype(F32)
    egl = _heads(egl_ref.at[pl.ds(n & (DN_G - 1), 1), :])
    return f(u_ref), f(w_ref), f(qg_ref), f(kd_ref), intra, egl, state


def dn_inter_fwd(u, w, qg, kdec, intra, egl, name, exch=None):
    def body(u_ref, w_ref, qg_ref, kd_ref, in_ref, egl_ref, o_ref, st_ref, state_scr):
        n2 = pl.program_id(0)

        @pl.when(n2 == 0)
        def _():
            state_scr[...] = jnp.zeros_like(state_scr)

        for c in range(DN_PAIR):
            v = functools.partial(_chunk_rows, c=c)
            st = state_scr[...]
            st_ref[c] = st
            o, ns = _dn_inter(*_inter_args(v(u_ref), v(w_ref), v(qg_ref), v(kd_ref), v(in_ref), egl_ref,
                                           DN_PAIR * n2 + c, st))
            _put_heads(v(o_ref), o)
            state_scr[...] = ns

    row = lambda w_: pl.BlockSpec((DN_PAIR * DN_C, w_), lambda n: (n, 0))
    return _call(
        body, name=name, grid=(N_CHUNK // DN_PAIR,),
        in_specs=[row(DN_W), row(DN_W), row(DN_W), row(DN_W), row(DN_H * DN_C),
                  pl.BlockSpec((DN_G, DN_W), lambda n: (n >> (DN_G_LOG2 - DN_PAIR_LOG2), 0))],
        out_specs=[row(DN_W), pl.BlockSpec((DN_PAIR, DN_H, DN_E, DN_E), lambda n: (n, 0, 0, 0))],
        out_shape=[SDS((SEQ, DN_W), F32), SDS((N_CHUNK, DN_H, DN_E, DN_E), F32)],
        scratch_shapes=[pltpu.VMEM((DN_H, DN_E, DN_E), F32)],
        sem=("arbitrary",), args=(u, w, qg, kdec, intra, egl), exch=exch)


def dn_inter_bwd(u, w, qg, kdec, intra, egl, states, do, name):
    last = N_CHUNK // DN_PAIR - 1

    def body(u_ref, w_ref, qg_ref, kd_ref, in_ref, egl_ref, st_ref, do_ref,
             du_ref, dw_ref, dqg_ref, dkd_ref, din_ref, degl_ref, dstate_scr):
        s = pl.program_id(0)

        @pl.when(s == 0)
        def _():
            dstate_scr[...] = jnp.zeros_like(dstate_scr)

        for c in reversed(range(DN_PAIR)):
            n = DN_PAIR * (last - s) + c
            v = functools.partial(_chunk_rows, c=c)
            _out, vjp = jax.vjp(_dn_inter, *_inter_args(v(u_ref), v(w_ref), v(qg_ref), v(kd_ref), v(in_ref), egl_ref,
                                                        n, st_ref[c]))
            du, dw, dqg, dkd, din, degl, dst = vjp((_heads(v(do_ref)), dstate_scr[...]))
            _put_heads(v(du_ref), du)
            _put_heads(v(dw_ref), dw)
            _put_heads(v(dqg_ref), dqg)
            _put_heads(v(dkd_ref), dkd)
            for hd in range(DN_H):
                v(din_ref)[:, DN_C * hd:DN_C * hd + DN_C] = din[hd]
            row = n & (DN_G - 1)

            @pl.when(row == DN_G - 1)
            def _():
                degl_ref[...] = jnp.zeros_like(degl_ref)

            new_row = jnp.concatenate([degl[hd] for hd in range(DN_H)], axis=1)
            rows = lax.broadcasted_iota(jnp.int32, (DN_G, DN_W), 0)
            degl_ref[...] = jnp.where(rows == row, jnp.broadcast_to(new_row, (DN_G, DN_W)), degl_ref[...])
            dstate_scr[...] = dst

    rev = lambda w_: pl.BlockSpec((DN_PAIR * DN_C, w_), lambda s: (last - s, 0))
    grp = pl.BlockSpec((DN_G, DN_W), lambda s: ((last - s) >> (DN_G_LOG2 - DN_PAIR_LOG2), 0))
    return pl.pallas_call(
        body, name=name, grid=(N_CHUNK // DN_PAIR,),
        in_specs=[rev(DN_W), rev(DN_W), rev(DN_W), rev(DN_W), rev(DN_H * DN_C), grp,
                  pl.BlockSpec((DN_PAIR, DN_H, DN_E, DN_E), lambda s: (last - s, 0, 0, 0)), rev(DN_W)],
        out_specs=[rev(DN_W), rev(DN_W), rev(DN_W), rev(DN_W), rev(DN_H * DN_C), grp],
        out_shape=[SDS((SEQ, DN_W), F32)] * 4 + [SDS((SEQ, DN_H * DN_C), F32), SDS((N_CHUNK, DN_W), F32)],
        scratch_shapes=[pltpu.VMEM((DN_H, DN_E, DN_E), F32)],
        compiler_params=_cparams(("arbitrary",)),
    )(u, w, qg, kdec, intra, egl, states, do)


OUT_T = 512


def _pool_consts(rows_total, t0, halo_before):
    lane = lax.broadcasted_iota(jnp.int32, (rows_total, POOL_W), 1)
    row = lax.broadcasted_iota(jnp.int32, (rows_total, POOL_W), 0)
    grp = (lane >= 64).astype(jnp.int32) + (lane >= 128).astype(jnp.int32) + (lane >= 192).astype(jnp.int32)
    win = jnp.where(grp == 0, 2, jnp.where(grp == 1, 4, jnp.where(grp == 2, 8, 16)))
    pos = t0 + row - halo_before
    cnt = jnp.minimum(pos + 1, win).astype(F32)
    return grp, cnt


def _pool_select(grp, s2, s4, s8, s16):
    return jnp.where(grp == 0, s2, jnp.where(grp == 1, s4, jnp.where(grp == 2, s8, s16)))


def _pooled(u_ext, t0):
    n = u_ext.shape[0]
    grp, cnt = _pool_consts(n, t0, POOL_HALO)
    s2 = u_ext + pltpu.roll(u_ext, 1, 0)
    s4 = s2 + pltpu.roll(s2, 2, 0)
    s8 = s4 + pltpu.roll(s4, 4, 0)
    s16 = s8 + pltpu.roll(s8, 8, 0)
    out = _pool_select(grp, s2, s4, s8, s16) / jnp.maximum(cnt, 1.0) - u_ext
    return out[POOL_HALO:, :]


def _merge_weights(l1, l4, l16):
    m = jnp.maximum(jnp.maximum(l1, l4), l16)
    e1 = jnp.exp(l1 - m)
    e4 = jnp.exp(l4 - m)
    e16 = jnp.exp(l16 - m)
    inv = 1.0 / (e1 + e4 + e16)
    return e1 * inv, e4 * inv, e16 * inv


def _out_parts(ol1_ref, ol4_ref, ol16_ref, pu_ref, puh_ref, odn_ref, z_ref, wbd_ref, i, t):
    w1, w4, w16 = _merge_weights(_slabs_load(ol1_ref, 2, 2), _slabs_load(ol4_ref, 2, 2), _slabs_load(ol16_ref, 2, 2))
    ya = w1 * _slabs_load(ol1_ref, 0, 2) + w4 * _slabs_load(ol4_ref, 0, 2) + w16 * _slabs_load(ol16_ref, 0, 2)
    halo = jnp.where(i > 0, puh_ref[...], 0.0)
    pooled = _pooled(jnp.concatenate([halo, pu_ref[...]], axis=0), i * t)
    pw = _dot(pooled.astype(BF16), wbd_ref[...])
    return ya, pooled, pw, (w1, w4, w16)


def _out_specs_common(t):
    def row(w, cb=0):
        return pl.BlockSpec((t, w), lambda i: (i, cb))

    halo = pl.BlockSpec((POOL_HALO, POOL_W),
                        lambda i: (jnp.maximum(i * (t // POOL_HALO) - 1, 0), R_PU // POOL_W))
    full = lambda a, b: pl.BlockSpec((a, b), lambda i: (0, 0))
    return [_slab_spec(4, t), _slab_spec(4, t), _slab_spec(4, t), row(POOL_W, R_PU // POOL_W), halo, row(DN_W), row(DN_W, R_DZ // DN_W),
            full(POOL_W, POOL_W), full(1, POOL_W), full(1, DN_W), full(D_MODEL, D_MODEL)]


def mix_out_fwd(x, ol1, ol4, ol16, rest, odn, wbd, scale, onorm_b, wout, name):
    t = OUT_T

    def body(x_ref, ol1_ref, ol4_ref, ol16_ref, pu_ref, puh_ref, odn_ref, z_ref, wbd_ref, sc_ref, on_ref, wo_ref, o_ref):
        i = pl.program_id(0)
        ya, _pooled_v, pw, _w = _out_parts(ol1_ref, ol4_ref, ol16_ref, pu_ref, puh_ref, odn_ref, z_ref, wbd_ref, i, t)
        yb = pw * sc_ref[...]
        acc = x_ref[...] + _dot(ya.astype(BF16), wo_ref[0:256, :]) + _dot(yb.astype(BF16), wo_ref[256:512, :])
        for hd in range(DN_H):
            sl = slice(DN_E * hd, DN_E * hd + DN_E)
            oh, _r = _rms_stats(odn_ref[:, sl])
            z = z_ref[:, sl]
            yc = oh * on_ref[:, sl] * (z * _sigmoid(z))
            acc = acc + _dot(yc.astype(BF16), wo_ref[512 + DN_E * hd:512 + DN_E * hd + DN_E, :])
        o_ref[...] = acc

    return pl.pallas_call(
        body, name=name, grid=(SEQ // t,),
        in_specs=[pl.BlockSpec((t, D_MODEL), lambda i: (i, 0))] + _out_specs_common(t),
        out_specs=pl.BlockSpec((t, D_MODEL), lambda i: (i, 0)),
        out_shape=SDS((SEQ, D_MODEL), F32),
        compiler_params=_cparams(("arbitrary",)),
    )(x, ol1, ol4, ol16, rest, rest, odn, rest, wbd, scale, onorm_b, wout)


def mix_out_bwd(dxo, ol1, ol4, ol16, rest, odn, wbd, scale, onorm_b, wout, headsum, name):
    t = OUT_T

    def body(dxo_ref, ol1_ref, ol4_ref, ol16_ref, pu_ref, puh_ref, odn_ref, z_ref, wbd_ref, sc_ref, on_ref, wo_ref, hs_ref,
             dwo_ref, d1_ref, d4_ref, d16_ref, dpl_ref, dodn_ref, dz_ref, dsc_ref, don_ref, dwbd_ref):
        i = pl.program_id(0)

        @pl.when(i == 0)
        def _():
            dwo_ref[...] = jnp.zeros_like(dwo_ref)
            dsc_ref[...] = jnp.zeros_like(dsc_ref)
            don_ref[...] = jnp.zeros_like(don_ref)
            dwbd_ref[...] = jnp.zeros_like(dwbd_ref)

        ya, pooled, pw, (w1, w4, w16) = _out_parts(ol1_ref, ol4_ref, ol16_ref, pu_ref, puh_ref, odn_ref, z_ref, wbd_ref, i, t)
        sc = sc_ref[...]
        dxb = dxo_ref[...].astype(BF16)
        dwo_ref[0:256, :] += _dot_tn(ya.astype(BF16), dxb)
        dwo_ref[256:512, :] += _dot_tn((pw * sc).astype(BF16), dxb)
        dya = _dot_nt(dxb, wo_ref[0:256, :])
        o1 = _slabs_load(ol1_ref, 0, 2)
        o4 = _slabs_load(ol4_ref, 0, 2)
        o16 = _slabs_load(ol16_ref, 0, 2)
        hs = hs_ref[...]
        s1 = _dot(dya * o1, hs, HI)
        s4 = _dot(dya * o4, hs, HI)
        s16 = _dot(dya * o16, hs, HI)
        sbar = w1 * s1 + w4 * s4 + w16 * s16
        _slabs_store(d1_ref, 0, w1 * dya)
        _slabs_store(d1_ref, 2, w1 * (s1 - sbar))
        _slabs_store(d4_ref, 0, w4 * dya)
        _slabs_store(d4_ref, 2, w4 * (s4 - sbar))
        _slabs_store(d16_ref, 0, w16 * dya)
        _slabs_store(d16_ref, 2, w16 * (s16 - sbar))
        dyb = _dot_nt(dxb, wo_ref[256:512, :])
        dsc_ref[...] += jnp.sum(dyb * pw, axis=0, keepdims=True)
        dpw = (dyb * sc).astype(BF16)
        dwbd_ref[...] += _dot_tn(pooled.astype(BF16), dpw)
        dpl_ref[...] = _dot_nt(dpw, wbd_ref[...])
        for hd in range(DN_H):
            sl = slice(DN_E * hd, DN_E * hd + DN_E)
            rows_w = slice(512 + DN_E * hd, 512 + DN_E * hd + DN_E)
            oh, r = _rms_stats(odn_ref[:, sl])
            z = z_ref[:, sl]
            sg = _sigmoid(z)
            sz = z * sg
            nw = on_ref[:, sl]
            on = oh * nw
            dwo_ref[rows_w, :] += _dot_tn((on * sz).astype(BF16), dxb)
            dyc = _dot_nt(dxb, wo_ref[rows_w, :])
            dz_ref[:, sl] = dyc * on * (sg * (1.0 + z * (1.0 - sg)))
            dx, dw = _rms_bwd(oh, r, nw, dyc * sz)
            dodn_ref[:, sl] = dx
            don_ref[:, sl] += dw

    row = lambda w: pl.BlockSpec((t, w), lambda i: (i, 0))
    full = lambda a, b: pl.BlockSpec((a, b), lambda i: (0, 0))
    return pl.pallas_call(
        body, name=name, grid=(SEQ // t,),
        in_specs=[row(D_MODEL)] + _out_specs_common(t) + [full(ATT_W, ATT_W)],
        out_specs=[full(D_MODEL, D_MODEL), _slab_spec(4, t), _slab_spec(4, t), _slab_spec(4, t), row(POOL_W), row(DN_W), row(DN_W),
                   full(1, POOL_W), full(1, DN_W), full(POOL_W, POOL_W)],
        out_shape=[SDS((D_MODEL, D_MODEL), F32), SDS((4, SEQ, 128), F32), SDS((4, SEQ, 128), F32), SDS((4, SEQ, 128), F32),
                   SDS((SEQ, POOL_W), F32), SDS((SEQ, DN_W), F32), SDS((SEQ, DN_W), F32),
                   SDS((1, POOL_W), F32), SDS((1, DN_W), F32), SDS((POOL_W, POOL_W), F32)],
        compiler_params=_cparams(("arbitrary",)),
    )(dxo, ol1, ol4, ol16, rest, rest, odn, rest, wbd, scale, onorm_b, wout, headsum)


def pool_bwd(dpooled, name):
    t = 512
    nt = SEQ // t

    def body(d_ref, dn_ref, o_ref):
        i = pl.program_id(0)
        halo = jnp.where(i < nt - 1, dn_ref[...], 0.0)
        d_ext = jnp.concatenate([d_ref[...], halo], axis=0)
        n = t + POOL_HALO
        grp, cnt = _pool_consts(n, i * t, 0)
        dq = d_ext / cnt
        s2 = dq + pltpu.roll(dq, n - 1, 0)
        s4 = s2 + pltpu.roll(s2, n - 2, 0)
        s8 = s4 + pltpu.roll(s4, n - 4, 0)
        s16 = s8 + pltpu.roll(s8, n - 8, 0)
        o_ref[...] = (_pool_select(grp, s2, s4, s8, s16) - d_ext)[0:t, :]

    return pl.pallas_call(
        body, name=name, grid=(nt,),
        in_specs=[pl.BlockSpec((t, POOL_W), lambda i: (i, 0)),
                  pl.BlockSpec((POOL_HALO, POOL_W),
                               lambda i: (jnp.minimum((i + 1) * (t // POOL_HALO), SEQ // POOL_HALO - 1), 0))],
        out_specs=pl.BlockSpec((t, POOL_W), lambda i: (i, 0)),
        out_shape=SDS((SEQ, POOL_W), F32),
        compiler_params=_cparams(("arbitrary",)),
    )(dpooled, dpooled)


N_PEER = N_DEV - 1
ANY_SPEC = pl.BlockSpec(memory_space=pl.ANY)


class Exchange:
    def __init__(self, arrays, mode):
        self.arrays = list(arrays)
        self.mode = mode
        n = len(self.arrays)
        if mode == "scatter":
            self.out_shape = [SDS(a.shape, a.dtype) for a in self.arrays]
        else:
            self.out_shape = [SDS((N_DEV,) + a.shape, a.dtype) for a in self.arrays]
        self.scratch = [pltpu.SemaphoreType.DMA((n * N_PEER,)), pltpu.SemaphoreType.DMA((n * N_PEER,)),
                        pltpu.SemaphoreType.DMA((n,))]

    @staticmethod
    def _place():
        x, y, c = lax.axis_index("x"), lax.axis_index("y"), lax.axis_index("c")
        chips = [(1 - x, y), (x, 1 - y), (1 - x, 1 - y)]
        return x, y, c, chips

    @staticmethod
    def _copy(sems, a, k, src, dst, to):
        send_sems, recv_sems, _ = sems
        return pltpu.make_async_remote_copy(
            src_ref=src, dst_ref=dst, send_sem=send_sems.at[a * N_PEER + k], recv_sem=recv_sems.at[a * N_PEER + k],
            device_id=to, device_id_type=MESH)

    def _scatter_peers(self):
        x, y, c, _ = self._place()
        out = []
        for fx, fy, fc in ((0, 0, 1), (1, 0, 0), (0, 1, 0), (1, 1, 0), (1, 0, 1), (0, 1, 1), (1, 1, 1)):
            px, py, pc = x ^ fx, y ^ fy, c ^ fc
            out.append(((px, py, pc), 4 * px + 2 * py + pc))
        return 4 * x + 2 * y + c, out

    def _local(self, ins, outs, sems, a, me):
        src = ins[a].at[me] if self.mode == "scatter" else ins[a]
        return pltpu.make_async_copy(src, outs[a].at[me], sems[2].at[a])

    def start(self, ins, outs, sems):
        if self.mode == "scatter":
            me, peers = self._scatter_peers()
            for a in range(len(ins)):
                self._local(ins, outs, sems, a, me).start()
                for k, (peer, pidx) in enumerate(peers):
                    self._copy(sems, a, k, ins[a].at[pidx], outs[a].at[me], peer).start()
            return
        x, y, c, chips = self._place()
        me = 4 * x + 2 * y + c
        for a in range(len(ins)):
            self._local(ins, outs, sems, a, me).start()
            self._copy(sems, a, 0, ins[a], outs[a].at[me], (x, y, 1 - c)).start()
            for j, (cx, cy) in enumerate(chips):
                self._copy(sems, a, 1 + j, ins[a], outs[a].at[me], (cx, cy, c)).start()

    def finish(self, ins, outs, sems):
        n = len(ins)
        if self.mode == "scatter":
            me, peers = self._scatter_peers()
            for a in range(n):
                for k, (peer, pidx) in enumerate(peers):
                    self._copy(sems, a, k, ins[a].at[pidx], outs[a].at[pidx], peer).wait_recv()
            for a in range(n):
                for k, (peer, pidx) in enumerate(peers):
                    self._copy(sems, a, k, ins[a].at[pidx], outs[a].at[me], peer).wait_send()
                self._local(ins, outs, sems, a, me).wait()
            return
        x, y, c, chips = self._place()
        me = 4 * x + 2 * y + c
        sib = (x, y, 1 - c)
        for a in range(n):
            for j, (cx, cy) in enumerate(chips):
                blk = outs[a].at[4 * cx + 2 * cy + c]
                self._copy(sems, a, 1 + j, ins[a], blk, (cx, cy, c)).wait_recv()
                self._copy(sems, a, 4 + j, blk, blk, sib).start()
        for a in range(n):
            self._copy(sems, a, 0, ins[a], outs[a].at[4 * x + 2 * y + (1 - c)], sib).wait_recv()
            for j, (cx, cy) in enumerate(chips):
                blk = outs[a].at[4 * cx + 2 * cy + (1 - c)]
                self._copy(sems, a, 4 + j, blk, blk, sib).wait_recv()
        for a in range(n):
            for k in range(N_PEER):
                self._copy(sems, a, k, ins[a], outs[a].at[me], sib).wait_send()
            self._local(ins, outs, sems, a, me).wait()


def run_exchanges(exchs, name):
    counts = [len(e.arrays) for e in exchs]
    n = sum(counts)

    def body(*refs):
        ins, outs, sems = refs[:n], refs[n:2 * n], refs[2 * n:]
        parts, off = [], 0
        for j, c in enumerate(counts):
            parts.append((ins[off:off + c], outs[off:off + c], sems[3 * j:3 * j + 3]))
            off += c
        for e, p in zip(exchs, parts):
            e.start(*p)
        for e, p in zip(exchs, parts):
            e.finish(*p)

    res = pl.pallas_call(
        body, name=name, in_specs=[ANY_SPEC] * n, out_specs=[ANY_SPEC] * n,
        out_shape=[s for e in exchs for s in e.out_shape], scratch_shapes=[s for e in exchs for s in e.scratch],
    )(*[a for e in exchs for a in e.arrays])
    out, off = [], 0
    for c in counts:
        out.append(list(res[off:off + c]))
        off += c
    return out


def run_exchange(exch, name):
    return run_exchanges([exch], name)[0]


def _call(body, *, name, grid, in_specs, out_specs, out_shape, scratch_shapes, sem, args, exch=None):
    if exch is None:
        res = pl.pallas_call(body, name=name, grid=grid, in_specs=in_specs, out_specs=out_specs, out_shape=out_shape,
                             scratch_shapes=scratch_shapes, compiler_params=_cparams(sem))(*args)
        return res, None
    single = not isinstance(out_shape, (list, tuple))
    out_specs_l = [out_specs] if single else list(out_specs)
    out_shape_l = [out_shape] if single else list(out_shape)
    n_in, n_out, n_scr, m = len(in_specs), len(out_specs_l), len(scratch_shapes), len(exch.arrays)

    def wrapped(*refs):
        p = 0
        ins = refs[p:p + n_in]; p += n_in
        xin = refs[p:p + m]; p += m
        outs = refs[p:p + n_out]; p += n_out
        xout = refs[p:p + m]; p += m
        scr = refs[p:p + n_scr]; p += n_scr
        sems = refs[p:]
        ids = [pl.program_id(ax) for ax in range(len(grid))]
        first = functools.reduce(jnp.logical_and, [i == 0 for i in ids])
        last = functools.reduce(jnp.logical_and, [i == g - 1 for i, g in zip(ids, grid)])

        @pl.when(first)
        def _():
            exch.start(xin, xout, sems)

        body(*ins, *outs, *scr)

        @pl.when(last)
        def _():
            exch.finish(xin, xout, sems)

    res = pl.pallas_call(
        wrapped, name=name, grid=grid, in_specs=list(in_specs) + [ANY_SPEC] * m,
        out_specs=out_specs_l + [ANY_SPEC] * m, out_shape=out_shape_l + exch.out_shape,
        scratch_shapes=list(scratch_shapes) + exch.scratch, compiler_params=_cparams(sem),
    )(*args, *exch.arrays)
    outs = res[:n_out]
    return (outs[0] if single else outs), res[n_out:]


def _adam_math(w, g, m, v):
    m2 = ADAM_B1 * m + (1.0 - ADAM_B1) * g
    v2 = ADAM_B2 * v + (1.0 - ADAM_B2) * (g * g)
    m_hat = m2 / (1.0 - ADAM_B1 ** ADAM_STEP)
    v_hat = v2 / (1.0 - ADAM_B2 ** ADAM_STEP)
    delta = -ADAM_LR * (m_hat / (jnp.sqrt(v_hat) + ADAM_EPS) + ADAM_WD * w)
    return delta, m2, v2


ADAM_ROW_BLOCKS = 2


def adam_shard(parts0, parts1, w, m, v, name):
    _, r, c = w.shape
    rb = r // ADAM_ROW_BLOCKS

    def body(p0_ref, p1_ref, w_ref, m_ref, v_ref, g_ref, d_ref, m2_ref, v2_ref):
        def run(p_ref):
            g = p_ref[0].astype(F32)
            for i in range(1, N_DEV):
                g = g + p_ref[i].astype(F32)
            delta, m2, v2 = _adam_math(w_ref[0], g, m_ref[0], v_ref[0])
            g_ref[0] = g
            d_ref[0] = delta
            m2_ref[0] = m2
            v2_ref[0] = v2

        @pl.when(pl.program_id(0) == 0)
        def _():
            run(p0_ref)

        @pl.when(pl.program_id(0) == 1)
        def _():
            run(p1_ref)

    def p_spec(layer):
        row = (lambda l, j: jnp.where(l == 0, j, ADAM_ROW_BLOCKS - 1)) if layer == 0 else (lambda l, j: jnp.where(l == 1, j, 0))
        return pl.BlockSpec((N_DEV, rb, c), lambda l, j: (0, row(l, j), 0))

    blk = pl.BlockSpec((1, rb, c), lambda l, j: (l, j, 0))
    return pl.pallas_call(
        body, name=name, grid=(DEPTH, ADAM_ROW_BLOCKS),
        in_specs=[p_spec(0), p_spec(1), blk, blk, blk], out_specs=[blk] * 4,
        out_shape=[SDS(w.shape, F32)] * 4,
        compiler_params=_cparams(("arbitrary", "arbitrary")),
    )(parts0, parts1, w, m, v)


def parts_sum(parts0, parts1, name):
    _, r, c = parts0.shape

    def body(p0_ref, p1_ref, g_ref):
        def run(p_ref):
            g = p_ref[0].astype(F32)
            for i in range(1, N_DEV):
                g = g + p_ref[i].astype(F32)
            g_ref[0] = g

        @pl.when(pl.program_id(0) == 0)
        def _():
            run(p0_ref)

        @pl.when(pl.program_id(0) == 1)
        def _():
            run(p1_ref)

    full = pl.BlockSpec((N_DEV, r, c), lambda l: (0, 0, 0))
    return pl.pallas_call(
        body, name=name, grid=(DEPTH,), in_specs=[full, full],
        out_specs=pl.BlockSpec((1, r, c), lambda l: (l, 0, 0)), out_shape=SDS((DEPTH, r, c), F32),
        compiler_params=_cparams(("arbitrary",)),
    )(parts0, parts1)


def adam_given(g, w, m, v, name):
    _, r, c = w.shape

    def body(g_ref, w_ref, m_ref, v_ref, d_ref, m2_ref, v2_ref):
        delta, m2, v2 = _adam_math(w_ref[0], g_ref[0], m_ref[0], v_ref[0])
        d_ref[0] = delta
        m2_ref[0] = m2
        v2_ref[0] = v2

    blk = pl.BlockSpec((1, r, c), lambda l: (l, 0, 0))
    return pl.pallas_call(
        body, name=name, grid=(DEPTH,), in_specs=[blk] * 4, out_specs=[blk] * 3, out_shape=[SDS(w.shape, F32)] * 3,
        compiler_params=_cparams(("arbitrary",)),
    )(g, w, m, v)


def adam_small(parts, w, m, v, name):
    def body(p_ref, w_ref, m_ref, v_ref, g_ref, d_ref, m2_ref, v2_ref):
        g = p_ref[0]
        for i in range(1, N_DEV):
            g = g + p_ref[i]
        delta, m2, v2 = _adam_math(w_ref[...], g, m_ref[...], v_ref[...])
        g_ref[...] = g
        d_ref[...] = delta
        m2_ref[...] = m2
        v2_ref[...] = v2

    return pl.pallas_call(
        body, name=name, out_shape=[SDS(w.shape, F32)] * 4, compiler_params=_cparams(),
    )(parts, w, m, v)


def _rot_cols(w):
    w4 = w.reshape(w.shape[0], 4, 2, 32)
    return jnp.stack([-w4[:, :, 1], w4[:, :, 0]], axis=2).reshape(w.shape[0], ATT_W)


def _rot_cols_t(dw_rot):
    d4 = dw_rot.reshape(dw_rot.shape[0], 4, 2, 32)
    return jnp.stack([d4[:, :, 1], -d4[:, :, 0]], axis=2).reshape(dw_rot.shape[0], ATT_W)


def build_wext(w_in):
    aq, ak, av, pu = w_in[:, 0:256], w_in[:, 256:512], w_in[:, 512:768], w_in[:, 768:1024]
    dqkvz = w_in[:, 1024:3072]
    gates = jnp.pad(w_in[:, 3072:3080], ((0, 0), (0, GATE_W - 2 * DN_H)))
    return jnp.concatenate([aq, ak, av, _rot_cols(aq), _rot_cols(ak), dqkvz, gates, pu], axis=1)


def fold_dwext(d):
    b = EXT_ATT
    aq = d[:, 0:256] + _rot_cols_t(d[:, 768:1024])
    ak = d[:, 256:512] + _rot_cols_t(d[:, 1024:1280])
    av = d[:, 512:768]
    dqkvz = d[:, b:b + 2048]
    gates = d[:, b + R_G:b + R_G + 2 * DN_H]
    pu = d[:, b + R_PU:b + R_PU + 256]
    return jnp.concatenate([aq, ak, av, pu, dqkvz, gates], axis=1)


def _block_diag(pw):
    z = jnp.zeros((4, 64, 4, 64), pw.dtype)
    for g in range(4):
        z = z.at[g, :, g, :].set(pw[g])
    return z.reshape(POOL_W, POOL_W)


def _diag_blocks(m):
    m4 = m.reshape(4, 64, 4, 64)
    return jnp.stack([m4[g, :, g, :] for g in range(4)], axis=0)


def _lanes(v, reps):
    return jnp.repeat(v, reps)[None, :]


def layer_fwd(p, xa, cos, sin, l, host=None):
    host = host or {}

    def carried(key):
        return host[key][0] if key in host else None

    def done(key, xo):
        if key in host:
            host[key][1](xo)

    xb, xo = ffn_fwd(xa, p["n1"], *p["f1"], f"ffn1_fwd_{l}", carried("ffn1"))
    done("ffn1", xo)
    att, rest = mix_in_fwd(xb, p["nm"], p["wext"], cos, sin, f"mix_in_fwd_{l}")
    ols = [att_fwd_s(att, d, f"att_fwd_{l}_{d}") for d in DILATIONS]
    qkv = dn_prep_fwd(rest, p["conv"], f"dn_prep_fwd_{l}")
    dn, xo = dn_intra_fwd(qkv, rest, p["alog"], p["dtb"], f"dn_intra_fwd_{l}", carried("dn_intra"))
    done("dn_intra", xo)
    (odn, states), xo = dn_inter_fwd(*dn, f"dn_inter_fwd_{l}", carried("dn_inter"))
    done("dn_inter", xo)
    xc = mix_out_fwd(xb, ols[0], ols[1], ols[2], rest, odn, p["wbd"], p["scale"], p["onorm"], p["wout"], f"mix_out_fwd_{l}")
    xd, xo = ffn_fwd(xc, p["n2"], *p["f2"], f"ffn2_fwd_{l}", carried("ffn2"))
    done("ffn2", xo)
    return xd, dict(xa=xa, xb=xb, xc=xc, att=att, rest=rest, ols=ols, qkv=qkv, dn=dn, odn=odn, states=states)


def layer_bwd(p, s, dx, cos, sin, headsum, l, scatter=False, carry=None):
    blocks = lambda ws: [w_.reshape(N_DEV, FF_BLK, D_MODEL) for w_ in ws]
    (dx, *d_f2, d_n2), carried = ffn_bwd(s["xc"], dx, p["n2"], *p["f2"], f"ffn2_bwd_{l}", carry)
    (d_wout, dol1, dol4, dol16, dpooled, dodn, dz, dscale, donorm, dwbd) = mix_out_bwd(
        dx, s["ols"][0], s["ols"][1], s["ols"][2], s["rest"], s["odn"], p["wbd"], p["scale"], p["onorm"], p["wout"],
        headsum, f"mix_out_bwd_{l}")
    dpu = pool_bwd(dpooled, f"pool_bwd_{l}")
    f2 = blocks(d_f2)
    d_dn = dn_inter_bwd(*s["dn"], s["states"], dodn, f"dn_inter_bwd_{l}")
    dqkv, dg, dalog, ddtb = dn_intra_bwd(s["qkv"], s["rest"], p["alog"], p["dtb"], *d_dn, f"dn_intra_bwd_{l}")
    d_dqkv, dconv = dn_prep_bwd(s["rest"], p["conv"], dqkv, f"dn_prep_bwd_{l}")
    datts = [att_bwd_s(s["att"], ol, dol, d, f"att_bwd_{l}_{d}")
             for d, ol, dol in zip(DILATIONS, s["ols"], (dol1, dol4, dol16))]
    dproj = assemble_dproj(datts, cos, sin, d_dqkv, dz, dg, dpu, f"assemble_dproj_{l}")
    dx, d_wext, d_nm = linear_bwd(s["xb"], dx, p["nm"], dproj, p["wext"], f"mix_in_bwd_{l}")
    d_win = fold_dwext(d_wext).reshape(D_MODEL, N_DEV, IN_BLK).transpose(1, 0, 2).astype(BF16)
    io = [d_win, d_wout.reshape(N_DEV, D_MODEL // N_DEV, D_MODEL).astype(BF16)]
    (dx, *d_f1, d_n1), xo = ffn_bwd(s["xa"], dx, p["n1"], *p["f1"], f"ffn1_bwd_{l}",
                                    Exchange(f2 + io, "scatter") if scatter else None)
    if scatter:
        f2, io = list(xo[:3]), list(xo[3:])
    big = dict(f1=blocks(d_f1), f2=f2, io=io)
    small = dict(ffn1_norm=d_n1[0], mix_norm=d_nm[0], ffn2_norm=d_n2[0], pool_w=_diag_blocks(dwbd),
                 pool_scale=dscale[0], dn_a_log=dalog.reshape(DN_H, DN_E).sum(-1),
                 dn_dt_bias=ddtb.reshape(DN_H, DN_E).sum(-1),
                 dn_out_norm=donorm.reshape(DN_H, DN_E).sum(0), dn_conv_w=dconv)
    return dx, big, small, carried


def small_operands(l, pool_w, pool_scale, dn_out_norm, dn_a_log, dn_dt_bias, ffn1_norm, mix_norm, ffn2_norm):
    return dict(
        wbd=_block_diag(pool_w[l]).astype(BF16),
        scale=pool_scale[l][None, :],
        onorm=jnp.tile(dn_out_norm[l], DN_H)[None, :],
        alog=_lanes(dn_a_log[l], DN_E),
        dtb=_lanes(dn_dt_bias[l], DN_E),
        n1=ffn1_norm[l][None, :], nm=mix_norm[l][None, :], n2=ffn2_norm[l][None, :])


def set_mixer_weights(p, win_g, wout_g, conv_g):
    p["wext"] = build_wext(win_g.transpose(1, 0, 2).reshape(D_MODEL, IN_W))
    p["wout"] = wout_g.reshape(D_MODEL, D_MODEL)
    p["conv"] = conv_g.transpose(1, 0, 2).reshape(DN_CONV, 3 * DN_W)


def rope_tables(pos):
    inv_freq = 10000.0 ** (-jnp.arange(0, ATT_E, 2, dtype=F32) / ATT_E)
    ang = pos.astype(F32)[:, None] * inv_freq
    return jnp.tile(jnp.cos(ang), (1, 8)), jnp.tile(jnp.sin(ang), (1, 8))


def head_sum_matrix():
    return jnp.kron(jnp.eye(4, dtype=F32), jnp.ones((ATT_E, ATT_E), F32))


SMALL_NAMES = ("ffn1_norm", "mix_norm", "ffn2_norm", "pool_w", "pool_scale", "dn_a_log", "dn_dt_bias",
               "dn_out_norm", "final_norm", "dn_conv_w")


PACK_UNIT = 8 * 128


def _pack_rows(n):
    return -(-n // PACK_UNIT) * 8


def _pack(parts):
    rows = []
    for p in parts:
        flat = p.reshape(-1)
        r = _pack_rows(flat.shape[0])
        rows.append(jnp.pad(flat, (0, r * 128 - flat.shape[0])).reshape(r, 128))
    return jnp.concatenate(rows, axis=0)


def _unpack(packed, shapes):
    out, row = [], 0
    for s in shapes:
        n = math.prod(s)
        r = _pack_rows(n)
        out.append(packed[row:row + r].reshape(-1)[:n].reshape(s))
        row += r
    return out


def kernel(x, positions, ffn1_norm, ffn1_w_gate, ffn1_w_up, ffn1_w_down, mix_norm, w_in, pool_w, pool_scale, dn_conv_w, dn_a_log, dn_dt_bias, dn_out_norm, w_out, ffn2_norm, ffn2_w_gate, ffn2_w_up, ffn2_w_down, final_norm, loss_target, m_ffn1_norm, m_ffn1_w_gate, m_ffn1_w_up, m_ffn1_w_down, m_mix_norm, m_w_in, m_pool_w, m_pool_scale, m_dn_conv_w, m_dn_a_log, m_dn_dt_bias, m_dn_out_norm, m_w_out, m_ffn2_norm, m_ffn2_w_gate, m_ffn2_w_up, m_ffn2_w_down, m_final_norm, v_ffn1_norm, v_ffn1_w_gate, v_ffn1_w_up, v_ffn1_w_down, v_mix_norm, v_w_in, v_pool_w, v_pool_scale, v_dn_conv_w, v_dn_a_log, v_dn_dt_bias, v_dn_out_norm, v_w_out, v_ffn2_norm, v_ffn2_w_gate, v_ffn2_w_up, v_ffn2_w_down, v_final_norm):
    me = 4 * lax.axis_index("x") + 2 * lax.axis_index("y") + lax.axis_index("c")
    x0 = x[0]
    target = loss_target[0]

    cos, sin = rope_tables(positions[0])
    headsum = head_sum_matrix()

    layers = [small_operands(l, pool_w, pool_scale, dn_out_norm, dn_a_log, dn_dt_bias, ffn1_norm, mix_norm, ffn2_norm)
              for l in range(DEPTH)]

    def whole(gathered):
        return gathered.reshape(D_FF, D_MODEL)

    def gather_ffn1(l):
        def on_done(xo):
            layers[l]["f1"] = tuple(whole(g) for g in xo)
        return Exchange(ffn_shard_operands(ffn1_w_gate[l], ffn1_w_up[l], ffn1_w_down[l]), "gather"), on_done

    def gather_mixer(l):
        def on_done(xo):
            set_mixer_weights(layers[l], *xo)
        return Exchange([w_in[l].astype(BF16), w_out[l].astype(BF16), dn_conv_w[l]], "gather"), on_done

    gathered_f2 = {}

    def gather_ffn2_part(l, part):
        def on_done(xo):
            gathered_f2[(l, part)] = [whole(g) for g in xo]
            if (l, 0) in gathered_f2 and (l, 1) in gathered_f2:
                layers[l]["f2"] = tuple(gathered_f2[(l, 0)] + gathered_f2[(l, 1)])
        ops = ffn_shard_operands(ffn2_w_gate[l], ffn2_w_up[l], ffn2_w_down[l])
        return Exchange(ops[:2] if part == 0 else ops[2:], "gather"), on_done

    first, on_first = gather_ffn1(0)
    on_first(run_exchange(first, "gather_ffn1_0"))
    saved = []
    xa = x0
    for l in range(DEPTH):
        host = {"ffn1": gather_mixer(l), "dn_intra": gather_ffn2_part(l, 0), "dn_inter": gather_ffn2_part(l, 1)}
        if l + 1 < DEPTH:
            host["ffn2"] = gather_ffn1(l + 1)
        xa, s = layer_fwd(layers[l], xa, cos, sin, l, host)
        saved.append(s)

    loss_row, dx, d_final = loss_head(xa, final_norm[None, :], target, "loss_head")
    loss = lax.psum(loss_row[0, 0], ("x", "y", "c"))

    small = {}
    big_parts = [None] * DEPTH
    carry = None
    for l in reversed(range(DEPTH)):
        dx, big, small[l], carried = layer_bwd(layers[l], saved[l], dx, cos, sin, headsum, l, True, carry)
        if carried is not None:
            big_parts[l + 1]["f1"] = list(carried)
        big_parts[l] = big
        carry = Exchange(big["f1"], "scatter")
    grad_x = dx[None]

    small_shapes = {"ffn1_norm": (DEPTH, D_MODEL), "mix_norm": (DEPTH, D_MODEL), "ffn2_norm": (DEPTH, D_MODEL),
                    "pool_w": (DEPTH, 4, 64, 64), "pool_scale": (DEPTH, POOL_W), "dn_a_log": (DEPTH, DN_H),
                    "dn_dt_bias": (DEPTH, DN_H), "dn_out_norm": (DEPTH, DN_E), "final_norm": (D_MODEL,),
                    "dn_conv_w": (DEPTH, DN_CONV, 3 * DN_W)}
    g_small = {n: (d_final[0] if n == "final_norm" else jnp.stack([small[l][n] for l in range(DEPTH)]))
               for n in SMALL_NAMES}
    f1_parts, (small_parts,) = run_exchanges(
        [carry, Exchange([_pack([g_small[n] for n in SMALL_NAMES])], "gather")], "scatter_ffn1_0_gather_small")
    big_parts[0]["f1"] = f1_parts

    def conv_full(a):
        return lax.dynamic_update_slice(jnp.zeros((DEPTH, DN_CONV, 3 * DN_W), F32), a, (0, 0, me * (3 * DN_W // N_DEV)))

    given = dict(ffn1_norm=(ffn1_norm, m_ffn1_norm, v_ffn1_norm), mix_norm=(mix_norm, m_mix_norm, v_mix_norm),
                 ffn2_norm=(ffn2_norm, m_ffn2_norm, v_ffn2_norm), pool_w=(pool_w, m_pool_w, v_pool_w),
                 pool_scale=(pool_scale, m_pool_scale, v_pool_scale), dn_a_log=(dn_a_log, m_dn_a_log, v_dn_a_log),
                 dn_dt_bias=(dn_dt_bias, m_dn_dt_bias, v_dn_dt_bias),
                 dn_out_norm=(dn_out_norm, m_dn_out_norm, v_dn_out_norm),
                 final_norm=(final_norm, m_final_norm, v_final_norm),
                 dn_conv_w=(conv_full(dn_conv_w), conv_full(m_dn_conv_w), conv_full(v_dn_conv_w)))
    packed_wmv = [_pack([given[n][k] for n in SMALL_NAMES]) for k in range(3)]
    small_out = adam_small(small_parts, *packed_wmv, "adam_small")
    shapes = [small_shapes[n] for n in SMALL_NAMES]
    small_res = {n: [] for n in SMALL_NAMES}
    for arr in small_out:
        for n, v_ in zip(SMALL_NAMES, _unpack(arr, shapes)):
            if n == "dn_conv_w":
                v_ = lax.dynamic_slice(v_, (0, 0, me * (3 * DN_W // N_DEV)), (DEPTH, DN_CONV, 3 * DN_W // N_DEV))
            small_res[n].append(v_)

    def parts_of(group, idx):
        return [big_parts[l][group][idx] for l in range(DEPTH)]

    def adam_transposed(group, idx, w, m, v, name):
        g = parts_sum(*parts_of(group, idx), f"sum_{name}").transpose(0, 2, 1)
        return [g] + list(adam_given(g, w, m, v, f"adam_{name}"))

    big_res = dict(
        ffn1_w_gate=adam_transposed("f1", 0, ffn1_w_gate, m_ffn1_w_gate, v_ffn1_w_gate, "ffn1_gate"),
        ffn1_w_up=adam_transposed("f1", 1, ffn1_w_up, m_ffn1_w_up, v_ffn1_w_up, "ffn1_up"),
        ffn1_w_down=adam_shard(*parts_of("f1", 2), ffn1_w_down, m_ffn1_w_down, v_ffn1_w_down, "adam_ffn1_down"),
        ffn2_w_gate=adam_transposed("f2", 0, ffn2_w_gate, m_ffn2_w_gate, v_ffn2_w_gate, "ffn2_gate"),
        ffn2_w_up=adam_transposed("f2", 1, ffn2_w_up, m_ffn2_w_up, v_ffn2_w_up, "ffn2_up"),
        ffn2_w_down=adam_shard(*parts_of("f2", 2), ffn2_w_down, m_ffn2_w_down, v_ffn2_w_down, "adam_ffn2_down"),
        w_in=adam_shard(*parts_of("io", 0), w_in, m_w_in, v_w_in, "adam_w_in"),
        w_out=adam_shard(*parts_of("io", 1), w_out, m_w_out, v_w_out, "adam_w_out"),
    )

    order = ("ffn1_norm", "ffn1_w_gate", "ffn1_w_up", "ffn1_w_down", "mix_norm", "w_in", "pool_w", "pool_scale",
             "dn_conv_w", "dn_a_log", "dn_dt_bias", "dn_out_norm", "w_out", "ffn2_norm", "ffn2_w_gate", "ffn2_w_up",
             "ffn2_w_down", "final_norm")
    res = {**small_res, **big_res}
    outs = [loss, grad_x]
    for k in range(4):
        outs.extend(res[n][k] for n in order)
    return tuple(outs)
```

```python
import functools
import math

import jax
import jax.numpy as jnp
from jax import lax
from jax.experimental import pallas as pl
from jax.experimental.pallas import tpu as pltpu

F32 = jnp.float32
BF16 = jnp.bfloat16
HI = lax.Precision.HIGHEST
INV_PREC = lax.Precision.HIGH
SDS = jax.ShapeDtypeStruct

N_DEV = 8
SEQ = 4096
D_MODEL = 1024
DEPTH = 2
D_FF = 2816
FF_BLK = D_FF // N_DEV
ATT_W = 256
ATT_E = 64
ATT_BLK = 128
DILATIONS = (1, 4, 16)
POOL_W = 256
POOL_HALO = 16
DN_W = 512
DN_H = 4
DN_E = 128
DN_C = 64
N_CHUNK = SEQ // DN_C
IN_W = 3080
IN_BLK = IN_W // N_DEV
EPS = 1e-6
EXT_ATT = 1280
GATE_W = 256
EXT_REST = 4 * DN_W + GATE_W + POOL_W
EXT_W = EXT_ATT + EXT_REST
R_DQKV, R_DZ, R_G, R_PU = 0, 1536, 2048, 2304

ADAM_LR, ADAM_B1, ADAM_B2, ADAM_EPS, ADAM_WD, ADAM_STEP = 0.001, 0.9, 0.999, 1e-08, 0.01, 10

VMEM_LIMIT = 60 * 1024 * 1024
MESH = pl.DeviceIdType.MESH


def _cparams(sem=None):
    kw = dict(vmem_limit_bytes=VMEM_LIMIT)
    if sem is not None:
        kw["dimension_semantics"] = sem
    return pltpu.CompilerParams(**kw)


def _dot(a, b, prec=None):
    return jnp.dot(a, b, preferred_element_type=F32, precision=prec)


def _dot_nt(a, b, prec=None):
    return lax.dot_general(a, b, (((1,), (1,)), ((), ())), preferred_element_type=F32, precision=prec)


def _dot_tn(a, b, prec=None):
    return lax.dot_general(a, b, (((0,), (0,)), ((), ())), preferred_element_type=F32, precision=prec)


def _sigmoid(x):
    return jax.nn.sigmoid(x)


def _rms_stats(x):
    r = lax.rsqrt(jnp.mean(x * x, axis=-1, keepdims=True) + EPS)
    return x * r, r


def _rms_bwd(xh, r, w, dh):
    dxh = dh * w
    dx = r * (dxh - xh * jnp.mean(dxh * xh, axis=-1, keepdims=True))
    return dx, jnp.sum(dh * xh, axis=0, keepdims=True)


FFN_T_FWD = 2048
FFN_T_BWD = 512
FF_TILE = 256
N_FF_TILE = D_FF // FF_TILE


def ffn_shard_operands(gate, up, down):
    return [gate.T.astype(BF16), up.T.astype(BF16), down.astype(BF16)]


def ffn_fwd(x, nw, wgt, wut, wd, name, exch=None):
    t = FFN_T_FWD

    def body(x_ref, nw_ref, wgt_ref, wut_ref, wd_ref, o_ref, h_scr, acc_scr):
        k = pl.program_id(1)

        @pl.when(k == 0)
        def _():
            xh, _r = _rms_stats(x_ref[...])
            h_scr[...] = (xh * nw_ref[...]).astype(BF16)
            acc_scr[...] = jnp.zeros_like(acc_scr)

        h = h_scr[...]
        hg = _dot_nt(h, wgt_ref[...])
        hu = _dot_nt(h, wut_ref[...])
        a = (hg * _sigmoid(hg) * hu).astype(BF16)
        acc_scr[...] += _dot(a, wd_ref[...])

        @pl.when(k == N_FF_TILE - 1)
        def _():
            o_ref[...] = x_ref[...] + 0.5 * acc_scr[...]

    w_spec = pl.BlockSpec((FF_TILE, D_MODEL), lambda i, k: (k, 0))
    return _call(
        body, name=name, grid=(SEQ // t, N_FF_TILE),
        in_specs=[pl.BlockSpec((t, D_MODEL), lambda i, k: (i, 0)),
                  pl.BlockSpec((1, D_MODEL), lambda i, k: (0, 0)), w_spec, w_spec, w_spec],
        out_specs=pl.BlockSpec((t, D_MODEL), lambda i, k: (i, 0)),
        out_shape=SDS((SEQ, D_MODEL), F32),
        scratch_shapes=[pltpu.VMEM((t, D_MODEL), BF16), pltpu.VMEM((t, D_MODEL), F32)],
        sem=("arbitrary", "arbitrary"), args=(x, nw, wgt, wut, wd), exch=exch)


def ffn_bwd(x, dxo, nw, wgt, wut, wd, name, exch=None):
    t = FFN_T_BWD
    nt = SEQ // t

    def body(x_ref, dxo_ref, nw_ref, wgt_ref, wut_ref, wd_ref, dx_ref, dwgt_ref, dwut_ref, dwd_ref, dnw_ref,
             dh_scr, ag_scr, au_scr, ad_scr, h_scr):
        k = pl.program_id(0)
        i = pl.program_id(1)
        rows = pl.ds(pl.multiple_of(i * t, t), t)
        nw_v = nw_ref[...]

        @pl.when(k == 0)
        def _():
            xh0, _r0 = _rms_stats(x_ref[...])
            h_scr[rows, :] = (xh0 * nw_v).astype(BF16)

        h = h_scr[rows, :]
        dy = (0.5 * dxo_ref[...]).astype(BF16)
        wgt = wgt_ref[...]
        wut = wut_ref[...]
        hg = _dot_nt(h, wgt)
        hu = _dot_nt(h, wut)
        sg = _sigmoid(hg)
        sil = hg * sg
        a = (sil * hu).astype(BF16)
        da = _dot_nt(dy, wd_ref[...])
        dhu = (da * sil).astype(BF16)
        dhg = (da * hu * (sg * (1.0 + hg * (1.0 - sg)))).astype(BF16)
        p_d = _dot_tn(a, dy)
        p_g = _dot_tn(dhg, h)
        p_u = _dot_tn(dhu, h)
        dh = _dot(dhg, wgt) + _dot(dhu, wut)

        @pl.when(i == 0)
        def _():
            ad_scr[...] = p_d
            ag_scr[...] = p_g
            au_scr[...] = p_u

        @pl.when(i > 0)
        def _():
            ad_scr[...] += p_d
            ag_scr[...] += p_g
            au_scr[...] += p_u

        @pl.when(i == nt - 1)
        def _():
            dwd_ref[...] = ad_scr[...].astype(BF16)
            dwgt_ref[...] = ag_scr[...].astype(BF16)
            dwut_ref[...] = au_scr[...].astype(BF16)

        @pl.when(k == 0)
        def _():
            dh_scr[rows, :] = dh

        @pl.when(k > 0)
        def _():
            dh_scr[rows, :] += dh

        @pl.when(jnp.logical_and(k == 0, i == 0))
        def _():
            dnw_ref[...] = jnp.zeros_like(dnw_ref)

        @pl.when(k == N_FF_TILE - 1)
        def _():
            xh, r = _rms_stats(x_ref[...])
            dx, dw = _rms_bwd(xh, r, nw_v, dh_scr[rows, :])
            dx_ref[...] = dxo_ref[...] + dx
            dnw_ref[...] += dw

    last = N_FF_TILE - 1
    w_spec = pl.BlockSpec((FF_TILE, D_MODEL), lambda k, i: (k, 0))
    return _call(
        body, name=name, grid=(N_FF_TILE, nt),
        in_specs=[pl.BlockSpec((t, D_MODEL), lambda k, i: (i, 0)),
                  pl.BlockSpec((t, D_MODEL), lambda k, i: (i, 0)),
                  pl.BlockSpec((1, D_MODEL), lambda k, i: (0, 0)), w_spec, w_spec, w_spec],
        out_specs=[pl.BlockSpec((t, D_MODEL), lambda k, i: (jnp.where(k == last, i, 0), 0)),
                   w_spec, w_spec, w_spec, pl.BlockSpec((1, D_MODEL), lambda k, i: (0, 0))],
        out_shape=[SDS((SEQ, D_MODEL), F32), SDS((D_FF, D_MODEL), BF16), SDS((D_FF, D_MODEL), BF16),
                   SDS((D_FF, D_MODEL), BF16), SDS((1, D_MODEL), F32)],
        scratch_shapes=[pltpu.VMEM((SEQ, D_MODEL), F32), pltpu.VMEM((FF_TILE, D_MODEL), F32),
                        pltpu.VMEM((FF_TILE, D_MODEL), F32), pltpu.VMEM((FF_TILE, D_MODEL), F32),
                        pltpu.VMEM((SEQ, D_MODEL), BF16)],
        sem=("arbitrary", "arbitrary"), args=(x, dxo, nw, wgt, wut, wd), exch=exch)


def loss_head(x, fw, target, name):
    t = 512

    def body(x_ref, fw_ref, tg_ref, loss_ref, dx_ref, dfw_ref):
        i = pl.program_id(0)
        xh, r = _rms_stats(x_ref[...])
        w = fw_ref[...]
        err = xh * w - tg_ref[...]
        part = 0.5 * jnp.sum(jnp.sum(err * err, axis=-1, keepdims=True), axis=0, keepdims=True) / D_MODEL
        dx, dw = _rms_bwd(xh, r, w, err * (1.0 / D_MODEL))
        dx_ref[...] = dx

        @pl.when(i == 0)
        def _():
            loss_ref[...] = jnp.zeros_like(loss_ref)
            dfw_ref[...] = jnp.zeros_like(dfw_ref)

        loss_ref[...] += jnp.broadcast_to(part, loss_ref.shape)
        dfw_ref[...] += dw

    return pl.pallas_call(
        body, name=name, grid=(SEQ // t,),
        in_specs=[pl.BlockSpec((t, D_MODEL), lambda i: (i, 0)),
                  pl.BlockSpec((1, D_MODEL), lambda i: (0, 0)),
                  pl.BlockSpec((t, D_MODEL), lambda i: (i, 0))],
        out_specs=[pl.BlockSpec((1, 128), lambda i: (0, 0)),
                   pl.BlockSpec((t, D_MODEL), lambda i: (i, 0)),
                   pl.BlockSpec((1, D_MODEL), lambda i: (0, 0))],
        out_shape=[SDS((1, 128), F32), SDS((SEQ, D_MODEL), F32), SDS((1, D_MODEL), F32)],
        compiler_params=_cparams(("arbitrary",)),
    )(x, fw, target)


MIX_T = 512


def _slabs_load(ref, first, n):
    return jnp.concatenate([ref[first + j] for j in range(n)], axis=1)


def _slabs_store(ref, first, val):
    for j in range(val.shape[1] // 128):
        ref[first + j] = val[:, 128 * j:128 * j + 128]


def _slab_spec(k, t):
    return pl.BlockSpec((k, t, 128), lambda i: (0, i, 0))


def mix_in_fwd(x, nw, wext, cos, sin, name):
    t = MIX_T

    def body(x_ref, nw_ref, w_ref, cos_ref, sin_ref, att_ref, rest_ref):
        xh, _r = _rms_stats(x_ref[...])
        h = (xh * nw_ref[...]).astype(BF16)
        pa = _dot(h, w_ref[:, 0:EXT_ATT])
        c = cos_ref[...]
        s = sin_ref[...]
        _slabs_store(att_ref, 0, pa[:, 0:256] * c + pa[:, 768:1024] * s)
        _slabs_store(att_ref, 2, pa[:, 256:512] * c + pa[:, 1024:1280] * s)
        _slabs_store(att_ref, 4, pa[:, 512:768])
        for j in range(EXT_REST // 256):
            rest_ref[:, 256 * j:256 * j + 256] = _dot(h, w_ref[:, EXT_ATT + 256 * j:EXT_ATT + 256 * j + 256])

    return pl.pallas_call(
        body, name=name, grid=(SEQ // t,),
        in_specs=[pl.BlockSpec((t, D_MODEL), lambda i: (i, 0)),
                  pl.BlockSpec((1, D_MODEL), lambda i: (0, 0)),
                  pl.BlockSpec((D_MODEL, EXT_W), lambda i: (0, 0)),
                  pl.BlockSpec((t, ATT_W), lambda i: (i, 0)),
                  pl.BlockSpec((t, ATT_W), lambda i: (i, 0))],
        out_specs=[_slab_spec(6, t),
                   pl.BlockSpec((t, EXT_REST), lambda i: (i, 0))],
        out_shape=[SDS((6, SEQ, 128), F32), SDS((SEQ, EXT_REST), F32)],
        compiler_params=_cparams(("arbitrary",)),
    )(x, nw, wext, cos, sin)


def assemble_dproj(datts, cos, sin, d_dqkv, dz, dg, dpu, name):
    t = 512

    def body(d1_ref, d4_ref, d16_ref, cos_ref, sin_ref, dqkv_ref, dz_ref, dg_ref, dpu_ref, o_ref):
        da6 = d1_ref[...] + d4_ref[...] + d16_ref[...]
        da = jnp.concatenate([da6[j] for j in range(6)], axis=1)
        c = cos_ref[...]
        s = sin_ref[...]
        dq = da[:, 0:256]
        dk = da[:, 256:512]
        o_ref[:, 0:256] = (dq * c).astype(BF16)
        o_ref[:, 256:512] = (dk * c).astype(BF16)
        o_ref[:, 512:768] = da[:, 512:768].astype(BF16)
        o_ref[:, 768:1024] = (dq * s).astype(BF16)
        o_ref[:, 1024:1280] = (dk * s).astype(BF16)
        b = EXT_ATT
        o_ref[:, b + R_DQKV:b + R_DQKV + 1536] = dqkv_ref[...].astype(BF16)
        o_ref[:, b + R_DZ:b + R_DZ + 512] = dz_ref[...].astype(BF16)
        o_ref[:, b + R_G:b + R_G + GATE_W] = dg_ref[...].astype(BF16)
        o_ref[:, b + R_PU:b + R_PU + 256] = dpu_ref[...].astype(BF16)

    row = lambda w: pl.BlockSpec((t, w), lambda i: (i, 0))
    return pl.pallas_call(
        body, name=name, grid=(SEQ // t,),
        in_specs=[_slab_spec(6, t), _slab_spec(6, t), _slab_spec(6, t),
                  row(256), row(256), row(1536), row(512), row(GATE_W), row(256)],
        out_specs=row(EXT_W),
        out_shape=SDS((SEQ, EXT_W), BF16),
        compiler_params=_cparams(("arbitrary",)),
    )(*datts, cos, sin, d_dqkv, dz, dg, dpu)


def linear_bwd(x, dxo, nw, dy, w, name):
    t = 512
    nb = 768
    n = w.shape[1]
    nt = SEQ // t
    nn = n // nb

    def body(x_ref, dxo_ref, nw_ref, dy_ref, w_ref, dx_ref, dw_ref, dnw_ref, dh_scr, h_scr):
        k = pl.program_id(0)
        i = pl.program_id(1)
        rows = pl.ds(pl.multiple_of(i * t, t), t)
        nw_v = nw_ref[...]

        @pl.when(k == 0)
        def _():
            xh0, _r0 = _rms_stats(x_ref[...])
            h_scr[rows, :] = (xh0 * nw_v).astype(BF16)

        h = h_scr[rows, :]
        dyv = dy_ref[...]
        p_w = _dot_tn(h, dyv)
        dh = _dot_nt(dyv, w_ref[...])

        @pl.when(i == 0)
        def _():
            dw_ref[...] = p_w

        @pl.when(i > 0)
        def _():
            dw_ref[...] += p_w

        @pl.when(k == 0)
        def _():
            dh_scr[rows, :] = dh

        @pl.when(k > 0)
        def _():
            dh_scr[rows, :] += dh

        @pl.when(jnp.logical_and(k == 0, i == 0))
        def _():
            dnw_ref[...] = jnp.zeros_like(dnw_ref)

        @pl.when(k == nn - 1)
        def _():
            xh, r = _rms_stats(x_ref[...])
            dx, dw = _rms_bwd(xh, r, nw_v, dh_scr[rows, :])
            dx_ref[...] = dxo_ref[...] + dx
            dnw_ref[...] += dw

    last = nn - 1
    return pl.pallas_call(
        body, name=name, grid=(nn, nt),
        in_specs=[pl.BlockSpec((t, D_MODEL), lambda k, i: (i, 0)),
                  pl.BlockSpec((t, D_MODEL), lambda k, i: (i, 0)),
                  pl.BlockSpec((1, D_MODEL), lambda k, i: (0, 0)),
                  pl.BlockSpec((t, nb), lambda k, i: (i, k)),
                  pl.BlockSpec((D_MODEL, nb), lambda k, i: (0, k))],
        out_specs=[pl.BlockSpec((t, D_MODEL), lambda k, i: (jnp.where(k == last, i, 0), 0)),
                   pl.BlockSpec((D_MODEL, nb), lambda k, i: (0, k)),
                   pl.BlockSpec((1, D_MODEL), lambda k, i: (0, 0))],
        out_shape=[SDS((SEQ, D_MODEL), F32), SDS((D_MODEL, n), F32), SDS((1, D_MODEL), F32)],
        scratch_shapes=[pltpu.VMEM((SEQ, D_MODEL), F32), pltpu.VMEM((SEQ, D_MODEL), BF16)],
        compiler_params=_cparams(("arbitrary", "arbitrary")),
    )(x, dxo, nw, dy, w)


def _att_masks():
    qi = lax.broadcasted_iota(jnp.int32, (ATT_BLK, ATT_BLK), 0)
    ki = lax.broadcasted_iota(jnp.int32, (ATT_BLK, ATT_BLK), 1)
    return ki <= qi, ki >= qi


NEG = -1e30


N_ATT_BLK = SEQ // ATT_BLK


def _class_rows(i, d):
    per_class = N_ATT_BLK // d
    shift = per_class.bit_length() - 1
    r = i >> shift
    j = i & (per_class - 1)
    span = ATT_BLK * d
    start = r + span * j
    prev = jnp.where(j == 0, start, start - span)
    nxt = jnp.where(j == per_class - 1, start, start + span)

    def rows(s0):
        if d == 1:
            return pl.ds(pl.multiple_of(s0, ATT_BLK), ATT_BLK)
        return pl.ds(s0, ATT_BLK, stride=d)

    return rows(start), rows(prev), rows(nxt), j != 0, j != per_class - 1


def _slab_heads(ref, slab, rows):
    x0 = ref[pl.ds(slab, 1), rows, :][0]
    x1 = ref[pl.ds(slab + 1, 1), rows, :][0]
    return jnp.stack([x0[:, 0:ATT_E], x0[:, ATT_E:], x1[:, 0:ATT_E], x1[:, ATT_E:]], axis=0)


def _put_slab_heads(ref, slab, rows, val):
    ref[pl.ds(slab, 1), rows, :] = jnp.concatenate([val[0], val[1]], axis=1)[None]
    ref[pl.ds(slab + 1, 1), rows, :] = jnp.concatenate([val[2], val[3]], axis=1)[None]


ATT_BLOCKS_PER_STEP = 4


def _resident_call(body, ins, out_slabs, name):
    n_in = len(ins)
    steps = N_ATT_BLK // ATT_BLOCKS_PER_STEP

    def wrapped(*refs):
        hbm_in, hbm_out = refs[:n_in], refs[n_in]
        vm_in, vm_out, sem = refs[n_in + 1:2 * n_in + 1], refs[2 * n_in + 1], refs[2 * n_in + 2]
        i = pl.program_id(0)

        @pl.when(i == 0)
        def _():
            copies = [pltpu.make_async_copy(h, v, sem.at[k]) for k, (h, v) in enumerate(zip(hbm_in, vm_in))]
            for cp in copies:
                cp.start()
            for cp in copies:
                cp.wait()

        for b in range(ATT_BLOCKS_PER_STEP):
            body(ATT_BLOCKS_PER_STEP * i + b, *vm_in, vm_out)

        @pl.when(i == steps - 1)
        def _():
            cp = pltpu.make_async_copy(vm_out, hbm_out, sem.at[n_in])
            cp.start()
            cp.wait()

    return pl.pallas_call(
        wrapped, name=name, grid=(steps,),
        in_specs=[ANY_SPEC] * n_in, out_specs=ANY_SPEC, out_shape=SDS((out_slabs, SEQ, 128), F32),
        scratch_shapes=[pltpu.VMEM(a.shape, a.dtype) for a in ins] + [pltpu.VMEM((out_slabs, SEQ, 128), F32),
                                                                      pltpu.SemaphoreType.DMA((n_in + 1,))],
        compiler_params=_cparams(("arbitrary",)),
    )(*ins)


def _att_fwd_math(ld, has_prev):
    m_d, m_p = _att_masks()
    m_p = jnp.logical_and(m_p, has_prev)
    q = ld("att", 0, "cur").astype(BF16)
    kc = ld("att", 2, "cur").astype(BF16)
    vc = ld("att", 4, "cur").astype(BF16)
    kp = ld("att", 2, "prev").astype(BF16)
    vp = ld("att", 4, "prev").astype(BF16)
    sd = jnp.where(m_d, _bdot(q, kc, 2, 2) * 0.125, NEG)
    sp = jnp.where(m_p, _bdot(q, kp, 2, 2) * 0.125, NEG)
    m = jnp.maximum(jnp.max(sd, axis=-1, keepdims=True), jnp.max(sp, axis=-1, keepdims=True))
    pd = jnp.exp(sd - m)
    pp = jnp.exp(sp - m)
    den = jnp.sum(pd, axis=-1, keepdims=True) + jnp.sum(pp, axis=-1, keepdims=True)
    inv = 1.0 / den
    o = _bdot((pd * inv).astype(BF16), vc, 2, 1) + _bdot((pp * inv).astype(BF16), vp, 2, 1)
    return o, jnp.broadcast_to(m + jnp.log(den), (4, ATT_BLK, ATT_E))


def _att_bwd_math(ld, has_prev, has_next):
    m_d, m_band = _att_masks()
    m_p = jnp.logical_and(m_band, has_prev)
    m_n = jnp.logical_and(m_band, has_next)

    def pair(q, k, v, lse, do, dterm, mask):
        s = jnp.where(mask, _bdot(q, k, 2, 2) * 0.125, NEG)
        p = jnp.exp(s - lse)
        dp = _bdot(do, v, 2, 2)
        ds = (p * (dp + dterm) * 0.125).astype(BF16)
        return p.astype(BF16), ds

    q_c = ld("att", 0, "cur").astype(BF16)
    k_c = ld("att", 2, "cur").astype(BF16)
    v_c = ld("att", 4, "cur").astype(BF16)
    k_p = ld("att", 2, "prev").astype(BF16)
    v_p = ld("att", 4, "prev").astype(BF16)
    q_n = ld("att", 0, "next").astype(BF16)
    o_c = ld("ol", 0, "cur")
    o_n = ld("ol", 0, "next")
    lse_c = ld("ol", 2, "cur")[:, :, 0:1]
    lse_n = ld("ol", 2, "next")[:, :, 0:1]
    do_c = ld("dol", 0, "cur")
    do_n = ld("dol", 0, "next")
    t_c = ld("dol", 2, "cur")[:, :, 0:1] - jnp.sum(do_c * o_c, axis=-1, keepdims=True)
    t_n = ld("dol", 2, "next")[:, :, 0:1] - jnp.sum(do_n * o_n, axis=-1, keepdims=True)
    do_cb = do_c.astype(BF16)
    do_nb = do_n.astype(BF16)
    p1, ds1 = pair(q_c, k_c, v_c, lse_c, do_cb, t_c, m_d)
    _p2, ds2 = pair(q_c, k_p, v_p, lse_c, do_cb, t_c, m_p)
    p3, ds3 = pair(q_n, k_c, v_c, lse_n, do_nb, t_n, m_n)
    return (_bdot(ds1, k_c, 2, 1) + _bdot(ds2, k_p, 2, 1), _bdot(ds1, q_c, 1, 1) + _bdot(ds3, q_n, 1, 1),
            _bdot(p1, do_cb, 1, 1) + _bdot(p3, do_nb, 1, 1))


ROWS_A = pl.ds(0, ATT_BLK)
ROWS_B = pl.ds(ATT_BLK, ATT_BLK)
N_ATT_PAIR = N_ATT_BLK // 2


def _pair_spec(k):
    return pl.BlockSpec((k, 2 * ATT_BLK, 128), lambda i: (0, i, 0))


def _before_pair_spec(k):
    return pl.BlockSpec((k, ATT_BLK, 128), lambda i: (0, jnp.maximum(2 * i - 1, 0), 0))


def _after_pair_spec(k):
    return pl.BlockSpec((k, ATT_BLK, 128), lambda i: (0, jnp.minimum(2 * i + 2, N_ATT_BLK - 1), 0))


def att_fwd_s(att, d, name):
    if d == 1:
        def body1(cur_ref, prev_ref, o_ref):
            i = pl.program_id(0)
            for rows, views, has_prev in (
                    (ROWS_A, {"cur": (cur_ref, ROWS_A), "prev": (prev_ref, ROWS_A)}, i != 0),
                    (ROWS_B, {"cur": (cur_ref, ROWS_B), "prev": (cur_ref, ROWS_A)}, True)):
                o, lse = _att_fwd_math(lambda _a, slab, where, v=views: _slab_heads(v[where][0], slab, v[where][1]), has_prev)
                _put_slab_heads(o_ref, 0, rows, o)
                _put_slab_heads(o_ref, 2, rows, lse)

        return pl.pallas_call(
            body1, name=name, grid=(N_ATT_PAIR,),
            in_specs=[_pair_spec(6), _before_pair_spec(6)], out_specs=_pair_spec(4),
            out_shape=SDS((4, SEQ, 128), F32), compiler_params=_cparams(("arbitrary",)),
        )(att, att)

    def body(i, att_ref, o_ref):
        cur, prev, _nxt, has_prev, _has_next = _class_rows(i, d)
        rows = {"cur": cur, "prev": prev}
        o, lse = _att_fwd_math(lambda _a, slab, where: _slab_heads(att_ref, slab, rows[where]), has_prev)
        _put_slab_heads(o_ref, 0, cur, o)
        _put_slab_heads(o_ref, 2, cur, lse)

    return _resident_call(body, [att], 4, name)


def att_bwd_s(att, ol, dol, d, name):
    if d == 1:
        def body1(a_p, a_c, a_n, ol_c, ol_n, dol_c, dol_n, d_ref):
            i = pl.program_id(0)
            first = {("att", "prev"): (a_p, ROWS_A), ("att", "cur"): (a_c, ROWS_A), ("att", "next"): (a_c, ROWS_B),
                     ("ol", "cur"): (ol_c, ROWS_A), ("ol", "next"): (ol_c, ROWS_B),
                     ("dol", "cur"): (dol_c, ROWS_A), ("dol", "next"): (dol_c, ROWS_B)}
            second = {("att", "prev"): (a_c, ROWS_A), ("att", "cur"): (a_c, ROWS_B), ("att", "next"): (a_n, ROWS_A),
                      ("ol", "cur"): (ol_c, ROWS_B), ("ol", "next"): (ol_n, ROWS_A),
                      ("dol", "cur"): (dol_c, ROWS_B), ("dol", "next"): (dol_n, ROWS_A)}
            for rows, views, has_prev, has_next in ((ROWS_A, first, i != 0, True),
                                                    (ROWS_B, second, True, i != N_ATT_PAIR - 1)):
                dq, dk, dv = _att_bwd_math(
                    lambda a, slab, where, v=views: _slab_heads(v[(a, where)][0], slab, v[(a, where)][1]), has_prev, has_next)
                _put_slab_heads(d_ref, 0, rows, dq)
                _put_slab_heads(d_ref, 2, rows, dk)
                _put_slab_heads(d_ref, 4, rows, dv)

        return pl.pallas_call(
            body1, name=name, grid=(N_ATT_PAIR,),
            in_specs=[_before_pair_spec(6), _pair_spec(6), _after_pair_spec(6), _pair_spec(4), _after_pair_spec(4),
                      _pair_spec(4), _after_pair_spec(4)],
            out_specs=_pair_spec(6), out_shape=SDS((6, SEQ, 128), F32), compiler_params=_cparams(("arbitrary",)),
        )(att, att, att, ol, ol, dol, dol)

    def body(i, att_ref, ol_ref, dol_ref, d_ref):
        cur, prev, nxt, has_prev, has_next = _class_rows(i, d)
        rows = {"cur": cur, "prev": prev, "next": nxt}
        refs = {"att": att_ref, "ol": ol_ref, "dol": dol_ref}
        dq, dk, dv = _att_bwd_math(lambda a, slab, where: _slab_heads(refs[a], slab, rows[where]), has_prev, has_next)
        _put_slab_heads(d_ref, 0, cur, dq)
        _put_slab_heads(d_ref, 2, cur, dk)
        _put_slab_heads(d_ref, 4, cur, dv)

    return _resident_call(body, [att, ol, dol], 6, name)


def _shift_down(x, k):
    rows = lax.broadcasted_iota(jnp.int32, x.shape, 0)
    return jnp.where(rows >= k, pltpu.roll(x, k, 0), 0.0)


def _shift_up(x, k):
    n = x.shape[0]
    rows = lax.broadcasted_iota(jnp.int32, x.shape, 0)
    return jnp.where(rows < n - k, pltpu.roll(x, n - k, 0), 0.0)


@functools.partial(jax.custom_vjp, nondiff_argnums=(1,))
def _delay(x, k):
    return _shift_down(x, k)


def _delay_fwd(x, k):
    return _shift_down(x, k), None


def _delay_bwd(k, _res, g):
    return (_shift_up(g, k),)


_delay.defvjp(_delay_fwd, _delay_bwd)

DN_CONV = 4


def _dn_prep_fn(u, w, kind):
    y = w[DN_CONV - 1:DN_CONV] * u
    for j in range(DN_CONV - 1):
        y = y + w[j:j + 1] * _delay(u, DN_CONV - 1 - j)
    y = y * _sigmoid(y)
    nrm = y * lax.rsqrt(jnp.sum(y * y, axis=-1, keepdims=True) + EPS)
    return jnp.where(kind == 0, nrm * (DN_E ** -0.5), jnp.where(kind == 1, nrm, y))


def dn_prep_fwd(rest, conv_w, name):
    def body(u_ref, w_ref, o_ref):
        j = pl.program_id(0)
        kind = (j >= DN_H).astype(jnp.int32) + (j >= 2 * DN_H).astype(jnp.int32)
        o_ref[...] = _dn_prep_fn(u_ref[...], w_ref[...], kind)

    return pl.pallas_call(
        body, name=name, grid=(3 * DN_H,),
        in_specs=[pl.BlockSpec((SEQ, DN_E), lambda j: (0, j)),
                  pl.BlockSpec((DN_CONV, DN_E), lambda j: (0, j))],
        out_specs=pl.BlockSpec((SEQ, DN_E), lambda j: (0, j)),
        out_shape=SDS((SEQ, 3 * DN_W), F32),
        compiler_params=_cparams(("arbitrary",)),
    )(rest, conv_w)


def dn_prep_bwd(rest, conv_w, dqkv, name):
    def body(u_ref, w_ref, g_ref, du_ref, dw_ref):
        j = pl.program_id(0)
        kind = (j >= DN_H).astype(jnp.int32) + (j >= 2 * DN_H).astype(jnp.int32)
        _y, vjp = jax.vjp(lambda u, w: _dn_prep_fn(u, w, kind), u_ref[...], w_ref[...])
        du, dw = vjp(g_ref[...])
        du_ref[...] = du
        dw_ref[...] = dw

    return pl.pallas_call(
        body, name=name, grid=(3 * DN_H,),
        in_specs=[pl.BlockSpec((SEQ, DN_E), lambda j: (0, j)),
                  pl.BlockSpec((DN_CONV, DN_E), lambda j: (0, j)),
                  pl.BlockSpec((SEQ, DN_E), lambda j: (0, j))],
        out_specs=[pl.BlockSpec((SEQ, DN_E), lambda j: (0, j)),
                   pl.BlockSpec((DN_CONV, DN_E), lambda j: (0, j))],
        out_shape=[SDS((SEQ, 3 * DN_W), F32), SDS((DN_CONV, 3 * DN_W), F32)],
        compiler_params=_cparams(("arbitrary",)),
    )(rest, conv_w, dqkv)


def _bdot(a, b, ca, cb, prec=None):
    return lax.dot_general(a, b, (((ca,), (cb,)), ((0,), (0,))), preferred_element_type=F32, precision=prec)


def _unit_lower_inverse(a):
    eye = (lax.broadcasted_iota(jnp.int32, (DN_C, DN_C), 0) == lax.broadcasted_iota(jnp.int32, (DN_C, DN_C), 1)).astype(F32)
    p = eye - a
    b = _bdot(a, a, 2, 1, INV_PREC)
    for lvl in range(5):
        p = p + _bdot(p, b, 2, 1, INV_PREC)
        if lvl < 4:
            b = _bdot(b, b, 2, 1, INV_PREC)
    return p


@jax.custom_vjp
def _tri_inv(a):
    return _unit_lower_inverse(a)


def _tri_inv_fwd(a):
    t = _unit_lower_inverse(a)
    return t, t


def _tri_inv_bwd(t, g):
    return (-_bdot(_bdot(t, g, 1, 1, INV_PREC), t, 2, 2, INV_PREC),)


_tri_inv.defvjp(_tri_inv_fwd, _tri_inv_bwd)


def _b16(x):
    return x.astype(BF16)


def _heads(ref, base=0):
    return jnp.stack([ref[:, base + DN_E * hd:base + DN_E * hd + DN_E] for hd in range(DN_H)], axis=0)


def _put_heads(ref, val, base=0):
    for hd in range(DN_H):
        ref[:, base + DN_E * hd:base + DN_E * hd + DN_E] = val[hd]


DN_G_LOG2 = 3
DN_G = 1 << DN_G_LOG2
N_INST = DN_G * DN_H


def _dn_intra(q, k, v, bb, ab, alog, dtb):
    ri = lax.broadcasted_iota(jnp.int32, (DN_C, DN_C), 0)
    ci = lax.broadcasted_iota(jnp.int32, (DN_C, DN_C), 1)
    lower = ri >= ci
    strict = ri > ci
    nh = q.shape[0]
    beta = _sigmoid(bb)
    xg = ab + dtb
    softplus = jnp.maximum(xg, 0.0) + jnp.log(1.0 + jnp.exp(-jnp.abs(xg)))
    gi = -jnp.exp(alog) * softplus
    g = _bdot(jnp.broadcast_to(lower.astype(F32), (nh, DN_C, DN_C)), gi, 2, 1, HI)
    eg = jnp.exp(g)
    kb = k * beta
    vb = v * beta
    g_col = g[:, :, 0:DN_C]
    g_row = _bdot(jnp.full((nh, DN_C, DN_E), 1.0 / DN_E, F32), g, 2, 2, HI)
    decay = jnp.where(lower, jnp.exp(jnp.where(lower, g_col - g_row, 0.0)), 0.0)
    kbf = _b16(k)
    a = jnp.where(strict, _bdot(_b16(kb), kbf, 2, 2) * decay, 0.0)
    tb = _b16(_tri_inv(a))
    u = _bdot(tb, _b16(vb), 2, 1)
    w = _bdot(tb, _b16(kb * eg), 2, 1)
    intra = jnp.where(lower, _bdot(_b16(q), kbf, 2, 2) * decay, 0.0)
    g_last = g[:, DN_C - 1:DN_C, :]
    return u, w, q * eg, k * jnp.exp(g_last - g), intra, jnp.exp(g_last)


def _dn_inter(u, w, qg, kdec, intra, egl, state):
    sb = _b16(state)
    v_new = u - _bdot(_b16(w), sb, 2, 1)
    o = _bdot(_b16(qg), sb, 2, 1) + _bdot(_b16(intra), _b16(v_new), 2, 1)
    return o, state * egl + _bdot(_b16(kdec), _b16(v_new), 1, 1)


def _inst(ref, base=0):
    per_head = [ref[:, base + DN_E * hd:base + DN_E * hd + DN_E].reshape(DN_G, DN_C, DN_E) for hd in range(DN_H)]
    return jnp.concatenate(per_head, axis=0)


def _inst_rows(ref):
    rows = [jnp.broadcast_to(ref[:, DN_E * hd:DN_E * hd + DN_E][None], (DN_G, 1, DN_E)) for hd in range(DN_H)]
    return jnp.concatenate(rows, axis=0)


def _put_inst(ref, val, width=DN_E, base=0):
    for hd in range(DN_H):
        ref[:, base + width * hd:base + width * hd + width] = val[DN_G * hd:DN_G * hd + DN_G].reshape(DN_G * DN_C, width)


@jax.custom_vjp
def _spread_gates(gates):
    t = gates.shape[0]
    return jnp.concatenate([jnp.broadcast_to(gates[:, j:j + 1], (t, DN_E)) for j in range(2 * DN_H)], axis=1)


def _spread_gates_fwd(gates):
    return _spread_gates(gates), None


def _spread_gates_bwd(_res, g):
    t = g.shape[0]
    lane = lax.broadcasted_iota(jnp.int32, (t, GATE_W), 1)
    out = jnp.zeros((t, GATE_W), F32)
    for j in range(2 * DN_H):
        s = jnp.sum(g[:, DN_E * j:DN_E * j + DN_E], axis=-1, keepdims=True)
        out = jnp.where(lane == j, s, out)
    return (out,)


_spread_gates.defvjp(_spread_gates_fwd, _spread_gates_bwd)


def _dn_intra_from_gates(q, k, v, gates, alog, dtb):
    wide = _spread_gates(gates)
    inst = lambda base: jnp.concatenate(
        [wide[:, base + DN_E * hd:base + DN_E * hd + DN_E].reshape(DN_G, DN_C, DN_E) for hd in range(DN_H)], axis=0)
    return _dn_intra(q, k, v, inst(0), inst(DN_W), alog, dtb)


def _intra_args(qkv_ref, g_ref, alog_ref, dtb_ref):
    return (_inst(qkv_ref), _inst(qkv_ref, DN_W), _inst(qkv_ref, 2 * DN_W), g_ref[...],
            _inst_rows(alog_ref), _inst_rows(dtb_ref))


def _intra_in_specs():
    t = DN_G * DN_C
    return [pl.BlockSpec((t, 3 * DN_W), lambda n: (n, 0)),
            pl.BlockSpec((t, GATE_W), lambda n: (n, R_G // GATE_W)),
            pl.BlockSpec((1, DN_W), lambda n: (0, 0)),
            pl.BlockSpec((1, DN_W), lambda n: (0, 0))]


def dn_intra_fwd(qkv, rest, alog_b, dtb_b, name, exch=None):
    t = DN_G * DN_C

    def body(qkv_ref, g_ref, alog_ref, dtb_ref, u_ref, w_ref, qg_ref, kd_ref, in_ref, egl_ref):
        u, w, qg, kdec, intra, egl = _dn_intra_from_gates(*_intra_args(qkv_ref, g_ref, alog_ref, dtb_ref))
        _put_inst(u_ref, u)
        _put_inst(w_ref, w.astype(BF16))
        _put_inst(qg_ref, qg.astype(BF16))
        _put_inst(kd_ref, kdec.astype(BF16))
        _put_inst(in_ref, intra.astype(BF16), DN_C)
        for hd in range(DN_H):
            egl_ref[:, DN_E * hd:DN_E * hd + DN_E] = egl[DN_G * hd:DN_G * hd + DN_G].reshape(DN_G, DN_E)

    row = lambda w_: pl.BlockSpec((t, w_), lambda n: (n, 0))
    return _call(
        body, name=name, grid=(N_CHUNK // DN_G,), in_specs=_intra_in_specs(),
        out_specs=[row(DN_W), row(DN_W), row(DN_W), row(DN_W), row(DN_H * DN_C),
                   pl.BlockSpec((DN_G, DN_W), lambda n: (n, 0))],
        out_shape=[SDS((SEQ, DN_W), F32), SDS((SEQ, DN_W), BF16), SDS((SEQ, DN_W), BF16), SDS((SEQ, DN_W), BF16),
                   SDS((SEQ, DN_H * DN_C), BF16), SDS((N_CHUNK, DN_W), F32)],
        scratch_shapes=[], sem=("arbitrary",), args=(qkv, rest, alog_b, dtb_b), exch=exch)


def dn_intra_bwd(qkv, rest, alog_b, dtb_b, du, dw, dqg, dkd, dintra, degl, name):
    t = DN_G * DN_C

    def body(qkv_ref, g_ref, alog_ref, dtb_ref, du_ref, dw_ref, dqg_ref, dkd_ref, din_ref, degl_ref,
             dqkv_ref, dg_ref, dalog_ref, ddtb_ref):
        @pl.when(pl.program_id(0) == 0)
        def _():
            dalog_ref[...] = jnp.zeros_like(dalog_ref)
            ddtb_ref[...] = jnp.zeros_like(ddtb_ref)

        _out, vjp = jax.vjp(_dn_intra_from_gates, *_intra_args(qkv_ref, g_ref, alog_ref, dtb_ref))
        d_in = jnp.concatenate([din_ref[:, DN_C * hd:DN_C * hd + DN_C].reshape(DN_G, DN_C, DN_C) for hd in range(DN_H)], axis=0)
        d_egl = jnp.concatenate([degl_ref[:, DN_E * hd:DN_E * hd + DN_E].reshape(DN_G, 1, DN_E) for hd in range(DN_H)], axis=0)
        dq, dk, dv, dg, dalog, ddtb = vjp((_inst(du_ref), _inst(dw_ref), _inst(dqg_ref), _inst(dkd_ref), d_in, d_egl))
        _put_inst(dqkv_ref, dq)
        _put_inst(dqkv_ref, dk, DN_E, DN_W)
        _put_inst(dqkv_ref, dv, DN_E, 2 * DN_W)
        dg_ref[...] = dg
        for hd in range(DN_H):
            sl = slice(DN_E * hd, DN_E * hd + DN_E)
            dalog_ref[:, sl] += jnp.sum(dalog[DN_G * hd:DN_G * hd + DN_G], axis=0)
            ddtb_ref[:, sl] += jnp.sum(ddtb[DN_G * hd:DN_G * hd + DN_G], axis=0)

    row = lambda w_: pl.BlockSpec((t, w_), lambda n: (n, 0))
    acc = pl.BlockSpec((1, DN_W), lambda n: (0, 0))
    return pl.pallas_call(
        body, name=name, grid=(N_CHUNK // DN_G,),
        in_specs=_intra_in_specs() + [row(DN_W), row(DN_W), row(DN_W), row(DN_W), row(DN_H * DN_C),
                                      pl.BlockSpec((DN_G, DN_W), lambda n: (n, 0))],
        out_specs=[row(3 * DN_W), row(GATE_W), acc, acc],
        out_shape=[SDS((SEQ, 3 * DN_W), F32), SDS((SEQ, GATE_W), F32), SDS((1, DN_W), F32), SDS((1, DN_W), F32)],
        compiler_params=_cparams(("arbitrary",)),
    )(qkv, rest, alog_b, dtb_b, du, dw, dqg, dkd, dintra, degl)


DN_RUN_LOG2 = 2
DN_RUN = 1 << DN_RUN_LOG2


def _chunk_rows(ref, c):
    return ref.at[pl.ds(DN_C * c, DN_C), :]


def _inter_args(u_ref, w_ref, qg_ref, kd_ref, in_ref, egl_ref, n, state):
    f = lambda r: _heads(r).astype(F32)
    intra = jnp.stack([in_ref[:, DN_C * hd:DN_C * hd + DN_C] for hd in range(DN_H)], axis=0).astype(F32)
    egl = _heads(egl_ref.at[pl.ds(n & (DN_G - 1), 1), :])
    return f(u_ref), f(w_ref), f(qg_ref), f(kd_ref), intra, egl, state


def dn_inter_fwd(u, w, qg, kdec, intra, egl, name, exch=None):
    def body(u_ref, w_ref, qg_ref, kd_ref, in_ref, egl_ref, o_ref, st_ref, state_scr):
        n2 = pl.program_id(0)

        @pl.when(n2 == 0)
        def _():
            state_scr[...] = jnp.zeros_like(state_scr)

        for c in range(DN_RUN):
            v = functools.partial(_chunk_rows, c=c)
            st = state_scr[...]
            st_ref[c] = st
            o, ns = _dn_inter(*_inter_args(v(u_ref), v(w_ref), v(qg_ref), v(kd_ref), v(in_ref), egl_ref,
                                           DN_RUN * n2 + c, st))
            _put_heads(v(o_ref), o)
            state_scr[...] = ns

    row = lambda w_: pl.BlockSpec((DN_RUN * DN_C, w_), lambda n: (n, 0))
    return _call(
        body, name=name, grid=(N_CHUNK // DN_RUN,),
        in_specs=[row(DN_W), row(DN_W), row(DN_W), row(DN_W), row(DN_H * DN_C),
                  pl.BlockSpec((DN_G, DN_W), lambda n: (n >> (DN_G_LOG2 - DN_RUN_LOG2), 0))],
        out_specs=[row(DN_W), pl.BlockSpec((DN_RUN, DN_H, DN_E, DN_E), lambda n: (n, 0, 0, 0))],
        out_shape=[SDS((SEQ, DN_W), F32), SDS((N_CHUNK, DN_H, DN_E, DN_E), F32)],
        scratch_shapes=[pltpu.VMEM((DN_H, DN_E, DN_E), F32)],
        sem=("arbitrary",), args=(u, w, qg, kdec, intra, egl), exch=exch)


def dn_inter_bwd(u, w, qg, kdec, intra, egl, states, do, name):
    last = N_CHUNK // DN_RUN - 1

    def body(u_ref, w_ref, qg_ref, kd_ref, in_ref, egl_ref, st_ref, do_ref,
             du_ref, dw_ref, dqg_ref, dkd_ref, din_ref, degl_ref, dstate_scr):
        s = pl.program_id(0)

        @pl.when(s == 0)
        def _():
            dstate_scr[...] = jnp.zeros_like(dstate_scr)

        for c in reversed(range(DN_RUN)):
            n = DN_RUN * (last - s) + c
            v = functools.partial(_chunk_rows, c=c)
            _out, vjp = jax.vjp(_dn_inter, *_inter_args(v(u_ref), v(w_ref), v(qg_ref), v(kd_ref), v(in_ref), egl_ref,
                                                        n, st_ref[c]))
            du, dw, dqg, dkd, din, degl, dst = vjp((_heads(v(do_ref)), dstate_scr[...]))
            _put_heads(v(du_ref), du)
            _put_heads(v(dw_ref), dw)
            _put_heads(v(dqg_ref), dqg)
            _put_heads(v(dkd_ref), dkd)
            for hd in range(DN_H):
                v(din_ref)[:, DN_C * hd:DN_C * hd + DN_C] = din[hd]
            row = n & (DN_G - 1)

            @pl.when(row == DN_G - 1)
            def _():
                degl_ref[...] = jnp.zeros_like(degl_ref)

            new_row = jnp.concatenate([degl[hd] for hd in range(DN_H)], axis=1)
            rows = lax.broadcasted_iota(jnp.int32, (DN_G, DN_W), 0)
            degl_ref[...] = jnp.where(rows == row, jnp.broadcast_to(new_row, (DN_G, DN_W)), degl_ref[...])
            dstate_scr[...] = dst

    rev = lambda w_: pl.BlockSpec((DN_RUN * DN_C, w_), lambda s: (last - s, 0))
    grp = pl.BlockSpec((DN_G, DN_W), lambda s: ((last - s) >> (DN_G_LOG2 - DN_RUN_LOG2), 0))
    return pl.pallas_call(
        body, name=name, grid=(N_CHUNK // DN_RUN,),
        in_specs=[rev(DN_W), rev(DN_W), rev(DN_W), rev(DN_W), rev(DN_H * DN_C), grp,
                  pl.BlockSpec((DN_RUN, DN_H, DN_E, DN_E), lambda s: (last - s, 0, 0, 0)), rev(DN_W)],
        out_specs=[rev(DN_W), rev(DN_W), rev(DN_W), rev(DN_W), rev(DN_H * DN_C), grp],
        out_shape=[SDS((SEQ, DN_W), F32)] * 4 + [SDS((SEQ, DN_H * DN_C), F32), SDS((N_CHUNK, DN_W), F32)],
        scratch_shapes=[pltpu.VMEM((DN_H, DN_E, DN_E), F32)],
        compiler_params=_cparams(("arbitrary",)),
    )(u, w, qg, kdec, intra, egl, states, do)


OUT_T = 512


def _pool_consts(rows_total, t0, halo_before):
    lane = lax.broadcasted_iota(jnp.int32, (rows_total, POOL_W), 1)
    row = lax.broadcasted_iota(jnp.int32, (rows_total, POOL_W), 0)
    grp = (lane >= 64).astype(jnp.int32) + (lane >= 128).astype(jnp.int32) + (lane >= 192).astype(jnp.int32)
    win = jnp.where(grp == 0, 2, jnp.where(grp == 1, 4, jnp.where(grp == 2, 8, 16)))
    pos = t0 + row - halo_before
    cnt = jnp.minimum(pos + 1, win).astype(F32)
    return grp, cnt


def _pool_select(grp, s2, s4, s8, s16):
    return jnp.where(grp == 0, s2, jnp.where(grp == 1, s4, jnp.where(grp == 2, s8, s16)))


def _pooled(u_ext, t0):
    n = u_ext.shape[0]
    grp, cnt = _pool_consts(n, t0, POOL_HALO)
    s2 = u_ext + pltpu.roll(u_ext, 1, 0)
    s4 = s2 + pltpu.roll(s2, 2, 0)
    s8 = s4 + pltpu.roll(s4, 4, 0)
    s16 = s8 + pltpu.roll(s8, 8, 0)
    out = _pool_select(grp, s2, s4, s8, s16) / jnp.maximum(cnt, 1.0) - u_ext
    return out[POOL_HALO:, :]


def _merge_weights(l1, l4, l16):
    m = jnp.maximum(jnp.maximum(l1, l4), l16)
    e1 = jnp.exp(l1 - m)
    e4 = jnp.exp(l4 - m)
    e16 = jnp.exp(l16 - m)
    inv = 1.0 / (e1 + e4 + e16)
    return e1 * inv, e4 * inv, e16 * inv


def _out_parts(ol1_ref, ol4_ref, ol16_ref, pu_ref, puh_ref, odn_ref, z_ref, wbd_ref, i, t):
    w1, w4, w16 = _merge_weights(_slabs_load(ol1_ref, 2, 2), _slabs_load(ol4_ref, 2, 2), _slabs_load(ol16_ref, 2, 2))
    ya = w1 * _slabs_load(ol1_ref, 0, 2) + w4 * _slabs_load(ol4_ref, 0, 2) + w16 * _slabs_load(ol16_ref, 0, 2)
    halo = jnp.where(i > 0, puh_ref[...], 0.0)
    pooled = _pooled(jnp.concatenate([halo, pu_ref[...]], axis=0), i * t)
    pw = _dot(pooled.astype(BF16), wbd_ref[...])
    return ya, pooled, pw, (w1, w4, w16)


def _out_specs_common(t):
    def row(w, cb=0):
        return pl.BlockSpec((t, w), lambda i: (i, cb))

    halo = pl.BlockSpec((POOL_HALO, POOL_W),
                        lambda i: (jnp.maximum(i * (t // POOL_HALO) - 1, 0), R_PU // POOL_W))
    full = lambda a, b: pl.BlockSpec((a, b), lambda i: (0, 0))
    return [_slab_spec(4, t), _slab_spec(4, t), _slab_spec(4, t), row(POOL_W, R_PU // POOL_W), halo, row(DN_W), row(DN_W, R_DZ // DN_W),
            full(POOL_W, POOL_W), full(1, POOL_W), full(1, DN_W), full(D_MODEL, D_MODEL)]


def mix_out_fwd(x, ol1, ol4, ol16, rest, odn, wbd, scale, onorm_b, wout, name):
    t = OUT_T

    def body(x_ref, ol1_ref, ol4_ref, ol16_ref, pu_ref, puh_ref, odn_ref, z_ref, wbd_ref, sc_ref, on_ref, wo_ref, o_ref):
        i = pl.program_id(0)
        ya, _pooled_v, pw, _w = _out_parts(ol1_ref, ol4_ref, ol16_ref, pu_ref, puh_ref, odn_ref, z_ref, wbd_ref, i, t)
        yb = pw * sc_ref[...]
        acc = x_ref[...] + _dot(ya.astype(BF16), wo_ref[0:256, :]) + _dot(yb.astype(BF16), wo_ref[256:512, :])
        for hd in range(DN_H):
            sl = slice(DN_E * hd, DN_E * hd + DN_E)
            oh, _r = _rms_stats(odn_ref[:, sl])
            z = z_ref[:, sl]
            yc = oh * on_ref[:, sl] * (z * _sigmoid(z))
            acc = acc + _dot(yc.astype(BF16), wo_ref[512 + DN_E * hd:512 + DN_E * hd + DN_E, :])
        o_ref[...] = acc

    return pl.pallas_call(
        body, name=name, grid=(SEQ // t,),
        in_specs=[pl.BlockSpec((t, D_MODEL), lambda i: (i, 0))] + _out_specs_common(t),
        out_specs=pl.BlockSpec((t, D_MODEL), lambda i: (i, 0)),
        out_shape=SDS((SEQ, D_MODEL), F32),
        compiler_params=_cparams(("arbitrary",)),
    )(x, ol1, ol4, ol16, rest, rest, odn, rest, wbd, scale, onorm_b, wout)


def mix_out_bwd(dxo, ol1, ol4, ol16, rest, odn, wbd, scale, onorm_b, wout, headsum, name):
    t = OUT_T

    def body(dxo_ref, ol1_ref, ol4_ref, ol16_ref, pu_ref, puh_ref, odn_ref, z_ref, wbd_ref, sc_ref, on_ref, wo_ref, hs_ref,
             dwo_ref, d1_ref, d4_ref, d16_ref, dpl_ref, dodn_ref, dz_ref, dsc_ref, don_ref, dwbd_ref):
        i = pl.program_id(0)

        @pl.when(i == 0)
        def _():
            dwo_ref[...] = jnp.zeros_like(dwo_ref)
            dsc_ref[...] = jnp.zeros_like(dsc_ref)
            don_ref[...] = jnp.zeros_like(don_ref)
            dwbd_ref[...] = jnp.zeros_like(dwbd_ref)

        ya, pooled, pw, (w1, w4, w16) = _out_parts(ol1_ref, ol4_ref, ol16_ref, pu_ref, puh_ref, odn_ref, z_ref, wbd_ref, i, t)
        sc = sc_ref[...]
        dxb = dxo_ref[...].astype(BF16)
        dwo_ref[0:256, :] += _dot_tn(ya.astype(BF16), dxb)
        dwo_ref[256:512, :] += _dot_tn((pw * sc).astype(BF16), dxb)
        dya = _dot_nt(dxb, wo_ref[0:256, :])
        o1 = _slabs_load(ol1_ref, 0, 2)
        o4 = _slabs_load(ol4_ref, 0, 2)
        o16 = _slabs_load(ol16_ref, 0, 2)
        hs = hs_ref[...]
        s1 = _dot(dya * o1, hs, HI)
        s4 = _dot(dya * o4, hs, HI)
        s16 = _dot(dya * o16, hs, HI)
        sbar = w1 * s1 + w4 * s4 + w16 * s16
        _slabs_store(d1_ref, 0, w1 * dya)
        _slabs_store(d1_ref, 2, w1 * (s1 - sbar))
        _slabs_store(d4_ref, 0, w4 * dya)
        _slabs_store(d4_ref, 2, w4 * (s4 - sbar))
        _slabs_store(d16_ref, 0, w16 * dya)
        _slabs_store(d16_ref, 2, w16 * (s16 - sbar))
        dyb = _dot_nt(dxb, wo_ref[256:512, :])
        dsc_ref[...] += jnp.sum(dyb * pw, axis=0, keepdims=True)
        dpw = (dyb * sc).astype(BF16)
        dwbd_ref[...] += _dot_tn(pooled.astype(BF16), dpw)
        dpl_ref[...] = _dot_nt(dpw, wbd_ref[...])
        for hd in range(DN_H):
            sl = slice(DN_E * hd, DN_E * hd + DN_E)
            rows_w = slice(512 + DN_E * hd, 512 + DN_E * hd + DN_E)
            oh, r = _rms_stats(odn_ref[:, sl])
            z = z_ref[:, sl]
            sg = _sigmoid(z)
            sz = z * sg
            nw = on_ref[:, sl]
            on = oh * nw
            dwo_ref[rows_w, :] += _dot_tn((on * sz).astype(BF16), dxb)
            dyc = _dot_nt(dxb, wo_ref[rows_w, :])
            dz_ref[:, sl] = dyc * on * (sg * (1.0 + z * (1.0 - sg)))
            dx, dw = _rms_bwd(oh, r, nw, dyc * sz)
            dodn_ref[:, sl] = dx
            don_ref[:, sl] += dw

    row = lambda w: pl.BlockSpec((t, w), lambda i: (i, 0))
    full = lambda a, b: pl.BlockSpec((a, b), lambda i: (0, 0))
    return pl.pallas_call(
        body, name=name, grid=(SEQ // t,),
        in_specs=[row(D_MODEL)] + _out_specs_common(t) + [full(ATT_W, ATT_W)],
        out_specs=[full(D_MODEL, D_MODEL), _slab_spec(4, t), _slab_spec(4, t), _slab_spec(4, t), row(POOL_W), row(DN_W), row(DN_W),
                   full(1, POOL_W), full(1, DN_W), full(POOL_W, POOL_W)],
        out_shape=[SDS((D_MODEL, D_MODEL), F32), SDS((4, SEQ, 128), F32), SDS((4, SEQ, 128), F32), SDS((4, SEQ, 128), F32),
                   SDS((SEQ, POOL_W), F32), SDS((SEQ, DN_W), F32), SDS((SEQ, DN_W), F32),
                   SDS((1, POOL_W), F32), SDS((1, DN_W), F32), SDS((POOL_W, POOL_W), F32)],
        compiler_params=_cparams(("arbitrary",)),
    )(dxo, ol1, ol4, ol16, rest, rest, odn, rest, wbd, scale, onorm_b, wout, headsum)


def pool_bwd(dpooled, name):
    t = 512
    nt = SEQ // t

    def body(d_ref, dn_ref, o_ref):
        i = pl.program_id(0)
        halo = jnp.where(i < nt - 1, dn_ref[...], 0.0)
        d_ext = jnp.concatenate([d_ref[...], halo], axis=0)
        n = t + POOL_HALO
        grp, cnt = _pool_consts(n, i * t, 0)
        dq = d_ext / cnt
        s2 = dq + pltpu.roll(dq, n - 1, 0)
        s4 = s2 + pltpu.roll(s2, n - 2, 0)
        s8 = s4 + pltpu.roll(s4, n - 4, 0)
        s16 = s8 + pltpu.roll(s8, n - 8, 0)
        o_ref[...] = (_pool_select(grp, s2, s4, s8, s16) - d_ext)[0:t, :]

    return pl.pallas_call(
        body, name=name, grid=(nt,),
        in_specs=[pl.BlockSpec((t, POOL_W), lambda i: (i, 0)),
                  pl.BlockSpec((POOL_HALO, POOL_W),
                               lambda i: (jnp.minimum((i + 1) * (t // POOL_HALO), SEQ // POOL_HALO - 1), 0))],
        out_specs=pl.BlockSpec((t, POOL_W), lambda i: (i, 0)),
        out_shape=SDS((SEQ, POOL_W), F32),
        compiler_params=_cparams(("arbitrary",)),
    )(dpooled, dpooled)


N_PEER = N_DEV - 1
ANY_SPEC = pl.BlockSpec(memory_space=pl.ANY)


class Exchange:
    def __init__(self, arrays, mode):
        self.arrays = list(arrays)
        self.mode = mode
        n = len(self.arrays)
        if mode == "scatter":
            self.out_shape = [SDS(a.shape, a.dtype) for a in self.arrays]
        else:
            self.out_shape = [SDS((N_DEV,) + a.shape, a.dtype) for a in self.arrays]
        self.scratch = [pltpu.SemaphoreType.DMA((n * N_PEER,)), pltpu.SemaphoreType.DMA((n * N_PEER,)),
                        pltpu.SemaphoreType.DMA((n,))]

    @staticmethod
    def _place():
        x, y, c = lax.axis_index("x"), lax.axis_index("y"), lax.axis_index("c")
        chips = [(1 - x, y), (x, 1 - y), (1 - x, 1 - y)]
        return x, y, c, chips

    @staticmethod
    def _copy(sems, a, k, src, dst, to):
        send_sems, recv_sems, _ = sems
        return pltpu.make_async_remote_copy(
            src_ref=src, dst_ref=dst, send_sem=send_sems.at[a * N_PEER + k], recv_sem=recv_sems.at[a * N_PEER + k],
            device_id=to, device_id_type=MESH)

    def _scatter_peers(self):
        x, y, c, _ = self._place()
        out = []
        for fx, fy, fc in ((0, 0, 1), (1, 0, 0), (0, 1, 0), (1, 1, 0), (1, 0, 1), (0, 1, 1), (1, 1, 1)):
            px, py, pc = x ^ fx, y ^ fy, c ^ fc
            out.append(((px, py, pc), 4 * px + 2 * py + pc))
        return 4 * x + 2 * y + c, out

    def _local(self, ins, outs, sems, a, me):
        src = ins[a].at[me] if self.mode == "scatter" else ins[a]
        return pltpu.make_async_copy(src, outs[a].at[me], sems[2].at[a])

    def start(self, ins, outs, sems):
        if self.mode == "scatter":
            me, peers = self._scatter_peers()
            for a in range(len(ins)):
                self._local(ins, outs, sems, a, me).start()
                for k, (peer, pidx) in enumerate(peers):
                    self._copy(sems, a, k, ins[a].at[pidx], outs[a].at[me], peer).start()
            return
        x, y, c, chips = self._place()
        me = 4 * x + 2 * y + c
        for a in range(len(ins)):
            self._local(ins, outs, sems, a, me).start()
            self._copy(sems, a, 0, ins[a], outs[a].at[me], (x, y, 1 - c)).start()
            for j, (cx, cy) in enumerate(chips):
                self._copy(sems, a, 1 + j, ins[a], outs[a].at[me], (cx, cy, c)).start()

    def finish(self, ins, outs, sems):
        n = len(ins)
        if self.mode == "scatter":
            me, peers = self._scatter_peers()
            for a in range(n):
                for k, (peer, pidx) in enumerate(peers):
                    self._copy(sems, a, k, ins[a].at[pidx], outs[a].at[pidx], peer).wait_recv()
            for a in range(n):
                for k, (peer, pidx) in enumerate(peers):
                    self._copy(sems, a, k, ins[a].at[pidx], outs[a].at[me], peer).wait_send()
                self._local(ins, outs, sems, a, me).wait()
            return
        x, y, c, chips = self._place()
        me = 4 * x + 2 * y + c
        sib = (x, y, 1 - c)
        for a in range(n):
            for j, (cx, cy) in enumerate(chips):
                blk = outs[a].at[4 * cx + 2 * cy + c]
                self._copy(sems, a, 1 + j, ins[a], blk, (cx, cy, c)).wait_recv()
                self._copy(sems, a, 4 + j, blk, blk, sib).start()
        for a in range(n):
            self._copy(sems, a, 0, ins[a], outs[a].at[4 * x + 2 * y + (1 - c)], sib).wait_recv()
            for j, (cx, cy) in enumerate(chips):
                blk = outs[a].at[4 * cx + 2 * cy + (1 - c)]
                self._copy(sems, a, 4 + j, blk, blk, sib).wait_recv()
        for a in range(n):
            for k in range(N_PEER):
                self._copy(sems, a, k, ins[a], outs[a].at[me], sib).wait_send()
            self._local(ins, outs, sems, a, me).wait()


def run_exchanges(exchs, name):
    counts = [len(e.arrays) for e in exchs]
    n = sum(counts)

    def body(*refs):
        ins, outs, sems = refs[:n], refs[n:2 * n], refs[2 * n:]
        parts, off = [], 0
        for j, c in enumerate(counts):
            parts.append((ins[off:off + c], outs[off:off + c], sems[3 * j:3 * j + 3]))
            off += c
        for e, p in zip(exchs, parts):
            e.start(*p)
        for e, p in zip(exchs, parts):
            e.finish(*p)

    res = pl.pallas_call(
        body, name=name, in_specs=[ANY_SPEC] * n, out_specs=[ANY_SPEC] * n,
        out_shape=[s for e in exchs for s in e.out_shape], scratch_shapes=[s for e in exchs for s in e.scratch],
    )(*[a for e in exchs for a in e.arrays])
    out, off = [], 0
    for c in counts:
        out.append(list(res[off:off + c]))
        off += c
    return out


def run_exchange(exch, name):
    return run_exchanges([exch], name)[0]


def _call(body, *, name, grid, in_specs, out_specs, out_shape, scratch_shapes, sem, args, exch=None):
    if exch is None:
        res = pl.pallas_call(body, name=name, grid=grid, in_specs=in_specs, out_specs=out_specs, out_shape=out_shape,
                             scratch_shapes=scratch_shapes, compiler_params=_cparams(sem))(*args)
        return res, None
    single = not isinstance(out_shape, (list, tuple))
    out_specs_l = [out_specs] if single else list(out_specs)
    out_shape_l = [out_shape] if single else list(out_shape)
    n_in, n_out, n_scr, m = len(in_specs), len(out_specs_l), len(scratch_shapes), len(exch.arrays)

    def wrapped(*refs):
        p = 0
        ins = refs[p:p + n_in]; p += n_in
        xin = refs[p:p + m]; p += m
        outs = refs[p:p + n_out]; p += n_out
        xout = refs[p:p + m]; p += m
        scr = refs[p:p + n_scr]; p += n_scr
        sems = refs[p:]
        ids = [pl.program_id(ax) for ax in range(len(grid))]
        first = functools.reduce(jnp.logical_and, [i == 0 for i in ids])
        last = functools.reduce(jnp.logical_and, [i == g - 1 for i, g in zip(ids, grid)])

        @pl.when(first)
        def _():
            exch.start(xin, xout, sems)

        body(*ins, *outs, *scr)

        @pl.when(last)
        def _():
            exch.finish(xin, xout, sems)

    res = pl.pallas_call(
        wrapped, name=name, grid=grid, in_specs=list(in_specs) + [ANY_SPEC] * m,
        out_specs=out_specs_l + [ANY_SPEC] * m, out_shape=out_shape_l + exch.out_shape,
        scratch_shapes=list(scratch_shapes) + exch.scratch, compiler_params=_cparams(sem),
    )(*args, *exch.arrays)
    outs = res[:n_out]
    return (outs[0] if single else outs), res[n_out:]


def _adam_math(w, g, m, v):
    m2 = ADAM_B1 * m + (1.0 - ADAM_B1) * g
    v2 = ADAM_B2 * v + (1.0 - ADAM_B2) * (g * g)
    m_hat = m2 / (1.0 - ADAM_B1 ** ADAM_STEP)
    v_hat = v2 / (1.0 - ADAM_B2 ** ADAM_STEP)
    delta = -ADAM_LR * (m_hat / (jnp.sqrt(v_hat) + ADAM_EPS) + ADAM_WD * w)
    return delta, m2, v2


ADAM_ROW_BLOCKS = 2


def adam_shard(parts0, parts1, w, m, v, name):
    _, r, c = w.shape
    rb = r // ADAM_ROW_BLOCKS

    def body(p0_ref, p1_ref, w_ref, m_ref, v_ref, g_ref, d_ref, m2_ref, v2_ref):
        def run(p_ref):
            g = p_ref[0].astype(F32)
            for i in range(1, N_DEV):
                g = g + p_ref[i].astype(F32)
            delta, m2, v2 = _adam_math(w_ref[0], g, m_ref[0], v_ref[0])
            g_ref[0] = g
            d_ref[0] = delta
            m2_ref[0] = m2
            v2_ref[0] = v2

        @pl.when(pl.program_id(0) == 0)
        def _():
            run(p0_ref)

        @pl.when(pl.program_id(0) == 1)
        def _():
            run(p1_ref)

    def p_spec(layer):
        row = (lambda l, j: jnp.where(l == 0, j, ADAM_ROW_BLOCKS - 1)) if layer == 0 else (lambda l, j: jnp.where(l == 1, j, 0))
        return pl.BlockSpec((N_DEV, rb, c), lambda l, j: (0, row(l, j), 0))

    blk = pl.BlockSpec((1, rb, c), lambda l, j: (l, j, 0))
    return pl.pallas_call(
        body, name=name, grid=(DEPTH, ADAM_ROW_BLOCKS),
        in_specs=[p_spec(0), p_spec(1), blk, blk, blk], out_specs=[blk] * 4,
        out_shape=[SDS(w.shape, F32)] * 4,
        compiler_params=_cparams(("arbitrary", "arbitrary")),
    )(parts0, parts1, w, m, v)


def parts_sum(parts0, parts1, name):
    _, r, c = parts0.shape

    def body(p0_ref, p1_ref, g_ref):
        def run(p_ref):
            g = p_ref[0].astype(F32)
            for i in range(1, N_DEV):
                g = g + p_ref[i].astype(F32)
            g_ref[0] = g

        @pl.when(pl.program_id(0) == 0)
        def _():
            run(p0_ref)

        @pl.when(pl.program_id(0) == 1)
        def _():
            run(p1_ref)

    full = pl.BlockSpec((N_DEV, r, c), lambda l: (0, 0, 0))
    return pl.pallas_call(
        body, name=name, grid=(DEPTH,), in_specs=[full, full],
        out_specs=pl.BlockSpec((1, r, c), lambda l: (l, 0, 0)), out_shape=SDS((DEPTH, r, c), F32),
        compiler_params=_cparams(("arbitrary",)),
    )(parts0, parts1)


def adam_given(g, w, m, v, name):
    _, r, c = w.shape

    def body(g_ref, w_ref, m_ref, v_ref, d_ref, m2_ref, v2_ref):
        delta, m2, v2 = _adam_math(w_ref[0], g_ref[0], m_ref[0], v_ref[0])
        d_ref[0] = delta
        m2_ref[0] = m2
        v2_ref[0] = v2

    blk = pl.BlockSpec((1, r, c), lambda l: (l, 0, 0))
    return pl.pallas_call(
        body, name=name, grid=(DEPTH,), in_specs=[blk] * 4, out_specs=[blk] * 3, out_shape=[SDS(w.shape, F32)] * 3,
        compiler_params=_cparams(("arbitrary",)),
    )(g, w, m, v)


def adam_small(parts, w, m, v, name):
    def body(p_ref, w_ref, m_ref, v_ref, g_ref, d_ref, m2_ref, v2_ref):
        g = p_ref[0]
        for i in range(1, N_DEV):
            g = g + p_ref[i]
        delta, m2, v2 = _adam_math(w_ref[...], g, m_ref[...], v_ref[...])
        g_ref[...] = g
        d_ref[...] = delta
        m2_ref[...] = m2
        v2_ref[...] = v2

    return pl.pallas_call(
        body, name=name, out_shape=[SDS(w.shape, F32)] * 4, compiler_params=_cparams(),
    )(parts, w, m, v)


def _rot_cols(w):
    w4 = w.reshape(w.shape[0], 4, 2, 32)
    return jnp.stack([-w4[:, :, 1], w4[:, :, 0]], axis=2).reshape(w.shape[0], ATT_W)


def _rot_cols_t(dw_rot):
    d4 = dw_rot.reshape(dw_rot.shape[0], 4, 2, 32)
    return jnp.stack([d4[:, :, 1], -d4[:, :, 0]], axis=2).reshape(dw_rot.shape[0], ATT_W)


def build_wext(w_in):
    aq, ak, av, pu = w_in[:, 0:256], w_in[:, 256:512], w_in[:, 512:768], w_in[:, 768:1024]
    dqkvz = w_in[:, 1024:3072]
    gates = jnp.pad(w_in[:, 3072:3080], ((0, 0), (0, GATE_W - 2 * DN_H)))
    return jnp.concatenate([aq, ak, av, _rot_cols(aq), _rot_cols(ak), dqkvz, gates, pu], axis=1)


def fold_dwext(d):
    b = EXT_ATT
    aq = d[:, 0:256] + _rot_cols_t(d[:, 768:1024])
    ak = d[:, 256:512] + _rot_cols_t(d[:, 1024:1280])
    av = d[:, 512:768]
    dqkvz = d[:, b:b + 2048]
    gates = d[:, b + R_G:b + R_G + 2 * DN_H]
    pu = d[:, b + R_PU:b + R_PU + 256]
    return jnp.concatenate([aq, ak, av, pu, dqkvz, gates], axis=1)


def _block_diag(pw):
    z = jnp.zeros((4, 64, 4, 64), pw.dtype)
    for g in range(4):
        z = z.at[g, :, g, :].set(pw[g])
    return z.reshape(POOL_W, POOL_W)


def _diag_blocks(m):
    m4 = m.reshape(4, 64, 4, 64)
    return jnp.stack([m4[g, :, g, :] for g in range(4)], axis=0)


def _lanes(v, reps):
    return jnp.repeat(v, reps)[None, :]


def layer_fwd(p, xa, cos, sin, l, host=None):
    host = host or {}

    def carried(key):
        return host[key][0] if key in host else None

    def done(key, xo):
        if key in host:
            host[key][1](xo)

    xb, xo = ffn_fwd(xa, p["n1"], *p["f1"], f"ffn1_fwd_{l}", carried("ffn1"))
    done("ffn1", xo)
    att, rest = mix_in_fwd(xb, p["nm"], p["wext"], cos, sin, f"mix_in_fwd_{l}")
    ols = [att_fwd_s(att, d, f"att_fwd_{l}_{d}") for d in DILATIONS]
    qkv = dn_prep_fwd(rest, p["conv"], f"dn_prep_fwd_{l}")
    dn, xo = dn_intra_fwd(qkv, rest, p["alog"], p["dtb"], f"dn_intra_fwd_{l}", carried("dn_intra"))
    done("dn_intra", xo)
    (odn, states), xo = dn_inter_fwd(*dn, f"dn_inter_fwd_{l}", carried("dn_inter"))
    done("dn_inter", xo)
    xc = mix_out_fwd(xb, ols[0], ols[1], ols[2], rest, odn, p["wbd"], p["scale"], p["onorm"], p["wout"], f"mix_out_fwd_{l}")
    xd, xo = ffn_fwd(xc, p["n2"], *p["f2"], f"ffn2_fwd_{l}", carried("ffn2"))
    done("ffn2", xo)
    return xd, dict(xa=xa, xb=xb, xc=xc, att=att, rest=rest, ols=ols, qkv=qkv, dn=dn, odn=odn, states=states)


def layer_bwd(p, s, dx, cos, sin, headsum, l, scatter=False, carry=None):
    blocks = lambda ws: [w_.reshape(N_DEV, FF_BLK, D_MODEL) for w_ in ws]
    (dx, *d_f2, d_n2), carried = ffn_bwd(s["xc"], dx, p["n2"], *p["f2"], f"ffn2_bwd_{l}", carry)
    (d_wout, dol1, dol4, dol16, dpooled, dodn, dz, dscale, donorm, dwbd) = mix_out_bwd(
        dx, s["ols"][0], s["ols"][1], s["ols"][2], s["rest"], s["odn"], p["wbd"], p["scale"], p["onorm"], p["wout"],
        headsum, f"mix_out_bwd_{l}")
    dpu = pool_bwd(dpooled, f"pool_bwd_{l}")
    f2 = blocks(d_f2)
    d_dn = dn_inter_bwd(*s["dn"], s["states"], dodn, f"dn_inter_bwd_{l}")
    dqkv, dg, dalog, ddtb = dn_intra_bwd(s["qkv"], s["rest"], p["alog"], p["dtb"], *d_dn, f"dn_intra_bwd_{l}")
    d_dqkv, dconv = dn_prep_bwd(s["rest"], p["conv"], dqkv, f"dn_prep_bwd_{l}")
    datts = [att_bwd_s(s["att"], ol, dol, d, f"att_bwd_{l}_{d}")
             for d, ol, dol in zip(DILATIONS, s["ols"], (dol1, dol4, dol16))]
    dproj = assemble_dproj(datts, cos, sin, d_dqkv, dz, dg, dpu, f"assemble_dproj_{l}")
    dx, d_wext, d_nm = linear_bwd(s["xb"], dx, p["nm"], dproj, p["wext"], f"mix_in_bwd_{l}")
    d_win = fold_dwext(d_wext).reshape(D_MODEL, N_DEV, IN_BLK).transpose(1, 0, 2).astype(BF16)
    io = [d_win, d_wout.reshape(N_DEV, D_MODEL // N_DEV, D_MODEL).astype(BF16)]
    (dx, *d_f1, d_n1), xo = ffn_bwd(s["xa"], dx, p["n1"], *p["f1"], f"ffn1_bwd_{l}",
                                    Exchange(f2 + io, "scatter") if scatter else None)
    if scatter:
        f2, io = list(xo[:3]), list(xo[3:])
    big = dict(f1=blocks(d_f1), f2=f2, io=io)
    small = dict(ffn1_norm=d_n1[0], mix_norm=d_nm[0], ffn2_norm=d_n2[0], pool_w=_diag_blocks(dwbd),
                 pool_scale=dscale[0], dn_a_log=dalog.reshape(DN_H, DN_E).sum(-1),
                 dn_dt_bias=ddtb.reshape(DN_H, DN_E).sum(-1),
                 dn_out_norm=donorm.reshape(DN_H, DN_E).sum(0), dn_conv_w=dconv)
    return dx, big, small, carried


def small_operands(l, pool_w, pool_scale, dn_out_norm, dn_a_log, dn_dt_bias, ffn1_norm, mix_norm, ffn2_norm):
    return dict(
        wbd=_block_diag(pool_w[l]).astype(BF16),
        scale=pool_scale[l][None, :],
        onorm=jnp.tile(dn_out_norm[l], DN_H)[None, :],
        alog=_lanes(dn_a_log[l], DN_E),
        dtb=_lanes(dn_dt_bias[l], DN_E),
        n1=ffn1_norm[l][None, :], nm=mix_norm[l][None, :], n2=ffn2_norm[l][None, :])


def set_mixer_weights(p, win_g, wout_g, conv_g):
    p["wext"] = build_wext(win_g.transpose(1, 0, 2).reshape(D_MODEL, IN_W))
    p["wout"] = wout_g.reshape(D_MODEL, D_MODEL)
    p["conv"] = conv_g.transpose(1, 0, 2).reshape(DN_CONV, 3 * DN_W)


def rope_tables(pos):
    inv_freq = 10000.0 ** (-jnp.arange(0, ATT_E, 2, dtype=F32) / ATT_E)
    ang = pos.astype(F32)[:, None] * inv_freq
    return jnp.tile(jnp.cos(ang), (1, 8)), jnp.tile(jnp.sin(ang), (1, 8))


def head_sum_matrix():
    return jnp.kron(jnp.eye(4, dtype=F32), jnp.ones((ATT_E, ATT_E), F32))


SMALL_NAMES = ("ffn1_norm", "mix_norm", "ffn2_norm", "pool_w", "pool_scale", "dn_a_log", "dn_dt_bias",
               "dn_out_norm", "final_norm", "dn_conv_w")


PACK_UNIT = 8 * 128


def _pack_rows(n):
    return -(-n // PACK_UNIT) * 8


def _pack(parts):
    rows = []
    for p in parts:
        flat = p.reshape(-1)
        r = _pack_rows(flat.shape[0])
        rows.append(jnp.pad(flat, (0, r * 128 - flat.shape[0])).reshape(r, 128))
    return jnp.concatenate(rows, axis=0)


def _unpack(packed, shapes):
    out, row = [], 0
    for s in shapes:
        n = math.prod(s)
        r = _pack_rows(n)
        out.append(packed[row:row + r].reshape(-1)[:n].reshape(s))
        row += r
    return out


def kernel(x, positions, ffn1_norm, ffn1_w_gate, ffn1_w_up, ffn1_w_down, mix_norm, w_in, pool_w, pool_scale, dn_conv_w, dn_a_log, dn_dt_bias, dn_out_norm, w_out, ffn2_norm, ffn2_w_gate, ffn2_w_up, ffn2_w_down, final_norm, loss_target, m_ffn1_norm, m_ffn1_w_gate, m_ffn1_w_up, m_ffn1_w_down, m_mix_norm, m_w_in, m_pool_w, m_pool_scale, m_dn_conv_w, m_dn_a_log, m_dn_dt_bias, m_dn_out_norm, m_w_out, m_ffn2_norm, m_ffn2_w_gate, m_ffn2_w_up, m_ffn2_w_down, m_final_norm, v_ffn1_norm, v_ffn1_w_gate, v_ffn1_w_up, v_ffn1_w_down, v_mix_norm, v_w_in, v_pool_w, v_pool_scale, v_dn_conv_w, v_dn_a_log, v_dn_dt_bias, v_dn_out_norm, v_w_out, v_ffn2_norm, v_ffn2_w_gate, v_ffn2_w_up, v_ffn2_w_down, v_final_norm):
    me = 4 * lax.axis_index("x") + 2 * lax.axis_index("y") + lax.axis_index("c")
    x0 = x[0]
    target = loss_target[0]

    cos, sin = rope_tables(positions[0])
    headsum = head_sum_matrix()

    layers = [small_operands(l, pool_w, pool_scale, dn_out_norm, dn_a_log, dn_dt_bias, ffn1_norm, mix_norm, ffn2_norm)
              for l in range(DEPTH)]

    def whole(gathered):
        return gathered.reshape(D_FF, D_MODEL)

    def gather_ffn1(l):
        def on_done(xo):
            layers[l]["f1"] = tuple(whole(g) for g in xo)
        return Exchange(ffn_shard_operands(ffn1_w_gate[l], ffn1_w_up[l], ffn1_w_down[l]), "gather"), on_done

    def gather_mixer(l):
        def on_done(xo):
            set_mixer_weights(layers[l], *xo)
        return Exchange([w_in[l].astype(BF16), w_out[l].astype(BF16), dn_conv_w[l]], "gather"), on_done

    gathered_f2 = {}

    def gather_ffn2_part(l, part):
        def on_done(xo):
            gathered_f2[(l, part)] = [whole(g) for g in xo]
            if (l, 0) in gathered_f2 and (l, 1) in gathered_f2:
                layers[l]["f2"] = tuple(gathered_f2[(l, 0)] + gathered_f2[(l, 1)])
        ops = ffn_shard_operands(ffn2_w_gate[l], ffn2_w_up[l], ffn2_w_down[l])
        return Exchange(ops[:2] if part == 0 else ops[2:], "gather"), on_done

    first, on_first = gather_ffn1(0)
    on_first(run_exchange(first, "gather_ffn1_0"))
    saved = []
    xa = x0
    for l in range(DEPTH):
        host = {"ffn1": gather_mixer(l), "dn_intra": gather_ffn2_part(l, 0), "dn_inter": gather_ffn2_part(l, 1)}
        if l + 1 < DEPTH:
            host["ffn2"] = gather_ffn1(l + 1)
        xa, s = layer_fwd(layers[l], xa, cos, sin, l, host)
        saved.append(s)

    loss_row, dx, d_final = loss_head(xa, final_norm[None, :], target, "loss_head")
    loss = lax.psum(loss_row[0, 0], ("x", "y", "c"))

    small = {}
    big_parts = [None] * DEPTH
    carry = None
    for l in reversed(range(DEPTH)):
        dx, big, small[l], carried = layer_bwd(layers[l], saved[l], dx, cos, sin, headsum, l, True, carry)
        if carried is not None:
            big_parts[l + 1]["f1"] = list(carried)
        big_parts[l] = big
        carry = Exchange(big["f1"], "scatter")
    grad_x = dx[None]

    small_shapes = {"ffn1_norm": (DEPTH, D_MODEL), "mix_norm": (DEPTH, D_MODEL), "ffn2_norm": (DEPTH, D_MODEL),
                    "pool_w": (DEPTH, 4, 64, 64), "pool_scale": (DEPTH, POOL_W), "dn_a_log": (DEPTH, DN_H),
                    "dn_dt_bias": (DEPTH, DN_H), "dn_out_norm": (DEPTH, DN_E), "final_norm": (D_MODEL,),
                    "dn_conv_w": (DEPTH, DN_CONV, 3 * DN_W)}
    g_small = {n: (d_final[0] if n == "final_norm" else jnp.stack([small[l][n] for l in range(DEPTH)]))
               for n in SMALL_NAMES}
    f1_parts, (small_parts,) = run_exchanges(
        [carry, Exchange([_pack([g_small[n] for n in SMALL_NAMES])], "gather")], "scatter_ffn1_0_gather_small")
    big_parts[0]["f1"] = f1_parts

    def conv_full(a):
        return lax.dynamic_update_slice(jnp.zeros((DEPTH, DN_CONV, 3 * DN_W), F32), a, (0, 0, me * (3 * DN_W // N_DEV)))

    given = dict(ffn1_norm=(ffn1_norm, m_ffn1_norm, v_ffn1_norm), mix_norm=(mix_norm, m_mix_norm, v_mix_norm),
                 ffn2_norm=(ffn2_norm, m_ffn2_norm, v_ffn2_norm), pool_w=(pool_w, m_pool_w, v_pool_w),
                 pool_scale=(pool_scale, m_pool_scale, v_pool_scale), dn_a_log=(dn_a_log, m_dn_a_log, v_dn_a_log),
                 dn_dt_bias=(dn_dt_bias, m_dn_dt_bias, v_dn_dt_bias),
                 dn_out_norm=(dn_out_norm, m_dn_out_norm, v_dn_out_norm),
                 final_norm=(final_norm, m_final_norm, v_final_norm),
                 dn_conv_w=(conv_full(dn_conv_w), conv_full(m_dn_conv_w), conv_full(v_dn_conv_w)))
    packed_wmv = [_pack([given[n][k] for n in SMALL_NAMES]) for k in range(3)]
    small_out = adam_small(small_parts, *packed_wmv, "adam_small")
    shapes = [small_shapes[n] for n in SMALL_NAMES]
    small_res = {n: [] for n in SMALL_NAMES}
    for arr in small_out:
        for n, v_ in zip(SMALL_NAMES, _unpack(arr, shapes)):
            if n == "dn_conv_w":
                v_ = lax.dynamic_slice(v_, (0, 0, me * (3 * DN_W // N_DEV)), (DEPTH, DN_CONV, 3 * DN_W // N_DEV))
            small_res[n].append(v_)

    def parts_of(group, idx):
        return [big_parts[l][group][idx] for l in range(DEPTH)]

    def adam_transposed(group, idx, w, m, v, name):
        g = parts_sum(*parts_of(group, idx), f"sum_{name}").transpose(0, 2, 1)
        return [g] + list(adam_given(g, w, m, v, f"adam_{name}"))

    big_res = dict(
        ffn1_w_gate=adam_transposed("f1", 0, ffn1_w_gate, m_ffn1_w_gate, v_ffn1_w_gate, "ffn1_gate"),
        ffn1_w_up=adam_transposed("f1", 1, ffn1_w_up, m_ffn1_w_up, v_ffn1_w_up, "ffn1_up"),
        ffn1_w_down=adam_shard(*parts_of("f1", 2), ffn1_w_down, m_ffn1_w_down, v_ffn1_w_down, "adam_ffn1_down"),
        ffn2_w_gate=adam_transposed("f2", 0, ffn2_w_gate, m_ffn2_w_gate, v_ffn2_w_gate, "ffn2_gate"),
        ffn2_w_up=adam_transposed("f2", 1, ffn2_w_up, m_ffn2_w_up, v_ffn2_w_up, "ffn2_up"),
        ffn2_w_down=adam_shard(*parts_of("f2", 2), ffn2_w_down, m_ffn2_w_down, v_ffn2_w_down, "adam_ffn2_down"),
        w_in=adam_shard(*parts_of("io", 0), w_in, m_w_in, v_w_in, "adam_w_in"),
        w_out=adam_shard(*parts_of("io", 1), w_out, m_w_out, v_w_out, "adam_w_out"),
    )

    order = ("ffn1_norm", "ffn1_w_gate", "ffn1_w_up", "ffn1_w_down", "mix_norm", "w_in", "pool_w", "pool_scale",
             "dn_conv_w", "dn_a_log", "dn_dt_bias", "dn_out_norm", "w_out", "ffn2_norm", "ffn2_w_gate", "ffn2_w_up",
             "ffn2_w_down", "final_norm")
    res = {**small_res, **big_res}
    outs = [loss, grad_x]
    for k in range(4):
        outs.extend(res[n][k] for n in order)
    return tuple(outs)
```

```python
import functools
import math

import jax
import jax.numpy as jnp
from jax import lax
from jax.experimental import pallas as pl
from jax.experimental.pallas import tpu as pltpu

F32 = jnp.float32
BF16 = jnp.bfloat16
HI = lax.Precision.HIGHEST
INV_PREC = lax.Precision.HIGH
SDS = jax.ShapeDtypeStruct

N_DEV = 8
SEQ = 4096
D_MODEL = 1024
DEPTH = 2
D_FF = 2816
FF_BLK = D_FF // N_DEV
ATT_W = 256
ATT_E = 64
ATT_BLK = 128
DILATIONS = (1, 4, 16)
POOL_W = 256
POOL_HALO = 16
DN_W = 512
DN_H = 4
DN_E = 128
DN_C = 64
N_CHUNK = SEQ // DN_C
IN_W = 3080
IN_BLK = IN_W // N_DEV
EPS = 1e-6
EXT_ATT = 1280
GATE_W = 256
EXT_REST = 4 * DN_W + GATE_W + POOL_W
EXT_W = EXT_ATT + EXT_REST
R_DQKV, R_DZ, R_G, R_PU = 0, 1536, 2048, 2304

ADAM_LR, ADAM_B1, ADAM_B2, ADAM_EPS, ADAM_WD, ADAM_STEP = 0.001, 0.9, 0.999, 1e-08, 0.01, 10

VMEM_LIMIT = 60 * 1024 * 1024
MESH = pl.DeviceIdType.MESH


def _cparams(sem=None):
    kw = dict(vmem_limit_bytes=VMEM_LIMIT)
    if sem is not None:
        kw["dimension_semantics"] = sem
    return pltpu.CompilerParams(**kw)


def _dot(a, b, prec=None):
    return jnp.dot(a, b, preferred_element_type=F32, precision=prec)


def _dot_nt(a, b, prec=None):
    return lax.dot_general(a, b, (((1,), (1,)), ((), ())), preferred_element_type=F32, precision=prec)


def _dot_tn(a, b, prec=None):
    return lax.dot_general(a, b, (((0,), (0,)), ((), ())), preferred_element_type=F32, precision=prec)


def _sigmoid(x):
    return jax.nn.sigmoid(x)


def _rms_stats(x):
    r = lax.rsqrt(jnp.mean(x * x, axis=-1, keepdims=True) + EPS)
    return x * r, r


def _rms_bwd(xh, r, w, dh):
    dxh = dh * w
    dx = r * (dxh - xh * jnp.mean(dxh * xh, axis=-1, keepdims=True))
    return dx, jnp.sum(dh * xh, axis=0, keepdims=True)


FFN_T_FWD = 2048
FFN_T_BWD = 512
FF_TILE = 256
N_FF_TILE = D_FF // FF_TILE


def ffn_shard_operands(gate, up, down):
    return [gate.T.astype(BF16), up.T.astype(BF16), down.astype(BF16)]


def ffn_fwd(x, nw, wgt, wut, wd, name, exch=None):
    t = FFN_T_FWD

    def body(x_ref, nw_ref, wgt_ref, wut_ref, wd_ref, o_ref, h_scr, acc_scr):
        k = pl.program_id(1)

        @pl.when(k == 0)
        def _():
            xh, _r = _rms_stats(x_ref[...])
            h_scr[...] = (xh * nw_ref[...]).astype(BF16)
            acc_scr[...] = jnp.zeros_like(acc_scr)

        h = h_scr[...]
        hg = _dot_nt(h, wgt_ref[...])
        hu = _dot_nt(h, wut_ref[...])
        a = (hg * _sigmoid(hg) * hu).astype(BF16)
        acc_scr[...] += _dot(a, wd_ref[...])

        @pl.when(k == N_FF_TILE - 1)
        def _():
            o_ref[...] = x_ref[...] + 0.5 * acc_scr[...]

    w_spec = pl.BlockSpec((FF_TILE, D_MODEL), lambda i, k: (k, 0))
    return _call(
        body, name=name, grid=(SEQ // t, N_FF_TILE),
        in_specs=[pl.BlockSpec((t, D_MODEL), lambda i, k: (i, 0)),
                  pl.BlockSpec((1, D_MODEL), lambda i, k: (0, 0)), w_spec, w_spec, w_spec],
        out_specs=pl.BlockSpec((t, D_MODEL), lambda i, k: (i, 0)),
        out_shape=SDS((SEQ, D_MODEL), F32),
        scratch_shapes=[pltpu.VMEM((t, D_MODEL), BF16), pltpu.VMEM((t, D_MODEL), F32)],
        sem=("arbitrary", "arbitrary"), args=(x, nw, wgt, wut, wd), exch=exch)


def ffn_bwd(x, dxo, nw, wgt, wut, wd, name, exch=None):
    t = FFN_T_BWD
    nt = SEQ // t

    def body(x_ref, dxo_ref, nw_ref, wgt_ref, wut_ref, wd_ref, dx_ref, dwgt_ref, dwut_ref, dwd_ref, dnw_ref,
             dh_scr, ag_scr, au_scr, ad_scr, h_scr):
        k = pl.program_id(0)
        i = pl.program_id(1)
        rows = pl.ds(pl.multiple_of(i * t, t), t)
        nw_v = nw_ref[...]

        @pl.when(k == 0)
        def _():
            xh0, _r0 = _rms_stats(x_ref[...])
            h_scr[rows, :] = (xh0 * nw_v).astype(BF16)

        h = h_scr[rows, :]
        dy = (0.5 * dxo_ref[...]).astype(BF16)
        wgt = wgt_ref[...]
        wut = wut_ref[...]
        hg = _dot_nt(h, wgt)
        hu = _dot_nt(h, wut)
        sg = _sigmoid(hg)
        sil = hg * sg
        a = (sil * hu).astype(BF16)
        da = _dot_nt(dy, wd_ref[...])
        dhu = (da * sil).astype(BF16)
        dhg = (da * hu * (sg * (1.0 + hg * (1.0 - sg)))).astype(BF16)
        p_d = _dot_tn(a, dy)
        p_g = _dot_tn(dhg, h)
        p_u = _dot_tn(dhu, h)
        dh = _dot(dhg, wgt) + _dot(dhu, wut)

        @pl.when(i == 0)
        def _():
            ad_scr[...] = p_d
            ag_scr[...] = p_g
            au_scr[...] = p_u

        @pl.when(i > 0)
        def _():
            ad_scr[...] += p_d
            ag_scr[...] += p_g
            au_scr[...] += p_u

        @pl.when(i == nt - 1)
        def _():
            dwd_ref[...] = ad_scr[...].astype(BF16)
            dwgt_ref[...] = ag_scr[...].astype(BF16)
            dwut_ref[...] = au_scr[...].astype(BF16)

        @pl.when(k == 0)
        def _():
            dh_scr[rows, :] = dh

        @pl.when(k > 0)
        def _():
            dh_scr[rows, :] += dh

        @pl.when(jnp.logical_and(k == 0, i == 0))
        def _():
            dnw_ref[...] = jnp.zeros_like(dnw_ref)

        @pl.when(k == N_FF_TILE - 1)
        def _():
            xh, r = _rms_stats(x_ref[...])
            dx, dw = _rms_bwd(xh, r, nw_v, dh_scr[rows, :])
            dx_ref[...] = dxo_ref[...] + dx
            dnw_ref[...] += dw

    last = N_FF_TILE - 1
    w_spec = pl.BlockSpec((FF_TILE, D_MODEL), lambda k, i: (k, 0))
    return _call(
        body, name=name, grid=(N_FF_TILE, nt),
        in_specs=[pl.BlockSpec((t, D_MODEL), lambda k, i: (i, 0)),
                  pl.BlockSpec((t, D_MODEL), lambda k, i: (i, 0)),
                  pl.BlockSpec((1, D_MODEL), lambda k, i: (0, 0)), w_spec, w_spec, w_spec],
        out_specs=[pl.BlockSpec((t, D_MODEL), lambda k, i: (jnp.where(k == last, i, 0), 0)),
                   w_spec, w_spec, w_spec, pl.BlockSpec((1, D_MODEL), lambda k, i: (0, 0))],
        out_shape=[SDS((SEQ, D_MODEL), F32), SDS((D_FF, D_MODEL), BF16), SDS((D_FF, D_MODEL), BF16),
                   SDS((D_FF, D_MODEL), BF16), SDS((1, D_MODEL), F32)],
        scratch_shapes=[pltpu.VMEM((SEQ, D_MODEL), F32), pltpu.VMEM((FF_TILE, D_MODEL), F32),
                        pltpu.VMEM((FF_TILE, D_MODEL), F32), pltpu.VMEM((FF_TILE, D_MODEL), F32),
                        pltpu.VMEM((SEQ, D_MODEL), BF16)],
        sem=("arbitrary", "arbitrary"), args=(x, dxo, nw, wgt, wut, wd), exch=exch)


def loss_head(x, fw, target, name):
    t = 512

    def body(x_ref, fw_ref, tg_ref, loss_ref, dx_ref, dfw_ref):
        i = pl.program_id(0)
        xh, r = _rms_stats(x_ref[...])
        w = fw_ref[...]
        err = xh * w - tg_ref[...]
        part = 0.5 * jnp.sum(jnp.sum(err * err, axis=-1, keepdims=True), axis=0, keepdims=True) / D_MODEL
        dx, dw = _rms_bwd(xh, r, w, err * (1.0 / D_MODEL))
        dx_ref[...] = dx

        @pl.when(i == 0)
        def _():
            loss_ref[...] = jnp.zeros_like(loss_ref)
            dfw_ref[...] = jnp.zeros_like(dfw_ref)

        loss_ref[...] += jnp.broadcast_to(part, loss_ref.shape)
        dfw_ref[...] += dw

    return pl.pallas_call(
        body, name=name, grid=(SEQ // t,),
        in_specs=[pl.BlockSpec((t, D_MODEL), lambda i: (i, 0)),
                  pl.BlockSpec((1, D_MODEL), lambda i: (0, 0)),
                  pl.BlockSpec((t, D_MODEL), lambda i: (i, 0))],
        out_specs=[pl.BlockSpec((1, 128), lambda i: (0, 0)),
                   pl.BlockSpec((t, D_MODEL), lambda i: (i, 0)),
                   pl.BlockSpec((1, D_MODEL), lambda i: (0, 0))],
        out_shape=[SDS((1, 128), F32), SDS((SEQ, D_MODEL), F32), SDS((1, D_MODEL), F32)],
        compiler_params=_cparams(("arbitrary",)),
    )(x, fw, target)


MIX_T = 512


def _slabs_load(ref, first, n):
    return jnp.concatenate([ref[first + j] for j in range(n)], axis=1)


def _slabs_store(ref, first, val):
    for j in range(val.shape[1] // 128):
        ref[first + j] = val[:, 128 * j:128 * j + 128]


def _slab_spec(k, t):
    return pl.BlockSpec((k, t, 128), lambda i: (0, i, 0))


def mix_in_fwd(x, nw, wext, cos, sin, name):
    t = MIX_T

    def body(x_ref, nw_ref, w_ref, cos_ref, sin_ref, att_ref, rest_ref):
        xh, _r = _rms_stats(x_ref[...])
        h = (xh * nw_ref[...]).astype(BF16)
        pa = _dot(h, w_ref[:, 0:EXT_ATT])
        c = cos_ref[...]
        s = sin_ref[...]
        _slabs_store(att_ref, 0, pa[:, 0:256] * c + pa[:, 768:1024] * s)
        _slabs_store(att_ref, 2, pa[:, 256:512] * c + pa[:, 1024:1280] * s)
        _slabs_store(att_ref, 4, pa[:, 512:768])
        for j in range(EXT_REST // 256):
            rest_ref[:, 256 * j:256 * j + 256] = _dot(h, w_ref[:, EXT_ATT + 256 * j:EXT_ATT + 256 * j + 256])

    return pl.pallas_call(
        body, name=name, grid=(SEQ // t,),
        in_specs=[pl.BlockSpec((t, D_MODEL), lambda i: (i, 0)),
                  pl.BlockSpec((1, D_MODEL), lambda i: (0, 0)),
                  pl.BlockSpec((D_MODEL, EXT_W), lambda i: (0, 0)),
                  pl.BlockSpec((t, ATT_W), lambda i: (i, 0)),
                  pl.BlockSpec((t, ATT_W), lambda i: (i, 0))],
        out_specs=[_slab_spec(6, t),
                   pl.BlockSpec((t, EXT_REST), lambda i: (i, 0))],
        out_shape=[SDS((6, SEQ, 128), F32), SDS((SEQ, EXT_REST), F32)],
        compiler_params=_cparams(("arbitrary",)),
    )(x, nw, wext, cos, sin)


def assemble_dproj(datts, cos, sin, d_dqkv, dz, dg, dpu, name):
    t = 512

    def body(d1_ref, d4_ref, d16_ref, cos_ref, sin_ref, dqkv_ref, dz_ref, dg_ref, dpu_ref, o_ref):
        da6 = d1_ref[...] + d4_ref[...] + d16_ref[...]
        da = jnp.concatenate([da6[j] for j in range(6)], axis=1)
        c = cos_ref[...]
        s = sin_ref[...]
        dq = da[:, 0:256]
        dk = da[:, 256:512]
        o_ref[:, 0:256] = (dq * c).astype(BF16)
        o_ref[:, 256:512] = (dk * c).astype(BF16)
        o_ref[:, 512:768] = da[:, 512:768].astype(BF16)
        o_ref[:, 768:1024] = (dq * s).astype(BF16)
        o_ref[:, 1024:1280] = (dk * s).astype(BF16)
        b = EXT_ATT
        o_ref[:, b + R_DQKV:b + R_DQKV + 1536] = dqkv_ref[...].astype(BF16)
        o_ref[:, b + R_DZ:b + R_DZ + 512] = dz_ref[...].astype(BF16)
        o_ref[:, b + R_G:b + R_G + GATE_W] = dg_ref[...].astype(BF16)
        o_ref[:, b + R_PU:b + R_PU + 256] = dpu_ref[...].astype(BF16)

    row = lambda w: pl.BlockSpec((t, w), lambda i: (i, 0))
    return pl.pallas_call(
        body, name=name, grid=(SEQ // t,),
        in_specs=[_slab_spec(6, t), _slab_spec(6, t), _slab_spec(6, t),
                  row(256), row(256), row(1536), row(512), row(GATE_W), row(256)],
        out_specs=row(EXT_W),
        out_shape=SDS((SEQ, EXT_W), BF16),
        compiler_params=_cparams(("arbitrary",)),
    )(*datts, cos, sin, d_dqkv, dz, dg, dpu)


def linear_bwd(x, dxo, nw, dy, w, name):
    t = 512
    nb = 768
    n = w.shape[1]
    nt = SEQ // t
    nn = n // nb

    def body(x_ref, dxo_ref, nw_ref, dy_ref, w_ref, dx_ref, dw_ref, dnw_ref, dh_scr, h_scr):
        k = pl.program_id(0)
        i = pl.program_id(1)
        rows = pl.ds(pl.multiple_of(i * t, t), t)
        nw_v = nw_ref[...]

        @pl.when(k == 0)
        def _():
            xh0, _r0 = _rms_stats(x_ref[...])
            h_scr[rows, :] = (xh0 * nw_v).astype(BF16)

        h = h_scr[rows, :]
        dyv = dy_ref[...]
        p_w = _dot_tn(h, dyv)
        dh = _dot_nt(dyv, w_ref[...])

        @pl.when(i == 0)
        def _():
            dw_ref[...] = p_w

        @pl.when(i > 0)
        def _():
            dw_ref[...] += p_w

        @pl.when(k == 0)
        def _():
            dh_scr[rows, :] = dh

        @pl.when(k > 0)
        def _():
            dh_scr[rows, :] += dh

        @pl.when(jnp.logical_and(k == 0, i == 0))
        def _():
            dnw_ref[...] = jnp.zeros_like(dnw_ref)

        @pl.when(k == nn - 1)
        def _():
            xh, r = _rms_stats(x_ref[...])
            dx, dw = _rms_bwd(xh, r, nw_v, dh_scr[rows, :])
            dx_ref[...] = dxo_ref[...] + dx
            dnw_ref[...] += dw

    last = nn - 1
    return pl.pallas_call(
        body, name=name, grid=(nn, nt),
        in_specs=[pl.BlockSpec((t, D_MODEL), lambda k, i: (i, 0)),
                  pl.BlockSpec((t, D_MODEL), lambda k, i: (i, 0)),
                  pl.BlockSpec((1, D_MODEL), lambda k, i: (0, 0)),
                  pl.BlockSpec((t, nb), lambda k, i: (i, k)),
                  pl.BlockSpec((D_MODEL, nb), lambda k, i: (0, k))],
        out_specs=[pl.BlockSpec((t, D_MODEL), lambda k, i: (jnp.where(k == last, i, 0), 0)),
                   pl.BlockSpec((D_MODEL, nb), lambda k, i: (0, k)),
                   pl.BlockSpec((1, D_MODEL), lambda k, i: (0, 0))],
        out_shape=[SDS((SEQ, D_MODEL), F32), SDS((D_MODEL, n), F32), SDS((1, D_MODEL), F32)],
        scratch_shapes=[pltpu.VMEM((SEQ, D_MODEL), F32), pltpu.VMEM((SEQ, D_MODEL), BF16)],
        compiler_params=_cparams(("arbitrary", "arbitrary")),
    )(x, dxo, nw, dy, w)


def _att_masks():
    qi = lax.broadcasted_iota(jnp.int32, (ATT_BLK, ATT_BLK), 0)
    ki = lax.broadcasted_iota(jnp.int32, (ATT_BLK, ATT_BLK), 1)
    return ki <= qi, ki >= qi


NEG = -1e30


N_ATT_BLK = SEQ // ATT_BLK


def _class_rows(i, d):
    per_class = N_ATT_BLK // d
    shift = per_class.bit_length() - 1
    r = i >> shift
    j = i & (per_class - 1)
    span = ATT_BLK * d
    start = r + span * j
    prev = jnp.where(j == 0, start, start - span)
    nxt = jnp.where(j == per_class - 1, start, start + span)

    def rows(s0):
        if d == 1:
            return pl.ds(pl.multiple_of(s0, ATT_BLK), ATT_BLK)
        return pl.ds(s0, ATT_BLK, stride=d)

    return rows(start), rows(prev), rows(nxt), j != 0, j != per_class - 1


def _slab_heads(ref, slab, rows):
    x0 = ref[pl.ds(slab, 1), rows, :][0]
    x1 = ref[pl.ds(slab + 1, 1), rows, :][0]
    return jnp.stack([x0[:, 0:ATT_E], x0[:, ATT_E:], x1[:, 0:ATT_E], x1[:, ATT_E:]], axis=0)


def _put_slab_heads(ref, slab, rows, val):
    ref[pl.ds(slab, 1), rows, :] = jnp.concatenate([val[0], val[1]], axis=1)[None]
    ref[pl.ds(slab + 1, 1), rows, :] = jnp.concatenate([val[2], val[3]], axis=1)[None]


ATT_BLOCKS_PER_STEP = 2


def _resident_call(body, ins, out_slabs, name):
    n_in = len(ins)
    steps = N_ATT_BLK // ATT_BLOCKS_PER_STEP

    def wrapped(*refs):
        hbm_in, hbm_out = refs[:n_in], refs[n_in]
        vm_in, vm_out, sem = refs[n_in + 1:2 * n_in + 1], refs[2 * n_in + 1], refs[2 * n_in + 2]
        i = pl.program_id(0)

        @pl.when(i == 0)
        def _():
            copies = [pltpu.make_async_copy(h, v, sem.at[k]) for k, (h, v) in enumerate(zip(hbm_in, vm_in))]
            for cp in copies:
                cp.start()
            for cp in copies:
                cp.wait()

        for b in range(ATT_BLOCKS_PER_STEP):
            body(ATT_BLOCKS_PER_STEP * i + b, *vm_in, vm_out)

        @pl.when(i == steps - 1)
        def _():
            cp = pltpu.make_async_copy(vm_out, hbm_out, sem.at[n_in])
            cp.start()
            cp.wait()

    return pl.pallas_call(
        wrapped, name=name, grid=(steps,),
        in_specs=[ANY_SPEC] * n_in, out_specs=ANY_SPEC, out_shape=SDS((out_slabs, SEQ, 128), F32),
        scratch_shapes=[pltpu.VMEM(a.shape, a.dtype) for a in ins] + [pltpu.VMEM((out_slabs, SEQ, 128), F32),
                                                                      pltpu.SemaphoreType.DMA((n_in + 1,))],
        compiler_params=_cparams(("arbitrary",)),
    )(*ins)


def _att_fwd_math(ld, has_prev):
    m_d, m_p = _att_masks()
    m_p = jnp.logical_and(m_p, has_prev)
    q = ld("att", 0, "cur").astype(BF16)
    kc = ld("att", 2, "cur").astype(BF16)
    vc = ld("att", 4, "cur").astype(BF16)
    kp = ld("att", 2, "prev").astype(BF16)
    vp = ld("att", 4, "prev").astype(BF16)
    sd = jnp.where(m_d, _bdot(q, kc, 2, 2) * 0.125, NEG)
    sp = jnp.where(m_p, _bdot(q, kp, 2, 2) * 0.125, NEG)
    m = jnp.maximum(jnp.max(sd, axis=-1, keepdims=True), jnp.max(sp, axis=-1, keepdims=True))
    pd = jnp.exp(sd - m)
    pp = jnp.exp(sp - m)
    den = jnp.sum(pd, axis=-1, keepdims=True) + jnp.sum(pp, axis=-1, keepdims=True)
    inv = 1.0 / den
    o = _bdot((pd * inv).astype(BF16), vc, 2, 1) + _bdot((pp * inv).astype(BF16), vp, 2, 1)
    return o, jnp.broadcast_to(m + jnp.log(den), (4, ATT_BLK, ATT_E))


def _att_bwd_math(ld, has_prev, has_next):
    m_d, m_band = _att_masks()
    m_p = jnp.logical_and(m_band, has_prev)
    m_n = jnp.logical_and(m_band, has_next)

    def pair(q, k, v, lse, do, dterm, mask):
        s = jnp.where(mask, _bdot(q, k, 2, 2) * 0.125, NEG)
        p = jnp.exp(s - lse)
        dp = _bdot(do, v, 2, 2)
        ds = (p * (dp + dterm) * 0.125).astype(BF16)
        return p.astype(BF16), ds

    q_c = ld("att", 0, "cur").astype(BF16)
    k_c = ld("att", 2, "cur").astype(BF16)
    v_c = ld("att", 4, "cur").astype(BF16)
    k_p = ld("att", 2, "prev").astype(BF16)
    v_p = ld("att", 4, "prev").astype(BF16)
    q_n = ld("att", 0, "next").astype(BF16)
    o_c = ld("ol", 0, "cur")
    o_n = ld("ol", 0, "next")
    lse_c = ld("ol", 2, "cur")[:, :, 0:1]
    lse_n = ld("ol", 2, "next")[:, :, 0:1]
    do_c = ld("dol", 0, "cur")
    do_n = ld("dol", 0, "next")
    t_c = ld("dol", 2, "cur")[:, :, 0:1] - jnp.sum(do_c * o_c, axis=-1, keepdims=True)
    t_n = ld("dol", 2, "next")[:, :, 0:1] - jnp.sum(do_n * o_n, axis=-1, keepdims=True)
    do_cb = do_c.astype(BF16)
    do_nb = do_n.astype(BF16)
    p1, ds1 = pair(q_c, k_c, v_c, lse_c, do_cb, t_c, m_d)
    _p2, ds2 = pair(q_c, k_p, v_p, lse_c, do_cb, t_c, m_p)
    p3, ds3 = pair(q_n, k_c, v_c, lse_n, do_nb, t_n, m_n)
    return (_bdot(ds1, k_c, 2, 1) + _bdot(ds2, k_p, 2, 1), _bdot(ds1, q_c, 1, 1) + _bdot(ds3, q_n, 1, 1),
            _bdot(p1, do_cb, 1, 1) + _bdot(p3, do_nb, 1, 1))


ROWS_A = pl.ds(0, ATT_BLK)
ROWS_B = pl.ds(ATT_BLK, ATT_BLK)
N_ATT_PAIR = N_ATT_BLK // 2


def _pair_spec(k):
    return pl.BlockSpec((k, 2 * ATT_BLK, 128), lambda i: (0, i, 0))


def _before_pair_spec(k):
    return pl.BlockSpec((k, ATT_BLK, 128), lambda i: (0, jnp.maximum(2 * i - 1, 0), 0))


def _after_pair_spec(k):
    return pl.BlockSpec((k, ATT_BLK, 128), lambda i: (0, jnp.minimum(2 * i + 2, N_ATT_BLK - 1), 0))


def att_fwd_s(att, d, name):
    if d == 1:
        def body1(cur_ref, prev_ref, o_ref):
            i = pl.program_id(0)
            for rows, views, has_prev in (
                    (ROWS_A, {"cur": (cur_ref, ROWS_A), "prev": (prev_ref, ROWS_A)}, i != 0),
                    (ROWS_B, {"cur": (cur_ref, ROWS_B), "prev": (cur_ref, ROWS_A)}, True)):
                o, lse = _att_fwd_math(lambda _a, slab, where, v=views: _slab_heads(v[where][0], slab, v[where][1]), has_prev)
                _put_slab_heads(o_ref, 0, rows, o)
                _put_slab_heads(o_ref, 2, rows, lse)

        return pl.pallas_call(
            body1, name=name, grid=(N_ATT_PAIR,),
            in_specs=[_pair_spec(6), _before_pair_spec(6)], out_specs=_pair_spec(4),
            out_shape=SDS((4, SEQ, 128), F32), compiler_params=_cparams(("arbitrary",)),
        )(att, att)

    def body(i, att_ref, o_ref):
        cur, prev, _nxt, has_prev, _has_next = _class_rows(i, d)
        rows = {"cur": cur, "prev": prev}
        o, lse = _att_fwd_math(lambda _a, slab, where: _slab_heads(att_ref, slab, rows[where]), has_prev)
        _put_slab_heads(o_ref, 0, cur, o)
        _put_slab_heads(o_ref, 2, cur, lse)

    return _resident_call(body, [att], 4, name)


def att_bwd_s(att, ol, dol, d, name):
    if d == 1:
        def body1(a_p, a_c, a_n, ol_c, ol_n, dol_c, dol_n, d_ref):
            i = pl.program_id(0)
            first = {("att", "prev"): (a_p, ROWS_A), ("att", "cur"): (a_c, ROWS_A), ("att", "next"): (a_c, ROWS_B),
                     ("ol", "cur"): (ol_c, ROWS_A), ("ol", "next"): (ol_c, ROWS_B),
                     ("dol", "cur"): (dol_c, ROWS_A), ("dol", "next"): (dol_c, ROWS_B)}
            second = {("att", "prev"): (a_c, ROWS_A), ("att", "cur"): (a_c, ROWS_B), ("att", "next"): (a_n, ROWS_A),
                      ("ol", "cur"): (ol_c, ROWS_B), ("ol", "next"): (ol_n, ROWS_A),
                      ("dol", "cur"): (dol_c, ROWS_B), ("dol", "next"): (dol_n, ROWS_A)}
            for rows, views, has_prev, has_next in ((ROWS_A, first, i != 0, True),
                                                    (ROWS_B, second, True, i != N_ATT_PAIR - 1)):
                dq, dk, dv = _att_bwd_math(
                    lambda a, slab, where, v=views: _slab_heads(v[(a, where)][0], slab, v[(a, where)][1]), has_prev, has_next)
                _put_slab_heads(d_ref, 0, rows, dq)
                _put_slab_heads(d_ref, 2, rows, dk)
                _put_slab_heads(d_ref, 4, rows, dv)

        return pl.pallas_call(
            body1, name=name, grid=(N_ATT_PAIR,),
            in_specs=[_before_pair_spec(6), _pair_spec(6), _after_pair_spec(6), _pair_spec(4), _after_pair_spec(4),
                      _pair_spec(4), _after_pair_spec(4)],
            out_specs=_pair_spec(6), out_shape=SDS((6, SEQ, 128), F32), compiler_params=_cparams(("arbitrary",)),
        )(att, att, att, ol, ol, dol, dol)

    def body(i, att_ref, ol_ref, dol_ref, d_ref):
        cur, prev, nxt, has_prev, has_next = _class_rows(i, d)
        rows = {"cur": cur, "prev": prev, "next": nxt}
        refs = {"att": att_ref, "ol": ol_ref, "dol": dol_ref}
        dq, dk, dv = _att_bwd_math(lambda a, slab, where: _slab_heads(refs[a], slab, rows[where]), has_prev, has_next)
        _put_slab_heads(d_ref, 0, cur, dq)
        _put_slab_heads(d_ref, 2, cur, dk)
        _put_slab_heads(d_ref, 4, cur, dv)

    return _resident_call(body, [att, ol, dol], 6, name)


def _shift_down(x, k):
    rows = lax.broadcasted_iota(jnp.int32, x.shape, 0)
    return jnp.where(rows >= k, pltpu.roll(x, k, 0), 0.0)


def _shift_up(x, k):
    n = x.shape[0]
    rows = lax.broadcasted_iota(jnp.int32, x.shape, 0)
    return jnp.where(rows < n - k, pltpu.roll(x, n - k, 0), 0.0)


@functools.partial(jax.custom_vjp, nondiff_argnums=(1,))
def _delay(x, k):
    return _shift_down(x, k)


def _delay_fwd(x, k):
    return _shift_down(x, k), None


def _delay_bwd(k, _res, g):
    return (_shift_up(g, k),)


_delay.defvjp(_delay_fwd, _delay_bwd)

DN_CONV = 4


def _dn_prep_fn(u, w, kind):
    y = w[DN_CONV - 1:DN_CONV] * u
    for j in range(DN_CONV - 1):
        y = y + w[j:j + 1] * _delay(u, DN_CONV - 1 - j)
    y = y * _sigmoid(y)
    nrm = y * lax.rsqrt(jnp.sum(y * y, axis=-1, keepdims=True) + EPS)
    return jnp.where(kind == 0, nrm * (DN_E ** -0.5), jnp.where(kind == 1, nrm, y))


def dn_prep_fwd(rest, conv_w, name):
    def body(u_ref, w_ref, o_ref):
        j = pl.program_id(0)
        kind = (j >= DN_H).astype(jnp.int32) + (j >= 2 * DN_H).astype(jnp.int32)
        o_ref[...] = _dn_prep_fn(u_ref[...], w_ref[...], kind)

    return pl.pallas_call(
        body, name=name, grid=(3 * DN_H,),
        in_specs=[pl.BlockSpec((SEQ, DN_E), lambda j: (0, j)),
                  pl.BlockSpec((DN_CONV, DN_E), lambda j: (0, j))],
        out_specs=pl.BlockSpec((SEQ, DN_E), lambda j: (0, j)),
        out_shape=SDS((SEQ, 3 * DN_W), F32),
        compiler_params=_cparams(("arbitrary",)),
    )(rest, conv_w)


def dn_prep_bwd(rest, conv_w, dqkv, name):
    def body(u_ref, w_ref, g_ref, du_ref, dw_ref):
        j = pl.program_id(0)
        kind = (j >= DN_H).astype(jnp.int32) + (j >= 2 * DN_H).astype(jnp.int32)
        _y, vjp = jax.vjp(lambda u, w: _dn_prep_fn(u, w, kind), u_ref[...], w_ref[...])
        du, dw = vjp(g_ref[...])
        du_ref[...] = du
        dw_ref[...] = dw

    return pl.pallas_call(
        body, name=name, grid=(3 * DN_H,),
        in_specs=[pl.BlockSpec((SEQ, DN_E), lambda j: (0, j)),
                  pl.BlockSpec((DN_CONV, DN_E), lambda j: (0, j)),
                  pl.BlockSpec((SEQ, DN_E), lambda j: (0, j))],
        out_specs=[pl.BlockSpec((SEQ, DN_E), lambda j: (0, j)),
                   pl.BlockSpec((DN_CONV, DN_E), lambda j: (0, j))],
        out_shape=[SDS((SEQ, 3 * DN_W), F32), SDS((DN_CONV, 3 * DN_W), F32)],
        compiler_params=_cparams(("arbitrary",)),
    )(rest, conv_w, dqkv)


def _bdot(a, b, ca, cb, prec=None):
    return lax.dot_general(a, b, (((ca,), (cb,)), ((0,), (0,))), preferred_element_type=F32, precision=prec)


def _unit_lower_inverse(a):
    eye = (lax.broadcasted_iota(jnp.int32, (DN_C, DN_C), 0) == lax.broadcasted_iota(jnp.int32, (DN_C, DN_C), 1)).astype(F32)
    p = eye - a
    b = _bdot(a, a, 2, 1, INV_PREC)
    for lvl in range(5):
        p = p + _bdot(p, b, 2, 1, INV_PREC)
        if lvl < 4:
            b = _bdot(b, b, 2, 1, INV_PREC)
    return p


@jax.custom_vjp
def _tri_inv(a):
    return _unit_lower_inverse(a)


def _tri_inv_fwd(a):
    t = _unit_lower_inverse(a)
    return t, t


def _tri_inv_bwd(t, g):
    return (-_bdot(_bdot(t, g, 1, 1, INV_PREC), t, 2, 2, INV_PREC),)


_tri_inv.defvjp(_tri_inv_fwd, _tri_inv_bwd)


def _b16(x):
    return x.astype(BF16)


def _heads(ref, base=0):
    return jnp.stack([ref[:, base + DN_E * hd:base + DN_E * hd + DN_E] for hd in range(DN_H)], axis=0)


def _put_heads(ref, val, base=0):
    for hd in range(DN_H):
        ref[:, base + DN_E * hd:base + DN_E * hd + DN_E] = val[hd]


DN_G_LOG2 = 3
DN_G = 1 << DN_G_LOG2
N_INST = DN_G * DN_H


def _dn_intra(q, k, v, bb, ab, alog, dtb):
    ri = lax.broadcasted_iota(jnp.int32, (DN_C, DN_C), 0)
    ci = lax.broadcasted_iota(jnp.int32, (DN_C, DN_C), 1)
    lower = ri >= ci
    strict = ri > ci
    nh = q.shape[0]
    beta = _sigmoid(bb)
    xg = ab + dtb
    softplus = jnp.maximum(xg, 0.0) + jnp.log(1.0 + jnp.exp(-jnp.abs(xg)))
    gi = -jnp.exp(alog) * softplus
    g = _bdot(jnp.broadcast_to(lower.astype(F32), (nh, DN_C, DN_C)), gi, 2, 1, HI)
    eg = jnp.exp(g)
    kb = k * beta
    vb = v * beta
    g_col = g[:, :, 0:DN_C]
    g_row = _bdot(jnp.full((nh, DN_C, DN_E), 1.0 / DN_E, F32), g, 2, 2, HI)
    decay = jnp.where(lower, jnp.exp(jnp.where(lower, g_col - g_row, 0.0)), 0.0)
    kbf = _b16(k)
    a = jnp.where(strict, _bdot(_b16(kb), kbf, 2, 2) * decay, 0.0)
    tb = _b16(_tri_inv(a))
    u = _bdot(tb, _b16(vb), 2, 1)
    w = _bdot(tb, _b16(kb * eg), 2, 1)
    intra = jnp.where(lower, _bdot(_b16(q), kbf, 2, 2) * decay, 0.0)
    g_last = g[:, DN_C - 1:DN_C, :]
    return u, w, q * eg, k * jnp.exp(g_last - g), intra, jnp.exp(g_last)


def _dn_inter(u, w, qg, kdec, intra, egl, state):
    sb = _b16(state)
    v_new = u - _bdot(_b16(w), sb, 2, 1)
    o = _bdot(_b16(qg), sb, 2, 1) + _bdot(_b16(intra), _b16(v_new), 2, 1)
    return o, state * egl + _bdot(_b16(kdec), _b16(v_new), 1, 1)


def _inst(ref, base=0):
    per_head = [ref[:, base + DN_E * hd:base + DN_E * hd + DN_E].reshape(DN_G, DN_C, DN_E) for hd in range(DN_H)]
    return jnp.concatenate(per_head, axis=0)


def _inst_rows(ref):
    rows = [jnp.broadcast_to(ref[:, DN_E * hd:DN_E * hd + DN_E][None], (DN_G, 1, DN_E)) for hd in range(DN_H)]
    return jnp.concatenate(rows, axis=0)


def _put_inst(ref, val, width=DN_E, base=0):
    for hd in range(DN_H):
        ref[:, base + width * hd:base + width * hd + width] = val[DN_G * hd:DN_G * hd + DN_G].reshape(DN_G * DN_C, width)


@jax.custom_vjp
def _spread_gates(gates):
    t = gates.shape[0]
    return jnp.concatenate([jnp.broadcast_to(gates[:, j:j + 1], (t, DN_E)) for j in range(2 * DN_H)], axis=1)


def _spread_gates_fwd(gates):
    return _spread_gates(gates), None


def _spread_gates_bwd(_res, g):
    t = g.shape[0]
    lane = lax.broadcasted_iota(jnp.int32, (t, GATE_W), 1)
    out = jnp.zeros((t, GATE_W), F32)
    for j in range(2 * DN_H):
        s = jnp.sum(g[:, DN_E * j:DN_E * j + DN_E], axis=-1, keepdims=True)
        out = jnp.where(lane == j, s, out)
    return (out,)


_spread_gates.defvjp(_spread_gates_fwd, _spread_gates_bwd)


def _dn_intra_from_gates(q, k, v, gates, alog, dtb):
    wide = _spread_gates(gates)
    inst = lambda base: jnp.concatenate(
        [wide[:, base + DN_E * hd:base + DN_E * hd + DN_E].reshape(DN_G, DN_C, DN_E) for hd in range(DN_H)], axis=0)
    return _dn_intra(q, k, v, inst(0), inst(DN_W), alog, dtb)


def _intra_args(qkv_ref, g_ref, alog_ref, dtb_ref):
    return (_inst(qkv_ref), _inst(qkv_ref, DN_W), _inst(qkv_ref, 2 * DN_W), g_ref[...],
            _inst_rows(alog_ref), _inst_rows(dtb_ref))


def _intra_in_specs():
    t = DN_G * DN_C
    return [pl.BlockSpec((t, 3 * DN_W), lambda n: (n, 0)),
            pl.BlockSpec((t, GATE_W), lambda n: (n, R_G // GATE_W)),
            pl.BlockSpec((1, DN_W), lambda n: (0, 0)),
            pl.BlockSpec((1, DN_W), lambda n: (0, 0))]


def dn_intra_fwd(qkv, rest, alog_b, dtb_b, name, exch=None):
    t = DN_G * DN_C

    def body(qkv_ref, g_ref, alog_ref, dtb_ref, u_ref, w_ref, qg_ref, kd_ref, in_ref, egl_ref):
        u, w, qg, kdec, intra, egl = _dn_intra_from_gates(*_intra_args(qkv_ref, g_ref, alog_ref, dtb_ref))
        _put_inst(u_ref, u)
        _put_inst(w_ref, w.astype(BF16))
        _put_inst(qg_ref, qg.astype(BF16))
        _put_inst(kd_ref, kdec.astype(BF16))
        _put_inst(in_ref, intra.astype(BF16), DN_C)
        for hd in range(DN_H):
            egl_ref[:, DN_E * hd:DN_E * hd + DN_E] = egl[DN_G * hd:DN_G * hd + DN_G].reshape(DN_G, DN_E)

    row = lambda w_: pl.BlockSpec((t, w_), lambda n: (n, 0))
    return _call(
        body, name=name, grid=(N_CHUNK // DN_G,), in_specs=_intra_in_specs(),
        out_specs=[row(DN_W), row(DN_W), row(DN_W), row(DN_W), row(DN_H * DN_C),
                   pl.BlockSpec((DN_G, DN_W), lambda n: (n, 0))],
        out_shape=[SDS((SEQ, DN_W), F32), SDS((SEQ, DN_W), BF16), SDS((SEQ, DN_W), BF16), SDS((SEQ, DN_W), BF16),
                   SDS((SEQ, DN_H * DN_C), BF16), SDS((N_CHUNK, DN_W), F32)],
        scratch_shapes=[], sem=("arbitrary",), args=(qkv, rest, alog_b, dtb_b), exch=exch)


def dn_intra_bwd(qkv, rest, alog_b, dtb_b, du, dw, dqg, dkd, dintra, degl, name):
    t = DN_G * DN_C

    def body(qkv_ref, g_ref, alog_ref, dtb_ref, du_ref, dw_ref, dqg_ref, dkd_ref, din_ref, degl_ref,
             dqkv_ref, dg_ref, dalog_ref, ddtb_ref):
        @pl.when(pl.program_id(0) == 0)
        def _():
            dalog_ref[...] = jnp.zeros_like(dalog_ref)
            ddtb_ref[...] = jnp.zeros_like(ddtb_ref)

        _out, vjp = jax.vjp(_dn_intra_from_gates, *_intra_args(qkv_ref, g_ref, alog_ref, dtb_ref))
        d_in = jnp.concatenate([din_ref[:, DN_C * hd:DN_C * hd + DN_C].reshape(DN_G, DN_C, DN_C) for hd in range(DN_H)], axis=0)
        d_egl = jnp.concatenate([degl_ref[:, DN_E * hd:DN_E * hd + DN_E].reshape(DN_G, 1, DN_E) for hd in range(DN_H)], axis=0)
        dq, dk, dv, dg, dalog, ddtb = vjp((_inst(du_ref), _inst(dw_ref), _inst(dqg_ref), _inst(dkd_ref), d_in, d_egl))
        _put_inst(dqkv_ref, dq)
        _put_inst(dqkv_ref, dk, DN_E, DN_W)
        _put_inst(dqkv_ref, dv, DN_E, 2 * DN_W)
        dg_ref[...] = dg
        for hd in range(DN_H):
            sl = slice(DN_E * hd, DN_E * hd + DN_E)
            dalog_ref[:, sl] += jnp.sum(dalog[DN_G * hd:DN_G * hd + DN_G], axis=0)
            ddtb_ref[:, sl] += jnp.sum(ddtb[DN_G * hd:DN_G * hd + DN_G], axis=0)

    row = lambda w_: pl.BlockSpec((t, w_), lambda n: (n, 0))
    acc = pl.BlockSpec((1, DN_W), lambda n: (0, 0))
    return pl.pallas_call(
        body, name=name, grid=(N_CHUNK // DN_G,),
        in_specs=_intra_in_specs() + [row(DN_W), row(DN_W), row(DN_W), row(DN_W), row(DN_H * DN_C),
                                      pl.BlockSpec((DN_G, DN_W), lambda n: (n, 0))],
        out_specs=[row(3 * DN_W), row(GATE_W), acc, acc],
        out_shape=[SDS((SEQ, 3 * DN_W), F32), SDS((SEQ, GATE_W), F32), SDS((1, DN_W), F32), SDS((1, DN_W), F32)],
        compiler_params=_cparams(("arbitrary",)),
    )(qkv, rest, alog_b, dtb_b, du, dw, dqg, dkd, dintra, degl)


DN_RUN_LOG2 = 3
DN_RUN = 1 << DN_RUN_LOG2


def _chunk_rows(ref, c):
    return ref.at[pl.ds(DN_C * c, DN_C), :]


def _inter_args(u_ref, w_ref, qg_ref, kd_ref, in_ref, egl_ref, n, state):
    f = lambda r: _heads(r).astype(F32)
    intra = jnp.stack([in_ref[:, DN_C * hd:DN_C * hd + DN_C] for hd in range(DN_H)], axis=0).astype(F32)
    egl = _heads(egl_ref.at[pl.ds(n & (DN_G - 1), 1), :])
    return f(u_ref), f(w_ref), f(qg_ref), f(kd_ref), intra, egl, state


def dn_inter_fwd(u, w, qg, kdec, intra, egl, name, exch=None):
    def body(u_ref, w_ref, qg_ref, kd_ref, in_ref, egl_ref, o_ref, st_ref, state_scr):
        n2 = pl.program_id(0)

        @pl.when(n2 == 0)
        def _():
            state_scr[...] = jnp.zeros_like(state_scr)

        for c in range(DN_RUN):
            v = functools.partial(_chunk_rows, c=c)
            st = state_scr[...]
            st_ref[c] = st
            o, ns = _dn_inter(*_inter_args(v(u_ref), v(w_ref), v(qg_ref), v(kd_ref), v(in_ref), egl_ref,
                                           DN_RUN * n2 + c, st))
            _put_heads(v(o_ref), o)
            state_scr[...] = ns

    row = lambda w_: pl.BlockSpec((DN_RUN * DN_C, w_), lambda n: (n, 0))
    return _call(
        body, name=name, grid=(N_CHUNK // DN_RUN,),
        in_specs=[row(DN_W), row(DN_W), row(DN_W), row(DN_W), row(DN_H * DN_C),
                  pl.BlockSpec((DN_G, DN_W), lambda n: (n >> (DN_G_LOG2 - DN_RUN_LOG2), 0))],
        out_specs=[row(DN_W), pl.BlockSpec((DN_RUN, DN_H, DN_E, DN_E), lambda n: (n, 0, 0, 0))],
        out_shape=[SDS((SEQ, DN_W), F32), SDS((N_CHUNK, DN_H, DN_E, DN_E), F32)],
        scratch_shapes=[pltpu.VMEM((DN_H, DN_E, DN_E), F32)],
        sem=("arbitrary",), args=(u, w, qg, kdec, intra, egl), exch=exch)


def dn_inter_bwd(u, w, qg, kdec, intra, egl, states, do, name):
    last = N_CHUNK // DN_RUN - 1

    def body(u_ref, w_ref, qg_ref, kd_ref, in_ref, egl_ref, st_ref, do_ref,
             du_ref, dw_ref, dqg_ref, dkd_ref, din_ref, degl_ref, dstate_scr):
        s = pl.program_id(0)

        @pl.when(s == 0)
        def _():
            dstate_scr[...] = jnp.zeros_like(dstate_scr)

        for c in reversed(range(DN_RUN)):
            n = DN_RUN * (last - s) + c
            v = functools.partial(_chunk_rows, c=c)
            _out, vjp = jax.vjp(_dn_inter, *_inter_args(v(u_ref), v(w_ref), v(qg_ref), v(kd_ref), v(in_ref), egl_ref,
                                                        n, st_ref[c]))
            du, dw, dqg, dkd, din, degl, dst = vjp((_heads(v(do_ref)), dstate_scr[...]))
            _put_heads(v(du_ref), du)
            _put_heads(v(dw_ref), dw)
            _put_heads(v(dqg_ref), dqg)
            _put_heads(v(dkd_ref), dkd)
            for hd in range(DN_H):
                v(din_ref)[:, DN_C * hd:DN_C * hd + DN_C] = din[hd]
            row = n & (DN_G - 1)

            @pl.when(row == DN_G - 1)
            def _():
                degl_ref[...] = jnp.zeros_like(degl_ref)

            new_row = jnp.concatenate([degl[hd] for hd in range(DN_H)], axis=1)
            rows = lax.broadcasted_iota(jnp.int32, (DN_G, DN_W), 0)
            degl_ref[...] = jnp.where(rows == row, jnp.broadcast_to(new_row, (DN_G, DN_W)), degl_ref[...])
            dstate_scr[...] = dst

    rev = lambda w_: pl.BlockSpec((DN_RUN * DN_C, w_), lambda s: (last - s, 0))
    grp = pl.BlockSpec((DN_G, DN_W), lambda s: ((last - s) >> (DN_G_LOG2 - DN_RUN_LOG2), 0))
    return pl.pallas_call(
        body, name=name, grid=(N_CHUNK // DN_RUN,),
        in_specs=[rev(DN_W), rev(DN_W), rev(DN_W), rev(DN_W), rev(DN_H * DN_C), grp,
                  pl.BlockSpec((DN_RUN, DN_H, DN_E, DN_E), lambda s: (last - s, 0, 0, 0)), rev(DN_W)],
        out_specs=[rev(DN_W), rev(DN_W), rev(DN_W), rev(DN_W), rev(DN_H * DN_C), grp],
        out_shape=[SDS((SEQ, DN_W), F32)] * 4 + [SDS((SEQ, DN_H * DN_C), F32), SDS((N_CHUNK, DN_W), F32)],
        scratch_shapes=[pltpu.VMEM((DN_H, DN_E, DN_E), F32)],
        compiler_params=_cparams(("arbitrary",)),
    )(u, w, qg, kdec, intra, egl, states, do)


OUT_T = 512


def _pool_consts(rows_total, t0, halo_before):
    lane = lax.broadcasted_iota(jnp.int32, (rows_total, POOL_W), 1)
    row = lax.broadcasted_iota(jnp.int32, (rows_total, POOL_W), 0)
    grp = (lane >= 64).astype(jnp.int32) + (lane >= 128).astype(jnp.int32) + (lane >= 192).astype(jnp.int32)
    win = jnp.where(grp == 0, 2, jnp.where(grp == 1, 4, jnp.where(grp == 2, 8, 16)))
    pos = t0 + row - halo_before
    cnt = jnp.minimum(pos + 1, win).astype(F32)
    return grp, cnt


def _pool_select(grp, s2, s4, s8, s16):
    return jnp.where(grp == 0, s2, jnp.where(grp == 1, s4, jnp.where(grp == 2, s8, s16)))


def _pooled(u_ext, t0):
    n = u_ext.shape[0]
    grp, cnt = _pool_consts(n, t0, POOL_HALO)
    s2 = u_ext + pltpu.roll(u_ext, 1, 0)
    s4 = s2 + pltpu.roll(s2, 2, 0)
    s8 = s4 + pltpu.roll(s4, 4, 0)
    s16 = s8 + pltpu.roll(s8, 8, 0)
    out = _pool_select(grp, s2, s4, s8, s16) / jnp.maximum(cnt, 1.0) - u_ext
    return out[POOL_HALO:, :]


def _merge_weights(l1, l4, l16):
    m = jnp.maximum(jnp.maximum(l1, l4), l16)
    e1 = jnp.exp(l1 - m)
    e4 = jnp.exp(l4 - m)
    e16 = jnp.exp(l16 - m)
    inv = 1.0 / (e1 + e4 + e16)
    return e1 * inv, e4 * inv, e16 * inv


def _out_parts(ol1_ref, ol4_ref, ol16_ref, pu_ref, puh_ref, odn_ref, z_ref, wbd_ref, i, t):
    w1, w4, w16 = _merge_weights(_slabs_load(ol1_ref, 2, 2), _slabs_load(ol4_ref, 2, 2), _slabs_load(ol16_ref, 2, 2))
    ya = w1 * _slabs_load(ol1_ref, 0, 2) + w4 * _slabs_load(ol4_ref, 0, 2) + w16 * _slabs_load(ol16_ref, 0, 2)
    halo = jnp.where(i > 0, puh_ref[...], 0.0)
    pooled = _pooled(jnp.concatenate([halo, pu_ref[...]], axis=0), i * t)
    pw = _dot(pooled.astype(BF16), wbd_ref[...])
    return ya, pooled, pw, (w1, w4, w16)


def _out_specs_common(t):
    def row(w, cb=0):
        return pl.BlockSpec((t, w), lambda i: (i, cb))

    halo = pl.BlockSpec((POOL_HALO, POOL_W),
                        lambda i: (jnp.maximum(i * (t // POOL_HALO) - 1, 0), R_PU // POOL_W))
    full = lambda a, b: pl.BlockSpec((a, b), lambda i: (0, 0))
    return [_slab_spec(4, t), _slab_spec(4, t), _slab_spec(4, t), row(POOL_W, R_PU // POOL_W), halo, row(DN_W), row(DN_W, R_DZ // DN_W),
            full(POOL_W, POOL_W), full(1, POOL_W), full(1, DN_W), full(D_MODEL, D_MODEL)]


def mix_out_fwd(x, ol1, ol4, ol16, rest, odn, wbd, scale, onorm_b, wout, name):
    t = OUT_T

    def body(x_ref, ol1_ref, ol4_ref, ol16_ref, pu_ref, puh_ref, odn_ref, z_ref, wbd_ref, sc_ref, on_ref, wo_ref, o_ref):
        i = pl.program_id(0)
        ya, _pooled_v, pw, _w = _out_parts(ol1_ref, ol4_ref, ol16_ref, pu_ref, puh_ref, odn_ref, z_ref, wbd_ref, i, t)
        yb = pw * sc_ref[...]
        acc = x_ref[...] + _dot(ya.astype(BF16), wo_ref[0:256, :]) + _dot(yb.astype(BF16), wo_ref[256:512, :])
        for hd in range(DN_H):
            sl = slice(DN_E * hd, DN_E * hd + DN_E)
            oh, _r = _rms_stats(odn_ref[:, sl])
            z = z_ref[:, sl]
            yc = oh * on_ref[:, sl] * (z * _sigmoid(z))
            acc = acc + _dot(yc.astype(BF16), wo_ref[512 + DN_E * hd:512 + DN_E * hd + DN_E, :])
        o_ref[...] = acc

    return pl.pallas_call(
        body, name=name, grid=(SEQ // t,),
        in_specs=[pl.BlockSpec((t, D_MODEL), lambda i: (i, 0))] + _out_specs_common(t),
        out_specs=pl.BlockSpec((t, D_MODEL), lambda i: (i, 0)),
        out_shape=SDS((SEQ, D_MODEL), F32),
        compiler_params=_cparams(("arbitrary",)),
    )(x, ol1, ol4, ol16, rest, rest, odn, rest, wbd, scale, onorm_b, wout)


def mix_out_bwd(dxo, ol1, ol4, ol16, rest, odn, wbd, scale, onorm_b, wout, headsum, name):
    t = OUT_T

    def body(dxo_ref, ol1_ref, ol4_ref, ol16_ref, pu_ref, puh_ref, odn_ref, z_ref, wbd_ref, sc_ref, on_ref, wo_ref, hs_ref,
             dwo_ref, d1_ref, d4_ref, d16_ref, dpl_ref, dodn_ref, dz_ref, dsc_ref, don_ref, dwbd_ref):
        i = pl.program_id(0)

        @pl.when(i == 0)
        def _():
            dwo_ref[...] = jnp.zeros_like(dwo_ref)
            dsc_ref[...] = jnp.zeros_like(dsc_ref)
            don_ref[...] = jnp.zeros_like(don_ref)
            dwbd_ref[...] = jnp.zeros_like(dwbd_ref)

        ya, pooled, pw, (w1, w4, w16) = _out_parts(ol1_ref, ol4_ref, ol16_ref, pu_ref, puh_ref, odn_ref, z_ref, wbd_ref, i, t)
        sc = sc_ref[...]
        dxb = dxo_ref[...].astype(BF16)
        dwo_ref[0:256, :] += _dot_tn(ya.astype(BF16), dxb)
        dwo_ref[256:512, :] += _dot_tn((pw * sc).astype(BF16), dxb)
        dya = _dot_nt(dxb, wo_ref[0:256, :])
        o1 = _slabs_load(ol1_ref, 0, 2)
        o4 = _slabs_load(ol4_ref, 0, 2)
        o16 = _slabs_load(ol16_ref, 0, 2)
        hs = hs_ref[...]
        s1 = _dot(dya * o1, hs, HI)
        s4 = _dot(dya * o4, hs, HI)
        s16 = _dot(dya * o16, hs, HI)
        sbar = w1 * s1 + w4 * s4 + w16 * s16
        _slabs_store(d1_ref, 0, w1 * dya)
        _slabs_store(d1_ref, 2, w1 * (s1 - sbar))
        _slabs_store(d4_ref, 0, w4 * dya)
        _slabs_store(d4_ref, 2, w4 * (s4 - sbar))
        _slabs_store(d16_ref, 0, w16 * dya)
        _slabs_store(d16_ref, 2, w16 * (s16 - sbar))
        dyb = _dot_nt(dxb, wo_ref[256:512, :])
        dsc_ref[...] += jnp.sum(dyb * pw, axis=0, keepdims=True)
        dpw = (dyb * sc).astype(BF16)
        dwbd_ref[...] += _dot_tn(pooled.astype(BF16), dpw)
        dpl_ref[...] = _dot_nt(dpw, wbd_ref[...])
        for hd in range(DN_H):
            sl = slice(DN_E * hd, DN_E * hd + DN_E)
            rows_w = slice(512 + DN_E * hd, 512 + DN_E * hd + DN_E)
            oh, r = _rms_stats(odn_ref[:, sl])
            z = z_ref[:, sl]
            sg = _sigmoid(z)
            sz = z * sg
            nw = on_ref[:, sl]
            on = oh * nw
            dwo_ref[rows_w, :] += _dot_tn((on * sz).astype(BF16), dxb)
            dyc = _dot_nt(dxb, wo_ref[rows_w, :])
            dz_ref[:, sl] = dyc * on * (sg * (1.0 + z * (1.0 - sg)))
            dx, dw = _rms_bwd(oh, r, nw, dyc * sz)
            dodn_ref[:, sl] = dx
            don_ref[:, sl] += dw

    row = lambda w: pl.BlockSpec((t, w), lambda i: (i, 0))
    full = lambda a, b: pl.BlockSpec((a, b), lambda i: (0, 0))
    return pl.pallas_call(
        body, name=name, grid=(SEQ // t,),
        in_specs=[row(D_MODEL)] + _out_specs_common(t) + [full(ATT_W, ATT_W)],
        out_specs=[full(D_MODEL, D_MODEL), _slab_spec(4, t), _slab_spec(4, t), _slab_spec(4, t), row(POOL_W), row(DN_W), row(DN_W),
                   full(1, POOL_W), full(1, DN_W), full(POOL_W, POOL_W)],
        out_shape=[SDS((D_MODEL, D_MODEL), F32), SDS((4, SEQ, 128), F32), SDS((4, SEQ, 128), F32), SDS((4, SEQ, 128), F32),
                   SDS((SEQ, POOL_W), F32), SDS((SEQ, DN_W), F32), SDS((SEQ, DN_W), F32),
                   SDS((1, POOL_W), F32), SDS((1, DN_W), F32), SDS((POOL_W, POOL_W), F32)],
        compiler_params=_cparams(("arbitrary",)),
    )(dxo, ol1, ol4, ol16, rest, rest, odn, rest, wbd, scale, onorm_b, wout, headsum)


def pool_bwd(dpooled, name):
    t = 512
    nt = SEQ // t

    def body(d_ref, dn_ref, o_ref):
        i = pl.program_id(0)
        halo = jnp.where(i < nt - 1, dn_ref[...], 0.0)
        d_ext = jnp.concatenate([d_ref[...], halo], axis=0)
        n = t + POOL_HALO
        grp, cnt = _pool_consts(n, i * t, 0)
        dq = d_ext / cnt
        s2 = dq + pltpu.roll(dq, n - 1, 0)
        s4 = s2 + pltpu.roll(s2, n - 2, 0)
        s8 = s4 + pltpu.roll(s4, n - 4, 0)
        s16 = s8 + pltpu.roll(s8, n - 8, 0)
        o_ref[...] = (_pool_select(grp, s2, s4, s8, s16) - d_ext)[0:t, :]

    return pl.pallas_call(
        body, name=name, grid=(nt,),
        in_specs=[pl.BlockSpec((t, POOL_W), lambda i: (i, 0)),
                  pl.BlockSpec((POOL_HALO, POOL_W),
                               lambda i: (jnp.minimum((i + 1) * (t // POOL_HALO), SEQ // POOL_HALO - 1), 0))],
        out_specs=pl.BlockSpec((t, POOL_W), lambda i: (i, 0)),
        out_shape=SDS((SEQ, POOL_W), F32),
        compiler_params=_cparams(("arbitrary",)),
    )(dpooled, dpooled)


N_PEER = N_DEV - 1
ANY_SPEC = pl.BlockSpec(memory_space=pl.ANY)


class Exchange:
    def __init__(self, arrays, mode):
        self.arrays = list(arrays)
        self.mode = mode
        n = len(self.arrays)
        if mode == "scatter":
            self.out_shape = [SDS(a.shape, a.dtype) for a in self.arrays]
        else:
            self.out_shape = [SDS((N_DEV,) + a.shape, a.dtype) for a in self.arrays]
        self.scratch = [pltpu.SemaphoreType.DMA((n * N_PEER,)), pltpu.SemaphoreType.DMA((n * N_PEER,)),
                        pltpu.SemaphoreType.DMA((n,))]

    @staticmethod
    def _place():
        x, y, c = lax.axis_index("x"), lax.axis_index("y"), lax.axis_index("c")
        chips = [(1 - x, y), (x, 1 - y), (1 - x, 1 - y)]
        return x, y, c, chips

    @staticmethod
    def _copy(sems, a, k, src, dst, to):
        send_sems, recv_sems, _ = sems
        return pltpu.make_async_remote_copy(
            src_ref=src, dst_ref=dst, send_sem=send_sems.at[a * N_PEER + k], recv_sem=recv_sems.at[a * N_PEER + k],
            device_id=to, device_id_type=MESH)

    def _scatter_peers(self):
        x, y, c, _ = self._place()
        out = []
        for fx, fy, fc in ((0, 0, 1), (1, 0, 0), (0, 1, 0), (1, 1, 0), (1, 0, 1), (0, 1, 1), (1, 1, 1)):
            px, py, pc = x ^ fx, y ^ fy, c ^ fc
            out.append(((px, py, pc), 4 * px + 2 * py + pc))
        return 4 * x + 2 * y + c, out

    def _local(self, ins, outs, sems, a, me):
        src = ins[a].at[me] if self.mode == "scatter" else ins[a]
        return pltpu.make_async_copy(src, outs[a].at[me], sems[2].at[a])

    def start(self, ins, outs, sems):
        if self.mode == "scatter":
            me, peers = self._scatter_peers()
            for a in range(len(ins)):
                self._local(ins, outs, sems, a, me).start()
                for k, (peer, pidx) in enumerate(peers):
                    self._copy(sems, a, k, ins[a].at[pidx], outs[a].at[me], peer).start()
            return
        x, y, c, chips = self._place()
        me = 4 * x + 2 * y + c
        for a in range(len(ins)):
            self._local(ins, outs, sems, a, me).start()
            self._copy(sems, a, 0, ins[a], outs[a].at[me], (x, y, 1 - c)).start()
            for j, (cx, cy) in enumerate(chips):
                self._copy(sems, a, 1 + j, ins[a], outs[a].at[me], (cx, cy, c)).start()

    def finish(self, ins, outs, sems):
        n = len(ins)
        if self.mode == "scatter":
            me, peers = self._scatter_peers()
            for a in range(n):
                for k, (peer, pidx) in enumerate(peers):
                    self._copy(sems, a, k, ins[a].at[pidx], outs[a].at[pidx], peer).wait_recv()
            for a in range(n):
                for k, (peer, pidx) in enumerate(peers):
                    self._copy(sems, a, k, ins[a].at[pidx], outs[a].at[me], peer).wait_send()
                self._local(ins, outs, sems, a, me).wait()
            return
        x, y, c, chips = self._place()
        me = 4 * x + 2 * y + c
        sib = (x, y, 1 - c)
        for a in range(n):
            for j, (cx, cy) in enumerate(chips):
                blk = outs[a].at[4 * cx + 2 * cy + c]
                self._copy(sems, a, 1 + j, ins[a], blk, (cx, cy, c)).wait_recv()
                self._copy(sems, a, 4 + j, blk, blk, sib).start()
        for a in range(n):
            self._copy(sems, a, 0, ins[a], outs[a].at[4 * x + 2 * y + (1 - c)], sib).wait_recv()
            for j, (cx, cy) in enumerate(chips):
                blk = outs[a].at[4 * cx + 2 * cy + (1 - c)]
                self._copy(sems, a, 4 + j, blk, blk, sib).wait_recv()
        for a in range(n):
            for k in range(N_PEER):
                self._copy(sems, a, k, ins[a], outs[a].at[me], sib).wait_send()
            self._local(ins, outs, sems, a, me).wait()


def run_exchanges(exchs, name):
    counts = [len(e.arrays) for e in exchs]
    n = sum(counts)

    def body(*refs):
        ins, outs, sems = refs[:n], refs[n:2 * n], refs[2 * n:]
        parts, off = [], 0
        for j, c in enumerate(counts):
            parts.append((ins[off:off + c], outs[off:off + c], sems[3 * j:3 * j + 3]))
            off += c
        for e, p in zip(exchs, parts):
            e.start(*p)
        for e, p in zip(exchs, parts):
            e.finish(*p)

    res = pl.pallas_call(
        body, name=name, in_specs=[ANY_SPEC] * n, out_specs=[ANY_SPEC] * n,
        out_shape=[s for e in exchs for s in e.out_shape], scratch_shapes=[s for e in exchs for s in e.scratch],
    )(*[a for e in exchs for a in e.arrays])
    out, off = [], 0
    for c in counts:
        out.append(list(res[off:off + c]))
        off += c
    return out


def run_exchange(exch, name):
    return run_exchanges([exch], name)[0]


def _call(body, *, name, grid, in_specs, out_specs, out_shape, scratch_shapes, sem, args, exch=None):
    if exch is None:
        res = pl.pallas_call(body, name=name, grid=grid, in_specs=in_specs, out_specs=out_specs, out_shape=out_shape,
                             scratch_shapes=scratch_shapes, compiler_params=_cparams(sem))(*args)
        return res, None
    single = not isinstance(out_shape, (list, tuple))
    out_specs_l = [out_specs] if single else list(out_specs)
    out_shape_l = [out_shape] if single else list(out_shape)
    n_in, n_out, n_scr, m = len(in_specs), len(out_specs_l), len(scratch_shapes), len(exch.arrays)

    def wrapped(*refs):
        p = 0
        ins = refs[p:p + n_in]; p += n_in
        xin = refs[p:p + m]; p += m
        outs = refs[p:p + n_out]; p += n_out
        xout = refs[p:p + m]; p += m
        scr = refs[p:p + n_scr]; p += n_scr
        sems = refs[p:]
        ids = [pl.program_id(ax) for ax in range(len(grid))]
        first = functools.reduce(jnp.logical_and, [i == 0 for i in ids])
        last = functools.reduce(jnp.logical_and, [i == g - 1 for i, g in zip(ids, grid)])

        @pl.when(first)
        def _():
            exch.start(xin, xout, sems)

        body(*ins, *outs, *scr)

        @pl.when(last)
        def _():
            exch.finish(xin, xout, sems)

    res = pl.pallas_call(
        wrapped, name=name, grid=grid, in_specs=list(in_specs) + [ANY_SPEC] * m,
        out_specs=out_specs_l + [ANY_SPEC] * m, out_shape=out_shape_l + exch.out_shape,
        scratch_shapes=list(scratch_shapes) + exch.scratch, compiler_params=_cparams(sem),
    )(*args, *exch.arrays)
    outs = res[:n_out]
    return (outs[0] if single else outs), res[n_out:]


def _adam_math(w, g, m, v):
    m2 = ADAM_B1 * m + (1.0 - ADAM_B1) * g
    v2 = ADAM_B2 * v + (1.0 - ADAM_B2) * (g * g)
    m_hat = m2 / (1.0 - ADAM_B1 ** ADAM_STEP)
    v_hat = v2 / (1.0 - ADAM_B2 ** ADAM_STEP)
    delta = -ADAM_LR * (m_hat / (jnp.sqrt(v_hat) + ADAM_EPS) + ADAM_WD * w)
    return delta, m2, v2


ADAM_ROW_BLOCKS = 2


def adam_shard(parts0, parts1, w, m, v, name):
    _, r, c = w.shape
    rb = r // ADAM_ROW_BLOCKS

    def body(p0_ref, p1_ref, w_ref, m_ref, v_ref, g_ref, d_ref, m2_ref, v2_ref):
        def run(p_ref):
            g = p_ref[0].astype(F32)
            for i in range(1, N_DEV):
                g = g + p_ref[i].astype(F32)
            delta, m2, v2 = _adam_math(w_ref[0], g, m_ref[0], v_ref[0])
            g_ref[0] = g
            d_ref[0] = delta
            m2_ref[0] = m2
            v2_ref[0] = v2

        @pl.when(pl.program_id(0) == 0)
        def _():
            run(p0_ref)

        @pl.when(pl.program_id(0) == 1)
        def _():
            run(p1_ref)

    def p_spec(layer):
        row = (lambda l, j: jnp.where(l == 0, j, ADAM_ROW_BLOCKS - 1)) if layer == 0 else (lambda l, j: jnp.where(l == 1, j, 0))
        return pl.BlockSpec((N_DEV, rb, c), lambda l, j: (0, row(l, j), 0))

    blk = pl.BlockSpec((1, rb, c), lambda l, j: (l, j, 0))
    return pl.pallas_call(
        body, name=name, grid=(DEPTH, ADAM_ROW_BLOCKS),
        in_specs=[p_spec(0), p_spec(1), blk, blk, blk], out_specs=[blk] * 4,
        out_shape=[SDS(w.shape, F32)] * 4,
        compiler_params=_cparams(("arbitrary", "arbitrary")),
    )(parts0, parts1, w, m, v)


def parts_sum(parts0, parts1, name):
    _, r, c = parts0.shape

    def body(p0_ref, p1_ref, g_ref):
        def run(p_ref):
            g = p_ref[0].astype(F32)
            for i in range(1, N_DEV):
                g = g + p_ref[i].astype(F32)
            g_ref[0] = g

        @pl.when(pl.program_id(0) == 0)
        def _():
            run(p0_ref)

        @pl.when(pl.program_id(0) == 1)
        def _():
            run(p1_ref)

    full = pl.BlockSpec((N_DEV, r, c), lambda l: (0, 0, 0))
    return pl.pallas_call(
        body, name=name, grid=(DEPTH,), in_specs=[full, full],
        out_specs=pl.BlockSpec((1, r, c), lambda l: (l, 0, 0)), out_shape=SDS((DEPTH, r, c), F32),
        compiler_params=_cparams(("arbitrary",)),
    )(parts0, parts1)


def adam_given(g, w, m, v, name):
    _, r, c = w.shape

    def body(g_ref, w_ref, m_ref, v_ref, d_ref, m2_ref, v2_ref):
        delta, m2, v2 = _adam_math(w_ref[0], g_ref[0], m_ref[0], v_ref[0])
        d_ref[0] = delta
        m2_ref[0] = m2
        v2_ref[0] = v2

    blk = pl.BlockSpec((1, r, c), lambda l: (l, 0, 0))
    return pl.pallas_call(
        body, name=name, grid=(DEPTH,), in_specs=[blk] * 4, out_specs=[blk] * 3, out_shape=[SDS(w.shape, F32)] * 3,
        compiler_params=_cparams(("arbitrary",)),
    )(g, w, m, v)


def adam_small(parts, w, m, v, name):
    def body(p_ref, w_ref, m_ref, v_ref, g_ref, d_ref, m2_ref, v2_ref):
        g = p_ref[0]
        for i in range(1, N_DEV):
            g = g + p_ref[i]
        delta, m2, v2 = _adam_math(w_ref[...], g, m_ref[...], v_ref[...])
        g_ref[...] = g
        d_ref[...] = delta
        m2_ref[...] = m2
        v2_ref[...] = v2

    return pl.pallas_call(
        body, name=name, out_shape=[SDS(w.shape, F32)] * 4, compiler_params=_cparams(),
    )(parts, w, m, v)


def _rot_cols(w):
    w4 = w.reshape(w.shape[0], 4, 2, 32)
    return jnp.stack([-w4[:, :, 1], w4[:, :, 0]], axis=2).reshape(w.shape[0], ATT_W)


def _rot_cols_t(dw_rot):
    d4 = dw_rot.reshape(dw_rot.shape[0], 4, 2, 32)
    return jnp.stack([d4[:, :, 1], -d4[:, :, 0]], axis=2).reshape(dw_rot.shape[0], ATT_W)


def build_wext(w_in):
    aq, ak, av, pu = w_in[:, 0:256], w_in[:, 256:512], w_in[:, 512:768], w_in[:, 768:1024]
    dqkvz = w_in[:, 1024:3072]
    gates = jnp.pad(w_in[:, 3072:3080], ((0, 0), (0, GATE_W - 2 * DN_H)))
    return jnp.concatenate([aq, ak, av, _rot_cols(aq), _rot_cols(ak), dqkvz, gates, pu], axis=1)


def fold_dwext(d):
    b = EXT_ATT
    aq = d[:, 0:256] + _rot_cols_t(d[:, 768:1024])
    ak = d[:, 256:512] + _rot_cols_t(d[:, 1024:1280])
    av = d[:, 512:768]
    dqkvz = d[:, b:b + 2048]
    gates = d[:, b + R_G:b + R_G + 2 * DN_H]
    pu = d[:, b + R_PU:b + R_PU + 256]
    return jnp.concatenate([aq, ak, av, pu, dqkvz, gates], axis=1)


def _block_diag(pw):
    z = jnp.zeros((4, 64, 4, 64), pw.dtype)
    for g in range(4):
        z = z.at[g, :, g, :].set(pw[g])
    return z.reshape(POOL_W, POOL_W)


def _diag_blocks(m):
    m4 = m.reshape(4, 64, 4, 64)
    return jnp.stack([m4[g, :, g, :] for g in range(4)], axis=0)


def _lanes(v, reps):
    return jnp.repeat(v, reps)[None, :]


def layer_fwd(p, xa, cos, sin, l, host=None):
    host = host or {}

    def carried(key):
        return host[key][0] if key in host else None

    def done(key, xo):
        if key in host:
            host[key][1](xo)

    xb, xo = ffn_fwd(xa, p["n1"], *p["f1"], f"ffn1_fwd_{l}", carried("ffn1"))
    done("ffn1", xo)
    att, rest = mix_in_fwd(xb, p["nm"], p["wext"], cos, sin, f"mix_in_fwd_{l}")
    ols = [att_fwd_s(att, d, f"att_fwd_{l}_{d}") for d in DILATIONS]
    qkv = dn_prep_fwd(rest, p["conv"], f"dn_prep_fwd_{l}")
    dn, xo = dn_intra_fwd(qkv, rest, p["alog"], p["dtb"], f"dn_intra_fwd_{l}", carried("dn_intra"))
    done("dn_intra", xo)
    (odn, states), xo = dn_inter_fwd(*dn, f"dn_inter_fwd_{l}", carried("dn_inter"))
    done("dn_inter", xo)
    xc = mix_out_fwd(xb, ols[0], ols[1], ols[2], rest, odn, p["wbd"], p["scale"], p["onorm"], p["wout"], f"mix_out_fwd_{l}")
    xd, xo = ffn_fwd(xc, p["n2"], *p["f2"], f"ffn2_fwd_{l}", carried("ffn2"))
    done("ffn2", xo)
    return xd, dict(xa=xa, xb=xb, xc=xc, att=att, rest=rest, ols=ols, qkv=qkv, dn=dn, odn=odn, states=states)


def layer_bwd(p, s, dx, cos, sin, headsum, l, scatter=False, carry=None):
    blocks = lambda ws: [w_.reshape(N_DEV, FF_BLK, D_MODEL) for w_ in ws]
    (dx, *d_f2, d_n2), carried = ffn_bwd(s["xc"], dx, p["n2"], *p["f2"], f"ffn2_bwd_{l}", carry)
    (d_wout, dol1, dol4, dol16, dpooled, dodn, dz, dscale, donorm, dwbd) = mix_out_bwd(
        dx, s["ols"][0], s["ols"][1], s["ols"][2], s["rest"], s["odn"], p["wbd"], p["scale"], p["onorm"], p["wout"],
        headsum, f"mix_out_bwd_{l}")
    dpu = pool_bwd(dpooled, f"pool_bwd_{l}")
    f2 = blocks(d_f2)
    d_dn = dn_inter_bwd(*s["dn"], s["states"], dodn, f"dn_inter_bwd_{l}")
    dqkv, dg, dalog, ddtb = dn_intra_bwd(s["qkv"], s["rest"], p["alog"], p["dtb"], *d_dn, f"dn_intra_bwd_{l}")
    d_dqkv, dconv = dn_prep_bwd(s["rest"], p["conv"], dqkv, f"dn_prep_bwd_{l}")
    datts = [att_bwd_s(s["att"], ol, dol, d, f"att_bwd_{l}_{d}")
             for d, ol, dol in zip(DILATIONS, s["ols"], (dol1, dol4, dol16))]
    dproj = assemble_dproj(datts, cos, sin, d_dqkv, dz, dg, dpu, f"assemble_dproj_{l}")
    dx, d_wext, d_nm = linear_bwd(s["xb"], dx, p["nm"], dproj, p["wext"], f"mix_in_bwd_{l}")
    d_win = fold_dwext(d_wext).reshape(D_MODEL, N_DEV, IN_BLK).transpose(1, 0, 2).astype(BF16)
    io = [d_win, d_wout.reshape(N_DEV, D_MODEL // N_DEV, D_MODEL).astype(BF16)]
    (dx, *d_f1, d_n1), xo = ffn_bwd(s["xa"], dx, p["n1"], *p["f1"], f"ffn1_bwd_{l}",
                                    Exchange(f2 + io, "scatter") if scatter else None)
    if scatter:
        f2, io = list(xo[:3]), list(xo[3:])
    big = dict(f1=blocks(d_f1), f2=f2, io=io)
    small = dict(ffn1_norm=d_n1[0], mix_norm=d_nm[0], ffn2_norm=d_n2[0], pool_w=_diag_blocks(dwbd),
                 pool_scale=dscale[0], dn_a_log=dalog.reshape(DN_H, DN_E).sum(-1),
                 dn_dt_bias=ddtb.reshape(DN_H, DN_E).sum(-1),
                 dn_out_norm=donorm.reshape(DN_H, DN_E).sum(0), dn_conv_w=dconv)
    return dx, big, small, carried


def small_operands(l, pool_w, pool_scale, dn_out_norm, dn_a_log, dn_dt_bias, ffn1_norm, mix_norm, ffn2_norm):
    return dict(
        wbd=_block_diag(pool_w[l]).astype(BF16),
        scale=pool_scale[l][None, :],
        onorm=jnp.tile(dn_out_norm[l], DN_H)[None, :],
        alog=_lanes(dn_a_log[l], DN_E),
        dtb=_lanes(dn_dt_bias[l], DN_E),
        n1=ffn1_norm[l][None, :], nm=mix_norm[l][None, :], n2=ffn2_norm[l][None, :])


def set_mixer_weights(p, win_g, wout_g, conv_g):
    p["wext"] = build_wext(win_g.transpose(1, 0, 2).reshape(D_MODEL, IN_W))
    p["wout"] = wout_g.reshape(D_MODEL, D_MODEL)
    p["conv"] = conv_g.transpose(1, 0, 2).reshape(DN_CONV, 3 * DN_W)


def rope_tables(pos):
    inv_freq = 10000.0 ** (-jnp.arange(0, ATT_E, 2, dtype=F32) / ATT_E)
    ang = pos.astype(F32)[:, None] * inv_freq
    return jnp.tile(jnp.cos(ang), (1, 8)), jnp.tile(jnp.sin(ang), (1, 8))


def head_sum_matrix():
    return jnp.kron(jnp.eye(4, dtype=F32), jnp.ones((ATT_E, ATT_E), F32))


SMALL_NAMES = ("ffn1_norm", "mix_norm", "ffn2_norm", "pool_w", "pool_scale", "dn_a_log", "dn_dt_bias",
               "dn_out_norm", "final_norm", "dn_conv_w")


PACK_UNIT = 8 * 128


def _pack_rows(n):
    return -(-n // PACK_UNIT) * 8


def _pack(parts):
    rows = []
    for p in parts:
        flat = p.reshape(-1)
        r = _pack_rows(flat.shape[0])
        rows.append(jnp.pad(flat, (0, r * 128 - flat.shape[0])).reshape(r, 128))
    return jnp.concatenate(rows, axis=0)


def _unpack(packed, shapes):
    out, row = [], 0
    for s in shapes:
        n = math.prod(s)
        r = _pack_rows(n)
        out.append(packed[row:row + r].reshape(-1)[:n].reshape(s))
        row += r
    return out


def kernel(x, positions, ffn1_norm, ffn1_w_gate, ffn1_w_up, ffn1_w_down, mix_norm, w_in, pool_w, pool_scale, dn_conv_w, dn_a_log, dn_dt_bias, dn_out_norm, w_out, ffn2_norm, ffn2_w_gate, ffn2_w_up, ffn2_w_down, final_norm, loss_target, m_ffn1_norm, m_ffn1_w_gate, m_ffn1_w_up, m_ffn1_w_down, m_mix_norm, m_w_in, m_pool_w, m_pool_scale, m_dn_conv_w, m_dn_a_log, m_dn_dt_bias, m_dn_out_norm, m_w_out, m_ffn2_norm, m_ffn2_w_gate, m_ffn2_w_up, m_ffn2_w_down, m_final_norm, v_ffn1_norm, v_ffn1_w_gate, v_ffn1_w_up, v_ffn1_w_down, v_mix_norm, v_w_in, v_pool_w, v_pool_scale, v_dn_conv_w, v_dn_a_log, v_dn_dt_bias, v_dn_out_norm, v_w_out, v_ffn2_norm, v_ffn2_w_gate, v_ffn2_w_up, v_ffn2_w_down, v_final_norm):
    me = 4 * lax.axis_index("x") + 2 * lax.axis_index("y") + lax.axis_index("c")
    x0 = x[0]
    target = loss_target[0]

    cos, sin = rope_tables(positions[0])
    headsum = head_sum_matrix()

    layers = [small_operands(l, pool_w, pool_scale, dn_out_norm, dn_a_log, dn_dt_bias, ffn1_norm, mix_norm, ffn2_norm)
              for l in range(DEPTH)]

    def whole(gathered):
        return gathered.reshape(D_FF, D_MODEL)

    def gather_ffn1(l):
        def on_done(xo):
            layers[l]["f1"] = tuple(whole(g) for g in xo)
        return Exchange(ffn_shard_operands(ffn1_w_gate[l], ffn1_w_up[l], ffn1_w_down[l]), "gather"), on_done

    def gather_mixer(l):
        def on_done(xo):
            set_mixer_weights(layers[l], *xo)
        return Exchange([w_in[l].astype(BF16), w_out[l].astype(BF16), dn_conv_w[l]], "gather"), on_done

    gathered_f2 = {}

    def gather_ffn2_part(l, part):
        def on_done(xo):
            gathered_f2[(l, part)] = [whole(g) for g in xo]
            if (l, 0) in gathered_f2 and (l, 1) in gathered_f2:
                layers[l]["f2"] = tuple(gathered_f2[(l, 0)] + gathered_f2[(l, 1)])
        ops = ffn_shard_operands(ffn2_w_gate[l], ffn2_w_up[l], ffn2_w_down[l])
        return Exchange(ops[:2] if part == 0 else ops[2:], "gather"), on_done

    first, on_first = gather_ffn1(0)
    on_first(run_exchange(first, "gather_ffn1_0"))
    saved = []
    xa = x0
    for l in range(DEPTH):
        host = {"ffn1": gather_mixer(l), "dn_intra": gather_ffn2_part(l, 0), "dn_inter": gather_ffn2_part(l, 1)}
        if l + 1 < DEPTH:
            host["ffn2"] = gather_ffn1(l + 1)
        xa, s = layer_fwd(layers[l], xa, cos, sin, l, host)
        saved.append(s)

    loss_row, dx, d_final = loss_head(xa, final_norm[None, :], target, "loss_head")
    loss = lax.psum(loss_row[0, 0], ("x", "y", "c"))

    small = {}
    big_parts = [None] * DEPTH
    carry = None
    for l in reversed(range(DEPTH)):
        dx, big, small[l], carried = layer_bwd(layers[l], saved[l], dx, cos, sin, headsum, l, True, carry)
        if carried is not None:
            big_parts[l + 1]["f1"] = list(carried)
        big_parts[l] = big
        carry = Exchange(big["f1"], "scatter")
    grad_x = dx[None]

    small_shapes = {"ffn1_norm": (DEPTH, D_MODEL), "mix_norm": (DEPTH, D_MODEL), "ffn2_norm": (DEPTH, D_MODEL),
                    "pool_w": (DEPTH, 4, 64, 64), "pool_scale": (DEPTH, POOL_W), "dn_a_log": (DEPTH, DN_H),
                    "dn_dt_bias": (DEPTH, DN_H), "dn_out_norm": (DEPTH, DN_E), "final_norm": (D_MODEL,),
                    "dn_conv_w": (DEPTH, DN_CONV, 3 * DN_W)}
    g_small = {n: (d_final[0] if n == "final_norm" else jnp.stack([small[l][n] for l in range(DEPTH)]))
               for n in SMALL_NAMES}
    f1_parts, (small_parts,) = run_exchanges(
        [carry, Exchange([_pack([g_small[n] for n in SMALL_NAMES])], "gather")], "scatter_ffn1_0_gather_small")
    big_parts[0]["f1"] = f1_parts

    def conv_full(a):
        return lax.dynamic_update_slice(jnp.zeros((DEPTH, DN_CONV, 3 * DN_W), F32), a, (0, 0, me * (3 * DN_W // N_DEV)))

    given = dict(ffn1_norm=(ffn1_norm, m_ffn1_norm, v_ffn1_norm), mix_norm=(mix_norm, m_mix_norm, v_mix_norm),
                 ffn2_norm=(ffn2_norm, m_ffn2_norm, v_ffn2_norm), pool_w=(pool_w, m_pool_w, v_pool_w),
                 pool_scale=(pool_scale, m_pool_scale, v_pool_scale), dn_a_log=(dn_a_log, m_dn_a_log, v_dn_a_log),
                 dn_dt_bias=(dn_dt_bias, m_dn_dt_bias, v_dn_dt_bias),
                 dn_out_norm=(dn_out_norm, m_dn_out_norm, v_dn_out_norm),
                 final_norm=(final_norm, m_final_norm, v_final_norm),
                 dn_conv_w=(conv_full(dn_conv_w), conv_full(m_dn_conv_w), conv_full(v_dn_conv_w)))
    packed_wmv = [_pack([given[n][k] for n in SMALL_NAMES]) for k in range(3)]
    small_out = adam_small(small_parts, *packed_wmv, "adam_small")
    shapes = [small_shapes[n] for n in SMALL_NAMES]
    small_res = {n: [] for n in SMALL_NAMES}
    for arr in small_out:
        for n, v_ in zip(SMALL_NAMES, _unpack(arr, shapes)):
            if n == "dn_conv_w":
                v_ = lax.dynamic_slice(v_, (0, 0, me * (3 * DN_W // N_DEV)), (DEPTH, DN_CONV, 3 * DN_W // N_DEV))
            small_res[n].append(v_)

    def parts_of(group, idx):
        return [big_parts[l][group][idx] for l in range(DEPTH)]

    def adam_transposed(group, idx, w, m, v, name):
        g = parts_sum(*parts_of(group, idx), f"sum_{name}").transpose(0, 2, 1)
        return [g] + list(adam_given(g, w, m, v, f"adam_{name}"))

    big_res = dict(
        ffn1_w_gate=adam_transposed("f1", 0, ffn1_w_gate, m_ffn1_w_gate, v_ffn1_w_gate, "ffn1_gate"),
        ffn1_w_up=adam_transposed("f1", 1, ffn1_w_up, m_ffn1_w_up, v_ffn1_w_up, "ffn1_up"),
        ffn1_w_down=adam_shard(*parts_of("f1", 2), ffn1_w_down, m_ffn1_w_down, v_ffn1_w_down, "adam_ffn1_down"),
        ffn2_w_gate=adam_transposed("f2", 0, ffn2_w_gate, m_ffn2_w_gate, v_ffn2_w_gate, "ffn2_gate"),
        ffn2_w_up=adam_transposed("f2", 1, ffn2_w_up, m_ffn2_w_up, v_ffn2_w_up, "ffn2_up"),
        ffn2_w_down=adam_shard(*parts_of("f2", 2), ffn2_w_down, m_ffn2_w_down, v_ffn2_w_down, "adam_ffn2_down"),
        w_in=adam_shard(*parts_of("io", 0), w_in, m_w_in, v_w_in, "adam_w_in"),
        w_out=adam_shard(*parts_of("io", 1), w_out, m_w_out, v_w_out, "adam_w_out"),
    )

    order = ("ffn1_norm", "ffn1_w_gate", "ffn1_w_up", "ffn1_w_down", "mix_norm", "w_in", "pool_w", "pool_scale",
             "dn_conv_w", "dn_a_log", "dn_dt_bias", "dn_out_norm", "w_out", "ffn2_norm", "ffn2_w_gate", "ffn2_w_up",
             "ffn2_w_down", "final_norm")
    res = {**small_res, **big_res}
    outs = [loss, grad_x]
    for k in range(4):
        outs.extend(res[n][k] for n in order)
    return tuple(outs)
```

```python
import functools
import math

import jax
import jax.numpy as jnp
from jax import lax
from jax.experimental import pallas as pl
from jax.experimental.pallas import tpu as pltpu

F32 = jnp.float32
BF16 = jnp.bfloat16
HI = lax.Precision.HIGHEST
INV_PREC = lax.Precision.HIGH
SDS = jax.ShapeDtypeStruct

N_DEV = 8
SEQ = 4096
D_MODEL = 1024
DEPTH = 2
D_FF = 2816
FF_BLK = D_FF // N_DEV
ATT_W = 256
ATT_E = 64
ATT_BLK = 128
DILATIONS = (1, 4, 16)
POOL_W = 256
POOL_HALO = 16
DN_W = 512
DN_H = 4
DN_E = 128
DN_C = 64
N_CHUNK = SEQ // DN_C
IN_W = 3080
IN_BLK = IN_W // N_DEV
EPS = 1e-6
EXT_ATT = 1280
GATE_W = 256
EXT_REST = 4 * DN_W + GATE_W + POOL_W
EXT_W = EXT_ATT + EXT_REST
R_DQKV, R_DZ, R_G, R_PU = 0, 1536, 2048, 2304

ADAM_LR, ADAM_B1, ADAM_B2, ADAM_EPS, ADAM_WD, ADAM_STEP = 0.001, 0.9, 0.999, 1e-08, 0.01, 10

VMEM_LIMIT = 60 * 1024 * 1024
MESH = pl.DeviceIdType.MESH


def _cparams(sem=None):
    kw = dict(vmem_limit_bytes=VMEM_LIMIT)
    if sem is not None:
        kw["dimension_semantics"] = sem
    return pltpu.CompilerParams(**kw)


def _dot(a, b, prec=None):
    return jnp.dot(a, b, preferred_element_type=F32, precision=prec)


def _dot_nt(a, b, prec=None):
    return lax.dot_general(a, b, (((1,), (1,)), ((), ())), preferred_element_type=F32, precision=prec)


def _dot_tn(a, b, prec=None):
    return lax.dot_general(a, b, (((0,), (0,)), ((), ())), preferred_element_type=F32, precision=prec)


def _sigmoid(x):
    return jax.nn.sigmoid(x)


def _rms_stats(x):
    r = lax.rsqrt(jnp.mean(x * x, axis=-1, keepdims=True) + EPS)
    return x * r, r


def _rms_bwd(xh, r, w, dh):
    dxh = dh * w
    dx = r * (dxh - xh * jnp.mean(dxh * xh, axis=-1, keepdims=True))
    return dx, jnp.sum(dh * xh, axis=0, keepdims=True)


FFN_T_FWD = 2048
FFN_T_BWD = 512
FF_TILE = 256
N_FF_TILE = D_FF // FF_TILE


def ffn_shard_operands(gate, up, down):
    return [gate.T.astype(BF16), up.T.astype(BF16), down.astype(BF16)]


def ffn_fwd(x, nw, wgt, wut, wd, name, exch=None):
    t = FFN_T_FWD

    def body(x_ref, nw_ref, wgt_ref, wut_ref, wd_ref, o_ref, h_scr, acc_scr):
        k = pl.program_id(1)

        @pl.when(k == 0)
        def _():
            xh, _r = _rms_stats(x_ref[...])
            h_scr[...] = (xh * nw_ref[...]).astype(BF16)
            acc_scr[...] = jnp.zeros_like(acc_scr)

        h = h_scr[...]
        hg = _dot_nt(h, wgt_ref[...])
        hu = _dot_nt(h, wut_ref[...])
        a = (hg * _sigmoid(hg) * hu).astype(BF16)
        acc_scr[...] += _dot(a, wd_ref[...])

        @pl.when(k == N_FF_TILE - 1)
        def _():
            o_ref[...] = x_ref[...] + 0.5 * acc_scr[...]

    w_spec = pl.BlockSpec((FF_TILE, D_MODEL), lambda i, k: (k, 0))
    return _call(
        body, name=name, grid=(SEQ // t, N_FF_TILE),
        in_specs=[pl.BlockSpec((t, D_MODEL), lambda i, k: (i, 0)),
                  pl.BlockSpec((1, D_MODEL), lambda i, k: (0, 0)), w_spec, w_spec, w_spec],
        out_specs=pl.BlockSpec((t, D_MODEL), lambda i, k: (i, 0)),
        out_shape=SDS((SEQ, D_MODEL), F32),
        scratch_shapes=[pltpu.VMEM((t, D_MODEL), BF16), pltpu.VMEM((t, D_MODEL), F32)],
        sem=("arbitrary", "arbitrary"), args=(x, nw, wgt, wut, wd), exch=exch)


FFN_T_DW = 1024


def ffn_bwd(x, dxo, nw, wgt, wut, wd, name, exch=None):
    t = FFN_T_BWD
    nt = SEQ // t

    def body(x_ref, dxo_ref, nw_ref, wgt_ref, wut_ref, wd_ref, dx_ref, a_ref, dhg_ref, dhu_ref, h_ref, dy_ref, dnw_ref,
             dh_scr, h_scr):
        k = pl.program_id(0)
        i = pl.program_id(1)
        rows = pl.ds(pl.multiple_of(i * t, t), t)
        nw_v = nw_ref[...]
        dy = (0.5 * dxo_ref[...]).astype(BF16)

        @pl.when(k == 0)
        def _():
            xh0, _r0 = _rms_stats(x_ref[...])
            h0 = (xh0 * nw_v).astype(BF16)
            h_scr[rows, :] = h0
            h_ref[...] = h0
            dy_ref[...] = dy

        h = h_scr[rows, :]
        wgt = wgt_ref[...]
        wut = wut_ref[...]
        hg = _dot_nt(h, wgt)
        hu = _dot_nt(h, wut)
        sg = _sigmoid(hg)
        sil = hg * sg
        a_ref[...] = (sil * hu).astype(BF16)
        da = _dot_nt(dy, wd_ref[...])
        dhu = (da * sil).astype(BF16)
        dhg = (da * hu * (sg * (1.0 + hg * (1.0 - sg)))).astype(BF16)
        dhu_ref[...] = dhu
        dhg_ref[...] = dhg
        dh = _dot(dhg, wgt) + _dot(dhu, wut)

        @pl.when(k == 0)
        def _():
            dh_scr[rows, :] = dh

        @pl.when(k > 0)
        def _():
            dh_scr[rows, :] += dh

        @pl.when(jnp.logical_and(k == 0, i == 0))
        def _():
            dnw_ref[...] = jnp.zeros_like(dnw_ref)

        @pl.when(k == N_FF_TILE - 1)
        def _():
            xh, r = _rms_stats(x_ref[...])
            dx, dw = _rms_bwd(xh, r, nw_v, dh_scr[rows, :])
            dx_ref[...] = dxo_ref[...] + dx
            dnw_ref[...] += dw

    last = N_FF_TILE - 1
    w_spec = pl.BlockSpec((FF_TILE, D_MODEL), lambda k, i: (k, 0))
    act_spec = pl.BlockSpec((t, FF_TILE), lambda k, i: (i, k))
    once = pl.BlockSpec((t, D_MODEL), lambda k, i: (jnp.where(k == 0, i, nt - 1), 0))
    (dx, a, dhg, dhu, h, dy, dnw), xo = _call(
        body, name=name, grid=(N_FF_TILE, nt),
        in_specs=[pl.BlockSpec((t, D_MODEL), lambda k, i: (i, 0)),
                  pl.BlockSpec((t, D_MODEL), lambda k, i: (i, 0)),
                  pl.BlockSpec((1, D_MODEL), lambda k, i: (0, 0)), w_spec, w_spec, w_spec],
        out_specs=[pl.BlockSpec((t, D_MODEL), lambda k, i: (jnp.where(k == last, i, 0), 0)),
                   act_spec, act_spec, act_spec, once, once, pl.BlockSpec((1, D_MODEL), lambda k, i: (0, 0))],
        out_shape=[SDS((SEQ, D_MODEL), F32), SDS((SEQ, D_FF), BF16), SDS((SEQ, D_FF), BF16), SDS((SEQ, D_FF), BF16),
                   SDS((SEQ, D_MODEL), BF16), SDS((SEQ, D_MODEL), BF16), SDS((1, D_MODEL), F32)],
        scratch_shapes=[pltpu.VMEM((SEQ, D_MODEL), F32), pltpu.VMEM((SEQ, D_MODEL), BF16)],
        sem=("arbitrary", "arbitrary"), args=(x, dxo, nw, wgt, wut, wd), exch=exch)

    tw = FFN_T_DW
    ntw = SEQ // tw

    def w_body(a_ref, dhg_ref, dhu_ref, h_ref, dy_ref, dwgt_ref, dwut_ref, dwd_ref, ag_scr, au_scr, ad_scr):
        i = pl.program_id(1)
        p_g = _dot_tn(dhg_ref[...], h_ref[...])
        p_u = _dot_tn(dhu_ref[...], h_ref[...])
        p_d = _dot_tn(a_ref[...], dy_ref[...])

        @pl.when(i == 0)
        def _():
            ag_scr[...] = p_g
            au_scr[...] = p_u
            ad_scr[...] = p_d

        @pl.when(i > 0)
        def _():
            ag_scr[...] += p_g
            au_scr[...] += p_u
            ad_scr[...] += p_d

        @pl.when(i == ntw - 1)
        def _():
            dwgt_ref[...] = ag_scr[...].astype(BF16)
            dwut_ref[...] = au_scr[...].astype(BF16)
            dwd_ref[...] = ad_scr[...].astype(BF16)

    col = pl.BlockSpec((tw, FF_TILE), lambda k, i: (i, k))
    tok = pl.BlockSpec((tw, D_MODEL), lambda k, i: (i, 0))
    dwgt, dwut, dwd = pl.pallas_call(
        w_body, name=name + "_dw", grid=(N_FF_TILE, ntw),
        in_specs=[col, col, col, tok, tok], out_specs=[w_spec, w_spec, w_spec],
        out_shape=[SDS((D_FF, D_MODEL), BF16)] * 3,
        scratch_shapes=[pltpu.VMEM((FF_TILE, D_MODEL), F32)] * 3,
        compiler_params=_cparams(("arbitrary", "arbitrary")),
    )(a, dhg, dhu, h, dy)
    return [dx, dwgt, dwut, dwd, dnw], xo


def ffn_bwd_fused(x, dxo, nw, wgt, wut, wd, name, exch=None):
    t = FFN_T_BWD
    nt = SEQ // t

    def body(x_ref, dxo_ref, nw_ref, wgt_ref, wut_ref, wd_ref, dx_ref, dwgt_ref, dwut_ref, dwd_ref, dnw_ref,
             dh_scr, ag_scr, au_scr, ad_scr, h_scr):
        k = pl.program_id(0)
        i = pl.program_id(1)
        rows = pl.ds(pl.multiple_of(i * t, t), t)
        nw_v = nw_ref[...]

        @pl.when(k == 0)
        def _():
            xh0, _r0 = _rms_stats(x_ref[...])
            h_scr[rows, :] = (xh0 * nw_v).astype(BF16)

        h = h_scr[rows, :]
        dy = (0.5 * dxo_ref[...]).astype(BF16)
        wgt = wgt_ref[...]
        wut = wut_ref[...]
        hg = _dot_nt(h, wgt)
        hu = _dot_nt(h, wut)
        sg = _sigmoid(hg)
        sil = hg * sg
        a = (sil * hu).astype(BF16)
        da = _dot_nt(dy, wd_ref[...])
        dhu = (da * sil).astype(BF16)
        dhg = (da * hu * (sg * (1.0 + hg * (1.0 - sg)))).astype(BF16)
        p_d = _dot_tn(a, dy)
        p_g = _dot_tn(dhg, h)
        p_u = _dot_tn(dhu, h)
        dh = _dot(dhg, wgt) + _dot(dhu, wut)

        @pl.when(i == 0)
        def _():
            ad_scr[...] = p_d
            ag_scr[...] = p_g
            au_scr[...] = p_u

        @pl.when(i > 0)
        def _():
            ad_scr[...] += p_d
            ag_scr[...] += p_g
            au_scr[...] += p_u

        @pl.when(i == nt - 1)
        def _():
            dwd_ref[...] = ad_scr[...].astype(BF16)
            dwgt_ref[...] = ag_scr[...].astype(BF16)
            dwut_ref[...] = au_scr[...].astype(BF16)

        @pl.when(k == 0)
        def _():
            dh_scr[rows, :] = dh

        @pl.when(k > 0)
        def _():
            dh_scr[rows, :] += dh

        @pl.when(jnp.logical_and(k == 0, i == 0))
        def _():
            dnw_ref[...] = jnp.zeros_like(dnw_ref)

        @pl.when(k == N_FF_TILE - 1)
        def _():
            xh, r = _rms_stats(x_ref[...])
            dx, dw = _rms_bwd(xh, r, nw_v, dh_scr[rows, :])
            dx_ref[...] = dxo_ref[...] + dx
            dnw_ref[...] += dw

    last = N_FF_TILE - 1
    w_spec = pl.BlockSpec((FF_TILE, D_MODEL), lambda k, i: (k, 0))
    return _call(
        body, name=name, grid=(N_FF_TILE, nt),
        in_specs=[pl.BlockSpec((t, D_MODEL), lambda k, i: (i, 0)),
                  pl.BlockSpec((t, D_MODEL), lambda k, i: (i, 0)),
                  pl.BlockSpec((1, D_MODEL), lambda k, i: (0, 0)), w_spec, w_spec, w_spec],
        out_specs=[pl.BlockSpec((t, D_MODEL), lambda k, i: (jnp.where(k == last, i, 0), 0)),
                   w_spec, w_spec, w_spec, pl.BlockSpec((1, D_MODEL), lambda k, i: (0, 0))],
        out_shape=[SDS((SEQ, D_MODEL), F32), SDS((D_FF, D_MODEL), BF16), SDS((D_FF, D_MODEL), BF16),
                   SDS((D_FF, D_MODEL), BF16), SDS((1, D_MODEL), F32)],
        scratch_shapes=[pltpu.VMEM((SEQ, D_MODEL), F32), pltpu.VMEM((FF_TILE, D_MODEL), F32),
                        pltpu.VMEM((FF_TILE, D_MODEL), F32), pltpu.VMEM((FF_TILE, D_MODEL), F32),
                        pltpu.VMEM((SEQ, D_MODEL), BF16)],
        sem=("arbitrary", "arbitrary"), args=(x, dxo, nw, wgt, wut, wd), exch=exch)


def loss_head(x, fw, target, name):
    t = 512

    def body(x_ref, fw_ref, tg_ref, loss_ref, dx_ref, dfw_ref):
        i = pl.program_id(0)
        xh, r = _rms_stats(x_ref[...])
        w = fw_ref[...]
        err = xh * w - tg_ref[...]
        part = 0.5 * jnp.sum(jnp.sum(err * err, axis=-1, keepdims=True), axis=0, keepdims=True) / D_MODEL
        dx, dw = _rms_bwd(xh, r, w, err * (1.0 / D_MODEL))
        dx_ref[...] = dx

        @pl.when(i == 0)
        def _():
            loss_ref[...] = jnp.zeros_like(loss_ref)
            dfw_ref[...] = jnp.zeros_like(dfw_ref)

        loss_ref[...] += jnp.broadcast_to(part, loss_ref.shape)
        dfw_ref[...] += dw

    return pl.pallas_call(
        body, name=name, grid=(SEQ // t,),
        in_specs=[pl.BlockSpec((t, D_MODEL), lambda i: (i, 0)),
                  pl.BlockSpec((1, D_MODEL), lambda i: (0, 0)),
                  pl.BlockSpec((t, D_MODEL), lambda i: (i, 0))],
        out_specs=[pl.BlockSpec((1, 128), lambda i: (0, 0)),
                   pl.BlockSpec((t, D_MODEL), lambda i: (i, 0)),
                   pl.BlockSpec((1, D_MODEL), lambda i: (0, 0))],
        out_shape=[SDS((1, 128), F32), SDS((SEQ, D_MODEL), F32), SDS((1, D_MODEL), F32)],
        compiler_params=_cparams(("arbitrary",)),
    )(x, fw, target)


MIX_T = 512


def _slabs_load(ref, first, n):
    return jnp.concatenate([ref[first + j] for j in range(n)], axis=1)


def _slabs_store(ref, first, val):
    for j in range(val.shape[1] // 128):
        ref[first + j] = val[:, 128 * j:128 * j + 128]


def _slab_spec(k, t):
    return pl.BlockSpec((k, t, 128), lambda i: (0, i, 0))


def mix_in_fwd(x, nw, wext, cos, sin, name):
    t = MIX_T

    def body(x_ref, nw_ref, w_ref, cos_ref, sin_ref, att_ref, rest_ref):
        xh, _r = _rms_stats(x_ref[...])
        h = (xh * nw_ref[...]).astype(BF16)
        pa = _dot(h, w_ref[:, 0:EXT_ATT])
        c = cos_ref[...]
        s = sin_ref[...]
        _slabs_store(att_ref, 0, pa[:, 0:256] * c + pa[:, 768:1024] * s)
        _slabs_store(att_ref, 2, pa[:, 256:512] * c + pa[:, 1024:1280] * s)
        _slabs_store(att_ref, 4, pa[:, 512:768])
        for j in range(EXT_REST // 256):
            rest_ref[:, 256 * j:256 * j + 256] = _dot(h, w_ref[:, EXT_ATT + 256 * j:EXT_ATT + 256 * j + 256])

    return pl.pallas_call(
        body, name=name, grid=(SEQ // t,),
        in_specs=[pl.BlockSpec((t, D_MODEL), lambda i: (i, 0)),
                  pl.BlockSpec((1, D_MODEL), lambda i: (0, 0)),
                  pl.BlockSpec((D_MODEL, EXT_W), lambda i: (0, 0)),
                  pl.BlockSpec((t, ATT_W), lambda i: (i, 0)),
                  pl.BlockSpec((t, ATT_W), lambda i: (i, 0))],
        out_specs=[_slab_spec(6, t),
                   pl.BlockSpec((t, EXT_REST), lambda i: (i, 0))],
        out_shape=[SDS((6, SEQ, 128), F32), SDS((SEQ, EXT_REST), F32)],
        compiler_params=_cparams(("arbitrary",)),
    )(x, nw, wext, cos, sin)


def assemble_dproj(datts, cos, sin, d_dqkv, dz, dg, dpu, name):
    t = 512

    def body(d1_ref, d4_ref, d16_ref, cos_ref, sin_ref, dqkv_ref, dz_ref, dg_ref, dpu_ref, o_ref):
        da6 = d1_ref[...] + d4_ref[...] + d16_ref[...]
        da = jnp.concatenate([da6[j] for j in range(6)], axis=1)
        c = cos_ref[...]
        s = sin_ref[...]
        dq = da[:, 0:256]
        dk = da[:, 256:512]
        o_ref[:, 0:256] = (dq * c).astype(BF16)
        o_ref[:, 256:512] = (dk * c).astype(BF16)
        o_ref[:, 512:768] = da[:, 512:768].astype(BF16)
        o_ref[:, 768:1024] = (dq * s).astype(BF16)
        o_ref[:, 1024:1280] = (dk * s).astype(BF16)
        b = EXT_ATT
        o_ref[:, b + R_DQKV:b + R_DQKV + 1536] = dqkv_ref[...].astype(BF16)
        o_ref[:, b + R_DZ:b + R_DZ + 512] = dz_ref[...].astype(BF16)
        o_ref[:, b + R_G:b + R_G + GATE_W] = dg_ref[...].astype(BF16)
        o_ref[:, b + R_PU:b + R_PU + 256] = dpu_ref[...].astype(BF16)

    row = lambda w: pl.BlockSpec((t, w), lambda i: (i, 0))
    return pl.pallas_call(
        body, name=name, grid=(SEQ // t,),
        in_specs=[_slab_spec(6, t), _slab_spec(6, t), _slab_spec(6, t),
                  row(256), row(256), row(1536), row(512), row(GATE_W), row(256)],
        out_specs=row(EXT_W),
        out_shape=SDS((SEQ, EXT_W), BF16),
        compiler_params=_cparams(("arbitrary",)),
    )(*datts, cos, sin, d_dqkv, dz, dg, dpu)


def linear_bwd(x, dxo, nw, dy, w, name):
    t = 512
    nb = 768
    n = w.shape[1]
    nt = SEQ // t
    nn = n // nb

    def body(x_ref, dxo_ref, nw_ref, dy_ref, w_ref, dx_ref, dw_ref, dnw_ref, dh_scr, h_scr):
        k = pl.program_id(0)
        i = pl.program_id(1)
        rows = pl.ds(pl.multiple_of(i * t, t), t)
        nw_v = nw_ref[...]

        @pl.when(k == 0)
        def _():
            xh0, _r0 = _rms_stats(x_ref[...])
            h_scr[rows, :] = (xh0 * nw_v).astype(BF16)

        h = h_scr[rows, :]
        dyv = dy_ref[...]
        p_w = _dot_tn(h, dyv)
        dh = _dot_nt(dyv, w_ref[...])

        @pl.when(i == 0)
        def _():
            dw_ref[...] = p_w

        @pl.when(i > 0)
        def _():
            dw_ref[...] += p_w

        @pl.when(k == 0)
        def _():
            dh_scr[rows, :] = dh

        @pl.when(k > 0)
        def _():
            dh_scr[rows, :] += dh

        @pl.when(jnp.logical_and(k == 0, i == 0))
        def _():
            dnw_ref[...] = jnp.zeros_like(dnw_ref)

        @pl.when(k == nn - 1)
        def _():
            xh, r = _rms_stats(x_ref[...])
            dx, dw = _rms_bwd(xh, r, nw_v, dh_scr[rows, :])
            dx_ref[...] = dxo_ref[...] + dx
            dnw_ref[...] += dw

    last = nn - 1
    return pl.pallas_call(
        body, name=name, grid=(nn, nt),
        in_specs=[pl.BlockSpec((t, D_MODEL), lambda k, i: (i, 0)),
                  pl.BlockSpec((t, D_MODEL), lambda k, i: (i, 0)),
                  pl.BlockSpec((1, D_MODEL), lambda k, i: (0, 0)),
                  pl.BlockSpec((t, nb), lambda k, i: (i, k)),
                  pl.BlockSpec((D_MODEL, nb), lambda k, i: (0, k))],
        out_specs=[pl.BlockSpec((t, D_MODEL), lambda k, i: (jnp.where(k == last, i, 0), 0)),
                   pl.BlockSpec((D_MODEL, nb), lambda k, i: (0, k)),
                   pl.BlockSpec((1, D_MODEL), lambda k, i: (0, 0))],
        out_shape=[SDS((SEQ, D_MODEL), F32), SDS((D_MODEL, n), F32), SDS((1, D_MODEL), F32)],
        scratch_shapes=[pltpu.VMEM((SEQ, D_MODEL), F32), pltpu.VMEM((SEQ, D_MODEL), BF16)],
        compiler_params=_cparams(("arbitrary", "arbitrary")),
    )(x, dxo, nw, dy, w)


def _att_masks():
    qi = lax.broadcasted_iota(jnp.int32, (ATT_BLK, ATT_BLK), 0)
    ki = lax.broadcasted_iota(jnp.int32, (ATT_BLK, ATT_BLK), 1)
    return ki <= qi, ki >= qi


NEG = -1e30


N_ATT_BLK = SEQ // ATT_BLK


def _class_rows(i, d):
    per_class = N_ATT_BLK // d
    shift = per_class.bit_length() - 1
    r = i >> shift
    j = i & (per_class - 1)
    span = ATT_BLK * d
    start = r + span * j
    prev = jnp.where(j == 0, start, start - span)
    nxt = jnp.where(j == per_class - 1, start, start + span)

    def rows(s0):
        if d == 1:
            return pl.ds(pl.multiple_of(s0, ATT_BLK), ATT_BLK)
        return pl.ds(s0, ATT_BLK, stride=d)

    return rows(start), rows(prev), rows(nxt), j != 0, j != per_class - 1


def _slab_heads(ref, slab, rows):
    x0 = ref[pl.ds(slab, 1), rows, :][0]
    x1 = ref[pl.ds(slab + 1, 1), rows, :][0]
    return jnp.stack([x0[:, 0:ATT_E], x0[:, ATT_E:], x1[:, 0:ATT_E], x1[:, ATT_E:]], axis=0)


def _put_slab_heads(ref, slab, rows, val):
    ref[pl.ds(slab, 1), rows, :] = jnp.concatenate([val[0], val[1]], axis=1)[None]
    ref[pl.ds(slab + 1, 1), rows, :] = jnp.concatenate([val[2], val[3]], axis=1)[None]


ATT_BLOCKS_PER_STEP = 2


def _resident_call(body, ins, out_slabs, name):
    n_in = len(ins)
    steps = N_ATT_BLK // ATT_BLOCKS_PER_STEP

    def wrapped(*refs):
        hbm_in, hbm_out = refs[:n_in], refs[n_in]
        vm_in, vm_out, sem = refs[n_in + 1:2 * n_in + 1], refs[2 * n_in + 1], refs[2 * n_in + 2]
        i = pl.program_id(0)

        @pl.when(i == 0)
        def _():
            copies = [pltpu.make_async_copy(h, v, sem.at[k]) for k, (h, v) in enumerate(zip(hbm_in, vm_in))]
            for cp in copies:
                cp.start()
            for cp in copies:
                cp.wait()

        for b in range(ATT_BLOCKS_PER_STEP):
            body(ATT_BLOCKS_PER_STEP * i + b, *vm_in, vm_out)

        @pl.when(i == steps - 1)
        def _():
            cp = pltpu.make_async_copy(vm_out, hbm_out, sem.at[n_in])
            cp.start()
            cp.wait()

    return pl.pallas_call(
        wrapped, name=name, grid=(steps,),
        in_specs=[ANY_SPEC] * n_in, out_specs=ANY_SPEC, out_shape=SDS((out_slabs, SEQ, 128), F32),
        scratch_shapes=[pltpu.VMEM(a.shape, a.dtype) for a in ins] + [pltpu.VMEM((out_slabs, SEQ, 128), F32),
                                                                      pltpu.SemaphoreType.DMA((n_in + 1,))],
        compiler_params=_cparams(("arbitrary",)),
    )(*ins)


def _att_fwd_math(ld, has_prev):
    m_d, m_p = _att_masks()
    m_p = jnp.logical_and(m_p, has_prev)
    q = ld("att", 0, "cur").astype(BF16)
    kc = ld("att", 2, "cur").astype(BF16)
    vc = ld("att", 4, "cur").astype(BF16)
    kp = ld("att", 2, "prev").astype(BF16)
    vp = ld("att", 4, "prev").astype(BF16)
    sd = jnp.where(m_d, _bdot(q, kc, 2, 2) * 0.125, NEG)
    sp = jnp.where(m_p, _bdot(q, kp, 2, 2) * 0.125, NEG)
    m = jnp.maximum(jnp.max(sd, axis=-1, keepdims=True), jnp.max(sp, axis=-1, keepdims=True))
    pd = jnp.exp(sd - m)
    pp = jnp.exp(sp - m)
    den = jnp.sum(pd, axis=-1, keepdims=True) + jnp.sum(pp, axis=-1, keepdims=True)
    inv = 1.0 / den
    o = _bdot((pd * inv).astype(BF16), vc, 2, 1) + _bdot((pp * inv).astype(BF16), vp, 2, 1)
    return o, jnp.broadcast_to(m + jnp.log(den), (4, ATT_BLK, ATT_E))


def _att_bwd_math(ld, has_prev, has_next):
    m_d, m_band = _att_masks()
    m_p = jnp.logical_and(m_band, has_prev)
    m_n = jnp.logical_and(m_band, has_next)

    def pair(q, k, v, lse, do, dterm, mask):
        s = jnp.where(mask, _bdot(q, k, 2, 2) * 0.125, NEG)
        p = jnp.exp(s - lse)
        dp = _bdot(do, v, 2, 2)
        ds = (p * (dp + dterm) * 0.125).astype(BF16)
        return p.astype(BF16), ds

    q_c = ld("att", 0, "cur").astype(BF16)
    k_c = ld("att", 2, "cur").astype(BF16)
    v_c = ld("att", 4, "cur").astype(BF16)
    k_p = ld("att", 2, "prev").astype(BF16)
    v_p = ld("att", 4, "prev").astype(BF16)
    q_n = ld("att", 0, "next").astype(BF16)
    o_c = ld("ol", 0, "cur")
    o_n = ld("ol", 0, "next")
    lse_c = ld("ol", 2, "cur")[:, :, 0:1]
    lse_n = ld("ol", 2, "next")[:, :, 0:1]
    do_c = ld("dol", 0, "cur")
    do_n = ld("dol", 0, "next")
    t_c = ld("dol", 2, "cur")[:, :, 0:1] - jnp.sum(do_c * o_c, axis=-1, keepdims=True)
    t_n = ld("dol", 2, "next")[:, :, 0:1] - jnp.sum(do_n * o_n, axis=-1, keepdims=True)
    do_cb = do_c.astype(BF16)
    do_nb = do_n.astype(BF16)
    p1, ds1 = pair(q_c, k_c, v_c, lse_c, do_cb, t_c, m_d)
    _p2, ds2 = pair(q_c, k_p, v_p, lse_c, do_cb, t_c, m_p)
    p3, ds3 = pair(q_n, k_c, v_c, lse_n, do_nb, t_n, m_n)
    return (_bdot(ds1, k_c, 2, 1) + _bdot(ds2, k_p, 2, 1), _bdot(ds1, q_c, 1, 1) + _bdot(ds3, q_n, 1, 1),
            _bdot(p1, do_cb, 1, 1) + _bdot(p3, do_nb, 1, 1))


ROWS_A = pl.ds(0, ATT_BLK)
ROWS_B = pl.ds(ATT_BLK, ATT_BLK)
N_ATT_PAIR = N_ATT_BLK // 2


def _pair_spec(k):
    return pl.BlockSpec((k, 2 * ATT_BLK, 128), lambda i: (0, i, 0))


def _before_pair_spec(k):
    return pl.BlockSpec((k, ATT_BLK, 128), lambda i: (0, jnp.maximum(2 * i - 1, 0), 0))


def _after_pair_spec(k):
    return pl.BlockSpec((k, ATT_BLK, 128), lambda i: (0, jnp.minimum(2 * i + 2, N_ATT_BLK - 1), 0))


def att_fwd_s(att, d, name):
    if d == 1:
        def body1(cur_ref, prev_ref, o_ref):
            i = pl.program_id(0)
            for rows, views, has_prev in (
                    (ROWS_A, {"cur": (cur_ref, ROWS_A), "prev": (prev_ref, ROWS_A)}, i != 0),
                    (ROWS_B, {"cur": (cur_ref, ROWS_B), "prev": (cur_ref, ROWS_A)}, True)):
                o, lse = _att_fwd_math(lambda _a, slab, where, v=views: _slab_heads(v[where][0], slab, v[where][1]), has_prev)
                _put_slab_heads(o_ref, 0, rows, o)
                _put_slab_heads(o_ref, 2, rows, lse)

        return pl.pallas_call(
            body1, name=name, grid=(N_ATT_PAIR,),
            in_specs=[_pair_spec(6), _before_pair_spec(6)], out_specs=_pair_spec(4),
            out_shape=SDS((4, SEQ, 128), F32), compiler_params=_cparams(("arbitrary",)),
        )(att, att)

    def body(i, att_ref, o_ref):
        cur, prev, _nxt, has_prev, _has_next = _class_rows(i, d)
        rows = {"cur": cur, "prev": prev}
        o, lse = _att_fwd_math(lambda _a, slab, where: _slab_heads(att_ref, slab, rows[where]), has_prev)
        _put_slab_heads(o_ref, 0, cur, o)
        _put_slab_heads(o_ref, 2, cur, lse)

    return _resident_call(body, [att], 4, name)


def att_bwd_s(att, ol, dol, d, name):
    if d == 1:
        def body1(a_p, a_c, a_n, ol_c, ol_n, dol_c, dol_n, d_ref):
            i = pl.program_id(0)
            first = {("att", "prev"): (a_p, ROWS_A), ("att", "cur"): (a_c, ROWS_A), ("att", "next"): (a_c, ROWS_B),
                     ("ol", "cur"): (ol_c, ROWS_A), ("ol", "next"): (ol_c, ROWS_B),
                     ("dol", "cur"): (dol_c, ROWS_A), ("dol", "next"): (dol_c, ROWS_B)}
            second = {("att", "prev"): (a_c, ROWS_A), ("att", "cur"): (a_c, ROWS_B), ("att", "next"): (a_n, ROWS_A),
                      ("ol", "cur"): (ol_c, ROWS_B), ("ol", "next"): (ol_n, ROWS_A),
                      ("dol", "cur"): (dol_c, ROWS_B), ("dol", "next"): (dol_n, ROWS_A)}
            for rows, views, has_prev, has_next in ((ROWS_A, first, i != 0, True),
                                                    (ROWS_B, second, True, i != N_ATT_PAIR - 1)):
                dq, dk, dv = _att_bwd_math(
                    lambda a, slab, where, v=views: _slab_heads(v[(a, where)][0], slab, v[(a, where)][1]), has_prev, has_next)
                _put_slab_heads(d_ref, 0, rows, dq)
                _put_slab_heads(d_ref, 2, rows, dk)
                _put_slab_heads(d_ref, 4, rows, dv)

        return pl.pallas_call(
            body1, name=name, grid=(N_ATT_PAIR,),
            in_specs=[_before_pair_spec(6), _pair_spec(6), _after_pair_spec(6), _pair_spec(4), _after_pair_spec(4),
                      _pair_spec(4), _after_pair_spec(4)],
            out_specs=_pair_spec(6), out_shape=SDS((6, SEQ, 128), F32), compiler_params=_cparams(("arbitrary",)),
        )(att, att, att, ol, ol, dol, dol)

    def body(i, att_ref, ol_ref, dol_ref, d_ref):
        cur, prev, nxt, has_prev, has_next = _class_rows(i, d)
        rows = {"cur": cur, "prev": prev, "next": nxt}
        refs = {"att": att_ref, "ol": ol_ref, "dol": dol_ref}
        dq, dk, dv = _att_bwd_math(lambda a, slab, where: _slab_heads(refs[a], slab, rows[where]), has_prev, has_next)
        _put_slab_heads(d_ref, 0, cur, dq)
        _put_slab_heads(d_ref, 2, cur, dk)
        _put_slab_heads(d_ref, 4, cur, dv)

    return _resident_call(body, [att, ol, dol], 6, name)


def _shift_down(x, k):
    rows = lax.broadcasted_iota(jnp.int32, x.shape, 0)
    return jnp.where(rows >= k, pltpu.roll(x, k, 0), 0.0)


def _shift_up(x, k):
    n = x.shape[0]
    rows = lax.broadcasted_iota(jnp.int32, x.shape, 0)
    return jnp.where(rows < n - k, pltpu.roll(x, n - k, 0), 0.0)


@functools.partial(jax.custom_vjp, nondiff_argnums=(1,))
def _delay(x, k):
    return _shift_down(x, k)


def _delay_fwd(x, k):
    return _shift_down(x, k), None


def _delay_bwd(k, _res, g):
    return (_shift_up(g, k),)


_delay.defvjp(_delay_fwd, _delay_bwd)

DN_CONV = 4


def _dn_prep_fn(u, w, kind):
    y = w[DN_CONV - 1:DN_CONV] * u
    for j in range(DN_CONV - 1):
        y = y + w[j:j + 1] * _delay(u, DN_CONV - 1 - j)
    y = y * _sigmoid(y)
    nrm = y * lax.rsqrt(jnp.sum(y * y, axis=-1, keepdims=True) + EPS)
    return jnp.where(kind == 0, nrm * (DN_E ** -0.5), jnp.where(kind == 1, nrm, y))


def dn_prep_fwd(rest, conv_w, name):
    def body(u_ref, w_ref, o_ref):
        j = pl.program_id(0)
        kind = (j >= DN_H).astype(jnp.int32) + (j >= 2 * DN_H).astype(jnp.int32)
        o_ref[...] = _dn_prep_fn(u_ref[...], w_ref[...], kind)

    return pl.pallas_call(
        body, name=name, grid=(3 * DN_H,),
        in_specs=[pl.BlockSpec((SEQ, DN_E), lambda j: (0, j)),
                  pl.BlockSpec((DN_CONV, DN_E), lambda j: (0, j))],
        out_specs=pl.BlockSpec((SEQ, DN_E), lambda j: (0, j)),
        out_shape=SDS((SEQ, 3 * DN_W), F32),
        compiler_params=_cparams(("arbitrary",)),
    )(rest, conv_w)


def dn_prep_bwd(rest, conv_w, dqkv, name):
    def body(u_ref, w_ref, g_ref, du_ref, dw_ref):
        j = pl.program_id(0)
        kind = (j >= DN_H).astype(jnp.int32) + (j >= 2 * DN_H).astype(jnp.int32)
        _y, vjp = jax.vjp(lambda u, w: _dn_prep_fn(u, w, kind), u_ref[...], w_ref[...])
        du, dw = vjp(g_ref[...])
        du_ref[...] = du
        dw_ref[...] = dw

    return pl.pallas_call(
        body, name=name, grid=(3 * DN_H,),
        in_specs=[pl.BlockSpec((SEQ, DN_E), lambda j: (0, j)),
                  pl.BlockSpec((DN_CONV, DN_E), lambda j: (0, j)),
                  pl.BlockSpec((SEQ, DN_E), lambda j: (0, j))],
        out_specs=[pl.BlockSpec((SEQ, DN_E), lambda j: (0, j)),
                   pl.BlockSpec((DN_CONV, DN_E), lambda j: (0, j))],
        out_shape=[SDS((SEQ, 3 * DN_W), F32), SDS((DN_CONV, 3 * DN_W), F32)],
        compiler_params=_cparams(("arbitrary",)),
    )(rest, conv_w, dqkv)


def _bdot(a, b, ca, cb, prec=None):
    return lax.dot_general(a, b, (((ca,), (cb,)), ((0,), (0,))), preferred_element_type=F32, precision=prec)


def _unit_lower_inverse(a):
    eye = (lax.broadcasted_iota(jnp.int32, (DN_C, DN_C), 0) == lax.broadcasted_iota(jnp.int32, (DN_C, DN_C), 1)).astype(F32)
    p = eye - a
    b = _bdot(a, a, 2, 1, INV_PREC)
    for lvl in range(5):
        p = p + _bdot(p, b, 2, 1, INV_PREC)
        if lvl < 4:
            b = _bdot(b, b, 2, 1, INV_PREC)
    return p


@jax.custom_vjp
def _tri_inv(a):
    return _unit_lower_inverse(a)


def _tri_inv_fwd(a):
    t = _unit_lower_inverse(a)
    return t, t


def _tri_inv_bwd(t, g):
    return (-_bdot(_bdot(t, g, 1, 1, INV_PREC), t, 2, 2, INV_PREC),)


_tri_inv.defvjp(_tri_inv_fwd, _tri_inv_bwd)


def _b16(x):
    return x.astype(BF16)


def _heads(ref, base=0):
    return jnp.stack([ref[:, base + DN_E * hd:base + DN_E * hd + DN_E] for hd in range(DN_H)], axis=0)


def _put_heads(ref, val, base=0):
    for hd in range(DN_H):
        ref[:, base + DN_E * hd:base + DN_E * hd + DN_E] = val[hd]


DN_G_LOG2 = 3
DN_G = 1 << DN_G_LOG2
N_INST = DN_G * DN_H


def _dn_intra(q, k, v, bb, ab, alog, dtb):
    ri = lax.broadcasted_iota(jnp.int32, (DN_C, DN_C), 0)
    ci = lax.broadcasted_iota(jnp.int32, (DN_C, DN_C), 1)
    lower = ri >= ci
    strict = ri > ci
    nh = q.shape[0]
    beta = _sigmoid(bb)
    xg = ab + dtb
    softplus = jnp.maximum(xg, 0.0) + jnp.log(1.0 + jnp.exp(-jnp.abs(xg)))
    gi = -jnp.exp(alog) * softplus
    g = _bdot(jnp.broadcast_to(lower.astype(F32), (nh, DN_C, DN_C)), gi, 2, 1, HI)
    eg = jnp.exp(g)
    kb = k * beta
    vb = v * beta
    g_col = g[:, :, 0:DN_C]
    g_row = _bdot(jnp.full((nh, DN_C, DN_E), 1.0 / DN_E, F32), g, 2, 2, HI)
    decay = jnp.where(lower, jnp.exp(jnp.where(lower, g_col - g_row, 0.0)), 0.0)
    kbf = _b16(k)
    a = jnp.where(strict, _bdot(_b16(kb), kbf, 2, 2) * decay, 0.0)
    tb = _b16(_tri_inv(a))
    u = _bdot(tb, _b16(vb), 2, 1)
    w = _bdot(tb, _b16(kb * eg), 2, 1)
    intra = jnp.where(lower, _bdot(_b16(q), kbf, 2, 2) * decay, 0.0)
    g_last = g[:, DN_C - 1:DN_C, :]
    return u, w, q * eg, k * jnp.exp(g_last - g), intra, jnp.exp(g_last)


def _dn_inter(u, w, qg, kdec, intra, egl, state):
    sb = _b16(state)
    v_new = u - _bdot(_b16(w), sb, 2, 1)
    o = _bdot(_b16(qg), sb, 2, 1) + _bdot(_b16(intra), _b16(v_new), 2, 1)
    return o, state * egl + _bdot(_b16(kdec), _b16(v_new), 1, 1)


def _inst(ref, base=0):
    per_head = [ref[:, base + DN_E * hd:base + DN_E * hd + DN_E].reshape(DN_G, DN_C, DN_E) for hd in range(DN_H)]
    return jnp.concatenate(per_head, axis=0)


def _inst_rows(ref):
    rows = [jnp.broadcast_to(ref[:, DN_E * hd:DN_E * hd + DN_E][None], (DN_G, 1, DN_E)) for hd in range(DN_H)]
    return jnp.concatenate(rows, axis=0)


def _put_inst(ref, val, width=DN_E, base=0):
    for hd in range(DN_H):
        ref[:, base + width * hd:base + width * hd + width] = val[DN_G * hd:DN_G * hd + DN_G].reshape(DN_G * DN_C, width)


@jax.custom_vjp
def _spread_gates(gates):
    t = gates.shape[0]
    return jnp.concatenate([jnp.broadcast_to(gates[:, j:j + 1], (t, DN_E)) for j in range(2 * DN_H)], axis=1)


def _spread_gates_fwd(gates):
    return _spread_gates(gates), None


def _spread_gates_bwd(_res, g):
    t = g.shape[0]
    lane = lax.broadcasted_iota(jnp.int32, (t, GATE_W), 1)
    out = jnp.zeros((t, GATE_W), F32)
    for j in range(2 * DN_H):
        s = jnp.sum(g[:, DN_E * j:DN_E * j + DN_E], axis=-1, keepdims=True)
        out = jnp.where(lane == j, s, out)
    return (out,)


_spread_gates.defvjp(_spread_gates_fwd, _spread_gates_bwd)


def _dn_intra_from_gates(q, k, v, gates, alog, dtb):
    wide = _spread_gates(gates)
    inst = lambda base: jnp.concatenate(
        [wide[:, base + DN_E * hd:base + DN_E * hd + DN_E].reshape(DN_G, DN_C, DN_E) for hd in range(DN_H)], axis=0)
    return _dn_intra(q, k, v, inst(0), inst(DN_W), alog, dtb)


def _intra_args(qkv_ref, g_ref, alog_ref, dtb_ref):
    return (_inst(qkv_ref), _inst(qkv_ref, DN_W), _inst(qkv_ref, 2 * DN_W), g_ref[...],
            _inst_rows(alog_ref), _inst_rows(dtb_ref))


def _intra_in_specs():
    t = DN_G * DN_C
    return [pl.BlockSpec((t, 3 * DN_W), lambda n: (n, 0)),
            pl.BlockSpec((t, GATE_W), lambda n: (n, R_G // GATE_W)),
            pl.BlockSpec((1, DN_W), lambda n: (0, 0)),
            pl.BlockSpec((1, DN_W), lambda n: (0, 0))]


def dn_intra_fwd(qkv, rest, alog_b, dtb_b, name, exch=None):
    t = DN_G * DN_C

    def body(qkv_ref, g_ref, alog_ref, dtb_ref, u_ref, w_ref, qg_ref, kd_ref, in_ref, egl_ref):
        u, w, qg, kdec, intra, egl = _dn_intra_from_gates(*_intra_args(qkv_ref, g_ref, alog_ref, dtb_ref))
        _put_inst(u_ref, u)
        _put_inst(w_ref, w.astype(BF16))
        _put_inst(qg_ref, qg.astype(BF16))
        _put_inst(kd_ref, kdec.astype(BF16))
        _put_inst(in_ref, intra.astype(BF16), DN_C)
        for hd in range(DN_H):
            egl_ref[:, DN_E * hd:DN_E * hd + DN_E] = egl[DN_G * hd:DN_G * hd + DN_G].reshape(DN_G, DN_E)

    row = lambda w_: pl.BlockSpec((t, w_), lambda n: (n, 0))
    return _call(
        body, name=name, grid=(N_CHUNK // DN_G,), in_specs=_intra_in_specs(),
        out_specs=[row(DN_W), row(DN_W), row(DN_W), row(DN_W), row(DN_H * DN_C),
                   pl.BlockSpec((DN_G, DN_W), lambda n: (n, 0))],
        out_shape=[SDS((SEQ, DN_W), F32), SDS((SEQ, DN_W), BF16), SDS((SEQ, DN_W), BF16), SDS((SEQ, DN_W), BF16),
                   SDS((SEQ, DN_H * DN_C), BF16), SDS((N_CHUNK, DN_W), F32)],
        scratch_shapes=[], sem=("arbitrary",), args=(qkv, rest, alog_b, dtb_b), exch=exch)


def dn_intra_bwd(qkv, rest, alog_b, dtb_b, du, dw, dqg, dkd, dintra, degl, name):
    t = DN_G * DN_C

    def body(qkv_ref, g_ref, alog_ref, dtb_ref, du_ref, dw_ref, dqg_ref, dkd_ref, din_ref, degl_ref,
             dqkv_ref, dg_ref, dalog_ref, ddtb_ref):
        @pl.when(pl.program_id(0) == 0)
        def _():
            dalog_ref[...] = jnp.zeros_like(dalog_ref)
            ddtb_ref[...] = jnp.zeros_like(ddtb_ref)

        _out, vjp = jax.vjp(_dn_intra_from_gates, *_intra_args(qkv_ref, g_ref, alog_ref, dtb_ref))
        d_in = jnp.concatenate([din_ref[:, DN_C * hd:DN_C * hd + DN_C].reshape(DN_G, DN_C, DN_C) for hd in range(DN_H)], axis=0)
        d_egl = jnp.concatenate([degl_ref[:, DN_E * hd:DN_E * hd + DN_E].reshape(DN_G, 1, DN_E) for hd in range(DN_H)], axis=0)
        dq, dk, dv, dg, dalog, ddtb = vjp((_inst(du_ref), _inst(dw_ref), _inst(dqg_ref), _inst(dkd_ref), d_in, d_egl))
        _put_inst(dqkv_ref, dq)
        _put_inst(dqkv_ref, dk, DN_E, DN_W)
        _put_inst(dqkv_ref, dv, DN_E, 2 * DN_W)
        dg_ref[...] = dg
        for hd in range(DN_H):
            sl = slice(DN_E * hd, DN_E * hd + DN_E)
            dalog_ref[:, sl] += jnp.sum(dalog[DN_G * hd:DN_G * hd + DN_G], axis=0)
            ddtb_ref[:, sl] += jnp.sum(ddtb[DN_G * hd:DN_G * hd + DN_G], axis=0)

    row = lambda w_: pl.BlockSpec((t, w_), lambda n: (n, 0))
    acc = pl.BlockSpec((1, DN_W), lambda n: (0, 0))
    return pl.pallas_call(
        body, name=name, grid=(N_CHUNK // DN_G,),
        in_specs=_intra_in_specs() + [row(DN_W), row(DN_W), row(DN_W), row(DN_W), row(DN_H * DN_C),
                                      pl.BlockSpec((DN_G, DN_W), lambda n: (n, 0))],
        out_specs=[row(3 * DN_W), row(GATE_W), acc, acc],
        out_shape=[SDS((SEQ, 3 * DN_W), F32), SDS((SEQ, GATE_W), F32), SDS((1, DN_W), F32), SDS((1, DN_W), F32)],
        compiler_params=_cparams(("arbitrary",)),
    )(qkv, rest, alog_b, dtb_b, du, dw, dqg, dkd, dintra, degl)


DN_RUN_LOG2 = 3
DN_RUN = 1 << DN_RUN_LOG2


def _chunk_rows(ref, c):
    return ref.at[pl.ds(DN_C * c, DN_C), :]


def _inter_args(u_ref, w_ref, qg_ref, kd_ref, in_ref, egl_ref, n, state):
    f = lambda r: _heads(r).astype(F32)
    intra = jnp.stack([in_ref[:, DN_C * hd:DN_C * hd + DN_C] for hd in range(DN_H)], axis=0).astype(F32)
    egl = _heads(egl_ref.at[pl.ds(n & (DN_G - 1), 1), :])
    return f(u_ref), f(w_ref), f(qg_ref), f(kd_ref), intra, egl, state


def dn_inter_fwd(u, w, qg, kdec, intra, egl, name, exch=None):
    def body(u_ref, w_ref, qg_ref, kd_ref, in_ref, egl_ref, o_ref, st_ref, state_scr):
        n2 = pl.program_id(0)

        @pl.when(n2 == 0)
        def _():
            state_scr[...] = jnp.zeros_like(state_scr)

        for c in range(DN_RUN):
            v = functools.partial(_chunk_rows, c=c)
            st = state_scr[...]
            st_ref[c] = st
            o, ns = _dn_inter(*_inter_args(v(u_ref), v(w_ref), v(qg_ref), v(kd_ref), v(in_ref), egl_ref,
                                           DN_RUN * n2 + c, st))
            _put_heads(v(o_ref), o)
            state_scr[...] = ns

    row = lambda w_: pl.BlockSpec((DN_RUN * DN_C, w_), lambda n: (n, 0))
    return _call(
        body, name=name, grid=(N_CHUNK // DN_RUN,),
        in_specs=[row(DN_W), row(DN_W), row(DN_W), row(DN_W), row(DN_H * DN_C),
                  pl.BlockSpec((DN_G, DN_W), lambda n: (n >> (DN_G_LOG2 - DN_RUN_LOG2), 0))],
        out_specs=[row(DN_W), pl.BlockSpec((DN_RUN, DN_H, DN_E, DN_E), lambda n: (n, 0, 0, 0))],
        out_shape=[SDS((SEQ, DN_W), F32), SDS((N_CHUNK, DN_H, DN_E, DN_E), F32)],
        scratch_shapes=[pltpu.VMEM((DN_H, DN_E, DN_E), F32)],
        sem=("arbitrary",), args=(u, w, qg, kdec, intra, egl), exch=exch)


def dn_inter_bwd(u, w, qg, kdec, intra, egl, states, do, name):
    last = N_CHUNK // DN_RUN - 1

    def body(u_ref, w_ref, qg_ref, kd_ref, in_ref, egl_ref, st_ref, do_ref,
             du_ref, dw_ref, dqg_ref, dkd_ref, din_ref, degl_ref, dstate_scr):
        s = pl.program_id(0)

        @pl.when(s == 0)
        def _():
            dstate_scr[...] = jnp.zeros_like(dstate_scr)

        for c in reversed(range(DN_RUN)):
            n = DN_RUN * (last - s) + c
            v = functools.partial(_chunk_rows, c=c)
            _out, vjp = jax.vjp(_dn_inter, *_inter_args(v(u_ref), v(w_ref), v(qg_ref), v(kd_ref), v(in_ref), egl_ref,
                                                        n, st_ref[c]))
            du, dw, dqg, dkd, din, degl, dst = vjp((_heads(v(do_ref)), dstate_scr[...]))
            _put_heads(v(du_ref), du)
            _put_heads(v(dw_ref), dw)
            _put_heads(v(dqg_ref), dqg)
            _put_heads(v(dkd_ref), dkd)
            for hd in range(DN_H):
                v(din_ref)[:, DN_C * hd:DN_C * hd + DN_C] = din[hd]
            row = n & (DN_G - 1)

            @pl.when(row == DN_G - 1)
            def _():
                degl_ref[...] = jnp.zeros_like(degl_ref)

            new_row = jnp.concatenate([degl[hd] for hd in range(DN_H)], axis=1)
            rows = lax.broadcasted_iota(jnp.int32, (DN_G, DN_W), 0)
            degl_ref[...] = jnp.where(rows == row, jnp.broadcast_to(new_row, (DN_G, DN_W)), degl_ref[...])
            dstate_scr[...] = dst

    rev = lambda w_: pl.BlockSpec((DN_RUN * DN_C, w_), lambda s: (last - s, 0))
    grp = pl.BlockSpec((DN_G, DN_W), lambda s: ((last - s) >> (DN_G_LOG2 - DN_RUN_LOG2), 0))
    return pl.pallas_call(
        body, name=name, grid=(N_CHUNK // DN_RUN,),
        in_specs=[rev(DN_W), rev(DN_W), rev(DN_W), rev(DN_W), rev(DN_H * DN_C), grp,
                  pl.BlockSpec((DN_RUN, DN_H, DN_E, DN_E), lambda s: (last - s, 0, 0, 0)), rev(DN_W)],
        out_specs=[rev(DN_W), rev(DN_W), rev(DN_W), rev(DN_W), rev(DN_H * DN_C), grp],
        out_shape=[SDS((SEQ, DN_W), F32)] * 4 + [SDS((SEQ, DN_H * DN_C), F32), SDS((N_CHUNK, DN_W), F32)],
        scratch_shapes=[pltpu.VMEM((DN_H, DN_E, DN_E), F32)],
        compiler_params=_cparams(("arbitrary",)),
    )(u, w, qg, kdec, intra, egl, states, do)


OUT_T = 512


def _pool_consts(rows_total, t0, halo_before):
    lane = lax.broadcasted_iota(jnp.int32, (rows_total, POOL_W), 1)
    row = lax.broadcasted_iota(jnp.int32, (rows_total, POOL_W), 0)
    grp = (lane >= 64).astype(jnp.int32) + (lane >= 128).astype(jnp.int32) + (lane >= 192).astype(jnp.int32)
    win = jnp.where(grp == 0, 2, jnp.where(grp == 1, 4, jnp.where(grp == 2, 8, 16)))
    pos = t0 + row - halo_before
    cnt = jnp.minimum(pos + 1, win).astype(F32)
    return grp, cnt


def _pool_select(grp, s2, s4, s8, s16):
    return jnp.where(grp == 0, s2, jnp.where(grp == 1, s4, jnp.where(grp == 2, s8, s16)))


def _pooled(u_ext, t0):
    n = u_ext.shape[0]
    grp, cnt = _pool_consts(n, t0, POOL_HALO)
    s2 = u_ext + pltpu.roll(u_ext, 1, 0)
    s4 = s2 + pltpu.roll(s2, 2, 0)
    s8 = s4 + pltpu.roll(s4, 4, 0)
    s16 = s8 + pltpu.roll(s8, 8, 0)
    out = _pool_select(grp, s2, s4, s8, s16) / jnp.maximum(cnt, 1.0) - u_ext
    return out[POOL_HALO:, :]


def _merge_weights(l1, l4, l16):
    m = jnp.maximum(jnp.maximum(l1, l4), l16)
    e1 = jnp.exp(l1 - m)
    e4 = jnp.exp(l4 - m)
    e16 = jnp.exp(l16 - m)
    inv = 1.0 / (e1 + e4 + e16)
    return e1 * inv, e4 * inv, e16 * inv


def _out_parts(ol1_ref, ol4_ref, ol16_ref, pu_ref, puh_ref, odn_ref, z_ref, wbd_ref, i, t):
    w1, w4, w16 = _merge_weights(_slabs_load(ol1_ref, 2, 2), _slabs_load(ol4_ref, 2, 2), _slabs_load(ol16_ref, 2, 2))
    ya = w1 * _slabs_load(ol1_ref, 0, 2) + w4 * _slabs_load(ol4_ref, 0, 2) + w16 * _slabs_load(ol16_ref, 0, 2)
    halo = jnp.where(i > 0, puh_ref[...], 0.0)
    pooled = _pooled(jnp.concatenate([halo, pu_ref[...]], axis=0), i * t)
    pw = _dot(pooled.astype(BF16), wbd_ref[...])
    return ya, pooled, pw, (w1, w4, w16)


def _out_specs_common(t):
    def row(w, cb=0):
        return pl.BlockSpec((t, w), lambda i: (i, cb))

    halo = pl.BlockSpec((POOL_HALO, POOL_W),
                        lambda i: (jnp.maximum(i * (t // POOL_HALO) - 1, 0), R_PU // POOL_W))
    full = lambda a, b: pl.BlockSpec((a, b), lambda i: (0, 0))
    return [_slab_spec(4, t), _slab_spec(4, t), _slab_spec(4, t), row(POOL_W, R_PU // POOL_W), halo, row(DN_W), row(DN_W, R_DZ // DN_W),
            full(POOL_W, POOL_W), full(1, POOL_W), full(1, DN_W), full(D_MODEL, D_MODEL)]


def mix_out_fwd(x, ol1, ol4, ol16, rest, odn, wbd, scale, onorm_b, wout, name):
    t = OUT_T

    def body(x_ref, ol1_ref, ol4_ref, ol16_ref, pu_ref, puh_ref, odn_ref, z_ref, wbd_ref, sc_ref, on_ref, wo_ref, o_ref):
        i = pl.program_id(0)
        ya, _pooled_v, pw, _w = _out_parts(ol1_ref, ol4_ref, ol16_ref, pu_ref, puh_ref, odn_ref, z_ref, wbd_ref, i, t)
        yb = pw * sc_ref[...]
        acc = x_ref[...] + _dot(ya.astype(BF16), wo_ref[0:256, :]) + _dot(yb.astype(BF16), wo_ref[256:512, :])
        for hd in range(DN_H):
            sl = slice(DN_E * hd, DN_E * hd + DN_E)
            oh, _r = _rms_stats(odn_ref[:, sl])
            z = z_ref[:, sl]
            yc = oh * on_ref[:, sl] * (z * _sigmoid(z))
            acc = acc + _dot(yc.astype(BF16), wo_ref[512 + DN_E * hd:512 + DN_E * hd + DN_E, :])
        o_ref[...] = acc

    return pl.pallas_call(
        body, name=name, grid=(SEQ // t,),
        in_specs=[pl.BlockSpec((t, D_MODEL), lambda i: (i, 0))] + _out_specs_common(t),
        out_specs=pl.BlockSpec((t, D_MODEL), lambda i: (i, 0)),
        out_shape=SDS((SEQ, D_MODEL), F32),
        compiler_params=_cparams(("arbitrary",)),
    )(x, ol1, ol4, ol16, rest, rest, odn, rest, wbd, scale, onorm_b, wout)


def mix_out_bwd(dxo, ol1, ol4, ol16, rest, odn, wbd, scale, onorm_b, wout, headsum, name):
    t = OUT_T

    def body(dxo_ref, ol1_ref, ol4_ref, ol16_ref, pu_ref, puh_ref, odn_ref, z_ref, wbd_ref, sc_ref, on_ref, wo_ref, hs_ref,
             dwo_ref, d1_ref, d4_ref, d16_ref, dpl_ref, dodn_ref, dz_ref, dsc_ref, don_ref, dwbd_ref):
        i = pl.program_id(0)

        @pl.when(i == 0)
        def _():
            dwo_ref[...] = jnp.zeros_like(dwo_ref)
            dsc_ref[...] = jnp.zeros_like(dsc_ref)
            don_ref[...] = jnp.zeros_like(don_ref)
            dwbd_ref[...] = jnp.zeros_like(dwbd_ref)

        ya, pooled, pw, (w1, w4, w16) = _out_parts(ol1_ref, ol4_ref, ol16_ref, pu_ref, puh_ref, odn_ref, z_ref, wbd_ref, i, t)
        sc = sc_ref[...]
        dxb = dxo_ref[...].astype(BF16)
        dwo_ref[0:256, :] += _dot_tn(ya.astype(BF16), dxb)
        dwo_ref[256:512, :] += _dot_tn((pw * sc).astype(BF16), dxb)
        dya = _dot_nt(dxb, wo_ref[0:256, :])
        o1 = _slabs_load(ol1_ref, 0, 2)
        o4 = _slabs_load(ol4_ref, 0, 2)
        o16 = _slabs_load(ol16_ref, 0, 2)
        hs = hs_ref[...]
        s1 = _dot(dya * o1, hs, HI)
        s4 = _dot(dya * o4, hs, HI)
        s16 = _dot(dya * o16, hs, HI)
        sbar = w1 * s1 + w4 * s4 + w16 * s16
        _slabs_store(d1_ref, 0, w1 * dya)
        _slabs_store(d1_ref, 2, w1 * (s1 - sbar))
        _slabs_store(d4_ref, 0, w4 * dya)
        _slabs_store(d4_ref, 2, w4 * (s4 - sbar))
        _slabs_store(d16_ref, 0, w16 * dya)
        _slabs_store(d16_ref, 2, w16 * (s16 - sbar))
        dyb = _dot_nt(dxb, wo_ref[256:512, :])
        dsc_ref[...] += jnp.sum(dyb * pw, axis=0, keepdims=True)
        dpw = (dyb * sc).astype(BF16)
        dwbd_ref[...] += _dot_tn(pooled.astype(BF16), dpw)
        dpl_ref[...] = _dot_nt(dpw, wbd_ref[...])
        for hd in range(DN_H):
            sl = slice(DN_E * hd, DN_E * hd + DN_E)
            rows_w = slice(512 + DN_E * hd, 512 + DN_E * hd + DN_E)
            oh, r = _rms_stats(odn_ref[:, sl])
            z = z_ref[:, sl]
            sg = _sigmoid(z)
            sz = z * sg
            nw = on_ref[:, sl]
            on = oh * nw
            dwo_ref[rows_w, :] += _dot_tn((on * sz).astype(BF16), dxb)
            dyc = _dot_nt(dxb, wo_ref[rows_w, :])
            dz_ref[:, sl] = dyc * on * (sg * (1.0 + z * (1.0 - sg)))
            dx, dw = _rms_bwd(oh, r, nw, dyc * sz)
            dodn_ref[:, sl] = dx
            don_ref[:, sl] += dw

    row = lambda w: pl.BlockSpec((t, w), lambda i: (i, 0))
    full = lambda a, b: pl.BlockSpec((a, b), lambda i: (0, 0))
    return pl.pallas_call(
        body, name=name, grid=(SEQ // t,),
        in_specs=[row(D_MODEL)] + _out_specs_common(t) + [full(ATT_W, ATT_W)],
        out_specs=[full(D_MODEL, D_MODEL), _slab_spec(4, t), _slab_spec(4, t), _slab_spec(4, t), row(POOL_W), row(DN_W), row(DN_W),
                   full(1, POOL_W), full(1, DN_W), full(POOL_W, POOL_W)],
        out_shape=[SDS((D_MODEL, D_MODEL), F32), SDS((4, SEQ, 128), F32), SDS((4, SEQ, 128), F32), SDS((4, SEQ, 128), F32),
                   SDS((SEQ, POOL_W), F32), SDS((SEQ, DN_W), F32), SDS((SEQ, DN_W), F32),
                   SDS((1, POOL_W), F32), SDS((1, DN_W), F32), SDS((POOL_W, POOL_W), F32)],
        compiler_params=_cparams(("arbitrary",)),
    )(dxo, ol1, ol4, ol16, rest, rest, odn, rest, wbd, scale, onorm_b, wout, headsum)


def pool_bwd(dpooled, name):
    t = 512
    nt = SEQ // t

    def body(d_ref, dn_ref, o_ref):
        i = pl.program_id(0)
        halo = jnp.where(i < nt - 1, dn_ref[...], 0.0)
        d_ext = jnp.concatenate([d_ref[...], halo], axis=0)
        n = t + POOL_HALO
        grp, cnt = _pool_consts(n, i * t, 0)
        dq = d_ext / cnt
        s2 = dq + pltpu.roll(dq, n - 1, 0)
        s4 = s2 + pltpu.roll(s2, n - 2, 0)
        s8 = s4 + pltpu.roll(s4, n - 4, 0)
        s16 = s8 + pltpu.roll(s8, n - 8, 0)
        o_ref[...] = (_pool_select(grp, s2, s4, s8, s16) - d_ext)[0:t, :]

    return pl.pallas_call(
        body, name=name, grid=(nt,),
        in_specs=[pl.BlockSpec((t, POOL_W), lambda i: (i, 0)),
                  pl.BlockSpec((POOL_HALO, POOL_W),
                               lambda i: (jnp.minimum((i + 1) * (t // POOL_HALO), SEQ // POOL_HALO - 1), 0))],
        out_specs=pl.BlockSpec((t, POOL_W), lambda i: (i, 0)),
        out_shape=SDS((SEQ, POOL_W), F32),
        compiler_params=_cparams(("arbitrary",)),
    )(dpooled, dpooled)


N_PEER = N_DEV - 1
ANY_SPEC = pl.BlockSpec(memory_space=pl.ANY)


class Exchange:
    def __init__(self, arrays, mode):
        self.arrays = list(arrays)
        self.mode = mode
        n = len(self.arrays)
        if mode == "scatter":
            self.out_shape = [SDS(a.shape, a.dtype) for a in self.arrays]
        else:
            self.out_shape = [SDS((N_DEV,) + a.shape, a.dtype) for a in self.arrays]
        self.scratch = [pltpu.SemaphoreType.DMA((n * N_PEER,)), pltpu.SemaphoreType.DMA((n * N_PEER,)),
                        pltpu.SemaphoreType.DMA((n,))]

    @staticmethod
    def _place():
        x, y, c = lax.axis_index("x"), lax.axis_index("y"), lax.axis_index("c")
        chips = [(1 - x, y), (x, 1 - y), (1 - x, 1 - y)]
        return x, y, c, chips

    @staticmethod
    def _copy(sems, a, k, src, dst, to):
        send_sems, recv_sems, _ = sems
        return pltpu.make_async_remote_copy(
            src_ref=src, dst_ref=dst, send_sem=send_sems.at[a * N_PEER + k], recv_sem=recv_sems.at[a * N_PEER + k],
            device_id=to, device_id_type=MESH)

    def _scatter_peers(self):
        x, y, c, _ = self._place()
        out = []
        for fx, fy, fc in ((0, 0, 1), (1, 0, 0), (0, 1, 0), (1, 1, 0), (1, 0, 1), (0, 1, 1), (1, 1, 1)):
            px, py, pc = x ^ fx, y ^ fy, c ^ fc
            out.append(((px, py, pc), 4 * px + 2 * py + pc))
        return 4 * x + 2 * y + c, out

    def _local(self, ins, outs, sems, a, me):
        src = ins[a].at[me] if self.mode == "scatter" else ins[a]
        return pltpu.make_async_copy(src, outs[a].at[me], sems[2].at[a])

    def start(self, ins, outs, sems):
        if self.mode == "scatter":
            me, peers = self._scatter_peers()
            for a in range(len(ins)):
                self._local(ins, outs, sems, a, me).start()
                for k, (peer, pidx) in enumerate(peers):
                    self._copy(sems, a, k, ins[a].at[pidx], outs[a].at[me], peer).start()
            return
        x, y, c, chips = self._place()
        me = 4 * x + 2 * y + c
        for a in range(len(ins)):
            self._local(ins, outs, sems, a, me).start()
            self._copy(sems, a, 0, ins[a], outs[a].at[me], (x, y, 1 - c)).start()
            for j, (cx, cy) in enumerate(chips):
                self._copy(sems, a, 1 + j, ins[a], outs[a].at[me], (cx, cy, c)).start()

    def finish(self, ins, outs, sems):
        n = len(ins)
        if self.mode == "scatter":
            me, peers = self._scatter_peers()
            for a in range(n):
                for k, (peer, pidx) in enumerate(peers):
                    self._copy(sems, a, k, ins[a].at[pidx], outs[a].at[pidx], peer).wait_recv()
            for a in range(n):
                for k, (peer, pidx) in enumerate(peers):
                    self._copy(sems, a, k, ins[a].at[pidx], outs[a].at[me], peer).wait_send()
                self._local(ins, outs, sems, a, me).wait()
            return
        x, y, c, chips = self._place()
        me = 4 * x + 2 * y + c
        sib = (x, y, 1 - c)
        for a in range(n):
            for j, (cx, cy) in enumerate(chips):
                blk = outs[a].at[4 * cx + 2 * cy + c]
                self._copy(sems, a, 1 + j, ins[a], blk, (cx, cy, c)).wait_recv()
                self._copy(sems, a, 4 + j, blk, blk, sib).start()
        for a in range(n):
            self._copy(sems, a, 0, ins[a], outs[a].at[4 * x + 2 * y + (1 - c)], sib).wait_recv()
            for j, (cx, cy) in enumerate(chips):
                blk = outs[a].at[4 * cx + 2 * cy + (1 - c)]
                self._copy(sems, a, 4 + j, blk, blk, sib).wait_recv()
        for a in range(n):
            for k in range(N_PEER):
                self._copy(sems, a, k, ins[a], outs[a].at[me], sib).wait_send()
            self._local(ins, outs, sems, a, me).wait()


def run_exchanges(exchs, name):
    counts = [len(e.arrays) for e in exchs]
    n = sum(counts)

    def body(*refs):
        ins, outs, sems = refs[:n], refs[n:2 * n], refs[2 * n:]
        parts, off = [], 0
        for j, c in enumerate(counts):
            parts.append((ins[off:off + c], outs[off:off + c], sems[3 * j:3 * j + 3]))
            off += c
        for e, p in zip(exchs, parts):
            e.start(*p)
        for e, p in zip(exchs, parts):
            e.finish(*p)

    res = pl.pallas_call(
        body, name=name, in_specs=[ANY_SPEC] * n, out_specs=[ANY_SPEC] * n,
        out_shape=[s for e in exchs for s in e.out_shape], scratch_shapes=[s for e in exchs for s in e.scratch],
    )(*[a for e in exchs for a in e.arrays])
    out, off = [], 0
    for c in counts:
        out.append(list(res[off:off + c]))
        off += c
    return out


def run_exchange(exch, name):
    return run_exchanges([exch], name)[0]


def _call(body, *, name, grid, in_specs, out_specs, out_shape, scratch_shapes, sem, args, exch=None):
    if exch is None:
        res = pl.pallas_call(body, name=name, grid=grid, in_specs=in_specs, out_specs=out_specs, out_shape=out_shape,
                             scratch_shapes=scratch_shapes, compiler_params=_cparams(sem))(*args)
        return res, None
    single = not isinstance(out_shape, (list, tuple))
    out_specs_l = [out_specs] if single else list(out_specs)
    out_shape_l = [out_shape] if single else list(out_shape)
    n_in, n_out, n_scr, m = len(in_specs), len(out_specs_l), len(scratch_shapes), len(exch.arrays)

    def wrapped(*refs):
        p = 0
        ins = refs[p:p + n_in]; p += n_in
        xin = refs[p:p + m]; p += m
        outs = refs[p:p + n_out]; p += n_out
        xout = refs[p:p + m]; p += m
        scr = refs[p:p + n_scr]; p += n_scr
        sems = refs[p:]
        ids = [pl.program_id(ax) for ax in range(len(grid))]
        first = functools.reduce(jnp.logical_and, [i == 0 for i in ids])
        last = functools.reduce(jnp.logical_and, [i == g - 1 for i, g in zip(ids, grid)])

        @pl.when(first)
        def _():
            exch.start(xin, xout, sems)

        body(*ins, *outs, *scr)

        @pl.when(last)
        def _():
            exch.finish(xin, xout, sems)

    res = pl.pallas_call(
        wrapped, name=name, grid=grid, in_specs=list(in_specs) + [ANY_SPEC] * m,
        out_specs=out_specs_l + [ANY_SPEC] * m, out_shape=out_shape_l + exch.out_shape,
        scratch_shapes=list(scratch_shapes) + exch.scratch, compiler_params=_cparams(sem),
    )(*args, *exch.arrays)
    outs = res[:n_out]
    return (outs[0] if single else outs), res[n_out:]


def _adam_math(w, g, m, v):
    m2 = ADAM_B1 * m + (1.0 - ADAM_B1) * g
    v2 = ADAM_B2 * v + (1.0 - ADAM_B2) * (g * g)
    m_hat = m2 / (1.0 - ADAM_B1 ** ADAM_STEP)
    v_hat = v2 / (1.0 - ADAM_B2 ** ADAM_STEP)
    delta = -ADAM_LR * (m_hat / (jnp.sqrt(v_hat) + ADAM_EPS) + ADAM_WD * w)
    return delta, m2, v2


ADAM_ROW_BLOCKS = 2


def adam_shard(parts0, parts1, w, m, v, name):
    _, r, c = w.shape
    rb = r // ADAM_ROW_BLOCKS

    def body(p0_ref, p1_ref, w_ref, m_ref, v_ref, g_ref, d_ref, m2_ref, v2_ref):
        def run(p_ref):
            g = p_ref[0].astype(F32)
            for i in range(1, N_DEV):
                g = g + p_ref[i].astype(F32)
            delta, m2, v2 = _adam_math(w_ref[0], g, m_ref[0], v_ref[0])
            g_ref[0] = g
            d_ref[0] = delta
            m2_ref[0] = m2
            v2_ref[0] = v2

        @pl.when(pl.program_id(0) == 0)
        def _():
            run(p0_ref)

        @pl.when(pl.program_id(0) == 1)
        def _():
            run(p1_ref)

    def p_spec(layer):
        row = (lambda l, j: jnp.where(l == 0, j, ADAM_ROW_BLOCKS - 1)) if layer == 0 else (lambda l, j: jnp.where(l == 1, j, 0))
        return pl.BlockSpec((N_DEV, rb, c), lambda l, j: (0, row(l, j), 0))

    blk = pl.BlockSpec((1, rb, c), lambda l, j: (l, j, 0))
    return pl.pallas_call(
        body, name=name, grid=(DEPTH, ADAM_ROW_BLOCKS),
        in_specs=[p_spec(0), p_spec(1), blk, blk, blk], out_specs=[blk] * 4,
        out_shape=[SDS(w.shape, F32)] * 4,
        compiler_params=_cparams(("arbitrary", "arbitrary")),
    )(parts0, parts1, w, m, v)


def parts_sum(parts0, parts1, name):
    _, r, c = parts0.shape

    def body(p0_ref, p1_ref, g_ref):
        def run(p_ref):
            g = p_ref[0].astype(F32)
            for i in range(1, N_DEV):
                g = g + p_ref[i].astype(F32)
            g_ref[0] = g

        @pl.when(pl.program_id(0) == 0)
        def _():
            run(p0_ref)

        @pl.when(pl.program_id(0) == 1)
        def _():
            run(p1_ref)

    full = pl.BlockSpec((N_DEV, r, c), lambda l: (0, 0, 0))
    return pl.pallas_call(
        body, name=name, grid=(DEPTH,), in_specs=[full, full],
        out_specs=pl.BlockSpec((1, r, c), lambda l: (l, 0, 0)), out_shape=SDS((DEPTH, r, c), F32),
        compiler_params=_cparams(("arbitrary",)),
    )(parts0, parts1)


def adam_given(g, w, m, v, name):
    _, r, c = w.shape

    def body(g_ref, w_ref, m_ref, v_ref, d_ref, m2_ref, v2_ref):
        delta, m2, v2 = _adam_math(w_ref[0], g_ref[0], m_ref[0], v_ref[0])
        d_ref[0] = delta
        m2_ref[0] = m2
        v2_ref[0] = v2

    blk = pl.BlockSpec((1, r, c), lambda l: (l, 0, 0))
    return pl.pallas_call(
        body, name=name, grid=(DEPTH,), in_specs=[blk] * 4, out_specs=[blk] * 3, out_shape=[SDS(w.shape, F32)] * 3,
        compiler_params=_cparams(("arbitrary",)),
    )(g, w, m, v)


def adam_small(parts, w, m, v, name):
    def body(p_ref, w_ref, m_ref, v_ref, g_ref, d_ref, m2_ref, v2_ref):
        g = p_ref[0]
        for i in range(1, N_DEV):
            g = g + p_ref[i]
        delta, m2, v2 = _adam_math(w_ref[...], g, m_ref[...], v_ref[...])
        g_ref[...] = g
        d_ref[...] = delta
        m2_ref[...] = m2
        v2_ref[...] = v2

    return pl.pallas_call(
        body, name=name, out_shape=[SDS(w.shape, F32)] * 4, compiler_params=_cparams(),
    )(parts, w, m, v)


def _rot_cols(w):
    w4 = w.reshape(w.shape[0], 4, 2, 32)
    return jnp.stack([-w4[:, :, 1], w4[:, :, 0]], axis=2).reshape(w.shape[0], ATT_W)


def _rot_cols_t(dw_rot):
    d4 = dw_rot.reshape(dw_rot.shape[0], 4, 2, 32)
    return jnp.stack([d4[:, :, 1], -d4[:, :, 0]], axis=2).reshape(dw_rot.shape[0], ATT_W)


def build_wext(w_in):
    aq, ak, av, pu = w_in[:, 0:256], w_in[:, 256:512], w_in[:, 512:768], w_in[:, 768:1024]
    dqkvz = w_in[:, 1024:3072]
    gates = jnp.pad(w_in[:, 3072:3080], ((0, 0), (0, GATE_W - 2 * DN_H)))
    return jnp.concatenate([aq, ak, av, _rot_cols(aq), _rot_cols(ak), dqkvz, gates, pu], axis=1)


def fold_dwext(d):
    b = EXT_ATT
    aq = d[:, 0:256] + _rot_cols_t(d[:, 768:1024])
    ak = d[:, 256:512] + _rot_cols_t(d[:, 1024:1280])
    av = d[:, 512:768]
    dqkvz = d[:, b:b + 2048]
    gates = d[:, b + R_G:b + R_G + 2 * DN_H]
    pu = d[:, b + R_PU:b + R_PU + 256]
    return jnp.concatenate([aq, ak, av, pu, dqkvz, gates], axis=1)


def _block_diag(pw):
    z = jnp.zeros((4, 64, 4, 64), pw.dtype)
    for g in range(4):
        z = z.at[g, :, g, :].set(pw[g])
    return z.reshape(POOL_W, POOL_W)


def _diag_blocks(m):
    m4 = m.reshape(4, 64, 4, 64)
    return jnp.stack([m4[g, :, g, :] for g in range(4)], axis=0)


def _lanes(v, reps):
    return jnp.repeat(v, reps)[None, :]


def layer_fwd(p, xa, cos, sin, l, host=None):
    host = host or {}

    def carried(key):
        return host[key][0] if key in host else None

    def done(key, xo):
        if key in host:
            host[key][1](xo)

    xb, xo = ffn_fwd(xa, p["n1"], *p["f1"], f"ffn1_fwd_{l}", carried("ffn1"))
    done("ffn1", xo)
    att, rest = mix_in_fwd(xb, p["nm"], p["wext"], cos, sin, f"mix_in_fwd_{l}")
    ols = [att_fwd_s(att, d, f"att_fwd_{l}_{d}") for d in DILATIONS]
    qkv = dn_prep_fwd(rest, p["conv"], f"dn_prep_fwd_{l}")
    dn, xo = dn_intra_fwd(qkv, rest, p["alog"], p["dtb"], f"dn_intra_fwd_{l}", carried("dn_intra"))
    done("dn_intra", xo)
    (odn, states), xo = dn_inter_fwd(*dn, f"dn_inter_fwd_{l}", carried("dn_inter"))
    done("dn_inter", xo)
    xc = mix_out_fwd(xb, ols[0], ols[1], ols[2], rest, odn, p["wbd"], p["scale"], p["onorm"], p["wout"], f"mix_out_fwd_{l}")
    xd, xo = ffn_fwd(xc, p["n2"], *p["f2"], f"ffn2_fwd_{l}", carried("ffn2"))
    done("ffn2", xo)
    return xd, dict(xa=xa, xb=xb, xc=xc, att=att, rest=rest, ols=ols, qkv=qkv, dn=dn, odn=odn, states=states)


def layer_bwd(p, s, dx, cos, sin, headsum, l, scatter=False, carry=None):
    blocks = lambda ws: [w_.reshape(N_DEV, FF_BLK, D_MODEL) for w_ in ws]
    (dx, *d_f2, d_n2), carried = ffn_bwd(s["xc"], dx, p["n2"], *p["f2"], f"ffn2_bwd_{l}", carry)
    (d_wout, dol1, dol4, dol16, dpooled, dodn, dz, dscale, donorm, dwbd) = mix_out_bwd(
        dx, s["ols"][0], s["ols"][1], s["ols"][2], s["rest"], s["odn"], p["wbd"], p["scale"], p["onorm"], p["wout"],
        headsum, f"mix_out_bwd_{l}")
    dpu = pool_bwd(dpooled, f"pool_bwd_{l}")
    f2 = blocks(d_f2)
    d_dn = dn_inter_bwd(*s["dn"], s["states"], dodn, f"dn_inter_bwd_{l}")
    dqkv, dg, dalog, ddtb = dn_intra_bwd(s["qkv"], s["rest"], p["alog"], p["dtb"], *d_dn, f"dn_intra_bwd_{l}")
    d_dqkv, dconv = dn_prep_bwd(s["rest"], p["conv"], dqkv, f"dn_prep_bwd_{l}")
    datts = [att_bwd_s(s["att"], ol, dol, d, f"att_bwd_{l}_{d}")
             for d, ol, dol in zip(DILATIONS, s["ols"], (dol1, dol4, dol16))]
    dproj = assemble_dproj(datts, cos, sin, d_dqkv, dz, dg, dpu, f"assemble_dproj_{l}")
    dx, d_wext, d_nm = linear_bwd(s["xb"], dx, p["nm"], dproj, p["wext"], f"mix_in_bwd_{l}")
    d_win = fold_dwext(d_wext).reshape(D_MODEL, N_DEV, IN_BLK).transpose(1, 0, 2).astype(BF16)
    io = [d_win, d_wout.reshape(N_DEV, D_MODEL // N_DEV, D_MODEL).astype(BF16)]
    (dx, *d_f1, d_n1), xo = ffn_bwd(s["xa"], dx, p["n1"], *p["f1"], f"ffn1_bwd_{l}",
                                    Exchange(f2 + io, "scatter") if scatter else None)
    if scatter:
        f2, io = list(xo[:3]), list(xo[3:])
    big = dict(f1=blocks(d_f1), f2=f2, io=io)
    small = dict(ffn1_norm=d_n1[0], mix_norm=d_nm[0], ffn2_norm=d_n2[0], pool_w=_diag_blocks(dwbd),
                 pool_scale=dscale[0], dn_a_log=dalog.reshape(DN_H, DN_E).sum(-1),
                 dn_dt_bias=ddtb.reshape(DN_H, DN_E).sum(-1),
                 dn_out_norm=donorm.reshape(DN_H, DN_E).sum(0), dn_conv_w=dconv)
    return dx, big, small, carried


def small_operands(l, pool_w, pool_scale, dn_out_norm, dn_a_log, dn_dt_bias, ffn1_norm, mix_norm, ffn2_norm):
    return dict(
        wbd=_block_diag(pool_w[l]).astype(BF16),
        scale=pool_scale[l][None, :],
        onorm=jnp.tile(dn_out_norm[l], DN_H)[None, :],
        alog=_lanes(dn_a_log[l], DN_E),
        dtb=_lanes(dn_dt_bias[l], DN_E),
        n1=ffn1_norm[l][None, :], nm=mix_norm[l][None, :], n2=ffn2_norm[l][None, :])


def set_mixer_weights(p, win_g, wout_g, conv_g):
    p["wext"] = build_wext(win_g.transpose(1, 0, 2).reshape(D_MODEL, IN_W))
    p["wout"] = wout_g.reshape(D_MODEL, D_MODEL)
    p["conv"] = conv_g.transpose(1, 0, 2).reshape(DN_CONV, 3 * DN_W)


def rope_tables(pos):
    inv_freq = 10000.0 ** (-jnp.arange(0, ATT_E, 2, dtype=F32) / ATT_E)
    ang = pos.astype(F32)[:, None] * inv_freq
    return jnp.tile(jnp.cos(ang), (1, 8)), jnp.tile(jnp.sin(ang), (1, 8))


def head_sum_matrix():
    return jnp.kron(jnp.eye(4, dtype=F32), jnp.ones((ATT_E, ATT_E), F32))


SMALL_NAMES = ("ffn1_norm", "mix_norm", "ffn2_norm", "pool_w", "pool_scale", "dn_a_log", "dn_dt_bias",
               "dn_out_norm", "final_norm", "dn_conv_w")


PACK_UNIT = 8 * 128


def _pack_rows(n):
    return -(-n // PACK_UNIT) * 8


def _pack(parts):
    rows = []
    for p in parts:
        flat = p.reshape(-1)
        r = _pack_rows(flat.shape[0])
        rows.append(jnp.pad(flat, (0, r * 128 - flat.shape[0])).reshape(r, 128))
    return jnp.concatenate(rows, axis=0)


def _unpack(packed, shapes):
    out, row = [], 0
    for s in shapes:
        n = math.prod(s)
        r = _pack_rows(n)
        out.append(packed[row:row + r].reshape(-1)[:n].reshape(s))
        row += r
    return out


def kernel(x, positions, ffn1_norm, ffn1_w_gate, ffn1_w_up, ffn1_w_down, mix_norm, w_in, pool_w, pool_scale, dn_conv_w, dn_a_log, dn_dt_bias, dn_out_norm, w_out, ffn2_norm, ffn2_w_gate, ffn2_w_up, ffn2_w_down, final_norm, loss_target, m_ffn1_norm, m_ffn1_w_gate, m_ffn1_w_up, m_ffn1_w_down, m_mix_norm, m_w_in, m_pool_w, m_pool_scale, m_dn_conv_w, m_dn_a_log, m_dn_dt_bias, m_dn_out_norm, m_w_out, m_ffn2_norm, m_ffn2_w_gate, m_ffn2_w_up, m_ffn2_w_down, m_final_norm, v_ffn1_norm, v_ffn1_w_gate, v_ffn1_w_up, v_ffn1_w_down, v_mix_norm, v_w_in, v_pool_w, v_pool_scale, v_dn_conv_w, v_dn_a_log, v_dn_dt_bias, v_dn_out_norm, v_w_out, v_ffn2_norm, v_ffn2_w_gate, v_ffn2_w_up, v_ffn2_w_down, v_final_norm):
    me = 4 * lax.axis_index("x") + 2 * lax.axis_index("y") + lax.axis_index("c")
    x0 = x[0]
    target = loss_target[0]

    cos, sin = rope_tables(positions[0])
    headsum = head_sum_matrix()

    layers = [small_operands(l, pool_w, pool_scale, dn_out_norm, dn_a_log, dn_dt_bias, ffn1_norm, mix_norm, ffn2_norm)
              for l in range(DEPTH)]

    def whole(gathered):
        return gathered.reshape(D_FF, D_MODEL)

    def gather_ffn1(l):
        def on_done(xo):
            layers[l]["f1"] = tuple(whole(g) for g in xo)
        return Exchange(ffn_shard_operands(ffn1_w_gate[l], ffn1_w_up[l], ffn1_w_down[l]), "gather"), on_done

    def gather_mixer(l):
        def on_done(xo):
            set_mixer_weights(layers[l], *xo)
        return Exchange([w_in[l].astype(BF16), w_out[l].astype(BF16), dn_conv_w[l]], "gather"), on_done

    gathered_f2 = {}

    def gather_ffn2_part(l, part):
        def on_done(xo):
            gathered_f2[(l, part)] = [whole(g) for g in xo]
            if (l, 0) in gathered_f2 and (l, 1) in gathered_f2:
                layers[l]["f2"] = tuple(gathered_f2[(l, 0)] + gathered_f2[(l, 1)])
        ops = ffn_shard_operands(ffn2_w_gate[l], ffn2_w_up[l], ffn2_w_down[l])
        return Exchange(ops[:2] if part == 0 else ops[2:], "gather"), on_done

    first, on_first = gather_ffn1(0)
    on_first(run_exchange(first, "gather_ffn1_0"))
    saved = []
    xa = x0
    for l in range(DEPTH):
        host = {"ffn1": gather_mixer(l), "dn_intra": gather_ffn2_part(l, 0), "dn_inter": gather_ffn2_part(l, 1)}
        if l + 1 < DEPTH:
            host["ffn2"] = gather_ffn1(l + 1)
        xa, s = layer_fwd(layers[l], xa, cos, sin, l, host)
        saved.append(s)

    loss_row, dx, d_final = loss_head(xa, final_norm[None, :], target, "loss_head")
    loss = lax.psum(loss_row[0, 0], ("x", "y", "c"))

    small = {}
    big_parts = [None] * DEPTH
    carry = None
    for l in reversed(range(DEPTH)):
        dx, big, small[l], carried = layer_bwd(layers[l], saved[l], dx, cos, sin, headsum, l, True, carry)
        if carried is not None:
            big_parts[l + 1]["f1"] = list(carried)
        big_parts[l] = big
        carry = Exchange(big["f1"], "scatter")
    grad_x = dx[None]

    small_shapes = {"ffn1_norm": (DEPTH, D_MODEL), "mix_norm": (DEPTH, D_MODEL), "ffn2_norm": (DEPTH, D_MODEL),
                    "pool_w": (DEPTH, 4, 64, 64), "pool_scale": (DEPTH, POOL_W), "dn_a_log": (DEPTH, DN_H),
                    "dn_dt_bias": (DEPTH, DN_H), "dn_out_norm": (DEPTH, DN_E), "final_norm": (D_MODEL,),
                    "dn_conv_w": (DEPTH, DN_CONV, 3 * DN_W)}
    g_small = {n: (d_final[0] if n == "final_norm" else jnp.stack([small[l][n] for l in range(DEPTH)]))
               for n in SMALL_NAMES}
    f1_parts, (small_parts,) = run_exchanges(
        [carry, Exchange([_pack([g_small[n] for n in SMALL_NAMES])], "gather")], "scatter_ffn1_0_gather_small")
    big_parts[0]["f1"] = f1_parts

    def conv_full(a):
        return lax.dynamic_update_slice(jnp.zeros((DEPTH, DN_CONV, 3 * DN_W), F32), a, (0, 0, me * (3 * DN_W // N_DEV)))

    given = dict(ffn1_norm=(ffn1_norm, m_ffn1_norm, v_ffn1_norm), mix_norm=(mix_norm, m_mix_norm, v_mix_norm),
                 ffn2_norm=(ffn2_norm, m_ffn2_norm, v_ffn2_norm), pool_w=(pool_w, m_pool_w, v_pool_w),
                 pool_scale=(pool_scale, m_pool_scale, v_pool_scale), dn_a_log=(dn_a_log, m_dn_a_log, v_dn_a_log),
                 dn_dt_bias=(dn_dt_bias, m_dn_dt_bias, v_dn_dt_bias),
                 dn_out_norm=(dn_out_norm, m_dn_out_norm, v_dn_out_norm),
                 final_norm=(final_norm, m_final_norm, v_final_norm),
                 dn_conv_w=(conv_full(dn_conv_w), conv_full(m_dn_conv_w), conv_full(v_dn_conv_w)))
    packed_wmv = [_pack([given[n][k] for n in SMALL_NAMES]) for k in range(3)]
    small_out = adam_small(small_parts, *packed_wmv, "adam_small")
    shapes = [small_shapes[n] for n in SMALL_NAMES]
    small_res = {n: [] for n in SMALL_NAMES}
    for arr in small_out:
        for n, v_ in zip(SMALL_NAMES, _unpack(arr, shapes)):
            if n == "dn_conv_w":
                v_ = lax.dynamic_slice(v_, (0, 0, me * (3 * DN_W // N_DEV)), (DEPTH, DN_CONV, 3 * DN_W // N_DEV))
            small_res[n].append(v_)

    def parts_of(group, idx):
        return [big_parts[l][group][idx] for l in range(DEPTH)]

    def adam_transposed(group, idx, w, m, v, name):
        g = parts_sum(*parts_of(group, idx), f"sum_{name}").transpose(0, 2, 1)
        return [g] + list(adam_given(g, w, m, v, f"adam_{name}"))

    big_res = dict(
        ffn1_w_gate=adam_transposed("f1", 0, ffn1_w_gate, m_ffn1_w_gate, v_ffn1_w_gate, "ffn1_gate"),
        ffn1_w_up=adam_transposed("f1", 1, ffn1_w_up, m_ffn1_w_up, v_ffn1_w_up, "ffn1_up"),
        ffn1_w_down=adam_shard(*parts_of("f1", 2), ffn1_w_down, m_ffn1_w_down, v_ffn1_w_down, "adam_ffn1_down"),
        ffn2_w_gate=adam_transposed("f2", 0, ffn2_w_gate, m_ffn2_w_gate, v_ffn2_w_gate, "ffn2_gate"),
        ffn2_w_up=adam_transposed("f2", 1, ffn2_w_up, m_ffn2_w_up, v_ffn2_w_up, "ffn2_up"),
        ffn2_w_down=adam_shard(*parts_of("f2", 2), ffn2_w_down, m_ffn2_w_down, v_ffn2_w_down, "adam_ffn2_down"),
        w_in=adam_shard(*parts_of("io", 0), w_in, m_w_in, v_w_in, "adam_w_in"),
        w_out=adam_shard(*parts_of("io", 1), w_out, m_w_out, v_w_out, "adam_w_out"),
    )

    order = ("ffn1_norm", "ffn1_w_gate", "ffn1_w_up", "ffn1_w_down", "mix_norm", "w_in", "pool_w", "pool_scale",
             "dn_conv_w", "dn_a_log", "dn_dt_bias", "dn_out_norm", "w_out", "ffn2_norm", "ffn2_w_gate", "ffn2_w_up",
             "ffn2_w_down", "final_norm")
    res = {**small_res, **big_res}
    outs = [loss, grad_x]
    for k in range(4):
        outs.extend(res[n][k] for n in order)
    return tuple(outs)
```

```python
import functools
import math

import jax
import jax.numpy as jnp
from jax import lax
from jax.experimental import pallas as pl
from jax.experimental.pallas import tpu as pltpu

F32 = jnp.float32
BF16 = jnp.bfloat16
HI = lax.Precision.HIGHEST
INV_PREC = lax.Precision.HIGH
SDS = jax.ShapeDtypeStruct

N_DEV = 8
SEQ = 4096
D_MODEL = 1024
DEPTH = 2
D_FF = 2816
FF_BLK = D_FF // N_DEV
ATT_W = 256
ATT_E = 64
ATT_BLK = 128
DILATIONS = (1, 4, 16)
POOL_W = 256
POOL_HALO = 16
DN_W = 512
DN_H = 4
DN_E = 128
DN_C = 64
N_CHUNK = SEQ // DN_C
IN_W = 3080
IN_BLK = IN_W // N_DEV
EPS = 1e-6
EXT_ATT = 1280
GATE_W = 256
EXT_REST = 4 * DN_W + GATE_W + POOL_W
EXT_W = EXT_ATT + EXT_REST
R_DQKV, R_DZ, R_G, R_PU = 0, 1536, 2048, 2304

ADAM_LR, ADAM_B1, ADAM_B2, ADAM_EPS, ADAM_WD, ADAM_STEP = 0.001, 0.9, 0.999, 1e-08, 0.01, 10

VMEM_LIMIT = 60 * 1024 * 1024
MESH = pl.DeviceIdType.MESH


def _cparams(sem=None):
    kw = dict(vmem_limit_bytes=VMEM_LIMIT)
    if sem is not None:
        kw["dimension_semantics"] = sem
    return pltpu.CompilerParams(**kw)


def _dot(a, b, prec=None):
    return jnp.dot(a, b, preferred_element_type=F32, precision=prec)


def _dot_nt(a, b, prec=None):
    return lax.dot_general(a, b, (((1,), (1,)), ((), ())), preferred_element_type=F32, precision=prec)


def _dot_tn(a, b, prec=None):
    return lax.dot_general(a, b, (((0,), (0,)), ((), ())), preferred_element_type=F32, precision=prec)


def _sigmoid(x):
    return jax.nn.sigmoid(x)


def _rms_stats(x):
    r = lax.rsqrt(jnp.mean(x * x, axis=-1, keepdims=True) + EPS)
    return x * r, r


def _rms_bwd(xh, r, w, dh):
    dxh = dh * w
    dx = r * (dxh - xh * jnp.mean(dxh * xh, axis=-1, keepdims=True))
    return dx, jnp.sum(dh * xh, axis=0, keepdims=True)


FFN_T_FWD = 2048
FFN_T_BWD = 512
FF_TILE = 256
N_FF_TILE = D_FF // FF_TILE


def ffn_shard_operands(gate, up, down):
    return [gate.T.astype(BF16), up.T.astype(BF16), down.astype(BF16)]


def ffn_fwd(x, nw, wgt, wut, wd, name, exch=None):
    t = FFN_T_FWD

    def body(x_ref, nw_ref, wgt_ref, wut_ref, wd_ref, o_ref, h_scr, acc_scr):
        k = pl.program_id(1)

        @pl.when(k == 0)
        def _():
            xh, _r = _rms_stats(x_ref[...])
            h_scr[...] = (xh * nw_ref[...]).astype(BF16)
            acc_scr[...] = jnp.zeros_like(acc_scr)

        h = h_scr[...]
        hg = _dot_nt(h, wgt_ref[...])
        hu = _dot_nt(h, wut_ref[...])
        a = (hg * _sigmoid(hg) * hu).astype(BF16)
        acc_scr[...] += _dot(a, wd_ref[...])

        @pl.when(k == N_FF_TILE - 1)
        def _():
            o_ref[...] = x_ref[...] + 0.5 * acc_scr[...]

    w_spec = pl.BlockSpec((FF_TILE, D_MODEL), lambda i, k: (k, 0))
    return _call(
        body, name=name, grid=(SEQ // t, N_FF_TILE),
        in_specs=[pl.BlockSpec((t, D_MODEL), lambda i, k: (i, 0)),
                  pl.BlockSpec((1, D_MODEL), lambda i, k: (0, 0)), w_spec, w_spec, w_spec],
        out_specs=pl.BlockSpec((t, D_MODEL), lambda i, k: (i, 0)),
        out_shape=SDS((SEQ, D_MODEL), F32),
        scratch_shapes=[pltpu.VMEM((t, D_MODEL), BF16), pltpu.VMEM((t, D_MODEL), F32)],
        sem=("arbitrary", "arbitrary"), args=(x, nw, wgt, wut, wd), exch=exch)


def ffn_bwd(x, dxo, nw, wgt, wut, wd, name, exch=None):
    t = FFN_T_BWD
    nt = SEQ // t

    def body(x_ref, dxo_ref, nw_ref, wgt_ref, wut_ref, wd_ref, dx_ref, dwgt_ref, dwut_ref, dwd_ref, dnw_ref,
             dh_scr, ag_scr, au_scr, ad_scr, h_scr):
        k = pl.program_id(0)
        i = pl.program_id(1)
        rows = pl.ds(pl.multiple_of(i * t, t), t)
        nw_v = nw_ref[...]

        @pl.when(k == 0)
        def _():
            xh0, _r0 = _rms_stats(x_ref[...])
            h_scr[rows, :] = (xh0 * nw_v).astype(BF16)

        h = h_scr[rows, :]
        dy = dxo_ref[...].astype(BF16)
        wgt = wgt_ref[...]
        wut = wut_ref[...]
        hg = _dot_nt(h, wgt)
        hu = _dot_nt(h, wut)
        sg = _sigmoid(hg)
        sil = hg * sg
        a = (sil * hu).astype(BF16)
        da = 0.5 * _dot_nt(dy, wd_ref[...])
        dhu = (da * sil).astype(BF16)
        dhg = (da * hu * (sg + sil * (1.0 - sg))).astype(BF16)
        p_d = _dot_tn(a, dy)
        p_g = _dot_tn(dhg, h)
        p_u = _dot_tn(dhu, h)
        dh = _dot(dhg, wgt) + _dot(dhu, wut)

        @pl.when(i == 0)
        def _():
            ad_scr[...] = p_d
            ag_scr[...] = p_g
            au_scr[...] = p_u

        @pl.when(i > 0)
        def _():
            ad_scr[...] += p_d
            ag_scr[...] += p_g
            au_scr[...] += p_u

        @pl.when(i == nt - 1)
        def _():
            dwd_ref[...] = (0.5 * ad_scr[...]).astype(BF16)
            dwgt_ref[...] = ag_scr[...].astype(BF16)
            dwut_ref[...] = au_scr[...].astype(BF16)

        @pl.when(k == 0)
        def _():
            dh_scr[rows, :] = dh

        @pl.when(k > 0)
        def _():
            dh_scr[rows, :] += dh

        @pl.when(jnp.logical_and(k == 0, i == 0))
        def _():
            dnw_ref[...] = jnp.zeros_like(dnw_ref)

        @pl.when(k == N_FF_TILE - 1)
        def _():
            xh, r = _rms_stats(x_ref[...])
            dx, dw = _rms_bwd(xh, r, nw_v, dh_scr[rows, :])
            dx_ref[...] = dxo_ref[...] + dx
            dnw_ref[...] += dw

    last = N_FF_TILE - 1
    w_spec = pl.BlockSpec((FF_TILE, D_MODEL), lambda k, i: (k, 0))
    return _call(
        body, name=name, grid=(N_FF_TILE, nt),
        in_specs=[pl.BlockSpec((t, D_MODEL), lambda k, i: (i, 0)),
                  pl.BlockSpec((t, D_MODEL), lambda k, i: (i, 0)),
                  pl.BlockSpec((1, D_MODEL), lambda k, i: (0, 0)), w_spec, w_spec, w_spec],
        out_specs=[pl.BlockSpec((t, D_MODEL), lambda k, i: (jnp.where(k == last, i, 0), 0)),
                   w_spec, w_spec, w_spec, pl.BlockSpec((1, D_MODEL), lambda k, i: (0, 0))],
        out_shape=[SDS((SEQ, D_MODEL), F32), SDS((D_FF, D_MODEL), BF16), SDS((D_FF, D_MODEL), BF16),
                   SDS((D_FF, D_MODEL), BF16), SDS((1, D_MODEL), F32)],
        scratch_shapes=[pltpu.VMEM((SEQ, D_MODEL), F32), pltpu.VMEM((FF_TILE, D_MODEL), F32),
                        pltpu.VMEM((FF_TILE, D_MODEL), F32), pltpu.VMEM((FF_TILE, D_MODEL), F32),
                        pltpu.VMEM((SEQ, D_MODEL), BF16)],
        sem=("arbitrary", "arbitrary"), args=(x, dxo, nw, wgt, wut, wd), exch=exch)


def loss_head(x, fw, target, name):
    t = 512

    def body(x_ref, fw_ref, tg_ref, loss_ref, dx_ref, dfw_ref):
        i = pl.program_id(0)
        xh, r = _rms_stats(x_ref[...])
        w = fw_ref[...]
        err = xh * w - tg_ref[...]
        part = 0.5 * jnp.sum(jnp.sum(err * err, axis=-1, keepdims=True), axis=0, keepdims=True) / D_MODEL
        dx, dw = _rms_bwd(xh, r, w, err * (1.0 / D_MODEL))
        dx_ref[...] = dx

        @pl.when(i == 0)
        def _():
            loss_ref[...] = jnp.zeros_like(loss_ref)
            dfw_ref[...] = jnp.zeros_like(dfw_ref)

        loss_ref[...] += jnp.broadcast_to(part, loss_ref.shape)
        dfw_ref[...] += dw

    return pl.pallas_call(
        body, name=name, grid=(SEQ // t,),
        in_specs=[pl.BlockSpec((t, D_MODEL), lambda i: (i, 0)),
                  pl.BlockSpec((1, D_MODEL), lambda i: (0, 0)),
                  pl.BlockSpec((t, D_MODEL), lambda i: (i, 0))],
        out_specs=[pl.BlockSpec((1, 128), lambda i: (0, 0)),
                   pl.BlockSpec((t, D_MODEL), lambda i: (i, 0)),
                   pl.BlockSpec((1, D_MODEL), lambda i: (0, 0))],
        out_shape=[SDS((1, 128), F32), SDS((SEQ, D_MODEL), F32), SDS((1, D_MODEL), F32)],
        compiler_params=_cparams(("arbitrary",)),
    )(x, fw, target)


MIX_T = 512


def _slabs_load(ref, first, n):
    return jnp.concatenate([ref[first + j] for j in range(n)], axis=1)


def _slabs_store(ref, first, val):
    for j in range(val.shape[1] // 128):
        ref[first + j] = val[:, 128 * j:128 * j + 128]


def _slab_spec(k, t):
    return pl.BlockSpec((k, t, 128), lambda i: (0, i, 0))


def mix_in_fwd(x, nw, wext, cos, sin, name):
    t = MIX_T

    def body(x_ref, nw_ref, w_ref, cos_ref, sin_ref, att_ref, rest_ref):
        xh, _r = _rms_stats(x_ref[...])
        h = (xh * nw_ref[...]).astype(BF16)
        pa = _dot(h, w_ref[:, 0:EXT_ATT])
        c = cos_ref[...]
        s = sin_ref[...]
        _slabs_store(att_ref, 0, pa[:, 0:256] * c + pa[:, 768:1024] * s)
        _slabs_store(att_ref, 2, pa[:, 256:512] * c + pa[:, 1024:1280] * s)
        _slabs_store(att_ref, 4, pa[:, 512:768])
        for j in range(EXT_REST // 256):
            rest_ref[:, 256 * j:256 * j + 256] = _dot(h, w_ref[:, EXT_ATT + 256 * j:EXT_ATT + 256 * j + 256])

    return pl.pallas_call(
        body, name=name, grid=(SEQ // t,),
        in_specs=[pl.BlockSpec((t, D_MODEL), lambda i: (i, 0)),
                  pl.BlockSpec((1, D_MODEL), lambda i: (0, 0)),
                  pl.BlockSpec((D_MODEL, EXT_W), lambda i: (0, 0)),
                  pl.BlockSpec((t, ATT_W), lambda i: (i, 0)),
                  pl.BlockSpec((t, ATT_W), lambda i: (i, 0))],
        out_specs=[_slab_spec(6, t),
                   pl.BlockSpec((t, EXT_REST), lambda i: (i, 0))],
        out_shape=[SDS((6, SEQ, 128), F32), SDS((SEQ, EXT_REST), F32)],
        compiler_params=_cparams(("arbitrary",)),
    )(x, nw, wext, cos, sin)


def assemble_dproj(datts, cos, sin, d_dqkv, dz, dg, dpu, name):
    t = 512

    def body(d1_ref, d4_ref, d16_ref, cos_ref, sin_ref, dqkv_ref, dz_ref, dg_ref, dpu_ref, o_ref):
        da6 = d1_ref[...] + d4_ref[...] + d16_ref[...]
        da = jnp.concatenate([da6[j] for j in range(6)], axis=1)
        c = cos_ref[...]
        s = sin_ref[...]
        dq = da[:, 0:256]
        dk = da[:, 256:512]
        o_ref[:, 0:256] = (dq * c).astype(BF16)
        o_ref[:, 256:512] = (dk * c).astype(BF16)
        o_ref[:, 512:768] = da[:, 512:768].astype(BF16)
        o_ref[:, 768:1024] = (dq * s).astype(BF16)
        o_ref[:, 1024:1280] = (dk * s).astype(BF16)
        b = EXT_ATT
        o_ref[:, b + R_DQKV:b + R_DQKV + 1536] = dqkv_ref[...].astype(BF16)
        o_ref[:, b + R_DZ:b + R_DZ + 512] = dz_ref[...].astype(BF16)
        o_ref[:, b + R_G:b + R_G + GATE_W] = dg_ref[...].astype(BF16)
        o_ref[:, b + R_PU:b + R_PU + 256] = dpu_ref[...].astype(BF16)

    row = lambda w: pl.BlockSpec((t, w), lambda i: (i, 0))
    return pl.pallas_call(
        body, name=name, grid=(SEQ // t,),
        in_specs=[_slab_spec(6, t), _slab_spec(6, t), _slab_spec(6, t),
                  row(256), row(256), row(1536), row(512), row(GATE_W), row(256)],
        out_specs=row(EXT_W),
        out_shape=SDS((SEQ, EXT_W), BF16),
        compiler_params=_cparams(("arbitrary",)),
    )(*datts, cos, sin, d_dqkv, dz, dg, dpu)


def linear_bwd(x, dxo, nw, dy, w, name):
    t = 512
    nb = 768
    n = w.shape[1]
    nt = SEQ // t
    nn = n // nb

    def body(x_ref, dxo_ref, nw_ref, dy_ref, w_ref, dx_ref, dw_ref, dnw_ref, dh_scr, h_scr):
        k = pl.program_id(0)
        i = pl.program_id(1)
        rows = pl.ds(pl.multiple_of(i * t, t), t)
        nw_v = nw_ref[...]

        @pl.when(k == 0)
        def _():
            xh0, _r0 = _rms_stats(x_ref[...])
            h_scr[rows, :] = (xh0 * nw_v).astype(BF16)

        h = h_scr[rows, :]
        dyv = dy_ref[...]
        p_w = _dot_tn(h, dyv)
        dh = _dot_nt(dyv, w_ref[...])

        @pl.when(i == 0)
        def _():
            dw_ref[...] = p_w

        @pl.when(i > 0)
        def _():
            dw_ref[...] += p_w

        @pl.when(k == 0)
        def _():
            dh_scr[rows, :] = dh

        @pl.when(k > 0)
        def _():
            dh_scr[rows, :] += dh

        @pl.when(jnp.logical_and(k == 0, i == 0))
        def _():
            dnw_ref[...] = jnp.zeros_like(dnw_ref)

        @pl.when(k == nn - 1)
        def _():
            xh, r = _rms_stats(x_ref[...])
            dx, dw = _rms_bwd(xh, r, nw_v, dh_scr[rows, :])
            dx_ref[...] = dxo_ref[...] + dx
            dnw_ref[...] += dw

    last = nn - 1
    return pl.pallas_call(
        body, name=name, grid=(nn, nt),
        in_specs=[pl.BlockSpec((t, D_MODEL), lambda k, i: (i, 0)),
                  pl.BlockSpec((t, D_MODEL), lambda k, i: (i, 0)),
                  pl.BlockSpec((1, D_MODEL), lambda k, i: (0, 0)),
                  pl.BlockSpec((t, nb), lambda k, i: (i, k)),
                  pl.BlockSpec((D_MODEL, nb), lambda k, i: (0, k))],
        out_specs=[pl.BlockSpec((t, D_MODEL), lambda k, i: (jnp.where(k == last, i, 0), 0)),
                   pl.BlockSpec((D_MODEL, nb), lambda k, i: (0, k)),
                   pl.BlockSpec((1, D_MODEL), lambda k, i: (0, 0))],
        out_shape=[SDS((SEQ, D_MODEL), F32), SDS((D_MODEL, n), F32), SDS((1, D_MODEL), F32)],
        scratch_shapes=[pltpu.VMEM((SEQ, D_MODEL), F32), pltpu.VMEM((SEQ, D_MODEL), BF16)],
        compiler_params=_cparams(("arbitrary", "arbitrary")),
    )(x, dxo, nw, dy, w)


def _att_masks():
    qi = lax.broadcasted_iota(jnp.int32, (ATT_BLK, ATT_BLK), 0)
    ki = lax.broadcasted_iota(jnp.int32, (ATT_BLK, ATT_BLK), 1)
    return ki <= qi, ki >= qi


NEG = -1e30


N_ATT_BLK = SEQ // ATT_BLK


def _class_rows(i, d):
    per_class = N_ATT_BLK // d
    shift = per_class.bit_length() - 1
    r = i >> shift
    j = i & (per_class - 1)
    span = ATT_BLK * d
    start = r + span * j
    prev = jnp.where(j == 0, start, start - span)
    nxt = jnp.where(j == per_class - 1, start, start + span)

    def rows(s0):
        if d == 1:
            return pl.ds(pl.multiple_of(s0, ATT_BLK), ATT_BLK)
        return pl.ds(s0, ATT_BLK, stride=d)

    return rows(start), rows(prev), rows(nxt), j != 0, j != per_class - 1


def _slab_heads(ref, slab, rows):
    x0 = ref[pl.ds(slab, 1), rows, :][0]
    x1 = ref[pl.ds(slab + 1, 1), rows, :][0]
    return jnp.stack([x0[:, 0:ATT_E], x0[:, ATT_E:], x1[:, 0:ATT_E], x1[:, ATT_E:]], axis=0)


def _put_slab_heads(ref, slab, rows, val):
    ref[pl.ds(slab, 1), rows, :] = jnp.concatenate([val[0], val[1]], axis=1)[None]
    ref[pl.ds(slab + 1, 1), rows, :] = jnp.concatenate([val[2], val[3]], axis=1)[None]


ATT_BLOCKS_PER_STEP = 2


def _resident_call(body, ins, out_slabs, name):
    n_in = len(ins)
    steps = N_ATT_BLK // ATT_BLOCKS_PER_STEP

    def wrapped(*refs):
        hbm_in, hbm_out = refs[:n_in], refs[n_in]
        vm_in, vm_out, sem = refs[n_in + 1:2 * n_in + 1], refs[2 * n_in + 1], refs[2 * n_in + 2]
        i = pl.program_id(0)

        @pl.when(i == 0)
        def _():
            copies = [pltpu.make_async_copy(h, v, sem.at[k]) for k, (h, v) in enumerate(zip(hbm_in, vm_in))]
            for cp in copies:
                cp.start()
            for cp in copies:
                cp.wait()

        for b in range(ATT_BLOCKS_PER_STEP):
            body(ATT_BLOCKS_PER_STEP * i + b, *vm_in, vm_out)

        @pl.when(i == steps - 1)
        def _():
            cp = pltpu.make_async_copy(vm_out, hbm_out, sem.at[n_in])
            cp.start()
            cp.wait()

    return pl.pallas_call(
        wrapped, name=name, grid=(steps,),
        in_specs=[ANY_SPEC] * n_in, out_specs=ANY_SPEC, out_shape=SDS((out_slabs, SEQ, 128), F32),
        scratch_shapes=[pltpu.VMEM(a.shape, a.dtype) for a in ins] + [pltpu.VMEM((out_slabs, SEQ, 128), F32),
                                                                      pltpu.SemaphoreType.DMA((n_in + 1,))],
        compiler_params=_cparams(("arbitrary",)),
    )(*ins)


def _att_fwd_math(ld, has_prev):
    m_d, m_p = _att_masks()
    m_p = jnp.logical_and(m_p, has_prev)
    q = ld("att", 0, "cur").astype(BF16)
    kc = ld("att", 2, "cur").astype(BF16)
    vc = ld("att", 4, "cur").astype(BF16)
    kp = ld("att", 2, "prev").astype(BF16)
    vp = ld("att", 4, "prev").astype(BF16)
    sd = jnp.where(m_d, _bdot(q, kc, 2, 2) * 0.125, NEG)
    sp = jnp.where(m_p, _bdot(q, kp, 2, 2) * 0.125, NEG)
    m = jnp.maximum(jnp.max(sd, axis=-1, keepdims=True), jnp.max(sp, axis=-1, keepdims=True))
    pd = jnp.exp(sd - m)
    pp = jnp.exp(sp - m)
    den = jnp.sum(pd, axis=-1, keepdims=True) + jnp.sum(pp, axis=-1, keepdims=True)
    inv = 1.0 / den
    o = _bdot((pd * inv).astype(BF16), vc, 2, 1) + _bdot((pp * inv).astype(BF16), vp, 2, 1)
    return o, jnp.broadcast_to(m + jnp.log(den), (4, ATT_BLK, ATT_E))


def _att_bwd_math(ld, has_prev, has_next):
    m_d, m_band = _att_masks()
    m_p = jnp.logical_and(m_band, has_prev)
    m_n = jnp.logical_and(m_band, has_next)

    def pair(q, k, v, lse, do, dterm, mask):
        s = jnp.where(mask, _bdot(q, k, 2, 2) * 0.125, NEG)
        p = jnp.exp(s - lse)
        dp = _bdot(do, v, 2, 2)
        ds = (p * (dp + dterm) * 0.125).astype(BF16)
        return p.astype(BF16), ds

    q_c = ld("att", 0, "cur").astype(BF16)
    k_c = ld("att", 2, "cur").astype(BF16)
    v_c = ld("att", 4, "cur").astype(BF16)
    k_p = ld("att", 2, "prev").astype(BF16)
    v_p = ld("att", 4, "prev").astype(BF16)
    q_n = ld("att", 0, "next").astype(BF16)
    o_c = ld("ol", 0, "cur")
    o_n = ld("ol", 0, "next")
    lse_c = ld("ol", 2, "cur")[:, :, 0:1]
    lse_n = ld("ol", 2, "next")[:, :, 0:1]
    do_c = ld("dol", 0, "cur")
    do_n = ld("dol", 0, "next")
    t_c = ld("dol", 2, "cur")[:, :, 0:1] - jnp.sum(do_c * o_c, axis=-1, keepdims=True)
    t_n = ld("dol", 2, "next")[:, :, 0:1] - jnp.sum(do_n * o_n, axis=-1, keepdims=True)
    do_cb = do_c.astype(BF16)
    do_nb = do_n.astype(BF16)
    p1, ds1 = pair(q_c, k_c, v_c, lse_c, do_cb, t_c, m_d)
    _p2, ds2 = pair(q_c, k_p, v_p, lse_c, do_cb, t_c, m_p)
    p3, ds3 = pair(q_n, k_c, v_c, lse_n, do_nb, t_n, m_n)
    return (_bdot(ds1, k_c, 2, 1) + _bdot(ds2, k_p, 2, 1), _bdot(ds1, q_c, 1, 1) + _bdot(ds3, q_n, 1, 1),
            _bdot(p1, do_cb, 1, 1) + _bdot(p3, do_nb, 1, 1))


ROWS_A = pl.ds(0, ATT_BLK)
ROWS_B = pl.ds(ATT_BLK, ATT_BLK)
N_ATT_PAIR = N_ATT_BLK // 2


def _pair_spec(k):
    return pl.BlockSpec((k, 2 * ATT_BLK, 128), lambda i: (0, i, 0))


def _before_pair_spec(k):
    return pl.BlockSpec((k, ATT_BLK, 128), lambda i: (0, jnp.maximum(2 * i - 1, 0), 0))


def _after_pair_spec(k):
    return pl.BlockSpec((k, ATT_BLK, 128), lambda i: (0, jnp.minimum(2 * i + 2, N_ATT_BLK - 1), 0))


def att_fwd_s(att, d, name):
    if d == 1:
        def body1(cur_ref, prev_ref, o_ref):
            i = pl.program_id(0)
            for rows, views, has_prev in (
                    (ROWS_A, {"cur": (cur_ref, ROWS_A), "prev": (prev_ref, ROWS_A)}, i != 0),
                    (ROWS_B, {"cur": (cur_ref, ROWS_B), "prev": (cur_ref, ROWS_A)}, True)):
                o, lse = _att_fwd_math(lambda _a, slab, where, v=views: _slab_heads(v[where][0], slab, v[where][1]), has_prev)
                _put_slab_heads(o_ref, 0, rows, o)
                _put_slab_heads(o_ref, 2, rows, lse)

        return pl.pallas_call(
            body1, name=name, grid=(N_ATT_PAIR,),
            in_specs=[_pair_spec(6), _before_pair_spec(6)], out_specs=_pair_spec(4),
            out_shape=SDS((4, SEQ, 128), F32), compiler_params=_cparams(("arbitrary",)),
        )(att, att)

    def body(i, att_ref, o_ref):
        cur, prev, _nxt, has_prev, _has_next = _class_rows(i, d)
        rows = {"cur": cur, "prev": prev}
        o, lse = _att_fwd_math(lambda _a, slab, where: _slab_heads(att_ref, slab, rows[where]), has_prev)
        _put_slab_heads(o_ref, 0, cur, o)
        _put_slab_heads(o_ref, 2, cur, lse)

    return _resident_call(body, [att], 4, name)


def att_bwd_s(att, ol, dol, d, name):
    if d == 1:
        def body1(a_p, a_c, a_n, ol_c, ol_n, dol_c, dol_n, d_ref):
            i = pl.program_id(0)
            first = {("att", "prev"): (a_p, ROWS_A), ("att", "cur"): (a_c, ROWS_A), ("att", "next"): (a_c, ROWS_B),
                     ("ol", "cur"): (ol_c, ROWS_A), ("ol", "next"): (ol_c, ROWS_B),
                     ("dol", "cur"): (dol_c, ROWS_A), ("dol", "next"): (dol_c, ROWS_B)}
            second = {("att", "prev"): (a_c, ROWS_A), ("att", "cur"): (a_c, ROWS_B), ("att", "next"): (a_n, ROWS_A),
                      ("ol", "cur"): (ol_c, ROWS_B), ("ol", "next"): (ol_n, ROWS_A),
                      ("dol", "cur"): (dol_c, ROWS_B), ("dol", "next"): (dol_n, ROWS_A)}
            for rows, views, has_prev, has_next in ((ROWS_A, first, i != 0, True),
                                                    (ROWS_B, second, True, i != N_ATT_PAIR - 1)):
                dq, dk, dv = _att_bwd_math(
                    lambda a, slab, where, v=views: _slab_heads(v[(a, where)][0], slab, v[(a, where)][1]), has_prev, has_next)
                _put_slab_heads(d_ref, 0, rows, dq)
                _put_slab_heads(d_ref, 2, rows, dk)
                _put_slab_heads(d_ref, 4, rows, dv)

        return pl.pallas_call(
            body1, name=name, grid=(N_ATT_PAIR,),
            in_specs=[_before_pair_spec(6), _pair_spec(6), _after_pair_spec(6), _pair_spec(4), _after_pair_spec(4),
                      _pair_spec(4), _after_pair_spec(4)],
            out_specs=_pair_spec(6), out_shape=SDS((6, SEQ, 128), F32), compiler_params=_cparams(("arbitrary",)),
        )(att, att, att, ol, ol, dol, dol)

    def body(i, att_ref, ol_ref, dol_ref, d_ref):
        cur, prev, nxt, has_prev, has_next = _class_rows(i, d)
        rows = {"cur": cur, "prev": prev, "next": nxt}
        refs = {"att": att_ref, "ol": ol_ref, "dol": dol_ref}
        dq, dk, dv = _att_bwd_math(lambda a, slab, where: _slab_heads(refs[a], slab, rows[where]), has_prev, has_next)
        _put_slab_heads(d_ref, 0, cur, dq)
        _put_slab_heads(d_ref, 2, cur, dk)
        _put_slab_heads(d_ref, 4, cur, dv)

    return _resident_call(body, [att, ol, dol], 6, name)


def _shift_down(x, k):
    rows = lax.broadcasted_iota(jnp.int32, x.shape, 0)
    return jnp.where(rows >= k, pltpu.roll(x, k, 0), 0.0)


def _shift_up(x, k):
    n = x.shape[0]
    rows = lax.broadcasted_iota(jnp.int32, x.shape, 0)
    return jnp.where(rows < n - k, pltpu.roll(x, n - k, 0), 0.0)


@functools.partial(jax.custom_vjp, nondiff_argnums=(1,))
def _delay(x, k):
    return _shift_down(x, k)


def _delay_fwd(x, k):
    return _shift_down(x, k), None


def _delay_bwd(k, _res, g):
    return (_shift_up(g, k),)


_delay.defvjp(_delay_fwd, _delay_bwd)

DN_CONV = 4


def _dn_prep_fn(u, w, kind):
    y = w[DN_CONV - 1:DN_CONV] * u
    for j in range(DN_CONV - 1):
        y = y + w[j:j + 1] * _delay(u, DN_CONV - 1 - j)
    y = y * _sigmoid(y)
    nrm = y * lax.rsqrt(jnp.sum(y * y, axis=-1, keepdims=True) + EPS)
    return jnp.where(kind == 0, nrm * (DN_E ** -0.5), jnp.where(kind == 1, nrm, y))


def dn_prep_fwd(rest, conv_w, name):
    def body(u_ref, w_ref, o_ref):
        j = pl.program_id(0)
        kind = (j >= DN_H).astype(jnp.int32) + (j >= 2 * DN_H).astype(jnp.int32)
        o_ref[...] = _dn_prep_fn(u_ref[...], w_ref[...], kind)

    return pl.pallas_call(
        body, name=name, grid=(3 * DN_H,),
        in_specs=[pl.BlockSpec((SEQ, DN_E), lambda j: (0, j)),
                  pl.BlockSpec((DN_CONV, DN_E), lambda j: (0, j))],
        out_specs=pl.BlockSpec((SEQ, DN_E), lambda j: (0, j)),
        out_shape=SDS((SEQ, 3 * DN_W), F32),
        compiler_params=_cparams(("arbitrary",)),
    )(rest, conv_w)


def dn_prep_bwd(rest, conv_w, dqkv, name):
    def body(u_ref, w_ref, g_ref, du_ref, dw_ref):
        j = pl.program_id(0)
        kind = (j >= DN_H).astype(jnp.int32) + (j >= 2 * DN_H).astype(jnp.int32)
        _y, vjp = jax.vjp(lambda u, w: _dn_prep_fn(u, w, kind), u_ref[...], w_ref[...])
        du, dw = vjp(g_ref[...])
        du_ref[...] = du
        dw_ref[...] = dw

    return pl.pallas_call(
        body, name=name, grid=(3 * DN_H,),
        in_specs=[pl.BlockSpec((SEQ, DN_E), lambda j: (0, j)),
                  pl.BlockSpec((DN_CONV, DN_E), lambda j: (0, j)),
                  pl.BlockSpec((SEQ, DN_E), lambda j: (0, j))],
        out_specs=[pl.BlockSpec((SEQ, DN_E), lambda j: (0, j)),
                   pl.BlockSpec((DN_CONV, DN_E), lambda j: (0, j))],
        out_shape=[SDS((SEQ, 3 * DN_W), F32), SDS((DN_CONV, 3 * DN_W), F32)],
        compiler_params=_cparams(("arbitrary",)),
    )(rest, conv_w, dqkv)


def _bdot(a, b, ca, cb, prec=None):
    return lax.dot_general(a, b, (((ca,), (cb,)), ((0,), (0,))), preferred_element_type=F32, precision=prec)


def _unit_lower_inverse(a):
    eye = (lax.broadcasted_iota(jnp.int32, (DN_C, DN_C), 0) == lax.broadcasted_iota(jnp.int32, (DN_C, DN_C), 1)).astype(F32)
    p = eye - a
    b = _bdot(a, a, 2, 1, INV_PREC)
    for lvl in range(5):
        p = p + _bdot(p, b, 2, 1, INV_PREC)
        if lvl < 4:
            b = _bdot(b, b, 2, 1, INV_PREC)
    return p


@jax.custom_vjp
def _tri_inv(a):
    return _unit_lower_inverse(a)


def _tri_inv_fwd(a):
    t = _unit_lower_inverse(a)
    return t, t


def _tri_inv_bwd(t, g):
    return (-_bdot(_bdot(t, g, 1, 1, INV_PREC), t, 2, 2, INV_PREC),)


_tri_inv.defvjp(_tri_inv_fwd, _tri_inv_bwd)


def _b16(x):
    return x.astype(BF16)


def _heads(ref, base=0):
    return jnp.stack([ref[:, base + DN_E * hd:base + DN_E * hd + DN_E] for hd in range(DN_H)], axis=0)


def _put_heads(ref, val, base=0):
    for hd in range(DN_H):
        ref[:, base + DN_E * hd:base + DN_E * hd + DN_E] = val[hd]


DN_G_LOG2 = 3
DN_G = 1 << DN_G_LOG2
N_INST = DN_G * DN_H


def _dn_intra(q, k, v, bb, ab, alog, dtb):
    ri = lax.broadcasted_iota(jnp.int32, (DN_C, DN_C), 0)
    ci = lax.broadcasted_iota(jnp.int32, (DN_C, DN_C), 1)
    lower = ri >= ci
    strict = ri > ci
    nh = q.shape[0]
    beta = _sigmoid(bb)
    xg = ab + dtb
    softplus = jnp.maximum(xg, 0.0) + jnp.log(1.0 + jnp.exp(-jnp.abs(xg)))
    gi = -jnp.exp(alog) * softplus
    g = _bdot(jnp.broadcast_to(lower.astype(F32), (nh, DN_C, DN_C)), gi, 2, 1, HI)
    eg = jnp.exp(g)
    kb = k * beta
    vb = v * beta
    g_col = g[:, :, 0:DN_C]
    g_row = _bdot(jnp.full((nh, DN_C, DN_E), 1.0 / DN_E, F32), g, 2, 2, HI)
    decay = jnp.where(lower, jnp.exp(jnp.where(lower, g_col - g_row, 0.0)), 0.0)
    kbf = _b16(k)
    a = jnp.where(strict, _bdot(_b16(kb), kbf, 2, 2) * decay, 0.0)
    tb = _b16(_tri_inv(a))
    u = _bdot(tb, _b16(vb), 2, 1)
    w = _bdot(tb, _b16(kb * eg), 2, 1)
    intra = jnp.where(lower, _bdot(_b16(q), kbf, 2, 2) * decay, 0.0)
    g_last = g[:, DN_C - 1:DN_C, :]
    return u, w, q * eg, k * jnp.exp(g_last - g), intra, jnp.exp(g_last)


def _dn_inter(u, w, qg, kdec, intra, egl, state):
    sb = _b16(state)
    v_new = u - _bdot(_b16(w), sb, 2, 1)
    o = _bdot(_b16(qg), sb, 2, 1) + _bdot(_b16(intra), _b16(v_new), 2, 1)
    return o, state * egl + _bdot(_b16(kdec), _b16(v_new), 1, 1)


def _inst(ref, base=0):
    per_head = [ref[:, base + DN_E * hd:base + DN_E * hd + DN_E].reshape(DN_G, DN_C, DN_E) for hd in range(DN_H)]
    return jnp.concatenate(per_head, axis=0)


def _inst_rows(ref):
    rows = [jnp.broadcast_to(ref[:, DN_E * hd:DN_E * hd + DN_E][None], (DN_G, 1, DN_E)) for hd in range(DN_H)]
    return jnp.concatenate(rows, axis=0)


def _put_inst(ref, val, width=DN_E, base=0):
    for hd in range(DN_H):
        ref[:, base + width * hd:base + width * hd + width] = val[DN_G * hd:DN_G * hd + DN_G].reshape(DN_G * DN_C, width)


@jax.custom_vjp
def _spread_gates(gates):
    t = gates.shape[0]
    return jnp.concatenate([jnp.broadcast_to(gates[:, j:j + 1], (t, DN_E)) for j in range(2 * DN_H)], axis=1)


def _spread_gates_fwd(gates):
    return _spread_gates(gates), None


def _spread_gates_bwd(_res, g):
    t = g.shape[0]
    lane = lax.broadcasted_iota(jnp.int32, (t, GATE_W), 1)
    out = jnp.zeros((t, GATE_W), F32)
    for j in range(2 * DN_H):
        s = jnp.sum(g[:, DN_E * j:DN_E * j + DN_E], axis=-1, keepdims=True)
        out = jnp.where(lane == j, s, out)
    return (out,)


_spread_gates.defvjp(_spread_gates_fwd, _spread_gates_bwd)


def _dn_intra_from_gates(q, k, v, gates, alog, dtb):
    wide = _spread_gates(gates)
    inst = lambda base: jnp.concatenate(
        [wide[:, base + DN_E * hd:base + DN_E * hd + DN_E].reshape(DN_G, DN_C, DN_E) for hd in range(DN_H)], axis=0)
    return _dn_intra(q, k, v, inst(0), inst(DN_W), alog, dtb)


def _intra_args(qkv_ref, g_ref, alog_ref, dtb_ref):
    return (_inst(qkv_ref), _inst(qkv_ref, DN_W), _inst(qkv_ref, 2 * DN_W), g_ref[...],
            _inst_rows(alog_ref), _inst_rows(dtb_ref))


def _intra_in_specs():
    t = DN_G * DN_C
    return [pl.BlockSpec((t, 3 * DN_W), lambda n: (n, 0)),
            pl.BlockSpec((t, GATE_W), lambda n: (n, R_G // GATE_W)),
            pl.BlockSpec((1, DN_W), lambda n: (0, 0)),
            pl.BlockSpec((1, DN_W), lambda n: (0, 0))]


def dn_intra_fwd(qkv, rest, alog_b, dtb_b, name, exch=None):
    t = DN_G * DN_C

    def body(qkv_ref, g_ref, alog_ref, dtb_ref, u_ref, w_ref, qg_ref, kd_ref, in_ref, egl_ref):
        u, w, qg, kdec, intra, egl = _dn_intra_from_gates(*_intra_args(qkv_ref, g_ref, alog_ref, dtb_ref))
        _put_inst(u_ref, u)
        _put_inst(w_ref, w.astype(BF16))
        _put_inst(qg_ref, qg.astype(BF16))
        _put_inst(kd_ref, kdec.astype(BF16))
        _put_inst(in_ref, intra.astype(BF16), DN_C)
        for hd in range(DN_H):
            egl_ref[:, DN_E * hd:DN_E * hd + DN_E] = egl[DN_G * hd:DN_G * hd + DN_G].reshape(DN_G, DN_E)

    row = lambda w_: pl.BlockSpec((t, w_), lambda n: (n, 0))
    return _call(
        body, name=name, grid=(N_CHUNK // DN_G,), in_specs=_intra_in_specs(),
        out_specs=[row(DN_W), row(DN_W), row(DN_W), row(DN_W), row(DN_H * DN_C),
                   pl.BlockSpec((DN_G, DN_W), lambda n: (n, 0))],
        out_shape=[SDS((SEQ, DN_W), F32), SDS((SEQ, DN_W), BF16), SDS((SEQ, DN_W), BF16), SDS((SEQ, DN_W), BF16),
                   SDS((SEQ, DN_H * DN_C), BF16), SDS((N_CHUNK, DN_W), F32)],
        scratch_shapes=[], sem=("arbitrary",), args=(qkv, rest, alog_b, dtb_b), exch=exch)


def dn_intra_bwd(qkv, rest, alog_b, dtb_b, du, dw, dqg, dkd, dintra, degl, name):
    t = DN_G * DN_C

    def body(qkv_ref, g_ref, alog_ref, dtb_ref, du_ref, dw_ref, dqg_ref, dkd_ref, din_ref, degl_ref,
             dqkv_ref, dg_ref, dalog_ref, ddtb_ref):
        @pl.when(pl.program_id(0) == 0)
        def _():
            dalog_ref[...] = jnp.zeros_like(dalog_ref)
            ddtb_ref[...] = jnp.zeros_like(ddtb_ref)

        _out, vjp = jax.vjp(_dn_intra_from_gates, *_intra_args(qkv_ref, g_ref, alog_ref, dtb_ref))
        d_in = jnp.concatenate([din_ref[:, DN_C * hd:DN_C * hd + DN_C].reshape(DN_G, DN_C, DN_C) for hd in range(DN_H)], axis=0)
        d_egl = jnp.concatenate([degl_ref[:, DN_E * hd:DN_E * hd + DN_E].reshape(DN_G, 1, DN_E) for hd in range(DN_H)], axis=0)
        dq, dk, dv, dg, dalog, ddtb = vjp((_inst(du_ref), _inst(dw_ref), _inst(dqg_ref), _inst(dkd_ref), d_in, d_egl))
        _put_inst(dqkv_ref, dq)
        _put_inst(dqkv_ref, dk, DN_E, DN_W)
        _put_inst(dqkv_ref, dv, DN_E, 2 * DN_W)
        dg_ref[...] = dg
        for hd in range(DN_H):
            sl = slice(DN_E * hd, DN_E * hd + DN_E)
            dalog_ref[:, sl] += jnp.sum(dalog[DN_G * hd:DN_G * hd + DN_G], axis=0)
            ddtb_ref[:, sl] += jnp.sum(ddtb[DN_G * hd:DN_G * hd + DN_G], axis=0)

    row = lambda w_: pl.BlockSpec((t, w_), lambda n: (n, 0))
    acc = pl.BlockSpec((1, DN_W), lambda n: (0, 0))
    return pl.pallas_call(
        body, name=name, grid=(N_CHUNK // DN_G,),
        in_specs=_intra_in_specs() + [row(DN_W), row(DN_W), row(DN_W), row(DN_W), row(DN_H * DN_C),
                                      pl.BlockSpec((DN_G, DN_W), lambda n: (n, 0))],
        out_specs=[row(3 * DN_W), row(GATE_W), acc, acc],
        out_shape=[SDS((SEQ, 3 * DN_W), F32), SDS((SEQ, GATE_W), F32), SDS((1, DN_W), F32), SDS((1, DN_W), F32)],
        compiler_params=_cparams(("arbitrary",)),
    )(qkv, rest, alog_b, dtb_b, du, dw, dqg, dkd, dintra, degl)


DN_RUN_LOG2 = 3
DN_RUN = 1 << DN_RUN_LOG2


def _chunk_rows(ref, c):
    return ref.at[pl.ds(DN_C * c, DN_C), :]


def _inter_args(u_ref, w_ref, qg_ref, kd_ref, in_ref, egl_ref, n, state):
    f = lambda r: _heads(r).astype(F32)
    intra = jnp.stack([in_ref[:, DN_C * hd:DN_C * hd + DN_C] for hd in range(DN_H)], axis=0).astype(F32)
    egl = _heads(egl_ref.at[pl.ds(n & (DN_G - 1), 1), :])
    return f(u_ref), f(w_ref), f(qg_ref), f(kd_ref), intra, egl, state


def dn_inter_fwd(u, w, qg, kdec, intra, egl, name, exch=None):
    def body(u_ref, w_ref, qg_ref, kd_ref, in_ref, egl_ref, o_ref, st_ref, state_scr):
        n2 = pl.program_id(0)

        @pl.when(n2 == 0)
        def _():
            state_scr[...] = jnp.zeros_like(state_scr)

        for c in range(DN_RUN):
            v = functools.partial(_chunk_rows, c=c)
            st = state_scr[...]
            st_ref[c] = st
            o, ns = _dn_inter(*_inter_args(v(u_ref), v(w_ref), v(qg_ref), v(kd_ref), v(in_ref), egl_ref,
                                           DN_RUN * n2 + c, st))
            _put_heads(v(o_ref), o)
            state_scr[...] = ns

    row = lambda w_: pl.BlockSpec((DN_RUN * DN_C, w_), lambda n: (n, 0))
    return _call(
        body, name=name, grid=(N_CHUNK // DN_RUN,),
        in_specs=[row(DN_W), row(DN_W), row(DN_W), row(DN_W), row(DN_H * DN_C),
                  pl.BlockSpec((DN_G, DN_W), lambda n: (n >> (DN_G_LOG2 - DN_RUN_LOG2), 0))],
        out_specs=[row(DN_W), pl.BlockSpec((DN_RUN, DN_H, DN_E, DN_E), lambda n: (n, 0, 0, 0))],
        out_shape=[SDS((SEQ, DN_W), F32), SDS((N_CHUNK, DN_H, DN_E, DN_E), F32)],
        scratch_shapes=[pltpu.VMEM((DN_H, DN_E, DN_E), F32)],
        sem=("arbitrary",), args=(u, w, qg, kdec, intra, egl), exch=exch)


def dn_inter_bwd(u, w, qg, kdec, intra, egl, states, do, name):
    last = N_CHUNK // DN_RUN - 1

    def body(u_ref, w_ref, qg_ref, kd_ref, in_ref, egl_ref, st_ref, do_ref,
             du_ref, dw_ref, dqg_ref, dkd_ref, din_ref, degl_ref, dstate_scr):
        s = pl.program_id(0)

        @pl.when(s == 0)
        def _():
            dstate_scr[...] = jnp.zeros_like(dstate_scr)

        for c in reversed(range(DN_RUN)):
            n = DN_RUN * (last - s) + c
            v = functools.partial(_chunk_rows, c=c)
            _out, vjp = jax.vjp(_dn_inter, *_inter_args(v(u_ref), v(w_ref), v(qg_ref), v(kd_ref), v(in_ref), egl_ref,
                                                        n, st_ref[c]))
            du, dw, dqg, dkd, din, degl, dst = vjp((_heads(v(do_ref)), dstate_scr[...]))
            _put_heads(v(du_ref), du)
            _put_heads(v(dw_ref), dw)
            _put_heads(v(dqg_ref), dqg)
            _put_heads(v(dkd_ref), dkd)
            for hd in range(DN_H):
                v(din_ref)[:, DN_C * hd:DN_C * hd + DN_C] = din[hd]
            row = n & (DN_G - 1)

            @pl.when(row == DN_G - 1)
            def _():
                degl_ref[...] = jnp.zeros_like(degl_ref)

            new_row = jnp.concatenate([degl[hd] for hd in range(DN_H)], axis=1)
            rows = lax.broadcasted_iota(jnp.int32, (DN_G, DN_W), 0)
            degl_ref[...] = jnp.where(rows == row, jnp.broadcast_to(new_row, (DN_G, DN_W)), degl_ref[...])
            dstate_scr[...] = dst

    rev = lambda w_: pl.BlockSpec((DN_RUN * DN_C, w_), lambda s: (last - s, 0))
    grp = pl.BlockSpec((DN_G, DN_W), lambda s: ((last - s) >> (DN_G_LOG2 - DN_RUN_LOG2), 0))
    return pl.pallas_call(
        body, name=name, grid=(N_CHUNK // DN_RUN,),
        in_specs=[rev(DN_W), rev(DN_W), rev(DN_W), rev(DN_W), rev(DN_H * DN_C), grp,
                  pl.BlockSpec((DN_RUN, DN_H, DN_E, DN_E), lambda s: (last - s, 0, 0, 0)), rev(DN_W)],
        out_specs=[rev(DN_W), rev(DN_W), rev(DN_W), rev(DN_W), rev(DN_H * DN_C), grp],
        out_shape=[SDS((SEQ, DN_W), F32)] * 4 + [SDS((SEQ, DN_H * DN_C), F32), SDS((N_CHUNK, DN_W), F32)],
        scratch_shapes=[pltpu.VMEM((DN_H, DN_E, DN_E), F32)],
        compiler_params=_cparams(("arbitrary",)),
    )(u, w, qg, kdec, intra, egl, states, do)


OUT_T = 512


def _pool_consts(rows_total, t0, halo_before):
    lane = lax.broadcasted_iota(jnp.int32, (rows_total, POOL_W), 1)
    row = lax.broadcasted_iota(jnp.int32, (rows_total, POOL_W), 0)
    grp = (lane >= 64).astype(jnp.int32) + (lane >= 128).astype(jnp.int32) + (lane >= 192).astype(jnp.int32)
    win = jnp.where(grp == 0, 2, jnp.where(grp == 1, 4, jnp.where(grp == 2, 8, 16)))
    pos = t0 + row - halo_before
    cnt = jnp.minimum(pos + 1, win).astype(F32)
    return grp, cnt


def _pool_select(grp, s2, s4, s8, s16):
    return jnp.where(grp == 0, s2, jnp.where(grp == 1, s4, jnp.where(grp == 2, s8, s16)))


def _pooled(u_ext, t0):
    n = u_ext.shape[0]
    grp, cnt = _pool_consts(n, t0, POOL_HALO)
    s2 = u_ext + pltpu.roll(u_ext, 1, 0)
    s4 = s2 + pltpu.roll(s2, 2, 0)
    s8 = s4 + pltpu.roll(s4, 4, 0)
    s16 = s8 + pltpu.roll(s8, 8, 0)
    out = _pool_select(grp, s2, s4, s8, s16) / jnp.maximum(cnt, 1.0) - u_ext
    return out[POOL_HALO:, :]


def _merge_weights(l1, l4, l16):
    m = jnp.maximum(jnp.maximum(l1, l4), l16)
    e1 = jnp.exp(l1 - m)
    e4 = jnp.exp(l4 - m)
    e16 = jnp.exp(l16 - m)
    inv = 1.0 / (e1 + e4 + e16)
    return e1 * inv, e4 * inv, e16 * inv


def _out_parts(ol1_ref, ol4_ref, ol16_ref, pu_ref, puh_ref, odn_ref, z_ref, wbd_ref, i, t):
    w1, w4, w16 = _merge_weights(_slabs_load(ol1_ref, 2, 2), _slabs_load(ol4_ref, 2, 2), _slabs_load(ol16_ref, 2, 2))
    ya = w1 * _slabs_load(ol1_ref, 0, 2) + w4 * _slabs_load(ol4_ref, 0, 2) + w16 * _slabs_load(ol16_ref, 0, 2)
    halo = jnp.where(i > 0, puh_ref[...], 0.0)
    pooled = _pooled(jnp.concatenate([halo, pu_ref[...]], axis=0), i * t)
    pw = _dot(pooled.astype(BF16), wbd_ref[...])
    return ya, pooled, pw, (w1, w4, w16)


def _out_specs_common(t):
    def row(w, cb=0):
        return pl.BlockSpec((t, w), lambda i: (i, cb))

    halo = pl.BlockSpec((POOL_HALO, POOL_W),
                        lambda i: (jnp.maximum(i * (t // POOL_HALO) - 1, 0), R_PU // POOL_W))
    full = lambda a, b: pl.BlockSpec((a, b), lambda i: (0, 0))
    return [_slab_spec(4, t), _slab_spec(4, t), _slab_spec(4, t), row(POOL_W, R_PU // POOL_W), halo, row(DN_W), row(DN_W, R_DZ // DN_W),
            full(POOL_W, POOL_W), full(1, POOL_W), full(1, DN_W), full(D_MODEL, D_MODEL)]


def mix_out_fwd(x, ol1, ol4, ol16, rest, odn, wbd, scale, onorm_b, wout, name):
    t = OUT_T

    def body(x_ref, ol1_ref, ol4_ref, ol16_ref, pu_ref, puh_ref, odn_ref, z_ref, wbd_ref, sc_ref, on_ref, wo_ref, o_ref):
        i = pl.program_id(0)
        ya, _pooled_v, pw, _w = _out_parts(ol1_ref, ol4_ref, ol16_ref, pu_ref, puh_ref, odn_ref, z_ref, wbd_ref, i, t)
        yb = pw * sc_ref[...]
        acc = x_ref[...] + _dot(ya.astype(BF16), wo_ref[0:256, :]) + _dot(yb.astype(BF16), wo_ref[256:512, :])
        for hd in range(DN_H):
            sl = slice(DN_E * hd, DN_E * hd + DN_E)
            oh, _r = _rms_stats(odn_ref[:, sl])
            z = z_ref[:, sl]
            yc = oh * on_ref[:, sl] * (z * _sigmoid(z))
            acc = acc + _dot(yc.astype(BF16), wo_ref[512 + DN_E * hd:512 + DN_E * hd + DN_E, :])
        o_ref[...] = acc

    return pl.pallas_call(
        body, name=name, grid=(SEQ // t,),
        in_specs=[pl.BlockSpec((t, D_MODEL), lambda i: (i, 0))] + _out_specs_common(t),
        out_specs=pl.BlockSpec((t, D_MODEL), lambda i: (i, 0)),
        out_shape=SDS((SEQ, D_MODEL), F32),
        compiler_params=_cparams(("arbitrary",)),
    )(x, ol1, ol4, ol16, rest, rest, odn, rest, wbd, scale, onorm_b, wout)


def mix_out_bwd(dxo, ol1, ol4, ol16, rest, odn, wbd, scale, onorm_b, wout, headsum, name):
    t = OUT_T

    def body(dxo_ref, ol1_ref, ol4_ref, ol16_ref, pu_ref, puh_ref, odn_ref, z_ref, wbd_ref, sc_ref, on_ref, wo_ref, hs_ref,
             dwo_ref, d1_ref, d4_ref, d16_ref, dpl_ref, dodn_ref, dz_ref, dsc_ref, don_ref, dwbd_ref):
        i = pl.program_id(0)

        @pl.when(i == 0)
        def _():
            dwo_ref[...] = jnp.zeros_like(dwo_ref)
            dsc_ref[...] = jnp.zeros_like(dsc_ref)
            don_ref[...] = jnp.zeros_like(don_ref)
            dwbd_ref[...] = jnp.zeros_like(dwbd_ref)

        ya, pooled, pw, (w1, w4, w16) = _out_parts(ol1_ref, ol4_ref, ol16_ref, pu_ref, puh_ref, odn_ref, z_ref, wbd_ref, i, t)
        sc = sc_ref[...]
        dxb = dxo_ref[...].astype(BF16)
        dwo_ref[0:256, :] += _dot_tn(ya.astype(BF16), dxb)
        dwo_ref[256:512, :] += _dot_tn((pw * sc).astype(BF16), dxb)
        dya = _dot_nt(dxb, wo_ref[0:256, :])
        o1 = _slabs_load(ol1_ref, 0, 2)
        o4 = _slabs_load(ol4_ref, 0, 2)
        o16 = _slabs_load(ol16_ref, 0, 2)
        hs = hs_ref[...]
        s1 = _dot(dya * o1, hs, HI)
        s4 = _dot(dya * o4, hs, HI)
        s16 = _dot(dya * o16, hs, HI)
        sbar = w1 * s1 + w4 * s4 + w16 * s16
        _slabs_store(d1_ref, 0, w1 * dya)
        _slabs_store(d1_ref, 2, w1 * (s1 - sbar))
        _slabs_store(d4_ref, 0, w4 * dya)
        _slabs_store(d4_ref, 2, w4 * (s4 - sbar))
        _slabs_store(d16_ref, 0, w16 * dya)
        _slabs_store(d16_ref, 2, w16 * (s16 - sbar))
        dyb = _dot_nt(dxb, wo_ref[256:512, :])
        dsc_ref[...] += jnp.sum(dyb * pw, axis=0, keepdims=True)
        dpw = (dyb * sc).astype(BF16)
        dwbd_ref[...] += _dot_tn(pooled.astype(BF16), dpw)
        dpl_ref[...] = _dot_nt(dpw, wbd_ref[...])
        for hd in range(DN_H):
            sl = slice(DN_E * hd, DN_E * hd + DN_E)
            rows_w = slice(512 + DN_E * hd, 512 + DN_E * hd + DN_E)
            oh, r = _rms_stats(odn_ref[:, sl])
            z = z_ref[:, sl]
            sg = _sigmoid(z)
            sz = z * sg
            nw = on_ref[:, sl]
            on = oh * nw
            dwo_ref[rows_w, :] += _dot_tn((on * sz).astype(BF16), dxb)
            dyc = _dot_nt(dxb, wo_ref[rows_w, :])
            dz_ref[:, sl] = dyc * on * (sg * (1.0 + z * (1.0 - sg)))
            dx, dw = _rms_bwd(oh, r, nw, dyc * sz)
            dodn_ref[:, sl] = dx
            don_ref[:, sl] += dw

    row = lambda w: pl.BlockSpec((t, w), lambda i: (i, 0))
    full = lambda a, b: pl.BlockSpec((a, b), lambda i: (0, 0))
    return pl.pallas_call(
        body, name=name, grid=(SEQ // t,),
        in_specs=[row(D_MODEL)] + _out_specs_common(t) + [full(ATT_W, ATT_W)],
        out_specs=[full(D_MODEL, D_MODEL), _slab_spec(4, t), _slab_spec(4, t), _slab_spec(4, t), row(POOL_W), row(DN_W), row(DN_W),
                   full(1, POOL_W), full(1, DN_W), full(POOL_W, POOL_W)],
        out_shape=[SDS((D_MODEL, D_MODEL), F32), SDS((4, SEQ, 128), F32), SDS((4, SEQ, 128), F32), SDS((4, SEQ, 128), F32),
                   SDS((SEQ, POOL_W), F32), SDS((SEQ, DN_W), F32), SDS((SEQ, DN_W), F32),
                   SDS((1, POOL_W), F32), SDS((1, DN_W), F32), SDS((POOL_W, POOL_W), F32)],
        compiler_params=_cparams(("arbitrary",)),
    )(dxo, ol1, ol4, ol16, rest, rest, odn, rest, wbd, scale, onorm_b, wout, headsum)


def pool_bwd(dpooled, name):
    t = 512
    nt = SEQ // t

    def body(d_ref, dn_ref, o_ref):
        i = pl.program_id(0)
        halo = jnp.where(i < nt - 1, dn_ref[...], 0.0)
        d_ext = jnp.concatenate([d_ref[...], halo], axis=0)
        n = t + POOL_HALO
        grp, cnt = _pool_consts(n, i * t, 0)
        dq = d_ext / cnt
        s2 = dq + pltpu.roll(dq, n - 1, 0)
        s4 = s2 + pltpu.roll(s2, n - 2, 0)
        s8 = s4 + pltpu.roll(s4, n - 4, 0)
        s16 = s8 + pltpu.roll(s8, n - 8, 0)
        o_ref[...] = (_pool_select(grp, s2, s4, s8, s16) - d_ext)[0:t, :]

    return pl.pallas_call(
        body, name=name, grid=(nt,),
        in_specs=[pl.BlockSpec((t, POOL_W), lambda i: (i, 0)),
                  pl.BlockSpec((POOL_HALO, POOL_W),
                               lambda i: (jnp.minimum((i + 1) * (t // POOL_HALO), SEQ // POOL_HALO - 1), 0))],
        out_specs=pl.BlockSpec((t, POOL_W), lambda i: (i, 0)),
        out_shape=SDS((SEQ, POOL_W), F32),
        compiler_params=_cparams(("arbitrary",)),
    )(dpooled, dpooled)


N_PEER = N_DEV - 1
ANY_SPEC = pl.BlockSpec(memory_space=pl.ANY)


class Exchange:
    def __init__(self, arrays, mode):
        self.arrays = list(arrays)
        self.mode = mode
        n = len(self.arrays)
        if mode == "scatter":
            self.out_shape = [SDS(a.shape, a.dtype) for a in self.arrays]
        else:
            self.out_shape = [SDS((N_DEV,) + a.shape, a.dtype) for a in self.arrays]
        self.scratch = [pltpu.SemaphoreType.DMA((n * N_PEER,)), pltpu.SemaphoreType.DMA((n * N_PEER,)),
                        pltpu.SemaphoreType.DMA((n,))]

    @staticmethod
    def _place():
        x, y, c = lax.axis_index("x"), lax.axis_index("y"), lax.axis_index("c")
        chips = [(1 - x, y), (x, 1 - y), (1 - x, 1 - y)]
        return x, y, c, chips

    @staticmethod
    def _copy(sems, a, k, src, dst, to):
        send_sems, recv_sems, _ = sems
        return pltpu.make_async_remote_copy(
            src_ref=src, dst_ref=dst, send_sem=send_sems.at[a * N_PEER + k], recv_sem=recv_sems.at[a * N_PEER + k],
            device_id=to, device_id_type=MESH)

    def _scatter_peers(self):
        x, y, c, _ = self._place()
        out = []
        for fx, fy, fc in ((0, 0, 1), (1, 0, 0), (0, 1, 0), (1, 1, 0), (1, 0, 1), (0, 1, 1), (1, 1, 1)):
            px, py, pc = x ^ fx, y ^ fy, c ^ fc
            out.append(((px, py, pc), 4 * px + 2 * py + pc))
        return 4 * x + 2 * y + c, out

    def _local(self, ins, outs, sems, a, me):
        src = ins[a].at[me] if self.mode == "scatter" else ins[a]
        return pltpu.make_async_copy(src, outs[a].at[me], sems[2].at[a])

    def start(self, ins, outs, sems):
        if self.mode == "scatter":
            me, peers = self._scatter_peers()
            for a in range(len(ins)):
                self._local(ins, outs, sems, a, me).start()
                for k, (peer, pidx) in enumerate(peers):
                    self._copy(sems, a, k, ins[a].at[pidx], outs[a].at[me], peer).start()
            return
        x, y, c, chips = self._place()
        me = 4 * x + 2 * y + c
        for a in range(len(ins)):
            self._local(ins, outs, sems, a, me).start()
            self._copy(sems, a, 0, ins[a], outs[a].at[me], (x, y, 1 - c)).start()
            for j, (cx, cy) in enumerate(chips):
                self._copy(sems, a, 1 + j, ins[a], outs[a].at[me], (cx, cy, c)).start()

    def finish(self, ins, outs, sems):
        n = len(ins)
        if self.mode == "scatter":
            me, peers = self._scatter_peers()
            for a in range(n):
                for k, (peer, pidx) in enumerate(peers):
                    self._copy(sems, a, k, ins[a].at[pidx], outs[a].at[pidx], peer).wait_recv()
            for a in range(n):
                for k, (peer, pidx) in enumerate(peers):
                    self._copy(sems, a, k, ins[a].at[pidx], outs[a].at[me], peer).wait_send()
                self._local(ins, outs, sems, a, me).wait()
            return
        x, y, c, chips = self._place()
        me = 4 * x + 2 * y + c
        sib = (x, y, 1 - c)
        for a in range(n):
            for j, (cx, cy) in enumerate(chips):
                blk = outs[a].at[4 * cx + 2 * cy + c]
                self._copy(sems, a, 1 + j, ins[a], blk, (cx, cy, c)).wait_recv()
                self._copy(sems, a, 4 + j, blk, blk, sib).start()
        for a in range(n):
            self._copy(sems, a, 0, ins[a], outs[a].at[4 * x + 2 * y + (1 - c)], sib).wait_recv()
            for j, (cx, cy) in enumerate(chips):
                blk = outs[a].at[4 * cx + 2 * cy + (1 - c)]
                self._copy(sems, a, 4 + j, blk, blk, sib).wait_recv()
        for a in range(n):
            for k in range(N_PEER):
                self._copy(sems, a, k, ins[a], outs[a].at[me], sib).wait_send()
            self._local(ins, outs, sems, a, me).wait()


def run_exchanges(exchs, name):
    counts = [len(e.arrays) for e in exchs]
    n = sum(counts)

    def body(*refs):
        ins, outs, sems = refs[:n], refs[n:2 * n], refs[2 * n:]
        parts, off = [], 0
        for j, c in enumerate(counts):
            parts.append((ins[off:off + c], outs[off:off + c], sems[3 * j:3 * j + 3]))
            off += c
        for e, p in zip(exchs, parts):
            e.start(*p)
        for e, p in zip(exchs, parts):
            e.finish(*p)

    res = pl.pallas_call(
        body, name=name, in_specs=[ANY_SPEC] * n, out_specs=[ANY_SPEC] * n,
        out_shape=[s for e in exchs for s in e.out_shape], scratch_shapes=[s for e in exchs for s in e.scratch],
    )(*[a for e in exchs for a in e.arrays])
    out, off = [], 0
    for c in counts:
        out.append(list(res[off:off + c]))
        off += c
    return out


def run_exchange(exch, name):
    return run_exchanges([exch], name)[0]


def _call(body, *, name, grid, in_specs, out_specs, out_shape, scratch_shapes, sem, args, exch=None):
    if exch is None:
        res = pl.pallas_call(body, name=name, grid=grid, in_specs=in_specs, out_specs=out_specs, out_shape=out_shape,
                             scratch_shapes=scratch_shapes, compiler_params=_cparams(sem))(*args)
        return res, None
    single = not isinstance(out_shape, (list, tuple))
    out_specs_l = [out_specs] if single else list(out_specs)
    out_shape_l = [out_shape] if single else list(out_shape)
    n_in, n_out, n_scr, m = len(in_specs), len(out_specs_l), len(scratch_shapes), len(exch.arrays)

    def wrapped(*refs):
        p = 0
        ins = refs[p:p + n_in]; p += n_in
        xin = refs[p:p + m]; p += m
        outs = refs[p:p + n_out]; p += n_out
        xout = refs[p:p + m]; p += m
        scr = refs[p:p + n_scr]; p += n_scr
        sems = refs[p:]
        ids = [pl.program_id(ax) for ax in range(len(grid))]
        first = functools.reduce(jnp.logical_and, [i == 0 for i in ids])
        last = functools.reduce(jnp.logical_and, [i == g - 1 for i, g in zip(ids, grid)])

        @pl.when(first)
        def _():
            exch.start(xin, xout, sems)

        body(*ins, *outs, *scr)

        @pl.when(last)
        def _():
            exch.finish(xin, xout, sems)

    res = pl.pallas_call(
        wrapped, name=name, grid=grid, in_specs=list(in_specs) + [ANY_SPEC] * m,
        out_specs=out_specs_l + [ANY_SPEC] * m, out_shape=out_shape_l + exch.out_shape,
        scratch_shapes=list(scratch_shapes) + exch.scratch, compiler_params=_cparams(sem),
    )(*args, *exch.arrays)
    outs = res[:n_out]
    return (outs[0] if single else outs), res[n_out:]


def _adam_math(w, g, m, v):
    m2 = ADAM_B1 * m + (1.0 - ADAM_B1) * g
    v2 = ADAM_B2 * v + (1.0 - ADAM_B2) * (g * g)
    m_hat = m2 / (1.0 - ADAM_B1 ** ADAM_STEP)
    v_hat = v2 / (1.0 - ADAM_B2 ** ADAM_STEP)
    delta = -ADAM_LR * (m_hat / (jnp.sqrt(v_hat) + ADAM_EPS) + ADAM_WD * w)
    return delta, m2, v2


ADAM_ROW_BLOCKS = 2


def adam_shard(parts0, parts1, w, m, v, name):
    _, r, c = w.shape
    rb = r // ADAM_ROW_BLOCKS

    def body(p0_ref, p1_ref, w_ref, m_ref, v_ref, g_ref, d_ref, m2_ref, v2_ref):
        def run(p_ref):
            g = p_ref[0].astype(F32)
            for i in range(1, N_DEV):
                g = g + p_ref[i].astype(F32)
            delta, m2, v2 = _adam_math(w_ref[0], g, m_ref[0], v_ref[0])
            g_ref[0] = g
            d_ref[0] = delta
            m2_ref[0] = m2
            v2_ref[0] = v2

        @pl.when(pl.program_id(0) == 0)
        def _():
            run(p0_ref)

        @pl.when(pl.program_id(0) == 1)
        def _():
            run(p1_ref)

    def p_spec(layer):
        row = (lambda l, j: jnp.where(l == 0, j, ADAM_ROW_BLOCKS - 1)) if layer == 0 else (lambda l, j: jnp.where(l == 1, j, 0))
        return pl.BlockSpec((N_DEV, rb, c), lambda l, j: (0, row(l, j), 0))

    blk = pl.BlockSpec((1, rb, c), lambda l, j: (l, j, 0))
    return pl.pallas_call(
        body, name=name, grid=(DEPTH, ADAM_ROW_BLOCKS),
        in_specs=[p_spec(0), p_spec(1), blk, blk, blk], out_specs=[blk] * 4,
        out_shape=[SDS(w.shape, F32)] * 4,
        compiler_params=_cparams(("arbitrary", "arbitrary")),
    )(parts0, parts1, w, m, v)


def parts_sum(parts0, parts1, name):
    _, r, c = parts0.shape

    def body(p0_ref, p1_ref, g_ref):
        def run(p_ref):
            g = p_ref[0].astype(F32)
            for i in range(1, N_DEV):
                g = g + p_ref[i].astype(F32)
            g_ref[0] = g

        @pl.when(pl.program_id(0) == 0)
        def _():
            run(p0_ref)

        @pl.when(pl.program_id(0) == 1)
        def _():
            run(p1_ref)

    full = pl.BlockSpec((N_DEV, r, c), lambda l: (0, 0, 0))
    return pl.pallas_call(
        body, name=name, grid=(DEPTH,), in_specs=[full, full],
        out_specs=pl.BlockSpec((1, r, c), lambda l: (l, 0, 0)), out_shape=SDS((DEPTH, r, c), F32),
        compiler_params=_cparams(("arbitrary",)),
    )(parts0, parts1)


def adam_given(g, w, m, v, name):
    _, r, c = w.shape

    def body(g_ref, w_ref, m_ref, v_ref, d_ref, m2_ref, v2_ref):
        delta, m2, v2 = _adam_math(w_ref[0], g_ref[0], m_ref[0], v_ref[0])
        d_ref[0] = delta
        m2_ref[0] = m2
        v2_ref[0] = v2

    blk = pl.BlockSpec((1, r, c), lambda l: (l, 0, 0))
    return pl.pallas_call(
        body, name=name, grid=(DEPTH,), in_specs=[blk] * 4, out_specs=[blk] * 3, out_shape=[SDS(w.shape, F32)] * 3,
        compiler_params=_cparams(("arbitrary",)),
    )(g, w, m, v)


def adam_small(parts, w, m, v, name):
    def body(p_ref, w_ref, m_ref, v_ref, g_ref, d_ref, m2_ref, v2_ref):
        g = p_ref[0]
        for i in range(1, N_DEV):
            g = g + p_ref[i]
        delta, m2, v2 = _adam_math(w_ref[...], g, m_ref[...], v_ref[...])
        g_ref[...] = g
        d_ref[...] = delta
        m2_ref[...] = m2
        v2_ref[...] = v2

    return pl.pallas_call(
        body, name=name, out_shape=[SDS(w.shape, F32)] * 4, compiler_params=_cparams(),
    )(parts, w, m, v)


def _rot_cols(w):
    w4 = w.reshape(w.shape[0], 4, 2, 32)
    return jnp.stack([-w4[:, :, 1], w4[:, :, 0]], axis=2).reshape(w.shape[0], ATT_W)


def _rot_cols_t(dw_rot):
    d4 = dw_rot.reshape(dw_rot.shape[0], 4, 2, 32)
    return jnp.stack([d4[:, :, 1], -d4[:, :, 0]], axis=2).reshape(dw_rot.shape[0], ATT_W)


def build_wext(w_in):
    aq, ak, av, pu = w_in[:, 0:256], w_in[:, 256:512], w_in[:, 512:768], w_in[:, 768:1024]
    dqkvz = w_in[:, 1024:3072]
    gates = jnp.pad(w_in[:, 3072:3080], ((0, 0), (0, GATE_W - 2 * DN_H)))
    return jnp.concatenate([aq, ak, av, _rot_cols(aq), _rot_cols(ak), dqkvz, gates, pu], axis=1)


def fold_dwext(d):
    b = EXT_ATT
    aq = d[:, 0:256] + _rot_cols_t(d[:, 768:1024])
    ak = d[:, 256:512] + _rot_cols_t(d[:, 1024:1280])
    av = d[:, 512:768]
    dqkvz = d[:, b:b + 2048]
    gates = d[:, b + R_G:b + R_G + 2 * DN_H]
    pu = d[:, b + R_PU:b + R_PU + 256]
    return jnp.concatenate([aq, ak, av, pu, dqkvz, gates], axis=1)


def _block_diag(pw):
    z = jnp.zeros((4, 64, 4, 64), pw.dtype)
    for g in range(4):
        z = z.at[g, :, g, :].set(pw[g])
    return z.reshape(POOL_W, POOL_W)


def _diag_blocks(m):
    m4 = m.reshape(4, 64, 4, 64)
    return jnp.stack([m4[g, :, g, :] for g in range(4)], axis=0)


def _lanes(v, reps):
    return jnp.repeat(v, reps)[None, :]


def layer_fwd(p, xa, cos, sin, l, host=None):
    host = host or {}

    def carried(key):
        return host[key][0] if key in host else None

    def done(key, xo):
        if key in host:
            host[key][1](xo)

    xb, xo = ffn_fwd(xa, p["n1"], *p["f1"], f"ffn1_fwd_{l}", carried("ffn1"))
    done("ffn1", xo)
    att, rest = mix_in_fwd(xb, p["nm"], p["wext"], cos, sin, f"mix_in_fwd_{l}")
    ols = [att_fwd_s(att, d, f"att_fwd_{l}_{d}") for d in DILATIONS]
    qkv = dn_prep_fwd(rest, p["conv"], f"dn_prep_fwd_{l}")
    dn, xo = dn_intra_fwd(qkv, rest, p["alog"], p["dtb"], f"dn_intra_fwd_{l}", carried("dn_intra"))
    done("dn_intra", xo)
    (odn, states), xo = dn_inter_fwd(*dn, f"dn_inter_fwd_{l}", carried("dn_inter"))
    done("dn_inter", xo)
    xc = mix_out_fwd(xb, ols[0], ols[1], ols[2], rest, odn, p["wbd"], p["scale"], p["onorm"], p["wout"], f"mix_out_fwd_{l}")
    xd, xo = ffn_fwd(xc, p["n2"], *p["f2"], f"ffn2_fwd_{l}", carried("ffn2"))
    done("ffn2", xo)
    return xd, dict(xa=xa, xb=xb, xc=xc, att=att, rest=rest, ols=ols, qkv=qkv, dn=dn, odn=odn, states=states)


def layer_bwd(p, s, dx, cos, sin, headsum, l, scatter=False, carry=None):
    blocks = lambda ws: [w_.reshape(N_DEV, FF_BLK, D_MODEL) for w_ in ws]
    (dx, *d_f2, d_n2), carried = ffn_bwd(s["xc"], dx, p["n2"], *p["f2"], f"ffn2_bwd_{l}", carry)
    (d_wout, dol1, dol4, dol16, dpooled, dodn, dz, dscale, donorm, dwbd) = mix_out_bwd(
        dx, s["ols"][0], s["ols"][1], s["ols"][2], s["rest"], s["odn"], p["wbd"], p["scale"], p["onorm"], p["wout"],
        headsum, f"mix_out_bwd_{l}")
    dpu = pool_bwd(dpooled, f"pool_bwd_{l}")
    f2 = blocks(d_f2)
    d_dn = dn_inter_bwd(*s["dn"], s["states"], dodn, f"dn_inter_bwd_{l}")
    dqkv, dg, dalog, ddtb = dn_intra_bwd(s["qkv"], s["rest"], p["alog"], p["dtb"], *d_dn, f"dn_intra_bwd_{l}")
    d_dqkv, dconv = dn_prep_bwd(s["rest"], p["conv"], dqkv, f"dn_prep_bwd_{l}")
    datts = [att_bwd_s(s["att"], ol, dol, d, f"att_bwd_{l}_{d}")
             for d, ol, dol in zip(DILATIONS, s["ols"], (dol1, dol4, dol16))]
    dproj = assemble_dproj(datts, cos, sin, d_dqkv, dz, dg, dpu, f"assemble_dproj_{l}")
    dx, d_wext, d_nm = linear_bwd(s["xb"], dx, p["nm"], dproj, p["wext"], f"mix_in_bwd_{l}")
    d_win = fold_dwext(d_wext).reshape(D_MODEL, N_DEV, IN_BLK).transpose(1, 0, 2).astype(BF16)
    io = [d_win, d_wout.reshape(N_DEV, D_MODEL // N_DEV, D_MODEL).astype(BF16)]
    (dx, *d_f1, d_n1), xo = ffn_bwd(s["xa"], dx, p["n1"], *p["f1"], f"ffn1_bwd_{l}",
                                    Exchange(f2 + io, "scatter") if scatter else None)
    if scatter:
        f2, io = list(xo[:3]), list(xo[3:])
    big = dict(f1=blocks(d_f1), f2=f2, io=io)
    small = dict(ffn1_norm=d_n1[0], mix_norm=d_nm[0], ffn2_norm=d_n2[0], pool_w=_diag_blocks(dwbd),
                 pool_scale=dscale[0], dn_a_log=dalog.reshape(DN_H, DN_E).sum(-1),
                 dn_dt_bias=ddtb.reshape(DN_H, DN_E).sum(-1),
                 dn_out_norm=donorm.reshape(DN_H, DN_E).sum(0), dn_conv_w=dconv)
    return dx, big, small, carried


def small_operands(l, pool_w, pool_scale, dn_out_norm, dn_a_log, dn_dt_bias, ffn1_norm, mix_norm, ffn2_norm):
    return dict(
        wbd=_block_diag(pool_w[l]).astype(BF16),
        scale=pool_scale[l][None, :],
        onorm=jnp.tile(dn_out_norm[l], DN_H)[None, :],
        alog=_lanes(dn_a_log[l], DN_E),
        dtb=_lanes(dn_dt_bias[l], DN_E),
        n1=ffn1_norm[l][None, :], nm=mix_norm[l][None, :], n2=ffn2_norm[l][None, :])


def set_mixer_weights(p, win_g, wout_g, conv_g):
    p["wext"] = build_wext(win_g.transpose(1, 0, 2).reshape(D_MODEL, IN_W))
    p["wout"] = wout_g.reshape(D_MODEL, D_MODEL)
    p["conv"] = conv_g.transpose(1, 0, 2).reshape(DN_CONV, 3 * DN_W)


def rope_tables(pos):
    inv_freq = 10000.0 ** (-jnp.arange(0, ATT_E, 2, dtype=F32) / ATT_E)
    ang = pos.astype(F32)[:, None] * inv_freq
    return jnp.tile(jnp.cos(ang), (1, 8)), jnp.tile(jnp.sin(ang), (1, 8))


def head_sum_matrix():
    return jnp.kron(jnp.eye(4, dtype=F32), jnp.ones((ATT_E, ATT_E), F32))


SMALL_NAMES = ("ffn1_norm", "mix_norm", "ffn2_norm", "pool_w", "pool_scale", "dn_a_log", "dn_dt_bias",
               "dn_out_norm", "final_norm", "dn_conv_w")


PACK_UNIT = 8 * 128


def _pack_rows(n):
    return -(-n // PACK_UNIT) * 8


def _pack(parts):
    rows = []
    for p in parts:
        flat = p.reshape(-1)
        r = _pack_rows(flat.shape[0])
        rows.append(jnp.pad(flat, (0, r * 128 - flat.shape[0])).reshape(r, 128))
    return jnp.concatenate(rows, axis=0)


def _unpack(packed, shapes):
    out, row = [], 0
    for s in shapes:
        n = math.prod(s)
        r = _pack_rows(n)
        out.append(packed[row:row + r].reshape(-1)[:n].reshape(s))
        row += r
    return out


def kernel(x, positions, ffn1_norm, ffn1_w_gate, ffn1_w_up, ffn1_w_down, mix_norm, w_in, pool_w, pool_scale, dn_conv_w, dn_a_log, dn_dt_bias, dn_out_norm, w_out, ffn2_norm, ffn2_w_gate, ffn2_w_up, ffn2_w_down, final_norm, loss_target, m_ffn1_norm, m_ffn1_w_gate, m_ffn1_w_up, m_ffn1_w_down, m_mix_norm, m_w_in, m_pool_w, m_pool_scale, m_dn_conv_w, m_dn_a_log, m_dn_dt_bias, m_dn_out_norm, m_w_out, m_ffn2_norm, m_ffn2_w_gate, m_ffn2_w_up, m_ffn2_w_down, m_final_norm, v_ffn1_norm, v_ffn1_w_gate, v_ffn1_w_up, v_ffn1_w_down, v_mix_norm, v_w_in, v_pool_w, v_pool_scale, v_dn_conv_w, v_dn_a_log, v_dn_dt_bias, v_dn_out_norm, v_w_out, v_ffn2_norm, v_ffn2_w_gate, v_ffn2_w_up, v_ffn2_w_down, v_final_norm):
    me = 4 * lax.axis_index("x") + 2 * lax.axis_index("y") + lax.axis_index("c")
    x0 = x[0]
    target = loss_target[0]

    cos, sin = rope_tables(positions[0])
    headsum = head_sum_matrix()

    layers = [small_operands(l, pool_w, pool_scale, dn_out_norm, dn_a_log, dn_dt_bias, ffn1_norm, mix_norm, ffn2_norm)
              for l in range(DEPTH)]

    def whole(gathered):
        return gathered.reshape(D_FF, D_MODEL)

    def gather_ffn1(l):
        def on_done(xo):
            layers[l]["f1"] = tuple(whole(g) for g in xo)
        return Exchange(ffn_shard_operands(ffn1_w_gate[l], ffn1_w_up[l], ffn1_w_down[l]), "gather"), on_done

    def gather_mixer(l):
        def on_done(xo):
            set_mixer_weights(layers[l], *xo)
        return Exchange([w_in[l].astype(BF16), w_out[l].astype(BF16), dn_conv_w[l]], "gather"), on_done

    gathered_f2 = {}

    def gather_ffn2_part(l, part):
        def on_done(xo):
            gathered_f2[(l, part)] = [whole(g) for g in xo]
            if (l, 0) in gathered_f2 and (l, 1) in gathered_f2:
                layers[l]["f2"] = tuple(gathered_f2[(l, 0)] + gathered_f2[(l, 1)])
        ops = ffn_shard_operands(ffn2_w_gate[l], ffn2_w_up[l], ffn2_w_down[l])
        return Exchange(ops[:2] if part == 0 else ops[2:], "gather"), on_done

    first, on_first = gather_ffn1(0)
    on_first(run_exchange(first, "gather_ffn1_0"))
    saved = []
    xa = x0
    for l in range(DEPTH):
        host = {"ffn1": gather_mixer(l), "dn_intra": gather_ffn2_part(l, 0), "dn_inter": gather_ffn2_part(l, 1)}
        if l + 1 < DEPTH:
            host["ffn2"] = gather_ffn1(l + 1)
        xa, s = layer_fwd(layers[l], xa, cos, sin, l, host)
        saved.append(s)

    loss_row, dx, d_final = loss_head(xa, final_norm[None, :], target, "loss_head")
    loss = lax.psum(loss_row[0, 0], ("x", "y", "c"))

    small = {}
    big_parts = [None] * DEPTH
    carry = None
    for l in reversed(range(DEPTH)):
        dx, big, small[l], carried = layer_bwd(layers[l], saved[l], dx, cos, sin, headsum, l, True, carry)
        if carried is not None:
            big_parts[l + 1]["f1"] = list(carried)
        big_parts[l] = big
        carry = Exchange(big["f1"], "scatter")
    grad_x = dx[None]

    small_shapes = {"ffn1_norm": (DEPTH, D_MODEL), "mix_norm": (DEPTH, D_MODEL), "ffn2_norm": (DEPTH, D_MODEL),
                    "pool_w": (DEPTH, 4, 64, 64), "pool_scale": (DEPTH, POOL_W), "dn_a_log": (DEPTH, DN_H),
                    "dn_dt_bias": (DEPTH, DN_H), "dn_out_norm": (DEPTH, DN_E), "final_norm": (D_MODEL,),
                    "dn_conv_w": (DEPTH, DN_CONV, 3 * DN_W)}
    g_small = {n: (d_final[0] if n == "final_norm" else jnp.stack([small[l][n] for l in range(DEPTH)]))
               for n in SMALL_NAMES}
    f1_parts, (small_parts,) = run_exchanges(
        [carry, Exchange([_pack([g_small[n] for n in SMALL_NAMES])], "gather")], "scatter_ffn1_0_gather_small")
    big_parts[0]["f1"] = f1_parts

    def conv_full(a):
        return lax.dynamic_update_slice(jnp.zeros((DEPTH, DN_CONV, 3 * DN_W), F32), a, (0, 0, me * (3 * DN_W // N_DEV)))

    given = dict(ffn1_norm=(ffn1_norm, m_ffn1_norm, v_ffn1_norm), mix_norm=(mix_norm, m_mix_norm, v_mix_norm),
                 ffn2_norm=(ffn2_norm, m_ffn2_norm, v_ffn2_norm), pool_w=(pool_w, m_pool_w, v_pool_w),
                 pool_scale=(pool_scale, m_pool_scale, v_pool_scale), dn_a_log=(dn_a_log, m_dn_a_log, v_dn_a_log),
                 dn_dt_bias=(dn_dt_bias, m_dn_dt_bias, v_dn_dt_bias),
                 dn_out_norm=(dn_out_norm, m_dn_out_norm, v_dn_out_norm),
                 final_norm=(final_norm, m_final_norm, v_final_norm),
                 dn_conv_w=(conv_full(dn_conv_w), conv_full(m_dn_conv_w), conv_full(v_dn_conv_w)))
    packed_wmv = [_pack([given[n][k] for n in SMALL_NAMES]) for k in range(3)]
    small_out = adam_small(small_parts, *packed_wmv, "adam_small")
    shapes = [small_shapes[n] for n in SMALL_NAMES]
    small_res = {n: [] for n in SMALL_NAMES}
    for arr in small_out:
        for n, v_ in zip(SMALL_NAMES, _unpack(arr, shapes)):
            if n == "dn_conv_w":
                v_ = lax.dynamic_slice(v_, (0, 0, me * (3 * DN_W // N_DEV)), (DEPTH, DN_CONV, 3 * DN_W // N_DEV))
            small_res[n].append(v_)

    def parts_of(group, idx):
        return [big_parts[l][group][idx] for l in range(DEPTH)]

    def adam_transposed(group, idx, w, m, v, name):
        g = parts_sum(*parts_of(group, idx), f"sum_{name}").transpose(0, 2, 1)
        return [g] + list(adam_given(g, w, m, v, f"adam_{name}"))

    big_res = dict(
        ffn1_w_gate=adam_transposed("f1", 0, ffn1_w_gate, m_ffn1_w_gate, v_ffn1_w_gate, "ffn1_gate"),
        ffn1_w_up=adam_transposed("f1", 1, ffn1_w_up, m_ffn1_w_up, v_ffn1_w_up, "ffn1_up"),
        ffn1_w_down=adam_shard(*parts_of("f1", 2), ffn1_w_down, m_ffn1_w_down, v_ffn1_w_down, "adam_ffn1_down"),
        ffn2_w_gate=adam_transposed("f2", 0, ffn2_w_gate, m_ffn2_w_gate, v_ffn2_w_gate, "ffn2_gate"),
        ffn2_w_up=adam_transposed("f2", 1, ffn2_w_up, m_ffn2_w_up, v_ffn2_w_up, "ffn2_up"),
        ffn2_w_down=adam_shard(*parts_of("f2", 2), ffn2_w_down, m_ffn2_w_down, v_ffn2_w_down, "adam_ffn2_down"),
        w_in=adam_shard(*parts_of("io", 0), w_in, m_w_in, v_w_in, "adam_w_in"),
        w_out=adam_shard(*parts_of("io", 1), w_out, m_w_out, v_w_out, "adam_w_out"),
    )

    order = ("ffn1_norm", "ffn1_w_gate", "ffn1_w_up", "ffn1_w_down", "mix_norm", "w_in", "pool_w", "pool_scale",
             "dn_conv_w", "dn_a_log", "dn_dt_bias", "dn_out_norm", "w_out", "ffn2_norm", "ffn2_w_gate", "ffn2_w_up",
             "ffn2_w_down", "final_norm")
    res = {**small_res, **big_res}
    outs = [loss, grad_x]
    for k in range(4):
        outs.extend(res[n][k] for n in order)
    return tuple(outs)
```
